```python
import math
import jax, jax.numpy as jnp
from jax import lax
import numpy as np

D_MODEL = 1024
BATCH = 4
SEQ = 4096
DEPTH = 1

N_MEM = 256
D_MIX = D_MODEL
FOX_HEADS = 8
FOX_HEAD_DIM = 64
FOX_WIDTH = FOX_HEADS * FOX_HEAD_DIM
QUERY_BLOCK = 128
FOX_FORGET_BIAS = 3.0
MLSTM_HEADS = 4
MLSTM_QK_DIM = 64
MLSTM_V_DIM = 128
MLSTM_QK_WIDTH = MLSTM_HEADS * MLSTM_QK_DIM
MLSTM_V_WIDTH = MLSTM_HEADS * MLSTM_V_DIM
MLSTM_CHUNK = 64
MLSTM_FORGET_BIAS = 3.0
CONV_WIDTH = 4
IN_WIDTHS = (FOX_WIDTH, FOX_WIDTH, FOX_WIDTH, FOX_HEADS,
             2 * MLSTM_QK_WIDTH, MLSTM_V_WIDTH, MLSTM_HEADS, MLSTM_HEADS, MLSTM_V_WIDTH)
IN_WIDTH = 3088
XATTN_HEADS = 4
XATTN_HEAD_DIM = D_MODEL // XATTN_HEADS
N_EXPERTS = 32
TOP_K = 4
D_EXPERT = D_MODEL
SWIGLU_LIMIT = 7.0
SWIGLU_ALPHA = 1.702
EXPERT_BLOCK = 256
DEEPNORM_ALPHA = (2 * DEPTH) ** 0.25
DEEPNORM_BETA = (8 * DEPTH) ** -0.25
LN_EPS = 1e-5
RMS_EPS = 1e-6

kernel_name = "fox_mlstm_hymba_deepnorm_moe"


def layer_norm(x, g, b):
    xf = x.astype(jnp.float32)
    mu = jnp.mean(xf, -1, keepdims=True)
    xc = xf - mu
    var = jnp.mean(xc * xc, -1, keepdims=True)
    return (xc * lax.rsqrt(var + LN_EPS) * g + b).astype(x.dtype)


def head_rmsnorm(h, gain):
    B, S = h.shape[0], h.shape[1]
    hf = h.astype(jnp.float32)
    hf = hf * lax.rsqrt(jnp.mean(hf * hf, -1, keepdims=True) + RMS_EPS)
    return hf.reshape(B, S, -1) * gain


def causal_conv(x, w):
    S = x.shape[1]
    xp = jnp.pad(x, ((0, 0), (CONV_WIDTH - 1, 0), (0, 0)))
    out = xp[:, 0:S] * w[0]
    for j in range(1, CONV_WIDTH):
        out = out + xp[:, j:j + S] * w[j]
    return out


def fox_attention(q, k, v, log_f):
    S = q.shape[2]
    c = jnp.cumsum(log_f, axis=-1)
    scale = FOX_HEAD_DIM ** -0.5
    outs = []
    for blk in range(S // QUERY_BLOCK):
        q0 = blk * QUERY_BLOCK
        q1 = q0 + QUERY_BLOCK
        s = jnp.einsum('bhqd,bhkd->bhqk', q[:, :, q0:q1], k[:, :, :q1]).astype(jnp.float32) * scale
        s = s + c[:, :, q0:q1, None] - c[:, :, None, :q1]
        causal = (q0 + jnp.arange(QUERY_BLOCK))[:, None] >= jnp.arange(q1)[None, :]
        p = jax.nn.softmax(jnp.where(causal, s, -jnp.inf), axis=-1)
        outs.append(jnp.einsum('bhqk,bhkd->bhqd', p.astype(v.dtype), v[:, :, :q1]))
    return jnp.concatenate(outs, axis=2)


def mlstm_chunkwise(q, k, v, i_pre, f_pre):
    B, H, S, Dk = q.shape
    Dv = v.shape[-1]
    L = MLSTM_CHUNK
    NC = S // L
    f32 = jnp.float32
    q = q.astype(f32).reshape(B, H, NC, L, Dk)
    k = (k.astype(f32) * Dk ** -0.5).reshape(B, H, NC, L, Dk)
    v = v.astype(f32).reshape(B, H, NC, L, Dv)
    log_i = i_pre.astype(f32).reshape(B, H, NC, L)
    log_f = jax.nn.log_sigmoid(f_pre.astype(f32)).reshape(B, H, NC, L)
    b = jnp.cumsum(log_f, axis=-1)
    g = b[..., -1]
    tri = jnp.tril(jnp.ones((L, L), dtype=bool))
    d = jnp.where(tri, b[..., :, None] - b[..., None, :] + log_i[..., None, :], -jnp.inf)
    a = g[..., None] - b + log_i
    m_loc = jnp.max(a, axis=-1)
    wa = jnp.exp(a - m_loc[..., None])
    kv_chunk = jnp.einsum('bhcl,bhcld,bhcle->bhcde', wa, k, v)
    n_chunk = jnp.einsum('bhcl,bhcld->bhcd', wa, k)

    def step(carry, inp):
        C, n, m = carry
        kv_c, n_c, g_c, m_c = inp
        m_new = jnp.maximum(g_c + m, m_c)
        decay = jnp.exp(g_c + m - m_new)
        scale_in = jnp.exp(m_c - m_new)
        C_new = decay[..., None, None] * C + scale_in[..., None, None] * kv_c
        n_new = decay[..., None] * n + scale_in[..., None] * n_c
        return (C_new, n_new, m_new), (C, n, m)

    init = (jnp.zeros((B, H, Dk, Dv), f32), jnp.zeros((B, H, Dk), f32), jnp.zeros((B, H), f32))
    xs = (jnp.moveaxis(kv_chunk, 2, 0), jnp.moveaxis(n_chunk, 2, 0),
          jnp.moveaxis(g, 2, 0), jnp.moveaxis(m_loc, 2, 0))
    _, (C_prev, n_prev, m_prev) = lax.scan(step, init, xs)
    C_prev = jnp.moveaxis(C_prev, 0, 2)
    n_prev = jnp.moveaxis(n_prev, 0, 2)
    m_prev = jnp.moveaxis(m_prev, 0, 2)

    inter_log = b + m_prev[..., None]
    m_t = jnp.maximum(inter_log, jnp.max(d, axis=-1))
    w_inter = jnp.exp(inter_log - m_t)
    p = jnp.exp(d - m_t[..., None]) * jnp.einsum('bhcld,bhcsd->bhcls', q, k)
    num = (w_inter[..., None] * jnp.einsum('bhcld,bhcde->bhcle', q, C_prev)
           + jnp.einsum('bhcls,bhcse->bhcle', p, v))
    den = w_inter * jnp.einsum('bhcld,bhcd->bhcl', q, n_prev) + jnp.sum(p, axis=-1)
    h = num / jnp.maximum(jnp.abs(den), jnp.exp(-m_t))[..., None]
    return h.reshape(B, H, S, Dv)


def hybrid_mixer(x, w_in, fox_f_bias, conv_w, i_bias, f_bias, fox_g, mlstm_g, w_out):
    B, S, _ = x.shape
    u = x @ w_in
    points = np.cumsum(IN_WIDTHS)[:-1].tolist()
    fq, fk, fv, ff, mqk, mv, mi, mf, mo = jnp.split(u, points, axis=-1)

    def heads(t, h):
        return t.reshape(B, S, h, -1).transpose(0, 2, 1, 3)

    log_f = jax.nn.log_sigmoid((ff + fox_f_bias).astype(jnp.float32)).transpose(0, 2, 1)
    fo = fox_attention(heads(fq, FOX_HEADS), heads(fk, FOX_HEADS), heads(fv, FOX_HEADS), log_f)
    fo = head_rmsnorm(fo.transpose(0, 2, 1, 3), fox_g)

    mqk = jax.nn.silu(causal_conv(mqk, conv_w))
    mq, mk = jnp.split(mqk, [MLSTM_QK_WIDTH], axis=-1)
    h = mlstm_chunkwise(heads(mq, MLSTM_HEADS), heads(mk, MLSTM_HEADS), heads(mv, MLSTM_HEADS),
                        (mi + i_bias).transpose(0, 2, 1), (mf + f_bias).transpose(0, 2, 1))
    mo_out = head_rmsnorm(h.transpose(0, 2, 1, 3), mlstm_g) * jax.nn.sigmoid(mo.astype(jnp.float32))

    mixed = jnp.concatenate([fo, mo_out], axis=-1).astype(x.dtype)
    return mixed @ w_out


def memory_cross_attention(x, mem, w_q, w_k, w_v, w_o):
    B, S, D = x.shape
    M = mem.shape[1]
    q = (x @ w_q).reshape(B, S, XATTN_HEADS, XATTN_HEAD_DIM)
    k = (mem @ w_k).reshape(B, M, XATTN_HEADS, XATTN_HEAD_DIM)
    v = (mem @ w_v).reshape(B, M, XATTN_HEADS, XATTN_HEAD_DIM)
    s = jnp.einsum('bqhd,bkhd->bhqk', q, k).astype(jnp.float32) * XATTN_HEAD_DIM ** -0.5
    p = jax.nn.softmax(s, axis=-1)
    o = jnp.einsum('bhqk,bkhd->bqhd', p.astype(v.dtype), v).reshape(B, S, D)
    return o @ w_o


def clamped_swiglu(h):
    gate, lin = h[..., :D_EXPERT], h[..., D_EXPERT:]
    gate = jnp.minimum(gate, SWIGLU_LIMIT)
    lin = jnp.clip(lin, -SWIGLU_LIMIT, SWIGLU_LIMIT)
    return gate * jax.nn.sigmoid(SWIGLU_ALPHA * gate) * (lin + 1)


def moe_ffn(x, w_router, b_router, w_gu, b_gu, w_down, b_down):
    B, S, D = x.shape
    N = B * S
    G = EXPERT_BLOCK
    xf = x.reshape(N, D)
    logits = (xf @ w_router + b_router).astype(jnp.float32)
    top_vals, top_idx = lax.top_k(logits, TOP_K)
    gates = jax.nn.softmax(top_vals, axis=-1)
    e_flat = top_idx.reshape(-1).astype(jnp.int32)
    tok_flat = jnp.repeat(jnp.arange(N, dtype=jnp.int32), TOP_K)
    g_flat = gates.reshape(-1)
    order = jnp.argsort(e_flat)
    e_sorted = e_flat[order]
    counts = jnp.bincount(e_flat, length=N_EXPERTS).astype(jnp.int32)
    padded = ((counts + G - 1) // G) * G
    start = jnp.cumsum(counts) - counts
    pad_end = jnp.cumsum(padded)
    pad_start = pad_end - padded
    rank = jnp.arange(N * TOP_K, dtype=jnp.int32) - start[e_sorted]
    dest = pad_start[e_sorted] + rank
    P = N * TOP_K + N_EXPERTS * G
    NB = P // G
    row_tok = jnp.full((P,), N, jnp.int32).at[dest].set(tok_flat[order])
    row_gate = jnp.zeros((P,), jnp.float32).at[dest].set(g_flat[order])
    blk_e = jnp.minimum(jnp.searchsorted(pad_end, jnp.arange(NB, dtype=jnp.int32) * G, side='right'),
                        N_EXPERTS - 1)
    x_pad = jnp.concatenate([xf, jnp.zeros((1, D), xf.dtype)], axis=0)
    xb = x_pad[row_tok].reshape(NB, G, D)

    def expert_block(args):
        xblk, e = args
        hid = clamped_swiglu(xblk @ w_gu[e] + b_gu[e])
        return hid @ w_down[e] + b_down[e]

    yb = lax.map(expert_block, (xb, blk_e)).reshape(P, D)
    y = jnp.zeros((N + 1, D), yb.dtype).at[row_tok].add(yb * row_gate[:, None].astype(yb.dtype))
    return y[:N].reshape(B, S, D)


def setup_inputs(seed: int = 0) -> dict:
    key = jax.random.key(seed)
    ks = iter(jax.random.split(key, 32))
    L, D, E, F = DEPTH, D_MODEL, N_EXPERTS, D_EXPERT

    def nrm(shape, scale):
        return jax.random.normal(next(ks), shape, jnp.float32) * scale

    return {
        "x": nrm((BATCH, SEQ, D), 1.0),
        "mem": nrm((BATCH, N_MEM, D), 1.0),
        "w_in": nrm((L, D, IN_WIDTH), D ** -0.5),
        "fox_f_bias": FOX_FORGET_BIAS + nrm((L, FOX_HEADS), 0.1),
        "mlstm_conv_w": nrm((L, CONV_WIDTH, 2 * MLSTM_QK_WIDTH), CONV_WIDTH ** -0.5),
        "mlstm_i_bias": nrm((L, MLSTM_HEADS), 0.1),
        "mlstm_f_bias": MLSTM_FORGET_BIAS + nrm((L, MLSTM_HEADS), 0.1),
        "fox_norm_g": 1.0 + nrm((L, FOX_WIDTH), 0.02),
        "mlstm_norm_g": 1.0 + nrm((L, MLSTM_V_WIDTH), 0.02),
        "w_mix_out": nrm((L, D_MIX, D), D_MIX ** -0.5 * DEEPNORM_BETA),
        "ln1_g": 1.0 + nrm((L, D), 0.02),
        "ln1_b": nrm((L, D), 0.02),
        "w_xq": nrm((L, D, D), D ** -0.5),
        "w_xk": nrm((L, D, D), D ** -0.5),
        "w_xv": nrm((L, D, D), D ** -0.5),
        "w_xo": nrm((L, D, D), D ** -0.5 * DEEPNORM_BETA),
        "ln2_g": 1.0 + nrm((L, D), 0.02),
        "ln2_b": nrm((L, D), 0.02),
        "w_router": nrm((L, D, E), D ** -0.5),
        "b_router": nrm((L, E), 0.01),
        "w_gate_up": nrm((L, E, D, 2 * F), D ** -0.5),
        "b_gate_up": nrm((L, E, 2 * F), 0.02),
        "w_down": nrm((L, E, F, D), F ** -0.5 * DEEPNORM_BETA),
        "b_down": nrm((L, E, D), 0.02),
        "ln3_g": 1.0 + nrm((L, D), 0.02),
        "ln3_b": nrm((L, D), 0.02),
    }


def reference(x, mem, w_in, fox_f_bias, mlstm_conv_w, mlstm_i_bias, mlstm_f_bias, fox_norm_g,
              mlstm_norm_g, w_mix_out, ln1_g, ln1_b, w_xq, w_xk, w_xv, w_xo, ln2_g, ln2_b,
              w_router, b_router, w_gate_up, b_gate_up, w_down, b_down, ln3_g, ln3_b):
    for l in range(DEPTH):
        mix = hybrid_mixer(x, w_in[l], fox_f_bias[l], mlstm_conv_w[l], mlstm_i_bias[l],
                           mlstm_f_bias[l], fox_norm_g[l], mlstm_norm_g[l], w_mix_out[l])
        x = layer_norm(DEEPNORM_ALPHA * x + mix.astype(x.dtype), ln1_g[l], ln1_b[l])
        xa = memory_cross_attention(x, mem, w_xq[l], w_xk[l], w_xv[l], w_xo[l])
        x = layer_norm(DEEPNORM_ALPHA * x + xa.astype(x.dtype), ln2_g[l], ln2_b[l])
        ff = moe_ffn(x, w_router[l], b_router[l], w_gate_up[l], b_gate_up[l], w_down[l], b_down[l])
        x = layer_norm(DEEPNORM_ALPHA * x + ff.astype(x.dtype), ln3_g[l], ln3_b[l])
    return x
```

```python
import functools

import jax
import jax.numpy as jnp
from jax import lax
from jax.experimental import pallas as pl
from jax.experimental.pallas import tpu as pltpu

F32 = jnp.float32
BF16 = jnp.bfloat16

D_MODEL = 1024
FOX_HEADS = 8
FOX_HEAD_DIM = 64
FOX_WIDTH = FOX_HEADS * FOX_HEAD_DIM
MLSTM_HEADS = 4
MLSTM_QK_DIM = 64
MLSTM_V_DIM = 128
MLSTM_QK_WIDTH = MLSTM_HEADS * MLSTM_QK_DIM
MLSTM_V_WIDTH = MLSTM_HEADS * MLSTM_V_DIM
CONV_WIDTH = 4
XATTN_HEADS = 4
XATTN_HEAD_DIM = D_MODEL // XATTN_HEADS
N_EXPERTS = 32
TOP_K = 4
D_EXPERT = D_MODEL
SWIGLU_LIMIT = 7.0
SWIGLU_ALPHA = 1.702
DEEPNORM_ALPHA = 2.0 ** 0.25
LN_EPS = 1e-5
RMS_EPS = 1e-6

LANES = 128
SEQ_BLOCK = 256
ROW_TILE = 512
EXPERT_ROWS = 256
GATE_I0 = FOX_HEADS
GATE_F0 = FOX_HEADS + MLSTM_HEADS
NEG_BIG = -1e30

MIB = 1024 * 1024


def _params(semantics, vmem_mib):
    return pltpu.CompilerParams(dimension_semantics=semantics,
                                vmem_limit_bytes=vmem_mib * MIB)


def _layer_norm(y, g, b):
    mu = jnp.mean(y, axis=-1, keepdims=True)
    yc = y - mu
    var = jnp.mean(yc * yc, axis=-1, keepdims=True)
    return yc * lax.rsqrt(var + LN_EPS) * g + b


def _dot(a, b):
    return jnp.dot(a, b, preferred_element_type=F32)


def _dot_nt(a, b):
    return lax.dot_general(a, b, (((1,), (1,)), ((), ())), preferred_element_type=F32)


def _dot_tn(a, b):
    return lax.dot_general(a, b, (((0,), (0,)), ((), ())), preferred_element_type=F32)


def _inproj_kernel(x_ref, w_ref, gb_ref, fq_ref, fk_ref, fv_ref, mqk_ref, mv_ref, mo_ref, g_ref):
    xb = x_ref[...].astype(BF16)

    def mm(c0, width):
        return _dot(xb, w_ref[:, c0:c0 + width])

    fq_ref[...] = (mm(0, 512) * (FOX_HEAD_DIM ** -0.5)).astype(BF16)
    fk_ref[...] = mm(512, 512).astype(BF16)
    fv_ref[...] = mm(1024, 512).astype(BF16)
    mqk_ref[...] = mm(1536, 512)
    mv_ref[...] = mm(2048, 512).astype(BF16)
    mo_ref[...] = mm(2560, 512)
    g_ref[...] = mm(3072, LANES) + gb_ref[...]


def _inproj(x2d, w_r, gate_bias):
    n = x2d.shape[0]
    tm = ROW_TILE
    row = lambda width: pl.BlockSpec((tm, width), lambda i: (i, 0))
    full = lambda a: pl.BlockSpec(a.shape, lambda i: (0, 0))
    out_shapes = (
        jax.ShapeDtypeStruct((n, 512), BF16),
        jax.ShapeDtypeStruct((n, 512), BF16),
        jax.ShapeDtypeStruct((n, 512), BF16),
        jax.ShapeDtypeStruct((n, 512), F32),
        jax.ShapeDtypeStruct((n, 512), BF16),
        jax.ShapeDtypeStruct((n, 512), F32),
        jax.ShapeDtypeStruct((n, LANES), F32),
    )
    return pl.pallas_call(
        _inproj_kernel,
        grid=(n // tm,),
        in_specs=[row(D_MODEL), full(w_r), full(gate_bias)],
        out_specs=tuple(row(s.shape[1]) for s in out_shapes),
        out_shape=out_shapes,
        compiler_params=_params(("parallel",), 48),
    )(x2d, w_r, gate_bias)


def _gateprep_kernel(g_ref, col_ref, row_ref, carry_ref):
    c = pl.program_id(1)

    @pl.when(c == 0)
    def _():
        carry_ref[...] = jnp.zeros_like(carry_ref)

    g = g_ref[...]
    lane = lax.broadcasted_iota(jnp.int32, g.shape, 1)
    is_i = (lane >= GATE_I0) & (lane < GATE_F0)
    logsig = jnp.minimum(g, 0.0) - jnp.log1p(jnp.exp(-jnp.abs(g)))
    blk = g.shape[0]
    r = lax.broadcasted_iota(jnp.int32, (blk, blk), 0)
    s = lax.broadcasted_iota(jnp.int32, (blk, blk), 1)
    tri = (s <= r).astype(F32)
    cs = jnp.dot(tri, logsig, precision=lax.Precision.HIGHEST, preferred_element_type=F32)
    carry = carry_ref[0:1, :]
    glob = cs + carry
    carry_ref[...] = jnp.broadcast_to(glob[blk - 1:blk, :], carry_ref.shape)
    out = jnp.where(lane < GATE_I0, glob, jnp.where(is_i, g, cs))
    col_ref[...] = out
    row_ref[0] = out.T[0:16, :]


def _gateprep(gates, batch, seq):
    n = gates.shape[0]
    nc = seq // SEQ_BLOCK
    return pl.pallas_call(
        _gateprep_kernel,
        grid=(batch, nc),
        in_specs=[pl.BlockSpec((SEQ_BLOCK, LANES), lambda b, c: (b * nc + c, 0))],
        out_specs=(pl.BlockSpec((SEQ_BLOCK, LANES), lambda b, c: (b * nc + c, 0)),
                   pl.BlockSpec((1, 16, SEQ_BLOCK), lambda b, c: (b, 0, c))),
        out_shape=(jax.ShapeDtypeStruct((n, LANES), F32),
                   jax.ShapeDtypeStruct((batch, 16, seq), F32)),
        scratch_shapes=[pltpu.VMEM((8, LANES), F32)],
        compiler_params=_params(("parallel", "arbitrary"), 32),
    )(gates)


def _fox_kernel(q_ref, k_ref, v_ref, c_ref, gain_ref, o_ref, m_sc, l_sc, acc_sc):
    hp = pl.program_id(1)
    qi = pl.program_id(2)
    tq = q_ref.shape[0]
    tk = SEQ_BLOCK
    q = q_ref[...]
    lane = lax.broadcasted_iota(jnp.int32, (tq, LANES), 1)
    lo = lane < FOX_HEAD_DIM
    zero = jnp.zeros_like(q)
    qh = (jnp.where(lo, q, zero), jnp.where(lo, zero, q))

    m_sc[...] = jnp.full(m_sc.shape, -jnp.inf, F32)
    l_sc[...] = jnp.zeros(l_sc.shape, F32)
    acc_sc[...] = jnp.zeros(acc_sc.shape, F32)

    def block(ki, masked):
        k0 = pl.multiple_of(ki * tk, tk)
        k = k_ref[pl.ds(k0, tk), :]
        v = v_ref[pl.ds(k0, tk), :]
        for h in range(2):
            crow = c_ref[0, 2 * hp + h, pl.ds(ki, 1), :]
            s = _dot_nt(qh[h], k) - crow
            if masked:
                rr = lax.broadcasted_iota(jnp.int32, (tq, tk), 0)
                cc = lax.broadcasted_iota(jnp.int32, (tq, tk), 1)
                s = jnp.where(cc <= rr, s, -jnp.inf)
            m_prev = m_sc[h]
            m_new = jnp.maximum(m_prev, jnp.max(s, axis=-1, keepdims=True))
            p = jnp.exp(s - m_new)
            alpha = jnp.exp(m_prev - m_new)
            l_sc[h] = alpha * l_sc[h] + jnp.sum(p, axis=-1, keepdims=True)
            acc_sc[h] = alpha * acc_sc[h] + _dot(p.astype(BF16), v)
            m_sc[h] = m_new

    def body(ki, carry):
        block(ki, False)
        return carry

    lax.fori_loop(0, qi, body, 0)
    block(qi, True)

    o = jnp.where(lo, acc_sc[0] / l_sc[0], acc_sc[1] / l_sc[1])
    sq = o * o
    ss0 = jnp.sum(jnp.where(lo, sq, 0.0), axis=-1, keepdims=True)
    ss1 = jnp.sum(jnp.where(lo, 0.0, sq), axis=-1, keepdims=True)
    inv = jnp.where(lo, lax.rsqrt(ss0 / FOX_HEAD_DIM + RMS_EPS),
                    lax.rsqrt(ss1 / FOX_HEAD_DIM + RMS_EPS))
    o_ref[...] = (o * inv * gain_ref[...]).astype(o_ref.dtype)


def _fox(fq, fk, fv, crow4, fox_gain, batch, seq):
    n = fq.shape[0]
    tq = SEQ_BLOCK
    nq = seq // tq
    npair = FOX_HEADS // 2
    kv_spec = pl.BlockSpec((seq, LANES), lambda b, hp, qi: (b, hp))
    return pl.pallas_call(
        _fox_kernel,
        grid=(batch, npair, nq),
        in_specs=[pl.BlockSpec((tq, LANES), lambda b, hp, qi: (b * nq + qi, hp)),
                  kv_spec, kv_spec,
                  pl.BlockSpec((1, 16, nq, SEQ_BLOCK), lambda b, hp, qi: (b, 0, 0, 0)),
                  pl.BlockSpec((1, LANES), lambda b, hp, qi: (0, hp))],
        out_specs=pl.BlockSpec((tq, LANES), lambda b, hp, qi: (b * nq + qi, hp)),
        out_shape=jax.ShapeDtypeStruct((n, FOX_WIDTH), BF16),
        scratch_shapes=[pltpu.VMEM((2, tq, 1), F32), pltpu.VMEM((2, tq, 1), F32),
                        pltpu.VMEM((2, tq, LANES), F32)],
        compiler_params=_params(("parallel", "parallel", "arbitrary"), 32),
    )(fq, fk, fv, crow4, fox_gain)


def _mlstm_kernel(qk_ref, v_ref, og_ref, col_ref, row_ref, cw_ref, gain_ref, o_ref,
                  tail_ref, buf_ref, c_sc, m_sc):
    c = pl.program_id(1)
    L = SEQ_BLOCK

    @pl.when(c == 0)
    def _():
        tail_ref[...] = jnp.zeros_like(tail_ref)
        c_sc[...] = jnp.zeros_like(c_sc)
        m_sc[...] = jnp.zeros_like(m_sc)

    x = qk_ref[...]
    buf_ref[0:8, :] = tail_ref[...]
    buf_ref[8:8 + L, :] = x
    tail_ref[...] = x[L - 8:L, :]
    y = x * cw_ref[CONV_WIDTH - 1:CONV_WIDTH, :]
    for j in range(CONV_WIDTH - 1):
        shift = CONV_WIDTH - 1 - j
        y = y + buf_ref[8 - shift:8 - shift + L, :] * cw_ref[j:j + 1, :]
    y = y * jax.nn.sigmoid(y)
    qc = y[:, :MLSTM_QK_WIDTH].astype(BF16)
    kf = y[:, MLSTM_QK_WIDTH:] * (MLSTM_QK_DIM ** -0.5)

    col = col_ref[...]
    row = row_ref[0]
    lane = lax.broadcasted_iota(jnp.int32, (L, LANES), 1)
    lo = lane < MLSTM_QK_DIM
    rr = lax.broadcasted_iota(jnp.int32, (L, L), 0)
    cc = lax.broadcasted_iota(jnp.int32, (L, L), 1)
    causal = cc <= rr
    ones_col = (lane == 0).astype(BF16)

    for h in range(MLSTM_HEADS):
        pair, half = divmod(h, 2)
        sel = lo if half == 0 else jnp.logical_not(lo)
        q2 = qc[:, pair * LANES:(pair + 1) * LANES]
        k2 = kf[:, pair * LANES:(pair + 1) * LANES]
        qh = jnp.where(sel, q2, jnp.zeros_like(q2))
        kb = k2.astype(BF16)
        bcol = col[:, GATE_F0 + h:GATE_F0 + h + 1]
        licol = col[:, GATE_I0 + h:GATE_I0 + h + 1]
        brow = row[GATE_F0 + h:GATE_F0 + h + 1, :]
        lirow = row[GATE_I0 + h:GATE_I0 + h + 1, :]
        g = bcol[L - 1:L, :]
        m_prev = m_sc[h][0:1, 0:1]

        d = jnp.where(causal, bcol + (lirow - brow), -jnp.inf)
        inter_log = bcol + m_prev
        m_t = jnp.maximum(inter_log, jnp.max(d, axis=-1, keepdims=True))
        w_inter = jnp.exp(inter_log - m_t)
        p = jnp.exp(d - m_t) * _dot_nt(qh, kb)
        vaug = jnp.concatenate([v_ref[:, h * LANES:(h + 1) * LANES], ones_col], axis=1)
        cstate = c_sc[h]
        tot = w_inter * _dot(qh, cstate.astype(BF16)) + _dot(p.astype(BF16), vaug)
        num = tot[:, :LANES]
        den = tot[:, LANES:LANES + 1]
        hout = num / jnp.maximum(jnp.abs(den), jnp.exp(-m_t))

        a = g - bcol + licol
        m_loc = jnp.max(a, axis=0, keepdims=True)
        wa = jnp.exp(a - m_loc)
        kw = jnp.where(sel, k2 * wa, 0.0).astype(BF16)
        kv = _dot_tn(kw, vaug)
        m_new = jnp.maximum(g + m_prev, m_loc)
        c_sc[h] = jnp.exp(g + m_prev - m_new) * cstate + jnp.exp(m_loc - m_new) * kv
        m_sc[h] = jnp.broadcast_to(m_new, m_sc.shape[1:])

        ms = jnp.mean(hout * hout, axis=-1, keepdims=True)
        hn = hout * lax.rsqrt(ms + RMS_EPS) * gain_ref[:, h * LANES:(h + 1) * LANES]
        gate = jax.nn.sigmoid(og_ref[:, h * LANES:(h + 1) * LANES])
        o_ref[:, h * LANES:(h + 1) * LANES] = (hn * gate).astype(o_ref.dtype)


def _mlstm(mqk, mv, mo, gcol, grow, conv_w, gain, batch, seq):
    n = mqk.shape[0]
    L = SEQ_BLOCK
    nc = seq // L
    row = lambda width: pl.BlockSpec((L, width), lambda b, c: (b * nc + c, 0))
    full = lambda a: pl.BlockSpec(a.shape, lambda b, c: (0, 0))
    return pl.pallas_call(
        _mlstm_kernel,
        grid=(batch, nc),
        in_specs=[row(512), row(512), row(512), row(LANES),
                  pl.BlockSpec((1, 16, L), lambda b, c: (b, 0, c)),
                  full(conv_w), full(gain)],
        out_specs=row(512),
        out_shape=jax.ShapeDtypeStruct((n, MLSTM_V_WIDTH), BF16),
        scratch_shapes=[pltpu.VMEM((8, 512), F32), pltpu.VMEM((8 + L, 512), F32),
                        pltpu.VMEM((MLSTM_HEADS, LANES, 2 * LANES), F32),
                        pltpu.VMEM((MLSTM_HEADS, 8, LANES), F32)],
        compiler_params=_params(("parallel", "arbitrary"), 48),
    )(mqk, mv, mo, gcol, grow, conv_w, gain)


def _outproj_kernel(fo_ref, mo_ref, w_ref, x_ref, g_ref, b_ref, o_ref):
    mix = _dot(fo_ref[...], w_ref[0:FOX_WIDTH, :]) + _dot(mo_ref[...], w_ref[FOX_WIDTH:, :])
    o_ref[...] = _layer_norm(DEEPNORM_ALPHA * x_ref[...] + mix, g_ref[...], b_ref[...])


def _outproj(fo, mo, w_out, x2d, g, b):
    n = x2d.shape[0]
    tm = ROW_TILE
    row = lambda width: pl.BlockSpec((tm, width), lambda i: (i, 0))
    full = lambda a: pl.BlockSpec(a.shape, lambda i: (0, 0))
    return pl.pallas_call(
        _outproj_kernel,
        grid=(n // tm,),
        in_specs=[row(512), row(512), full(w_out), row(D_MODEL), full(g), full(b)],
        out_specs=row(D_MODEL),
        out_shape=jax.ShapeDtypeStruct((n, D_MODEL), F32),
        compiler_params=_params(("parallel",), 48),
    )(fo, mo, w_out, x2d, g, b)


def _memkv_kernel(mem_ref, wk_ref, wv_ref, k_ref, v_ref):
    mb = mem_ref[...].astype(BF16)
    k_ref[...] = (_dot(mb, wk_ref[...]) * (XATTN_HEAD_DIM ** -0.5)).astype(BF16)
    v_ref[...] = _dot(mb, wv_ref[...]).astype(BF16)


def _memkv(mem2d, wk, wv, n_mem):
    n = mem2d.shape[0]
    row = pl.BlockSpec((n_mem, D_MODEL), lambda i: (i, 0))
    full = lambda a: pl.BlockSpec(a.shape, lambda i: (0, 0))
    return pl.pallas_call(
        _memkv_kernel,
        grid=(n // n_mem,),
        in_specs=[row, full(wk), full(wv)],
        out_specs=(row, row),
        out_shape=(jax.ShapeDtypeStruct((n, D_MODEL), BF16),) * 2,
        compiler_params=_params(("parallel",), 32),
    )(mem2d, wk, wv)


def _xattn_kernel(x_ref, k_ref, v_ref, wq_ref, wo_ref, g_ref, b_ref, wr_ref, br_ref,
                  o_ref, ob_ref, lg_ref):
    x = x_ref[...]
    q = _dot(x.astype(BF16), wq_ref[...]).astype(BF16)
    outs = []
    for h in range(XATTN_HEADS):
        sl = slice(h * XATTN_HEAD_DIM, (h + 1) * XATTN_HEAD_DIM)
        s = _dot_nt(q[:, sl], k_ref[:, sl])
        p = jnp.exp(s - jnp.max(s, axis=-1, keepdims=True))
        l = jnp.sum(p, axis=-1, keepdims=True)
        outs.append((_dot(p.astype(BF16), v_ref[:, sl]) / l).astype(BF16))
    o = jnp.concatenate(outs, axis=1)
    xa = _dot(o, wo_ref[...])
    x2 = _layer_norm(DEEPNORM_ALPHA * x + xa, g_ref[...], b_ref[...])
    o_ref[...] = x2
    ob_ref[...] = x2.astype(BF16)
    lg_ref[...] = jnp.dot(x2, wr_ref[...], precision=lax.Precision.HIGHEST,
                          preferred_element_type=F32) + br_ref[...]


def _xattn(x1, kmem, vmem, wq, wo, g, b, wr, br, batch, seq, n_mem):
    n = x1.shape[0]
    tm = ROW_TILE
    nt = seq // tm
    row = lambda width: pl.BlockSpec((tm, width), lambda bi, i: (bi * nt + i, 0))
    full = lambda a: pl.BlockSpec(a.shape, lambda bi, i: (0, 0))
    kv = pl.BlockSpec((n_mem, D_MODEL), lambda bi, i: (bi, 0))
    return pl.pallas_call(
        _xattn_kernel,
        grid=(batch, nt),
        in_specs=[row(D_MODEL), kv, kv, full(wq), full(wo), full(g), full(b), full(wr), full(br)],
        out_specs=(row(D_MODEL), row(D_MODEL), row(LANES)),
        out_shape=(jax.ShapeDtypeStruct((n, D_MODEL), F32),
                   jax.ShapeDtypeStruct((n, D_MODEL), BF16),
                   jax.ShapeDtypeStruct((n, LANES), F32)),
        compiler_params=_params(("parallel", "parallel"), 48),
    )(x1, kmem, vmem, wq, wo, g, b, wr, br)


def _route_kernel(lg_ref, idx_ref, rank_ref, gate_ref, cnt_ref, carry_ref):
    i = pl.program_id(0)

    @pl.when(i == 0)
    def _():
        carry_ref[...] = jnp.zeros_like(carry_ref)

    lg = lg_ref[...]
    t = lg.shape[0]
    lane = lax.broadcasted_iota(jnp.int32, lg.shape, 1)
    lane_f = lane.astype(F32)
    sels, vals, idxs = [], [], []
    for _ in range(TOP_K):
        mx = jnp.max(lg, axis=-1, keepdims=True)
        first = jnp.min(jnp.where(lg == mx, lane_f, float(LANES)), axis=-1, keepdims=True)
        sel = lane_f == first
        sels.append(sel)
        vals.append(mx)
        idxs.append(first)
        lg = jnp.where(sel, -jnp.inf, lg)
    exps = [jnp.exp(v - vals[0]) for v in vals]
    tot = exps[0] + exps[1] + exps[2] + exps[3]

    selmat = (sels[0] | sels[1] | sels[2] | sels[3])
    r = lax.broadcasted_iota(jnp.int32, (t, t), 0)
    s = lax.broadcasted_iota(jnp.int32, (t, t), 1)
    strict = (s < r).astype(BF16)
    selb = selmat.astype(BF16)
    carry = carry_ref[0:1, :]
    rankmat = _dot(strict, selb) + carry
    new_carry = carry + jnp.sum(selmat.astype(F32), axis=0, keepdims=True)
    carry_ref[...] = jnp.broadcast_to(new_carry, carry_ref.shape)
    cnt_ref[...] = jnp.broadcast_to(new_carry, cnt_ref.shape).astype(jnp.int32)

    idx_out = jnp.zeros(lg.shape, jnp.int32)
    rank_out = jnp.zeros(lg.shape, F32)
    gate_out = jnp.zeros(lg.shape, F32)
    for k in range(TOP_K):
        here = lane == k
        rk = jnp.sum(jnp.where(sels[k], rankmat, 0.0), axis=-1, keepdims=True)
        idx_out = jnp.where(here, idxs[k].astype(jnp.int32), idx_out)
        rank_out = jnp.where(here, rk, rank_out)
        gate_out = jnp.where(here, exps[k] / tot, gate_out)
    idx_ref[...] = idx_out
    rank_ref[...] = rank_out.astype(jnp.int32)
    gate_ref[...] = gate_out


def _route(logits):
    n = logits.shape[0]
    t = SEQ_BLOCK
    row = pl.BlockSpec((t, LANES), lambda i: (i, 0))
    return pl.pallas_call(
        _route_kernel,
        grid=(n // t,),
        in_specs=[row],
        out_specs=(row, row, row, pl.BlockSpec((8, LANES), lambda i: (0, 0))),
        out_shape=(jax.ShapeDtypeStruct((n, LANES), jnp.int32),
                   jax.ShapeDtypeStruct((n, LANES), jnp.int32),
                   jax.ShapeDtypeStruct((n, LANES), F32),
                   jax.ShapeDtypeStruct((8, LANES), jnp.int32)),
        scratch_shapes=[pltpu.VMEM((8, LANES), F32)],
        compiler_params=_params(("arbitrary",), 32),
    )(logits)


def _expert_kernel(blk_e_ref, nb_ref, x_ref, wgu_ref, bgu_ref, wd_ref, bd_ref, y_ref,
                   wgu_sc, wd_sc, prev_ref):
    i = pl.program_id(0)
    e = blk_e_ref[i]

    @pl.when(i == 0)
    def _():
        prev_ref[0] = -1

    @pl.when(i < nb_ref[0])
    def _():
        @pl.when(e != prev_ref[0])
        def _():
            wgu_sc[...] = wgu_ref[0].astype(BF16)
            wd_sc[...] = wd_ref[0].astype(BF16)
            prev_ref[0] = e

        h = _dot(x_ref[...], wgu_sc[...]) + bgu_ref[0]
        gate = jnp.minimum(h[:, :D_EXPERT], SWIGLU_LIMIT)
        lin = jnp.clip(h[:, D_EXPERT:], -SWIGLU_LIMIT, SWIGLU_LIMIT)
        hid = gate * jax.nn.sigmoid(SWIGLU_ALPHA * gate) * (lin + 1.0)
        y_ref[...] = _dot(hid.astype(BF16), wd_sc[...]) + bd_ref[0]

    @pl.when(i >= nb_ref[0])
    def _():
        y_ref[...] = jnp.zeros_like(y_ref)


def _experts(blk_e, nb_used, xs, w_gu, b_gu, w_d, b_d):
    p = xs.shape[0]
    g = EXPERT_ROWS
    grid_spec = pltpu.PrefetchScalarGridSpec(
        num_scalar_prefetch=2,
        grid=(p // g,),
        in_specs=[pl.BlockSpec((g, D_MODEL), lambda i, be, nb: (i, 0)),
                  pl.BlockSpec((1, D_MODEL, 2 * D_EXPERT), lambda i, be, nb: (be[i], 0, 0)),
                  pl.BlockSpec((1, 1, 2 * D_EXPERT), lambda i, be, nb: (be[i], 0, 0)),
                  pl.BlockSpec((1, D_EXPERT, D_MODEL), lambda i, be, nb: (be[i], 0, 0)),
                  pl.BlockSpec((1, 1, D_MODEL), lambda i, be, nb: (be[i], 0, 0))],
        out_specs=pl.BlockSpec((g, D_MODEL), lambda i, be, nb: (i, 0)),
        scratch_shapes=[pltpu.VMEM((D_MODEL, 2 * D_EXPERT), BF16),
                        pltpu.VMEM((D_EXPERT, D_MODEL), BF16),
                        pltpu.SMEM((1,), jnp.int32)],
    )
    return pl.pallas_call(
        _expert_kernel,
        grid_spec=grid_spec,
        out_shape=jax.ShapeDtypeStruct((p, D_MODEL), F32),
        compiler_params=_params(("arbitrary",), 56),
    )(blk_e, nb_used, xs, w_gu, b_gu, w_d, b_d)


def _combine_kernel(y_ref, gate_ref, x_ref, g_ref, b_ref, o_ref):
    gate = gate_ref[...]
    ff = y_ref[0] * gate[:, 0:1]
    for k in range(1, TOP_K):
        ff = ff + y_ref[k] * gate[:, k:k + 1]
    o_ref[...] = _layer_norm(DEEPNORM_ALPHA * x_ref[...] + ff, g_ref[...], b_ref[...])


def _combine(yg, gate, x2, g, b):
    n = x2.shape[0]
    tm = SEQ_BLOCK
    row = lambda width: pl.BlockSpec((tm, width), lambda i: (i, 0))
    full = lambda a: pl.BlockSpec(a.shape, lambda i: (0, 0))
    return pl.pallas_call(
        _combine_kernel,
        grid=(n // tm,),
        in_specs=[pl.BlockSpec((TOP_K, tm, D_MODEL), lambda i: (0, i, 0)),
                  row(LANES), row(D_MODEL), full(g), full(b)],
        out_specs=row(D_MODEL),
        out_shape=jax.ShapeDtypeStruct((n, D_MODEL), F32),
        compiler_params=_params(("parallel",), 32),
    )(yg, gate, x2, g, b)


def _gather_rows(table, idx):
    return jnp.take(table, idx, axis=0)


def _layer(x, mem, w_in, fox_f_bias, conv_w, i_bias, f_bias, fox_g, mlstm_g, w_mix_out,
           ln1_g, ln1_b, w_xq, w_xk, w_xv, w_xo, ln2_g, ln2_b, w_router, b_router,
           w_gate_up, b_gate_up, w_down, b_down, ln3_g, ln3_b):
    batch, seq, d = x.shape
    n_mem = mem.shape[1]
    n = batch * seq
    x2d = x.reshape(n, d)

    o_ff = 3 * FOX_WIDTH
    o_mqk = o_ff + FOX_HEADS
    o_mv = o_mqk + 2 * MLSTM_QK_WIDTH
    o_mi = o_mv + MLSTM_V_WIDTH
    o_mf = o_mi + MLSTM_HEADS
    o_mo = o_mf + MLSTM_HEADS
    n_gate = FOX_HEADS + 2 * MLSTM_HEADS
    w_r = jnp.concatenate(
        [w_in[:, :o_ff], w_in[:, o_mqk:o_mi], w_in[:, o_mo:],
         w_in[:, o_ff:o_mqk], w_in[:, o_mi:o_mo],
         jnp.zeros((d, LANES - n_gate), w_in.dtype)], axis=1).astype(BF16)
    gate_bias = jnp.concatenate(
        [fox_f_bias, i_bias, f_bias, jnp.zeros((LANES - n_gate,), F32)]).reshape(1, LANES)

    fq, fk, fv, mqk, mv, mo, gates = _inproj(x2d, w_r, gate_bias)
    gcol, grow = _gateprep(gates, batch, seq)
    crow4 = grow.reshape(batch, 16, seq // SEQ_BLOCK, SEQ_BLOCK)
    fo = _fox(fq, fk, fv, crow4, fox_g.reshape(1, FOX_WIDTH), batch, seq)
    mo_out = _mlstm(mqk, mv, mo, gcol, grow, conv_w, mlstm_g.reshape(1, MLSTM_V_WIDTH), batch, seq)
    x1 = _outproj(fo, mo_out, w_mix_out.astype(BF16), x2d, ln1_g.reshape(1, d), ln1_b.reshape(1, d))

    kmem, vmem = _memkv(mem.reshape(batch * n_mem, d), w_xk.astype(BF16), w_xv.astype(BF16), n_mem)
    wr_pad = jnp.concatenate([w_router, jnp.zeros((d, LANES - N_EXPERTS), F32)], axis=1)
    br_pad = jnp.concatenate([b_router, jnp.full((LANES - N_EXPERTS,), NEG_BIG, F32)]).reshape(1, LANES)
    x2, x2b, logits = _xattn(x1, kmem, vmem, w_xq.astype(BF16), w_xo.astype(BF16),
                             ln2_g.reshape(1, d), ln2_b.reshape(1, d), wr_pad, br_pad,
                             batch, seq, n_mem)

    idx, rank, gate, cnt = _route(logits)
    counts = cnt[0, :N_EXPERTS]
    g_rows = EXPERT_ROWS
    padded = ((counts + g_rows - 1) // g_rows) * g_rows
    pad_end = jnp.cumsum(padded)
    pad_start = pad_end - padded
    pos = pad_start[idx[:, :TOP_K]] + rank[:, :TOP_K]
    p_rows = n * TOP_K + N_EXPERTS * g_rows
    nb = p_rows // g_rows
    tok = jnp.broadcast_to(jnp.arange(n, dtype=jnp.int32)[:, None], (n, TOP_K))
    row_tok = jnp.zeros((p_rows,), jnp.int32).at[pos.reshape(-1)].set(tok.reshape(-1))
    blk_e = jnp.minimum(
        jnp.searchsorted(pad_end, jnp.arange(nb, dtype=jnp.int32) * g_rows, side='right'),
        N_EXPERTS - 1).astype(jnp.int32)
    nb_used = (pad_end[-1:] // g_rows).astype(jnp.int32)

    xs = _gather_rows(x2b, row_tok)
    y = _experts(blk_e, nb_used, xs, w_gate_up, b_gate_up.reshape(N_EXPERTS, 1, -1),
                 w_down, b_down.reshape(N_EXPERTS, 1, -1))
    yg = _gather_rows(y, pos.T.reshape(-1)).reshape(TOP_K, n, d)
    out = _combine(yg, gate, x2, ln3_g.reshape(1, d), ln3_b.reshape(1, d))
    return out.reshape(batch, seq, d)


def kernel(x, mem, w_in, fox_f_bias, mlstm_conv_w, mlstm_i_bias, mlstm_f_bias, fox_norm_g, mlstm_norm_g, w_mix_out, ln1_g, ln1_b, w_xq, w_xk, w_xv, w_xo, ln2_g, ln2_b, w_router, b_router, w_gate_up, b_gate_up, w_down, b_down, ln3_g, ln3_b):
    for l in range(w_in.shape[0]):
        x = _layer(x, mem, w_in[l], fox_f_bias[l], mlstm_conv_w[l], mlstm_i_bias[l],
                   mlstm_f_bias[l], fox_norm_g[l], mlstm_norm_g[l], w_mix_out[l],
                   ln1_g[l], ln1_b[l], w_xq[l], w_xk[l], w_xv[l], w_xo[l], ln2_g[l], ln2_b[l],
                   w_router[l], b_router[l], w_gate_up[l], b_gate_up[l], w_down[l], b_down[l],
                   ln3_g[l], ln3_b[l])
    return x
```

```python
import jax
import jax.numpy as jnp
from jax import lax
from jax.experimental import pallas as pl
from jax.experimental.pallas import tpu as pltpu
from jax.experimental.pallas import tpu_sc as plsc

F32 = jnp.float32
BF16 = jnp.bfloat16

D_MODEL = 1024
FOX_HEADS = 8
FOX_HEAD_DIM = 64
FOX_WIDTH = FOX_HEADS * FOX_HEAD_DIM
MLSTM_HEADS = 4
MLSTM_QK_DIM = 64
MLSTM_V_DIM = 128
MLSTM_QK_WIDTH = MLSTM_HEADS * MLSTM_QK_DIM
MLSTM_V_WIDTH = MLSTM_HEADS * MLSTM_V_DIM
CONV_WIDTH = 4
XATTN_HEADS = 4
XATTN_HEAD_DIM = D_MODEL // XATTN_HEADS
N_EXPERTS = 32
TOP_K = 4
D_EXPERT = D_MODEL
SWIGLU_LIMIT = 7.0
SWIGLU_ALPHA = 1.702
DEEPNORM_ALPHA = 2.0 ** 0.25
LN_EPS = 1e-5
RMS_EPS = 1e-6

LANES = 128
SEQ_BLOCK = 256
ROW_TILE = 512
EXPERT_ROWS = 256
GATE_I0 = FOX_HEADS
GATE_F0 = FOX_HEADS + MLSTM_HEADS
NEG_BIG = -1e30

MIB = 1024 * 1024


def _params(semantics, vmem_mib):
    return pltpu.CompilerParams(dimension_semantics=semantics,
                                vmem_limit_bytes=vmem_mib * MIB)


def _layer_norm(y, g, b):
    mu = jnp.mean(y, axis=-1, keepdims=True)
    yc = y - mu
    var = jnp.mean(yc * yc, axis=-1, keepdims=True)
    return yc * lax.rsqrt(var + LN_EPS) * g + b


def _dot(a, b):
    return jnp.dot(a, b, preferred_element_type=F32)


def _dot_nt(a, b):
    return lax.dot_general(a, b, (((1,), (1,)), ((), ())), preferred_element_type=F32)


def _dot_tn(a, b):
    return lax.dot_general(a, b, (((0,), (0,)), ((), ())), preferred_element_type=F32)


def _pack_bf16_pairs(x):
    w = x.shape[1] // 2
    lo = pltpu.bitcast(x[:, :w].astype(BF16).astype(F32), jnp.uint32)
    hi = pltpu.bitcast(x[:, w:].astype(BF16).astype(F32), jnp.uint32)
    return (lo >> 16) | hi


def _unpack_bf16_pairs(u):
    lo = pltpu.bitcast(u << 16, F32).astype(BF16)
    hi = pltpu.bitcast(u & jnp.uint32(0xFFFF0000), F32).astype(BF16)
    return jnp.concatenate([lo, hi], axis=1)


def _inproj_kernel(x_ref, w_ref, gb_ref, fq_ref, fk_ref, fv_ref, mqk_ref, mv_ref, mo_ref, g_ref):
    xb = x_ref[...].astype(BF16)

    def mm(c0, width):
        return _dot(xb, w_ref[:, c0:c0 + width])

    fq_ref[...] = (mm(0, 512) * (FOX_HEAD_DIM ** -0.5)).astype(BF16)
    fk_ref[...] = mm(512, 512).astype(BF16)
    fv_ref[...] = mm(1024, 512).astype(BF16)
    mqk_ref[...] = mm(1536, 512)
    mv_ref[...] = mm(2048, 512).astype(BF16)
    mo_ref[...] = mm(2560, 512)
    g_ref[...] = mm(3072, LANES) + gb_ref[...]


def _inproj(x2d, w_r, gate_bias):
    n = x2d.shape[0]
    tm = ROW_TILE
    row = lambda width: pl.BlockSpec((tm, width), lambda i: (i, 0))
    full = lambda a: pl.BlockSpec(a.shape, lambda i: (0, 0))
    out_shapes = (
        jax.ShapeDtypeStruct((n, 512), BF16),
        jax.ShapeDtypeStruct((n, 512), BF16),
        jax.ShapeDtypeStruct((n, 512), BF16),
        jax.ShapeDtypeStruct((n, 512), F32),
        jax.ShapeDtypeStruct((n, 512), BF16),
        jax.ShapeDtypeStruct((n, 512), F32),
        jax.ShapeDtypeStruct((n, LANES), F32),
    )
    return pl.pallas_call(
        _inproj_kernel,
        grid=(n // tm,),
        in_specs=[row(D_MODEL), full(w_r), full(gate_bias)],
        out_specs=tuple(row(s.shape[1]) for s in out_shapes),
        out_shape=out_shapes,
        compiler_params=_params(("parallel",), 48),
    )(x2d, w_r, gate_bias)


def _gateprep_kernel(g_ref, col_ref, row_ref, carry_ref):
    c = pl.program_id(1)

    @pl.when(c == 0)
    def _():
        carry_ref[...] = jnp.zeros_like(carry_ref)

    g = g_ref[...]
    lane = lax.broadcasted_iota(jnp.int32, g.shape, 1)
    is_i = (lane >= GATE_I0) & (lane < GATE_F0)
    logsig = jnp.minimum(g, 0.0) - jnp.log1p(jnp.exp(-jnp.abs(g)))
    blk = g.shape[0]
    r = lax.broadcasted_iota(jnp.int32, (blk, blk), 0)
    s = lax.broadcasted_iota(jnp.int32, (blk, blk), 1)
    tri = (s <= r).astype(F32)
    cs = jnp.dot(tri, logsig, precision=lax.Precision.HIGHEST, preferred_element_type=F32)
    carry = carry_ref[0:1, :]
    glob = cs + carry
    carry_ref[...] = jnp.broadcast_to(glob[blk - 1:blk, :], carry_ref.shape)
    out = jnp.where(lane < GATE_I0, glob, jnp.where(is_i, g, cs))
    col_ref[...] = out
    row_ref[0] = out.T[0:16, :]


def _gateprep(gates, batch, seq):
    n = gates.shape[0]
    nc = seq // SEQ_BLOCK
    return pl.pallas_call(
        _gateprep_kernel,
        grid=(batch, nc),
        in_specs=[pl.BlockSpec((SEQ_BLOCK, LANES), lambda b, c: (b * nc + c, 0))],
        out_specs=(pl.BlockSpec((SEQ_BLOCK, LANES), lambda b, c: (b * nc + c, 0)),
                   pl.BlockSpec((1, 16, SEQ_BLOCK), lambda b, c: (b, 0, c))),
        out_shape=(jax.ShapeDtypeStruct((n, LANES), F32),
                   jax.ShapeDtypeStruct((batch, 16, seq), F32)),
        scratch_shapes=[pltpu.VMEM((8, LANES), F32)],
        compiler_params=_params(("parallel", "arbitrary"), 32),
    )(gates)


def _fox_kernel(q_ref, k_ref, v_ref, c_ref, gain_ref, o_ref, m_sc, l_sc, acc_sc):
    hp = pl.program_id(1)
    qi = pl.program_id(2)
    tq = q_ref.shape[0]
    tk = SEQ_BLOCK
    q = q_ref[...]
    lane = lax.broadcasted_iota(jnp.int32, (tq, LANES), 1)
    lo = lane < FOX_HEAD_DIM
    zero = jnp.zeros_like(q)
    qh = (jnp.where(lo, q, zero), jnp.where(lo, zero, q))

    m_sc[...] = jnp.full(m_sc.shape, -jnp.inf, F32)
    l_sc[...] = jnp.zeros(l_sc.shape, F32)
    acc_sc[...] = jnp.zeros(acc_sc.shape, F32)

    def block(ki, masked):
        k0 = pl.multiple_of(ki * tk, tk)
        k = k_ref[pl.ds(k0, tk), :]
        v = v_ref[pl.ds(k0, tk), :]
        for h in range(2):
            crow = c_ref[0, 2 * hp + h, pl.ds(ki, 1), :]
            s = _dot_nt(qh[h], k) - crow
            if masked:
                rr = lax.broadcasted_iota(jnp.int32, (tq, tk), 0)
                cc = lax.broadcasted_iota(jnp.int32, (tq, tk), 1)
                s = jnp.where(cc <= rr, s, -jnp.inf)
            m_prev = m_sc[h]
            m_new = jnp.maximum(m_prev, jnp.max(s, axis=-1, keepdims=True))
            p = jnp.exp(s - m_new)
            alpha = jnp.exp(m_prev - m_new)
            l_sc[h] = alpha * l_sc[h] + jnp.sum(p, axis=-1, keepdims=True)
            acc_sc[h] = alpha * acc_sc[h] + _dot(p.astype(BF16), v)
            m_sc[h] = m_new

    def body(ki, carry):
        block(ki, False)
        return carry

    lax.fori_loop(0, qi, body, 0)
    block(qi, True)

    o = jnp.where(lo, acc_sc[0] / l_sc[0], acc_sc[1] / l_sc[1])
    sq = o * o
    ss0 = jnp.sum(jnp.where(lo, sq, 0.0), axis=-1, keepdims=True)
    ss1 = jnp.sum(jnp.where(lo, 0.0, sq), axis=-1, keepdims=True)
    inv = jnp.where(lo, lax.rsqrt(ss0 / FOX_HEAD_DIM + RMS_EPS),
                    lax.rsqrt(ss1 / FOX_HEAD_DIM + RMS_EPS))
    o_ref[...] = (o * inv * gain_ref[...]).astype(o_ref.dtype)


def _fox(fq, fk, fv, crow4, fox_gain, batch, seq):
    n = fq.shape[0]
    tq = SEQ_BLOCK
    nq = seq // tq
    npair = FOX_HEADS // 2
    kv_spec = pl.BlockSpec((seq, LANES), lambda b, hp, qi: (b, hp))
    return pl.pallas_call(
        _fox_kernel,
        grid=(batch, npair, nq),
        in_specs=[pl.BlockSpec((tq, LANES), lambda b, hp, qi: (b * nq + qi, hp)),
                  kv_spec, kv_spec,
                  pl.BlockSpec((1, 16, nq, SEQ_BLOCK), lambda b, hp, qi: (b, 0, 0, 0)),
                  pl.BlockSpec((1, LANES), lambda b, hp, qi: (0, hp))],
        out_specs=pl.BlockSpec((tq, LANES), lambda b, hp, qi: (b * nq + qi, hp)),
        out_shape=jax.ShapeDtypeStruct((n, FOX_WIDTH), BF16),
        scratch_shapes=[pltpu.VMEM((2, tq, 1), F32), pltpu.VMEM((2, tq, 1), F32),
                        pltpu.VMEM((2, tq, LANES), F32)],
        compiler_params=_params(("parallel", "parallel", "arbitrary"), 32),
    )(fq, fk, fv, crow4, fox_gain)


def _mlstm_kernel(qk_ref, v_ref, og_ref, col_ref, row_ref, cw_ref, gain_ref, o_ref,
                  tail_ref, buf_ref, c_sc, m_sc):
    c = pl.program_id(1)
    L = SEQ_BLOCK

    @pl.when(c == 0)
    def _():
        tail_ref[...] = jnp.zeros_like(tail_ref)
        c_sc[...] = jnp.zeros_like(c_sc)
        m_sc[...] = jnp.zeros_like(m_sc)

    x = qk_ref[...]
    buf_ref[0:8, :] = tail_ref[...]
    buf_ref[8:8 + L, :] = x
    tail_ref[...] = x[L - 8:L, :]
    y = x * cw_ref[CONV_WIDTH - 1:CONV_WIDTH, :]
    for j in range(CONV_WIDTH - 1):
        shift = CONV_WIDTH - 1 - j
        y = y + buf_ref[8 - shift:8 - shift + L, :] * cw_ref[j:j + 1, :]
    y = y * jax.nn.sigmoid(y)
    qc = y[:, :MLSTM_QK_WIDTH].astype(BF16)
    kf = y[:, MLSTM_QK_WIDTH:] * (MLSTM_QK_DIM ** -0.5)

    col = col_ref[...]
    row = row_ref[0]
    lane = lax.broadcasted_iota(jnp.int32, (L, LANES), 1)
    lo = lane < MLSTM_QK_DIM
    rr = lax.broadcasted_iota(jnp.int32, (L, L), 0)
    cc = lax.broadcasted_iota(jnp.int32, (L, L), 1)
    causal = cc <= rr
    ones_col = (lane == 0).astype(BF16)

    for h in range(MLSTM_HEADS):
        pair, half = divmod(h, 2)
        sel = lo if half == 0 else jnp.logical_not(lo)
        q2 = qc[:, pair * LANES:(pair + 1) * LANES]
        k2 = kf[:, pair * LANES:(pair + 1) * LANES]
        qh = jnp.where(sel, q2, jnp.zeros_like(q2))
        kb = k2.astype(BF16)
        bcol = col[:, GATE_F0 + h:GATE_F0 + h + 1]
        licol = col[:, GATE_I0 + h:GATE_I0 + h + 1]
        brow = row[GATE_F0 + h:GATE_F0 + h + 1, :]
        lirow = row[GATE_I0 + h:GATE_I0 + h + 1, :]
        g = bcol[L - 1:L, :]
        m_prev = m_sc[h][0:1, 0:1]

        d = jnp.where(causal, bcol + (lirow - brow), -jnp.inf)
        inter_log = bcol + m_prev
        m_t = jnp.maximum(inter_log, jnp.max(d, axis=-1, keepdims=True))
        w_inter = jnp.exp(inter_log - m_t)
        p = jnp.exp(d - m_t) * _dot_nt(qh, kb)
        vaug = jnp.concatenate([v_ref[:, h * LANES:(h + 1) * LANES], ones_col], axis=1)
        cstate = c_sc[h]
        tot = w_inter * _dot(qh, cstate.astype(BF16)) + _dot(p.astype(BF16), vaug)
        num = tot[:, :LANES]
        den = tot[:, LANES:LANES + 1]
        hout = num / jnp.maximum(jnp.abs(den), jnp.exp(-m_t))

        a = g - bcol + licol
        m_loc = jnp.max(a, axis=0, keepdims=True)
        wa = jnp.exp(a - m_loc)
        kw = jnp.where(sel, k2 * wa, 0.0).astype(BF16)
        kv = _dot_tn(kw, vaug)
        m_new = jnp.maximum(g + m_prev, m_loc)
        c_sc[h] = jnp.exp(g + m_prev - m_new) * cstate + jnp.exp(m_loc - m_new) * kv
        m_sc[h] = jnp.broadcast_to(m_new, m_sc.shape[1:])

        ms = jnp.mean(hout * hout, axis=-1, keepdims=True)
        hn = hout * lax.rsqrt(ms + RMS_EPS) * gain_ref[:, h * LANES:(h + 1) * LANES]
        gate = jax.nn.sigmoid(og_ref[:, h * LANES:(h + 1) * LANES])
        o_ref[:, h * LANES:(h + 1) * LANES] = (hn * gate).astype(o_ref.dtype)


def _mlstm(mqk, mv, mo, gcol, grow, conv_w, gain, batch, seq):
    n = mqk.shape[0]
    L = SEQ_BLOCK
    nc = seq // L
    row = lambda width: pl.BlockSpec((L, width), lambda b, c: (b * nc + c, 0))
    full = lambda a: pl.BlockSpec(a.shape, lambda b, c: (0, 0))
    return pl.pallas_call(
        _mlstm_kernel,
        grid=(batch, nc),
        in_specs=[row(512), row(512), row(512), row(LANES),
                  pl.BlockSpec((1, 16, L), lambda b, c: (b, 0, c)),
                  full(conv_w), full(gain)],
        out_specs=row(512),
        out_shape=jax.ShapeDtypeStruct((n, MLSTM_V_WIDTH), BF16),
        scratch_shapes=[pltpu.VMEM((8, 512), F32), pltpu.VMEM((8 + L, 512), F32),
                        pltpu.VMEM((MLSTM_HEADS, LANES, 2 * LANES), F32),
                        pltpu.VMEM((MLSTM_HEADS, 8, LANES), F32)],
        compiler_params=_params(("parallel", "arbitrary"), 48),
    )(mqk, mv, mo, gcol, grow, conv_w, gain)


def _outproj_kernel(fo_ref, mo_ref, w_ref, x_ref, g_ref, b_ref, o_ref):
    mix = _dot(fo_ref[...], w_ref[0:FOX_WIDTH, :]) + _dot(mo_ref[...], w_ref[FOX_WIDTH:, :])
    o_ref[...] = _layer_norm(DEEPNORM_ALPHA * x_ref[...] + mix, g_ref[...], b_ref[...])


def _outproj(fo, mo, w_out, x2d, g, b):
    n = x2d.shape[0]
    tm = ROW_TILE
    row = lambda width: pl.BlockSpec((tm, width), lambda i: (i, 0))
    full = lambda a: pl.BlockSpec(a.shape, lambda i: (0, 0))
    return pl.pallas_call(
        _outproj_kernel,
        grid=(n // tm,),
        in_specs=[row(512), row(512), full(w_out), row(D_MODEL), full(g), full(b)],
        out_specs=row(D_MODEL),
        out_shape=jax.ShapeDtypeStruct((n, D_MODEL), F32),
        compiler_params=_params(("parallel",), 48),
    )(fo, mo, w_out, x2d, g, b)


def _memkv_kernel(mem_ref, wk_ref, wv_ref, k_ref, v_ref):
    mb = mem_ref[...].astype(BF16)
    k_ref[...] = (_dot(mb, wk_ref[...]) * (XATTN_HEAD_DIM ** -0.5)).astype(BF16)
    v_ref[...] = _dot(mb, wv_ref[...]).astype(BF16)


def _memkv(mem2d, wk, wv, n_mem):
    n = mem2d.shape[0]
    row = pl.BlockSpec((n_mem, D_MODEL), lambda i: (i, 0))
    full = lambda a: pl.BlockSpec(a.shape, lambda i: (0, 0))
    return pl.pallas_call(
        _memkv_kernel,
        grid=(n // n_mem,),
        in_specs=[row, full(wk), full(wv)],
        out_specs=(row, row),
        out_shape=(jax.ShapeDtypeStruct((n, D_MODEL), BF16),) * 2,
        compiler_params=_params(("parallel",), 32),
    )(mem2d, wk, wv)


def _xattn_kernel(x_ref, k_ref, v_ref, wq_ref, wo_ref, g_ref, b_ref, wr_ref, br_ref,
                  o_ref, ob_ref, lg_ref):
    x = x_ref[...]
    q = _dot(x.astype(BF16), wq_ref[...]).astype(BF16)
    outs = []
    for h in range(XATTN_HEADS):
        sl = slice(h * XATTN_HEAD_DIM, (h + 1) * XATTN_HEAD_DIM)
        s = _dot_nt(q[:, sl], k_ref[:, sl])
        p = jnp.exp(s - jnp.max(s, axis=-1, keepdims=True))
        l = jnp.sum(p, axis=-1, keepdims=True)
        outs.append((_dot(p.astype(BF16), v_ref[:, sl]) / l).astype(BF16))
    o = jnp.concatenate(outs, axis=1)
    xa = _dot(o, wo_ref[...])
    x2 = _layer_norm(DEEPNORM_ALPHA * x + xa, g_ref[...], b_ref[...])
    o_ref[...] = x2
    ob_ref[...] = _pack_bf16_pairs(x2)
    lg_ref[...] = jnp.dot(x2, wr_ref[...], precision=lax.Precision.HIGHEST,
                          preferred_element_type=F32) + br_ref[...]


def _xattn(x1, kmem, vmem, wq, wo, g, b, wr, br, batch, seq, n_mem):
    n = x1.shape[0]
    tm = ROW_TILE
    nt = seq // tm
    row = lambda width: pl.BlockSpec((tm, width), lambda bi, i: (bi * nt + i, 0))
    full = lambda a: pl.BlockSpec(a.shape, lambda bi, i: (0, 0))
    kv = pl.BlockSpec((n_mem, D_MODEL), lambda bi, i: (bi, 0))
    return pl.pallas_call(
        _xattn_kernel,
        grid=(batch, nt),
        in_specs=[row(D_MODEL), kv, kv, full(wq), full(wo), full(g), full(b), full(wr), full(br)],
        out_specs=(row(D_MODEL), row(D_MODEL // 2), row(LANES)),
        out_shape=(jax.ShapeDtypeStruct((n, D_MODEL), F32),
                   jax.ShapeDtypeStruct((n, D_MODEL // 2), jnp.uint32),
                   jax.ShapeDtypeStruct((n, LANES), F32)),
        compiler_params=_params(("parallel", "parallel"), 48),
    )(x1, kmem, vmem, wq, wo, g, b, wr, br)


def _route_kernel(lg_ref, idx_ref, rank_ref, gate_ref, cnt_ref, carry_ref):
    i = pl.program_id(0)

    @pl.when(i == 0)
    def _():
        carry_ref[...] = jnp.zeros_like(carry_ref)

    lg = lg_ref[...]
    t = lg.shape[0]
    lane = lax.broadcasted_iota(jnp.int32, lg.shape, 1)
    lane_f = lane.astype(F32)
    sels, vals, idxs = [], [], []
    for _ in range(TOP_K):
        mx = jnp.max(lg, axis=-1, keepdims=True)
        first = jnp.min(jnp.where(lg == mx, lane_f, float(LANES)), axis=-1, keepdims=True)
        sel = lane_f == first
        sels.append(sel)
        vals.append(mx)
        idxs.append(first)
        lg = jnp.where(sel, -jnp.inf, lg)
    exps = [jnp.exp(v - vals[0]) for v in vals]
    tot = exps[0] + exps[1] + exps[2] + exps[3]

    selmat = (sels[0] | sels[1] | sels[2] | sels[3])
    r = lax.broadcasted_iota(jnp.int32, (t, t), 0)
    s = lax.broadcasted_iota(jnp.int32, (t, t), 1)
    strict = (s < r).astype(BF16)
    selb = selmat.astype(BF16)
    carry = carry_ref[0:1, :]
    rankmat = _dot(strict, selb) + carry
    new_carry = carry + jnp.sum(selmat.astype(F32), axis=0, keepdims=True)
    carry_ref[...] = jnp.broadcast_to(new_carry, carry_ref.shape)
    cnt_ref[...] = jnp.broadcast_to(new_carry, cnt_ref.shape).astype(jnp.int32)

    idx_out = jnp.zeros(lg.shape, jnp.int32)
    rank_out = jnp.zeros(lg.shape, F32)
    gate_out = jnp.zeros(lg.shape, F32)
    for k in range(TOP_K):
        here = lane == k
        rk = jnp.sum(jnp.where(sels[k], rankmat, 0.0), axis=-1, keepdims=True)
        idx_out = jnp.where(here, idxs[k].astype(jnp.int32), idx_out)
        rank_out = jnp.where(here, rk, rank_out)
        gate_out = jnp.where(here, exps[k] / tot, gate_out)
    idx_ref[...] = idx_out
    rank_ref[...] = rank_out.astype(jnp.int32)
    gate_ref[...] = gate_out


def _route(logits):
    n = logits.shape[0]
    t = SEQ_BLOCK
    row = pl.BlockSpec((t, LANES), lambda i: (i, 0))
    return pl.pallas_call(
        _route_kernel,
        grid=(n // t,),
        in_specs=[row],
        out_specs=(row, row, row, pl.BlockSpec((8, LANES), lambda i: (0, 0))),
        out_shape=(jax.ShapeDtypeStruct((n, LANES), jnp.int32),
                   jax.ShapeDtypeStruct((n, LANES), jnp.int32),
                   jax.ShapeDtypeStruct((n, LANES), F32),
                   jax.ShapeDtypeStruct((8, LANES), jnp.int32)),
        scratch_shapes=[pltpu.VMEM((8, LANES), F32)],
        compiler_params=_params(("arbitrary",), 32),
    )(logits)


def _expert_kernel(blk_e_ref, nb_ref, x_ref, wgu_ref, bgu_ref, wd_ref, bd_ref, y_ref,
                   wgu_sc, wd_sc, prev_ref):
    i = pl.program_id(0)
    e = blk_e_ref[i]

    @pl.when(i == 0)
    def _():
        prev_ref[0] = -1

    @pl.when(i < nb_ref[0])
    def _():
        @pl.when(e != prev_ref[0])
        def _():
            wgu_sc[...] = wgu_ref[0].astype(BF16)
            wd_sc[...] = wd_ref[0].astype(BF16)
            prev_ref[0] = e

        h = _dot(_unpack_bf16_pairs(x_ref[...]), wgu_sc[...]) + bgu_ref[0]
        gate = jnp.minimum(h[:, :D_EXPERT], SWIGLU_LIMIT)
        lin = jnp.clip(h[:, D_EXPERT:], -SWIGLU_LIMIT, SWIGLU_LIMIT)
        hid = gate * jax.nn.sigmoid(SWIGLU_ALPHA * gate) * (lin + 1.0)
        y_ref[...] = _dot(hid.astype(BF16), wd_sc[...]) + bd_ref[0]

    @pl.when(i >= nb_ref[0])
    def _():
        y_ref[...] = jnp.zeros_like(y_ref)


def _experts(blk_e, nb_used, xs, w_gu, b_gu, w_d, b_d):
    p = xs.shape[0]
    g = EXPERT_ROWS
    grid_spec = pltpu.PrefetchScalarGridSpec(
        num_scalar_prefetch=2,
        grid=(p // g,),
        in_specs=[pl.BlockSpec((g, D_MODEL // 2), lambda i, be, nb: (i, 0)),
                  pl.BlockSpec((1, D_MODEL, 2 * D_EXPERT), lambda i, be, nb: (be[i], 0, 0)),
                  pl.BlockSpec((1, 1, 2 * D_EXPERT), lambda i, be, nb: (be[i], 0, 0)),
                  pl.BlockSpec((1, D_EXPERT, D_MODEL), lambda i, be, nb: (be[i], 0, 0)),
                  pl.BlockSpec((1, 1, D_MODEL), lambda i, be, nb: (be[i], 0, 0))],
        out_specs=pl.BlockSpec((g, D_MODEL), lambda i, be, nb: (i, 0)),
        scratch_shapes=[pltpu.VMEM((D_MODEL, 2 * D_EXPERT), BF16),
                        pltpu.VMEM((D_EXPERT, D_MODEL), BF16),
                        pltpu.SMEM((1,), jnp.int32)],
    )
    return pl.pallas_call(
        _expert_kernel,
        grid_spec=grid_spec,
        out_shape=jax.ShapeDtypeStruct((p, D_MODEL), F32),
        compiler_params=_params(("arbitrary",), 56),
    )(blk_e, nb_used, xs, w_gu, b_gu, w_d, b_d)


def _combine_kernel(y_ref, gate_ref, x_ref, g_ref, b_ref, o_ref):
    gate = gate_ref[...]
    ff = y_ref[0] * gate[:, 0:1]
    for k in range(1, TOP_K):
        ff = ff + y_ref[k] * gate[:, k:k + 1]
    o_ref[...] = _layer_norm(DEEPNORM_ALPHA * x_ref[...] + ff, g_ref[...], b_ref[...])


def _combine(yg, gate, x2, g, b):
    n = x2.shape[0]
    tm = SEQ_BLOCK
    row = lambda width: pl.BlockSpec((tm, width), lambda i: (i, 0))
    full = lambda a: pl.BlockSpec(a.shape, lambda i: (0, 0))
    return pl.pallas_call(
        _combine_kernel,
        grid=(n // tm,),
        in_specs=[pl.BlockSpec((TOP_K, tm, D_MODEL), lambda i: (0, i, 0)),
                  row(LANES), row(D_MODEL), full(g), full(b)],
        out_specs=row(D_MODEL),
        out_shape=jax.ShapeDtypeStruct((n, D_MODEL), F32),
        compiler_params=_params(("parallel",), 32),
    )(yg, gate, x2, g, b)


SC_GATHER_WINDOW = 128
SC_GATHER_PIECE = 256


def _gather_rows(table, idx):
    k = table.shape[1] // SC_GATHER_PIECE
    pieces = table.reshape(table.shape[0] * k, SC_GATHER_PIECE)
    idx_k = (idx[:, None] * k + jnp.arange(k, dtype=jnp.int32)[None, :]).reshape(-1)
    return _gather_pieces(pieces, idx_k).reshape(idx.shape[0], table.shape[1])


def _gather_pieces(table, idx):
    m = idx.shape[0]
    width = table.shape[1]
    window = SC_GATHER_WINDOW
    mesh = plsc.VectorSubcoreMesh(core_axis_name="core", subcore_axis_name="subcore")

    @pl.kernel(out_type=jax.ShapeDtypeStruct((m, width), table.dtype), mesh=mesh)
    def gather_kernel(table_hbm, idx_hbm, out_hbm):
        def body(idx_vmem, out_vmem):
            pltpu.sync_copy(table_hbm.at[idx_vmem.at[0]], out_vmem)

        pltpu.emit_pipeline(
            body,
            grid=(m // window,),
            in_specs=[pl.BlockSpec((1, window), lambda i: (0, i))],
            out_specs=[pl.BlockSpec((window, width), lambda i: (i, 0))],
            core_axis_name=("core", "subcore"),
            dimension_semantics=(pltpu.PARALLEL,),
        )(idx_hbm, out_hbm)

    return gather_kernel(table, idx.reshape(1, m))


def _layer(x, mem, w_in, fox_f_bias, conv_w, i_bias, f_bias, fox_g, mlstm_g, w_mix_out,
           ln1_g, ln1_b, w_xq, w_xk, w_xv, w_xo, ln2_g, ln2_b, w_router, b_router,
           w_gate_up, b_gate_up, w_down, b_down, ln3_g, ln3_b):
    batch, seq, d = x.shape
    n_mem = mem.shape[1]
    n = batch * seq
    x2d = x.reshape(n, d)

    o_ff = 3 * FOX_WIDTH
    o_mqk = o_ff + FOX_HEADS
    o_mv = o_mqk + 2 * MLSTM_QK_WIDTH
    o_mi = o_mv + MLSTM_V_WIDTH
    o_mf = o_mi + MLSTM_HEADS
    o_mo = o_mf + MLSTM_HEADS
    n_gate = FOX_HEADS + 2 * MLSTM_HEADS
    w_r = jnp.concatenate(
        [w_in[:, :o_ff], w_in[:, o_mqk:o_mi], w_in[:, o_mo:],
         w_in[:, o_ff:o_mqk], w_in[:, o_mi:o_mo],
         jnp.zeros((d, LANES - n_gate), w_in.dtype)], axis=1).astype(BF16)
    gate_bias = jnp.concatenate(
        [fox_f_bias, i_bias, f_bias, jnp.zeros((LANES - n_gate,), F32)]).reshape(1, LANES)

    fq, fk, fv, mqk, mv, mo, gates = _inproj(x2d, w_r, gate_bias)
    gcol, grow = _gateprep(gates, batch, seq)
    crow4 = grow.reshape(batch, 16, seq // SEQ_BLOCK, SEQ_BLOCK)
    fo = _fox(fq, fk, fv, crow4, fox_g.reshape(1, FOX_WIDTH), batch, seq)
    mo_out = _mlstm(mqk, mv, mo, gcol, grow, conv_w, mlstm_g.reshape(1, MLSTM_V_WIDTH), batch, seq)
    x1 = _outproj(fo, mo_out, w_mix_out.astype(BF16), x2d, ln1_g.reshape(1, d), ln1_b.reshape(1, d))

    kmem, vmem = _memkv(mem.reshape(batch * n_mem, d), w_xk.astype(BF16), w_xv.astype(BF16), n_mem)
    wr_pad = jnp.concatenate([w_router, jnp.zeros((d, LANES - N_EXPERTS), F32)], axis=1)
    br_pad = jnp.concatenate([b_router, jnp.full((LANES - N_EXPERTS,), NEG_BIG, F32)]).reshape(1, LANES)
    x2, x2b, logits = _xattn(x1, kmem, vmem, w_xq.astype(BF16), w_xo.astype(BF16),
                             ln2_g.reshape(1, d), ln2_b.reshape(1, d), wr_pad, br_pad,
                             batch, seq, n_mem)

    idx, rank, gate, cnt = _route(logits)
    counts = cnt[0, :N_EXPERTS]
    g_rows = EXPERT_ROWS
    padded = ((counts + g_rows - 1) // g_rows) * g_rows
    pad_end = jnp.cumsum(padded)
    pad_start = pad_end - padded
    pos = pad_start[idx[:, :TOP_K]] + rank[:, :TOP_K]
    p_rows = n * TOP_K + N_EXPERTS * g_rows
    nb = p_rows // g_rows
    tok = jnp.broadcast_to(jnp.arange(n, dtype=jnp.int32)[:, None], (n, TOP_K))
    row_tok = jnp.zeros((p_rows,), jnp.int32).at[pos.reshape(-1)].set(tok.reshape(-1))
    blk_e = jnp.minimum(
        jnp.searchsorted(pad_end, jnp.arange(nb, dtype=jnp.int32) * g_rows, side='right'),
        N_EXPERTS - 1).astype(jnp.int32)
    nb_used = (pad_end[-1:] // g_rows).astype(jnp.int32)

    xs = _gather_rows(x2b, row_tok)
    y = _experts(blk_e, nb_used, xs, w_gate_up, b_gate_up.reshape(N_EXPERTS, 1, -1),
                 w_down, b_down.reshape(N_EXPERTS, 1, -1))
    yg = _gather_rows(y, pos.T.reshape(-1)).reshape(TOP_K, n, d)
    out = _combine(yg, gate, x2, ln3_g.reshape(1, d), ln3_b.reshape(1, d))
    return out.reshape(batch, seq, d)


def kernel(x, mem, w_in, fox_f_bias, mlstm_conv_w, mlstm_i_bias, mlstm_f_bias, fox_norm_g, mlstm_norm_g, w_mix_out, ln1_g, ln1_b, w_xq, w_xk, w_xv, w_xo, ln2_g, ln2_b, w_router, b_router, w_gate_up, b_gate_up, w_down, b_down, ln3_g, ln3_b):
    for l in range(w_in.shape[0]):
        x = _layer(x, mem, w_in[l], fox_f_bias[l], mlstm_conv_w[l], mlstm_i_bias[l],
                   mlstm_f_bias[l], fox_norm_g[l], mlstm_norm_g[l], w_mix_out[l],
                   ln1_g[l], ln1_b[l], w_xq[l], w_xk[l], w_xv[l], w_xo[l], ln2_g[l], ln2_b[l],
                   w_router[l], b_router[l], w_gate_up[l], b_gate_up[l], w_down[l], b_down[l],
                   ln3_g[l], ln3_b[l])
    return x
```

```python
import jax
import jax.numpy as jnp
from jax import lax
from jax.experimental import pallas as pl
from jax.experimental.pallas import tpu as pltpu
from jax.experimental.pallas import tpu_sc as plsc

F32 = jnp.float32
BF16 = jnp.bfloat16

D_MODEL = 1024
FOX_HEADS = 8
FOX_HEAD_DIM = 64
FOX_WIDTH = FOX_HEADS * FOX_HEAD_DIM
MLSTM_HEADS = 4
MLSTM_QK_DIM = 64
MLSTM_V_DIM = 128
MLSTM_QK_WIDTH = MLSTM_HEADS * MLSTM_QK_DIM
MLSTM_V_WIDTH = MLSTM_HEADS * MLSTM_V_DIM
CONV_WIDTH = 4
XATTN_HEADS = 4
XATTN_HEAD_DIM = D_MODEL // XATTN_HEADS
N_EXPERTS = 32
TOP_K = 4
D_EXPERT = D_MODEL
SWIGLU_LIMIT = 7.0
SWIGLU_ALPHA = 1.702
DEEPNORM_ALPHA = 2.0 ** 0.25
LN_EPS = 1e-5
RMS_EPS = 1e-6

LANES = 128
SEQ_BLOCK = 256
ROW_TILE = 512
EXPERT_ROWS = 256
X_PLANES = D_MODEL // 2 // LANES
Y_PLANES = D_MODEL // LANES
GATE_I0 = FOX_HEADS
GATE_F0 = FOX_HEADS + MLSTM_HEADS
NEG_BIG = -1e30

MIB = 1024 * 1024


def _params(semantics, vmem_mib):
    return pltpu.CompilerParams(dimension_semantics=semantics,
                                vmem_limit_bytes=vmem_mib * MIB)


def _layer_norm(y, g, b):
    mu = jnp.mean(y, axis=-1, keepdims=True)
    yc = y - mu
    var = jnp.mean(yc * yc, axis=-1, keepdims=True)
    return yc * lax.rsqrt(var + LN_EPS) * g + b


def _dot(a, b):
    return jnp.dot(a, b, preferred_element_type=F32)


def _dot_nt(a, b):
    return lax.dot_general(a, b, (((1,), (1,)), ((), ())), preferred_element_type=F32)


def _dot_tn(a, b):
    return lax.dot_general(a, b, (((0,), (0,)), ((), ())), preferred_element_type=F32)


def _pack_bf16_pairs(x):
    w = x.shape[1] // 2
    lo = pltpu.bitcast(x[:, :w].astype(BF16).astype(F32), jnp.uint32)
    hi = pltpu.bitcast(x[:, w:].astype(BF16).astype(F32), jnp.uint32)
    return (lo >> 16) | hi


def _unpack_bf16_pairs(u):
    lo = pltpu.bitcast(u << 16, F32).astype(BF16)
    hi = pltpu.bitcast(u & jnp.uint32(0xFFFF0000), F32).astype(BF16)
    return jnp.concatenate([lo, hi], axis=1)


def _inproj_kernel(x_ref, w_ref, gb_ref, fq_ref, fk_ref, fv_ref, mqk_ref, mv_ref, mo_ref, g_ref):
    xb = x_ref[...].astype(BF16)

    def mm(c0, width):
        return _dot(xb, w_ref[:, c0:c0 + width])

    fq_ref[...] = (mm(0, 512) * (FOX_HEAD_DIM ** -0.5)).astype(BF16)
    fk_ref[...] = mm(512, 512).astype(BF16)
    fv_ref[...] = mm(1024, 512).astype(BF16)
    mqk_ref[...] = mm(1536, 512)
    mv_ref[...] = mm(2048, 512).astype(BF16)
    mo_ref[...] = mm(2560, 512)
    g_ref[...] = mm(3072, LANES) + gb_ref[...]


def _inproj(x2d, w_r, gate_bias):
    n = x2d.shape[0]
    tm = ROW_TILE
    row = lambda width: pl.BlockSpec((tm, width), lambda i: (i, 0))
    full = lambda a: pl.BlockSpec(a.shape, lambda i: (0, 0))
    out_shapes = (
        jax.ShapeDtypeStruct((n, 512), BF16),
        jax.ShapeDtypeStruct((n, 512), BF16),
        jax.ShapeDtypeStruct((n, 512), BF16),
        jax.ShapeDtypeStruct((n, 512), F32),
        jax.ShapeDtypeStruct((n, 512), BF16),
        jax.ShapeDtypeStruct((n, 512), F32),
        jax.ShapeDtypeStruct((n, LANES), F32),
    )
    return pl.pallas_call(
        _inproj_kernel,
        grid=(n // tm,),
        in_specs=[row(D_MODEL), full(w_r), full(gate_bias)],
        out_specs=tuple(row(s.shape[1]) for s in out_shapes),
        out_shape=out_shapes,
        compiler_params=_params(("parallel",), 48),
    )(x2d, w_r, gate_bias)


def _gateprep_kernel(g_ref, col_ref, row_ref, carry_ref):
    c = pl.program_id(1)

    @pl.when(c == 0)
    def _():
        carry_ref[...] = jnp.zeros_like(carry_ref)

    g = g_ref[...]
    lane = lax.broadcasted_iota(jnp.int32, g.shape, 1)
    is_i = (lane >= GATE_I0) & (lane < GATE_F0)
    logsig = jnp.minimum(g, 0.0) - jnp.log1p(jnp.exp(-jnp.abs(g)))
    blk = g.shape[0]
    r = lax.broadcasted_iota(jnp.int32, (blk, blk), 0)
    s = lax.broadcasted_iota(jnp.int32, (blk, blk), 1)
    tri = (s <= r).astype(F32)
    cs = jnp.dot(tri, logsig, precision=lax.Precision.HIGHEST, preferred_element_type=F32)
    carry = carry_ref[0:1, :]
    glob = cs + carry
    carry_ref[...] = jnp.broadcast_to(glob[blk - 1:blk, :], carry_ref.shape)
    out = jnp.where(lane < GATE_I0, glob, jnp.where(is_i, g, cs))
    col_ref[...] = out
    row_ref[0] = out.T[0:16, :]


def _gateprep(gates, batch, seq):
    n = gates.shape[0]
    nc = seq // SEQ_BLOCK
    return pl.pallas_call(
        _gateprep_kernel,
        grid=(batch, nc),
        in_specs=[pl.BlockSpec((SEQ_BLOCK, LANES), lambda b, c: (b * nc + c, 0))],
        out_specs=(pl.BlockSpec((SEQ_BLOCK, LANES), lambda b, c: (b * nc + c, 0)),
                   pl.BlockSpec((1, 16, SEQ_BLOCK), lambda b, c: (b, 0, c))),
        out_shape=(jax.ShapeDtypeStruct((n, LANES), F32),
                   jax.ShapeDtypeStruct((batch, 16, seq), F32)),
        scratch_shapes=[pltpu.VMEM((8, LANES), F32)],
        compiler_params=_params(("parallel", "arbitrary"), 32),
    )(gates)


def _fox_kernel(q_ref, k_ref, v_ref, c_ref, gain_ref, o_ref, m_sc, l_sc, acc_sc):
    hp = pl.program_id(1)
    qi = pl.program_id(2)
    tq = q_ref.shape[0]
    tk = SEQ_BLOCK
    q = q_ref[...]
    lane = lax.broadcasted_iota(jnp.int32, (tq, LANES), 1)
    lo = lane < FOX_HEAD_DIM
    zero = jnp.zeros_like(q)
    qh = (jnp.where(lo, q, zero), jnp.where(lo, zero, q))

    m_sc[...] = jnp.full(m_sc.shape, -jnp.inf, F32)
    l_sc[...] = jnp.zeros(l_sc.shape, F32)
    acc_sc[...] = jnp.zeros(acc_sc.shape, F32)

    def block(ki, masked):
        k0 = pl.multiple_of(ki * tk, tk)
        k = k_ref[pl.ds(k0, tk), :]
        v = v_ref[pl.ds(k0, tk), :]
        for h in range(2):
            crow = c_ref[0, 2 * hp + h, pl.ds(ki, 1), :]
            s = _dot_nt(qh[h], k) - crow
            if masked:
                rr = lax.broadcasted_iota(jnp.int32, (tq, tk), 0)
                cc = lax.broadcasted_iota(jnp.int32, (tq, tk), 1)
                s = jnp.where(cc <= rr, s, -jnp.inf)
            m_prev = m_sc[h]
            m_new = jnp.maximum(m_prev, jnp.max(s, axis=-1, keepdims=True))
            p = jnp.exp(s - m_new)
            alpha = jnp.exp(m_prev - m_new)
            l_sc[h] = alpha * l_sc[h] + jnp.sum(p, axis=-1, keepdims=True)
            acc_sc[h] = alpha * acc_sc[h] + _dot(p.astype(BF16), v)
            m_sc[h] = m_new

    def body(ki, carry):
        block(ki, False)
        return carry

    lax.fori_loop(0, qi, body, 0)
    block(qi, True)

    o = jnp.where(lo, acc_sc[0] / l_sc[0], acc_sc[1] / l_sc[1])
    sq = o * o
    ss0 = jnp.sum(jnp.where(lo, sq, 0.0), axis=-1, keepdims=True)
    ss1 = jnp.sum(jnp.where(lo, 0.0, sq), axis=-1, keepdims=True)
    inv = jnp.where(lo, lax.rsqrt(ss0 / FOX_HEAD_DIM + RMS_EPS),
                    lax.rsqrt(ss1 / FOX_HEAD_DIM + RMS_EPS))
    o_ref[...] = (o * inv * gain_ref[...]).astype(o_ref.dtype)


def _fox(fq, fk, fv, crow4, fox_gain, batch, seq):
    n = fq.shape[0]
    tq = SEQ_BLOCK
    nq = seq // tq
    npair = FOX_HEADS // 2
    kv_spec = pl.BlockSpec((seq, LANES), lambda b, hp, qi: (b, hp))
    return pl.pallas_call(
        _fox_kernel,
        grid=(batch, npair, nq),
        in_specs=[pl.BlockSpec((tq, LANES), lambda b, hp, qi: (b * nq + qi, hp)),
                  kv_spec, kv_spec,
                  pl.BlockSpec((1, 16, nq, SEQ_BLOCK), lambda b, hp, qi: (b, 0, 0, 0)),
                  pl.BlockSpec((1, LANES), lambda b, hp, qi: (0, hp))],
        out_specs=pl.BlockSpec((tq, LANES), lambda b, hp, qi: (b * nq + qi, hp)),
        out_shape=jax.ShapeDtypeStruct((n, FOX_WIDTH), BF16),
        scratch_shapes=[pltpu.VMEM((2, tq, 1), F32), pltpu.VMEM((2, tq, 1), F32),
                        pltpu.VMEM((2, tq, LANES), F32)],
        compiler_params=_params(("parallel", "parallel", "arbitrary"), 32),
    )(fq, fk, fv, crow4, fox_gain)


def _mlstm_kernel(qk_ref, v_ref, og_ref, col_ref, row_ref, cw_ref, gain_ref, o_ref,
                  tail_ref, buf_ref, c_sc, m_sc):
    c = pl.program_id(1)
    L = SEQ_BLOCK

    @pl.when(c == 0)
    def _():
        tail_ref[...] = jnp.zeros_like(tail_ref)
        c_sc[...] = jnp.zeros_like(c_sc)
        m_sc[...] = jnp.zeros_like(m_sc)

    x = qk_ref[...]
    buf_ref[0:8, :] = tail_ref[...]
    buf_ref[8:8 + L, :] = x
    tail_ref[...] = x[L - 8:L, :]
    y = x * cw_ref[CONV_WIDTH - 1:CONV_WIDTH, :]
    for j in range(CONV_WIDTH - 1):
        shift = CONV_WIDTH - 1 - j
        y = y + buf_ref[8 - shift:8 - shift + L, :] * cw_ref[j:j + 1, :]
    y = y * jax.nn.sigmoid(y)
    qc = y[:, :MLSTM_QK_WIDTH].astype(BF16)
    kf = y[:, MLSTM_QK_WIDTH:] * (MLSTM_QK_DIM ** -0.5)

    col = col_ref[...]
    row = row_ref[0]
    lane = lax.broadcasted_iota(jnp.int32, (L, LANES), 1)
    lo = lane < MLSTM_QK_DIM
    rr = lax.broadcasted_iota(jnp.int32, (L, L), 0)
    cc = lax.broadcasted_iota(jnp.int32, (L, L), 1)
    causal = cc <= rr
    ones_col = (lane == 0).astype(BF16)

    for h in range(MLSTM_HEADS):
        pair, half = divmod(h, 2)
        sel = lo if half == 0 else jnp.logical_not(lo)
        q2 = qc[:, pair * LANES:(pair + 1) * LANES]
        k2 = kf[:, pair * LANES:(pair + 1) * LANES]
        qh = jnp.where(sel, q2, jnp.zeros_like(q2))
        kb = k2.astype(BF16)
        bcol = col[:, GATE_F0 + h:GATE_F0 + h + 1]
        licol = col[:, GATE_I0 + h:GATE_I0 + h + 1]
        brow = row[GATE_F0 + h:GATE_F0 + h + 1, :]
        lirow = row[GATE_I0 + h:GATE_I0 + h + 1, :]
        g = bcol[L - 1:L, :]
        m_prev = m_sc[h][0:1, 0:1]

        d = jnp.where(causal, bcol + (lirow - brow), -jnp.inf)
        inter_log = bcol + m_prev
        m_t = jnp.maximum(inter_log, jnp.max(d, axis=-1, keepdims=True))
        w_inter = jnp.exp(inter_log - m_t)
        p = jnp.exp(d - m_t) * _dot_nt(qh, kb)
        vaug = jnp.concatenate([v_ref[:, h * LANES:(h + 1) * LANES], ones_col], axis=1)
        cstate = c_sc[h]
        tot = w_inter * _dot(qh, cstate.astype(BF16)) + _dot(p.astype(BF16), vaug)
        num = tot[:, :LANES]
        den = tot[:, LANES:LANES + 1]
        hout = num / jnp.maximum(jnp.abs(den), jnp.exp(-m_t))

        a = g - bcol + licol
        m_loc = jnp.max(a, axis=0, keepdims=True)
        wa = jnp.exp(a - m_loc)
        kw = jnp.where(sel, k2 * wa, 0.0).astype(BF16)
        kv = _dot_tn(kw, vaug)
        m_new = jnp.maximum(g + m_prev, m_loc)
        c_sc[h] = jnp.exp(g + m_prev - m_new) * cstate + jnp.exp(m_loc - m_new) * kv
        m_sc[h] = jnp.broadcast_to(m_new, m_sc.shape[1:])

        ms = jnp.mean(hout * hout, axis=-1, keepdims=True)
        hn = hout * lax.rsqrt(ms + RMS_EPS) * gain_ref[:, h * LANES:(h + 1) * LANES]
        gate = jax.nn.sigmoid(og_ref[:, h * LANES:(h + 1) * LANES])
        o_ref[:, h * LANES:(h + 1) * LANES] = (hn * gate).astype(o_ref.dtype)


def _mlstm(mqk, mv, mo, gcol, grow, conv_w, gain, batch, seq):
    n = mqk.shape[0]
    L = SEQ_BLOCK
    nc = seq // L
    row = lambda width: pl.BlockSpec((L, width), lambda b, c: (b * nc + c, 0))
    full = lambda a: pl.BlockSpec(a.shape, lambda b, c: (0, 0))
    return pl.pallas_call(
        _mlstm_kernel,
        grid=(batch, nc),
        in_specs=[row(512), row(512), row(512), row(LANES),
                  pl.BlockSpec((1, 16, L), lambda b, c: (b, 0, c)),
                  full(conv_w), full(gain)],
        out_specs=row(512),
        out_shape=jax.ShapeDtypeStruct((n, MLSTM_V_WIDTH), BF16),
        scratch_shapes=[pltpu.VMEM((8, 512), F32), pltpu.VMEM((8 + L, 512), F32),
                        pltpu.VMEM((MLSTM_HEADS, LANES, 2 * LANES), F32),
                        pltpu.VMEM((MLSTM_HEADS, 8, LANES), F32)],
        compiler_params=_params(("parallel", "arbitrary"), 48),
    )(mqk, mv, mo, gcol, grow, conv_w, gain)


def _outproj_kernel(fo_ref, mo_ref, w_ref, x_ref, g_ref, b_ref, o_ref):
    mix = _dot(fo_ref[...], w_ref[0:FOX_WIDTH, :]) + _dot(mo_ref[...], w_ref[FOX_WIDTH:, :])
    o_ref[...] = _layer_norm(DEEPNORM_ALPHA * x_ref[...] + mix, g_ref[...], b_ref[...])


def _outproj(fo, mo, w_out, x2d, g, b):
    n = x2d.shape[0]
    tm = ROW_TILE
    row = lambda width: pl.BlockSpec((tm, width), lambda i: (i, 0))
    full = lambda a: pl.BlockSpec(a.shape, lambda i: (0, 0))
    return pl.pallas_call(
        _outproj_kernel,
        grid=(n // tm,),
        in_specs=[row(512), row(512), full(w_out), row(D_MODEL), full(g), full(b)],
        out_specs=row(D_MODEL),
        out_shape=jax.ShapeDtypeStruct((n, D_MODEL), F32),
        compiler_params=_params(("parallel",), 48),
    )(fo, mo, w_out, x2d, g, b)


def _memkv_kernel(mem_ref, wk_ref, wv_ref, k_ref, v_ref):
    mb = mem_ref[...].astype(BF16)
    k_ref[...] = (_dot(mb, wk_ref[...]) * (XATTN_HEAD_DIM ** -0.5)).astype(BF16)
    v_ref[...] = _dot(mb, wv_ref[...]).astype(BF16)


def _memkv(mem2d, wk, wv, n_mem):
    n = mem2d.shape[0]
    row = pl.BlockSpec((n_mem, D_MODEL), lambda i: (i, 0))
    full = lambda a: pl.BlockSpec(a.shape, lambda i: (0, 0))
    return pl.pallas_call(
        _memkv_kernel,
        grid=(n // n_mem,),
        in_specs=[row, full(wk), full(wv)],
        out_specs=(row, row),
        out_shape=(jax.ShapeDtypeStruct((n, D_MODEL), BF16),) * 2,
        compiler_params=_params(("parallel",), 32),
    )(mem2d, wk, wv)


def _xattn_kernel(x_ref, k_ref, v_ref, wq_ref, wo_ref, g_ref, b_ref, wr_ref, br_ref,
                  o_ref, ob_ref, lg_ref):
    x = x_ref[...]
    q = _dot(x.astype(BF16), wq_ref[...]).astype(BF16)
    outs = []
    for h in range(XATTN_HEADS):
        sl = slice(h * XATTN_HEAD_DIM, (h + 1) * XATTN_HEAD_DIM)
        s = _dot_nt(q[:, sl], k_ref[:, sl])
        p = jnp.exp(s - jnp.max(s, axis=-1, keepdims=True))
        l = jnp.sum(p, axis=-1, keepdims=True)
        outs.append((_dot(p.astype(BF16), v_ref[:, sl]) / l).astype(BF16))
    o = jnp.concatenate(outs, axis=1)
    xa = _dot(o, wo_ref[...])
    x2 = _layer_norm(DEEPNORM_ALPHA * x + xa, g_ref[...], b_ref[...])
    o_ref[...] = x2
    packed = _pack_bf16_pairs(x2)
    for j in range(ob_ref.shape[0]):
        ob_ref[j] = packed[:, j * LANES:(j + 1) * LANES]
    lg_ref[...] = jnp.dot(x2, wr_ref[...], precision=lax.Precision.HIGHEST,
                          preferred_element_type=F32) + br_ref[...]


def _xattn(x1, kmem, vmem, wq, wo, g, b, wr, br, batch, seq, n_mem):
    n = x1.shape[0]
    tm = ROW_TILE
    nt = seq // tm
    row = lambda width: pl.BlockSpec((tm, width), lambda bi, i: (bi * nt + i, 0))
    full = lambda a: pl.BlockSpec(a.shape, lambda bi, i: (0, 0))
    kv = pl.BlockSpec((n_mem, D_MODEL), lambda bi, i: (bi, 0))
    return pl.pallas_call(
        _xattn_kernel,
        grid=(batch, nt),
        in_specs=[row(D_MODEL), kv, kv, full(wq), full(wo), full(g), full(b), full(wr), full(br)],
        out_specs=(row(D_MODEL),
                   pl.BlockSpec((X_PLANES, tm, LANES), lambda bi, i: (0, bi * nt + i, 0)),
                   row(LANES)),
        out_shape=(jax.ShapeDtypeStruct((n, D_MODEL), F32),
                   jax.ShapeDtypeStruct((X_PLANES, n, LANES), jnp.uint32),
                   jax.ShapeDtypeStruct((n, LANES), F32)),
        compiler_params=_params(("parallel", "parallel"), 48),
    )(x1, kmem, vmem, wq, wo, g, b, wr, br)


def _route_kernel(lg_ref, idx_ref, rank_ref, gate_ref, cnt_ref, carry_ref):
    i = pl.program_id(0)

    @pl.when(i == 0)
    def _():
        carry_ref[...] = jnp.zeros_like(carry_ref)

    lg = lg_ref[...]
    t = lg.shape[0]
    lane = lax.broadcasted_iota(jnp.int32, lg.shape, 1)
    lane_f = lane.astype(F32)
    sels, vals, idxs = [], [], []
    for _ in range(TOP_K):
        mx = jnp.max(lg, axis=-1, keepdims=True)
        first = jnp.min(jnp.where(lg == mx, lane_f, float(LANES)), axis=-1, keepdims=True)
        sel = lane_f == first
        sels.append(sel)
        vals.append(mx)
        idxs.append(first)
        lg = jnp.where(sel, -jnp.inf, lg)
    exps = [jnp.exp(v - vals[0]) for v in vals]
    tot = exps[0] + exps[1] + exps[2] + exps[3]

    selmat = (sels[0] | sels[1] | sels[2] | sels[3])
    r = lax.broadcasted_iota(jnp.int32, (t, t), 0)
    s = lax.broadcasted_iota(jnp.int32, (t, t), 1)
    strict = (s < r).astype(BF16)
    selb = selmat.astype(BF16)
    carry = carry_ref[0:1, :]
    rankmat = _dot(strict, selb) + carry
    new_carry = carry + jnp.sum(selmat.astype(F32), axis=0, keepdims=True)
    carry_ref[...] = jnp.broadcast_to(new_carry, carry_ref.shape)
    cnt_ref[...] = jnp.broadcast_to(new_carry, cnt_ref.shape).astype(jnp.int32)

    idx_out = jnp.zeros(lg.shape, jnp.int32)
    rank_out = jnp.zeros(lg.shape, F32)
    gate_out = jnp.zeros(lg.shape, F32)
    for k in range(TOP_K):
        here = lane == k
        rk = jnp.sum(jnp.where(sels[k], rankmat, 0.0), axis=-1, keepdims=True)
        idx_out = jnp.where(here, idxs[k].astype(jnp.int32), idx_out)
        rank_out = jnp.where(here, rk, rank_out)
        gate_out = jnp.where(here, exps[k] / tot, gate_out)
    idx_ref[...] = idx_out
    rank_ref[...] = rank_out.astype(jnp.int32)
    gate_ref[...] = gate_out


def _route(logits):
    n = logits.shape[0]
    t = SEQ_BLOCK
    row = pl.BlockSpec((t, LANES), lambda i: (i, 0))
    return pl.pallas_call(
        _route_kernel,
        grid=(n // t,),
        in_specs=[row],
        out_specs=(row, row, row, pl.BlockSpec((8, LANES), lambda i: (0, 0))),
        out_shape=(jax.ShapeDtypeStruct((n, LANES), jnp.int32),
                   jax.ShapeDtypeStruct((n, LANES), jnp.int32),
                   jax.ShapeDtypeStruct((n, LANES), F32),
                   jax.ShapeDtypeStruct((8, LANES), jnp.int32)),
        scratch_shapes=[pltpu.VMEM((8, LANES), F32)],
        compiler_params=_params(("arbitrary",), 32),
    )(logits)


def _expert_kernel(blk_e_ref, nb_ref, x_ref, wgu_ref, bgu_ref, wd_ref, bd_ref, y_ref,
                   wgu_sc, wd_sc, prev_ref):
    i = pl.program_id(0)
    e = blk_e_ref[i]

    @pl.when(i == 0)
    def _():
        prev_ref[0] = -1

    @pl.when(i < nb_ref[0])
    def _():
        @pl.when(e != prev_ref[0])
        def _():
            wgu_sc[...] = wgu_ref[0].astype(BF16)
            wd_sc[...] = wd_ref[0].astype(BF16)
            prev_ref[0] = e

        packed = jnp.concatenate([x_ref[j] for j in range(X_PLANES)], axis=1)
        h = _dot(_unpack_bf16_pairs(packed), wgu_sc[...]) + bgu_ref[0]
        gate = jnp.minimum(h[:, :D_EXPERT], SWIGLU_LIMIT)
        lin = jnp.clip(h[:, D_EXPERT:], -SWIGLU_LIMIT, SWIGLU_LIMIT)
        hid = gate * jax.nn.sigmoid(SWIGLU_ALPHA * gate) * (lin + 1.0)
        y = _dot(hid.astype(BF16), wd_sc[...]) + bd_ref[0]
        for j in range(Y_PLANES):
            y_ref[j] = y[:, j * LANES:(j + 1) * LANES]

    @pl.when(i >= nb_ref[0])
    def _():
        y_ref[...] = jnp.zeros_like(y_ref)


def _experts(blk_e, nb_used, xs, w_gu, b_gu, w_d, b_d):
    p = xs.shape[1]
    g = EXPERT_ROWS
    grid_spec = pltpu.PrefetchScalarGridSpec(
        num_scalar_prefetch=2,
        grid=(p // g,),
        in_specs=[pl.BlockSpec((X_PLANES, g, LANES), lambda i, be, nb: (0, i, 0)),
                  pl.BlockSpec((1, D_MODEL, 2 * D_EXPERT), lambda i, be, nb: (be[i], 0, 0)),
                  pl.BlockSpec((1, 1, 2 * D_EXPERT), lambda i, be, nb: (be[i], 0, 0)),
                  pl.BlockSpec((1, D_EXPERT, D_MODEL), lambda i, be, nb: (be[i], 0, 0)),
                  pl.BlockSpec((1, 1, D_MODEL), lambda i, be, nb: (be[i], 0, 0))],
        out_specs=pl.BlockSpec((Y_PLANES, g, LANES), lambda i, be, nb: (0, i, 0)),
        scratch_shapes=[pltpu.VMEM((D_MODEL, 2 * D_EXPERT), BF16),
                        pltpu.VMEM((D_EXPERT, D_MODEL), BF16),
                        pltpu.SMEM((1,), jnp.int32)],
    )
    return pl.pallas_call(
        _expert_kernel,
        grid_spec=grid_spec,
        out_shape=jax.ShapeDtypeStruct((Y_PLANES, p, LANES), F32),
        compiler_params=_params(("arbitrary",), 56),
    )(blk_e, nb_used, xs, w_gu, b_gu, w_d, b_d)


def _combine_kernel(y_ref, gate_ref, x_ref, g_ref, b_ref, o_ref):
    gate = gate_ref[...]
    ff = None
    for k in range(TOP_K):
        yk = jnp.concatenate([y_ref[k, j] for j in range(Y_PLANES)], axis=1) * gate[:, k:k + 1]
        ff = yk if ff is None else ff + yk
    o_ref[...] = _layer_norm(DEEPNORM_ALPHA * x_ref[...] + ff, g_ref[...], b_ref[...])


def _combine(yg, gate, x2, g, b):
    n = x2.shape[0]
    tm = SEQ_BLOCK
    row = lambda width: pl.BlockSpec((tm, width), lambda i: (i, 0))
    full = lambda a: pl.BlockSpec(a.shape, lambda i: (0, 0))
    return pl.pallas_call(
        _combine_kernel,
        grid=(n // tm,),
        in_specs=[pl.BlockSpec((TOP_K, Y_PLANES, tm, LANES), lambda i: (0, 0, i, 0)),
                  row(LANES), row(D_MODEL), full(g), full(b)],
        out_specs=row(D_MODEL),
        out_shape=jax.ShapeDtypeStruct((n, D_MODEL), F32),
        compiler_params=_params(("parallel",), 32),
    )(yg, gate, x2, g, b)


SC_WINDOW = 128


def _sc_mesh():
    return plsc.VectorSubcoreMesh(core_axis_name="core", subcore_axis_name="subcore")


def _sc_gather(table, idx):
    m = idx.shape[0]

    @pl.kernel(out_type=jax.ShapeDtypeStruct((m, LANES), table.dtype), mesh=_sc_mesh())
    def gather_kernel(table_hbm, idx_hbm, out_hbm):
        def body(idx_vmem, out_vmem):
            pltpu.sync_copy(table_hbm.at[idx_vmem.at[0]], out_vmem)

        pltpu.emit_pipeline(
            body,
            grid=(m // SC_WINDOW,),
            in_specs=[pl.BlockSpec((1, SC_WINDOW), lambda i: (0, i))],
            out_specs=[pl.BlockSpec((SC_WINDOW, LANES), lambda i: (i, 0))],
            core_axis_name=("core", "subcore"),
            dimension_semantics=(pltpu.PARALLEL,),
        )(idx_hbm, out_hbm)

    return gather_kernel(table, idx.reshape(1, m))


def _sc_scatter(src, idx, out_rows):
    m = idx.shape[0]
    src_blocks = src.shape[0] // SC_WINDOW

    @pl.kernel(out_type=jax.ShapeDtypeStruct((out_rows, LANES), src.dtype), mesh=_sc_mesh())
    def scatter_kernel(src_hbm, idx_hbm, out_hbm):
        def body(src_vmem, idx_vmem):
            pltpu.sync_copy(src_vmem, out_hbm.at[idx_vmem.at[0]])

        pltpu.emit_pipeline(
            body,
            grid=(m // SC_WINDOW,),
            in_specs=[pl.BlockSpec((SC_WINDOW, LANES), lambda i: (i % src_blocks, 0)),
                      pl.BlockSpec((1, SC_WINDOW), lambda i: (0, i))],
            out_specs=[],
            core_axis_name=("core", "subcore"),
            dimension_semantics=(pltpu.PARALLEL,),
        )(src_hbm, idx_hbm)

    return scatter_kernel(src, idx.reshape(1, m))


def _layer(x, mem, w_in, fox_f_bias, conv_w, i_bias, f_bias, fox_g, mlstm_g, w_mix_out,
           ln1_g, ln1_b, w_xq, w_xk, w_xv, w_xo, ln2_g, ln2_b, w_router, b_router,
           w_gate_up, b_gate_up, w_down, b_down, ln3_g, ln3_b):
    batch, seq, d = x.shape
    n_mem = mem.shape[1]
    n = batch * seq
    x2d = x.reshape(n, d)

    o_ff = 3 * FOX_WIDTH
    o_mqk = o_ff + FOX_HEADS
    o_mv = o_mqk + 2 * MLSTM_QK_WIDTH
    o_mi = o_mv + MLSTM_V_WIDTH
    o_mf = o_mi + MLSTM_HEADS
    o_mo = o_mf + MLSTM_HEADS
    n_gate = FOX_HEADS + 2 * MLSTM_HEADS
    w_r = jnp.concatenate(
        [w_in[:, :o_ff], w_in[:, o_mqk:o_mi], w_in[:, o_mo:],
         w_in[:, o_ff:o_mqk], w_in[:, o_mi:o_mo],
         jnp.zeros((d, LANES - n_gate), w_in.dtype)], axis=1).astype(BF16)
    gate_bias = jnp.concatenate(
        [fox_f_bias, i_bias, f_bias, jnp.zeros((LANES - n_gate,), F32)]).reshape(1, LANES)

    fq, fk, fv, mqk, mv, mo, gates = _inproj(x2d, w_r, gate_bias)
    gcol, grow = _gateprep(gates, batch, seq)
    crow4 = grow.reshape(batch, 16, seq // SEQ_BLOCK, SEQ_BLOCK)
    fo = _fox(fq, fk, fv, crow4, fox_g.reshape(1, FOX_WIDTH), batch, seq)
    mo_out = _mlstm(mqk, mv, mo, gcol, grow, conv_w, mlstm_g.reshape(1, MLSTM_V_WIDTH), batch, seq)
    x1 = _outproj(fo, mo_out, w_mix_out.astype(BF16), x2d, ln1_g.reshape(1, d), ln1_b.reshape(1, d))

    kmem, vmem = _memkv(mem.reshape(batch * n_mem, d), w_xk.astype(BF16), w_xv.astype(BF16), n_mem)
    wr_pad = jnp.concatenate([w_router, jnp.zeros((d, LANES - N_EXPERTS), F32)], axis=1)
    br_pad = jnp.concatenate([b_router, jnp.full((LANES - N_EXPERTS,), NEG_BIG, F32)]).reshape(1, LANES)
    x2, x2p, logits = _xattn(x1, kmem, vmem, w_xq.astype(BF16), w_xo.astype(BF16),
                             ln2_g.reshape(1, d), ln2_b.reshape(1, d), wr_pad, br_pad,
                             batch, seq, n_mem)

    idx, rank, gate, cnt = _route(logits)
    counts = cnt[0, :N_EXPERTS]
    g_rows = EXPERT_ROWS
    padded = ((counts + g_rows - 1) // g_rows) * g_rows
    pad_end = jnp.cumsum(padded)
    pad_start = pad_end - padded
    experts = jnp.arange(N_EXPERTS, dtype=jnp.int32)
    sel = idx[:, :TOP_K, None] == experts[None, None, :]
    pos = jnp.sum(jnp.where(sel, pad_start[None, None, :], 0), axis=-1) + rank[:, :TOP_K]
    p_rows = n * TOP_K + N_EXPERTS * g_rows
    nb = p_rows // g_rows
    blk_start = jnp.arange(nb, dtype=jnp.int32) * g_rows
    blk_e = jnp.minimum(jnp.sum((pad_end[None, :] <= blk_start[:, None]).astype(jnp.int32), axis=1),
                        N_EXPERTS - 1)
    nb_used = (pad_end[-1:] // g_rows).astype(jnp.int32)
    pos_t = pos.T

    def piece_index(planes):
        off = jnp.arange(planes, dtype=jnp.int32) * p_rows
        return (pos_t[:, None, :] + off[None, :, None]).reshape(-1)

    xs = _sc_scatter(x2p.reshape(X_PLANES * n, LANES), piece_index(X_PLANES), X_PLANES * p_rows)
    y = _experts(blk_e, nb_used, xs.reshape(X_PLANES, p_rows, LANES), w_gate_up,
                 b_gate_up.reshape(N_EXPERTS, 1, -1), w_down, b_down.reshape(N_EXPERTS, 1, -1))
    yg = _sc_gather(y.reshape(Y_PLANES * p_rows, LANES), piece_index(Y_PLANES))
    out = _combine(yg.reshape(TOP_K, Y_PLANES, n, LANES), gate, x2,
                   ln3_g.reshape(1, d), ln3_b.reshape(1, d))
    return out.reshape(batch, seq, d)


def kernel(x, mem, w_in, fox_f_bias, mlstm_conv_w, mlstm_i_bias, mlstm_f_bias, fox_norm_g, mlstm_norm_g, w_mix_out, ln1_g, ln1_b, w_xq, w_xk, w_xv, w_xo, ln2_g, ln2_b, w_router, b_router, w_gate_up, b_gate_up, w_down, b_down, ln3_g, ln3_b):
    for l in range(w_in.shape[0]):
        x = _layer(x, mem, w_in[l], fox_f_bias[l], mlstm_conv_w[l], mlstm_i_bias[l],
                   mlstm_f_bias[l], fox_norm_g[l], mlstm_norm_g[l], w_mix_out[l],
                   ln1_g[l], ln1_b[l], w_xq[l], w_xk[l], w_xv[l], w_xo[l], ln2_g[l], ln2_b[l],
                   w_router[l], b_router[l], w_gate_up[l], b_gate_up[l], w_down[l], b_down[l],
                   ln3_g[l], ln3_b[l])
    return x
```

```python
import jax
import jax.numpy as jnp
from jax import lax
from jax.experimental import pallas as pl
from jax.experimental.pallas import tpu as pltpu
from jax.experimental.pallas import tpu_sc as plsc

F32 = jnp.float32
BF16 = jnp.bfloat16

D_MODEL = 1024
FOX_HEADS = 8
FOX_HEAD_DIM = 64
FOX_WIDTH = FOX_HEADS * FOX_HEAD_DIM
MLSTM_HEADS = 4
MLSTM_QK_DIM = 64
MLSTM_V_DIM = 128
MLSTM_QK_WIDTH = MLSTM_HEADS * MLSTM_QK_DIM
MLSTM_V_WIDTH = MLSTM_HEADS * MLSTM_V_DIM
CONV_WIDTH = 4
XATTN_HEADS = 4
XATTN_HEAD_DIM = D_MODEL // XATTN_HEADS
N_EXPERTS = 32
TOP_K = 4
D_EXPERT = D_MODEL
SWIGLU_LIMIT = 7.0
SWIGLU_ALPHA = 1.702
DEEPNORM_ALPHA = 2.0 ** 0.25
LN_EPS = 1e-5
RMS_EPS = 1e-6

LANES = 128
SEQ_BLOCK = 256
ROW_TILE = 512
EXPERT_ROWS = 256
X_PLANES = D_MODEL // 2 // LANES
Y_PLANES = D_MODEL // LANES
GATE_I0 = FOX_HEADS
GATE_F0 = FOX_HEADS + MLSTM_HEADS
NEG_BIG = -1e30

MIB = 1024 * 1024


def _params(semantics, vmem_mib):
    return pltpu.CompilerParams(dimension_semantics=semantics,
                                vmem_limit_bytes=vmem_mib * MIB)


def _layer_norm(y, g, b):
    mu = jnp.mean(y, axis=-1, keepdims=True)
    yc = y - mu
    var = jnp.mean(yc * yc, axis=-1, keepdims=True)
    return yc * lax.rsqrt(var + LN_EPS) * g + b


def _dot(a, b):
    return jnp.dot(a, b, preferred_element_type=F32)


def _dot_nt(a, b):
    return lax.dot_general(a, b, (((1,), (1,)), ((), ())), preferred_element_type=F32)


def _dot_tn(a, b):
    return lax.dot_general(a, b, (((0,), (0,)), ((), ())), preferred_element_type=F32)


def _pack_bf16_pairs(x):
    w = x.shape[1] // 2
    lo = pltpu.bitcast(x[:, :w].astype(BF16).astype(F32), jnp.uint32)
    hi = pltpu.bitcast(x[:, w:].astype(BF16).astype(F32), jnp.uint32)
    return (lo >> 16) | hi


def _unpack_bf16_pairs(u):
    lo = pltpu.bitcast(u << 16, F32).astype(BF16)
    hi = pltpu.bitcast(u & jnp.uint32(0xFFFF0000), F32).astype(BF16)
    return jnp.concatenate([lo, hi], axis=1)


def _inproj_kernel(x_ref, w_ref, wt_ref, gb_ref, fqt_ref, fk_ref, fvt_ref, mqk_ref, mv_ref, mo_ref,
                   g_ref):
    xb = x_ref[...].astype(BF16)

    def mm(c0, width):
        return _dot(xb, w_ref[:, c0:c0 + width])

    fqt_ref[...] = (_dot_nt(wt_ref[0:FOX_WIDTH, :], xb) * (FOX_HEAD_DIM ** -0.5)).astype(BF16)
    fvt_ref[...] = _dot_nt(wt_ref[FOX_WIDTH:, :], xb).astype(BF16)
    fk_ref[...] = mm(0, 512).astype(BF16)
    mqk_ref[...] = mm(512, 512)
    mv_ref[...] = mm(1024, 512).astype(BF16)
    mo_ref[...] = mm(1536, 512)
    g_ref[...] = mm(2048, LANES) + gb_ref[...]


def _inproj(x2d, w_r, w_t, gate_bias):
    n = x2d.shape[0]
    tm = ROW_TILE
    row = lambda width: pl.BlockSpec((tm, width), lambda i: (i, 0))
    col = pl.BlockSpec((FOX_WIDTH, tm), lambda i: (0, i))
    full = lambda a: pl.BlockSpec(a.shape, lambda i: (0, 0))
    out_shapes = (
        jax.ShapeDtypeStruct((FOX_WIDTH, n), BF16),
        jax.ShapeDtypeStruct((n, 512), BF16),
        jax.ShapeDtypeStruct((FOX_WIDTH, n), BF16),
        jax.ShapeDtypeStruct((n, 512), F32),
        jax.ShapeDtypeStruct((n, 512), BF16),
        jax.ShapeDtypeStruct((n, 512), F32),
        jax.ShapeDtypeStruct((n, LANES), F32),
    )
    return pl.pallas_call(
        _inproj_kernel,
        grid=(n // tm,),
        in_specs=[row(D_MODEL), full(w_r), full(w_t), full(gate_bias)],
        out_specs=(col, row(512), col, row(512), row(512), row(512), row(LANES)),
        out_shape=out_shapes,
        compiler_params=_params(("parallel",), 48),
    )(x2d, w_r, w_t, gate_bias)


def _gateprep_kernel(g_ref, col_ref, row_ref, cp_ref, carry_ref):
    c = pl.program_id(1)

    @pl.when(c == 0)
    def _():
        carry_ref[...] = jnp.zeros_like(carry_ref)

    g = g_ref[...]
    lane = lax.broadcasted_iota(jnp.int32, g.shape, 1)
    is_i = (lane >= GATE_I0) & (lane < GATE_F0)
    logsig = jnp.minimum(g, 0.0) - jnp.log1p(jnp.exp(-jnp.abs(g)))
    blk = g.shape[0]
    r = lax.broadcasted_iota(jnp.int32, (blk, blk), 0)
    s = lax.broadcasted_iota(jnp.int32, (blk, blk), 1)
    tri = (s <= r).astype(F32)
    cs = jnp.dot(tri, logsig, precision=lax.Precision.HIGHEST, preferred_element_type=F32)
    carry = carry_ref[0:1, :]
    glob = cs + carry
    carry_ref[...] = jnp.broadcast_to(glob[blk - 1:blk, :], carry_ref.shape)
    out = jnp.where(lane < GATE_I0, glob, jnp.where(is_i, g, cs))
    col_ref[...] = out
    row_ref[0] = out.T[0:16, :]

    negc = -glob
    hi = negc.astype(BF16)
    r1 = negc - hi.astype(F32)
    mid = r1.astype(BF16)
    lo = (r1 - mid.astype(F32)).astype(BF16)
    rr = lax.broadcasted_iota(jnp.int32, (LANES, LANES), 0)
    cc = lax.broadcasted_iota(jnp.int32, (LANES, LANES), 1)
    j = (cc >= 3).astype(jnp.int32)
    for p in range(FOX_HEADS // 2):
        acc = None
        for i, piece in enumerate((hi, mid, lo)):
            onehot = ((rr == 2 * p + j) & (cc - 3 * j == i) & (cc < 6)).astype(BF16)
            term = _dot(piece, onehot)
            acc = term if acc is None else acc + term
        cp_ref[0, p] = acc.astype(BF16)


def _gateprep(gates, batch, seq):
    n = gates.shape[0]
    nc = seq // SEQ_BLOCK
    return pl.pallas_call(
        _gateprep_kernel,
        grid=(batch, nc),
        in_specs=[pl.BlockSpec((SEQ_BLOCK, LANES), lambda b, c: (b * nc + c, 0))],
        out_specs=(pl.BlockSpec((SEQ_BLOCK, LANES), lambda b, c: (b * nc + c, 0)),
                   pl.BlockSpec((1, 16, SEQ_BLOCK), lambda b, c: (b, 0, c)),
                   pl.BlockSpec((1, FOX_HEADS // 2, SEQ_BLOCK, LANES), lambda b, c: (b, 0, c, 0))),
        out_shape=(jax.ShapeDtypeStruct((n, LANES), F32),
                   jax.ShapeDtypeStruct((batch, 16, seq), F32),
                   jax.ShapeDtypeStruct((batch, FOX_HEADS // 2, seq, LANES), BF16)),
        scratch_shapes=[pltpu.VMEM((8, LANES), F32)],
        compiler_params=_params(("parallel", "arbitrary"), 32),
    )(gates)


def _fox_kernel(qt_ref, k_ref, cp_ref, vt_ref, gain_ref, o_ref, vaug_sc, qaug_sc, m_sc, acc_sc):
    qi = pl.program_id(2)
    tq = qt_ref.shape[1]
    tk = SEQ_BLOCK
    n_v = 2 * FOX_HEAD_DIM

    @pl.when(qi == 0)
    def _():
        vaug_sc[0:n_v, :] = vt_ref[...]
        vaug_sc[n_v:, :] = jnp.ones((vaug_sc.shape[0] - n_v, vaug_sc.shape[1]), BF16)

    qt = qt_ref[...]
    row = lax.broadcasted_iota(jnp.int32, (LANES, tq), 0)
    zero = jnp.zeros_like(qt)
    for h in range(2):
        head_rows = (row < FOX_HEAD_DIM) if h == 0 else (row >= FOX_HEAD_DIM)
        qaug_sc[h, 0:LANES, :] = jnp.where(head_rows, qt, zero)
        qaug_sc[h, LANES:, :] = ((row >= 3 * h) & (row < 3 * h + 3)).astype(BF16)

    m_sc[...] = jnp.full(m_sc.shape, -jnp.inf, F32)
    acc_sc[...] = jnp.zeros(acc_sc.shape, F32)

    def block(ki, masked):
        k0 = pl.multiple_of(ki * tk, tk)
        lhs = jnp.concatenate([k_ref[pl.ds(k0, tk), :], cp_ref[0, 0, pl.ds(k0, tk), :]], axis=1)
        vblk = vaug_sc[:, pl.ds(k0, tk)]
        for h in range(2):
            st = _dot(lhs, qaug_sc[h])
            if masked:
                kk = lax.broadcasted_iota(jnp.int32, (tk, tq), 0)
                tt = lax.broadcasted_iota(jnp.int32, (tk, tq), 1)
                st = jnp.where(kk <= tt, st, -jnp.inf)
            m_prev = m_sc[h]
            m_new = jnp.maximum(m_prev, jnp.max(st, axis=0, keepdims=True))
            p = jnp.exp(st - m_new)
            alpha = jnp.exp(m_prev - m_new)
            acc_sc[h] = alpha * acc_sc[h] + _dot(vblk, p.astype(BF16))
            m_sc[h] = m_new

    def body(ki, carry):
        block(ki, False)
        return carry

    lax.fori_loop(0, qi, body, 0)
    block(qi, True)

    a0 = acc_sc[0]
    a1 = acc_sc[1]
    ot = jnp.where(row < FOX_HEAD_DIM, a0[0:n_v] / a0[n_v:n_v + 1], a1[0:n_v] / a1[n_v:n_v + 1])
    o = ot.T
    lane = lax.broadcasted_iota(jnp.int32, (tq, LANES), 1)
    lo = lane < FOX_HEAD_DIM
    sq = o * o
    ss0 = jnp.sum(jnp.where(lo, sq, 0.0), axis=-1, keepdims=True)
    ss1 = jnp.sum(jnp.where(lo, 0.0, sq), axis=-1, keepdims=True)
    inv = jnp.where(lo, lax.rsqrt(ss0 / FOX_HEAD_DIM + RMS_EPS),
                    lax.rsqrt(ss1 / FOX_HEAD_DIM + RMS_EPS))
    o_ref[...] = (o * inv * gain_ref[...]).astype(o_ref.dtype)


FOX_VAUG_ROWS = 2 * FOX_HEAD_DIM + 16


def _fox(fqt, fk, cpieces, fvt, fox_gain, batch, seq):
    n = fk.shape[0]
    tq = SEQ_BLOCK
    nq = seq // tq
    npair = FOX_HEADS // 2
    return pl.pallas_call(
        _fox_kernel,
        grid=(batch, npair, nq),
        in_specs=[pl.BlockSpec((LANES, tq), lambda b, hp, qi: (hp, b * nq + qi)),
                  pl.BlockSpec((seq, LANES), lambda b, hp, qi: (b, hp)),
                  pl.BlockSpec((1, 1, seq, LANES), lambda b, hp, qi: (b, hp, 0, 0)),
                  pl.BlockSpec((LANES, seq), lambda b, hp, qi: (hp, b)),
                  pl.BlockSpec((1, LANES), lambda b, hp, qi: (0, hp))],
        out_specs=pl.BlockSpec((tq, LANES), lambda b, hp, qi: (b * nq + qi, hp)),
        out_shape=jax.ShapeDtypeStruct((n, FOX_WIDTH), BF16),
        scratch_shapes=[pltpu.VMEM((FOX_VAUG_ROWS, seq), BF16),
                        pltpu.VMEM((2, 2 * LANES, tq), BF16),
                        pltpu.VMEM((2, 1, tq), F32),
                        pltpu.VMEM((2, FOX_VAUG_ROWS, tq), F32)],
        compiler_params=_params(("parallel", "parallel", "arbitrary"), 32),
    )(fqt, fk, cpieces, fvt, fox_gain)


def _mlstm_kernel(qk_ref, v_ref, og_ref, col_ref, row_ref, cw_ref, gain_ref, o_ref,
                  tail_ref, buf_ref, c_sc, m_sc):
    c = pl.program_id(1)
    L = SEQ_BLOCK

    @pl.when(c == 0)
    def _():
        tail_ref[...] = jnp.zeros_like(tail_ref)
        c_sc[...] = jnp.zeros_like(c_sc)
        m_sc[...] = jnp.zeros_like(m_sc)

    x = qk_ref[...]
    buf_ref[0:8, :] = tail_ref[...]
    buf_ref[8:8 + L, :] = x
    tail_ref[...] = x[L - 8:L, :]
    y = x * cw_ref[CONV_WIDTH - 1:CONV_WIDTH, :]
    for j in range(CONV_WIDTH - 1):
        shift = CONV_WIDTH - 1 - j
        y = y + buf_ref[8 - shift:8 - shift + L, :] * cw_ref[j:j + 1, :]
    y = y * jax.nn.sigmoid(y)
    qc = y[:, :MLSTM_QK_WIDTH].astype(BF16)
    kf = y[:, MLSTM_QK_WIDTH:] * (MLSTM_QK_DIM ** -0.5)

    col = col_ref[...]
    row = row_ref[0]
    lane = lax.broadcasted_iota(jnp.int32, (L, LANES), 1)
    lo = lane < MLSTM_QK_DIM
    rr = lax.broadcasted_iota(jnp.int32, (L, L), 0)
    cc = lax.broadcasted_iota(jnp.int32, (L, L), 1)
    causal = cc <= rr
    ones_col = (lane == 0).astype(BF16)

    for h in range(MLSTM_HEADS):
        pair, half = divmod(h, 2)
        sel = lo if half == 0 else jnp.logical_not(lo)
        q2 = qc[:, pair * LANES:(pair + 1) * LANES]
        k2 = kf[:, pair * LANES:(pair + 1) * LANES]
        qh = jnp.where(sel, q2, jnp.zeros_like(q2))
        kb = k2.astype(BF16)
        bcol = col[:, GATE_F0 + h:GATE_F0 + h + 1]
        licol = col[:, GATE_I0 + h:GATE_I0 + h + 1]
        brow = row[GATE_F0 + h:GATE_F0 + h + 1, :]
        lirow = row[GATE_I0 + h:GATE_I0 + h + 1, :]
        g = bcol[L - 1:L, :]
        m_prev = m_sc[h][0:1, 0:1]

        d = jnp.where(causal, bcol + (lirow - brow), -jnp.inf)
        inter_log = bcol + m_prev
        m_t = jnp.maximum(inter_log, jnp.max(d, axis=-1, keepdims=True))
        w_inter = jnp.exp(inter_log - m_t)
        p = jnp.exp(d - m_t) * _dot_nt(qh, kb)
        vaug = jnp.concatenate([v_ref[:, h * LANES:(h + 1) * LANES], ones_col], axis=1)
        cstate = c_sc[h]
        tot = w_inter * _dot(qh, cstate.astype(BF16)) + _dot(p.astype(BF16), vaug)
        num = tot[:, :LANES]
        den = tot[:, LANES:LANES + 1]
        hout = num / jnp.maximum(jnp.abs(den), jnp.exp(-m_t))

        a = g - bcol + licol
        m_loc = jnp.max(a, axis=0, keepdims=True)
        wa = jnp.exp(a - m_loc)
        kw = jnp.where(sel, k2 * wa, 0.0).astype(BF16)
        kv = _dot_tn(kw, vaug)
        m_new = jnp.maximum(g + m_prev, m_loc)
        c_sc[h] = jnp.exp(g + m_prev - m_new) * cstate + jnp.exp(m_loc - m_new) * kv
        m_sc[h] = jnp.broadcast_to(m_new, m_sc.shape[1:])

        ms = jnp.mean(hout * hout, axis=-1, keepdims=True)
        hn = hout * lax.rsqrt(ms + RMS_EPS) * gain_ref[:, h * LANES:(h + 1) * LANES]
        gate = jax.nn.sigmoid(og_ref[:, h * LANES:(h + 1) * LANES])
        o_ref[:, h * LANES:(h + 1) * LANES] = (hn * gate).astype(o_ref.dtype)


def _mlstm(mqk, mv, mo, gcol, grow, conv_w, gain, batch, seq):
    n = mqk.shape[0]
    L = SEQ_BLOCK
    nc = seq // L
    row = lambda width: pl.BlockSpec((L, width), lambda b, c: (b * nc + c, 0))
    full = lambda a: pl.BlockSpec(a.shape, lambda b, c: (0, 0))
    return pl.pallas_call(
        _mlstm_kernel,
        grid=(batch, nc),
        in_specs=[row(512), row(512), row(512), row(LANES),
                  pl.BlockSpec((1, 16, L), lambda b, c: (b, 0, c)),
                  full(conv_w), full(gain)],
        out_specs=row(512),
        out_shape=jax.ShapeDtypeStruct((n, MLSTM_V_WIDTH), BF16),
        scratch_shapes=[pltpu.VMEM((8, 512), F32), pltpu.VMEM((8 + L, 512), F32),
                        pltpu.VMEM((MLSTM_HEADS, LANES, 2 * LANES), F32),
                        pltpu.VMEM((MLSTM_HEADS, 8, LANES), F32)],
        compiler_params=_params(("parallel", "arbitrary"), 48),
    )(mqk, mv, mo, gcol, grow, conv_w, gain)


def _outproj_kernel(fo_ref, mo_ref, w_ref, x_ref, g_ref, b_ref, o_ref):
    mix = _dot(fo_ref[...], w_ref[0:FOX_WIDTH, :]) + _dot(mo_ref[...], w_ref[FOX_WIDTH:, :])
    o_ref[...] = _layer_norm(DEEPNORM_ALPHA * x_ref[...] + mix, g_ref[...], b_ref[...])


def _outproj(fo, mo, w_out, x2d, g, b):
    n = x2d.shape[0]
    tm = ROW_TILE
    row = lambda width: pl.BlockSpec((tm, width), lambda i: (i, 0))
    full = lambda a: pl.BlockSpec(a.shape, lambda i: (0, 0))
    return pl.pallas_call(
        _outproj_kernel,
        grid=(n // tm,),
        in_specs=[row(512), row(512), full(w_out), row(D_MODEL), full(g), full(b)],
        out_specs=row(D_MODEL),
        out_shape=jax.ShapeDtypeStruct((n, D_MODEL), F32),
        compiler_params=_params(("parallel",), 48),
    )(fo, mo, w_out, x2d, g, b)


def _memkv_kernel(mem_ref, wk_ref, wv_ref, k_ref, v_ref):
    mb = mem_ref[...].astype(BF16)
    k_ref[...] = (_dot(mb, wk_ref[...]) * (XATTN_HEAD_DIM ** -0.5)).astype(BF16)
    v_ref[...] = _dot(mb, wv_ref[...]).astype(BF16)


def _memkv(mem2d, wk, wv, n_mem):
    n = mem2d.shape[0]
    row = pl.BlockSpec((n_mem, D_MODEL), lambda i: (i, 0))
    full = lambda a: pl.BlockSpec(a.shape, lambda i: (0, 0))
    return pl.pallas_call(
        _memkv_kernel,
        grid=(n // n_mem,),
        in_specs=[row, full(wk), full(wv)],
        out_specs=(row, row),
        out_shape=(jax.ShapeDtypeStruct((n, D_MODEL), BF16),) * 2,
        compiler_params=_params(("parallel",), 32),
    )(mem2d, wk, wv)


def _xattn_kernel(x_ref, k_ref, v_ref, wq_ref, wo_ref, g_ref, b_ref, wr_ref, br_ref,
                  o_ref, ob_ref, lg_ref):
    x = x_ref[...]
    q = _dot(x.astype(BF16), wq_ref[...]).astype(BF16)
    outs = []
    for h in range(XATTN_HEADS):
        sl = slice(h * XATTN_HEAD_DIM, (h + 1) * XATTN_HEAD_DIM)
        s = _dot_nt(q[:, sl], k_ref[:, sl])
        p = jnp.exp(s - jnp.max(s, axis=-1, keepdims=True))
        l = jnp.sum(p, axis=-1, keepdims=True)
        outs.append((_dot(p.astype(BF16), v_ref[:, sl]) / l).astype(BF16))
    o = jnp.concatenate(outs, axis=1)
    xa = _dot(o, wo_ref[...])
    x2 = _layer_norm(DEEPNORM_ALPHA * x + xa, g_ref[...], b_ref[...])
    o_ref[...] = x2
    packed = _pack_bf16_pairs(x2)
    for j in range(ob_ref.shape[0]):
        ob_ref[j] = packed[:, j * LANES:(j + 1) * LANES]
    lg_ref[...] = jnp.dot(x2, wr_ref[...], precision=lax.Precision.HIGHEST,
                          preferred_element_type=F32) + br_ref[...]


def _xattn(x1, kmem, vmem, wq, wo, g, b, wr, br, batch, seq, n_mem):
    n = x1.shape[0]
    tm = ROW_TILE
    nt = seq // tm
    row = lambda width: pl.BlockSpec((tm, width), lambda bi, i: (bi * nt + i, 0))
    full = lambda a: pl.BlockSpec(a.shape, lambda bi, i: (0, 0))
    kv = pl.BlockSpec((n_mem, D_MODEL), lambda bi, i: (bi, 0))
    return pl.pallas_call(
        _xattn_kernel,
        grid=(batch, nt),
        in_specs=[row(D_MODEL), kv, kv, full(wq), full(wo), full(g), full(b), full(wr), full(br)],
        out_specs=(row(D_MODEL),
                   pl.BlockSpec((X_PLANES, tm, LANES), lambda bi, i: (0, bi * nt + i, 0)),
                   row(LANES)),
        out_shape=(jax.ShapeDtypeStruct((n, D_MODEL), F32),
                   jax.ShapeDtypeStruct((X_PLANES, n, LANES), jnp.uint32),
                   jax.ShapeDtypeStruct((n, LANES), F32)),
        compiler_params=_params(("parallel", "parallel"), 48),
    )(x1, kmem, vmem, wq, wo, g, b, wr, br)


def _route_kernel(lg_ref, idx_ref, rank_ref, gate_ref, cnt_ref, carry_ref):
    i = pl.program_id(0)

    @pl.when(i == 0)
    def _():
        carry_ref[...] = jnp.zeros_like(carry_ref)

    lg = lg_ref[...]
    t = lg.shape[0]
    lane = lax.broadcasted_iota(jnp.int32, lg.shape, 1)
    lane_f = lane.astype(F32)
    sels, vals, idxs = [], [], []
    for _ in range(TOP_K):
        mx = jnp.max(lg, axis=-1, keepdims=True)
        first = jnp.min(jnp.where(lg == mx, lane_f, float(LANES)), axis=-1, keepdims=True)
        sel = lane_f == first
        sels.append(sel)
        vals.append(mx)
        idxs.append(first)
        lg = jnp.where(sel, -jnp.inf, lg)
    exps = [jnp.exp(v - vals[0]) for v in vals]
    tot = exps[0] + exps[1] + exps[2] + exps[3]

    selmat = (sels[0] | sels[1] | sels[2] | sels[3])
    r = lax.broadcasted_iota(jnp.int32, (t, t), 0)
    s = lax.broadcasted_iota(jnp.int32, (t, t), 1)
    strict = (s < r).astype(BF16)
    selb = selmat.astype(BF16)
    carry = carry_ref[0:1, :]
    rankmat = _dot(strict, selb) + carry
    new_carry = carry + jnp.sum(selmat.astype(F32), axis=0, keepdims=True)
    carry_ref[...] = jnp.broadcast_to(new_carry, carry_ref.shape)
    cnt_ref[...] = jnp.broadcast_to(new_carry, cnt_ref.shape).astype(jnp.int32)

    idx_out = jnp.zeros(lg.shape, jnp.int32)
    rank_out = jnp.zeros(lg.shape, F32)
    gate_out = jnp.zeros(lg.shape, F32)
    for k in range(TOP_K):
        here = lane == k
        rk = jnp.sum(jnp.where(sels[k], rankmat, 0.0), axis=-1, keepdims=True)
        idx_out = jnp.where(here, idxs[k].astype(jnp.int32), idx_out)
        rank_out = jnp.where(here, rk, rank_out)
        gate_out = jnp.where(here, exps[k] / tot, gate_out)
    idx_ref[...] = idx_out
    rank_ref[...] = rank_out.astype(jnp.int32)
    gate_ref[...] = gate_out


def _route(logits):
    n = logits.shape[0]
    t = SEQ_BLOCK
    row = pl.BlockSpec((t, LANES), lambda i: (i, 0))
    return pl.pallas_call(
        _route_kernel,
        grid=(n // t,),
        in_specs=[row],
        out_specs=(row, row, row, pl.BlockSpec((8, LANES), lambda i: (0, 0))),
        out_shape=(jax.ShapeDtypeStruct((n, LANES), jnp.int32),
                   jax.ShapeDtypeStruct((n, LANES), jnp.int32),
                   jax.ShapeDtypeStruct((n, LANES), F32),
                   jax.ShapeDtypeStruct((8, LANES), jnp.int32)),
        scratch_shapes=[pltpu.VMEM((8, LANES), F32)],
        compiler_params=_params(("arbitrary",), 32),
    )(logits)


def _expert_kernel(blk_e_ref, nb_ref, x_ref, wgu_ref, bgu_ref, wd_ref, bd_ref, y_ref,
                   wgu_sc, wd_sc, prev_ref):
    i = pl.program_id(0)
    e = blk_e_ref[i]

    @pl.when(i == 0)
    def _():
        prev_ref[0] = -1

    @pl.when(i < nb_ref[0])
    def _():
        @pl.when(e != prev_ref[0])
        def _():
            wgu_sc[...] = wgu_ref[0].astype(BF16)
            wd_sc[...] = wd_ref[0].astype(BF16)
            prev_ref[0] = e

        packed = jnp.concatenate([x_ref[j] for j in range(X_PLANES)], axis=1)
        h = _dot(_unpack_bf16_pairs(packed), wgu_sc[...]) + bgu_ref[0]
        gate = jnp.minimum(h[:, :D_EXPERT], SWIGLU_LIMIT)
        lin = jnp.clip(h[:, D_EXPERT:], -SWIGLU_LIMIT, SWIGLU_LIMIT)
        hid = gate * jax.nn.sigmoid(SWIGLU_ALPHA * gate) * (lin + 1.0)
        y = _dot(hid.astype(BF16), wd_sc[...]) + bd_ref[0]
        for j in range(Y_PLANES):
            y_ref[j] = y[:, j * LANES:(j + 1) * LANES]

    @pl.when(i >= nb_ref[0])
    def _():
        y_ref[...] = jnp.zeros_like(y_ref)


def _experts(blk_e, nb_used, xs, w_gu, b_gu, w_d, b_d):
    p = xs.shape[1]
    g = EXPERT_ROWS
    grid_spec = pltpu.PrefetchScalarGridSpec(
        num_scalar_prefetch=2,
        grid=(p // g,),
        in_specs=[pl.BlockSpec((X_PLANES, g, LANES), lambda i, be, nb: (0, i, 0)),
                  pl.BlockSpec((1, D_MODEL, 2 * D_EXPERT), lambda i, be, nb: (be[i], 0, 0)),
                  pl.BlockSpec((1, 1, 2 * D_EXPERT), lambda i, be, nb: (be[i], 0, 0)),
                  pl.BlockSpec((1, D_EXPERT, D_MODEL), lambda i, be, nb: (be[i], 0, 0)),
                  pl.BlockSpec((1, 1, D_MODEL), lambda i, be, nb: (be[i], 0, 0))],
        out_specs=pl.BlockSpec((Y_PLANES, g, LANES), lambda i, be, nb: (0, i, 0)),
        scratch_shapes=[pltpu.VMEM((D_MODEL, 2 * D_EXPERT), BF16),
                        pltpu.VMEM((D_EXPERT, D_MODEL), BF16),
                        pltpu.SMEM((1,), jnp.int32)],
    )
    return pl.pallas_call(
        _expert_kernel,
        grid_spec=grid_spec,
        out_shape=jax.ShapeDtypeStruct((Y_PLANES, p, LANES), F32),
        compiler_params=_params(("arbitrary",), 56),
    )(blk_e, nb_used, xs, w_gu, b_gu, w_d, b_d)


def _combine_kernel(y_ref, gate_ref, x_ref, g_ref, b_ref, o_ref):
    gate = gate_ref[...]
    ff = None
    for k in range(TOP_K):
        yk = jnp.concatenate([y_ref[k, j] for j in range(Y_PLANES)], axis=1) * gate[:, k:k + 1]
        ff = yk if ff is None else ff + yk
    o_ref[...] = _layer_norm(DEEPNORM_ALPHA * x_ref[...] + ff, g_ref[...], b_ref[...])


def _combine(yg, gate, x2, g, b):
    n = x2.shape[0]
    tm = SEQ_BLOCK
    row = lambda width: pl.BlockSpec((tm, width), lambda i: (i, 0))
    full = lambda a: pl.BlockSpec(a.shape, lambda i: (0, 0))
    return pl.pallas_call(
        _combine_kernel,
        grid=(n // tm,),
        in_specs=[pl.BlockSpec((TOP_K, Y_PLANES, tm, LANES), lambda i: (0, 0, i, 0)),
                  row(LANES), row(D_MODEL), full(g), full(b)],
        out_specs=row(D_MODEL),
        out_shape=jax.ShapeDtypeStruct((n, D_MODEL), F32),
        compiler_params=_params(("parallel",), 32),
    )(yg, gate, x2, g, b)


SC_WINDOW = 128


def _sc_mesh():
    return plsc.VectorSubcoreMesh(core_axis_name="core", subcore_axis_name="subcore")


def _sc_gather(table, idx):
    m = idx.shape[0]

    @pl.kernel(out_type=jax.ShapeDtypeStruct((m, LANES), table.dtype), mesh=_sc_mesh())
    def gather_kernel(table_hbm, idx_hbm, out_hbm):
        def body(idx_vmem, out_vmem):
            pltpu.sync_copy(table_hbm.at[idx_vmem.at[0]], out_vmem)

        pltpu.emit_pipeline(
            body,
            grid=(m // SC_WINDOW,),
            in_specs=[pl.BlockSpec((1, SC_WINDOW), lambda i: (0, i))],
            out_specs=[pl.BlockSpec((SC_WINDOW, LANES), lambda i: (i, 0))],
            core_axis_name=("core", "subcore"),
            dimension_semantics=(pltpu.PARALLEL,),
        )(idx_hbm, out_hbm)

    return gather_kernel(table, idx.reshape(1, m))


def _sc_scatter(src, idx, out_rows):
    m = idx.shape[0]
    src_blocks = src.shape[0] // SC_WINDOW

    @pl.kernel(out_type=jax.ShapeDtypeStruct((out_rows, LANES), src.dtype), mesh=_sc_mesh())
    def scatter_kernel(src_hbm, idx_hbm, out_hbm):
        def body(src_vmem, idx_vmem):
            pltpu.sync_copy(src_vmem, out_hbm.at[idx_vmem.at[0]])

        pltpu.emit_pipeline(
            body,
            grid=(m // SC_WINDOW,),
            in_specs=[pl.BlockSpec((SC_WINDOW, LANES), lambda i: (i % src_blocks, 0)),
                      pl.BlockSpec((1, SC_WINDOW), lambda i: (0, i))],
            out_specs=[],
            core_axis_name=("core", "subcore"),
            dimension_semantics=(pltpu.PARALLEL,),
        )(src_hbm, idx_hbm)

    return scatter_kernel(src, idx.reshape(1, m))


def _layer(x, mem, w_in, fox_f_bias, conv_w, i_bias, f_bias, fox_g, mlstm_g, w_mix_out,
           ln1_g, ln1_b, w_xq, w_xk, w_xv, w_xo, ln2_g, ln2_b, w_router, b_router,
           w_gate_up, b_gate_up, w_down, b_down, ln3_g, ln3_b):
    batch, seq, d = x.shape
    n_mem = mem.shape[1]
    n = batch * seq
    x2d = x.reshape(n, d)

    o_ff = 3 * FOX_WIDTH
    o_mqk = o_ff + FOX_HEADS
    o_mv = o_mqk + 2 * MLSTM_QK_WIDTH
    o_mi = o_mv + MLSTM_V_WIDTH
    o_mf = o_mi + MLSTM_HEADS
    o_mo = o_mf + MLSTM_HEADS
    n_gate = FOX_HEADS + 2 * MLSTM_HEADS
    w_r = jnp.concatenate(
        [w_in[:, FOX_WIDTH:2 * FOX_WIDTH], w_in[:, o_mqk:o_mi], w_in[:, o_mo:],
         w_in[:, o_ff:o_mqk], w_in[:, o_mi:o_mo],
         jnp.zeros((d, LANES - n_gate), w_in.dtype)], axis=1).astype(BF16)
    w_t = jnp.concatenate([w_in[:, :FOX_WIDTH], w_in[:, 2 * FOX_WIDTH:o_ff]], axis=1).T.astype(BF16)
    gate_bias = jnp.concatenate(
        [fox_f_bias, i_bias, f_bias, jnp.zeros((LANES - n_gate,), F32)]).reshape(1, LANES)

    fqt, fk, fvt, mqk, mv, mo, gates = _inproj(x2d, w_r, w_t, gate_bias)
    gcol, grow, cpieces = _gateprep(gates, batch, seq)
    fo = _fox(fqt, fk, cpieces, fvt, fox_g.reshape(1, FOX_WIDTH), batch, seq)
    mo_out = _mlstm(mqk, mv, mo, gcol, grow, conv_w, mlstm_g.reshape(1, MLSTM_V_WIDTH), batch, seq)
    x1 = _outproj(fo, mo_out, w_mix_out.astype(BF16), x2d, ln1_g.reshape(1, d), ln1_b.reshape(1, d))

    kmem, vmem = _memkv(mem.reshape(batch * n_mem, d), w_xk.astype(BF16), w_xv.astype(BF16), n_mem)
    wr_pad = jnp.concatenate([w_router, jnp.zeros((d, LANES - N_EXPERTS), F32)], axis=1)
    br_pad = jnp.concatenate([b_router, jnp.full((LANES - N_EXPERTS,), NEG_BIG, F32)]).reshape(1, LANES)
    x2, x2p, logits = _xattn(x1, kmem, vmem, w_xq.astype(BF16), w_xo.astype(BF16),
                             ln2_g.reshape(1, d), ln2_b.reshape(1, d), wr_pad, br_pad,
                             batch, seq, n_mem)

    idx, rank, gate, cnt = _route(logits)
    counts = cnt[0, :N_EXPERTS]
    g_rows = EXPERT_ROWS
    padded = ((counts + g_rows - 1) // g_rows) * g_rows
    pad_end = jnp.cumsum(padded)
    pad_start = pad_end - padded
    experts = jnp.arange(N_EXPERTS, dtype=jnp.int32)
    sel = idx[:, :TOP_K, None] == experts[None, None, :]
    pos = jnp.sum(jnp.where(sel, pad_start[None, None, :], 0), axis=-1) + rank[:, :TOP_K]
    p_rows = n * TOP_K + N_EXPERTS * g_rows
    nb = p_rows // g_rows
    blk_start = jnp.arange(nb, dtype=jnp.int32) * g_rows
    blk_e = jnp.minimum(jnp.sum((pad_end[None, :] <= blk_start[:, None]).astype(jnp.int32), axis=1),
                        N_EXPERTS - 1)
    nb_used = (pad_end[-1:] // g_rows).astype(jnp.int32)
    pos_t = pos.T

    def piece_index(planes):
        off = jnp.arange(planes, dtype=jnp.int32) * p_rows
        return (pos_t[:, None, :] + off[None, :, None]).reshape(-1)

    xs = _sc_scatter(x2p.reshape(X_PLANES * n, LANES), piece_index(X_PLANES), X_PLANES * p_rows)
    y = _experts(blk_e, nb_used, xs.reshape(X_PLANES, p_rows, LANES), w_gate_up,
                 b_gate_up.reshape(N_EXPERTS, 1, -1), w_down, b_down.reshape(N_EXPERTS, 1, -1))
    yg = _sc_gather(y.reshape(Y_PLANES * p_rows, LANES), piece_index(Y_PLANES))
    out = _combine(yg.reshape(TOP_K, Y_PLANES, n, LANES), gate, x2,
                   ln3_g.reshape(1, d), ln3_b.reshape(1, d))
    return out.reshape(batch, seq, d)


def kernel(x, mem, w_in, fox_f_bias, mlstm_conv_w, mlstm_i_bias, mlstm_f_bias, fox_norm_g, mlstm_norm_g, w_mix_out, ln1_g, ln1_b, w_xq, w_xk, w_xv, w_xo, ln2_g, ln2_b, w_router, b_router, w_gate_up, b_gate_up, w_down, b_down, ln3_g, ln3_b):
    for l in range(w_in.shape[0]):
        x = _layer(x, mem, w_in[l], fox_f_bias[l], mlstm_conv_w[l], mlstm_i_bias[l],
                   mlstm_f_bias[l], fox_norm_g[l], mlstm_norm_g[l], w_mix_out[l],
                   ln1_g[l], ln1_b[l], w_xq[l], w_xk[l], w_xv[l], w_xo[l], ln2_g[l], ln2_b[l],
                   w_router[l], b_router[l], w_gate_up[l], b_gate_up[l], w_down[l], b_down[l],
                   ln3_g[l], ln3_b[l])
    return x
```

```python
import jax
import jax.numpy as jnp
from jax import lax
from jax.experimental import pallas as pl
from jax.experimental.pallas import tpu as pltpu
from jax.experimental.pallas import tpu_sc as plsc

F32 = jnp.float32
BF16 = jnp.bfloat16

D_MODEL = 1024
FOX_HEADS = 8
FOX_HEAD_DIM = 64
FOX_WIDTH = FOX_HEADS * FOX_HEAD_DIM
MLSTM_HEADS = 4
MLSTM_QK_DIM = 64
MLSTM_V_DIM = 128
MLSTM_QK_WIDTH = MLSTM_HEADS * MLSTM_QK_DIM
MLSTM_V_WIDTH = MLSTM_HEADS * MLSTM_V_DIM
CONV_WIDTH = 4
XATTN_HEADS = 4
XATTN_HEAD_DIM = D_MODEL // XATTN_HEADS
N_EXPERTS = 32
TOP_K = 4
D_EXPERT = D_MODEL
SWIGLU_LIMIT = 7.0
SWIGLU_ALPHA = 1.702
DEEPNORM_ALPHA = 2.0 ** 0.25
LN_EPS = 1e-5
RMS_EPS = 1e-6

LANES = 128
SEQ_BLOCK = 256
ROW_TILE = 512
EXPERT_ROWS = 256
X_PLANES = D_MODEL // 2 // LANES
Y_PLANES = D_MODEL // LANES
GATE_I0 = FOX_HEADS
GATE_F0 = FOX_HEADS + MLSTM_HEADS
NEG_BIG = -1e30

MIB = 1024 * 1024


def _params(semantics, vmem_mib):
    return pltpu.CompilerParams(dimension_semantics=semantics,
                                vmem_limit_bytes=vmem_mib * MIB)


def _layer_norm(y, g, b):
    mu = jnp.mean(y, axis=-1, keepdims=True)
    yc = y - mu
    var = jnp.mean(yc * yc, axis=-1, keepdims=True)
    return yc * lax.rsqrt(var + LN_EPS) * g + b


def _dot(a, b):
    return jnp.dot(a, b, preferred_element_type=F32)


def _dot_nt(a, b):
    return lax.dot_general(a, b, (((1,), (1,)), ((), ())), preferred_element_type=F32)


def _dot_tn(a, b):
    return lax.dot_general(a, b, (((0,), (0,)), ((), ())), preferred_element_type=F32)


def _pack_bf16_pairs(x):
    w = x.shape[1] // 2
    lo = pltpu.bitcast(x[:, :w].astype(BF16).astype(F32), jnp.uint32)
    hi = pltpu.bitcast(x[:, w:].astype(BF16).astype(F32), jnp.uint32)
    return (lo >> 16) | hi


def _unpack_bf16_pairs(u):
    lo = pltpu.bitcast(u << 16, F32).astype(BF16)
    hi = pltpu.bitcast(u & jnp.uint32(0xFFFF0000), F32).astype(BF16)
    return jnp.concatenate([lo, hi], axis=1)


def _inproj_kernel(x_ref, w_ref, wt_ref, gb_ref, fqt_ref, fk_ref, fvt_ref, mqk_ref, mv_ref, mo_ref,
                   g_ref):
    xb = x_ref[...].astype(BF16)

    def mm(c0, width):
        return _dot(xb, w_ref[:, c0:c0 + width])

    fqt_ref[...] = (_dot_nt(wt_ref[0:FOX_WIDTH, :], xb) * (FOX_HEAD_DIM ** -0.5)).astype(BF16)
    fvt_ref[...] = _dot_nt(wt_ref[FOX_WIDTH:, :], xb).astype(BF16)
    fk_ref[...] = mm(0, 512).astype(BF16)
    mqk_ref[...] = mm(512, 512)
    mv_ref[...] = mm(1024, 512).astype(BF16)
    mo_ref[...] = mm(1536, 512)
    g_ref[...] = mm(2048, LANES) + gb_ref[...]


def _inproj(x2d, w_r, w_t, gate_bias):
    n = x2d.shape[0]
    tm = ROW_TILE
    row = lambda width: pl.BlockSpec((tm, width), lambda i: (i, 0))
    col = pl.BlockSpec((FOX_WIDTH, tm), lambda i: (0, i))
    full = lambda a: pl.BlockSpec(a.shape, lambda i: (0, 0))
    out_shapes = (
        jax.ShapeDtypeStruct((FOX_WIDTH, n), BF16),
        jax.ShapeDtypeStruct((n, 512), BF16),
        jax.ShapeDtypeStruct((FOX_WIDTH, n), BF16),
        jax.ShapeDtypeStruct((n, 512), F32),
        jax.ShapeDtypeStruct((n, 512), BF16),
        jax.ShapeDtypeStruct((n, 512), F32),
        jax.ShapeDtypeStruct((n, LANES), F32),
    )
    return pl.pallas_call(
        _inproj_kernel,
        grid=(n // tm,),
        in_specs=[row(D_MODEL), full(w_r), full(w_t), full(gate_bias)],
        out_specs=(col, row(512), col, row(512), row(512), row(512), row(LANES)),
        out_shape=out_shapes,
        compiler_params=_params(("parallel",), 48),
    )(x2d, w_r, w_t, gate_bias)


def _gateprep_kernel(g_ref, col_ref, row_ref, cp_ref, carry_ref):
    c = pl.program_id(1)

    @pl.when(c == 0)
    def _():
        carry_ref[...] = jnp.zeros_like(carry_ref)

    g = g_ref[...]
    lane = lax.broadcasted_iota(jnp.int32, g.shape, 1)
    is_i = (lane >= GATE_I0) & (lane < GATE_F0)
    logsig = jnp.minimum(g, 0.0) - jnp.log1p(jnp.exp(-jnp.abs(g)))
    blk = g.shape[0]
    r = lax.broadcasted_iota(jnp.int32, (blk, blk), 0)
    s = lax.broadcasted_iota(jnp.int32, (blk, blk), 1)
    tri = (s <= r).astype(F32)
    cs = jnp.dot(tri, logsig, precision=lax.Precision.HIGHEST, preferred_element_type=F32)
    carry = carry_ref[0:1, :]
    glob = cs + carry
    carry_ref[...] = jnp.broadcast_to(glob[blk - 1:blk, :], carry_ref.shape)
    out = jnp.where(lane < GATE_I0, glob, jnp.where(is_i, g, cs))
    col_ref[...] = out
    row_ref[0] = out.T[0:16, :]

    negc = -glob
    hi = negc.astype(BF16)
    r1 = negc - hi.astype(F32)
    mid = r1.astype(BF16)
    lo = (r1 - mid.astype(F32)).astype(BF16)
    rr = lax.broadcasted_iota(jnp.int32, (LANES, LANES), 0)
    cc = lax.broadcasted_iota(jnp.int32, (LANES, LANES), 1)
    j = (cc >= 3).astype(jnp.int32)
    for p in range(FOX_HEADS // 2):
        acc = None
        for i, piece in enumerate((hi, mid, lo)):
            onehot = ((rr == 2 * p + j) & (cc - 3 * j == i) & (cc < 6)).astype(BF16)
            term = _dot(piece, onehot)
            acc = term if acc is None else acc + term
        cp_ref[0, p] = acc.astype(BF16)


def _gateprep(gates, batch, seq):
    n = gates.shape[0]
    nc = seq // SEQ_BLOCK
    return pl.pallas_call(
        _gateprep_kernel,
        grid=(batch, nc),
        in_specs=[pl.BlockSpec((SEQ_BLOCK, LANES), lambda b, c: (b * nc + c, 0))],
        out_specs=(pl.BlockSpec((SEQ_BLOCK, LANES), lambda b, c: (b * nc + c, 0)),
                   pl.BlockSpec((1, 16, SEQ_BLOCK), lambda b, c: (b, 0, c)),
                   pl.BlockSpec((1, FOX_HEADS // 2, SEQ_BLOCK, LANES), lambda b, c: (b, 0, c, 0))),
        out_shape=(jax.ShapeDtypeStruct((n, LANES), F32),
                   jax.ShapeDtypeStruct((batch, 16, seq), F32),
                   jax.ShapeDtypeStruct((batch, FOX_HEADS // 2, seq, LANES), BF16)),
        scratch_shapes=[pltpu.VMEM((8, LANES), F32)],
        compiler_params=_params(("parallel", "arbitrary"), 32),
    )(gates)


FOX_ONES_ROWS = 16
FOX_VAUG_ROWS = FOX_HEAD_DIM + FOX_ONES_ROWS
FOX_QUERY_TILE = 2 * SEQ_BLOCK
FOX_KEY_UNROLL = 2


def _fox_kernel(qt_ref, k_ref, cp_ref, vt_ref, gain_ref, o_ref,
                kaug_sc, vaug_sc, qaug_sc, st_sc, acc0_sc, acc1_sc):
    qi = pl.program_id(2)
    tq = qt_ref.shape[1]
    tg = tq
    tk = SEQ_BLOCK
    hd = FOX_HEAD_DIM
    seq = k_ref.shape[0]

    @pl.when(qi == 0)
    def _():
        lane = lax.broadcasted_iota(jnp.int32, (tk, LANES), 1)

        def build(blk, carry):
            r0 = pl.multiple_of(blk * tk, tk)
            kp = k_ref[pl.ds(r0, tk), :].astype(F32)
            cp = cp_ref[0, 0, pl.ds(r0, tk), :].astype(F32)
            for h in range(2):
                kh = kp if h == 0 else pltpu.roll(kp, hd, axis=1)
                ch = pltpu.roll(cp, hd - 3 * h, axis=1)
                kaug = jnp.where(lane < hd, kh, jnp.where(lane < hd + 3, ch, 0.0))
                kaug_sc[h, pl.ds(r0, tk), :] = kaug.astype(BF16)
            return carry

        lax.fori_loop(0, seq // tk, build, 0)
        for h in range(2):
            vaug_sc[h, 0:hd, :] = vt_ref[h * hd:(h + 1) * hd, :]
            vaug_sc[h, hd:, :] = jnp.ones((FOX_ONES_ROWS, seq), BF16)

    ones3 = (lax.broadcasted_iota(jnp.int32, (FOX_ONES_ROWS, tq), 0) < 3).astype(BF16)
    for h in range(2):
        qaug_sc[h, 0:hd, :] = qt_ref[h * hd:(h + 1) * hd, :]
        qaug_sc[h, hd:hd + FOX_ONES_ROWS, :] = ones3
        qaug_sc[h, hd + FOX_ONES_ROWS:, :] = jnp.zeros((LANES - hd - FOX_ONES_ROWS, tq), BF16)

    def score_step(kg, m, masked=False):
        k0 = pl.multiple_of(kg * tg, tg)
        out = []
        for h in range(2):
            st = _dot(kaug_sc[h, pl.ds(k0, tg), :], qaug_sc[h])
            if masked:
                kk = lax.broadcasted_iota(jnp.int32, (tg, tq), 0)
                tt = lax.broadcasted_iota(jnp.int32, (tg, tq), 1)
                st = jnp.where(kk <= tt, st, -jnp.inf)
            st_sc[h, pl.ds(k0, tg), :] = st
            out.append(jnp.maximum(m[h], jnp.max(st, axis=0, keepdims=True)))
        return tuple(out)

    m = (jnp.full((1, tq), -jnp.inf, F32),) * 2
    m = lax.fori_loop(0, qi, score_step, m)
    m = score_step(qi, m, masked=True)

    acc0_sc[...] = jnp.zeros(acc0_sc.shape, F32)
    acc1_sc[...] = jnp.zeros(acc1_sc.shape, F32)
    acc = (acc0_sc, acc1_sc)

    def sum_step(kg, carry):
        k0 = pl.multiple_of(kg * tg, tg)
        for h in range(2):
            p = jnp.exp(st_sc[h, pl.ds(k0, tg), :] - m[h])
            acc[h][...] += _dot(vaug_sc[h, :, pl.ds(k0, tg)], p.astype(BF16))
        return carry

    lax.fori_loop(0, qi + 1, sum_step, 0)

    a0 = acc0_sc[...]
    a1 = acc1_sc[...]
    ot = jnp.concatenate([a0[0:hd] / a0[hd:hd + 1], a1[0:hd] / a1[hd:hd + 1]], axis=0)
    o = ot.T
    lane = lax.broadcasted_iota(jnp.int32, (tq, LANES), 1)
    lo = lane < FOX_HEAD_DIM
    sq = o * o
    ss0 = jnp.sum(jnp.where(lo, sq, 0.0), axis=-1, keepdims=True)
    ss1 = jnp.sum(jnp.where(lo, 0.0, sq), axis=-1, keepdims=True)
    inv = jnp.where(lo, lax.rsqrt(ss0 / FOX_HEAD_DIM + RMS_EPS),
                    lax.rsqrt(ss1 / FOX_HEAD_DIM + RMS_EPS))
    o_ref[...] = (o * inv * gain_ref[...]).astype(o_ref.dtype)


def _fox(fqt, fk, cpieces, fvt, fox_gain, batch, seq):
    n = fk.shape[0]
    tq = FOX_QUERY_TILE
    nq = seq // tq
    npair = FOX_HEADS // 2
    return pl.pallas_call(
        _fox_kernel,
        grid=(batch, npair, nq),
        in_specs=[pl.BlockSpec((LANES, tq), lambda b, hp, qi: (hp, b * nq + qi)),
                  pl.BlockSpec((seq, LANES), lambda b, hp, qi: (b, hp)),
                  pl.BlockSpec((1, 1, seq, LANES), lambda b, hp, qi: (b, hp, 0, 0)),
                  pl.BlockSpec((LANES, seq), lambda b, hp, qi: (hp, b)),
                  pl.BlockSpec((1, LANES), lambda b, hp, qi: (0, hp))],
        out_specs=pl.BlockSpec((tq, LANES), lambda b, hp, qi: (b * nq + qi, hp)),
        out_shape=jax.ShapeDtypeStruct((n, FOX_WIDTH), BF16),
        scratch_shapes=[pltpu.VMEM((2, seq, LANES), BF16),
                        pltpu.VMEM((2, FOX_VAUG_ROWS, seq), BF16),
                        pltpu.VMEM((2, LANES, tq), BF16),
                        pltpu.VMEM((2, seq, tq), F32),
                        pltpu.VMEM((FOX_VAUG_ROWS, tq), F32),
                        pltpu.VMEM((FOX_VAUG_ROWS, tq), F32)],
        compiler_params=_params(("parallel", "parallel", "arbitrary"), 48),
    )(fqt, fk, cpieces, fvt, fox_gain)


def _mlstm_kernel(qk_ref, v_ref, og_ref, col_ref, row_ref, cw_ref, gain_ref, o_ref,
                  tail_ref, buf_ref, c_sc, m_sc):
    c = pl.program_id(1)
    L = SEQ_BLOCK

    @pl.when(c == 0)
    def _():
        tail_ref[...] = jnp.zeros_like(tail_ref)
        c_sc[...] = jnp.zeros_like(c_sc)
        m_sc[...] = jnp.zeros_like(m_sc)

    x = qk_ref[...]
    buf_ref[0:8, :] = tail_ref[...]
    buf_ref[8:8 + L, :] = x
    tail_ref[...] = x[L - 8:L, :]
    y = x * cw_ref[CONV_WIDTH - 1:CONV_WIDTH, :]
    for j in range(CONV_WIDTH - 1):
        shift = CONV_WIDTH - 1 - j
        y = y + buf_ref[8 - shift:8 - shift + L, :] * cw_ref[j:j + 1, :]
    y = y * jax.nn.sigmoid(y)
    qc = y[:, :MLSTM_QK_WIDTH].astype(BF16)
    kf = y[:, MLSTM_QK_WIDTH:] * (MLSTM_QK_DIM ** -0.5)

    col = col_ref[...]
    row = row_ref[0]
    lane = lax.broadcasted_iota(jnp.int32, (L, LANES), 1)
    lo = lane < MLSTM_QK_DIM
    rr = lax.broadcasted_iota(jnp.int32, (L, L), 0)
    cc = lax.broadcasted_iota(jnp.int32, (L, L), 1)
    causal = cc <= rr
    ones_col = (lane == 0).astype(BF16)

    for h in range(MLSTM_HEADS):
        pair, half = divmod(h, 2)
        sel = lo if half == 0 else jnp.logical_not(lo)
        q2 = qc[:, pair * LANES:(pair + 1) * LANES]
        k2 = kf[:, pair * LANES:(pair + 1) * LANES]
        qh = jnp.where(sel, q2, jnp.zeros_like(q2))
        kb = k2.astype(BF16)
        bcol = col[:, GATE_F0 + h:GATE_F0 + h + 1]
        licol = col[:, GATE_I0 + h:GATE_I0 + h + 1]
        brow = row[GATE_F0 + h:GATE_F0 + h + 1, :]
        lirow = row[GATE_I0 + h:GATE_I0 + h + 1, :]
        g = bcol[L - 1:L, :]
        m_prev = m_sc[h][0:1, 0:1]

        d = jnp.where(causal, bcol + (lirow - brow), -jnp.inf)
        inter_log = bcol + m_prev
        m_t = jnp.maximum(inter_log, jnp.max(d, axis=-1, keepdims=True))
        w_inter = jnp.exp(inter_log - m_t)
        p = jnp.exp(d - m_t) * _dot_nt(qh, kb)
        vaug = jnp.concatenate([v_ref[:, h * LANES:(h + 1) * LANES], ones_col], axis=1)
        cstate = c_sc[h]
        tot = w_inter * _dot(qh, cstate.astype(BF16)) + _dot(p.astype(BF16), vaug)
        num = tot[:, :LANES]
        den = tot[:, LANES:LANES + 1]
        hout = num / jnp.maximum(jnp.abs(den), jnp.exp(-m_t))

        a = g - bcol + licol
        m_loc = jnp.max(a, axis=0, keepdims=True)
        wa = jnp.exp(a - m_loc)
        kw = jnp.where(sel, k2 * wa, 0.0).astype(BF16)
        kv = _dot_tn(kw, vaug)
        m_new = jnp.maximum(g + m_prev, m_loc)
        c_sc[h] = jnp.exp(g + m_prev - m_new) * cstate + jnp.exp(m_loc - m_new) * kv
        m_sc[h] = jnp.broadcast_to(m_new, m_sc.shape[1:])

        ms = jnp.mean(hout * hout, axis=-1, keepdims=True)
        hn = hout * lax.rsqrt(ms + RMS_EPS) * gain_ref[:, h * LANES:(h + 1) * LANES]
        gate = jax.nn.sigmoid(og_ref[:, h * LANES:(h + 1) * LANES])
        o_ref[:, h * LANES:(h + 1) * LANES] = (hn * gate).astype(o_ref.dtype)


def _mlstm(mqk, mv, mo, gcol, grow, conv_w, gain, batch, seq):
    n = mqk.shape[0]
    L = SEQ_BLOCK
    nc = seq // L
    row = lambda width: pl.BlockSpec((L, width), lambda b, c: (b * nc + c, 0))
    full = lambda a: pl.BlockSpec(a.shape, lambda b, c: (0, 0))
    return pl.pallas_call(
        _mlstm_kernel,
        grid=(batch, nc),
        in_specs=[row(512), row(512), row(512), row(LANES),
                  pl.BlockSpec((1, 16, L), lambda b, c: (b, 0, c)),
                  full(conv_w), full(gain)],
        out_specs=row(512),
        out_shape=jax.ShapeDtypeStruct((n, MLSTM_V_WIDTH), BF16),
        scratch_shapes=[pltpu.VMEM((8, 512), F32), pltpu.VMEM((8 + L, 512), F32),
                        pltpu.VMEM((MLSTM_HEADS, LANES, 2 * LANES), F32),
                        pltpu.VMEM((MLSTM_HEADS, 8, LANES), F32)],
        compiler_params=_params(("parallel", "arbitrary"), 48),
    )(mqk, mv, mo, gcol, grow, conv_w, gain)


def _outproj_kernel(fo_ref, mo_ref, w_ref, x_ref, g_ref, b_ref, o_ref):
    mix = _dot(fo_ref[...], w_ref[0:FOX_WIDTH, :]) + _dot(mo_ref[...], w_ref[FOX_WIDTH:, :])
    o_ref[...] = _layer_norm(DEEPNORM_ALPHA * x_ref[...] + mix, g_ref[...], b_ref[...])


def _outproj(fo, mo, w_out, x2d, g, b):
    n = x2d.shape[0]
    tm = ROW_TILE
    row = lambda width: pl.BlockSpec((tm, width), lambda i: (i, 0))
    full = lambda a: pl.BlockSpec(a.shape, lambda i: (0, 0))
    return pl.pallas_call(
        _outproj_kernel,
        grid=(n // tm,),
        in_specs=[row(512), row(512), full(w_out), row(D_MODEL), full(g), full(b)],
        out_specs=row(D_MODEL),
        out_shape=jax.ShapeDtypeStruct((n, D_MODEL), F32),
        compiler_params=_params(("parallel",), 48),
    )(fo, mo, w_out, x2d, g, b)


def _memkv_kernel(mem_ref, wk_ref, wv_ref, k_ref, v_ref):
    mb = mem_ref[...].astype(BF16)
    k_ref[...] = (_dot(mb, wk_ref[...]) * (XATTN_HEAD_DIM ** -0.5)).astype(BF16)
    v_ref[...] = _dot(mb, wv_ref[...]).astype(BF16)


def _memkv(mem2d, wk, wv, n_mem):
    n = mem2d.shape[0]
    row = pl.BlockSpec((n_mem, D_MODEL), lambda i: (i, 0))
    full = lambda a: pl.BlockSpec(a.shape, lambda i: (0, 0))
    return pl.pallas_call(
        _memkv_kernel,
        grid=(n // n_mem,),
        in_specs=[row, full(wk), full(wv)],
        out_specs=(row, row),
        out_shape=(jax.ShapeDtypeStruct((n, D_MODEL), BF16),) * 2,
        compiler_params=_params(("parallel",), 32),
    )(mem2d, wk, wv)


def _xattn_kernel(x_ref, k_ref, v_ref, wq_ref, wo_ref, g_ref, b_ref, wr_ref, br_ref,
                  o_ref, ob_ref, lg_ref):
    x = x_ref[...]
    q = _dot(x.astype(BF16), wq_ref[...]).astype(BF16)
    outs = []
    for h in range(XATTN_HEADS):
        sl = slice(h * XATTN_HEAD_DIM, (h + 1) * XATTN_HEAD_DIM)
        s = _dot_nt(q[:, sl], k_ref[:, sl])
        p = jnp.exp(s - jnp.max(s, axis=-1, keepdims=True))
        l = jnp.sum(p, axis=-1, keepdims=True)
        outs.append((_dot(p.astype(BF16), v_ref[:, sl]) / l).astype(BF16))
    o = jnp.concatenate(outs, axis=1)
    xa = _dot(o, wo_ref[...])
    x2 = _layer_norm(DEEPNORM_ALPHA * x + xa, g_ref[...], b_ref[...])
    o_ref[...] = x2
    packed = _pack_bf16_pairs(x2)
    for j in range(ob_ref.shape[0]):
        ob_ref[j] = packed[:, j * LANES:(j + 1) * LANES]
    lg_ref[...] = jnp.dot(x2, wr_ref[...], precision=lax.Precision.HIGHEST,
                          preferred_element_type=F32) + br_ref[...]


def _xattn(x1, kmem, vmem, wq, wo, g, b, wr, br, batch, seq, n_mem):
    n = x1.shape[0]
    tm = ROW_TILE
    nt = seq // tm
    row = lambda width: pl.BlockSpec((tm, width), lambda bi, i: (bi * nt + i, 0))
    full = lambda a: pl.BlockSpec(a.shape, lambda bi, i: (0, 0))
    kv = pl.BlockSpec((n_mem, D_MODEL), lambda bi, i: (bi, 0))
    return pl.pallas_call(
        _xattn_kernel,
        grid=(batch, nt),
        in_specs=[row(D_MODEL), kv, kv, full(wq), full(wo), full(g), full(b), full(wr), full(br)],
        out_specs=(row(D_MODEL),
                   pl.BlockSpec((X_PLANES, tm, LANES), lambda bi, i: (0, bi * nt + i, 0)),
                   row(LANES)),
        out_shape=(jax.ShapeDtypeStruct((n, D_MODEL), F32),
                   jax.ShapeDtypeStruct((X_PLANES, n, LANES), jnp.uint32),
                   jax.ShapeDtypeStruct((n, LANES), F32)),
        compiler_params=_params(("parallel", "parallel"), 48),
    )(x1, kmem, vmem, wq, wo, g, b, wr, br)


def _route_kernel(lg_ref, idx_ref, rank_ref, gate_ref, cnt_ref, carry_ref):
    i = pl.program_id(0)

    @pl.when(i == 0)
    def _():
        carry_ref[...] = jnp.zeros_like(carry_ref)

    lg = lg_ref[...]
    t = lg.shape[0]
    lane = lax.broadcasted_iota(jnp.int32, lg.shape, 1)
    lane_f = lane.astype(F32)
    sels, vals, idxs = [], [], []
    for _ in range(TOP_K):
        mx = jnp.max(lg, axis=-1, keepdims=True)
        first = jnp.min(jnp.where(lg == mx, lane_f, float(LANES)), axis=-1, keepdims=True)
        sel = lane_f == first
        sels.append(sel)
        vals.append(mx)
        idxs.append(first)
        lg = jnp.where(sel, -jnp.inf, lg)
    exps = [jnp.exp(v - vals[0]) for v in vals]
    tot = exps[0] + exps[1] + exps[2] + exps[3]

    selmat = (sels[0] | sels[1] | sels[2] | sels[3])
    r = lax.broadcasted_iota(jnp.int32, (t, t), 0)
    s = lax.broadcasted_iota(jnp.int32, (t, t), 1)
    strict = (s < r).astype(BF16)
    selb = selmat.astype(BF16)
    carry = carry_ref[0:1, :]
    rankmat = _dot(strict, selb) + carry
    new_carry = carry + jnp.sum(selmat.astype(F32), axis=0, keepdims=True)
    carry_ref[...] = jnp.broadcast_to(new_carry, carry_ref.shape)
    cnt_ref[...] = jnp.broadcast_to(new_carry, cnt_ref.shape).astype(jnp.int32)

    idx_out = jnp.zeros(lg.shape, jnp.int32)
    rank_out = jnp.zeros(lg.shape, F32)
    gate_out = jnp.zeros(lg.shape, F32)
    for k in range(TOP_K):
        here = lane == k
        rk = jnp.sum(jnp.where(sels[k], rankmat, 0.0), axis=-1, keepdims=True)
        idx_out = jnp.where(here, idxs[k].astype(jnp.int32), idx_out)
        rank_out = jnp.where(here, rk, rank_out)
        gate_out = jnp.where(here, exps[k] / tot, gate_out)
    idx_ref[...] = idx_out
    rank_ref[...] = rank_out.astype(jnp.int32)
    gate_ref[...] = gate_out


def _route(logits):
    n = logits.shape[0]
    t = SEQ_BLOCK
    row = pl.BlockSpec((t, LANES), lambda i: (i, 0))
    return pl.pallas_call(
        _route_kernel,
        grid=(n // t,),
        in_specs=[row],
        out_specs=(row, row, row, pl.BlockSpec((8, LANES), lambda i: (0, 0))),
        out_shape=(jax.ShapeDtypeStruct((n, LANES), jnp.int32),
                   jax.ShapeDtypeStruct((n, LANES), jnp.int32),
                   jax.ShapeDtypeStruct((n, LANES), F32),
                   jax.ShapeDtypeStruct((8, LANES), jnp.int32)),
        scratch_shapes=[pltpu.VMEM((8, LANES), F32)],
        compiler_params=_params(("arbitrary",), 32),
    )(logits)


def _expert_kernel(blk_e_ref, nb_ref, x_ref, wgu_ref, bgu_ref, wd_ref, bd_ref, y_ref,
                   wgu_sc, wd_sc, prev_ref):
    i = pl.program_id(0)
    e = blk_e_ref[i]

    @pl.when(i == 0)
    def _():
        prev_ref[0] = -1

    @pl.when(i < nb_ref[0])
    def _():
        @pl.when(e != prev_ref[0])
        def _():
            wgu_sc[...] = wgu_ref[0].astype(BF16)
            wd_sc[...] = wd_ref[0].astype(BF16)
            prev_ref[0] = e

        packed = jnp.concatenate([x_ref[j] for j in range(X_PLANES)], axis=1)
        h = _dot(_unpack_bf16_pairs(packed), wgu_sc[...]) + bgu_ref[0]
        gate = jnp.minimum(h[:, :D_EXPERT], SWIGLU_LIMIT)
        lin = jnp.clip(h[:, D_EXPERT:], -SWIGLU_LIMIT, SWIGLU_LIMIT)
        hid = gate * jax.nn.sigmoid(SWIGLU_ALPHA * gate) * (lin + 1.0)
        y = _dot(hid.astype(BF16), wd_sc[...]) + bd_ref[0]
        for j in range(Y_PLANES):
            y_ref[j] = y[:, j * LANES:(j + 1) * LANES]

    @pl.when(i >= nb_ref[0])
    def _():
        y_ref[...] = jnp.zeros_like(y_ref)


def _experts(blk_e, nb_used, xs, w_gu, b_gu, w_d, b_d):
    p = xs.shape[1]
    g = EXPERT_ROWS
    grid_spec = pltpu.PrefetchScalarGridSpec(
        num_scalar_prefetch=2,
        grid=(p // g,),
        in_specs=[pl.BlockSpec((X_PLANES, g, LANES), lambda i, be, nb: (0, i, 0)),
                  pl.BlockSpec((1, D_MODEL, 2 * D_EXPERT), lambda i, be, nb: (be[i], 0, 0)),
                  pl.BlockSpec((1, 1, 2 * D_EXPERT), lambda i, be, nb: (be[i], 0, 0)),
                  pl.BlockSpec((1, D_EXPERT, D_MODEL), lambda i, be, nb: (be[i], 0, 0)),
                  pl.BlockSpec((1, 1, D_MODEL), lambda i, be, nb: (be[i], 0, 0))],
        out_specs=pl.BlockSpec((Y_PLANES, g, LANES), lambda i, be, nb: (0, i, 0)),
        scratch_shapes=[pltpu.VMEM((D_MODEL, 2 * D_EXPERT), BF16),
                        pltpu.VMEM((D_EXPERT, D_MODEL), BF16),
                        pltpu.SMEM((1,), jnp.int32)],
    )
    return pl.pallas_call(
        _expert_kernel,
        grid_spec=grid_spec,
        out_shape=jax.ShapeDtypeStruct((Y_PLANES, p, LANES), F32),
        compiler_params=_params(("arbitrary",), 56),
    )(blk_e, nb_used, xs, w_gu, b_gu, w_d, b_d)


def _combine_kernel(y_ref, gate_ref, x_ref, g_ref, b_ref, o_ref):
    gate = gate_ref[...]
    ff = None
    for k in range(TOP_K):
        yk = jnp.concatenate([y_ref[k, j] for j in range(Y_PLANES)], axis=1) * gate[:, k:k + 1]
        ff = yk if ff is None else ff + yk
    o_ref[...] = _layer_norm(DEEPNORM_ALPHA * x_ref[...] + ff, g_ref[...], b_ref[...])


def _combine(yg, gate, x2, g, b):
    n = x2.shape[0]
    tm = SEQ_BLOCK
    row = lambda width: pl.BlockSpec((tm, width), lambda i: (i, 0))
    full = lambda a: pl.BlockSpec(a.shape, lambda i: (0, 0))
    return pl.pallas_call(
        _combine_kernel,
        grid=(n // tm,),
        in_specs=[pl.BlockSpec((TOP_K, Y_PLANES, tm, LANES), lambda i: (0, 0, i, 0)),
                  row(LANES), row(D_MODEL), full(g), full(b)],
        out_specs=row(D_MODEL),
        out_shape=jax.ShapeDtypeStruct((n, D_MODEL), F32),
        compiler_params=_params(("parallel",), 32),
    )(yg, gate, x2, g, b)


SC_WINDOW = 128


def _sc_mesh():
    return plsc.VectorSubcoreMesh(core_axis_name="core", subcore_axis_name="subcore")


def _sc_gather(table, idx):
    m = idx.shape[0]

    @pl.kernel(out_type=jax.ShapeDtypeStruct((m, LANES), table.dtype), mesh=_sc_mesh())
    def gather_kernel(table_hbm, idx_hbm, out_hbm):
        def body(idx_vmem, out_vmem):
            pltpu.sync_copy(table_hbm.at[idx_vmem.at[0]], out_vmem)

        pltpu.emit_pipeline(
            body,
            grid=(m // SC_WINDOW,),
            in_specs=[pl.BlockSpec((1, SC_WINDOW), lambda i: (0, i))],
            out_specs=[pl.BlockSpec((SC_WINDOW, LANES), lambda i: (i, 0))],
            core_axis_name=("core", "subcore"),
            dimension_semantics=(pltpu.PARALLEL,),
        )(idx_hbm, out_hbm)

    return gather_kernel(table, idx.reshape(1, m))


def _sc_scatter(src, idx, out_rows):
    m = idx.shape[0]
    src_blocks = src.shape[0] // SC_WINDOW

    @pl.kernel(out_type=jax.ShapeDtypeStruct((out_rows, LANES), src.dtype), mesh=_sc_mesh())
    def scatter_kernel(src_hbm, idx_hbm, out_hbm):
        def body(src_vmem, idx_vmem):
            pltpu.sync_copy(src_vmem, out_hbm.at[idx_vmem.at[0]])

        pltpu.emit_pipeline(
            body,
            grid=(m // SC_WINDOW,),
            in_specs=[pl.BlockSpec((SC_WINDOW, LANES), lambda i: (i % src_blocks, 0)),
                      pl.BlockSpec((1, SC_WINDOW), lambda i: (0, i))],
            out_specs=[],
            core_axis_name=("core", "subcore"),
            dimension_semantics=(pltpu.PARALLEL,),
        )(src_hbm, idx_hbm)

    return scatter_kernel(src, idx.reshape(1, m))


def _layer(x, mem, w_in, fox_f_bias, conv_w, i_bias, f_bias, fox_g, mlstm_g, w_mix_out,
           ln1_g, ln1_b, w_xq, w_xk, w_xv, w_xo, ln2_g, ln2_b, w_router, b_router,
           w_gate_up, b_gate_up, w_down, b_down, ln3_g, ln3_b):
    batch, seq, d = x.shape
    n_mem = mem.shape[1]
    n = batch * seq
    x2d = x.reshape(n, d)

    o_ff = 3 * FOX_WIDTH
    o_mqk = o_ff + FOX_HEADS
    o_mv = o_mqk + 2 * MLSTM_QK_WIDTH
    o_mi = o_mv + MLSTM_V_WIDTH
    o_mf = o_mi + MLSTM_HEADS
    o_mo = o_mf + MLSTM_HEADS
    n_gate = FOX_HEADS + 2 * MLSTM_HEADS
    w_r = jnp.concatenate(
        [w_in[:, FOX_WIDTH:2 * FOX_WIDTH], w_in[:, o_mqk:o_mi], w_in[:, o_mo:],
         w_in[:, o_ff:o_mqk], w_in[:, o_mi:o_mo],
         jnp.zeros((d, LANES - n_gate), w_in.dtype)], axis=1).astype(BF16)
    w_t = jnp.concatenate([w_in[:, :FOX_WIDTH], w_in[:, 2 * FOX_WIDTH:o_ff]], axis=1).T.astype(BF16)
    gate_bias = jnp.concatenate(
        [fox_f_bias, i_bias, f_bias, jnp.zeros((LANES - n_gate,), F32)]).reshape(1, LANES)

    fqt, fk, fvt, mqk, mv, mo, gates = _inproj(x2d, w_r, w_t, gate_bias)
    gcol, grow, cpieces = _gateprep(gates, batch, seq)
    fo = _fox(fqt, fk, cpieces, fvt, fox_g.reshape(1, FOX_WIDTH), batch, seq)
    mo_out = _mlstm(mqk, mv, mo, gcol, grow, conv_w, mlstm_g.reshape(1, MLSTM_V_WIDTH), batch, seq)
    x1 = _outproj(fo, mo_out, w_mix_out.astype(BF16), x2d, ln1_g.reshape(1, d), ln1_b.reshape(1, d))

    kmem, vmem = _memkv(mem.reshape(batch * n_mem, d), w_xk.astype(BF16), w_xv.astype(BF16), n_mem)
    wr_pad = jnp.concatenate([w_router, jnp.zeros((d, LANES - N_EXPERTS), F32)], axis=1)
    br_pad = jnp.concatenate([b_router, jnp.full((LANES - N_EXPERTS,), NEG_BIG, F32)]).reshape(1, LANES)
    x2, x2p, logits = _xattn(x1, kmem, vmem, w_xq.astype(BF16), w_xo.astype(BF16),
                             ln2_g.reshape(1, d), ln2_b.reshape(1, d), wr_pad, br_pad,
                             batch, seq, n_mem)

    idx, rank, gate, cnt = _route(logits)
    counts = cnt[0, :N_EXPERTS]
    g_rows = EXPERT_ROWS
    padded = ((counts + g_rows - 1) // g_rows) * g_rows
    pad_end = jnp.cumsum(padded)
    pad_start = pad_end - padded
    experts = jnp.arange(N_EXPERTS, dtype=jnp.int32)
    sel = idx[:, :TOP_K, None] == experts[None, None, :]
    pos = jnp.sum(jnp.where(sel, pad_start[None, None, :], 0), axis=-1) + rank[:, :TOP_K]
    p_rows = n * TOP_K + N_EXPERTS * g_rows
    nb = p_rows // g_rows
    blk_start = jnp.arange(nb, dtype=jnp.int32) * g_rows
    blk_e = jnp.minimum(jnp.sum((pad_end[None, :] <= blk_start[:, None]).astype(jnp.int32), axis=1),
                        N_EXPERTS - 1)
    nb_used = (pad_end[-1:] // g_rows).astype(jnp.int32)
    pos_t = pos.T

    def piece_index(planes):
        off = jnp.arange(planes, dtype=jnp.int32) * p_rows
        return (pos_t[:, None, :] + off[None, :, None]).reshape(-1)

    xs = _sc_scatter(x2p.reshape(X_PLANES * n, LANES), piece_index(X_PLANES), X_PLANES * p_rows)
    y = _experts(blk_e, nb_used, xs.reshape(X_PLANES, p_rows, LANES), w_gate_up,
                 b_gate_up.reshape(N_EXPERTS, 1, -1), w_down, b_down.reshape(N_EXPERTS, 1, -1))
    yg = _sc_gather(y.reshape(Y_PLANES * p_rows, LANES), piece_index(Y_PLANES))
    out = _combine(yg.reshape(TOP_K, Y_PLANES, n, LANES), gate, x2,
                   ln3_g.reshape(1, d), ln3_b.reshape(1, d))
    return out.reshape(batch, seq, d)


def kernel(x, mem, w_in, fox_f_bias, mlstm_conv_w, mlstm_i_bias, mlstm_f_bias, fox_norm_g, mlstm_norm_g, w_mix_out, ln1_g, ln1_b, w_xq, w_xk, w_xv, w_xo, ln2_g, ln2_b, w_router, b_router, w_gate_up, b_gate_up, w_down, b_down, ln3_g, ln3_b):
    for l in range(w_in.shape[0]):
        x = _layer(x, mem, w_in[l], fox_f_bias[l], mlstm_conv_w[l], mlstm_i_bias[l],
                   mlstm_f_bias[l], fox_norm_g[l], mlstm_norm_g[l], w_mix_out[l],
                   ln1_g[l], ln1_b[l], w_xq[l], w_xk[l], w_xv[l], w_xo[l], ln2_g[l], ln2_b[l],
                   w_router[l], b_router[l], w_gate_up[l], b_gate_up[l], w_down[l], b_down[l],
                   ln3_g[l], ln3_b[l])
    return x
```

```python
import jax
import jax.numpy as jnp
from jax import lax
from jax.experimental import pallas as pl
from jax.experimental.pallas import tpu as pltpu
from jax.experimental.pallas import tpu_sc as plsc

F32 = jnp.float32
BF16 = jnp.bfloat16

D_MODEL = 1024
FOX_HEADS = 8
FOX_HEAD_DIM = 64
FOX_WIDTH = FOX_HEADS * FOX_HEAD_DIM
MLSTM_HEADS = 4
MLSTM_QK_DIM = 64
MLSTM_V_DIM = 128
MLSTM_QK_WIDTH = MLSTM_HEADS * MLSTM_QK_DIM
MLSTM_V_WIDTH = MLSTM_HEADS * MLSTM_V_DIM
CONV_WIDTH = 4
XATTN_HEADS = 4
XATTN_HEAD_DIM = D_MODEL // XATTN_HEADS
N_EXPERTS = 32
TOP_K = 4
D_EXPERT = D_MODEL
SWIGLU_LIMIT = 7.0
SWIGLU_ALPHA = 1.702
DEEPNORM_ALPHA = 2.0 ** 0.25
LN_EPS = 1e-5
RMS_EPS = 1e-6

LANES = 128
SEQ_BLOCK = 256
ROW_TILE = 512
EXPERT_ROWS = 512
EXPERT_CHUNK = 256
ROUTE_TILE = 512
X_PLANES = D_MODEL // 2 // LANES
Y_PLANES = D_MODEL // LANES
GATE_I0 = FOX_HEADS
GATE_F0 = FOX_HEADS + MLSTM_HEADS
NEG_BIG = -1e30

MIB = 1024 * 1024


def _params(semantics, vmem_mib):
    return pltpu.CompilerParams(dimension_semantics=semantics,
                                vmem_limit_bytes=vmem_mib * MIB)


def _layer_norm(y, g, b):
    mu = jnp.mean(y, axis=-1, keepdims=True)
    yc = y - mu
    var = jnp.mean(yc * yc, axis=-1, keepdims=True)
    return yc * lax.rsqrt(var + LN_EPS) * g + b


def _dot(a, b):
    return jnp.dot(a, b, preferred_element_type=F32)


def _dot_nt(a, b):
    return lax.dot_general(a, b, (((1,), (1,)), ((), ())), preferred_element_type=F32)


def _dot_tn(a, b):
    return lax.dot_general(a, b, (((0,), (0,)), ((), ())), preferred_element_type=F32)


def _pack_bf16_pairs(x):
    w = x.shape[1] // 2
    lo = pltpu.bitcast(x[:, :w].astype(BF16).astype(F32), jnp.uint32)
    hi = pltpu.bitcast(x[:, w:].astype(BF16).astype(F32), jnp.uint32)
    return (lo >> 16) | hi


def _unpack_bf16_pairs(u):
    lo = pltpu.bitcast(u << 16, F32).astype(BF16)
    hi = pltpu.bitcast(u & jnp.uint32(0xFFFF0000), F32).astype(BF16)
    return jnp.concatenate([lo, hi], axis=1)


def _inproj_kernel(x_ref, w_ref, wt_ref, gb_ref, fqt_ref, fk_ref, fvt_ref, mqk_ref, mv_ref, mo_ref,
                   g_ref):
    xb = x_ref[...].astype(BF16)

    def mm(c0, width):
        return _dot(xb, w_ref[:, c0:c0 + width])

    fqt_ref[...] = (_dot_nt(wt_ref[0:FOX_WIDTH, :], xb) * (FOX_HEAD_DIM ** -0.5)).astype(BF16)
    fvt_ref[...] = _dot_nt(wt_ref[FOX_WIDTH:, :], xb).astype(BF16)
    fk_ref[...] = mm(0, 512).astype(BF16)
    mqk_ref[...] = mm(512, 512)
    mv_ref[...] = mm(1024, 512).astype(BF16)
    mo_ref[...] = mm(1536, 512)
    g_ref[...] = mm(2048, LANES) + gb_ref[...]


def _inproj(x2d, w_r, w_t, gate_bias):
    n = x2d.shape[0]
    tm = ROW_TILE
    row = lambda width: pl.BlockSpec((tm, width), lambda i: (i, 0))
    col = pl.BlockSpec((FOX_WIDTH, tm), lambda i: (0, i))
    full = lambda a: pl.BlockSpec(a.shape, lambda i: (0, 0))
    out_shapes = (
        jax.ShapeDtypeStruct((FOX_WIDTH, n), BF16),
        jax.ShapeDtypeStruct((n, 512), BF16),
        jax.ShapeDtypeStruct((FOX_WIDTH, n), BF16),
        jax.ShapeDtypeStruct((n, 512), F32),
        jax.ShapeDtypeStruct((n, 512), BF16),
        jax.ShapeDtypeStruct((n, 512), F32),
        jax.ShapeDtypeStruct((n, LANES), F32),
    )
    return pl.pallas_call(
        _inproj_kernel,
        grid=(n // tm,),
        in_specs=[row(D_MODEL), full(w_r), full(w_t), full(gate_bias)],
        out_specs=(col, row(512), col, row(512), row(512), row(512), row(LANES)),
        out_shape=out_shapes,
        compiler_params=_params(("parallel",), 48),
    )(x2d, w_r, w_t, gate_bias)


def _gateprep_kernel(g_ref, col_ref, row_ref, cp_ref, carry_ref):
    c = pl.program_id(1)

    @pl.when(c == 0)
    def _():
        carry_ref[...] = jnp.zeros_like(carry_ref)

    g = g_ref[...]
    lane = lax.broadcasted_iota(jnp.int32, g.shape, 1)
    is_i = (lane >= GATE_I0) & (lane < GATE_F0)
    logsig = jnp.minimum(g, 0.0) - jnp.log1p(jnp.exp(-jnp.abs(g)))
    blk = g.shape[0]
    r = lax.broadcasted_iota(jnp.int32, (blk, blk), 0)
    s = lax.broadcasted_iota(jnp.int32, (blk, blk), 1)
    tri = (s <= r).astype(F32)
    cs = jnp.dot(tri, logsig, precision=lax.Precision.HIGHEST, preferred_element_type=F32)
    carry = carry_ref[0:1, :]
    glob = cs + carry
    carry_ref[...] = jnp.broadcast_to(glob[blk - 1:blk, :], carry_ref.shape)
    out = jnp.where(lane < GATE_I0, glob, jnp.where(is_i, g, cs))
    col_ref[...] = out
    row_ref[0] = out.T[0:16, :]

    negc = -glob
    hi = negc.astype(BF16)
    r1 = negc - hi.astype(F32)
    mid = r1.astype(BF16)
    lo = (r1 - mid.astype(F32)).astype(BF16)
    rr = lax.broadcasted_iota(jnp.int32, (LANES, LANES), 0)
    cc = lax.broadcasted_iota(jnp.int32, (LANES, LANES), 1)
    j = (cc >= 3).astype(jnp.int32)
    for p in range(FOX_HEADS // 2):
        acc = None
        for i, piece in enumerate((hi, mid, lo)):
            onehot = ((rr == 2 * p + j) & (cc - 3 * j == i) & (cc < 6)).astype(BF16)
            term = _dot(piece, onehot)
            acc = term if acc is None else acc + term
        cp_ref[0, p] = acc.astype(BF16)


def _gateprep(gates, batch, seq):
    n = gates.shape[0]
    nc = seq // SEQ_BLOCK
    return pl.pallas_call(
        _gateprep_kernel,
        grid=(batch, nc),
        in_specs=[pl.BlockSpec((SEQ_BLOCK, LANES), lambda b, c: (b * nc + c, 0))],
        out_specs=(pl.BlockSpec((SEQ_BLOCK, LANES), lambda b, c: (b * nc + c, 0)),
                   pl.BlockSpec((1, 16, SEQ_BLOCK), lambda b, c: (b, 0, c)),
                   pl.BlockSpec((1, FOX_HEADS // 2, SEQ_BLOCK, LANES), lambda b, c: (b, 0, c, 0))),
        out_shape=(jax.ShapeDtypeStruct((n, LANES), F32),
                   jax.ShapeDtypeStruct((batch, 16, seq), F32),
                   jax.ShapeDtypeStruct((batch, FOX_HEADS // 2, seq, LANES), BF16)),
        scratch_shapes=[pltpu.VMEM((8, LANES), F32)],
        compiler_params=_params(("parallel", "arbitrary"), 32),
    )(gates)


FOX_ONES_ROWS = 16
FOX_VAUG_ROWS = FOX_HEAD_DIM + FOX_ONES_ROWS
FOX_QUERY_TILE = 2 * SEQ_BLOCK
FOX_KEY_UNROLL = 2


def _fox_kernel(qt_ref, k_ref, cp_ref, vt_ref, gain_ref, o_ref,
                kaug_sc, vaug_sc, qaug_sc, st_sc, acc0_sc, acc1_sc):
    qi = pl.program_id(2)
    tq = qt_ref.shape[1]
    tg = tq
    tk = SEQ_BLOCK
    hd = FOX_HEAD_DIM
    seq = k_ref.shape[0]

    @pl.when(qi == 0)
    def _():
        lane = lax.broadcasted_iota(jnp.int32, (tk, LANES), 1)

        def build(blk, carry):
            r0 = pl.multiple_of(blk * tk, tk)
            kp = k_ref[pl.ds(r0, tk), :].astype(F32)
            cp = cp_ref[0, 0, pl.ds(r0, tk), :].astype(F32)
            for h in range(2):
                kh = kp if h == 0 else pltpu.roll(kp, hd, axis=1)
                ch = pltpu.roll(cp, hd - 3 * h, axis=1)
                kaug = jnp.where(lane < hd, kh, jnp.where(lane < hd + 3, ch, 0.0))
                kaug_sc[h, pl.ds(r0, tk), :] = kaug.astype(BF16)
            return carry

        lax.fori_loop(0, seq // tk, build, 0)
        for h in range(2):
            vaug_sc[h, 0:hd, :] = vt_ref[h * hd:(h + 1) * hd, :]
            vaug_sc[h, hd:, :] = jnp.ones((FOX_ONES_ROWS, seq), BF16)

    ones3 = (lax.broadcasted_iota(jnp.int32, (FOX_ONES_ROWS, tq), 0) < 3).astype(BF16)
    for h in range(2):
        qaug_sc[h, 0:hd, :] = qt_ref[h * hd:(h + 1) * hd, :]
        qaug_sc[h, hd:hd + FOX_ONES_ROWS, :] = ones3
        qaug_sc[h, hd + FOX_ONES_ROWS:, :] = jnp.zeros((LANES - hd - FOX_ONES_ROWS, tq), BF16)

    def score_step(kg, m, masked=False):
        k0 = pl.multiple_of(kg * tg, tg)
        out = []
        for h in range(2):
            st = _dot(kaug_sc[h, pl.ds(k0, tg), :], qaug_sc[h])
            if masked:
                kk = lax.broadcasted_iota(jnp.int32, (tg, tq), 0)
                tt = lax.broadcasted_iota(jnp.int32, (tg, tq), 1)
                st = jnp.where(kk <= tt, st, -jnp.inf)
            st_sc[h, pl.ds(k0, tg), :] = st
            out.append(jnp.maximum(m[h], jnp.max(st, axis=0, keepdims=True)))
        return tuple(out)

    m = (jnp.full((1, tq), -jnp.inf, F32),) * 2
    m = lax.fori_loop(0, qi, score_step, m)
    m = score_step(qi, m, masked=True)

    acc0_sc[...] = jnp.zeros(acc0_sc.shape, F32)
    acc1_sc[...] = jnp.zeros(acc1_sc.shape, F32)
    acc = (acc0_sc, acc1_sc)

    def sum_step(kg, carry):
        k0 = pl.multiple_of(kg * tg, tg)
        for h in range(2):
            p = jnp.exp(st_sc[h, pl.ds(k0, tg), :] - m[h])
            acc[h][...] += _dot(vaug_sc[h, :, pl.ds(k0, tg)], p.astype(BF16))
        return carry

    lax.fori_loop(0, qi + 1, sum_step, 0)

    a0 = acc0_sc[...]
    a1 = acc1_sc[...]
    ot = jnp.concatenate([a0[0:hd] / a0[hd:hd + 1], a1[0:hd] / a1[hd:hd + 1]], axis=0)
    o = ot.T
    lane = lax.broadcasted_iota(jnp.int32, (tq, LANES), 1)
    lo = lane < FOX_HEAD_DIM
    sq = o * o
    ss0 = jnp.sum(jnp.where(lo, sq, 0.0), axis=-1, keepdims=True)
    ss1 = jnp.sum(jnp.where(lo, 0.0, sq), axis=-1, keepdims=True)
    inv = jnp.where(lo, lax.rsqrt(ss0 / FOX_HEAD_DIM + RMS_EPS),
                    lax.rsqrt(ss1 / FOX_HEAD_DIM + RMS_EPS))
    o_ref[...] = (o * inv * gain_ref[...]).astype(o_ref.dtype)


def _fox(fqt, fk, cpieces, fvt, fox_gain, batch, seq):
    n = fk.shape[0]
    tq = FOX_QUERY_TILE
    nq = seq // tq
    npair = FOX_HEADS // 2
    return pl.pallas_call(
        _fox_kernel,
        grid=(batch, npair, nq),
        in_specs=[pl.BlockSpec((LANES, tq), lambda b, hp, qi: (hp, b * nq + qi)),
                  pl.BlockSpec((seq, LANES), lambda b, hp, qi: (b, hp)),
                  pl.BlockSpec((1, 1, seq, LANES), lambda b, hp, qi: (b, hp, 0, 0)),
                  pl.BlockSpec((LANES, seq), lambda b, hp, qi: (hp, b)),
                  pl.BlockSpec((1, LANES), lambda b, hp, qi: (0, hp))],
        out_specs=pl.BlockSpec((tq, LANES), lambda b, hp, qi: (b * nq + qi, hp)),
        out_shape=jax.ShapeDtypeStruct((n, FOX_WIDTH), BF16),
        scratch_shapes=[pltpu.VMEM((2, seq, LANES), BF16),
                        pltpu.VMEM((2, FOX_VAUG_ROWS, seq), BF16),
                        pltpu.VMEM((2, LANES, tq), BF16),
                        pltpu.VMEM((2, seq, tq), F32),
                        pltpu.VMEM((FOX_VAUG_ROWS, tq), F32),
                        pltpu.VMEM((FOX_VAUG_ROWS, tq), F32)],
        compiler_params=_params(("parallel", "parallel", "arbitrary"), 48),
    )(fqt, fk, cpieces, fvt, fox_gain)


def _mlstm_kernel(qk_ref, v_ref, og_ref, col_ref, row_ref, cw_ref, gain_ref, o_ref,
                  tail_ref, buf_ref, c_sc, m_sc):
    c = pl.program_id(1)
    L = SEQ_BLOCK

    @pl.when(c == 0)
    def _():
        tail_ref[...] = jnp.zeros_like(tail_ref)
        c_sc[...] = jnp.zeros_like(c_sc)
        m_sc[...] = jnp.zeros_like(m_sc)

    x = qk_ref[...]
    buf_ref[0:8, :] = tail_ref[...]
    buf_ref[8:8 + L, :] = x
    tail_ref[...] = x[L - 8:L, :]
    y = x * cw_ref[CONV_WIDTH - 1:CONV_WIDTH, :]
    for j in range(CONV_WIDTH - 1):
        shift = CONV_WIDTH - 1 - j
        y = y + buf_ref[8 - shift:8 - shift + L, :] * cw_ref[j:j + 1, :]
    y = y * jax.nn.sigmoid(y)
    qc = y[:, :MLSTM_QK_WIDTH].astype(BF16)
    kf = y[:, MLSTM_QK_WIDTH:] * (MLSTM_QK_DIM ** -0.5)

    col = col_ref[...]
    row = row_ref[0]
    lane = lax.broadcasted_iota(jnp.int32, (L, LANES), 1)
    lo = lane < MLSTM_QK_DIM
    rr = lax.broadcasted_iota(jnp.int32, (L, L), 0)
    cc = lax.broadcasted_iota(jnp.int32, (L, L), 1)
    causal = cc <= rr
    ones_col = (lane == 0).astype(BF16)

    for h in range(MLSTM_HEADS):
        pair, half = divmod(h, 2)
        sel = lo if half == 0 else jnp.logical_not(lo)
        q2 = qc[:, pair * LANES:(pair + 1) * LANES]
        k2 = kf[:, pair * LANES:(pair + 1) * LANES]
        qh = jnp.where(sel, q2, jnp.zeros_like(q2))
        kb = k2.astype(BF16)
        bcol = col[:, GATE_F0 + h:GATE_F0 + h + 1]
        licol = col[:, GATE_I0 + h:GATE_I0 + h + 1]
        brow = row[GATE_F0 + h:GATE_F0 + h + 1, :]
        lirow = row[GATE_I0 + h:GATE_I0 + h + 1, :]
        g = bcol[L - 1:L, :]
        m_prev = m_sc[h][0:1, 0:1]

        d = jnp.where(causal, bcol + (lirow - brow), -jnp.inf)
        inter_log = bcol + m_prev
        m_t = jnp.maximum(inter_log, jnp.max(d, axis=-1, keepdims=True))
        w_inter = jnp.exp(inter_log - m_t)
        p = jnp.exp(d - m_t) * _dot_nt(qh, kb)
        vaug = jnp.concatenate([v_ref[:, h * LANES:(h + 1) * LANES], ones_col], axis=1)
        cstate = c_sc[h]
        tot = w_inter * _dot(qh, cstate.astype(BF16)) + _dot(p.astype(BF16), vaug)
        num = tot[:, :LANES]
        den = tot[:, LANES:LANES + 1]
        hout = num / jnp.maximum(jnp.abs(den), jnp.exp(-m_t))

        a = g - bcol + licol
        m_loc = jnp.max(a, axis=0, keepdims=True)
        wa = jnp.exp(a - m_loc)
        kw = jnp.where(sel, k2 * wa, 0.0).astype(BF16)
        kv = _dot_tn(kw, vaug)
        m_new = jnp.maximum(g + m_prev, m_loc)
        c_sc[h] = jnp.exp(g + m_prev - m_new) * cstate + jnp.exp(m_loc - m_new) * kv
        m_sc[h] = jnp.broadcast_to(m_new, m_sc.shape[1:])

        ms = jnp.mean(hout * hout, axis=-1, keepdims=True)
        hn = hout * lax.rsqrt(ms + RMS_EPS) * gain_ref[:, h * LANES:(h + 1) * LANES]
        gate = jax.nn.sigmoid(og_ref[:, h * LANES:(h + 1) * LANES])
        o_ref[:, h * LANES:(h + 1) * LANES] = (hn * gate).astype(o_ref.dtype)


def _mlstm(mqk, mv, mo, gcol, grow, conv_w, gain, batch, seq):
    n = mqk.shape[0]
    L = SEQ_BLOCK
    nc = seq // L
    row = lambda width: pl.BlockSpec((L, width), lambda b, c: (b * nc + c, 0))
    full = lambda a: pl.BlockSpec(a.shape, lambda b, c: (0, 0))
    return pl.pallas_call(
        _mlstm_kernel,
        grid=(batch, nc),
        in_specs=[row(512), row(512), row(512), row(LANES),
                  pl.BlockSpec((1, 16, L), lambda b, c: (b, 0, c)),
                  full(conv_w), full(gain)],
        out_specs=row(512),
        out_shape=jax.ShapeDtypeStruct((n, MLSTM_V_WIDTH), BF16),
        scratch_shapes=[pltpu.VMEM((8, 512), F32), pltpu.VMEM((8 + L, 512), F32),
                        pltpu.VMEM((MLSTM_HEADS, LANES, 2 * LANES), F32),
                        pltpu.VMEM((MLSTM_HEADS, 8, LANES), F32)],
        compiler_params=_params(("parallel", "arbitrary"), 48),
    )(mqk, mv, mo, gcol, grow, conv_w, gain)


def _outproj_kernel(fo_ref, mo_ref, w_ref, x_ref, g_ref, b_ref, o_ref):
    mix = _dot(fo_ref[...], w_ref[0:FOX_WIDTH, :]) + _dot(mo_ref[...], w_ref[FOX_WIDTH:, :])
    o_ref[...] = _layer_norm(DEEPNORM_ALPHA * x_ref[...] + mix, g_ref[...], b_ref[...])


def _outproj(fo, mo, w_out, x2d, g, b):
    n = x2d.shape[0]
    tm = ROW_TILE
    row = lambda width: pl.BlockSpec((tm, width), lambda i: (i, 0))
    full = lambda a: pl.BlockSpec(a.shape, lambda i: (0, 0))
    return pl.pallas_call(
        _outproj_kernel,
        grid=(n // tm,),
        in_specs=[row(512), row(512), full(w_out), row(D_MODEL), full(g), full(b)],
        out_specs=row(D_MODEL),
        out_shape=jax.ShapeDtypeStruct((n, D_MODEL), F32),
        compiler_params=_params(("parallel",), 48),
    )(fo, mo, w_out, x2d, g, b)


def _memkv_kernel(mem_ref, wk_ref, wv_ref, k_ref, v_ref):
    mb = mem_ref[...].astype(BF16)
    k_ref[...] = (_dot(mb, wk_ref[...]) * (XATTN_HEAD_DIM ** -0.5)).astype(BF16)
    v_ref[...] = _dot(mb, wv_ref[...]).astype(BF16)


def _memkv(mem2d, wk, wv, n_mem):
    n = mem2d.shape[0]
    row = pl.BlockSpec((n_mem, D_MODEL), lambda i: (i, 0))
    full = lambda a: pl.BlockSpec(a.shape, lambda i: (0, 0))
    return pl.pallas_call(
        _memkv_kernel,
        grid=(n // n_mem,),
        in_specs=[row, full(wk), full(wv)],
        out_specs=(row, row),
        out_shape=(jax.ShapeDtypeStruct((n, D_MODEL), BF16),) * 2,
        compiler_params=_params(("parallel",), 32),
    )(mem2d, wk, wv)


def _xattn_kernel(x_ref, k_ref, v_ref, wq_ref, wo_ref, g_ref, b_ref, wr_ref, br_ref,
                  o_ref, ob_ref, lg_ref):
    half = x_ref.shape[0] // 2
    for r in range(2):
        rows = slice(r * half, (r + 1) * half)
        x = x_ref[rows, :]
        q = _dot(x.astype(BF16), wq_ref[...]).astype(BF16)
        outs = []
        for h in range(XATTN_HEADS):
            sl = slice(h * XATTN_HEAD_DIM, (h + 1) * XATTN_HEAD_DIM)
            s = _dot_nt(q[:, sl], k_ref[:, sl])
            p = jnp.exp(s - jnp.max(s, axis=-1, keepdims=True))
            l = jnp.sum(p, axis=-1, keepdims=True)
            outs.append((_dot(p.astype(BF16), v_ref[:, sl]) / l).astype(BF16))
        o = jnp.concatenate(outs, axis=1)
        xa = _dot(o, wo_ref[...])
        x2 = _layer_norm(DEEPNORM_ALPHA * x + xa, g_ref[...], b_ref[...])
        o_ref[rows, :] = x2
        packed = _pack_bf16_pairs(x2)
        for j in range(ob_ref.shape[0]):
            ob_ref[j, rows, :] = packed[:, j * LANES:(j + 1) * LANES]
        x2h = x2.astype(BF16)
        x2l = (x2 - x2h.astype(F32)).astype(BF16)
        a = _dot_nt(wr_ref[...], x2h)
        b = _dot_nt(wr_ref[0:N_EXPERTS, :], x2l)
        lg_ref[:, rows] = a[0:N_EXPERTS] + a[N_EXPERTS:] + b + br_ref[...]


def _xattn(x1, kmem, vmem, wq, wo, g, b, wr, br, batch, seq, n_mem):
    n = x1.shape[0]
    tm = ROW_TILE
    nt = seq // tm
    row = lambda width: pl.BlockSpec((tm, width), lambda bi, i: (bi * nt + i, 0))
    full = lambda a: pl.BlockSpec(a.shape, lambda bi, i: (0, 0))
    kv = pl.BlockSpec((n_mem, D_MODEL), lambda bi, i: (bi, 0))
    return pl.pallas_call(
        _xattn_kernel,
        grid=(batch, nt),
        in_specs=[row(D_MODEL), kv, kv, full(wq), full(wo), full(g), full(b), full(wr), full(br)],
        out_specs=(row(D_MODEL),
                   pl.BlockSpec((X_PLANES, tm, LANES), lambda bi, i: (0, bi * nt + i, 0)),
                   pl.BlockSpec((N_EXPERTS, tm), lambda bi, i: (0, bi * nt + i))),
        out_shape=(jax.ShapeDtypeStruct((n, D_MODEL), F32),
                   jax.ShapeDtypeStruct((X_PLANES, n, LANES), jnp.uint32),
                   jax.ShapeDtypeStruct((N_EXPERTS, n), F32)),
        compiler_params=_params(("parallel", "parallel"), 48),
    )(x1, kmem, vmem, wq, wo, g, b, wr, br)


def _route_kernel(lg_ref, idx_ref, rank_ref, gate_ref, cnt_ref, carry_ref):
    i = pl.program_id(0)

    @pl.when(i == 0)
    def _():
        carry_ref[...] = jnp.zeros_like(carry_ref)

    lg = lg_ref[...]
    t = lg.shape[1]
    e_idx = lax.broadcasted_iota(jnp.int32, lg.shape, 0).astype(F32)
    sels, vals, idxs = [], [], []
    for _ in range(TOP_K):
        mx = jnp.max(lg, axis=0, keepdims=True)
        first = jnp.min(jnp.where(lg == mx, e_idx, float(N_EXPERTS)), axis=0, keepdims=True)
        sel = e_idx == first
        sels.append(sel)
        vals.append(mx)
        idxs.append(first)
        lg = jnp.where(sel, -jnp.inf, lg)
    exps = [jnp.exp(v - vals[0]) for v in vals]
    tot = exps[0] + exps[1] + exps[2] + exps[3]

    selmat = (sels[0] | sels[1] | sels[2] | sels[3])
    r = lax.broadcasted_iota(jnp.int32, (t, t), 0)
    s = lax.broadcasted_iota(jnp.int32, (t, t), 1)
    earlier = (r < s).astype(BF16)
    carry = carry_ref[:, 0:1]
    rankmat = _dot(selmat.astype(BF16), earlier) + carry
    new_carry = carry + jnp.sum(selmat.astype(F32), axis=1, keepdims=True)
    carry_ref[...] = jnp.broadcast_to(new_carry, carry_ref.shape)
    cnt_ref[...] = jnp.broadcast_to(new_carry, cnt_ref.shape).astype(jnp.int32)

    row8 = lax.broadcasted_iota(jnp.int32, (8, t), 0)
    row128 = lax.broadcasted_iota(jnp.int32, (LANES, t), 0)
    idx_out = jnp.zeros((8, t), F32)
    rank_out = jnp.zeros((8, t), F32)
    gate_out = jnp.zeros((LANES, t), F32)
    for k in range(TOP_K):
        rk = jnp.sum(jnp.where(sels[k], rankmat, 0.0), axis=0, keepdims=True)
        idx_out = jnp.where(row8 == k, idxs[k], idx_out)
        rank_out = jnp.where(row8 == k, rk, rank_out)
        gate_out = jnp.where(row128 == k, exps[k] / tot, gate_out)
    idx_ref[...] = idx_out.astype(jnp.int32)
    rank_ref[...] = rank_out.astype(jnp.int32)
    gate_ref[...] = gate_out.T


def _route(logits_t):
    n = logits_t.shape[1]
    t = ROUTE_TILE
    col = lambda rows: pl.BlockSpec((rows, t), lambda i: (0, i))
    return pl.pallas_call(
        _route_kernel,
        grid=(n // t,),
        in_specs=[col(N_EXPERTS)],
        out_specs=(col(8), col(8), pl.BlockSpec((t, LANES), lambda i: (i, 0)),
                   pl.BlockSpec((N_EXPERTS, LANES), lambda i: (0, 0))),
        out_shape=(jax.ShapeDtypeStruct((8, n), jnp.int32),
                   jax.ShapeDtypeStruct((8, n), jnp.int32),
                   jax.ShapeDtypeStruct((n, LANES), F32),
                   jax.ShapeDtypeStruct((N_EXPERTS, LANES), jnp.int32)),
        scratch_shapes=[pltpu.VMEM((N_EXPERTS, LANES), F32)],
        compiler_params=_params(("arbitrary",), 32),
    )(logits_t)


def _expert_kernel(blk_e_ref, nb_ref, x_ref, wgu_ref, bgu_ref, wd_ref, bd_ref, y_ref,
                   wgu_sc, wd_sc, prev_ref):
    i = pl.program_id(0)
    e = blk_e_ref[i]

    @pl.when(i == 0)
    def _():
        prev_ref[0] = -1

    @pl.when(i < nb_ref[0])
    def _():
        @pl.when(e != prev_ref[0])
        def _():
            wgu_sc[...] = wgu_ref[0].astype(BF16)
            wd_sc[...] = wd_ref[0].astype(BF16)
            prev_ref[0] = e

        packed = jnp.concatenate([x_ref[j] for j in range(X_PLANES)], axis=1)
        xb = _unpack_bf16_pairs(packed)
        hids = []
        for c in range(D_EXPERT // EXPERT_CHUNK):
            g0 = c * EXPERT_CHUNK
            l0 = D_EXPERT + g0
            gate = _dot(xb, wgu_sc[:, g0:g0 + EXPERT_CHUNK]) + bgu_ref[0, :, g0:g0 + EXPERT_CHUNK]
            lin = _dot(xb, wgu_sc[:, l0:l0 + EXPERT_CHUNK]) + bgu_ref[0, :, l0:l0 + EXPERT_CHUNK]
            gate = jnp.minimum(gate, SWIGLU_LIMIT)
            lin = jnp.clip(lin, -SWIGLU_LIMIT, SWIGLU_LIMIT)
            hids.append((gate * jax.nn.sigmoid(SWIGLU_ALPHA * gate) * (lin + 1.0)).astype(BF16))
        y = _dot(jnp.concatenate(hids, axis=1), wd_sc[...]) + bd_ref[0]
        for j in range(Y_PLANES):
            y_ref[j] = y[:, j * LANES:(j + 1) * LANES]

    @pl.when(i >= nb_ref[0])
    def _():
        y_ref[...] = jnp.zeros_like(y_ref)


def _experts(blk_e, nb_used, xs, w_gu, b_gu, w_d, b_d):
    p = xs.shape[1]
    g = EXPERT_ROWS
    grid_spec = pltpu.PrefetchScalarGridSpec(
        num_scalar_prefetch=2,
        grid=(p // g,),
        in_specs=[pl.BlockSpec((X_PLANES, g, LANES), lambda i, be, nb: (0, i, 0)),
                  pl.BlockSpec((1, D_MODEL, 2 * D_EXPERT), lambda i, be, nb: (be[i], 0, 0)),
                  pl.BlockSpec((1, 1, 2 * D_EXPERT), lambda i, be, nb: (be[i], 0, 0)),
                  pl.BlockSpec((1, D_EXPERT, D_MODEL), lambda i, be, nb: (be[i], 0, 0)),
                  pl.BlockSpec((1, 1, D_MODEL), lambda i, be, nb: (be[i], 0, 0))],
        out_specs=pl.BlockSpec((Y_PLANES, g, LANES), lambda i, be, nb: (0, i, 0)),
        scratch_shapes=[pltpu.VMEM((D_MODEL, 2 * D_EXPERT), BF16),
                        pltpu.VMEM((D_EXPERT, D_MODEL), BF16),
                        pltpu.SMEM((1,), jnp.int32)],
    )
    return pl.pallas_call(
        _expert_kernel,
        grid_spec=grid_spec,
        out_shape=jax.ShapeDtypeStruct((Y_PLANES, p, LANES), F32),
        compiler_params=_params(("arbitrary",), 56),
    )(blk_e, nb_used, xs, w_gu, b_gu, w_d, b_d)


def _combine_kernel(y_ref, gate_ref, x_ref, g_ref, b_ref, o_ref):
    gate = gate_ref[...]
    ff = None
    for k in range(TOP_K):
        yk = jnp.concatenate([y_ref[k, j] for j in range(Y_PLANES)], axis=1) * gate[:, k:k + 1]
        ff = yk if ff is None else ff + yk
    o_ref[...] = _layer_norm(DEEPNORM_ALPHA * x_ref[...] + ff, g_ref[...], b_ref[...])


def _combine(yg, gate, x2, g, b):
    n = x2.shape[0]
    tm = SEQ_BLOCK
    row = lambda width: pl.BlockSpec((tm, width), lambda i: (i, 0))
    full = lambda a: pl.BlockSpec(a.shape, lambda i: (0, 0))
    return pl.pallas_call(
        _combine_kernel,
        grid=(n // tm,),
        in_specs=[pl.BlockSpec((TOP_K, Y_PLANES, tm, LANES), lambda i: (0, 0, i, 0)),
                  row(LANES), row(D_MODEL), full(g), full(b)],
        out_specs=row(D_MODEL),
        out_shape=jax.ShapeDtypeStruct((n, D_MODEL), F32),
        compiler_params=_params(("parallel",), 32),
    )(yg, gate, x2, g, b)


SC_WINDOW = 128


def _sc_mesh():
    return plsc.VectorSubcoreMesh(core_axis_name="core", subcore_axis_name="subcore")


def _sc_gather(table, idx):
    m = idx.shape[0]

    @pl.kernel(out_type=jax.ShapeDtypeStruct((m, LANES), table.dtype), mesh=_sc_mesh())
    def gather_kernel(table_hbm, idx_hbm, out_hbm):
        def body(idx_vmem, out_vmem):
            pltpu.sync_copy(table_hbm.at[idx_vmem.at[0]], out_vmem)

        pltpu.emit_pipeline(
            body,
            grid=(m // SC_WINDOW,),
            in_specs=[pl.BlockSpec((1, SC_WINDOW), lambda i: (0, i))],
            out_specs=[pl.BlockSpec((SC_WINDOW, LANES), lambda i: (i, 0))],
            core_axis_name=("core", "subcore"),
            dimension_semantics=(pltpu.PARALLEL,),
        )(idx_hbm, out_hbm)

    return gather_kernel(table, idx.reshape(1, m))


def _sc_scatter(src, idx, out_rows):
    m = idx.shape[0]
    src_blocks = src.shape[0] // SC_WINDOW

    @pl.kernel(out_type=jax.ShapeDtypeStruct((out_rows, LANES), src.dtype), mesh=_sc_mesh())
    def scatter_kernel(src_hbm, idx_hbm, out_hbm):
        def body(src_vmem, idx_vmem):
            pltpu.sync_copy(src_vmem, out_hbm.at[idx_vmem.at[0]])

        pltpu.emit_pipeline(
            body,
            grid=(m // SC_WINDOW,),
            in_specs=[pl.BlockSpec((SC_WINDOW, LANES), lambda i: (i % src_blocks, 0)),
                      pl.BlockSpec((1, SC_WINDOW), lambda i: (0, i))],
            out_specs=[],
            core_axis_name=("core", "subcore"),
            dimension_semantics=(pltpu.PARALLEL,),
        )(src_hbm, idx_hbm)

    return scatter_kernel(src, idx.reshape(1, m))


def _layer(x, mem, w_in, fox_f_bias, conv_w, i_bias, f_bias, fox_g, mlstm_g, w_mix_out,
           ln1_g, ln1_b, w_xq, w_xk, w_xv, w_xo, ln2_g, ln2_b, w_router, b_router,
           w_gate_up, b_gate_up, w_down, b_down, ln3_g, ln3_b):
    batch, seq, d = x.shape
    n_mem = mem.shape[1]
    n = batch * seq
    x2d = x.reshape(n, d)

    o_ff = 3 * FOX_WIDTH
    o_mqk = o_ff + FOX_HEADS
    o_mv = o_mqk + 2 * MLSTM_QK_WIDTH
    o_mi = o_mv + MLSTM_V_WIDTH
    o_mf = o_mi + MLSTM_HEADS
    o_mo = o_mf + MLSTM_HEADS
    n_gate = FOX_HEADS + 2 * MLSTM_HEADS
    w_r = jnp.concatenate(
        [w_in[:, FOX_WIDTH:2 * FOX_WIDTH], w_in[:, o_mqk:o_mi], w_in[:, o_mo:],
         w_in[:, o_ff:o_mqk], w_in[:, o_mi:o_mo],
         jnp.zeros((d, LANES - n_gate), w_in.dtype)], axis=1).astype(BF16)
    w_t = jnp.concatenate([w_in[:, :FOX_WIDTH], w_in[:, 2 * FOX_WIDTH:o_ff]], axis=1).T.astype(BF16)
    gate_bias = jnp.concatenate(
        [fox_f_bias, i_bias, f_bias, jnp.zeros((LANES - n_gate,), F32)]).reshape(1, LANES)

    fqt, fk, fvt, mqk, mv, mo, gates = _inproj(x2d, w_r, w_t, gate_bias)
    gcol, grow, cpieces = _gateprep(gates, batch, seq)
    fo = _fox(fqt, fk, cpieces, fvt, fox_g.reshape(1, FOX_WIDTH), batch, seq)
    mo_out = _mlstm(mqk, mv, mo, gcol, grow, conv_w, mlstm_g.reshape(1, MLSTM_V_WIDTH), batch, seq)
    x1 = _outproj(fo, mo_out, w_mix_out.astype(BF16), x2d, ln1_g.reshape(1, d), ln1_b.reshape(1, d))

    kmem, vmem = _memkv(mem.reshape(batch * n_mem, d), w_xk.astype(BF16), w_xv.astype(BF16), n_mem)
    wrt = w_router.T
    wrt_hi = wrt.astype(BF16)
    wrt_lo = (wrt - wrt_hi.astype(F32)).astype(BF16)
    x2, x2p, logits_t = _xattn(x1, kmem, vmem, w_xq.astype(BF16), w_xo.astype(BF16),
                               ln2_g.reshape(1, d), ln2_b.reshape(1, d),
                               jnp.concatenate([wrt_hi, wrt_lo], axis=0),
                               b_router.reshape(N_EXPERTS, 1), batch, seq, n_mem)

    idx_t, rank_t, gate, cnt = _route(logits_t)
    counts = cnt[:, 0]
    g_rows = EXPERT_ROWS
    padded = ((counts + g_rows - 1) // g_rows) * g_rows
    pad_end = jnp.cumsum(padded)
    pad_start = pad_end - padded
    experts = jnp.arange(N_EXPERTS, dtype=jnp.int32)
    sel = idx_t[:TOP_K, :, None] == experts[None, None, :]
    pos_t = jnp.sum(jnp.where(sel, pad_start[None, None, :], 0), axis=-1) + rank_t[:TOP_K]
    p_rows = n * TOP_K + N_EXPERTS * g_rows
    nb = p_rows // g_rows
    blk_start = jnp.arange(nb, dtype=jnp.int32) * g_rows
    blk_e = jnp.minimum(jnp.sum((pad_end[None, :] <= blk_start[:, None]).astype(jnp.int32), axis=1),
                        N_EXPERTS - 1)
    nb_used = (pad_end[-1:] // g_rows).astype(jnp.int32)

    def piece_index(planes):
        off = jnp.arange(planes, dtype=jnp.int32) * p_rows
        return (pos_t[:, None, :] + off[None, :, None]).reshape(-1)

    xs = _sc_scatter(x2p.reshape(X_PLANES * n, LANES), piece_index(X_PLANES), X_PLANES * p_rows)
    y = _experts(blk_e, nb_used, xs.reshape(X_PLANES, p_rows, LANES), w_gate_up,
                 b_gate_up.reshape(N_EXPERTS, 1, -1), w_down, b_down.reshape(N_EXPERTS, 1, -1))
    yg = _sc_gather(y.reshape(Y_PLANES * p_rows, LANES), piece_index(Y_PLANES))
    out = _combine(yg.reshape(TOP_K, Y_PLANES, n, LANES), gate, x2,
                   ln3_g.reshape(1, d), ln3_b.reshape(1, d))
    return out.reshape(batch, seq, d)


def kernel(x, mem, w_in, fox_f_bias, mlstm_conv_w, mlstm_i_bias, mlstm_f_bias, fox_norm_g, mlstm_norm_g, w_mix_out, ln1_g, ln1_b, w_xq, w_xk, w_xv, w_xo, ln2_g, ln2_b, w_router, b_router, w_gate_up, b_gate_up, w_down, b_down, ln3_g, ln3_b):
    for l in range(w_in.shape[0]):
        x = _layer(x, mem, w_in[l], fox_f_bias[l], mlstm_conv_w[l], mlstm_i_bias[l],
                   mlstm_f_bias[l], fox_norm_g[l], mlstm_norm_g[l], w_mix_out[l],
                   ln1_g[l], ln1_b[l], w_xq[l], w_xk[l], w_xv[l], w_xo[l], ln2_g[l], ln2_b[l],
                   w_router[l], b_router[l], w_gate_up[l], b_gate_up[l], w_down[l], b_down[l],
                   ln3_g[l], ln3_b[l])
    return x
```

```python
import jax
import jax.numpy as jnp
from jax import lax
from jax.experimental import pallas as pl
from jax.experimental.pallas import tpu as pltpu
from jax.experimental.pallas import tpu_sc as plsc

F32 = jnp.float32
BF16 = jnp.bfloat16

D_MODEL = 1024
FOX_HEADS = 8
FOX_HEAD_DIM = 64
FOX_WIDTH = FOX_HEADS * FOX_HEAD_DIM
MLSTM_HEADS = 4
MLSTM_QK_DIM = 64
MLSTM_V_DIM = 128
MLSTM_QK_WIDTH = MLSTM_HEADS * MLSTM_QK_DIM
MLSTM_V_WIDTH = MLSTM_HEADS * MLSTM_V_DIM
CONV_WIDTH = 4
XATTN_HEADS = 4
XATTN_HEAD_DIM = D_MODEL // XATTN_HEADS
N_EXPERTS = 32
TOP_K = 4
D_EXPERT = D_MODEL
SWIGLU_LIMIT = 7.0
SWIGLU_ALPHA = 1.702
DEEPNORM_ALPHA = 2.0 ** 0.25
LN_EPS = 1e-5
RMS_EPS = 1e-6

LANES = 128
SEQ_BLOCK = 256
ROW_TILE = 512
EXPERT_ROWS = 512
EXPERT_CHUNK = 256
ROUTE_TILE = 512
X_PLANES = D_MODEL // 2 // LANES
Y_PLANES = D_MODEL // LANES
GATE_I0 = FOX_HEADS
GATE_F0 = FOX_HEADS + MLSTM_HEADS
NEG_BIG = -1e30

MIB = 1024 * 1024


def _params(semantics, vmem_mib):
    return pltpu.CompilerParams(dimension_semantics=semantics,
                                vmem_limit_bytes=vmem_mib * MIB)


def _layer_norm(y, g, b):
    mu = jnp.mean(y, axis=-1, keepdims=True)
    yc = y - mu
    var = jnp.mean(yc * yc, axis=-1, keepdims=True)
    return yc * lax.rsqrt(var + LN_EPS) * g + b


def _dot(a, b):
    return jnp.dot(a, b, preferred_element_type=F32)


def _dot_nt(a, b):
    return lax.dot_general(a, b, (((1,), (1,)), ((), ())), preferred_element_type=F32)


def _dot_tn(a, b):
    return lax.dot_general(a, b, (((0,), (0,)), ((), ())), preferred_element_type=F32)


def _pack_bf16_pairs(x):
    w = x.shape[1] // 2
    lo = pltpu.bitcast(x[:, :w].astype(BF16).astype(F32), jnp.uint32)
    hi = pltpu.bitcast(x[:, w:].astype(BF16).astype(F32), jnp.uint32)
    return (lo >> 16) | hi


def _unpack_bf16_pairs(u):
    lo = pltpu.bitcast(u << 16, F32).astype(BF16)
    hi = pltpu.bitcast(u & jnp.uint32(0xFFFF0000), F32).astype(BF16)
    return jnp.concatenate([lo, hi], axis=1)


def _inproj_kernel(x_ref, w_ref, wt_ref, gb_ref, fqt_ref, fk_ref, fvt_ref, mqk_ref, mv_ref, mo_ref,
                   g_ref):
    xb = x_ref[...].astype(BF16)

    def mm(c0, width):
        return _dot(xb, w_ref[:, c0:c0 + width])

    fqt_ref[...] = (_dot_nt(wt_ref[0:FOX_WIDTH, :], xb) * (FOX_HEAD_DIM ** -0.5)).astype(BF16)
    fvt_ref[...] = _dot_nt(wt_ref[FOX_WIDTH:, :], xb).astype(BF16)
    fk_ref[...] = mm(0, 512).astype(BF16)
    mqk_ref[...] = mm(512, 512)
    mv_ref[...] = mm(1024, 512).astype(BF16)
    mo_ref[...] = mm(1536, 512)
    g_ref[...] = mm(2048, LANES) + gb_ref[...]


def _inproj(x2d, w_r, w_t, gate_bias):
    n = x2d.shape[0]
    tm = ROW_TILE
    row = lambda width: pl.BlockSpec((tm, width), lambda i: (i, 0))
    col = pl.BlockSpec((FOX_WIDTH, tm), lambda i: (0, i))
    full = lambda a: pl.BlockSpec(a.shape, lambda i: (0, 0))
    out_shapes = (
        jax.ShapeDtypeStruct((FOX_WIDTH, n), BF16),
        jax.ShapeDtypeStruct((n, 512), BF16),
        jax.ShapeDtypeStruct((FOX_WIDTH, n), BF16),
        jax.ShapeDtypeStruct((n, 512), F32),
        jax.ShapeDtypeStruct((n, 512), BF16),
        jax.ShapeDtypeStruct((n, 512), F32),
        jax.ShapeDtypeStruct((n, LANES), F32),
    )
    return pl.pallas_call(
        _inproj_kernel,
        grid=(n // tm,),
        in_specs=[row(D_MODEL), full(w_r), full(w_t), full(gate_bias)],
        out_specs=(col, row(512), col, row(512), row(512), row(512), row(LANES)),
        out_shape=out_shapes,
        compiler_params=_params(("parallel",), 48),
    )(x2d, w_r, w_t, gate_bias)


def _gateprep_kernel(g_ref, col_ref, row_ref, cp_ref, carry_ref):
    c = pl.program_id(1)

    @pl.when(c == 0)
    def _():
        carry_ref[...] = jnp.zeros_like(carry_ref)

    g = g_ref[...]
    lane = lax.broadcasted_iota(jnp.int32, g.shape, 1)
    is_i = (lane >= GATE_I0) & (lane < GATE_F0)
    logsig = jnp.minimum(g, 0.0) - jnp.log1p(jnp.exp(-jnp.abs(g)))
    blk = g.shape[0]
    r = lax.broadcasted_iota(jnp.int32, (blk, blk), 0)
    s = lax.broadcasted_iota(jnp.int32, (blk, blk), 1)
    tri = (s <= r).astype(F32)
    cs = jnp.dot(tri, logsig, precision=lax.Precision.HIGHEST, preferred_element_type=F32)
    carry = carry_ref[0:1, :]
    glob = cs + carry
    carry_ref[...] = jnp.broadcast_to(glob[blk - 1:blk, :], carry_ref.shape)
    out = jnp.where(lane < GATE_I0, glob, jnp.where(is_i, g, cs))
    col_ref[...] = out
    row_ref[0] = out.T[0:16, :]

    negc = -glob
    hi = negc.astype(BF16)
    r1 = negc - hi.astype(F32)
    mid = r1.astype(BF16)
    lo = (r1 - mid.astype(F32)).astype(BF16)
    rr = lax.broadcasted_iota(jnp.int32, (LANES, LANES), 0)
    cc = lax.broadcasted_iota(jnp.int32, (LANES, LANES), 1)
    j = (cc >= 3).astype(jnp.int32)
    for p in range(FOX_HEADS // 2):
        acc = None
        for i, piece in enumerate((hi, mid, lo)):
            onehot = ((rr == 2 * p + j) & (cc - 3 * j == i) & (cc < 6)).astype(BF16)
            term = _dot(piece, onehot)
            acc = term if acc is None else acc + term
        cp_ref[0, p] = acc.astype(BF16)


def _gateprep(gates, batch, seq):
    n = gates.shape[0]
    nc = seq // SEQ_BLOCK
    return pl.pallas_call(
        _gateprep_kernel,
        grid=(batch, nc),
        in_specs=[pl.BlockSpec((SEQ_BLOCK, LANES), lambda b, c: (b * nc + c, 0))],
        out_specs=(pl.BlockSpec((SEQ_BLOCK, LANES), lambda b, c: (b * nc + c, 0)),
                   pl.BlockSpec((1, 16, SEQ_BLOCK), lambda b, c: (b, 0, c)),
                   pl.BlockSpec((1, FOX_HEADS // 2, SEQ_BLOCK, LANES), lambda b, c: (b, 0, c, 0))),
        out_shape=(jax.ShapeDtypeStruct((n, LANES), F32),
                   jax.ShapeDtypeStruct((batch, 16, seq), F32),
                   jax.ShapeDtypeStruct((batch, FOX_HEADS // 2, seq, LANES), BF16)),
        scratch_shapes=[pltpu.VMEM((8, LANES), F32)],
        compiler_params=_params(("parallel", "arbitrary"), 32),
    )(gates)


FOX_ONES_ROWS = 16
FOX_VAUG_ROWS = FOX_HEAD_DIM + FOX_ONES_ROWS
FOX_QUERY_TILE = 2 * SEQ_BLOCK
FOX_KEY_UNROLL = 2


def _fox_kernel(qt_ref, k_ref, cp_ref, vt_ref, gain_ref, o_ref,
                kaug_sc, vaug_sc, qaug_sc, sta_sc, stb_sc, m_sc, acc0_sc, acc1_sc):
    qi = pl.program_id(2)
    tq = qt_ref.shape[1]
    tg = tq
    tk = SEQ_BLOCK
    hd = FOX_HEAD_DIM
    seq = k_ref.shape[0]

    @pl.when(qi == 0)
    def _():
        lane = lax.broadcasted_iota(jnp.int32, (tk, LANES), 1)

        def build(blk, carry):
            r0 = pl.multiple_of(blk * tk, tk)
            kp = k_ref[pl.ds(r0, tk), :].astype(F32)
            cp = cp_ref[0, 0, pl.ds(r0, tk), :].astype(F32)
            for h in range(2):
                kh = kp if h == 0 else pltpu.roll(kp, hd, axis=1)
                ch = pltpu.roll(cp, hd - 3 * h, axis=1)
                kaug = jnp.where(lane < hd, kh, jnp.where(lane < hd + 3, ch, 0.0))
                kaug_sc[h, pl.ds(r0, tk), :] = kaug.astype(BF16)
            return carry

        lax.fori_loop(0, seq // tk, build, 0)
        for h in range(2):
            vaug_sc[h, 0:hd, :] = vt_ref[h * hd:(h + 1) * hd, :]
            vaug_sc[h, hd:, :] = jnp.ones((FOX_ONES_ROWS, seq), BF16)

    ones3 = (lax.broadcasted_iota(jnp.int32, (FOX_ONES_ROWS, tq), 0) < 3).astype(BF16)
    for h in range(2):
        qaug_sc[h, 0:hd, :] = qt_ref[h * hd:(h + 1) * hd, :]
        qaug_sc[h, hd:hd + FOX_ONES_ROWS, :] = ones3
        qaug_sc[h, hd + FOX_ONES_ROWS:, :] = jnp.zeros((LANES - hd - FOX_ONES_ROWS, tq), BF16)

    def put_scores(kg, slot, masked=False):
        k0 = pl.multiple_of(kg * tg, tg)
        for h in range(2):
            st = _dot(kaug_sc[h, pl.ds(k0, tg), :], qaug_sc[h])
            if masked:
                kk = lax.broadcasted_iota(jnp.int32, (tg, tq), 0)
                tt = lax.broadcasted_iota(jnp.int32, (tg, tq), 1)
                st = jnp.where(kk <= tt, st, -jnp.inf)
            slot[h] = st

    m_sc[...] = jnp.full(m_sc.shape, -jnp.inf, F32)
    acc0_sc[...] = jnp.zeros(acc0_sc.shape, F32)
    acc1_sc[...] = jnp.zeros(acc1_sc.shape, F32)
    acc = (acc0_sc, acc1_sc)

    def absorb(kg, slot):
        k0 = pl.multiple_of(kg * tg, tg)
        half = tg // 2
        for h in range(2):
            st = slot[h]
            m_prev = m_sc[h]
            m_new = jnp.maximum(m_prev, jnp.max(st, axis=0, keepdims=True))
            pv = None
            for u in range(2):
                p = jnp.exp(st[u * half:(u + 1) * half] - m_new)
                part = _dot(vaug_sc[h, :, pl.ds(k0 + u * half, half)], p.astype(BF16))
                pv = part if pv is None else pv + part
            acc[h][...] = jnp.exp(m_prev - m_new) * acc[h][...] + pv
            m_sc[h] = m_new

    sa, sb = sta_sc, stb_sc
    n_loop = jnp.maximum(qi - 1, 0) // 2

    @pl.when(qi > 0)
    def _():
        put_scores(0, sa)

    def body(j, carry):
        put_scores(2 * j + 1, sb)
        absorb(2 * j, sa)
        put_scores(2 * j + 2, sa)
        absorb(2 * j + 1, sb)
        return carry

    lax.fori_loop(0, n_loop, body, 0)
    done = 2 * n_loop

    @pl.when(qi == 0)
    def _():
        put_scores(qi, sa, masked=True)
        absorb(qi, sa)

    @pl.when((qi > 0) & (qi - done == 1))
    def _():
        put_scores(qi, sb, masked=True)
        absorb(done, sa)
        absorb(qi, sb)

    @pl.when((qi > 0) & (qi - done == 2))
    def _():
        put_scores(done + 1, sb)
        absorb(done, sa)
        put_scores(qi, sa, masked=True)
        absorb(done + 1, sb)
        absorb(qi, sa)

    a0 = acc0_sc[...]
    a1 = acc1_sc[...]
    ot = jnp.concatenate([a0[0:hd] / a0[hd:hd + 1], a1[0:hd] / a1[hd:hd + 1]], axis=0)
    o = ot.T
    lane = lax.broadcasted_iota(jnp.int32, (tq, LANES), 1)
    lo = lane < FOX_HEAD_DIM
    sq = o * o
    ss0 = jnp.sum(jnp.where(lo, sq, 0.0), axis=-1, keepdims=True)
    ss1 = jnp.sum(jnp.where(lo, 0.0, sq), axis=-1, keepdims=True)
    inv = jnp.where(lo, lax.rsqrt(ss0 / FOX_HEAD_DIM + RMS_EPS),
                    lax.rsqrt(ss1 / FOX_HEAD_DIM + RMS_EPS))
    o_ref[...] = (o * inv * gain_ref[...]).astype(o_ref.dtype)


def _fox(fqt, fk, cpieces, fvt, fox_gain, batch, seq):
    n = fk.shape[0]
    tq = FOX_QUERY_TILE
    nq = seq // tq
    npair = FOX_HEADS // 2
    return pl.pallas_call(
        _fox_kernel,
        grid=(batch, npair, nq),
        in_specs=[pl.BlockSpec((LANES, tq), lambda b, hp, qi: (hp, b * nq + qi)),
                  pl.BlockSpec((seq, LANES), lambda b, hp, qi: (b, hp)),
                  pl.BlockSpec((1, 1, seq, LANES), lambda b, hp, qi: (b, hp, 0, 0)),
                  pl.BlockSpec((LANES, seq), lambda b, hp, qi: (hp, b)),
                  pl.BlockSpec((1, LANES), lambda b, hp, qi: (0, hp))],
        out_specs=pl.BlockSpec((tq, LANES), lambda b, hp, qi: (b * nq + qi, hp)),
        out_shape=jax.ShapeDtypeStruct((n, FOX_WIDTH), BF16),
        scratch_shapes=[pltpu.VMEM((2, seq, LANES), BF16),
                        pltpu.VMEM((2, FOX_VAUG_ROWS, seq), BF16),
                        pltpu.VMEM((2, LANES, tq), BF16),
                        pltpu.VMEM((2, tq, tq), F32),
                        pltpu.VMEM((2, tq, tq), F32),
                        pltpu.VMEM((2, 1, tq), F32),
                        pltpu.VMEM((FOX_VAUG_ROWS, tq), F32),
                        pltpu.VMEM((FOX_VAUG_ROWS, tq), F32)],
        compiler_params=_params(("parallel", "parallel", "arbitrary"), 48),
    )(fqt, fk, cpieces, fvt, fox_gain)


def _mlstm_kernel(qk_ref, v_ref, og_ref, col_ref, row_ref, cw_ref, gain_ref, o_ref,
                  tail_ref, buf_ref, c_sc, m_sc):
    c = pl.program_id(1)
    L = SEQ_BLOCK

    @pl.when(c == 0)
    def _():
        tail_ref[...] = jnp.zeros_like(tail_ref)
        c_sc[...] = jnp.zeros_like(c_sc)
        m_sc[...] = jnp.zeros_like(m_sc)

    x = qk_ref[...]
    buf_ref[0:8, :] = tail_ref[...]
    buf_ref[8:8 + L, :] = x
    tail_ref[...] = x[L - 8:L, :]
    y = x * cw_ref[CONV_WIDTH - 1:CONV_WIDTH, :]
    for j in range(CONV_WIDTH - 1):
        shift = CONV_WIDTH - 1 - j
        y = y + buf_ref[8 - shift:8 - shift + L, :] * cw_ref[j:j + 1, :]
    y = y * jax.nn.sigmoid(y)
    qc = y[:, :MLSTM_QK_WIDTH].astype(BF16)
    kf = y[:, MLSTM_QK_WIDTH:] * (MLSTM_QK_DIM ** -0.5)

    col = col_ref[...]
    row = row_ref[0]
    lane = lax.broadcasted_iota(jnp.int32, (L, LANES), 1)
    lo = lane < MLSTM_QK_DIM
    rr = lax.broadcasted_iota(jnp.int32, (L, L), 0)
    cc = lax.broadcasted_iota(jnp.int32, (L, L), 1)
    causal = cc <= rr
    ones_col = (lane == 0).astype(BF16)

    for h in range(MLSTM_HEADS):
        pair, half = divmod(h, 2)
        sel = lo if half == 0 else jnp.logical_not(lo)
        q2 = qc[:, pair * LANES:(pair + 1) * LANES]
        k2 = kf[:, pair * LANES:(pair + 1) * LANES]
        qh = jnp.where(sel, q2, jnp.zeros_like(q2))
        kb = k2.astype(BF16)
        bcol = col[:, GATE_F0 + h:GATE_F0 + h + 1]
        licol = col[:, GATE_I0 + h:GATE_I0 + h + 1]
        brow = row[GATE_F0 + h:GATE_F0 + h + 1, :]
        lirow = row[GATE_I0 + h:GATE_I0 + h + 1, :]
        g = bcol[L - 1:L, :]
        m_prev = m_sc[h][0:1, 0:1]

        d = jnp.where(causal, bcol + (lirow - brow), -jnp.inf)
        inter_log = bcol + m_prev
        m_t = jnp.maximum(inter_log, jnp.max(d, axis=-1, keepdims=True))
        w_inter = jnp.exp(inter_log - m_t)
        p = jnp.exp(d - m_t) * _dot_nt(qh, kb)
        vaug = jnp.concatenate([v_ref[:, h * LANES:(h + 1) * LANES], ones_col], axis=1)
        cstate = c_sc[h]
        tot = w_inter * _dot(qh, cstate.astype(BF16)) + _dot(p.astype(BF16), vaug)
        num = tot[:, :LANES]
        den = tot[:, LANES:LANES + 1]
        hout = num / jnp.maximum(jnp.abs(den), jnp.exp(-m_t))

        a = g - bcol + licol
        m_loc = jnp.max(a, axis=0, keepdims=True)
        wa = jnp.exp(a - m_loc)
        kw = jnp.where(sel, k2 * wa, 0.0).astype(BF16)
        kv = _dot_tn(kw, vaug)
        m_new = jnp.maximum(g + m_prev, m_loc)
        c_sc[h] = jnp.exp(g + m_prev - m_new) * cstate + jnp.exp(m_loc - m_new) * kv
        m_sc[h] = jnp.broadcast_to(m_new, m_sc.shape[1:])

        ms = jnp.mean(hout * hout, axis=-1, keepdims=True)
        hn = hout * lax.rsqrt(ms + RMS_EPS) * gain_ref[:, h * LANES:(h + 1) * LANES]
        gate = jax.nn.sigmoid(og_ref[:, h * LANES:(h + 1) * LANES])
        o_ref[:, h * LANES:(h + 1) * LANES] = (hn * gate).astype(o_ref.dtype)


def _mlstm(mqk, mv, mo, gcol, grow, conv_w, gain, batch, seq):
    n = mqk.shape[0]
    L = SEQ_BLOCK
    nc = seq // L
    row = lambda width: pl.BlockSpec((L, width), lambda b, c: (b * nc + c, 0))
    full = lambda a: pl.BlockSpec(a.shape, lambda b, c: (0, 0))
    return pl.pallas_call(
        _mlstm_kernel,
        grid=(batch, nc),
        in_specs=[row(512), row(512), row(512), row(LANES),
                  pl.BlockSpec((1, 16, L), lambda b, c: (b, 0, c)),
                  full(conv_w), full(gain)],
        out_specs=row(512),
        out_shape=jax.ShapeDtypeStruct((n, MLSTM_V_WIDTH), BF16),
        scratch_shapes=[pltpu.VMEM((8, 512), F32), pltpu.VMEM((8 + L, 512), F32),
                        pltpu.VMEM((MLSTM_HEADS, LANES, 2 * LANES), F32),
                        pltpu.VMEM((MLSTM_HEADS, 8, LANES), F32)],
        compiler_params=_params(("parallel", "arbitrary"), 48),
    )(mqk, mv, mo, gcol, grow, conv_w, gain)


def _outproj_kernel(fo_ref, mo_ref, w_ref, x_ref, g_ref, b_ref, o_ref):
    mix = _dot(fo_ref[...], w_ref[0:FOX_WIDTH, :]) + _dot(mo_ref[...], w_ref[FOX_WIDTH:, :])
    o_ref[...] = _layer_norm(DEEPNORM_ALPHA * x_ref[...] + mix, g_ref[...], b_ref[...])


def _outproj(fo, mo, w_out, x2d, g, b):
    n = x2d.shape[0]
    tm = ROW_TILE
    row = lambda width: pl.BlockSpec((tm, width), lambda i: (i, 0))
    full = lambda a: pl.BlockSpec(a.shape, lambda i: (0, 0))
    return pl.pallas_call(
        _outproj_kernel,
        grid=(n // tm,),
        in_specs=[row(512), row(512), full(w_out), row(D_MODEL), full(g), full(b)],
        out_specs=row(D_MODEL),
        out_shape=jax.ShapeDtypeStruct((n, D_MODEL), F32),
        compiler_params=_params(("parallel",), 48),
    )(fo, mo, w_out, x2d, g, b)


def _memkv_kernel(mem_ref, wk_ref, wv_ref, k_ref, v_ref):
    mb = mem_ref[...].astype(BF16)
    k_ref[...] = (_dot(mb, wk_ref[...]) * (XATTN_HEAD_DIM ** -0.5)).astype(BF16)
    v_ref[...] = _dot(mb, wv_ref[...]).astype(BF16)


def _memkv(mem2d, wk, wv, n_mem):
    n = mem2d.shape[0]
    row = pl.BlockSpec((n_mem, D_MODEL), lambda i: (i, 0))
    full = lambda a: pl.BlockSpec(a.shape, lambda i: (0, 0))
    return pl.pallas_call(
        _memkv_kernel,
        grid=(n // n_mem,),
        in_specs=[row, full(wk), full(wv)],
        out_specs=(row, row),
        out_shape=(jax.ShapeDtypeStruct((n, D_MODEL), BF16),) * 2,
        compiler_params=_params(("parallel",), 32),
    )(mem2d, wk, wv)


def _xattn_kernel(x_ref, k_ref, v_ref, wq_ref, wo_ref, g_ref, b_ref, wr_ref, br_ref,
                  o_ref, ob_ref, lg_ref):
    half = x_ref.shape[0] // 2
    for r in range(2):
        rows = slice(r * half, (r + 1) * half)
        x = x_ref[rows, :]
        q = _dot(x.astype(BF16), wq_ref[...]).astype(BF16)
        outs = []
        for h in range(XATTN_HEADS):
            sl = slice(h * XATTN_HEAD_DIM, (h + 1) * XATTN_HEAD_DIM)
            s = _dot_nt(q[:, sl], k_ref[:, sl])
            p = jnp.exp(s - jnp.max(s, axis=-1, keepdims=True))
            l = jnp.sum(p, axis=-1, keepdims=True)
            outs.append((_dot(p.astype(BF16), v_ref[:, sl]) / l).astype(BF16))
        o = jnp.concatenate(outs, axis=1)
        xa = _dot(o, wo_ref[...])
        x2 = _layer_norm(DEEPNORM_ALPHA * x + xa, g_ref[...], b_ref[...])
        o_ref[rows, :] = x2
        packed = _pack_bf16_pairs(x2)
        for j in range(ob_ref.shape[0]):
            ob_ref[j, rows, :] = packed[:, j * LANES:(j + 1) * LANES]
        x2h = x2.astype(BF16)
        x2l = (x2 - x2h.astype(F32)).astype(BF16)
        a = _dot_nt(wr_ref[...], x2h)
        b = _dot_nt(wr_ref[0:N_EXPERTS, :], x2l)
        lg_ref[:, rows] = a[0:N_EXPERTS] + a[N_EXPERTS:] + b + br_ref[...]


def _xattn(x1, kmem, vmem, wq, wo, g, b, wr, br, batch, seq, n_mem):
    n = x1.shape[0]
    tm = ROW_TILE
    nt = seq // tm
    row = lambda width: pl.BlockSpec((tm, width), lambda bi, i: (bi * nt + i, 0))
    full = lambda a: pl.BlockSpec(a.shape, lambda bi, i: (0, 0))
    kv = pl.BlockSpec((n_mem, D_MODEL), lambda bi, i: (bi, 0))
    return pl.pallas_call(
        _xattn_kernel,
        grid=(batch, nt),
        in_specs=[row(D_MODEL), kv, kv, full(wq), full(wo), full(g), full(b), full(wr), full(br)],
        out_specs=(row(D_MODEL),
                   pl.BlockSpec((X_PLANES, tm, LANES), lambda bi, i: (0, bi * nt + i, 0)),
                   pl.BlockSpec((N_EXPERTS, tm), lambda bi, i: (0, bi * nt + i))),
        out_shape=(jax.ShapeDtypeStruct((n, D_MODEL), F32),
                   jax.ShapeDtypeStruct((X_PLANES, n, LANES), jnp.uint32),
                   jax.ShapeDtypeStruct((N_EXPERTS, n), F32)),
        compiler_params=_params(("parallel", "parallel"), 48),
    )(x1, kmem, vmem, wq, wo, g, b, wr, br)


def _route_kernel(lg_ref, idx_ref, rank_ref, gate_ref, cnt_ref, carry_ref):
    i = pl.program_id(0)

    @pl.when(i == 0)
    def _():
        carry_ref[...] = jnp.zeros_like(carry_ref)

    lg = lg_ref[...]
    t = lg.shape[1]
    e_idx = lax.broadcasted_iota(jnp.int32, lg.shape, 0).astype(F32)
    sels, vals, idxs = [], [], []
    for _ in range(TOP_K):
        mx = jnp.max(lg, axis=0, keepdims=True)
        first = jnp.min(jnp.where(lg == mx, e_idx, float(N_EXPERTS)), axis=0, keepdims=True)
        sel = e_idx == first
        sels.append(sel)
        vals.append(mx)
        idxs.append(first)
        lg = jnp.where(sel, -jnp.inf, lg)
    exps = [jnp.exp(v - vals[0]) for v in vals]
    tot = exps[0] + exps[1] + exps[2] + exps[3]

    selmat = (sels[0] | sels[1] | sels[2] | sels[3])
    r = lax.broadcasted_iota(jnp.int32, (t, t), 0)
    s = lax.broadcasted_iota(jnp.int32, (t, t), 1)
    earlier = (r < s).astype(BF16)
    carry = carry_ref[:, 0:1]
    rankmat = _dot(selmat.astype(BF16), earlier) + carry
    new_carry = carry + jnp.sum(selmat.astype(F32), axis=1, keepdims=True)
    carry_ref[...] = jnp.broadcast_to(new_carry, carry_ref.shape)
    cnt_ref[...] = jnp.broadcast_to(new_carry, cnt_ref.shape).astype(jnp.int32)

    row8 = lax.broadcasted_iota(jnp.int32, (8, t), 0)
    row128 = lax.broadcasted_iota(jnp.int32, (LANES, t), 0)
    idx_out = jnp.zeros((8, t), F32)
    rank_out = jnp.zeros((8, t), F32)
    gate_out = jnp.zeros((LANES, t), F32)
    for k in range(TOP_K):
        rk = jnp.sum(jnp.where(sels[k], rankmat, 0.0), axis=0, keepdims=True)
        idx_out = jnp.where(row8 == k, idxs[k], idx_out)
        rank_out = jnp.where(row8 == k, rk, rank_out)
        gate_out = jnp.where(row128 == k, exps[k] / tot, gate_out)
    idx_ref[...] = idx_out.astype(jnp.int32)
    rank_ref[...] = rank_out.astype(jnp.int32)
    gate_ref[...] = gate_out.T


def _route(logits_t):
    n = logits_t.shape[1]
    t = ROUTE_TILE
    col = lambda rows: pl.BlockSpec((rows, t), lambda i: (0, i))
    return pl.pallas_call(
        _route_kernel,
        grid=(n // t,),
        in_specs=[col(N_EXPERTS)],
        out_specs=(col(8), col(8), pl.BlockSpec((t, LANES), lambda i: (i, 0)),
                   pl.BlockSpec((N_EXPERTS, LANES), lambda i: (0, 0))),
        out_shape=(jax.ShapeDtypeStruct((8, n), jnp.int32),
                   jax.ShapeDtypeStruct((8, n), jnp.int32),
                   jax.ShapeDtypeStruct((n, LANES), F32),
                   jax.ShapeDtypeStruct((N_EXPERTS, LANES), jnp.int32)),
        scratch_shapes=[pltpu.VMEM((N_EXPERTS, LANES), F32)],
        compiler_params=_params(("arbitrary",), 32),
    )(logits_t)


def _expert_kernel(blk_e_ref, nb_ref, x_ref, wgu_ref, bgu_ref, wd_ref, bd_ref, y_ref,
                   wgu_sc, wd_sc, prev_ref):
    i = pl.program_id(0)
    e = blk_e_ref[i]

    @pl.when(i == 0)
    def _():
        prev_ref[0] = -1

    @pl.when(i < nb_ref[0])
    def _():
        @pl.when(e != prev_ref[0])
        def _():
            wgu_sc[...] = wgu_ref[0].astype(BF16)
            wd_sc[...] = wd_ref[0].astype(BF16)
            prev_ref[0] = e

        packed = jnp.concatenate([x_ref[j] for j in range(X_PLANES)], axis=1)
        xb = _unpack_bf16_pairs(packed)
        hids = []
        for c in range(D_EXPERT // EXPERT_CHUNK):
            g0 = c * EXPERT_CHUNK
            l0 = D_EXPERT + g0
            gate = _dot(xb, wgu_sc[:, g0:g0 + EXPERT_CHUNK]) + bgu_ref[0, :, g0:g0 + EXPERT_CHUNK]
            lin = _dot(xb, wgu_sc[:, l0:l0 + EXPERT_CHUNK]) + bgu_ref[0, :, l0:l0 + EXPERT_CHUNK]
            gate = jnp.minimum(gate, SWIGLU_LIMIT)
            lin = jnp.clip(lin, -SWIGLU_LIMIT, SWIGLU_LIMIT)
            hids.append((gate * jax.nn.sigmoid(SWIGLU_ALPHA * gate) * (lin + 1.0)).astype(BF16))
        y = _dot(jnp.concatenate(hids, axis=1), wd_sc[...]) + bd_ref[0]
        for j in range(Y_PLANES):
            y_ref[j] = y[:, j * LANES:(j + 1) * LANES]

    @pl.when(i >= nb_ref[0])
    def _():
        y_ref[...] = jnp.zeros_like(y_ref)


def _experts(blk_e, nb_used, xs, w_gu, b_gu, w_d, b_d):
    p = xs.shape[1]
    g = EXPERT_ROWS
    grid_spec = pltpu.PrefetchScalarGridSpec(
        num_scalar_prefetch=2,
        grid=(p // g,),
        in_specs=[pl.BlockSpec((X_PLANES, g, LANES), lambda i, be, nb: (0, i, 0)),
                  pl.BlockSpec((1, D_MODEL, 2 * D_EXPERT), lambda i, be, nb: (be[i], 0, 0)),
                  pl.BlockSpec((1, 1, 2 * D_EXPERT), lambda i, be, nb: (be[i], 0, 0)),
                  pl.BlockSpec((1, D_EXPERT, D_MODEL), lambda i, be, nb: (be[i], 0, 0)),
                  pl.BlockSpec((1, 1, D_MODEL), lambda i, be, nb: (be[i], 0, 0))],
        out_specs=pl.BlockSpec((Y_PLANES, g, LANES), lambda i, be, nb: (0, i, 0)),
        scratch_shapes=[pltpu.VMEM((D_MODEL, 2 * D_EXPERT), BF16),
                        pltpu.VMEM((D_EXPERT, D_MODEL), BF16),
                        pltpu.SMEM((1,), jnp.int32)],
    )
    return pl.pallas_call(
        _expert_kernel,
        grid_spec=grid_spec,
        out_shape=jax.ShapeDtypeStruct((Y_PLANES, p, LANES), F32),
        compiler_params=_params(("arbitrary",), 56),
    )(blk_e, nb_used, xs, w_gu, b_gu, w_d, b_d)


def _combine_kernel(y_ref, gate_ref, x_ref, g_ref, b_ref, o_ref):
    gate = gate_ref[...]
    ff = None
    for k in range(TOP_K):
        yk = jnp.concatenate([y_ref[k, j] for j in range(Y_PLANES)], axis=1) * gate[:, k:k + 1]
        ff = yk if ff is None else ff + yk
    o_ref[...] = _layer_norm(DEEPNORM_ALPHA * x_ref[...] + ff, g_ref[...], b_ref[...])


def _combine(yg, gate, x2, g, b):
    n = x2.shape[0]
    tm = SEQ_BLOCK
    row = lambda width: pl.BlockSpec((tm, width), lambda i: (i, 0))
    full = lambda a: pl.BlockSpec(a.shape, lambda i: (0, 0))
    return pl.pallas_call(
        _combine_kernel,
        grid=(n // tm,),
        in_specs=[pl.BlockSpec((TOP_K, Y_PLANES, tm, LANES), lambda i: (0, 0, i, 0)),
                  row(LANES), row(D_MODEL), full(g), full(b)],
        out_specs=row(D_MODEL),
        out_shape=jax.ShapeDtypeStruct((n, D_MODEL), F32),
        compiler_params=_params(("parallel",), 32),
    )(yg, gate, x2, g, b)


SC_WINDOW = 128


def _sc_mesh():
    return plsc.VectorSubcoreMesh(core_axis_name="core", subcore_axis_name="subcore")


def _sc_gather(table, idx):
    m = idx.shape[0]

    @pl.kernel(out_type=jax.ShapeDtypeStruct((m, LANES), table.dtype), mesh=_sc_mesh())
    def gather_kernel(table_hbm, idx_hbm, out_hbm):
        def body(idx_vmem, out_vmem):
            pltpu.sync_copy(table_hbm.at[idx_vmem.at[0]], out_vmem)

        pltpu.emit_pipeline(
            body,
            grid=(m // SC_WINDOW,),
            in_specs=[pl.BlockSpec((1, SC_WINDOW), lambda i: (0, i))],
            out_specs=[pl.BlockSpec((SC_WINDOW, LANES), lambda i: (i, 0))],
            core_axis_name=("core", "subcore"),
            dimension_semantics=(pltpu.PARALLEL,),
        )(idx_hbm, out_hbm)

    return gather_kernel(table, idx.reshape(1, m))


def _sc_scatter(src, idx, out_rows):
    m = idx.shape[0]
    src_blocks = src.shape[0] // SC_WINDOW

    @pl.kernel(out_type=jax.ShapeDtypeStruct((out_rows, LANES), src.dtype), mesh=_sc_mesh())
    def scatter_kernel(src_hbm, idx_hbm, out_hbm):
        def body(src_vmem, idx_vmem):
            pltpu.sync_copy(src_vmem, out_hbm.at[idx_vmem.at[0]])

        pltpu.emit_pipeline(
            body,
            grid=(m // SC_WINDOW,),
            in_specs=[pl.BlockSpec((SC_WINDOW, LANES), lambda i: (i % src_blocks, 0)),
                      pl.BlockSpec((1, SC_WINDOW), lambda i: (0, i))],
            out_specs=[],
            core_axis_name=("core", "subcore"),
            dimension_semantics=(pltpu.PARALLEL,),
        )(src_hbm, idx_hbm)

    return scatter_kernel(src, idx.reshape(1, m))


def _layer(x, mem, w_in, fox_f_bias, conv_w, i_bias, f_bias, fox_g, mlstm_g, w_mix_out,
           ln1_g, ln1_b, w_xq, w_xk, w_xv, w_xo, ln2_g, ln2_b, w_router, b_router,
           w_gate_up, b_gate_up, w_down, b_down, ln3_g, ln3_b):
    batch, seq, d = x.shape
    n_mem = mem.shape[1]
    n = batch * seq
    x2d = x.reshape(n, d)

    o_ff = 3 * FOX_WIDTH
    o_mqk = o_ff + FOX_HEADS
    o_mv = o_mqk + 2 * MLSTM_QK_WIDTH
    o_mi = o_mv + MLSTM_V_WIDTH
    o_mf = o_mi + MLSTM_HEADS
    o_mo = o_mf + MLSTM_HEADS
    n_gate = FOX_HEADS + 2 * MLSTM_HEADS
    w_r = jnp.concatenate(
        [w_in[:, FOX_WIDTH:2 * FOX_WIDTH], w_in[:, o_mqk:o_mi], w_in[:, o_mo:],
         w_in[:, o_ff:o_mqk], w_in[:, o_mi:o_mo],
         jnp.zeros((d, LANES - n_gate), w_in.dtype)], axis=1).astype(BF16)
    w_t = jnp.concatenate([w_in[:, :FOX_WIDTH], w_in[:, 2 * FOX_WIDTH:o_ff]], axis=1).T.astype(BF16)
    gate_bias = jnp.concatenate(
        [fox_f_bias, i_bias, f_bias, jnp.zeros((LANES - n_gate,), F32)]).reshape(1, LANES)

    fqt, fk, fvt, mqk, mv, mo, gates = _inproj(x2d, w_r, w_t, gate_bias)
    gcol, grow, cpieces = _gateprep(gates, batch, seq)
    fo = _fox(fqt, fk, cpieces, fvt, fox_g.reshape(1, FOX_WIDTH), batch, seq)
    mo_out = _mlstm(mqk, mv, mo, gcol, grow, conv_w, mlstm_g.reshape(1, MLSTM_V_WIDTH), batch, seq)
    x1 = _outproj(fo, mo_out, w_mix_out.astype(BF16), x2d, ln1_g.reshape(1, d), ln1_b.reshape(1, d))

    kmem, vmem = _memkv(mem.reshape(batch * n_mem, d), w_xk.astype(BF16), w_xv.astype(BF16), n_mem)
    wrt = w_router.T
    wrt_hi = wrt.astype(BF16)
    wrt_lo = (wrt - wrt_hi.astype(F32)).astype(BF16)
    x2, x2p, logits_t = _xattn(x1, kmem, vmem, w_xq.astype(BF16), w_xo.astype(BF16),
                               ln2_g.reshape(1, d), ln2_b.reshape(1, d),
                               jnp.concatenate([wrt_hi, wrt_lo], axis=0),
                               b_router.reshape(N_EXPERTS, 1), batch, seq, n_mem)

    idx_t, rank_t, gate, cnt = _route(logits_t)
    counts = cnt[:, 0]
    g_rows = EXPERT_ROWS
    padded = ((counts + g_rows - 1) // g_rows) * g_rows
    pad_end = jnp.cumsum(padded)
    pad_start = pad_end - padded
    experts = jnp.arange(N_EXPERTS, dtype=jnp.int32)
    sel = idx_t[:TOP_K, :, None] == experts[None, None, :]
    pos_t = jnp.sum(jnp.where(sel, pad_start[None, None, :], 0), axis=-1) + rank_t[:TOP_K]
    p_rows = n * TOP_K + N_EXPERTS * g_rows
    nb = p_rows // g_rows
    blk_start = jnp.arange(nb, dtype=jnp.int32) * g_rows
    blk_e = jnp.minimum(jnp.sum((pad_end[None, :] <= blk_start[:, None]).astype(jnp.int32), axis=1),
                        N_EXPERTS - 1)
    nb_used = (pad_end[-1:] // g_rows).astype(jnp.int32)

    def piece_index(planes):
        off = jnp.arange(planes, dtype=jnp.int32) * p_rows
        return (pos_t[:, None, :] + off[None, :, None]).reshape(-1)

    xs = _sc_scatter(x2p.reshape(X_PLANES * n, LANES), piece_index(X_PLANES), X_PLANES * p_rows)
    y = _experts(blk_e, nb_used, xs.reshape(X_PLANES, p_rows, LANES), w_gate_up,
                 b_gate_up.reshape(N_EXPERTS, 1, -1), w_down, b_down.reshape(N_EXPERTS, 1, -1))
    yg = _sc_gather(y.reshape(Y_PLANES * p_rows, LANES), piece_index(Y_PLANES))
    out = _combine(yg.reshape(TOP_K, Y_PLANES, n, LANES), gate, x2,
                   ln3_g.reshape(1, d), ln3_b.reshape(1, d))
    return out.reshape(batch, seq, d)


def kernel(x, mem, w_in, fox_f_bias, mlstm_conv_w, mlstm_i_bias, mlstm_f_bias, fox_norm_g, mlstm_norm_g, w_mix_out, ln1_g, ln1_b, w_xq, w_xk, w_xv, w_xo, ln2_g, ln2_b, w_router, b_router, w_gate_up, b_gate_up, w_down, b_down, ln3_g, ln3_b):
    for l in range(w_in.shape[0]):
        x = _layer(x, mem, w_in[l], fox_f_bias[l], mlstm_conv_w[l], mlstm_i_bias[l],
                   mlstm_f_bias[l], fox_norm_g[l], mlstm_norm_g[l], w_mix_out[l],
                   ln1_g[l], ln1_b[l], w_xq[l], w_xk[l], w_xv[l], w_xo[l], ln2_g[l], ln2_b[l],
                   w_router[l], b_router[l], w_gate_up[l], b_gate_up[l], w_down[l], b_down[l],
                   ln3_g[l], ln3_b[l])
    return x
```

```python
import jax
import jax.numpy as jnp
from jax import lax
from jax.experimental import pallas as pl
from jax.experimental.pallas import tpu as pltpu
from jax.experimental.pallas import tpu_sc as plsc

F32 = jnp.float32
BF16 = jnp.bfloat16

D_MODEL = 1024
FOX_HEADS = 8
FOX_HEAD_DIM = 64
FOX_WIDTH = FOX_HEADS * FOX_HEAD_DIM
MLSTM_HEADS = 4
MLSTM_QK_DIM = 64
MLSTM_V_DIM = 128
MLSTM_QK_WIDTH = MLSTM_HEADS * MLSTM_QK_DIM
MLSTM_V_WIDTH = MLSTM_HEADS * MLSTM_V_DIM
CONV_WIDTH = 4
XATTN_HEADS = 4
XATTN_HEAD_DIM = D_MODEL // XATTN_HEADS
N_EXPERTS = 32
TOP_K = 4
D_EXPERT = D_MODEL
SWIGLU_LIMIT = 7.0
SWIGLU_ALPHA = 1.702
DEEPNORM_ALPHA = 2.0 ** 0.25
LN_EPS = 1e-5
RMS_EPS = 1e-6

LANES = 128
SEQ_BLOCK = 256
ROW_TILE = 512
EXPERT_ROWS = 512
EXPERT_CHUNK = 256
ROUTE_TILE = 512
X_PLANES = D_MODEL // 2 // LANES
Y_PLANES = X_PLANES
GATE_I0 = FOX_HEADS
GATE_F0 = FOX_HEADS + MLSTM_HEADS

MIB = 1024 * 1024


def _params(semantics, vmem_mib):
    return pltpu.CompilerParams(dimension_semantics=semantics,
                                vmem_limit_bytes=vmem_mib * MIB)


def _layer_norm(y, g, b):
    mu = jnp.mean(y, axis=-1, keepdims=True)
    yc = y - mu
    var = jnp.mean(yc * yc, axis=-1, keepdims=True)
    return yc * lax.rsqrt(var + LN_EPS) * g + b


def _dot(a, b):
    return jnp.dot(a, b, preferred_element_type=F32)


def _dot_nt(a, b):
    return lax.dot_general(a, b, (((1,), (1,)), ((), ())), preferred_element_type=F32)


def _dot_tn(a, b):
    return lax.dot_general(a, b, (((0,), (0,)), ((), ())), preferred_element_type=F32)


def _pack_bf16_pairs(x):
    w = x.shape[1] // 2
    lo = pltpu.bitcast(x[:, :w].astype(BF16).astype(F32), jnp.uint32)
    hi = pltpu.bitcast(x[:, w:].astype(BF16).astype(F32), jnp.uint32)
    return (lo >> 16) | hi


def _unpack_bf16_pairs(u, dtype=BF16):
    lo = pltpu.bitcast(u << 16, F32).astype(dtype)
    hi = pltpu.bitcast(u & jnp.uint32(0xFFFF0000), F32).astype(dtype)
    return jnp.concatenate([lo, hi], axis=1)


def _inproj_kernel(x_ref, w_ref, wt_ref, gb_ref, fqt_ref, fk_ref, fvt_ref, mqk_ref, mv_ref, mo_ref,
                   g_ref):
    xb = x_ref[...].astype(BF16)

    def mm(c0, width):
        return _dot(xb, w_ref[:, c0:c0 + width])

    fqt_ref[...] = (_dot_nt(wt_ref[0:FOX_WIDTH, :], xb) * (FOX_HEAD_DIM ** -0.5)).astype(BF16)
    fvt_ref[...] = _dot_nt(wt_ref[FOX_WIDTH:, :], xb).astype(BF16)
    fk_ref[...] = mm(0, 512).astype(BF16)
    mqk_ref[...] = mm(512, 512)
    mv_ref[...] = mm(1024, 512).astype(BF16)
    mo_ref[...] = mm(1536, 512)
    g_ref[...] = mm(2048, LANES) + gb_ref[...]


def _inproj(x2d, w_r, w_t, gate_bias):
    n = x2d.shape[0]
    tm = ROW_TILE
    row = lambda width: pl.BlockSpec((tm, width), lambda i: (i, 0))
    col = pl.BlockSpec((FOX_WIDTH, tm), lambda i: (0, i))
    full = lambda a: pl.BlockSpec(a.shape, lambda i: (0, 0))
    out_shapes = (
        jax.ShapeDtypeStruct((FOX_WIDTH, n), BF16),
        jax.ShapeDtypeStruct((n, 512), BF16),
        jax.ShapeDtypeStruct((FOX_WIDTH, n), BF16),
        jax.ShapeDtypeStruct((n, 512), F32),
        jax.ShapeDtypeStruct((n, 512), BF16),
        jax.ShapeDtypeStruct((n, 512), F32),
        jax.ShapeDtypeStruct((n, LANES), F32),
    )
    return pl.pallas_call(
        _inproj_kernel,
        grid=(n // tm,),
        in_specs=[row(D_MODEL), full(w_r), full(w_t), full(gate_bias)],
        out_specs=(col, row(512), col, row(512), row(512), row(512), row(LANES)),
        out_shape=out_shapes,
        compiler_params=_params(("parallel",), 48),
    )(x2d, w_r, w_t, gate_bias)


def _split3(x):
    hi = x.astype(BF16)
    r1 = x - hi.astype(F32)
    mid = r1.astype(BF16)
    lo = (r1 - mid.astype(F32)).astype(BF16)
    return hi, mid, lo


def _gateprep_kernel(g_ref, sel_ref, col_ref, row_ref, cp_ref, carry_ref):
    c = pl.program_id(1)

    @pl.when(c == 0)
    def _():
        carry_ref[...] = jnp.zeros_like(carry_ref)

    g = g_ref[...]
    lane = lax.broadcasted_iota(jnp.int32, g.shape, 1)
    is_i = (lane >= GATE_I0) & (lane < GATE_F0)
    logsig = jnp.minimum(g, 0.0) - jnp.log1p(jnp.exp(-jnp.abs(g)))
    blk = g.shape[0]
    r = lax.broadcasted_iota(jnp.int32, (blk, blk), 0)
    s = lax.broadcasted_iota(jnp.int32, (blk, blk), 1)
    tri = (s <= r).astype(BF16)
    cs = None
    for piece in _split3(logsig):
        term = _dot(tri, piece)
        cs = term if cs is None else cs + term
    carry = carry_ref[0:1, :]
    glob = cs + carry
    carry_ref[...] = jnp.broadcast_to(glob[blk - 1:blk, :], carry_ref.shape)
    out = jnp.where(lane < GATE_I0, glob, jnp.where(is_i, g, cs))
    col_ref[...] = out
    row_ref[0] = out.T[0:16, :]

    pieces = jnp.concatenate(_split3(-glob), axis=1)
    moved = _dot(pieces, sel_ref[...])
    for p in range(FOX_HEADS // 2):
        cp_ref[0, p] = moved[:, p * LANES:(p + 1) * LANES].astype(BF16)


def _piece_selector():
    src = jnp.arange(3 * LANES, dtype=jnp.int32)[:, None]
    dst = jnp.arange(4 * LANES, dtype=jnp.int32)[None, :]
    piece, head = src // LANES, src % LANES
    pair, lane = dst // LANES, dst % LANES
    hit = (lane < 6) & (lane % 3 == piece) & (head == 2 * pair + lane // 3)
    return hit.astype(BF16)


def _gateprep(gates, batch, seq):
    n = gates.shape[0]
    nc = seq // SEQ_BLOCK
    sel = _piece_selector()
    return pl.pallas_call(
        _gateprep_kernel,
        grid=(batch, nc),
        in_specs=[pl.BlockSpec((SEQ_BLOCK, LANES), lambda b, c: (b * nc + c, 0)),
                  pl.BlockSpec(sel.shape, lambda b, c: (0, 0))],
        out_specs=(pl.BlockSpec((SEQ_BLOCK, LANES), lambda b, c: (b * nc + c, 0)),
                   pl.BlockSpec((1, 16, SEQ_BLOCK), lambda b, c: (b, 0, c)),
                   pl.BlockSpec((1, FOX_HEADS // 2, SEQ_BLOCK, LANES), lambda b, c: (b, 0, c, 0))),
        out_shape=(jax.ShapeDtypeStruct((n, LANES), F32),
                   jax.ShapeDtypeStruct((batch, 16, seq), F32),
                   jax.ShapeDtypeStruct((batch, FOX_HEADS // 2, seq, LANES), BF16)),
        scratch_shapes=[pltpu.VMEM((8, LANES), F32)],
        compiler_params=_params(("parallel", "arbitrary"), 32),
    )(gates, sel)


FOX_ONES_ROWS = 16
FOX_VAUG_ROWS = FOX_HEAD_DIM + FOX_ONES_ROWS
FOX_QUERY_TILE = 2 * SEQ_BLOCK


def _fox_kernel(qt_ref, k_ref, cp_ref, vt_ref, gain_ref, o_ref,
                kaug_sc, vaug_sc, qaug_sc, sta_sc, stb_sc, m_sc, acc0_sc, acc1_sc):
    qi = pl.program_id(2)
    tq = qt_ref.shape[1]
    tg = tq
    tk = SEQ_BLOCK
    hd = FOX_HEAD_DIM
    seq = k_ref.shape[0]

    @pl.when(qi == 0)
    def _():
        lane = lax.broadcasted_iota(jnp.int32, (tk, LANES), 1)

        def build(blk, carry):
            r0 = pl.multiple_of(blk * tk, tk)
            kp = k_ref[pl.ds(r0, tk), :].astype(F32)
            cp = cp_ref[0, 0, pl.ds(r0, tk), :].astype(F32)
            for h in range(2):
                kh = kp if h == 0 else pltpu.roll(kp, hd, axis=1)
                ch = pltpu.roll(cp, hd - 3 * h, axis=1)
                kaug = jnp.where(lane < hd, kh, jnp.where(lane < hd + 3, ch, 0.0))
                kaug_sc[h, pl.ds(r0, tk), :] = kaug.astype(BF16)
            return carry

        lax.fori_loop(0, seq // tk, build, 0)
        for h in range(2):
            vaug_sc[h, 0:hd, :] = vt_ref[h * hd:(h + 1) * hd, :]
            vaug_sc[h, hd:, :] = jnp.ones((FOX_ONES_ROWS, seq), BF16)

    ones3 = (lax.broadcasted_iota(jnp.int32, (FOX_ONES_ROWS, tq), 0) < 3).astype(BF16)
    for h in range(2):
        qaug_sc[h, 0:hd, :] = qt_ref[h * hd:(h + 1) * hd, :]
        qaug_sc[h, hd:hd + FOX_ONES_ROWS, :] = ones3
        qaug_sc[h, hd + FOX_ONES_ROWS:, :] = jnp.zeros((LANES - hd - FOX_ONES_ROWS, tq), BF16)

    def put_scores(kg, slot, masked=False):
        k0 = pl.multiple_of(kg * tg, tg)
        for h in range(2):
            st = _dot(kaug_sc[h, pl.ds(k0, tg), :], qaug_sc[h])
            if masked:
                kk = lax.broadcasted_iota(jnp.int32, (tg, tq), 0)
                tt = lax.broadcasted_iota(jnp.int32, (tg, tq), 1)
                st = jnp.where(kk <= tt, st, -jnp.inf)
            slot[h] = st

    m_sc[...] = jnp.full(m_sc.shape, -jnp.inf, F32)
    acc0_sc[...] = jnp.zeros(acc0_sc.shape, F32)
    acc1_sc[...] = jnp.zeros(acc1_sc.shape, F32)
    acc = (acc0_sc, acc1_sc)

    def absorb(kg, slot):
        k0 = pl.multiple_of(kg * tg, tg)
        half = tg // 2
        for h in range(2):
            st = slot[h]
            m_prev = m_sc[h]
            m_new = jnp.maximum(m_prev, jnp.max(st, axis=0, keepdims=True))
            pv = None
            for u in range(2):
                p = jnp.exp(st[u * half:(u + 1) * half] - m_new)
                part = _dot(vaug_sc[h, :, pl.ds(k0 + u * half, half)], p.astype(BF16))
                pv = part if pv is None else pv + part
            acc[h][...] = jnp.exp(m_prev - m_new) * acc[h][...] + pv
            m_sc[h] = m_new

    sa, sb = sta_sc, stb_sc
    n_loop = jnp.maximum(qi - 1, 0) // 2

    @pl.when(qi > 0)
    def _():
        put_scores(0, sa)

    def body(j, carry):
        put_scores(2 * j + 1, sb)
        absorb(2 * j, sa)
        put_scores(2 * j + 2, sa)
        absorb(2 * j + 1, sb)
        return carry

    lax.fori_loop(0, n_loop, body, 0)
    done = 2 * n_loop

    @pl.when(qi == 0)
    def _():
        put_scores(qi, sa, masked=True)
        absorb(qi, sa)

    @pl.when((qi > 0) & (qi - done == 1))
    def _():
        put_scores(qi, sb, masked=True)
        absorb(done, sa)
        absorb(qi, sb)

    @pl.when((qi > 0) & (qi - done == 2))
    def _():
        put_scores(done + 1, sb)
        absorb(done, sa)
        put_scores(qi, sa, masked=True)
        absorb(done + 1, sb)
        absorb(qi, sa)

    a0 = acc0_sc[...]
    a1 = acc1_sc[...]
    ot = jnp.concatenate([a0[0:hd] / a0[hd:hd + 1], a1[0:hd] / a1[hd:hd + 1]], axis=0)
    o = ot.T
    lane = lax.broadcasted_iota(jnp.int32, (tq, LANES), 1)
    lo = lane < FOX_HEAD_DIM
    sq = o * o
    ss0 = jnp.sum(jnp.where(lo, sq, 0.0), axis=-1, keepdims=True)
    ss1 = jnp.sum(jnp.where(lo, 0.0, sq), axis=-1, keepdims=True)
    inv = jnp.where(lo, lax.rsqrt(ss0 / FOX_HEAD_DIM + RMS_EPS),
                    lax.rsqrt(ss1 / FOX_HEAD_DIM + RMS_EPS))
    o_ref[...] = (o * inv * gain_ref[...]).astype(o_ref.dtype)


def _fox(fqt, fk, cpieces, fvt, fox_gain, batch, seq):
    n = fk.shape[0]
    tq = FOX_QUERY_TILE
    nq = seq // tq
    npair = FOX_HEADS // 2
    return pl.pallas_call(
        _fox_kernel,
        grid=(batch, npair, nq),
        in_specs=[pl.BlockSpec((LANES, tq), lambda b, hp, qi: (hp, b * nq + qi)),
                  pl.BlockSpec((seq, LANES), lambda b, hp, qi: (b, hp)),
                  pl.BlockSpec((1, 1, seq, LANES), lambda b, hp, qi: (b, hp, 0, 0)),
                  pl.BlockSpec((LANES, seq), lambda b, hp, qi: (hp, b)),
                  pl.BlockSpec((1, LANES), lambda b, hp, qi: (0, hp))],
        out_specs=pl.BlockSpec((tq, LANES), lambda b, hp, qi: (b * nq + qi, hp)),
        out_shape=jax.ShapeDtypeStruct((n, FOX_WIDTH), BF16),
        scratch_shapes=[pltpu.VMEM((2, seq, LANES), BF16),
                        pltpu.VMEM((2, FOX_VAUG_ROWS, seq), BF16),
                        pltpu.VMEM((2, LANES, tq), BF16),
                        pltpu.VMEM((2, tq, tq), F32),
                        pltpu.VMEM((2, tq, tq), F32),
                        pltpu.VMEM((2, 1, tq), F32),
                        pltpu.VMEM((FOX_VAUG_ROWS, tq), F32),
                        pltpu.VMEM((FOX_VAUG_ROWS, tq), F32)],
        compiler_params=_params(("parallel", "parallel", "arbitrary"), 48),
    )(fqt, fk, cpieces, fvt, fox_gain)


def _mlstm_kernel(qk_ref, v_ref, og_ref, col_ref, row_ref, cw_ref, gain_ref, o_ref,
                  tail_ref, buf_ref, c_sc, m_sc):
    c = pl.program_id(1)
    L = SEQ_BLOCK

    @pl.when(c == 0)
    def _():
        tail_ref[...] = jnp.zeros_like(tail_ref)
        c_sc[...] = jnp.zeros_like(c_sc)
        m_sc[...] = jnp.zeros_like(m_sc)

    x = qk_ref[...]
    buf_ref[0:8, :] = tail_ref[...]
    buf_ref[8:8 + L, :] = x
    tail_ref[...] = x[L - 8:L, :]
    y = x * cw_ref[CONV_WIDTH - 1:CONV_WIDTH, :]
    for j in range(CONV_WIDTH - 1):
        shift = CONV_WIDTH - 1 - j
        y = y + buf_ref[8 - shift:8 - shift + L, :] * cw_ref[j:j + 1, :]
    y = y * jax.nn.sigmoid(y)
    qc = y[:, :MLSTM_QK_WIDTH].astype(BF16)
    kf = y[:, MLSTM_QK_WIDTH:] * (MLSTM_QK_DIM ** -0.5)

    col = col_ref[...]
    row = row_ref[0]
    lane = lax.broadcasted_iota(jnp.int32, (L, LANES), 1)
    lo = lane < MLSTM_QK_DIM
    rr = lax.broadcasted_iota(jnp.int32, (L, L), 0)
    cc = lax.broadcasted_iota(jnp.int32, (L, L), 1)
    causal = cc <= rr
    ones_col = (lane == 0).astype(BF16)

    for h in range(MLSTM_HEADS):
        pair, half = divmod(h, 2)
        sel = lo if half == 0 else jnp.logical_not(lo)
        q2 = qc[:, pair * LANES:(pair + 1) * LANES]
        k2 = kf[:, pair * LANES:(pair + 1) * LANES]
        qh = jnp.where(sel, q2, jnp.zeros_like(q2))
        kb = k2.astype(BF16)
        bcol = col[:, GATE_F0 + h:GATE_F0 + h + 1]
        licol = col[:, GATE_I0 + h:GATE_I0 + h + 1]
        brow = row[GATE_F0 + h:GATE_F0 + h + 1, :]
        lirow = row[GATE_I0 + h:GATE_I0 + h + 1, :]
        g = bcol[L - 1:L, :]
        m_prev = m_sc[h][0:1, 0:1]

        d = jnp.where(causal, bcol + (lirow - brow), -jnp.inf)
        inter_log = bcol + m_prev
        m_t = jnp.maximum(inter_log, jnp.max(d, axis=-1, keepdims=True))
        w_inter = jnp.exp(inter_log - m_t)
        p = jnp.exp(d - m_t) * _dot_nt(qh, kb)
        vaug = jnp.concatenate([v_ref[:, h * LANES:(h + 1) * LANES], ones_col], axis=1)
        cstate = c_sc[h]
        tot = w_inter * _dot(qh, cstate.astype(BF16)) + _dot(p.astype(BF16), vaug)
        num = tot[:, :LANES]
        den = tot[:, LANES:LANES + 1]
        hout = num / jnp.maximum(jnp.abs(den), jnp.exp(-m_t))

        a = g - bcol + licol
        m_loc = jnp.max(a, axis=0, keepdims=True)
        wa = jnp.exp(a - m_loc)
        kw = jnp.where(sel, k2 * wa, 0.0).astype(BF16)
        kv = _dot_tn(kw, vaug)
        m_new = jnp.maximum(g + m_prev, m_loc)
        c_sc[h] = jnp.exp(g + m_prev - m_new) * cstate + jnp.exp(m_loc - m_new) * kv
        m_sc[h] = jnp.broadcast_to(m_new, m_sc.shape[1:])

        ms = jnp.mean(hout * hout, axis=-1, keepdims=True)
        hn = hout * lax.rsqrt(ms + RMS_EPS) * gain_ref[:, h * LANES:(h + 1) * LANES]
        gate = jax.nn.sigmoid(og_ref[:, h * LANES:(h + 1) * LANES])
        o_ref[:, h * LANES:(h + 1) * LANES] = (hn * gate).astype(o_ref.dtype)


def _mlstm(mqk, mv, mo, gcol, grow, conv_w, gain, batch, seq):
    n = mqk.shape[0]
    L = SEQ_BLOCK
    nc = seq // L
    row = lambda width: pl.BlockSpec((L, width), lambda b, c: (b * nc + c, 0))
    full = lambda a: pl.BlockSpec(a.shape, lambda b, c: (0, 0))
    return pl.pallas_call(
        _mlstm_kernel,
        grid=(batch, nc),
        in_specs=[row(512), row(512), row(512), row(LANES),
                  pl.BlockSpec((1, 16, L), lambda b, c: (b, 0, c)),
                  full(conv_w), full(gain)],
        out_specs=row(512),
        out_shape=jax.ShapeDtypeStruct((n, MLSTM_V_WIDTH), BF16),
        scratch_shapes=[pltpu.VMEM((8, 512), F32), pltpu.VMEM((8 + L, 512), F32),
                        pltpu.VMEM((MLSTM_HEADS, LANES, 2 * LANES), F32),
                        pltpu.VMEM((MLSTM_HEADS, 8, LANES), F32)],
        compiler_params=_params(("parallel", "arbitrary"), 48),
    )(mqk, mv, mo, gcol, grow, conv_w, gain)


def _outproj_kernel(fo_ref, mo_ref, w_ref, x_ref, g_ref, b_ref, o_ref):
    mix = _dot(fo_ref[...], w_ref[0:FOX_WIDTH, :]) + _dot(mo_ref[...], w_ref[FOX_WIDTH:, :])
    o_ref[...] = _layer_norm(DEEPNORM_ALPHA * x_ref[...] + mix, g_ref[...], b_ref[...])


def _outproj(fo, mo, w_out, x2d, g, b):
    n = x2d.shape[0]
    tm = ROW_TILE
    row = lambda width: pl.BlockSpec((tm, width), lambda i: (i, 0))
    full = lambda a: pl.BlockSpec(a.shape, lambda i: (0, 0))
    return pl.pallas_call(
        _outproj_kernel,
        grid=(n // tm,),
        in_specs=[row(512), row(512), full(w_out), row(D_MODEL), full(g), full(b)],
        out_specs=row(D_MODEL),
        out_shape=jax.ShapeDtypeStruct((n, D_MODEL), F32),
        compiler_params=_params(("parallel",), 48),
    )(fo, mo, w_out, x2d, g, b)


def _memkv_kernel(mem_ref, wk_ref, wv_ref, k_ref, v_ref):
    mb = mem_ref[...].astype(BF16)
    k_ref[...] = (_dot(mb, wk_ref[...]) * (XATTN_HEAD_DIM ** -0.5)).astype(BF16)
    v_ref[...] = _dot(mb, wv_ref[...]).astype(BF16)


def _memkv(mem2d, wk, wv, n_mem):
    n = mem2d.shape[0]
    row = pl.BlockSpec((n_mem, D_MODEL), lambda i: (i, 0))
    full = lambda a: pl.BlockSpec(a.shape, lambda i: (0, 0))
    return pl.pallas_call(
        _memkv_kernel,
        grid=(n // n_mem,),
        in_specs=[row, full(wk), full(wv)],
        out_specs=(row, row),
        out_shape=(jax.ShapeDtypeStruct((n, D_MODEL), BF16),) * 2,
        compiler_params=_params(("parallel",), 32),
    )(mem2d, wk, wv)


def _xattn_kernel(x_ref, k_ref, v_ref, wq_ref, wo_ref, g_ref, b_ref, wr_ref, br_ref,
                  o_ref, ob_ref, lg_ref):
    half = x_ref.shape[0] // 2
    for r in range(2):
        rows = slice(r * half, (r + 1) * half)
        x = x_ref[rows, :]
        q = _dot(x.astype(BF16), wq_ref[...]).astype(BF16)
        outs = []
        for h in range(XATTN_HEADS):
            sl = slice(h * XATTN_HEAD_DIM, (h + 1) * XATTN_HEAD_DIM)
            s = _dot_nt(q[:, sl], k_ref[:, sl])
            p = jnp.exp(s - jnp.max(s, axis=-1, keepdims=True))
            l = jnp.sum(p, axis=-1, keepdims=True)
            outs.append((_dot(p.astype(BF16), v_ref[:, sl]) / l).astype(BF16))
        o = jnp.concatenate(outs, axis=1)
        xa = _dot(o, wo_ref[...])
        x2 = _layer_norm(DEEPNORM_ALPHA * x + xa, g_ref[...], b_ref[...])
        o_ref[rows, :] = x2
        packed = _pack_bf16_pairs(x2)
        for j in range(ob_ref.shape[0]):
            ob_ref[j, rows, :] = packed[:, j * LANES:(j + 1) * LANES]
        x2h = x2.astype(BF16)
        x2l = (x2 - x2h.astype(F32)).astype(BF16)
        a = _dot_nt(wr_ref[...], x2h)
        b = _dot_nt(wr_ref[0:N_EXPERTS, :], x2l)
        lg_ref[:, rows] = a[0:N_EXPERTS] + a[N_EXPERTS:] + b + br_ref[...]


def _xattn(x1, kmem, vmem, wq, wo, g, b, wr, br, batch, seq, n_mem):
    n = x1.shape[0]
    tm = ROW_TILE
    nt = seq // tm
    row = lambda width: pl.BlockSpec((tm, width), lambda bi, i: (bi * nt + i, 0))
    full = lambda a: pl.BlockSpec(a.shape, lambda bi, i: (0, 0))
    kv = pl.BlockSpec((n_mem, D_MODEL), lambda bi, i: (bi, 0))
    return pl.pallas_call(
        _xattn_kernel,
        grid=(batch, nt),
        in_specs=[row(D_MODEL), kv, kv, full(wq), full(wo), full(g), full(b), full(wr), full(br)],
        out_specs=(row(D_MODEL),
                   pl.BlockSpec((X_PLANES, tm, LANES), lambda bi, i: (0, bi * nt + i, 0)),
                   pl.BlockSpec((N_EXPERTS, tm), lambda bi, i: (0, bi * nt + i))),
        out_shape=(jax.ShapeDtypeStruct((n, D_MODEL), F32),
                   jax.ShapeDtypeStruct((X_PLANES, n, LANES), jnp.uint32),
                   jax.ShapeDtypeStruct((N_EXPERTS, n), F32)),
        compiler_params=_params(("parallel", "parallel"), 48),
    )(x1, kmem, vmem, wq, wo, g, b, wr, br)


def _route_kernel(lg_ref, idx_ref, rank_ref, gate_ref, cnt_ref, carry_ref):
    i = pl.program_id(0)

    @pl.when(i == 0)
    def _():
        carry_ref[...] = jnp.zeros_like(carry_ref)

    lg = lg_ref[...]
    t = lg.shape[1]
    e_idx = lax.broadcasted_iota(jnp.int32, lg.shape, 0).astype(F32)
    sels, vals, idxs = [], [], []
    for _ in range(TOP_K):
        mx = jnp.max(lg, axis=0, keepdims=True)
        first = jnp.min(jnp.where(lg == mx, e_idx, float(N_EXPERTS)), axis=0, keepdims=True)
        sel = e_idx == first
        sels.append(sel)
        vals.append(mx)
        idxs.append(first)
        lg = jnp.where(sel, -jnp.inf, lg)
    exps = [jnp.exp(v - vals[0]) for v in vals]
    tot = exps[0] + exps[1] + exps[2] + exps[3]

    selmat = (sels[0] | sels[1] | sels[2] | sels[3])
    r = lax.broadcasted_iota(jnp.int32, (t, t), 0)
    s = lax.broadcasted_iota(jnp.int32, (t, t), 1)
    earlier = (r < s).astype(BF16)
    carry = carry_ref[:, 0:1]
    rankmat = _dot(selmat.astype(BF16), earlier) + carry
    new_carry = carry + jnp.sum(selmat.astype(F32), axis=1, keepdims=True)
    carry_ref[...] = jnp.broadcast_to(new_carry, carry_ref.shape)
    cnt_ref[...] = jnp.broadcast_to(new_carry, cnt_ref.shape).astype(jnp.int32)

    row8 = lax.broadcasted_iota(jnp.int32, (8, t), 0)
    row128 = lax.broadcasted_iota(jnp.int32, (LANES, t), 0)
    idx_out = jnp.zeros((8, t), F32)
    rank_out = jnp.zeros((8, t), F32)
    gate_out = jnp.zeros((LANES, t), F32)
    for k in range(TOP_K):
        rk = jnp.sum(jnp.where(sels[k], rankmat, 0.0), axis=0, keepdims=True)
        idx_out = jnp.where(row8 == k, idxs[k], idx_out)
        rank_out = jnp.where(row8 == k, rk, rank_out)
        gate_out = jnp.where(row128 == k, exps[k] / tot, gate_out)
    idx_ref[...] = idx_out.astype(jnp.int32)
    rank_ref[...] = rank_out.astype(jnp.int32)
    gate_ref[...] = gate_out.T


def _route(logits_t):
    n = logits_t.shape[1]
    t = ROUTE_TILE
    col = lambda rows: pl.BlockSpec((rows, t), lambda i: (0, i))
    return pl.pallas_call(
        _route_kernel,
        grid=(n // t,),
        in_specs=[col(N_EXPERTS)],
        out_specs=(col(8), col(8), pl.BlockSpec((t, LANES), lambda i: (i, 0)),
                   pl.BlockSpec((N_EXPERTS, LANES), lambda i: (0, 0))),
        out_shape=(jax.ShapeDtypeStruct((8, n), jnp.int32),
                   jax.ShapeDtypeStruct((8, n), jnp.int32),
                   jax.ShapeDtypeStruct((n, LANES), F32),
                   jax.ShapeDtypeStruct((N_EXPERTS, LANES), jnp.int32)),
        scratch_shapes=[pltpu.VMEM((N_EXPERTS, LANES), F32)],
        compiler_params=_params(("arbitrary",), 32),
    )(logits_t)


def _expert_kernel(blk_e_ref, nb_ref, x_ref, wgu_ref, bgu_ref, wd_ref, bd_ref, y_ref,
                   wgu_sc, wd_sc, prev_ref):
    i = pl.program_id(0)
    e = blk_e_ref[i]

    @pl.when(i == 0)
    def _():
        prev_ref[0] = -1

    @pl.when(i < nb_ref[0])
    def _():
        @pl.when(e != prev_ref[0])
        def _():
            wgu_sc[...] = wgu_ref[0].astype(BF16)
            wd_sc[...] = wd_ref[0].astype(BF16)
            prev_ref[0] = e

        packed = jnp.concatenate([x_ref[j] for j in range(X_PLANES)], axis=1)
        xb = _unpack_bf16_pairs(packed)
        hids = []
        for c in range(D_EXPERT // EXPERT_CHUNK):
            g0 = c * EXPERT_CHUNK
            l0 = D_EXPERT + g0
            gate = _dot(xb, wgu_sc[:, g0:g0 + EXPERT_CHUNK]) + bgu_ref[0, :, g0:g0 + EXPERT_CHUNK]
            lin = _dot(xb, wgu_sc[:, l0:l0 + EXPERT_CHUNK]) + bgu_ref[0, :, l0:l0 + EXPERT_CHUNK]
            gate = jnp.minimum(gate, SWIGLU_LIMIT)
            lin = jnp.clip(lin, -SWIGLU_LIMIT, SWIGLU_LIMIT)
            hids.append((gate * jax.nn.sigmoid(SWIGLU_ALPHA * gate) * (lin + 1.0)).astype(BF16))
        y = _pack_bf16_pairs(_dot(jnp.concatenate(hids, axis=1), wd_sc[...]) + bd_ref[0])
        for j in range(Y_PLANES):
            y_ref[j] = y[:, j * LANES:(j + 1) * LANES]

    @pl.when(i >= nb_ref[0])
    def _():
        y_ref[...] = jnp.zeros_like(y_ref)


def _experts(blk_e, nb_used, xs, w_gu, b_gu, w_d, b_d):
    p = xs.shape[1]
    g = EXPERT_ROWS
    grid_spec = pltpu.PrefetchScalarGridSpec(
        num_scalar_prefetch=2,
        grid=(p // g,),
        in_specs=[pl.BlockSpec((X_PLANES, g, LANES), lambda i, be, nb: (0, i, 0)),
                  pl.BlockSpec((1, D_MODEL, 2 * D_EXPERT), lambda i, be, nb: (be[i], 0, 0)),
                  pl.BlockSpec((1, 1, 2 * D_EXPERT), lambda i, be, nb: (be[i], 0, 0)),
                  pl.BlockSpec((1, D_EXPERT, D_MODEL), lambda i, be, nb: (be[i], 0, 0)),
                  pl.BlockSpec((1, 1, D_MODEL), lambda i, be, nb: (be[i], 0, 0))],
        out_specs=pl.BlockSpec((Y_PLANES, g, LANES), lambda i, be, nb: (0, i, 0)),
        scratch_shapes=[pltpu.VMEM((D_MODEL, 2 * D_EXPERT), BF16),
                        pltpu.VMEM((D_EXPERT, D_MODEL), BF16),
                        pltpu.SMEM((1,), jnp.int32)],
    )
    return pl.pallas_call(
        _expert_kernel,
        grid_spec=grid_spec,
        out_shape=jax.ShapeDtypeStruct((Y_PLANES, p, LANES), jnp.uint32),
        compiler_params=_params(("arbitrary",), 56),
    )(blk_e, nb_used, xs, w_gu, b_gu, w_d, b_d)


def _combine_kernel(y_ref, gate_ref, x_ref, g_ref, b_ref, o_ref):
    gate = gate_ref[...]
    ff = None
    for k in range(TOP_K):
        packed = jnp.concatenate([y_ref[k, j] for j in range(Y_PLANES)], axis=1)
        yk = _unpack_bf16_pairs(packed, F32) * gate[:, k:k + 1]
        ff = yk if ff is None else ff + yk
    o_ref[...] = _layer_norm(DEEPNORM_ALPHA * x_ref[...] + ff, g_ref[...], b_ref[...])


def _combine(yg, gate, x2, g, b):
    n = x2.shape[0]
    tm = SEQ_BLOCK
    row = lambda width: pl.BlockSpec((tm, width), lambda i: (i, 0))
    full = lambda a: pl.BlockSpec(a.shape, lambda i: (0, 0))
    return pl.pallas_call(
        _combine_kernel,
        grid=(n // tm,),
        in_specs=[pl.BlockSpec((TOP_K, Y_PLANES, tm, LANES), lambda i: (0, 0, i, 0)),
                  row(LANES), row(D_MODEL), full(g), full(b)],
        out_specs=row(D_MODEL),
        out_shape=jax.ShapeDtypeStruct((n, D_MODEL), F32),
        compiler_params=_params(("parallel",), 32),
    )(yg, gate, x2, g, b)


SC_WINDOW = 128


def _sc_mesh():
    return plsc.VectorSubcoreMesh(core_axis_name="core", subcore_axis_name="subcore")


def _sc_gather(table, idx):
    m = idx.shape[0]

    @pl.kernel(out_type=jax.ShapeDtypeStruct((m, LANES), table.dtype), mesh=_sc_mesh())
    def gather_kernel(table_hbm, idx_hbm, out_hbm):
        def body(idx_vmem, out_vmem):
            pltpu.sync_copy(table_hbm.at[idx_vmem.at[0]], out_vmem)

        pltpu.emit_pipeline(
            body,
            grid=(m // SC_WINDOW,),
            in_specs=[pl.BlockSpec((1, SC_WINDOW), lambda i: (0, i))],
            out_specs=[pl.BlockSpec((SC_WINDOW, LANES), lambda i: (i, 0))],
            core_axis_name=("core", "subcore"),
            dimension_semantics=(pltpu.PARALLEL,),
        )(idx_hbm, out_hbm)

    return gather_kernel(table, idx.reshape(1, m))


def _sc_scatter(src, idx, out_rows):
    m = idx.shape[0]
    src_blocks = src.shape[0] // SC_WINDOW

    @pl.kernel(out_type=jax.ShapeDtypeStruct((out_rows, LANES), src.dtype), mesh=_sc_mesh())
    def scatter_kernel(src_hbm, idx_hbm, out_hbm):
        def body(src_vmem, idx_vmem):
            pltpu.sync_copy(src_vmem, out_hbm.at[idx_vmem.at[0]])

        pltpu.emit_pipeline(
            body,
            grid=(m // SC_WINDOW,),
            in_specs=[pl.BlockSpec((SC_WINDOW, LANES), lambda i: (i % src_blocks, 0)),
                      pl.BlockSpec((1, SC_WINDOW), lambda i: (0, i))],
            out_specs=[],
            core_axis_name=("core", "subcore"),
            dimension_semantics=(pltpu.PARALLEL,),
        )(src_hbm, idx_hbm)

    return scatter_kernel(src, idx.reshape(1, m))


def _layer(x, mem, w_in, fox_f_bias, conv_w, i_bias, f_bias, fox_g, mlstm_g, w_mix_out,
           ln1_g, ln1_b, w_xq, w_xk, w_xv, w_xo, ln2_g, ln2_b, w_router, b_router,
           w_gate_up, b_gate_up, w_down, b_down, ln3_g, ln3_b):
    batch, seq, d = x.shape
    n_mem = mem.shape[1]
    n = batch * seq
    x2d = x.reshape(n, d)

    o_ff = 3 * FOX_WIDTH
    o_mqk = o_ff + FOX_HEADS
    o_mv = o_mqk + 2 * MLSTM_QK_WIDTH
    o_mi = o_mv + MLSTM_V_WIDTH
    o_mf = o_mi + MLSTM_HEADS
    o_mo = o_mf + MLSTM_HEADS
    n_gate = FOX_HEADS + 2 * MLSTM_HEADS
    w_r = jnp.concatenate(
        [w_in[:, FOX_WIDTH:2 * FOX_WIDTH], w_in[:, o_mqk:o_mi], w_in[:, o_mo:],
         w_in[:, o_ff:o_mqk], w_in[:, o_mi:o_mo],
         jnp.zeros((d, LANES - n_gate), w_in.dtype)], axis=1).astype(BF16)
    w_t = jnp.concatenate([w_in[:, :FOX_WIDTH], w_in[:, 2 * FOX_WIDTH:o_ff]], axis=1).T.astype(BF16)
    gate_bias = jnp.concatenate(
        [fox_f_bias, i_bias, f_bias, jnp.zeros((LANES - n_gate,), F32)]).reshape(1, LANES)

    fqt, fk, fvt, mqk, mv, mo, gates = _inproj(x2d, w_r, w_t, gate_bias)
    gcol, grow, cpieces = _gateprep(gates, batch, seq)
    fo = _fox(fqt, fk, cpieces, fvt, fox_g.reshape(1, FOX_WIDTH), batch, seq)
    mo_out = _mlstm(mqk, mv, mo, gcol, grow, conv_w, mlstm_g.reshape(1, MLSTM_V_WIDTH), batch, seq)
    x1 = _outproj(fo, mo_out, w_mix_out.astype(BF16), x2d, ln1_g.reshape(1, d), ln1_b.reshape(1, d))

    kmem, vmem = _memkv(mem.reshape(batch * n_mem, d), w_xk.astype(BF16), w_xv.astype(BF16), n_mem)
    wrt = w_router.T
    wrt_hi = wrt.astype(BF16)
    wrt_lo = (wrt - wrt_hi.astype(F32)).astype(BF16)
    x2, x2p, logits_t = _xattn(x1, kmem, vmem, w_xq.astype(BF16), w_xo.astype(BF16),
                               ln2_g.reshape(1, d), ln2_b.reshape(1, d),
                               jnp.concatenate([wrt_hi, wrt_lo], axis=0),
                               b_router.reshape(N_EXPERTS, 1), batch, seq, n_mem)

    idx_t, rank_t, gate, cnt = _route(logits_t)
    counts = cnt[:, 0]
    g_rows = EXPERT_ROWS
    padded = ((counts + g_rows - 1) // g_rows) * g_rows
    pad_end = jnp.cumsum(padded)
    pad_start = pad_end - padded
    experts = jnp.arange(N_EXPERTS, dtype=jnp.int32)
    sel = idx_t[:TOP_K, :, None] == experts[None, None, :]
    pos_t = jnp.sum(jnp.where(sel, pad_start[None, None, :], 0), axis=-1) + rank_t[:TOP_K]
    p_rows = n * TOP_K + N_EXPERTS * g_rows
    nb = p_rows // g_rows
    blk_start = jnp.arange(nb, dtype=jnp.int32) * g_rows
    blk_e = jnp.minimum(jnp.sum((pad_end[None, :] <= blk_start[:, None]).astype(jnp.int32), axis=1),
                        N_EXPERTS - 1)
    nb_used = (pad_end[-1:] // g_rows).astype(jnp.int32)

    def piece_index(planes):
        off = jnp.arange(planes, dtype=jnp.int32) * p_rows
        return (pos_t[:, None, :] + off[None, :, None]).reshape(-1)

    xs = _sc_scatter(x2p.reshape(X_PLANES * n, LANES), piece_index(X_PLANES), X_PLANES * p_rows)
    y = _experts(blk_e, nb_used, xs.reshape(X_PLANES, p_rows, LANES), w_gate_up,
                 b_gate_up.reshape(N_EXPERTS, 1, -1), w_down, b_down.reshape(N_EXPERTS, 1, -1))
    yg = _sc_gather(y.reshape(Y_PLANES * p_rows, LANES), piece_index(Y_PLANES))
    out = _combine(yg.reshape(TOP_K, Y_PLANES, n, LANES), gate, x2,
                   ln3_g.reshape(1, d), ln3_b.reshape(1, d))
    return out.reshape(batch, seq, d)


def kernel(x, mem, w_in, fox_f_bias, mlstm_conv_w, mlstm_i_bias, mlstm_f_bias, fox_norm_g, mlstm_norm_g, w_mix_out, ln1_g, ln1_b, w_xq, w_xk, w_xv, w_xo, ln2_g, ln2_b, w_router, b_router, w_gate_up, b_gate_up, w_down, b_down, ln3_g, ln3_b):
    for l in range(w_in.shape[0]):
        x = _layer(x, mem, w_in[l], fox_f_bias[l], mlstm_conv_w[l], mlstm_i_bias[l],
                   mlstm_f_bias[l], fox_norm_g[l], mlstm_norm_g[l], w_mix_out[l],
                   ln1_g[l], ln1_b[l], w_xq[l], w_xk[l], w_xv[l], w_xo[l], ln2_g[l], ln2_b[l],
                   w_router[l], b_router[l], w_gate_up[l], b_gate_up[l], w_down[l], b_down[l],
                   ln3_g[l], ln3_b[l])
    return x
```

```python
import jax
import jax.numpy as jnp
from jax import lax
from jax.experimental import pallas as pl
from jax.experimental.pallas import tpu as pltpu
from jax.experimental.pallas import tpu_sc as plsc

F32 = jnp.float32
BF16 = jnp.bfloat16

D_MODEL = 1024
FOX_HEADS = 8
FOX_HEAD_DIM = 64
FOX_WIDTH = FOX_HEADS * FOX_HEAD_DIM
MLSTM_HEADS = 4
MLSTM_QK_DIM = 64
MLSTM_V_DIM = 128
MLSTM_QK_WIDTH = MLSTM_HEADS * MLSTM_QK_DIM
MLSTM_V_WIDTH = MLSTM_HEADS * MLSTM_V_DIM
CONV_WIDTH = 4
XATTN_HEADS = 4
XATTN_HEAD_DIM = D_MODEL // XATTN_HEADS
N_EXPERTS = 32
TOP_K = 4
D_EXPERT = D_MODEL
SWIGLU_LIMIT = 7.0
SWIGLU_ALPHA = 1.702
DEEPNORM_ALPHA = 2.0 ** 0.25
LN_EPS = 1e-5
RMS_EPS = 1e-6

LANES = 128
SEQ_BLOCK = 256
ROW_TILE = 512
EXPERT_ROWS = 512
EXPERT_CHUNK = 256
ROUTE_TILE = 512
X_PLANES = D_MODEL // 2 // LANES
Y_PLANES = X_PLANES
GATE_I0 = FOX_HEADS
GATE_F0 = FOX_HEADS + MLSTM_HEADS

MIB = 1024 * 1024


def _params(semantics, vmem_mib):
    return pltpu.CompilerParams(dimension_semantics=semantics,
                                vmem_limit_bytes=vmem_mib * MIB)


def _layer_norm(y, g, b):
    mu = jnp.mean(y, axis=-1, keepdims=True)
    yc = y - mu
    var = jnp.mean(yc * yc, axis=-1, keepdims=True)
    return yc * lax.rsqrt(var + LN_EPS) * g + b


def _dot(a, b):
    return jnp.dot(a, b, preferred_element_type=F32)


def _dot_nt(a, b):
    return lax.dot_general(a, b, (((1,), (1,)), ((), ())), preferred_element_type=F32)


def _dot_tn(a, b):
    return lax.dot_general(a, b, (((0,), (0,)), ((), ())), preferred_element_type=F32)


def _pack_bf16_pairs(x):
    w = x.shape[1] // 2
    lo = pltpu.bitcast(x[:, :w].astype(BF16).astype(F32), jnp.uint32)
    hi = pltpu.bitcast(x[:, w:].astype(BF16).astype(F32), jnp.uint32)
    return (lo >> 16) | hi


def _unpack_bf16_pairs(u, dtype=BF16):
    lo = pltpu.bitcast(u << 16, F32).astype(dtype)
    hi = pltpu.bitcast(u & jnp.uint32(0xFFFF0000), F32).astype(dtype)
    return jnp.concatenate([lo, hi], axis=1)


def _inproj_kernel(x_ref, w_ref, wt_ref, gb_ref, fqt_ref, fk_ref, fvt_ref, mqk_ref, mv_ref, mo_ref,
                   g_ref):
    xb = x_ref[...].astype(BF16)

    def mm(c0, width):
        return _dot(xb, w_ref[:, c0:c0 + width])

    fqt_ref[...] = (_dot_nt(wt_ref[0:FOX_WIDTH, :], xb) * (FOX_HEAD_DIM ** -0.5)).astype(BF16)
    fvt_ref[...] = _dot_nt(wt_ref[FOX_WIDTH:, :], xb).astype(BF16)
    fk_ref[...] = mm(0, 512).astype(BF16)
    mqk_ref[...] = mm(512, 512)
    mv_ref[...] = mm(1024, 512).astype(BF16)
    mo_ref[...] = mm(1536, 512)
    g_ref[...] = mm(2048, LANES) + gb_ref[...]


def _inproj(x2d, w_r, w_t, gate_bias):
    n = x2d.shape[0]
    tm = ROW_TILE
    row = lambda width: pl.BlockSpec((tm, width), lambda i: (i, 0))
    col = pl.BlockSpec((FOX_WIDTH, tm), lambda i: (0, i))
    full = lambda a: pl.BlockSpec(a.shape, lambda i: (0, 0))
    out_shapes = (
        jax.ShapeDtypeStruct((FOX_WIDTH, n), BF16),
        jax.ShapeDtypeStruct((n, 512), BF16),
        jax.ShapeDtypeStruct((FOX_WIDTH, n), BF16),
        jax.ShapeDtypeStruct((n, 512), F32),
        jax.ShapeDtypeStruct((n, 512), BF16),
        jax.ShapeDtypeStruct((n, 512), F32),
        jax.ShapeDtypeStruct((n, LANES), F32),
    )
    return pl.pallas_call(
        _inproj_kernel,
        grid=(n // tm,),
        in_specs=[row(D_MODEL), full(w_r), full(w_t), full(gate_bias)],
        out_specs=(col, row(512), col, row(512), row(512), row(512), row(LANES)),
        out_shape=out_shapes,
        compiler_params=_params(("parallel",), 48),
    )(x2d, w_r, w_t, gate_bias)


def _split3(x):
    hi = x.astype(BF16)
    r1 = x - hi.astype(F32)
    mid = r1.astype(BF16)
    lo = (r1 - mid.astype(F32)).astype(BF16)
    return hi, mid, lo


def _gateprep_kernel(g_ref, sel_ref, col_ref, row_ref, cp_ref, carry_ref):
    c = pl.program_id(1)

    @pl.when(c == 0)
    def _():
        carry_ref[...] = jnp.zeros_like(carry_ref)

    g = g_ref[...]
    lane = lax.broadcasted_iota(jnp.int32, g.shape, 1)
    is_i = (lane >= GATE_I0) & (lane < GATE_F0)
    logsig = jnp.minimum(g, 0.0) - jnp.log1p(jnp.exp(-jnp.abs(g)))
    blk = g.shape[0]
    r = lax.broadcasted_iota(jnp.int32, (blk, blk), 0)
    s = lax.broadcasted_iota(jnp.int32, (blk, blk), 1)
    tri = (s <= r).astype(BF16)
    cs = None
    for piece in _split3(logsig):
        term = _dot(tri, piece)
        cs = term if cs is None else cs + term
    carry = carry_ref[0:1, :]
    glob = cs + carry
    carry_ref[...] = jnp.broadcast_to(glob[blk - 1:blk, :], carry_ref.shape)
    out = jnp.where(lane < GATE_I0, glob, jnp.where(is_i, g, cs))
    col_ref[...] = out
    row_ref[0] = out.T[0:16, :]

    pieces = jnp.concatenate(_split3(-glob), axis=1)
    moved = _dot(pieces, sel_ref[...])
    for p in range(FOX_HEADS // 2):
        cp_ref[0, p] = moved[:, p * LANES:(p + 1) * LANES].astype(BF16)


def _piece_selector():
    src = jnp.arange(3 * LANES, dtype=jnp.int32)[:, None]
    dst = jnp.arange(4 * LANES, dtype=jnp.int32)[None, :]
    piece, head = src // LANES, src % LANES
    pair, lane = dst // LANES, dst % LANES
    hit = (lane < 6) & (lane % 3 == piece) & (head == 2 * pair + lane // 3)
    return hit.astype(BF16)


def _gateprep(gates, batch, seq):
    n = gates.shape[0]
    nc = seq // SEQ_BLOCK
    sel = _piece_selector()
    return pl.pallas_call(
        _gateprep_kernel,
        grid=(batch, nc),
        in_specs=[pl.BlockSpec((SEQ_BLOCK, LANES), lambda b, c: (b * nc + c, 0)),
                  pl.BlockSpec(sel.shape, lambda b, c: (0, 0))],
        out_specs=(pl.BlockSpec((SEQ_BLOCK, LANES), lambda b, c: (b * nc + c, 0)),
                   pl.BlockSpec((1, 16, SEQ_BLOCK), lambda b, c: (b, 0, c)),
                   pl.BlockSpec((1, FOX_HEADS // 2, SEQ_BLOCK, LANES), lambda b, c: (b, 0, c, 0))),
        out_shape=(jax.ShapeDtypeStruct((n, LANES), F32),
                   jax.ShapeDtypeStruct((batch, 16, seq), F32),
                   jax.ShapeDtypeStruct((batch, FOX_HEADS // 2, seq, LANES), BF16)),
        scratch_shapes=[pltpu.VMEM((8, LANES), F32)],
        compiler_params=_params(("parallel", "arbitrary"), 32),
    )(gates, sel)


FOX_ONES_ROWS = 16
FOX_VAUG_ROWS = FOX_HEAD_DIM + FOX_ONES_ROWS
FOX_QUERY_TILE = 2 * SEQ_BLOCK


def _fox_kernel(qt_ref, k_ref, cp_ref, vt_ref, gain_ref, o_ref,
                kaug_sc, vaug_sc, qaug_sc, sta_sc, stb_sc, m_sc, acc0_sc, acc1_sc):
    qi = pl.program_id(2)
    tq = qt_ref.shape[1]
    tg = tq
    tk = SEQ_BLOCK
    hd = FOX_HEAD_DIM
    seq = k_ref.shape[0]

    @pl.when(qi == 0)
    def _():
        lane = lax.broadcasted_iota(jnp.int32, (tk, LANES), 1)

        def build(blk, carry):
            r0 = pl.multiple_of(blk * tk, tk)
            kp = k_ref[pl.ds(r0, tk), :].astype(F32)
            cp = cp_ref[0, 0, pl.ds(r0, tk), :].astype(F32)
            for h in range(2):
                kh = kp if h == 0 else pltpu.roll(kp, hd, axis=1)
                ch = pltpu.roll(cp, hd - 3 * h, axis=1)
                kaug = jnp.where(lane < hd, kh, jnp.where(lane < hd + 3, ch, 0.0))
                kaug_sc[h, pl.ds(r0, tk), :] = kaug.astype(BF16)
            return carry

        lax.fori_loop(0, seq // tk, build, 0)
        for h in range(2):
            vaug_sc[h, 0:hd, :] = vt_ref[h * hd:(h + 1) * hd, :]
            vaug_sc[h, hd:, :] = jnp.ones((FOX_ONES_ROWS, seq), BF16)

    ones3 = (lax.broadcasted_iota(jnp.int32, (FOX_ONES_ROWS, tq), 0) < 3).astype(BF16)
    for h in range(2):
        qaug_sc[h, 0:hd, :] = qt_ref[h * hd:(h + 1) * hd, :]
        qaug_sc[h, hd:hd + FOX_ONES_ROWS, :] = ones3
        qaug_sc[h, hd + FOX_ONES_ROWS:, :] = jnp.zeros((LANES - hd - FOX_ONES_ROWS, tq), BF16)

    def put_scores(kg, slot, masked=False):
        k0 = pl.multiple_of(kg * tg, tg)
        for h in range(2):
            st = _dot(kaug_sc[h, pl.ds(k0, tg), :], qaug_sc[h])
            if masked:
                kk = lax.broadcasted_iota(jnp.int32, (tg, tq), 0)
                tt = lax.broadcasted_iota(jnp.int32, (tg, tq), 1)
                st = jnp.where(kk <= tt, st, -jnp.inf)
            slot[h] = st

    m_sc[...] = jnp.full(m_sc.shape, -jnp.inf, F32)
    acc0_sc[...] = jnp.zeros(acc0_sc.shape, F32)
    acc1_sc[...] = jnp.zeros(acc1_sc.shape, F32)
    acc = (acc0_sc, acc1_sc)

    def absorb(kg, slot):
        k0 = pl.multiple_of(kg * tg, tg)
        half = tg // 2
        for h in range(2):
            st = slot[h]
            m_prev = m_sc[h]
            m_new = jnp.maximum(m_prev, jnp.max(st, axis=0, keepdims=True))
            pv = None
            for u in range(2):
                p = jnp.exp(st[u * half:(u + 1) * half] - m_new)
                part = _dot(vaug_sc[h, :, pl.ds(k0 + u * half, half)], p.astype(BF16))
                pv = part if pv is None else pv + part
            acc[h][...] = jnp.exp(m_prev - m_new) * acc[h][...] + pv
            m_sc[h] = m_new

    sa, sb = sta_sc, stb_sc
    n_loop = jnp.maximum(qi - 1, 0) // 2

    @pl.when(qi > 0)
    def _():
        put_scores(0, sa)

    def body(j, carry):
        put_scores(2 * j + 1, sb)
        absorb(2 * j, sa)
        put_scores(2 * j + 2, sa)
        absorb(2 * j + 1, sb)
        return carry

    lax.fori_loop(0, n_loop, body, 0)
    done = 2 * n_loop

    @pl.when(qi == 0)
    def _():
        put_scores(qi, sa, masked=True)
        absorb(qi, sa)

    @pl.when((qi > 0) & (qi - done == 1))
    def _():
        put_scores(qi, sb, masked=True)
        absorb(done, sa)
        absorb(qi, sb)

    @pl.when((qi > 0) & (qi - done == 2))
    def _():
        put_scores(done + 1, sb)
        absorb(done, sa)
        put_scores(qi, sa, masked=True)
        absorb(done + 1, sb)
        absorb(qi, sa)

    a0 = acc0_sc[...]
    a1 = acc1_sc[...]
    ot = jnp.concatenate([a0[0:hd] / a0[hd:hd + 1], a1[0:hd] / a1[hd:hd + 1]], axis=0)
    o = ot.T
    lane = lax.broadcasted_iota(jnp.int32, (tq, LANES), 1)
    lo = lane < FOX_HEAD_DIM
    sq = o * o
    ss0 = jnp.sum(jnp.where(lo, sq, 0.0), axis=-1, keepdims=True)
    ss1 = jnp.sum(jnp.where(lo, 0.0, sq), axis=-1, keepdims=True)
    inv = jnp.where(lo, lax.rsqrt(ss0 / FOX_HEAD_DIM + RMS_EPS),
                    lax.rsqrt(ss1 / FOX_HEAD_DIM + RMS_EPS))
    o_ref[...] = (o * inv * gain_ref[...]).astype(o_ref.dtype)


def _fox(fqt, fk, cpieces, fvt, fox_gain, batch, seq):
    n = fk.shape[0]
    tq = FOX_QUERY_TILE
    nq = seq // tq
    npair = FOX_HEADS // 2
    return pl.pallas_call(
        _fox_kernel,
        grid=(batch, npair, nq),
        in_specs=[pl.BlockSpec((LANES, tq), lambda b, hp, qi: (hp, b * nq + qi)),
                  pl.BlockSpec((seq, LANES), lambda b, hp, qi: (b, hp)),
                  pl.BlockSpec((1, 1, seq, LANES), lambda b, hp, qi: (b, hp, 0, 0)),
                  pl.BlockSpec((LANES, seq), lambda b, hp, qi: (hp, b)),
                  pl.BlockSpec((1, LANES), lambda b, hp, qi: (0, hp))],
        out_specs=pl.BlockSpec((tq, LANES), lambda b, hp, qi: (b * nq + qi, hp)),
        out_shape=jax.ShapeDtypeStruct((n, FOX_WIDTH), BF16),
        scratch_shapes=[pltpu.VMEM((2, seq, LANES), BF16),
                        pltpu.VMEM((2, FOX_VAUG_ROWS, seq), BF16),
                        pltpu.VMEM((2, LANES, tq), BF16),
                        pltpu.VMEM((2, tq, tq), F32),
                        pltpu.VMEM((2, tq, tq), F32),
                        pltpu.VMEM((2, 1, tq), F32),
                        pltpu.VMEM((FOX_VAUG_ROWS, tq), F32),
                        pltpu.VMEM((FOX_VAUG_ROWS, tq), F32)],
        compiler_params=_params(("parallel", "parallel", "arbitrary"), 48),
    )(fqt, fk, cpieces, fvt, fox_gain)


def _mlstm_kernel(qk_ref, v_ref, og_ref, col_ref, row_ref, cw_ref, gain_ref, o_ref,
                  tail_ref, buf_ref, c_sc, m_sc):
    c = pl.program_id(1)
    L = SEQ_BLOCK

    @pl.when(c == 0)
    def _():
        tail_ref[...] = jnp.zeros_like(tail_ref)
        c_sc[...] = jnp.zeros_like(c_sc)
        m_sc[...] = jnp.zeros_like(m_sc)

    x = qk_ref[...]
    buf_ref[0:8, :] = tail_ref[...]
    buf_ref[8:8 + L, :] = x
    tail_ref[...] = x[L - 8:L, :]
    y = x * cw_ref[CONV_WIDTH - 1:CONV_WIDTH, :]
    for j in range(CONV_WIDTH - 1):
        shift = CONV_WIDTH - 1 - j
        y = y + buf_ref[8 - shift:8 - shift + L, :] * cw_ref[j:j + 1, :]
    y = y * jax.nn.sigmoid(y)
    qc = y[:, :MLSTM_QK_WIDTH].astype(BF16)
    kf = y[:, MLSTM_QK_WIDTH:] * (MLSTM_QK_DIM ** -0.5)

    col = col_ref[...]
    row = row_ref[0]
    lane = lax.broadcasted_iota(jnp.int32, (L, LANES), 1)
    lo = lane < MLSTM_QK_DIM
    rr = lax.broadcasted_iota(jnp.int32, (L, L), 0)
    cc = lax.broadcasted_iota(jnp.int32, (L, L), 1)
    causal = cc <= rr
    ones_col = (lane == 0).astype(BF16)

    for h in range(MLSTM_HEADS):
        pair, half = divmod(h, 2)
        sel = lo if half == 0 else jnp.logical_not(lo)
        q2 = qc[:, pair * LANES:(pair + 1) * LANES]
        k2 = kf[:, pair * LANES:(pair + 1) * LANES]
        qh = jnp.where(sel, q2, jnp.zeros_like(q2))
        kb = k2.astype(BF16)
        bcol = col[:, GATE_F0 + h:GATE_F0 + h + 1]
        licol = col[:, GATE_I0 + h:GATE_I0 + h + 1]
        brow = row[GATE_F0 + h:GATE_F0 + h + 1, :]
        lirow = row[GATE_I0 + h:GATE_I0 + h + 1, :]
        g = bcol[L - 1:L, :]
        m_prev = m_sc[h][0:1, 0:1]

        d = jnp.where(causal, bcol + (lirow - brow), -jnp.inf)
        inter_log = bcol + m_prev
        m_t = jnp.maximum(inter_log, jnp.max(d, axis=-1, keepdims=True))
        w_inter = jnp.exp(inter_log - m_t)
        p = jnp.exp(d - m_t) * _dot_nt(qh, kb)
        vaug = jnp.concatenate([v_ref[:, h * LANES:(h + 1) * LANES], ones_col], axis=1)
        cstate = c_sc[h]
        tot = w_inter * _dot(qh, cstate.astype(BF16)) + _dot(p.astype(BF16), vaug)
        num = tot[:, :LANES]
        den = tot[:, LANES:LANES + 1]
        hout = num / jnp.maximum(jnp.abs(den), jnp.exp(-m_t))

        a = g - bcol + licol
        m_loc = jnp.max(a, axis=0, keepdims=True)
        wa = jnp.exp(a - m_loc)
        kw = jnp.where(sel, k2 * wa, 0.0).astype(BF16)
        kv = _dot_tn(kw, vaug)
        m_new = jnp.maximum(g + m_prev, m_loc)
        c_sc[h] = jnp.exp(g + m_prev - m_new) * cstate + jnp.exp(m_loc - m_new) * kv
        m_sc[h] = jnp.broadcast_to(m_new, m_sc.shape[1:])

        ms = jnp.mean(hout * hout, axis=-1, keepdims=True)
        hn = hout * lax.rsqrt(ms + RMS_EPS) * gain_ref[:, h * LANES:(h + 1) * LANES]
        gate = jax.nn.sigmoid(og_ref[:, h * LANES:(h + 1) * LANES])
        o_ref[:, h * LANES:(h + 1) * LANES] = (hn * gate).astype(o_ref.dtype)


def _mlstm(mqk, mv, mo, gcol, grow, conv_w, gain, batch, seq):
    n = mqk.shape[0]
    L = SEQ_BLOCK
    nc = seq // L
    row = lambda width: pl.BlockSpec((L, width), lambda b, c: (b * nc + c, 0))
    full = lambda a: pl.BlockSpec(a.shape, lambda b, c: (0, 0))
    return pl.pallas_call(
        _mlstm_kernel,
        grid=(batch, nc),
        in_specs=[row(512), row(512), row(512), row(LANES),
                  pl.BlockSpec((1, 16, L), lambda b, c: (b, 0, c)),
                  full(conv_w), full(gain)],
        out_specs=row(512),
        out_shape=jax.ShapeDtypeStruct((n, MLSTM_V_WIDTH), BF16),
        scratch_shapes=[pltpu.VMEM((8, 512), F32), pltpu.VMEM((8 + L, 512), F32),
                        pltpu.VMEM((MLSTM_HEADS, LANES, 2 * LANES), F32),
                        pltpu.VMEM((MLSTM_HEADS, 8, LANES), F32)],
        compiler_params=_params(("parallel", "arbitrary"), 48),
    )(mqk, mv, mo, gcol, grow, conv_w, gain)


def _outproj_kernel(fo_ref, mo_ref, w_ref, x_ref, g_ref, b_ref, o_ref):
    mix = _dot(fo_ref[...], w_ref[0:FOX_WIDTH, :]) + _dot(mo_ref[...], w_ref[FOX_WIDTH:, :])
    o_ref[...] = _layer_norm(DEEPNORM_ALPHA * x_ref[...] + mix, g_ref[...], b_ref[...])


def _outproj(fo, mo, w_out, x2d, g, b):
    n = x2d.shape[0]
    tm = ROW_TILE
    row = lambda width: pl.BlockSpec((tm, width), lambda i: (i, 0))
    full = lambda a: pl.BlockSpec(a.shape, lambda i: (0, 0))
    return pl.pallas_call(
        _outproj_kernel,
        grid=(n // tm,),
        in_specs=[row(512), row(512), full(w_out), row(D_MODEL), full(g), full(b)],
        out_specs=row(D_MODEL),
        out_shape=jax.ShapeDtypeStruct((n, D_MODEL), F32),
        compiler_params=_params(("parallel",), 48),
    )(fo, mo, w_out, x2d, g, b)


def _memkv_kernel(mem_ref, wk_ref, wv_ref, k_ref, v_ref):
    mb = mem_ref[...].astype(BF16)
    k_ref[...] = (_dot(mb, wk_ref[...]) * (XATTN_HEAD_DIM ** -0.5)).astype(BF16)
    v_ref[...] = _dot(mb, wv_ref[...]).astype(BF16)


def _memkv(mem2d, wk, wv, n_mem):
    n = mem2d.shape[0]
    row = pl.BlockSpec((n_mem, D_MODEL), lambda i: (i, 0))
    full = lambda a: pl.BlockSpec(a.shape, lambda i: (0, 0))
    return pl.pallas_call(
        _memkv_kernel,
        grid=(n // n_mem,),
        in_specs=[row, full(wk), full(wv)],
        out_specs=(row, row),
        out_shape=(jax.ShapeDtypeStruct((n, D_MODEL), BF16),) * 2,
        compiler_params=_params(("parallel",), 32),
    )(mem2d, wk, wv)


def _xattn_kernel(x_ref, k_ref, v_ref, wq_ref, wo_ref, g_ref, b_ref, wr_ref, br_ref,
                  o_ref, ob_ref, lg_ref):
    half = x_ref.shape[0] // 2
    for r in range(2):
        rows = slice(r * half, (r + 1) * half)
        x = x_ref[rows, :]
        q = _dot(x.astype(BF16), wq_ref[...]).astype(BF16)
        outs = []
        for h in range(XATTN_HEADS):
            sl = slice(h * XATTN_HEAD_DIM, (h + 1) * XATTN_HEAD_DIM)
            s = _dot_nt(q[:, sl], k_ref[:, sl])
            p = jnp.exp(s - jnp.max(s, axis=-1, keepdims=True))
            l = jnp.sum(p, axis=-1, keepdims=True)
            outs.append((_dot(p.astype(BF16), v_ref[:, sl]) / l).astype(BF16))
        o = jnp.concatenate(outs, axis=1)
        xa = _dot(o, wo_ref[...])
        x2 = _layer_norm(DEEPNORM_ALPHA * x + xa, g_ref[...], b_ref[...])
        o_ref[rows, :] = x2
        packed = _pack_bf16_pairs(x2)
        for j in range(ob_ref.shape[0]):
            ob_ref[j, rows, :] = packed[:, j * LANES:(j + 1) * LANES]
        x2h = x2.astype(BF16)
        x2l = (x2 - x2h.astype(F32)).astype(BF16)
        a = _dot_nt(wr_ref[...], x2h)
        b = _dot_nt(wr_ref[0:N_EXPERTS, :], x2l)
        lg_ref[:, rows] = a[0:N_EXPERTS] + a[N_EXPERTS:] + b + br_ref[...]


def _xattn(x1, kmem, vmem, wq, wo, g, b, wr, br, batch, seq, n_mem):
    n = x1.shape[0]
    tm = ROW_TILE
    nt = seq // tm
    row = lambda width: pl.BlockSpec((tm, width), lambda bi, i: (bi * nt + i, 0))
    full = lambda a: pl.BlockSpec(a.shape, lambda bi, i: (0, 0))
    kv = pl.BlockSpec((n_mem, D_MODEL), lambda bi, i: (bi, 0))
    return pl.pallas_call(
        _xattn_kernel,
        grid=(batch, nt),
        in_specs=[row(D_MODEL), kv, kv, full(wq), full(wo), full(g), full(b), full(wr), full(br)],
        out_specs=(row(D_MODEL),
                   pl.BlockSpec((X_PLANES, tm, LANES), lambda bi, i: (0, bi * nt + i, 0)),
                   pl.BlockSpec((N_EXPERTS, tm), lambda bi, i: (0, bi * nt + i))),
        out_shape=(jax.ShapeDtypeStruct((n, D_MODEL), F32),
                   jax.ShapeDtypeStruct((X_PLANES, n, LANES), jnp.uint32),
                   jax.ShapeDtypeStruct((N_EXPERTS, n), F32)),
        compiler_params=_params(("parallel", "parallel"), 48),
    )(x1, kmem, vmem, wq, wo, g, b, wr, br)


def _route_kernel(lg_ref, idx_ref, rank_ref, gate_ref, cnt_ref, carry_ref):
    i = pl.program_id(0)

    @pl.when(i == 0)
    def _():
        carry_ref[...] = jnp.zeros_like(carry_ref)

    lg = lg_ref[...]
    t = lg.shape[1]
    e_idx = lax.broadcasted_iota(jnp.int32, lg.shape, 0).astype(F32)
    sels, vals, idxs = [], [], []
    for _ in range(TOP_K):
        mx = jnp.max(lg, axis=0, keepdims=True)
        first = jnp.min(jnp.where(lg == mx, e_idx, float(N_EXPERTS)), axis=0, keepdims=True)
        sel = e_idx == first
        sels.append(sel)
        vals.append(mx)
        idxs.append(first)
        lg = jnp.where(sel, -jnp.inf, lg)
    exps = [jnp.exp(v - vals[0]) for v in vals]
    tot = exps[0] + exps[1] + exps[2] + exps[3]

    selmat = (sels[0] | sels[1] | sels[2] | sels[3])
    r = lax.broadcasted_iota(jnp.int32, (t, t), 0)
    s = lax.broadcasted_iota(jnp.int32, (t, t), 1)
    earlier = (r < s).astype(BF16)
    carry = carry_ref[:, 0:1]
    rankmat = _dot(selmat.astype(BF16), earlier) + carry
    new_carry = carry + jnp.sum(selmat.astype(F32), axis=1, keepdims=True)
    carry_ref[...] = jnp.broadcast_to(new_carry, carry_ref.shape)
    cnt_ref[...] = jnp.broadcast_to(new_carry, cnt_ref.shape).astype(jnp.int32)

    row8 = lax.broadcasted_iota(jnp.int32, (8, t), 0)
    row128 = lax.broadcasted_iota(jnp.int32, (LANES, t), 0)
    idx_out = jnp.zeros((8, t), F32)
    rank_out = jnp.zeros((8, t), F32)
    gate_out = jnp.zeros((LANES, t), F32)
    for k in range(TOP_K):
        rk = jnp.sum(jnp.where(sels[k], rankmat, 0.0), axis=0, keepdims=True)
        idx_out = jnp.where(row8 == k, idxs[k], idx_out)
        rank_out = jnp.where(row8 == k, rk, rank_out)
        gate_out = jnp.where(row128 == k, exps[k] / tot, gate_out)
    idx_ref[...] = idx_out.astype(jnp.int32)
    rank_ref[...] = rank_out.astype(jnp.int32)
    gate_ref[...] = gate_out.T


def _route(logits_t):
    n = logits_t.shape[1]
    t = ROUTE_TILE
    col = lambda rows: pl.BlockSpec((rows, t), lambda i: (0, i))
    return pl.pallas_call(
        _route_kernel,
        grid=(n // t,),
        in_specs=[col(N_EXPERTS)],
        out_specs=(col(8), col(8), pl.BlockSpec((t, LANES), lambda i: (i, 0)),
                   pl.BlockSpec((N_EXPERTS, LANES), lambda i: (0, 0))),
        out_shape=(jax.ShapeDtypeStruct((8, n), jnp.int32),
                   jax.ShapeDtypeStruct((8, n), jnp.int32),
                   jax.ShapeDtypeStruct((n, LANES), F32),
                   jax.ShapeDtypeStruct((N_EXPERTS, LANES), jnp.int32)),
        scratch_shapes=[pltpu.VMEM((N_EXPERTS, LANES), F32)],
        compiler_params=_params(("arbitrary",), 32),
    )(logits_t)


def _expert_kernel(blk_e_ref, blk_rows_ref, x_ref, wgu_ref, bgu_ref, wd_ref, bd_ref, y_ref,
                   wgu_sc, wd_sc, prev_ref):
    i = pl.program_id(0)
    e = blk_e_ref[i]
    rows = blk_rows_ref[i]
    g = y_ref.shape[1]

    @pl.when(i == 0)
    def _():
        prev_ref[0] = -1

    @pl.when((rows > 0) & (e != prev_ref[0]))
    def _():
        wgu_sc[...] = wgu_ref[0].astype(BF16)
        wd_sc[...] = wd_ref[0].astype(BF16)
        prev_ref[0] = e

    def ffn(m):
        packed = jnp.concatenate([x_ref[j, 0:m, :] for j in range(X_PLANES)], axis=1)
        xb = _unpack_bf16_pairs(packed)
        hids = []
        for c in range(D_EXPERT // EXPERT_CHUNK):
            g0 = c * EXPERT_CHUNK
            l0 = D_EXPERT + g0
            gate = _dot(xb, wgu_sc[:, g0:g0 + EXPERT_CHUNK]) + bgu_ref[0, :, g0:g0 + EXPERT_CHUNK]
            lin = _dot(xb, wgu_sc[:, l0:l0 + EXPERT_CHUNK]) + bgu_ref[0, :, l0:l0 + EXPERT_CHUNK]
            gate = jnp.minimum(gate, SWIGLU_LIMIT)
            lin = jnp.clip(lin, -SWIGLU_LIMIT, SWIGLU_LIMIT)
            hids.append((gate * jax.nn.sigmoid(SWIGLU_ALPHA * gate) * (lin + 1.0)).astype(BF16))
        y = _pack_bf16_pairs(_dot(jnp.concatenate(hids, axis=1), wd_sc[...]) + bd_ref[0])
        for j in range(Y_PLANES):
            y_ref[j, 0:m, :] = y[:, j * LANES:(j + 1) * LANES]

    @pl.when(rows > g // 2)
    def _():
        ffn(g)

    @pl.when((rows > 0) & (rows <= g // 2))
    def _():
        ffn(g // 2)
        y_ref[:, g // 2:, :] = jnp.zeros((Y_PLANES, g // 2, LANES), y_ref.dtype)

    @pl.when(rows == 0)
    def _():
        y_ref[...] = jnp.zeros_like(y_ref)


def _experts(blk_e, blk_rows, xs, w_gu, b_gu, w_d, b_d):
    p = xs.shape[1]
    g = EXPERT_ROWS
    grid_spec = pltpu.PrefetchScalarGridSpec(
        num_scalar_prefetch=2,
        grid=(p // g,),
        in_specs=[pl.BlockSpec((X_PLANES, g, LANES), lambda i, be, nb: (0, i, 0)),
                  pl.BlockSpec((1, D_MODEL, 2 * D_EXPERT), lambda i, be, nb: (be[i], 0, 0)),
                  pl.BlockSpec((1, 1, 2 * D_EXPERT), lambda i, be, nb: (be[i], 0, 0)),
                  pl.BlockSpec((1, D_EXPERT, D_MODEL), lambda i, be, nb: (be[i], 0, 0)),
                  pl.BlockSpec((1, 1, D_MODEL), lambda i, be, nb: (be[i], 0, 0))],
        out_specs=pl.BlockSpec((Y_PLANES, g, LANES), lambda i, be, nb: (0, i, 0)),
        scratch_shapes=[pltpu.VMEM((D_MODEL, 2 * D_EXPERT), BF16),
                        pltpu.VMEM((D_EXPERT, D_MODEL), BF16),
                        pltpu.SMEM((1,), jnp.int32)],
    )
    return pl.pallas_call(
        _expert_kernel,
        grid_spec=grid_spec,
        out_shape=jax.ShapeDtypeStruct((Y_PLANES, p, LANES), jnp.uint32),
        compiler_params=_params(("arbitrary",), 56),
    )(blk_e, blk_rows, xs, w_gu, b_gu, w_d, b_d)


def _combine_kernel(y_ref, gate_ref, x_ref, g_ref, b_ref, o_ref):
    gate = gate_ref[...]
    ff = None
    for k in range(TOP_K):
        packed = jnp.concatenate([y_ref[k, j] for j in range(Y_PLANES)], axis=1)
        yk = _unpack_bf16_pairs(packed, F32) * gate[:, k:k + 1]
        ff = yk if ff is None else ff + yk
    o_ref[...] = _layer_norm(DEEPNORM_ALPHA * x_ref[...] + ff, g_ref[...], b_ref[...])


def _combine(yg, gate, x2, g, b):
    n = x2.shape[0]
    tm = SEQ_BLOCK
    row = lambda width: pl.BlockSpec((tm, width), lambda i: (i, 0))
    full = lambda a: pl.BlockSpec(a.shape, lambda i: (0, 0))
    return pl.pallas_call(
        _combine_kernel,
        grid=(n // tm,),
        in_specs=[pl.BlockSpec((TOP_K, Y_PLANES, tm, LANES), lambda i: (0, 0, i, 0)),
                  row(LANES), row(D_MODEL), full(g), full(b)],
        out_specs=row(D_MODEL),
        out_shape=jax.ShapeDtypeStruct((n, D_MODEL), F32),
        compiler_params=_params(("parallel",), 32),
    )(yg, gate, x2, g, b)


SC_WINDOW = 128


def _sc_mesh():
    return plsc.VectorSubcoreMesh(core_axis_name="core", subcore_axis_name="subcore")


def _sc_gather(table, idx):
    m = idx.shape[0]

    @pl.kernel(out_type=jax.ShapeDtypeStruct((m, LANES), table.dtype), mesh=_sc_mesh())
    def gather_kernel(table_hbm, idx_hbm, out_hbm):
        def body(idx_vmem, out_vmem):
            pltpu.sync_copy(table_hbm.at[idx_vmem.at[0]], out_vmem)

        pltpu.emit_pipeline(
            body,
            grid=(m // SC_WINDOW,),
            in_specs=[pl.BlockSpec((1, SC_WINDOW), lambda i: (0, i))],
            out_specs=[pl.BlockSpec((SC_WINDOW, LANES), lambda i: (i, 0))],
            core_axis_name=("core", "subcore"),
            dimension_semantics=(pltpu.PARALLEL,),
        )(idx_hbm, out_hbm)

    return gather_kernel(table, idx.reshape(1, m))


def _sc_scatter(src, idx, out_rows):
    m = idx.shape[0]
    src_blocks = src.shape[0] // SC_WINDOW

    @pl.kernel(out_type=jax.ShapeDtypeStruct((out_rows, LANES), src.dtype), mesh=_sc_mesh())
    def scatter_kernel(src_hbm, idx_hbm, out_hbm):
        def body(src_vmem, idx_vmem):
            pltpu.sync_copy(src_vmem, out_hbm.at[idx_vmem.at[0]])

        pltpu.emit_pipeline(
            body,
            grid=(m // SC_WINDOW,),
            in_specs=[pl.BlockSpec((SC_WINDOW, LANES), lambda i: (i % src_blocks, 0)),
                      pl.BlockSpec((1, SC_WINDOW), lambda i: (0, i))],
            out_specs=[],
            core_axis_name=("core", "subcore"),
            dimension_semantics=(pltpu.PARALLEL,),
        )(src_hbm, idx_hbm)

    return scatter_kernel(src, idx.reshape(1, m))


def _layer(x, mem, w_in, fox_f_bias, conv_w, i_bias, f_bias, fox_g, mlstm_g, w_mix_out,
           ln1_g, ln1_b, w_xq, w_xk, w_xv, w_xo, ln2_g, ln2_b, w_router, b_router,
           w_gate_up, b_gate_up, w_down, b_down, ln3_g, ln3_b):
    batch, seq, d = x.shape
    n_mem = mem.shape[1]
    n = batch * seq
    x2d = x.reshape(n, d)

    o_ff = 3 * FOX_WIDTH
    o_mqk = o_ff + FOX_HEADS
    o_mv = o_mqk + 2 * MLSTM_QK_WIDTH
    o_mi = o_mv + MLSTM_V_WIDTH
    o_mf = o_mi + MLSTM_HEADS
    o_mo = o_mf + MLSTM_HEADS
    n_gate = FOX_HEADS + 2 * MLSTM_HEADS
    w_r = jnp.concatenate(
        [w_in[:, FOX_WIDTH:2 * FOX_WIDTH], w_in[:, o_mqk:o_mi], w_in[:, o_mo:],
         w_in[:, o_ff:o_mqk], w_in[:, o_mi:o_mo],
         jnp.zeros((d, LANES - n_gate), w_in.dtype)], axis=1).astype(BF16)
    w_t = jnp.concatenate([w_in[:, :FOX_WIDTH], w_in[:, 2 * FOX_WIDTH:o_ff]], axis=1).T.astype(BF16)
    gate_bias = jnp.concatenate(
        [fox_f_bias, i_bias, f_bias, jnp.zeros((LANES - n_gate,), F32)]).reshape(1, LANES)

    fqt, fk, fvt, mqk, mv, mo, gates = _inproj(x2d, w_r, w_t, gate_bias)
    gcol, grow, cpieces = _gateprep(gates, batch, seq)
    fo = _fox(fqt, fk, cpieces, fvt, fox_g.reshape(1, FOX_WIDTH), batch, seq)
    mo_out = _mlstm(mqk, mv, mo, gcol, grow, conv_w, mlstm_g.reshape(1, MLSTM_V_WIDTH), batch, seq)
    x1 = _outproj(fo, mo_out, w_mix_out.astype(BF16), x2d, ln1_g.reshape(1, d), ln1_b.reshape(1, d))

    kmem, vmem = _memkv(mem.reshape(batch * n_mem, d), w_xk.astype(BF16), w_xv.astype(BF16), n_mem)
    wrt = w_router.T
    wrt_hi = wrt.astype(BF16)
    wrt_lo = (wrt - wrt_hi.astype(F32)).astype(BF16)
    x2, x2p, logits_t = _xattn(x1, kmem, vmem, w_xq.astype(BF16), w_xo.astype(BF16),
                               ln2_g.reshape(1, d), ln2_b.reshape(1, d),
                               jnp.concatenate([wrt_hi, wrt_lo], axis=0),
                               b_router.reshape(N_EXPERTS, 1), batch, seq, n_mem)

    idx_t, rank_t, gate, cnt = _route(logits_t)
    counts = cnt[:, 0]
    g_rows = EXPERT_ROWS
    padded = ((counts + g_rows - 1) // g_rows) * g_rows
    pad_end = jnp.cumsum(padded)
    pad_start = pad_end - padded
    experts = jnp.arange(N_EXPERTS, dtype=jnp.int32)
    sel = idx_t[:TOP_K, :, None] == experts[None, None, :]
    pos_t = jnp.sum(jnp.where(sel, pad_start[None, None, :], 0), axis=-1) + rank_t[:TOP_K]
    p_rows = n * TOP_K + N_EXPERTS * g_rows
    nb = p_rows // g_rows
    blk_start = jnp.arange(nb, dtype=jnp.int32) * g_rows
    blk_e = jnp.minimum(jnp.sum((pad_end[None, :] <= blk_start[:, None]).astype(jnp.int32), axis=1),
                        N_EXPERTS - 1)
    own = blk_e[:, None] == experts[None, :]
    row_end = jnp.sum(jnp.where(own, (pad_start + counts)[None, :], 0), axis=1)
    blk_rows = jnp.clip(row_end - blk_start, 0, g_rows).astype(jnp.int32)

    def piece_index(planes):
        off = jnp.arange(planes, dtype=jnp.int32) * p_rows
        return (pos_t[:, None, :] + off[None, :, None]).reshape(-1)

    xs = _sc_scatter(x2p.reshape(X_PLANES * n, LANES), piece_index(X_PLANES), X_PLANES * p_rows)
    y = _experts(blk_e, blk_rows, xs.reshape(X_PLANES, p_rows, LANES), w_gate_up,
                 b_gate_up.reshape(N_EXPERTS, 1, -1), w_down, b_down.reshape(N_EXPERTS, 1, -1))
    yg = _sc_gather(y.reshape(Y_PLANES * p_rows, LANES), piece_index(Y_PLANES))
    out = _combine(yg.reshape(TOP_K, Y_PLANES, n, LANES), gate, x2,
                   ln3_g.reshape(1, d), ln3_b.reshape(1, d))
    return out.reshape(batch, seq, d)


def kernel(x, mem, w_in, fox_f_bias, mlstm_conv_w, mlstm_i_bias, mlstm_f_bias, fox_norm_g, mlstm_norm_g, w_mix_out, ln1_g, ln1_b, w_xq, w_xk, w_xv, w_xo, ln2_g, ln2_b, w_router, b_router, w_gate_up, b_gate_up, w_down, b_down, ln3_g, ln3_b):
    for l in range(w_in.shape[0]):
        x = _layer(x, mem, w_in[l], fox_f_bias[l], mlstm_conv_w[l], mlstm_i_bias[l],
                   mlstm_f_bias[l], fox_norm_g[l], mlstm_norm_g[l], w_mix_out[l],
                   ln1_g[l], ln1_b[l], w_xq[l], w_xk[l], w_xv[l], w_xo[l], ln2_g[l], ln2_b[l],
                   w_router[l], b_router[l], w_gate_up[l], b_gate_up[l], w_down[l], b_down[l],
                   ln3_g[l], ln3_b[l])
    return x
```

```python
import jax
import jax.numpy as jnp
from jax import lax
from jax.experimental import pallas as pl
from jax.experimental.pallas import tpu as pltpu
from jax.experimental.pallas import tpu_sc as plsc

F32 = jnp.float32
BF16 = jnp.bfloat16

D_MODEL = 1024
FOX_HEADS = 8
FOX_HEAD_DIM = 64
FOX_WIDTH = FOX_HEADS * FOX_HEAD_DIM
MLSTM_HEADS = 4
MLSTM_QK_DIM = 64
MLSTM_V_DIM = 128
MLSTM_QK_WIDTH = MLSTM_HEADS * MLSTM_QK_DIM
MLSTM_V_WIDTH = MLSTM_HEADS * MLSTM_V_DIM
CONV_WIDTH = 4
XATTN_HEADS = 4
XATTN_HEAD_DIM = D_MODEL // XATTN_HEADS
N_EXPERTS = 32
TOP_K = 4
D_EXPERT = D_MODEL
SWIGLU_LIMIT = 7.0
SWIGLU_ALPHA = 1.702
DEEPNORM_ALPHA = 2.0 ** 0.25
LN_EPS = 1e-5
RMS_EPS = 1e-6

LANES = 128
SEQ_BLOCK = 256
ROW_TILE = 512
EXPERT_ROWS = 512
EXPERT_CHUNK = 256
ROUTE_TILE = 512
X_PLANES = D_MODEL // 2 // LANES
Y_PLANES = X_PLANES
GATE_I0 = FOX_HEADS
GATE_F0 = FOX_HEADS + MLSTM_HEADS

MIB = 1024 * 1024


def _params(semantics, vmem_mib):
    return pltpu.CompilerParams(dimension_semantics=semantics,
                                vmem_limit_bytes=vmem_mib * MIB)


def _layer_norm(y, g, b):
    mu = jnp.mean(y, axis=-1, keepdims=True)
    yc = y - mu
    var = jnp.mean(yc * yc, axis=-1, keepdims=True)
    return yc * lax.rsqrt(var + LN_EPS) * g + b


def _dot(a, b):
    return jnp.dot(a, b, preferred_element_type=F32)


def _dot_nt(a, b):
    return lax.dot_general(a, b, (((1,), (1,)), ((), ())), preferred_element_type=F32)


def _dot_tn(a, b):
    return lax.dot_general(a, b, (((0,), (0,)), ((), ())), preferred_element_type=F32)


def _pack_bf16_pairs(x):
    w = x.shape[1] // 2
    lo = pltpu.bitcast(x[:, :w].astype(BF16).astype(F32), jnp.uint32)
    hi = pltpu.bitcast(x[:, w:].astype(BF16).astype(F32), jnp.uint32)
    return (lo >> 16) | hi


def _unpack_bf16_pairs(u, dtype=BF16):
    lo = pltpu.bitcast(u << 16, F32).astype(dtype)
    hi = pltpu.bitcast(u & jnp.uint32(0xFFFF0000), F32).astype(dtype)
    return jnp.concatenate([lo, hi], axis=1)


def _inproj_kernel(x_ref, w_ref, wt_ref, gb_ref, fqt_ref, fk_ref, fvt_ref, mqk_ref, mv_ref, mo_ref,
                   g_ref):
    xb = x_ref[...].astype(BF16)

    def mm(c0, width):
        return _dot(xb, w_ref[:, c0:c0 + width])

    fqt_ref[...] = (_dot_nt(wt_ref[0:FOX_WIDTH, :], xb) * (FOX_HEAD_DIM ** -0.5)).astype(BF16)
    fvt_ref[...] = _dot_nt(wt_ref[FOX_WIDTH:, :], xb).astype(BF16)
    fk_ref[...] = mm(0, 512).astype(BF16)
    mqk_ref[...] = mm(512, 512)
    mv_ref[...] = mm(1024, 512).astype(BF16)
    mo_ref[...] = mm(1536, 512)
    g_ref[...] = mm(2048, LANES) + gb_ref[...]


def _inproj(x2d, w_r, w_t, gate_bias):
    n = x2d.shape[0]
    tm = ROW_TILE
    row = lambda width: pl.BlockSpec((tm, width), lambda i: (i, 0))
    col = pl.BlockSpec((FOX_WIDTH, tm), lambda i: (0, i))
    full = lambda a: pl.BlockSpec(a.shape, lambda i: (0, 0))
    out_shapes = (
        jax.ShapeDtypeStruct((FOX_WIDTH, n), BF16),
        jax.ShapeDtypeStruct((n, 512), BF16),
        jax.ShapeDtypeStruct((FOX_WIDTH, n), BF16),
        jax.ShapeDtypeStruct((n, 512), F32),
        jax.ShapeDtypeStruct((n, 512), BF16),
        jax.ShapeDtypeStruct((n, 512), F32),
        jax.ShapeDtypeStruct((n, LANES), F32),
    )
    return pl.pallas_call(
        _inproj_kernel,
        grid=(n // tm,),
        in_specs=[row(D_MODEL), full(w_r), full(w_t), full(gate_bias)],
        out_specs=(col, row(512), col, row(512), row(512), row(512), row(LANES)),
        out_shape=out_shapes,
        compiler_params=_params(("parallel",), 48),
    )(x2d, w_r, w_t, gate_bias)


def _split3(x):
    hi = x.astype(BF16)
    r1 = x - hi.astype(F32)
    mid = r1.astype(BF16)
    lo = (r1 - mid.astype(F32)).astype(BF16)
    return hi, mid, lo


def _gateprep_kernel(g_ref, sel_ref, col_ref, row_ref, cp_ref, carry_ref):
    c = pl.program_id(1)

    @pl.when(c == 0)
    def _():
        carry_ref[...] = jnp.zeros_like(carry_ref)

    g = g_ref[...]
    lane = lax.broadcasted_iota(jnp.int32, g.shape, 1)
    is_i = (lane >= GATE_I0) & (lane < GATE_F0)
    logsig = jnp.minimum(g, 0.0) - jnp.log1p(jnp.exp(-jnp.abs(g)))
    blk = g.shape[0]
    r = lax.broadcasted_iota(jnp.int32, (blk, blk), 0)
    s = lax.broadcasted_iota(jnp.int32, (blk, blk), 1)
    tri = (s <= r).astype(BF16)
    cs = None
    for piece in _split3(logsig):
        term = _dot(tri, piece)
        cs = term if cs is None else cs + term
    carry = carry_ref[0:1, :]
    glob = cs + carry
    carry_ref[...] = jnp.broadcast_to(glob[blk - 1:blk, :], carry_ref.shape)
    out = jnp.where(lane < GATE_I0, glob, jnp.where(is_i, g, cs))
    col_ref[...] = out
    row_ref[0] = out.T[0:16, :]

    pieces = jnp.concatenate(_split3(-glob), axis=1)
    moved = _dot(pieces, sel_ref[...])
    for p in range(FOX_HEADS // 2):
        cp_ref[0, p] = moved[:, p * LANES:(p + 1) * LANES].astype(BF16)


def _piece_selector():
    src = jnp.arange(3 * LANES, dtype=jnp.int32)[:, None]
    dst = jnp.arange(4 * LANES, dtype=jnp.int32)[None, :]
    piece, head = src // LANES, src % LANES
    pair, lane = dst // LANES, dst % LANES
    hit = (lane < 6) & (lane % 3 == piece) & (head == 2 * pair + lane // 3)
    return hit.astype(BF16)


def _gateprep(gates, batch, seq):
    n = gates.shape[0]
    nc = seq // SEQ_BLOCK
    sel = _piece_selector()
    return pl.pallas_call(
        _gateprep_kernel,
        grid=(batch, nc),
        in_specs=[pl.BlockSpec((SEQ_BLOCK, LANES), lambda b, c: (b * nc + c, 0)),
                  pl.BlockSpec(sel.shape, lambda b, c: (0, 0))],
        out_specs=(pl.BlockSpec((SEQ_BLOCK, LANES), lambda b, c: (b * nc + c, 0)),
                   pl.BlockSpec((1, 16, SEQ_BLOCK), lambda b, c: (b, 0, c)),
                   pl.BlockSpec((1, FOX_HEADS // 2, SEQ_BLOCK, LANES), lambda b, c: (b, 0, c, 0))),
        out_shape=(jax.ShapeDtypeStruct((n, LANES), F32),
                   jax.ShapeDtypeStruct((batch, 16, seq), F32),
                   jax.ShapeDtypeStruct((batch, FOX_HEADS // 2, seq, LANES), BF16)),
        scratch_shapes=[pltpu.VMEM((8, LANES), F32)],
        compiler_params=_params(("parallel", "arbitrary"), 32),
    )(gates, sel)


FOX_ONES_ROWS = 16
FOX_VAUG_ROWS = FOX_HEAD_DIM + FOX_ONES_ROWS
FOX_QUERY_TILE = 2 * SEQ_BLOCK


def _fox_kernel(qt_ref, k_ref, cp_ref, vt_ref, gain_ref, o_ref,
                kaug_sc, vaug_sc, qaug_sc, sta_sc, stb_sc, m_sc, acc0_sc, acc1_sc):
    qi = pl.program_id(2)
    tq = qt_ref.shape[1]
    tg = tq
    tk = SEQ_BLOCK
    hd = FOX_HEAD_DIM
    seq = k_ref.shape[0]

    @pl.when(qi == 0)
    def _():
        lane = lax.broadcasted_iota(jnp.int32, (tk, LANES), 1)

        def build(blk, carry):
            r0 = pl.multiple_of(blk * tk, tk)
            kp = k_ref[pl.ds(r0, tk), :].astype(F32)
            cp = cp_ref[0, 0, pl.ds(r0, tk), :].astype(F32)
            for h in range(2):
                kh = kp if h == 0 else pltpu.roll(kp, hd, axis=1)
                ch = pltpu.roll(cp, hd - 3 * h, axis=1)
                kaug = jnp.where(lane < hd, kh, jnp.where(lane < hd + 3, ch, 0.0))
                kaug_sc[h, pl.ds(r0, tk), :] = kaug.astype(BF16)
            return carry

        lax.fori_loop(0, seq // tk, build, 0)
        for h in range(2):
            vaug_sc[h, 0:hd, :] = vt_ref[h * hd:(h + 1) * hd, :]
            vaug_sc[h, hd:, :] = jnp.ones((FOX_ONES_ROWS, seq), BF16)

    ones3 = (lax.broadcasted_iota(jnp.int32, (FOX_ONES_ROWS, tq), 0) < 3).astype(BF16)
    for h in range(2):
        qaug_sc[h, 0:hd, :] = qt_ref[h * hd:(h + 1) * hd, :]
        qaug_sc[h, hd:hd + FOX_ONES_ROWS, :] = ones3
        qaug_sc[h, hd + FOX_ONES_ROWS:, :] = jnp.zeros((LANES - hd - FOX_ONES_ROWS, tq), BF16)

    def put_scores(kg, slot, masked=False):
        k0 = pl.multiple_of(kg * tg, tg)
        for h in range(2):
            st = _dot(kaug_sc[h, pl.ds(k0, tg), :], qaug_sc[h])
            if masked:
                kk = lax.broadcasted_iota(jnp.int32, (tg, tq), 0)
                tt = lax.broadcasted_iota(jnp.int32, (tg, tq), 1)
                st = jnp.where(kk <= tt, st, -jnp.inf)
            slot[h] = st

    m_sc[...] = jnp.full(m_sc.shape, -jnp.inf, F32)
    acc0_sc[...] = jnp.zeros(acc0_sc.shape, F32)
    acc1_sc[...] = jnp.zeros(acc1_sc.shape, F32)
    acc = (acc0_sc, acc1_sc)

    def absorb(kg, slot):
        k0 = pl.multiple_of(kg * tg, tg)
        half = tg // 2
        for h in range(2):
            st = slot[h]
            m_prev = m_sc[h]
            m_new = jnp.maximum(m_prev, jnp.max(st, axis=0, keepdims=True))
            pv = None
            for u in range(2):
                p = jnp.exp(st[u * half:(u + 1) * half] - m_new)
                part = _dot(vaug_sc[h, :, pl.ds(k0 + u * half, half)], p.astype(BF16))
                pv = part if pv is None else pv + part
            acc[h][...] = jnp.exp(m_prev - m_new) * acc[h][...] + pv
            m_sc[h] = m_new

    sa, sb = sta_sc, stb_sc
    n_loop = jnp.maximum(qi - 1, 0) // 2

    @pl.when(qi > 0)
    def _():
        put_scores(0, sa)

    def body(j, carry):
        put_scores(2 * j + 1, sb)
        absorb(2 * j, sa)
        put_scores(2 * j + 2, sa)
        absorb(2 * j + 1, sb)
        return carry

    lax.fori_loop(0, n_loop, body, 0)
    done = 2 * n_loop

    @pl.when(qi == 0)
    def _():
        put_scores(qi, sa, masked=True)
        absorb(qi, sa)

    @pl.when((qi > 0) & (qi - done == 1))
    def _():
        put_scores(qi, sb, masked=True)
        absorb(done, sa)
        absorb(qi, sb)

    @pl.when((qi > 0) & (qi - done == 2))
    def _():
        put_scores(done + 1, sb)
        absorb(done, sa)
        put_scores(qi, sa, masked=True)
        absorb(done + 1, sb)
        absorb(qi, sa)

    a0 = acc0_sc[...]
    a1 = acc1_sc[...]
    ot = jnp.concatenate([a0[0:hd] / a0[hd:hd + 1], a1[0:hd] / a1[hd:hd + 1]], axis=0)
    o = ot.T
    lane = lax.broadcasted_iota(jnp.int32, (tq, LANES), 1)
    lo = lane < FOX_HEAD_DIM
    sq = o * o
    ss0 = jnp.sum(jnp.where(lo, sq, 0.0), axis=-1, keepdims=True)
    ss1 = jnp.sum(jnp.where(lo, 0.0, sq), axis=-1, keepdims=True)
    inv = jnp.where(lo, lax.rsqrt(ss0 / FOX_HEAD_DIM + RMS_EPS),
                    lax.rsqrt(ss1 / FOX_HEAD_DIM + RMS_EPS))
    o_ref[...] = (o * inv * gain_ref[...]).astype(o_ref.dtype)


def _fox(fqt, fk, cpieces, fvt, fox_gain, batch, seq):
    n = fk.shape[0]
    tq = FOX_QUERY_TILE
    nq = seq // tq
    npair = FOX_HEADS // 2
    return pl.pallas_call(
        _fox_kernel,
        grid=(batch, npair, nq),
        in_specs=[pl.BlockSpec((LANES, tq), lambda b, hp, qi: (hp, b * nq + qi)),
                  pl.BlockSpec((seq, LANES), lambda b, hp, qi: (b, hp)),
                  pl.BlockSpec((1, 1, seq, LANES), lambda b, hp, qi: (b, hp, 0, 0)),
                  pl.BlockSpec((LANES, seq), lambda b, hp, qi: (hp, b)),
                  pl.BlockSpec((1, LANES), lambda b, hp, qi: (0, hp))],
        out_specs=pl.BlockSpec((tq, LANES), lambda b, hp, qi: (b * nq + qi, hp)),
        out_shape=jax.ShapeDtypeStruct((n, FOX_WIDTH), BF16),
        scratch_shapes=[pltpu.VMEM((2, seq, LANES), BF16),
                        pltpu.VMEM((2, FOX_VAUG_ROWS, seq), BF16),
                        pltpu.VMEM((2, LANES, tq), BF16),
                        pltpu.VMEM((2, tq, tq), F32),
                        pltpu.VMEM((2, tq, tq), F32),
                        pltpu.VMEM((2, 1, tq), F32),
                        pltpu.VMEM((FOX_VAUG_ROWS, tq), F32),
                        pltpu.VMEM((FOX_VAUG_ROWS, tq), F32)],
        compiler_params=_params(("parallel", "parallel", "arbitrary"), 48),
    )(fqt, fk, cpieces, fvt, fox_gain)


def _mlstm_kernel(qk_ref, v_ref, og_ref, col_ref, row_ref, cw_ref, gain_ref, o_ref,
                  tail_ref, buf_ref, c_sc, m_sc):
    c = pl.program_id(1)
    L = SEQ_BLOCK

    @pl.when(c == 0)
    def _():
        tail_ref[...] = jnp.zeros_like(tail_ref)
        c_sc[...] = jnp.zeros_like(c_sc)
        m_sc[...] = jnp.zeros_like(m_sc)

    x = qk_ref[...]
    buf_ref[0:8, :] = tail_ref[...]
    buf_ref[8:8 + L, :] = x
    tail_ref[...] = x[L - 8:L, :]
    y = x * cw_ref[CONV_WIDTH - 1:CONV_WIDTH, :]
    for j in range(CONV_WIDTH - 1):
        shift = CONV_WIDTH - 1 - j
        y = y + buf_ref[8 - shift:8 - shift + L, :] * cw_ref[j:j + 1, :]
    y = y * jax.nn.sigmoid(y)
    qc = y[:, :MLSTM_QK_WIDTH].astype(BF16)
    kf = y[:, MLSTM_QK_WIDTH:] * (MLSTM_QK_DIM ** -0.5)

    col = col_ref[...]
    row = row_ref[0]
    lane = lax.broadcasted_iota(jnp.int32, (L, LANES), 1)
    lo = lane < MLSTM_QK_DIM
    rr = lax.broadcasted_iota(jnp.int32, (L, L), 0)
    cc = lax.broadcasted_iota(jnp.int32, (L, L), 1)
    causal = cc <= rr
    ones_col = (lane == 0).astype(BF16)

    for h in range(MLSTM_HEADS):
        pair, half = divmod(h, 2)
        sel = lo if half == 0 else jnp.logical_not(lo)
        q2 = qc[:, pair * LANES:(pair + 1) * LANES]
        k2 = kf[:, pair * LANES:(pair + 1) * LANES]
        qh = jnp.where(sel, q2, jnp.zeros_like(q2))
        kb = k2.astype(BF16)
        bcol = col[:, GATE_F0 + h:GATE_F0 + h + 1]
        licol = col[:, GATE_I0 + h:GATE_I0 + h + 1]
        brow = row[GATE_F0 + h:GATE_F0 + h + 1, :]
        lirow = row[GATE_I0 + h:GATE_I0 + h + 1, :]
        g = bcol[L - 1:L, :]
        m_prev = m_sc[h][0:1, 0:1]

        d = jnp.where(causal, bcol + (lirow - brow), -jnp.inf)
        inter_log = bcol + m_prev
        m_t = jnp.maximum(inter_log, jnp.max(d, axis=-1, keepdims=True))
        w_inter = jnp.exp(inter_log - m_t)
        p = jnp.exp(d - m_t) * _dot_nt(qh, kb)
        vaug = jnp.concatenate([v_ref[:, h * LANES:(h + 1) * LANES], ones_col], axis=1)
        cstate = c_sc[h]
        tot = w_inter * _dot(qh, cstate.astype(BF16)) + _dot(p.astype(BF16), vaug)
        num = tot[:, :LANES]
        den = tot[:, LANES:LANES + 1]
        hout = num / jnp.maximum(jnp.abs(den), jnp.exp(-m_t))

        a = g - bcol + licol
        m_loc = jnp.max(a, axis=0, keepdims=True)
        wa = jnp.exp(a - m_loc)
        kw = jnp.where(sel, k2 * wa, 0.0).astype(BF16)
        kv = _dot_tn(kw, vaug)
        m_new = jnp.maximum(g + m_prev, m_loc)
        c_sc[h] = jnp.exp(g + m_prev - m_new) * cstate + jnp.exp(m_loc - m_new) * kv
        m_sc[h] = jnp.broadcast_to(m_new, m_sc.shape[1:])

        ms = jnp.mean(hout * hout, axis=-1, keepdims=True)
        hn = hout * lax.rsqrt(ms + RMS_EPS) * gain_ref[:, h * LANES:(h + 1) * LANES]
        gate = jax.nn.sigmoid(og_ref[:, h * LANES:(h + 1) * LANES])
        o_ref[:, h * LANES:(h + 1) * LANES] = (hn * gate).astype(o_ref.dtype)


def _mlstm(mqk, mv, mo, gcol, grow, conv_w, gain, batch, seq):
    n = mqk.shape[0]
    L = SEQ_BLOCK
    nc = seq // L
    row = lambda width: pl.BlockSpec((L, width), lambda b, c: (b * nc + c, 0))
    full = lambda a: pl.BlockSpec(a.shape, lambda b, c: (0, 0))
    return pl.pallas_call(
        _mlstm_kernel,
        grid=(batch, nc),
        in_specs=[row(512), row(512), row(512), row(LANES),
                  pl.BlockSpec((1, 16, L), lambda b, c: (b, 0, c)),
                  full(conv_w), full(gain)],
        out_specs=row(512),
        out_shape=jax.ShapeDtypeStruct((n, MLSTM_V_WIDTH), BF16),
        scratch_shapes=[pltpu.VMEM((8, 512), F32), pltpu.VMEM((8 + L, 512), F32),
                        pltpu.VMEM((MLSTM_HEADS, LANES, 2 * LANES), F32),
                        pltpu.VMEM((MLSTM_HEADS, 8, LANES), F32)],
        compiler_params=_params(("parallel", "arbitrary"), 48),
    )(mqk, mv, mo, gcol, grow, conv_w, gain)


def _outproj_kernel(fo_ref, mo_ref, w_ref, x_ref, g_ref, b_ref, o_ref):
    mix = _dot(fo_ref[...], w_ref[0:FOX_WIDTH, :]) + _dot(mo_ref[...], w_ref[FOX_WIDTH:, :])
    o_ref[...] = _layer_norm(DEEPNORM_ALPHA * x_ref[...] + mix, g_ref[...], b_ref[...])


def _outproj(fo, mo, w_out, x2d, g, b):
    n = x2d.shape[0]
    tm = ROW_TILE
    row = lambda width: pl.BlockSpec((tm, width), lambda i: (i, 0))
    full = lambda a: pl.BlockSpec(a.shape, lambda i: (0, 0))
    return pl.pallas_call(
        _outproj_kernel,
        grid=(n // tm,),
        in_specs=[row(512), row(512), full(w_out), row(D_MODEL), full(g), full(b)],
        out_specs=row(D_MODEL),
        out_shape=jax.ShapeDtypeStruct((n, D_MODEL), F32),
        compiler_params=_params(("parallel",), 48),
    )(fo, mo, w_out, x2d, g, b)


def _memkv_kernel(mem_ref, wk_ref, wv_ref, k_ref, v_ref):
    mb = mem_ref[...].astype(BF16)
    k_ref[...] = (_dot(mb, wk_ref[...]) * (XATTN_HEAD_DIM ** -0.5)).astype(BF16)
    v_ref[...] = _dot(mb, wv_ref[...]).astype(BF16)


def _memkv(mem2d, wk, wv, n_mem):
    n = mem2d.shape[0]
    row = pl.BlockSpec((n_mem, D_MODEL), lambda i: (i, 0))
    full = lambda a: pl.BlockSpec(a.shape, lambda i: (0, 0))
    return pl.pallas_call(
        _memkv_kernel,
        grid=(n // n_mem,),
        in_specs=[row, full(wk), full(wv)],
        out_specs=(row, row),
        out_shape=(jax.ShapeDtypeStruct((n, D_MODEL), BF16),) * 2,
        compiler_params=_params(("parallel",), 32),
    )(mem2d, wk, wv)


def _xattn_kernel(x_ref, k_ref, v_ref, wq_ref, wo_ref, g_ref, b_ref, wr_ref, br_ref,
                  o_ref, ob_ref, lg_ref):
    half = x_ref.shape[0] // 2
    for r in range(2):
        rows = slice(r * half, (r + 1) * half)
        x = x_ref[rows, :]
        q = _dot(x.astype(BF16), wq_ref[...]).astype(BF16)
        outs = []
        for h in range(XATTN_HEADS):
            sl = slice(h * XATTN_HEAD_DIM, (h + 1) * XATTN_HEAD_DIM)
            s = _dot_nt(q[:, sl], k_ref[:, sl])
            p = jnp.exp(s - jnp.max(s, axis=-1, keepdims=True))
            l = jnp.sum(p, axis=-1, keepdims=True)
            outs.append((_dot(p.astype(BF16), v_ref[:, sl]) / l).astype(BF16))
        o = jnp.concatenate(outs, axis=1)
        xa = _dot(o, wo_ref[...])
        x2 = _layer_norm(DEEPNORM_ALPHA * x + xa, g_ref[...], b_ref[...])
        o_ref[rows, :] = x2
        packed = _pack_bf16_pairs(x2)
        for j in range(ob_ref.shape[0]):
            ob_ref[j, rows, :] = packed[:, j * LANES:(j + 1) * LANES]
        x2h = x2.astype(BF16)
        x2l = (x2 - x2h.astype(F32)).astype(BF16)
        a = _dot_nt(wr_ref[...], x2h)
        b = _dot_nt(wr_ref[0:N_EXPERTS, :], x2l)
        lg_ref[:, rows] = a[0:N_EXPERTS] + a[N_EXPERTS:] + b + br_ref[...]


def _xattn(x1, kmem, vmem, wq, wo, g, b, wr, br, batch, seq, n_mem):
    n = x1.shape[0]
    tm = ROW_TILE
    nt = seq // tm
    row = lambda width: pl.BlockSpec((tm, width), lambda bi, i: (bi * nt + i, 0))
    full = lambda a: pl.BlockSpec(a.shape, lambda bi, i: (0, 0))
    kv = pl.BlockSpec((n_mem, D_MODEL), lambda bi, i: (bi, 0))
    return pl.pallas_call(
        _xattn_kernel,
        grid=(batch, nt),
        in_specs=[row(D_MODEL), kv, kv, full(wq), full(wo), full(g), full(b), full(wr), full(br)],
        out_specs=(row(D_MODEL),
                   pl.BlockSpec((X_PLANES, tm, LANES), lambda bi, i: (0, bi * nt + i, 0)),
                   pl.BlockSpec((N_EXPERTS, tm), lambda bi, i: (0, bi * nt + i))),
        out_shape=(jax.ShapeDtypeStruct((n, D_MODEL), F32),
                   jax.ShapeDtypeStruct((X_PLANES, n, LANES), jnp.uint32),
                   jax.ShapeDtypeStruct((N_EXPERTS, n), F32)),
        compiler_params=_params(("parallel", "parallel"), 48),
    )(x1, kmem, vmem, wq, wo, g, b, wr, br)


def _route_kernel(lg_ref, idx_ref, rank_ref, gate_ref, cnt_ref, carry_ref):
    i = pl.program_id(0)

    @pl.when(i == 0)
    def _():
        carry_ref[...] = jnp.zeros_like(carry_ref)

    lg = lg_ref[...]
    t = lg.shape[1]
    e_idx = lax.broadcasted_iota(jnp.int32, lg.shape, 0).astype(F32)
    sels, vals, idxs = [], [], []
    for _ in range(TOP_K):
        mx = jnp.max(lg, axis=0, keepdims=True)
        first = jnp.min(jnp.where(lg == mx, e_idx, float(N_EXPERTS)), axis=0, keepdims=True)
        sel = e_idx == first
        sels.append(sel)
        vals.append(mx)
        idxs.append(first)
        lg = jnp.where(sel, -jnp.inf, lg)
    exps = [jnp.exp(v - vals[0]) for v in vals]
    tot = exps[0] + exps[1] + exps[2] + exps[3]

    selmat = (sels[0] | sels[1] | sels[2] | sels[3])
    r = lax.broadcasted_iota(jnp.int32, (t, t), 0)
    s = lax.broadcasted_iota(jnp.int32, (t, t), 1)
    earlier = (r < s).astype(BF16)
    carry = carry_ref[:, 0:1]
    rankmat = _dot(selmat.astype(BF16), earlier) + carry
    new_carry = carry + jnp.sum(selmat.astype(F32), axis=1, keepdims=True)
    carry_ref[...] = jnp.broadcast_to(new_carry, carry_ref.shape)
    cnt_ref[...] = jnp.broadcast_to(new_carry, cnt_ref.shape).astype(jnp.int32)

    row8 = lax.broadcasted_iota(jnp.int32, (8, t), 0)
    row128 = lax.broadcasted_iota(jnp.int32, (LANES, t), 0)
    idx_out = jnp.zeros((8, t), F32)
    rank_out = jnp.zeros((8, t), F32)
    gate_out = jnp.zeros((LANES, t), F32)
    for k in range(TOP_K):
        rk = jnp.sum(jnp.where(sels[k], rankmat, 0.0), axis=0, keepdims=True)
        idx_out = jnp.where(row8 == k, idxs[k], idx_out)
        rank_out = jnp.where(row8 == k, rk, rank_out)
        gate_out = jnp.where(row128 == k, exps[k] / tot, gate_out)
    idx_ref[...] = idx_out.astype(jnp.int32)
    rank_ref[...] = rank_out.astype(jnp.int32)
    gate_ref[...] = gate_out.T


def _route(logits_t):
    n = logits_t.shape[1]
    t = ROUTE_TILE
    col = lambda rows: pl.BlockSpec((rows, t), lambda i: (0, i))
    return pl.pallas_call(
        _route_kernel,
        grid=(n // t,),
        in_specs=[col(N_EXPERTS)],
        out_specs=(col(8), col(8), pl.BlockSpec((t, LANES), lambda i: (i, 0)),
                   pl.BlockSpec((N_EXPERTS, LANES), lambda i: (0, 0))),
        out_shape=(jax.ShapeDtypeStruct((8, n), jnp.int32),
                   jax.ShapeDtypeStruct((8, n), jnp.int32),
                   jax.ShapeDtypeStruct((n, LANES), F32),
                   jax.ShapeDtypeStruct((N_EXPERTS, LANES), jnp.int32)),
        scratch_shapes=[pltpu.VMEM((N_EXPERTS, LANES), F32)],
        compiler_params=_params(("arbitrary",), 32),
    )(logits_t)


def _expert_kernel(blk_e_ref, blk_rows_ref, next_e_ref, x_ref, wgu_hbm, bgu_ref, wd_hbm, bd_ref,
                   y_ref, wgu_f32, wd_f32, wgu_sc, wd_sc, state_ref, sem):
    i = pl.program_id(0)
    e = blk_e_ref[i]
    rows = blk_rows_ref[i]
    g = y_ref.shape[1]

    def weight_copies(expert, slot):
        return (pltpu.make_async_copy(wgu_hbm.at[expert], wgu_f32.at[slot], sem.at[0, slot]),
                pltpu.make_async_copy(wd_hbm.at[expert], wd_f32.at[slot], sem.at[1, slot]))

    @pl.when(i == 0)
    def _():
        state_ref[0] = -1
        state_ref[1] = 0

    @pl.when((rows > 0) & (e != state_ref[0]))
    def _():
        slot = state_ref[1]

        @pl.when(i == 0)
        def _():
            for cp in weight_copies(e, slot):
                cp.start()

        for cp in weight_copies(e, slot):
            cp.wait()
        wgu_sc[...] = wgu_f32[slot].astype(BF16)
        wd_sc[...] = wd_f32[slot].astype(BF16)
        nxt = next_e_ref[i]

        @pl.when(nxt >= 0)
        def _():
            for cp in weight_copies(nxt, 1 - slot):
                cp.start()

        state_ref[0] = e
        state_ref[1] = 1 - slot

    def ffn(m):
        packed = jnp.concatenate([x_ref[j, 0:m, :] for j in range(X_PLANES)], axis=1)
        xb = _unpack_bf16_pairs(packed)
        hids = []
        for c in range(D_EXPERT // EXPERT_CHUNK):
            g0 = c * EXPERT_CHUNK
            l0 = D_EXPERT + g0
            gate = _dot(xb, wgu_sc[:, g0:g0 + EXPERT_CHUNK]) + bgu_ref[0, :, g0:g0 + EXPERT_CHUNK]
            lin = _dot(xb, wgu_sc[:, l0:l0 + EXPERT_CHUNK]) + bgu_ref[0, :, l0:l0 + EXPERT_CHUNK]
            gate = jnp.minimum(gate, SWIGLU_LIMIT)
            lin = jnp.clip(lin, -SWIGLU_LIMIT, SWIGLU_LIMIT)
            hids.append((gate * jax.nn.sigmoid(SWIGLU_ALPHA * gate) * (lin + 1.0)).astype(BF16))
        y = _pack_bf16_pairs(_dot(jnp.concatenate(hids, axis=1), wd_sc[...]) + bd_ref[0])
        for j in range(Y_PLANES):
            y_ref[j, 0:m, :] = y[:, j * LANES:(j + 1) * LANES]

    @pl.when(rows > g // 2)
    def _():
        ffn(g)

    @pl.when((rows > 0) & (rows <= g // 2))
    def _():
        ffn(g // 2)
        y_ref[:, g // 2:, :] = jnp.zeros((Y_PLANES, g // 2, LANES), y_ref.dtype)

    @pl.when(rows == 0)
    def _():
        y_ref[...] = jnp.zeros_like(y_ref)


def _experts(blk_e, blk_rows, next_e, xs, w_gu, b_gu, w_d, b_d):
    p = xs.shape[1]
    g = EXPERT_ROWS
    grid_spec = pltpu.PrefetchScalarGridSpec(
        num_scalar_prefetch=3,
        grid=(p // g,),
        in_specs=[pl.BlockSpec((X_PLANES, g, LANES), lambda i, be, br, ne: (0, i, 0)),
                  pl.BlockSpec(memory_space=pl.ANY),
                  pl.BlockSpec((1, 1, 2 * D_EXPERT), lambda i, be, br, ne: (be[i], 0, 0)),
                  pl.BlockSpec(memory_space=pl.ANY),
                  pl.BlockSpec((1, 1, D_MODEL), lambda i, be, br, ne: (be[i], 0, 0))],
        out_specs=pl.BlockSpec((Y_PLANES, g, LANES), lambda i, be, br, ne: (0, i, 0)),
        scratch_shapes=[pltpu.VMEM((2, D_MODEL, 2 * D_EXPERT), F32),
                        pltpu.VMEM((2, D_EXPERT, D_MODEL), F32),
                        pltpu.VMEM((D_MODEL, 2 * D_EXPERT), BF16),
                        pltpu.VMEM((D_EXPERT, D_MODEL), BF16),
                        pltpu.SMEM((2,), jnp.int32),
                        pltpu.SemaphoreType.DMA((2, 2))],
    )
    return pl.pallas_call(
        _expert_kernel,
        grid_spec=grid_spec,
        out_shape=jax.ShapeDtypeStruct((Y_PLANES, p, LANES), jnp.uint32),
        compiler_params=_params(("arbitrary",), 56),
    )(blk_e, blk_rows, next_e, xs, w_gu, b_gu, w_d, b_d)


def _combine_kernel(y_ref, gate_ref, x_ref, g_ref, b_ref, o_ref):
    gate = gate_ref[...]
    ff = None
    for k in range(TOP_K):
        packed = jnp.concatenate([y_ref[k, j] for j in range(Y_PLANES)], axis=1)
        yk = _unpack_bf16_pairs(packed, F32) * gate[:, k:k + 1]
        ff = yk if ff is None else ff + yk
    o_ref[...] = _layer_norm(DEEPNORM_ALPHA * x_ref[...] + ff, g_ref[...], b_ref[...])


def _combine(yg, gate, x2, g, b):
    n = x2.shape[0]
    tm = SEQ_BLOCK
    row = lambda width: pl.BlockSpec((tm, width), lambda i: (i, 0))
    full = lambda a: pl.BlockSpec(a.shape, lambda i: (0, 0))
    return pl.pallas_call(
        _combine_kernel,
        grid=(n // tm,),
        in_specs=[pl.BlockSpec((TOP_K, Y_PLANES, tm, LANES), lambda i: (0, 0, i, 0)),
                  row(LANES), row(D_MODEL), full(g), full(b)],
        out_specs=row(D_MODEL),
        out_shape=jax.ShapeDtypeStruct((n, D_MODEL), F32),
        compiler_params=_params(("parallel",), 32),
    )(yg, gate, x2, g, b)


SC_WINDOW = 128


def _sc_mesh():
    return plsc.VectorSubcoreMesh(core_axis_name="core", subcore_axis_name="subcore")


def _sc_gather(table, idx):
    m = idx.shape[0]

    @pl.kernel(out_type=jax.ShapeDtypeStruct((m, LANES), table.dtype), mesh=_sc_mesh())
    def gather_kernel(table_hbm, idx_hbm, out_hbm):
        def body(idx_vmem, out_vmem):
            pltpu.sync_copy(table_hbm.at[idx_vmem.at[0]], out_vmem)

        pltpu.emit_pipeline(
            body,
            grid=(m // SC_WINDOW,),
            in_specs=[pl.BlockSpec((1, SC_WINDOW), lambda i: (0, i))],
            out_specs=[pl.BlockSpec((SC_WINDOW, LANES), lambda i: (i, 0))],
            core_axis_name=("core", "subcore"),
            dimension_semantics=(pltpu.PARALLEL,),
        )(idx_hbm, out_hbm)

    return gather_kernel(table, idx.reshape(1, m))


def _sc_scatter(src, idx, out_rows):
    m = idx.shape[0]
    src_blocks = src.shape[0] // SC_WINDOW

    @pl.kernel(out_type=jax.ShapeDtypeStruct((out_rows, LANES), src.dtype), mesh=_sc_mesh())
    def scatter_kernel(src_hbm, idx_hbm, out_hbm):
        def body(src_vmem, idx_vmem):
            pltpu.sync_copy(src_vmem, out_hbm.at[idx_vmem.at[0]])

        pltpu.emit_pipeline(
            body,
            grid=(m // SC_WINDOW,),
            in_specs=[pl.BlockSpec((SC_WINDOW, LANES), lambda i: (i % src_blocks, 0)),
                      pl.BlockSpec((1, SC_WINDOW), lambda i: (0, i))],
            out_specs=[],
            core_axis_name=("core", "subcore"),
            dimension_semantics=(pltpu.PARALLEL,),
        )(src_hbm, idx_hbm)

    return scatter_kernel(src, idx.reshape(1, m))


def _layer(x, mem, w_in, fox_f_bias, conv_w, i_bias, f_bias, fox_g, mlstm_g, w_mix_out,
           ln1_g, ln1_b, w_xq, w_xk, w_xv, w_xo, ln2_g, ln2_b, w_router, b_router,
           w_gate_up, b_gate_up, w_down, b_down, ln3_g, ln3_b):
    batch, seq, d = x.shape
    n_mem = mem.shape[1]
    n = batch * seq
    x2d = x.reshape(n, d)

    o_ff = 3 * FOX_WIDTH
    o_mqk = o_ff + FOX_HEADS
    o_mv = o_mqk + 2 * MLSTM_QK_WIDTH
    o_mi = o_mv + MLSTM_V_WIDTH
    o_mf = o_mi + MLSTM_HEADS
    o_mo = o_mf + MLSTM_HEADS
    n_gate = FOX_HEADS + 2 * MLSTM_HEADS
    w_r = jnp.concatenate(
        [w_in[:, FOX_WIDTH:2 * FOX_WIDTH], w_in[:, o_mqk:o_mi], w_in[:, o_mo:],
         w_in[:, o_ff:o_mqk], w_in[:, o_mi:o_mo],
         jnp.zeros((d, LANES - n_gate), w_in.dtype)], axis=1).astype(BF16)
    w_t = jnp.concatenate([w_in[:, :FOX_WIDTH], w_in[:, 2 * FOX_WIDTH:o_ff]], axis=1).T.astype(BF16)
    gate_bias = jnp.concatenate(
        [fox_f_bias, i_bias, f_bias, jnp.zeros((LANES - n_gate,), F32)]).reshape(1, LANES)

    fqt, fk, fvt, mqk, mv, mo, gates = _inproj(x2d, w_r, w_t, gate_bias)
    gcol, grow, cpieces = _gateprep(gates, batch, seq)
    fo = _fox(fqt, fk, cpieces, fvt, fox_g.reshape(1, FOX_WIDTH), batch, seq)
    mo_out = _mlstm(mqk, mv, mo, gcol, grow, conv_w, mlstm_g.reshape(1, MLSTM_V_WIDTH), batch, seq)
    x1 = _outproj(fo, mo_out, w_mix_out.astype(BF16), x2d, ln1_g.reshape(1, d), ln1_b.reshape(1, d))

    kmem, vmem = _memkv(mem.reshape(batch * n_mem, d), w_xk.astype(BF16), w_xv.astype(BF16), n_mem)
    wrt = w_router.T
    wrt_hi = wrt.astype(BF16)
    wrt_lo = (wrt - wrt_hi.astype(F32)).astype(BF16)
    x2, x2p, logits_t = _xattn(x1, kmem, vmem, w_xq.astype(BF16), w_xo.astype(BF16),
                               ln2_g.reshape(1, d), ln2_b.reshape(1, d),
                               jnp.concatenate([wrt_hi, wrt_lo], axis=0),
                               b_router.reshape(N_EXPERTS, 1), batch, seq, n_mem)

    idx_t, rank_t, gate, cnt = _route(logits_t)
    counts = cnt[:, 0]
    g_rows = EXPERT_ROWS
    padded = ((counts + g_rows - 1) // g_rows) * g_rows
    pad_end = jnp.cumsum(padded)
    pad_start = pad_end - padded
    experts = jnp.arange(N_EXPERTS, dtype=jnp.int32)
    sel = idx_t[:TOP_K, :, None] == experts[None, None, :]
    pos_t = jnp.sum(jnp.where(sel, pad_start[None, None, :], 0), axis=-1) + rank_t[:TOP_K]
    p_rows = n * TOP_K + N_EXPERTS * g_rows
    nb = p_rows // g_rows
    blk_start = jnp.arange(nb, dtype=jnp.int32) * g_rows
    blk_e = jnp.minimum(jnp.sum((pad_end[None, :] <= blk_start[:, None]).astype(jnp.int32), axis=1),
                        N_EXPERTS - 1)
    own = blk_e[:, None] == experts[None, :]
    row_end = jnp.sum(jnp.where(own, (pad_start + counts)[None, :], 0), axis=1)
    blk_rows = jnp.clip(row_end - blk_start, 0, g_rows).astype(jnp.int32)
    later = (experts[None, :] > blk_e[:, None]) & (counts[None, :] > 0)
    next_e = jnp.min(jnp.where(later, experts[None, :], N_EXPERTS), axis=1)
    next_e = jnp.where(next_e == N_EXPERTS, -1, next_e).astype(jnp.int32)

    def piece_index(planes):
        off = jnp.arange(planes, dtype=jnp.int32) * p_rows
        return (pos_t[:, None, :] + off[None, :, None]).reshape(-1)

    xs = _sc_scatter(x2p.reshape(X_PLANES * n, LANES), piece_index(X_PLANES), X_PLANES * p_rows)
    y = _experts(blk_e, blk_rows, next_e, xs.reshape(X_PLANES, p_rows, LANES), w_gate_up,
                 b_gate_up.reshape(N_EXPERTS, 1, -1), w_down, b_down.reshape(N_EXPERTS, 1, -1))
    yg = _sc_gather(y.reshape(Y_PLANES * p_rows, LANES), piece_index(Y_PLANES))
    out = _combine(yg.reshape(TOP_K, Y_PLANES, n, LANES), gate, x2,
                   ln3_g.reshape(1, d), ln3_b.reshape(1, d))
    return out.reshape(batch, seq, d)


def kernel(x, mem, w_in, fox_f_bias, mlstm_conv_w, mlstm_i_bias, mlstm_f_bias, fox_norm_g, mlstm_norm_g, w_mix_out, ln1_g, ln1_b, w_xq, w_xk, w_xv, w_xo, ln2_g, ln2_b, w_router, b_router, w_gate_up, b_gate_up, w_down, b_down, ln3_g, ln3_b):
    for l in range(w_in.shape[0]):
        x = _layer(x, mem, w_in[l], fox_f_bias[l], mlstm_conv_w[l], mlstm_i_bias[l],
                   mlstm_f_bias[l], fox_norm_g[l], mlstm_norm_g[l], w_mix_out[l],
                   ln1_g[l], ln1_b[l], w_xq[l], w_xk[l], w_xv[l], w_xo[l], ln2_g[l], ln2_b[l],
                   w_router[l], b_router[l], w_gate_up[l], b_gate_up[l], w_down[l], b_down[l],
                   ln3_g[l], ln3_b[l])
    return x
```

```python
import jax
import jax.numpy as jnp
from jax import lax
from jax.experimental import pallas as pl
from jax.experimental.pallas import tpu as pltpu
from jax.experimental.pallas import tpu_sc as plsc

F32 = jnp.float32
BF16 = jnp.bfloat16

D_MODEL = 1024
FOX_HEADS = 8
FOX_HEAD_DIM = 64
FOX_WIDTH = FOX_HEADS * FOX_HEAD_DIM
MLSTM_HEADS = 4
MLSTM_QK_DIM = 64
MLSTM_V_DIM = 128
MLSTM_QK_WIDTH = MLSTM_HEADS * MLSTM_QK_DIM
MLSTM_V_WIDTH = MLSTM_HEADS * MLSTM_V_DIM
CONV_WIDTH = 4
XATTN_HEADS = 4
XATTN_HEAD_DIM = D_MODEL // XATTN_HEADS
N_EXPERTS = 32
TOP_K = 4
D_EXPERT = D_MODEL
SWIGLU_LIMIT = 7.0
SWIGLU_ALPHA = 1.702
DEEPNORM_ALPHA = 2.0 ** 0.25
LN_EPS = 1e-5
RMS_EPS = 1e-6

LANES = 128
SEQ_BLOCK = 256
ROW_TILE = 512
EXPERT_ROWS = 512
EXPERT_CHUNK = 256
ROUTE_TILE = 512
X_PLANES = D_MODEL // 2 // LANES
Y_PLANES = X_PLANES
GATE_I0 = FOX_HEADS
GATE_F0 = FOX_HEADS + MLSTM_HEADS

MIB = 1024 * 1024


def _params(semantics, vmem_mib):
    return pltpu.CompilerParams(dimension_semantics=semantics,
                                vmem_limit_bytes=vmem_mib * MIB)


def _layer_norm(y, g, b):
    mu = jnp.mean(y, axis=-1, keepdims=True)
    yc = y - mu
    var = jnp.mean(yc * yc, axis=-1, keepdims=True)
    return yc * lax.rsqrt(var + LN_EPS) * g + b


def _dot(a, b):
    return jnp.dot(a, b, preferred_element_type=F32)


def _dot_nt(a, b):
    return lax.dot_general(a, b, (((1,), (1,)), ((), ())), preferred_element_type=F32)


def _dot_tn(a, b):
    return lax.dot_general(a, b, (((0,), (0,)), ((), ())), preferred_element_type=F32)


def _pack_bf16_pairs(x):
    w = x.shape[1] // 2
    lo = pltpu.bitcast(x[:, :w].astype(BF16).astype(F32), jnp.uint32)
    hi = pltpu.bitcast(x[:, w:].astype(BF16).astype(F32), jnp.uint32)
    return (lo >> 16) | hi


def _unpack_bf16_pairs(u, dtype=BF16):
    lo = pltpu.bitcast(u << 16, F32).astype(dtype)
    hi = pltpu.bitcast(u & jnp.uint32(0xFFFF0000), F32).astype(dtype)
    return jnp.concatenate([lo, hi], axis=1)


def _inproj_kernel(x_ref, w_ref, wt_ref, gb_ref, fqt_ref, fk_ref, fvt_ref, mqk_ref, mvt_ref,
                   mot_ref, g_ref):
    xb = x_ref[...].astype(BF16)

    def mm(c0, width):
        return _dot(xb, w_ref[:, c0:c0 + width])

    def mm_t(r0, height):
        return _dot_nt(wt_ref[r0:r0 + height, :], xb)

    r_fv = FOX_WIDTH
    r_mv = r_fv + FOX_WIDTH
    r_mo = r_mv + MLSTM_V_WIDTH
    fqt_ref[...] = (mm_t(0, FOX_WIDTH) * (FOX_HEAD_DIM ** -0.5)).astype(BF16)
    fvt_ref[...] = mm_t(r_fv, FOX_WIDTH).astype(BF16)
    mvt_ref[...] = mm_t(r_mv, MLSTM_V_WIDTH).astype(BF16)
    mot_ref[...] = mm_t(r_mo, MLSTM_V_WIDTH)
    c_qk = FOX_WIDTH
    c_g = c_qk + 2 * MLSTM_QK_WIDTH
    fk_ref[...] = mm(0, FOX_WIDTH).astype(BF16)
    mqk_ref[...] = mm(c_qk, 2 * MLSTM_QK_WIDTH)
    g_ref[...] = mm(c_g, LANES) + gb_ref[...]


def _inproj(x2d, w_r, w_t, gate_bias):
    n = x2d.shape[0]
    tm = ROW_TILE
    row = lambda width: pl.BlockSpec((tm, width), lambda i: (i, 0))
    col = lambda height: pl.BlockSpec((height, tm), lambda i: (0, i))
    full = lambda a: pl.BlockSpec(a.shape, lambda i: (0, 0))
    out_shapes = (
        jax.ShapeDtypeStruct((FOX_WIDTH, n), BF16),
        jax.ShapeDtypeStruct((n, FOX_WIDTH), BF16),
        jax.ShapeDtypeStruct((FOX_WIDTH, n), BF16),
        jax.ShapeDtypeStruct((n, 2 * MLSTM_QK_WIDTH), F32),
        jax.ShapeDtypeStruct((MLSTM_V_WIDTH, n), BF16),
        jax.ShapeDtypeStruct((MLSTM_V_WIDTH, n), F32),
        jax.ShapeDtypeStruct((n, LANES), F32),
    )
    return pl.pallas_call(
        _inproj_kernel,
        grid=(n // tm,),
        in_specs=[row(D_MODEL), full(w_r), full(w_t), full(gate_bias)],
        out_specs=(col(FOX_WIDTH), row(FOX_WIDTH), col(FOX_WIDTH), row(2 * MLSTM_QK_WIDTH),
                   col(MLSTM_V_WIDTH), col(MLSTM_V_WIDTH), row(LANES)),
        out_shape=out_shapes,
        compiler_params=_params(("parallel",), 48),
    )(x2d, w_r, w_t, gate_bias)


def _split3(x):
    hi = x.astype(BF16)
    r1 = x - hi.astype(F32)
    mid = r1.astype(BF16)
    lo = (r1 - mid.astype(F32)).astype(BF16)
    return hi, mid, lo


def _gateprep_kernel(g_ref, sel_ref, col_ref, row_ref, cp_ref, carry_ref):
    c = pl.program_id(1)

    @pl.when(c == 0)
    def _():
        carry_ref[...] = jnp.zeros_like(carry_ref)

    g = g_ref[...]
    lane = lax.broadcasted_iota(jnp.int32, g.shape, 1)
    is_i = (lane >= GATE_I0) & (lane < GATE_F0)
    logsig = jnp.minimum(g, 0.0) - jnp.log1p(jnp.exp(-jnp.abs(g)))
    blk = g.shape[0]
    r = lax.broadcasted_iota(jnp.int32, (blk, blk), 0)
    s = lax.broadcasted_iota(jnp.int32, (blk, blk), 1)
    tri = (s <= r).astype(BF16)
    cs = None
    for piece in _split3(logsig):
        term = _dot(tri, piece)
        cs = term if cs is None else cs + term
    carry = carry_ref[0:1, :]
    glob = cs + carry
    carry_ref[...] = jnp.broadcast_to(glob[blk - 1:blk, :], carry_ref.shape)
    out = jnp.where(lane < GATE_I0, glob, jnp.where(is_i, g, cs))
    col_ref[...] = out
    row_ref[0] = out.T[0:16, :]

    pieces = jnp.concatenate(_split3(-glob), axis=1)
    moved = _dot(pieces, sel_ref[...])
    for p in range(FOX_HEADS // 2):
        cp_ref[0, p] = moved[:, p * LANES:(p + 1) * LANES].astype(BF16)


def _piece_selector():
    src = jnp.arange(3 * LANES, dtype=jnp.int32)[:, None]
    dst = jnp.arange(4 * LANES, dtype=jnp.int32)[None, :]
    piece, head = src // LANES, src % LANES
    pair, lane = dst // LANES, dst % LANES
    hit = (lane < 6) & (lane % 3 == piece) & (head == 2 * pair + lane // 3)
    return hit.astype(BF16)


def _gateprep(gates, batch, seq):
    n = gates.shape[0]
    nc = seq // SEQ_BLOCK
    sel = _piece_selector()
    return pl.pallas_call(
        _gateprep_kernel,
        grid=(batch, nc),
        in_specs=[pl.BlockSpec((SEQ_BLOCK, LANES), lambda b, c: (b * nc + c, 0)),
                  pl.BlockSpec(sel.shape, lambda b, c: (0, 0))],
        out_specs=(pl.BlockSpec((SEQ_BLOCK, LANES), lambda b, c: (b * nc + c, 0)),
                   pl.BlockSpec((1, 16, SEQ_BLOCK), lambda b, c: (b, 0, c)),
                   pl.BlockSpec((1, FOX_HEADS // 2, SEQ_BLOCK, LANES), lambda b, c: (b, 0, c, 0))),
        out_shape=(jax.ShapeDtypeStruct((n, LANES), F32),
                   jax.ShapeDtypeStruct((batch, 16, seq), F32),
                   jax.ShapeDtypeStruct((batch, FOX_HEADS // 2, seq, LANES), BF16)),
        scratch_shapes=[pltpu.VMEM((8, LANES), F32)],
        compiler_params=_params(("parallel", "arbitrary"), 32),
    )(gates, sel)


FOX_ONES_ROWS = 16
FOX_VAUG_ROWS = FOX_HEAD_DIM + FOX_ONES_ROWS
FOX_QUERY_TILE = 2 * SEQ_BLOCK


def _fox_kernel(qt_ref, k_ref, cp_ref, vt_ref, gain_ref, o_ref,
                kaug_sc, vaug_sc, qaug_sc, sta_sc, stb_sc, m_sc, acc0_sc, acc1_sc):
    qi = pl.program_id(2)
    tq = qt_ref.shape[1]
    tg = tq
    tk = SEQ_BLOCK
    hd = FOX_HEAD_DIM
    seq = k_ref.shape[0]

    @pl.when(qi == 0)
    def _():
        lane = lax.broadcasted_iota(jnp.int32, (tk, LANES), 1)

        def build(blk, carry):
            r0 = pl.multiple_of(blk * tk, tk)
            kp = k_ref[pl.ds(r0, tk), :].astype(F32)
            cp = cp_ref[0, 0, pl.ds(r0, tk), :].astype(F32)
            for h in range(2):
                kh = kp if h == 0 else pltpu.roll(kp, hd, axis=1)
                ch = pltpu.roll(cp, hd - 3 * h, axis=1)
                kaug = jnp.where(lane < hd, kh, jnp.where(lane < hd + 3, ch, 0.0))
                kaug_sc[h, pl.ds(r0, tk), :] = kaug.astype(BF16)
            return carry

        lax.fori_loop(0, seq // tk, build, 0)
        for h in range(2):
            vaug_sc[h, 0:hd, :] = vt_ref[h * hd:(h + 1) * hd, :]
            vaug_sc[h, hd:, :] = jnp.ones((FOX_ONES_ROWS, seq), BF16)

    ones3 = (lax.broadcasted_iota(jnp.int32, (FOX_ONES_ROWS, tq), 0) < 3).astype(BF16)
    for h in range(2):
        qaug_sc[h, 0:hd, :] = qt_ref[h * hd:(h + 1) * hd, :]
        qaug_sc[h, hd:hd + FOX_ONES_ROWS, :] = ones3
        qaug_sc[h, hd + FOX_ONES_ROWS:, :] = jnp.zeros((LANES - hd - FOX_ONES_ROWS, tq), BF16)

    def put_scores(kg, slot, masked=False):
        k0 = pl.multiple_of(kg * tg, tg)
        for h in range(2):
            st = _dot(kaug_sc[h, pl.ds(k0, tg), :], qaug_sc[h])
            if masked:
                kk = lax.broadcasted_iota(jnp.int32, (tg, tq), 0)
                tt = lax.broadcasted_iota(jnp.int32, (tg, tq), 1)
                st = jnp.where(kk <= tt, st, -jnp.inf)
            slot[h] = st

    m_sc[...] = jnp.full(m_sc.shape, -jnp.inf, F32)
    acc0_sc[...] = jnp.zeros(acc0_sc.shape, F32)
    acc1_sc[...] = jnp.zeros(acc1_sc.shape, F32)
    acc = (acc0_sc, acc1_sc)

    def absorb(kg, slot):
        k0 = pl.multiple_of(kg * tg, tg)
        half = tg // 2
        for h in range(2):
            st = slot[h]
            m_prev = m_sc[h]
            m_new = jnp.maximum(m_prev, jnp.max(st, axis=0, keepdims=True))
            pv = None
            for u in range(2):
                p = jnp.exp(st[u * half:(u + 1) * half] - m_new)
                part = _dot(vaug_sc[h, :, pl.ds(k0 + u * half, half)], p.astype(BF16))
                pv = part if pv is None else pv + part
            acc[h][...] = jnp.exp(m_prev - m_new) * acc[h][...] + pv
            m_sc[h] = m_new

    sa, sb = sta_sc, stb_sc
    n_loop = jnp.maximum(qi - 1, 0) // 2

    @pl.when(qi > 0)
    def _():
        put_scores(0, sa)

    def body(j, carry):
        put_scores(2 * j + 1, sb)
        absorb(2 * j, sa)
        put_scores(2 * j + 2, sa)
        absorb(2 * j + 1, sb)
        return carry

    lax.fori_loop(0, n_loop, body, 0)
    done = 2 * n_loop

    @pl.when(qi == 0)
    def _():
        put_scores(qi, sa, masked=True)
        absorb(qi, sa)

    @pl.when((qi > 0) & (qi - done == 1))
    def _():
        put_scores(qi, sb, masked=True)
        absorb(done, sa)
        absorb(qi, sb)

    @pl.when((qi > 0) & (qi - done == 2))
    def _():
        put_scores(done + 1, sb)
        absorb(done, sa)
        put_scores(qi, sa, masked=True)
        absorb(done + 1, sb)
        absorb(qi, sa)

    a0 = acc0_sc[...]
    a1 = acc1_sc[...]
    ot = jnp.concatenate([a0[0:hd] / a0[hd:hd + 1], a1[0:hd] / a1[hd:hd + 1]], axis=0)
    o = ot.T
    lane = lax.broadcasted_iota(jnp.int32, (tq, LANES), 1)
    lo = lane < FOX_HEAD_DIM
    sq = o * o
    ss0 = jnp.sum(jnp.where(lo, sq, 0.0), axis=-1, keepdims=True)
    ss1 = jnp.sum(jnp.where(lo, 0.0, sq), axis=-1, keepdims=True)
    inv = jnp.where(lo, lax.rsqrt(ss0 / FOX_HEAD_DIM + RMS_EPS),
                    lax.rsqrt(ss1 / FOX_HEAD_DIM + RMS_EPS))
    o_ref[...] = (o * inv * gain_ref[...]).astype(o_ref.dtype)


def _fox(fqt, fk, cpieces, fvt, fox_gain, batch, seq):
    n = fk.shape[0]
    tq = FOX_QUERY_TILE
    nq = seq // tq
    npair = FOX_HEADS // 2
    return pl.pallas_call(
        _fox_kernel,
        grid=(batch, npair, nq),
        in_specs=[pl.BlockSpec((LANES, tq), lambda b, hp, qi: (hp, b * nq + qi)),
                  pl.BlockSpec((seq, LANES), lambda b, hp, qi: (b, hp)),
                  pl.BlockSpec((1, 1, seq, LANES), lambda b, hp, qi: (b, hp, 0, 0)),
                  pl.BlockSpec((LANES, seq), lambda b, hp, qi: (hp, b)),
                  pl.BlockSpec((1, LANES), lambda b, hp, qi: (0, hp))],
        out_specs=pl.BlockSpec((tq, LANES), lambda b, hp, qi: (b * nq + qi, hp)),
        out_shape=jax.ShapeDtypeStruct((n, FOX_WIDTH), BF16),
        scratch_shapes=[pltpu.VMEM((2, seq, LANES), BF16),
                        pltpu.VMEM((2, FOX_VAUG_ROWS, seq), BF16),
                        pltpu.VMEM((2, LANES, tq), BF16),
                        pltpu.VMEM((2, tq, tq), F32),
                        pltpu.VMEM((2, tq, tq), F32),
                        pltpu.VMEM((2, 1, tq), F32),
                        pltpu.VMEM((FOX_VAUG_ROWS, tq), F32),
                        pltpu.VMEM((FOX_VAUG_ROWS, tq), F32)],
        compiler_params=_params(("parallel", "parallel", "arbitrary"), 48),
    )(fqt, fk, cpieces, fvt, fox_gain)


MLSTM_ONES_ROWS = 16


def _mlstm_kernel(qk_ref, vt_ref, ogt_ref, col_ref, row_ref, cw_ref, gain_ref, o_ref,
                  tail_ref, buf_ref, c_sc, m_sc):
    c = pl.program_id(1)
    L = SEQ_BLOCK

    @pl.when(c == 0)
    def _():
        tail_ref[...] = jnp.zeros_like(tail_ref)
        c_sc[...] = jnp.zeros_like(c_sc)
        m_sc[...] = jnp.zeros_like(m_sc)

    x = qk_ref[...]
    buf_ref[0:8, :] = tail_ref[...]
    buf_ref[8:8 + L, :] = x
    tail_ref[...] = x[L - 8:L, :]
    y = x * cw_ref[CONV_WIDTH - 1:CONV_WIDTH, :]
    for j in range(CONV_WIDTH - 1):
        shift = CONV_WIDTH - 1 - j
        y = y + buf_ref[8 - shift:8 - shift + L, :] * cw_ref[j:j + 1, :]
    y = y * jax.nn.sigmoid(y)
    kf = y[:, MLSTM_QK_WIDTH:] * (MLSTM_QK_DIM ** -0.5)
    n_pair = MLSTM_HEADS // 2
    qt = [y[:, p * LANES:(p + 1) * LANES].T for p in range(n_pair)]
    kb = [kf[:, p * LANES:(p + 1) * LANES].astype(BF16) for p in range(n_pair)]

    col = col_ref[...]
    row = row_ref[0]
    lane = lax.broadcasted_iota(jnp.int32, (L, LANES), 1)
    sub = lax.broadcasted_iota(jnp.int32, (LANES, L), 0)
    ss = lax.broadcasted_iota(jnp.int32, (L, L), 0)
    ll = lax.broadcasted_iota(jnp.int32, (L, L), 1)
    causal = ss <= ll
    ones_rows = jnp.ones((MLSTM_ONES_ROWS, L), BF16)

    for h in range(MLSTM_HEADS):
        pair, half = divmod(h, 2)
        head_sub = (sub < MLSTM_QK_DIM) if half == 0 else (sub >= MLSTM_QK_DIM)
        head_lane = (lane < MLSTM_QK_DIM) if half == 0 else (lane >= MLSTM_QK_DIM)
        qth = jnp.where(head_sub, qt[pair], 0.0).astype(BF16)
        rcol = col[:, GATE_I0 + h:GATE_I0 + h + 1] - col[:, GATE_F0 + h:GATE_F0 + h + 1]
        brow = row[GATE_F0 + h:GATE_F0 + h + 1, :]
        lirow = row[GATE_I0 + h:GATE_I0 + h + 1, :]
        g = brow[:, L - 1:L]
        m_prev = m_sc[h][0:1, 0:1]

        dt = jnp.where(causal, rcol + brow, -jnp.inf)
        inter_log = brow + m_prev
        m_t = jnp.maximum(inter_log, jnp.max(dt, axis=0, keepdims=True))
        w_inter = jnp.exp(inter_log - m_t)
        pt = jnp.exp(dt - m_t) * _dot(kb[pair], qth)
        vaug = jnp.concatenate([vt_ref[h * LANES:(h + 1) * LANES, :], ones_rows], axis=0)
        cstate = c_sc[h]
        tot = w_inter * _dot(cstate.astype(BF16), qth) + _dot(vaug, pt.astype(BF16))
        den = tot[MLSTM_V_DIM:MLSTM_V_DIM + 1]
        hout = tot[0:MLSTM_V_DIM] * (1.0 / jnp.maximum(jnp.abs(den), jnp.exp(-m_t)))

        a = g + (lirow - brow)
        m_loc = jnp.max(a, axis=1, keepdims=True)
        vw = (vaug.astype(F32) * jnp.exp(a - m_loc)).astype(BF16)
        kmask = jnp.where(head_lane, kb[pair], jnp.zeros_like(kb[pair]))
        kv = _dot(vw, kmask)
        m_new = jnp.maximum(g + m_prev, m_loc)
        c_sc[h] = jnp.exp(g + m_prev - m_new) * cstate + jnp.exp(m_loc - m_new) * kv
        m_sc[h] = jnp.broadcast_to(m_new, m_sc.shape[1:])

        ms = jnp.mean(hout * hout, axis=0, keepdims=True)
        gain = gain_ref[h * LANES:(h + 1) * LANES, :]
        hn = hout * lax.rsqrt(ms + RMS_EPS) * jnp.concatenate([gain] * (L // LANES), axis=1)
        gate = jax.nn.sigmoid(ogt_ref[h * LANES:(h + 1) * LANES, :])
        o_ref[:, h * LANES:(h + 1) * LANES] = (hn * gate).T.astype(o_ref.dtype)


def _mlstm(mqk, mvt, mot, gcol, grow, conv_w, gain_lanes, batch, seq):
    n = mqk.shape[0]
    L = SEQ_BLOCK
    nc = seq // L
    qk_width = 2 * MLSTM_QK_WIDTH
    row = lambda width: pl.BlockSpec((L, width), lambda b, c: (b * nc + c, 0))
    col = pl.BlockSpec((MLSTM_V_WIDTH, L), lambda b, c: (0, b * nc + c))
    full = lambda a: pl.BlockSpec(a.shape, lambda b, c: (0, 0))
    return pl.pallas_call(
        _mlstm_kernel,
        grid=(batch, nc),
        in_specs=[row(qk_width), col, col, row(LANES),
                  pl.BlockSpec((1, 16, L), lambda b, c: (b, 0, c)),
                  full(conv_w), full(gain_lanes)],
        out_specs=row(MLSTM_V_WIDTH),
        out_shape=jax.ShapeDtypeStruct((n, MLSTM_V_WIDTH), BF16),
        scratch_shapes=[pltpu.VMEM((8, qk_width), F32), pltpu.VMEM((8 + L, qk_width), F32),
                        pltpu.VMEM((MLSTM_HEADS, MLSTM_V_DIM + MLSTM_ONES_ROWS, LANES), F32),
                        pltpu.VMEM((MLSTM_HEADS, 8, LANES), F32)],
        compiler_params=_params(("parallel", "arbitrary"), 48),
    )(mqk, mvt, mot, gcol, grow, conv_w, gain_lanes)


def _outproj_kernel(fo_ref, mo_ref, w_ref, x_ref, g_ref, b_ref, o_ref):
    mix = _dot(fo_ref[...], w_ref[0:FOX_WIDTH, :]) + _dot(mo_ref[...], w_ref[FOX_WIDTH:, :])
    o_ref[...] = _layer_norm(DEEPNORM_ALPHA * x_ref[...] + mix, g_ref[...], b_ref[...])


def _outproj(fo, mo, w_out, x2d, g, b):
    n = x2d.shape[0]
    tm = ROW_TILE
    row = lambda width: pl.BlockSpec((tm, width), lambda i: (i, 0))
    full = lambda a: pl.BlockSpec(a.shape, lambda i: (0, 0))
    return pl.pallas_call(
        _outproj_kernel,
        grid=(n // tm,),
        in_specs=[row(512), row(512), full(w_out), row(D_MODEL), full(g), full(b)],
        out_specs=row(D_MODEL),
        out_shape=jax.ShapeDtypeStruct((n, D_MODEL), F32),
        compiler_params=_params(("parallel",), 48),
    )(fo, mo, w_out, x2d, g, b)


def _memkv_kernel(mem_ref, wk_ref, wv_ref, k_ref, v_ref):
    mb = mem_ref[...].astype(BF16)
    k_ref[...] = (_dot(mb, wk_ref[...]) * (XATTN_HEAD_DIM ** -0.5)).astype(BF16)
    v_ref[...] = _dot(mb, wv_ref[...]).astype(BF16)


def _memkv(mem2d, wk, wv, n_mem):
    n = mem2d.shape[0]
    row = pl.BlockSpec((n_mem, D_MODEL), lambda i: (i, 0))
    full = lambda a: pl.BlockSpec(a.shape, lambda i: (0, 0))
    return pl.pallas_call(
        _memkv_kernel,
        grid=(n // n_mem,),
        in_specs=[row, full(wk), full(wv)],
        out_specs=(row, row),
        out_shape=(jax.ShapeDtypeStruct((n, D_MODEL), BF16),) * 2,
        compiler_params=_params(("parallel",), 32),
    )(mem2d, wk, wv)


def _xattn_kernel(x_ref, k_ref, v_ref, wq_ref, wo_ref, g_ref, b_ref, wr_ref, br_ref,
                  o_ref, ob_ref, lg_ref):
    half = x_ref.shape[0] // 2
    for r in range(2):
        rows = slice(r * half, (r + 1) * half)
        x = x_ref[rows, :]
        q = _dot(x.astype(BF16), wq_ref[...]).astype(BF16)
        outs = []
        for h in range(XATTN_HEADS):
            sl = slice(h * XATTN_HEAD_DIM, (h + 1) * XATTN_HEAD_DIM)
            s = _dot_nt(q[:, sl], k_ref[:, sl])
            p = jnp.exp(s - jnp.max(s, axis=-1, keepdims=True))
            l = jnp.sum(p, axis=-1, keepdims=True)
            outs.append((_dot(p.astype(BF16), v_ref[:, sl]) / l).astype(BF16))
        o = jnp.concatenate(outs, axis=1)
        xa = _dot(o, wo_ref[...])
        x2 = _layer_norm(DEEPNORM_ALPHA * x + xa, g_ref[...], b_ref[...])
        o_ref[rows, :] = x2
        packed = _pack_bf16_pairs(x2)
        for j in range(ob_ref.shape[0]):
            ob_ref[j, rows, :] = packed[:, j * LANES:(j + 1) * LANES]
        x2h = x2.astype(BF16)
        x2l = (x2 - x2h.astype(F32)).astype(BF16)
        a = _dot_nt(wr_ref[...], x2h)
        b = _dot_nt(wr_ref[0:N_EXPERTS, :], x2l)
        lg_ref[:, rows] = a[0:N_EXPERTS] + a[N_EXPERTS:] + b + br_ref[...]


def _xattn(x1, kmem, vmem, wq, wo, g, b, wr, br, batch, seq, n_mem):
    n = x1.shape[0]
    tm = ROW_TILE
    nt = seq // tm
    row = lambda width: pl.BlockSpec((tm, width), lambda bi, i: (bi * nt + i, 0))
    full = lambda a: pl.BlockSpec(a.shape, lambda bi, i: (0, 0))
    kv = pl.BlockSpec((n_mem, D_MODEL), lambda bi, i: (bi, 0))
    return pl.pallas_call(
        _xattn_kernel,
        grid=(batch, nt),
        in_specs=[row(D_MODEL), kv, kv, full(wq), full(wo), full(g), full(b), full(wr), full(br)],
        out_specs=(row(D_MODEL),
                   pl.BlockSpec((X_PLANES, tm, LANES), lambda bi, i: (0, bi * nt + i, 0)),
                   pl.BlockSpec((N_EXPERTS, tm), lambda bi, i: (0, bi * nt + i))),
        out_shape=(jax.ShapeDtypeStruct((n, D_MODEL), F32),
                   jax.ShapeDtypeStruct((X_PLANES, n, LANES), jnp.uint32),
                   jax.ShapeDtypeStruct((N_EXPERTS, n), F32)),
        compiler_params=_params(("parallel", "parallel"), 48),
    )(x1, kmem, vmem, wq, wo, g, b, wr, br)


def _route_kernel(lg_ref, idx_ref, rank_ref, gate_ref, cnt_ref, carry_ref):
    i = pl.program_id(0)

    @pl.when(i == 0)
    def _():
        carry_ref[...] = jnp.zeros_like(carry_ref)

    lg = lg_ref[...]
    t = lg.shape[1]
    e_idx = lax.broadcasted_iota(jnp.int32, lg.shape, 0).astype(F32)
    sels, vals, idxs = [], [], []
    for _ in range(TOP_K):
        mx = jnp.max(lg, axis=0, keepdims=True)
        first = jnp.min(jnp.where(lg == mx, e_idx, float(N_EXPERTS)), axis=0, keepdims=True)
        sel = e_idx == first
        sels.append(sel)
        vals.append(mx)
        idxs.append(first)
        lg = jnp.where(sel, -jnp.inf, lg)
    exps = [jnp.exp(v - vals[0]) for v in vals]
    tot = exps[0] + exps[1] + exps[2] + exps[3]

    selmat = (sels[0] | sels[1] | sels[2] | sels[3])
    r = lax.broadcasted_iota(jnp.int32, (t, t), 0)
    s = lax.broadcasted_iota(jnp.int32, (t, t), 1)
    earlier = (r < s).astype(BF16)
    carry = carry_ref[:, 0:1]
    rankmat = _dot(selmat.astype(BF16), earlier) + carry
    new_carry = carry + jnp.sum(selmat.astype(F32), axis=1, keepdims=True)
    carry_ref[...] = jnp.broadcast_to(new_carry, carry_ref.shape)
    cnt_ref[...] = jnp.broadcast_to(new_carry, cnt_ref.shape).astype(jnp.int32)

    row8 = lax.broadcasted_iota(jnp.int32, (8, t), 0)
    row128 = lax.broadcasted_iota(jnp.int32, (LANES, t), 0)
    idx_out = jnp.zeros((8, t), F32)
    rank_out = jnp.zeros((8, t), F32)
    gate_out = jnp.zeros((LANES, t), F32)
    for k in range(TOP_K):
        rk = jnp.sum(jnp.where(sels[k], rankmat, 0.0), axis=0, keepdims=True)
        idx_out = jnp.where(row8 == k, idxs[k], idx_out)
        rank_out = jnp.where(row8 == k, rk, rank_out)
        gate_out = jnp.where(row128 == k, exps[k] / tot, gate_out)
    idx_ref[...] = idx_out.astype(jnp.int32)
    rank_ref[...] = rank_out.astype(jnp.int32)
    gate_ref[...] = gate_out.T


def _route(logits_t):
    n = logits_t.shape[1]
    t = ROUTE_TILE
    col = lambda rows: pl.BlockSpec((rows, t), lambda i: (0, i))
    return pl.pallas_call(
        _route_kernel,
        grid=(n // t,),
        in_specs=[col(N_EXPERTS)],
        out_specs=(col(8), col(8), pl.BlockSpec((t, LANES), lambda i: (i, 0)),
                   pl.BlockSpec((N_EXPERTS, LANES), lambda i: (0, 0))),
        out_shape=(jax.ShapeDtypeStruct((8, n), jnp.int32),
                   jax.ShapeDtypeStruct((8, n), jnp.int32),
                   jax.ShapeDtypeStruct((n, LANES), F32),
                   jax.ShapeDtypeStruct((N_EXPERTS, LANES), jnp.int32)),
        scratch_shapes=[pltpu.VMEM((N_EXPERTS, LANES), F32)],
        compiler_params=_params(("arbitrary",), 32),
    )(logits_t)


def _expert_kernel(blk_e_ref, blk_rows_ref, next_e_ref, x_ref, wgu_hbm, bgu_ref, wd_hbm, bd_ref,
                   y_ref, wgu_f32, wd_f32, wgu_sc, wd_sc, state_ref, sem):
    i = pl.program_id(0)
    e = blk_e_ref[i]
    rows = blk_rows_ref[i]
    g = y_ref.shape[1]

    def weight_copies(expert, slot):
        return (pltpu.make_async_copy(wgu_hbm.at[expert], wgu_f32.at[slot], sem.at[0, slot]),
                pltpu.make_async_copy(wd_hbm.at[expert], wd_f32.at[slot], sem.at[1, slot]))

    @pl.when(i == 0)
    def _():
        state_ref[0] = -1
        state_ref[1] = 0

    @pl.when((rows > 0) & (e != state_ref[0]))
    def _():
        slot = state_ref[1]

        @pl.when(i == 0)
        def _():
            for cp in weight_copies(e, slot):
                cp.start()

        for cp in weight_copies(e, slot):
            cp.wait()
        wgu_sc[...] = wgu_f32[slot].astype(BF16)
        wd_sc[...] = wd_f32[slot].astype(BF16)
        nxt = next_e_ref[i]

        @pl.when(nxt >= 0)
        def _():
            for cp in weight_copies(nxt, 1 - slot):
                cp.start()

        state_ref[0] = e
        state_ref[1] = 1 - slot

    def ffn(m):
        packed = jnp.concatenate([x_ref[j, 0:m, :] for j in range(X_PLANES)], axis=1)
        xb = _unpack_bf16_pairs(packed)
        hids = []
        for c in range(D_EXPERT // EXPERT_CHUNK):
            g0 = c * EXPERT_CHUNK
            l0 = D_EXPERT + g0
            gate = _dot(xb, wgu_sc[:, g0:g0 + EXPERT_CHUNK]) + bgu_ref[0, :, g0:g0 + EXPERT_CHUNK]
            lin = _dot(xb, wgu_sc[:, l0:l0 + EXPERT_CHUNK]) + bgu_ref[0, :, l0:l0 + EXPERT_CHUNK]
            gate = jnp.minimum(gate, SWIGLU_LIMIT)
            lin = jnp.clip(lin, -SWIGLU_LIMIT, SWIGLU_LIMIT)
            hids.append((gate * jax.nn.sigmoid(SWIGLU_ALPHA * gate) * (lin + 1.0)).astype(BF16))
        y = _pack_bf16_pairs(_dot(jnp.concatenate(hids, axis=1), wd_sc[...]) + bd_ref[0])
        for j in range(Y_PLANES):
            y_ref[j, 0:m, :] = y[:, j * LANES:(j + 1) * LANES]

    @pl.when(rows > g // 2)
    def _():
        ffn(g)

    @pl.when((rows > 0) & (rows <= g // 2))
    def _():
        ffn(g // 2)
        y_ref[:, g // 2:, :] = jnp.zeros((Y_PLANES, g // 2, LANES), y_ref.dtype)

    @pl.when(rows == 0)
    def _():
        y_ref[...] = jnp.zeros_like(y_ref)


def _experts(blk_e, blk_rows, next_e, xs, w_gu, b_gu, w_d, b_d):
    p = xs.shape[1]
    g = EXPERT_ROWS
    grid_spec = pltpu.PrefetchScalarGridSpec(
        num_scalar_prefetch=3,
        grid=(p // g,),
        in_specs=[pl.BlockSpec((X_PLANES, g, LANES), lambda i, be, br, ne: (0, i, 0)),
                  pl.BlockSpec(memory_space=pl.ANY),
                  pl.BlockSpec((1, 1, 2 * D_EXPERT), lambda i, be, br, ne: (be[i], 0, 0)),
                  pl.BlockSpec(memory_space=pl.ANY),
                  pl.BlockSpec((1, 1, D_MODEL), lambda i, be, br, ne: (be[i], 0, 0))],
        out_specs=pl.BlockSpec((Y_PLANES, g, LANES), lambda i, be, br, ne: (0, i, 0)),
        scratch_shapes=[pltpu.VMEM((2, D_MODEL, 2 * D_EXPERT), F32),
                        pltpu.VMEM((2, D_EXPERT, D_MODEL), F32),
                        pltpu.VMEM((D_MODEL, 2 * D_EXPERT), BF16),
                        pltpu.VMEM((D_EXPERT, D_MODEL), BF16),
                        pltpu.SMEM((2,), jnp.int32),
                        pltpu.SemaphoreType.DMA((2, 2))],
    )
    return pl.pallas_call(
        _expert_kernel,
        grid_spec=grid_spec,
        out_shape=jax.ShapeDtypeStruct((Y_PLANES, p, LANES), jnp.uint32),
        compiler_params=_params(("arbitrary",), 56),
    )(blk_e, blk_rows, next_e, xs, w_gu, b_gu, w_d, b_d)


def _combine_kernel(y_ref, gate_ref, x_ref, g_ref, b_ref, o_ref):
    gate = gate_ref[...]
    ff = None
    for k in range(TOP_K):
        packed = jnp.concatenate([y_ref[k, j] for j in range(Y_PLANES)], axis=1)
        yk = _unpack_bf16_pairs(packed, F32) * gate[:, k:k + 1]
        ff = yk if ff is None else ff + yk
    o_ref[...] = _layer_norm(DEEPNORM_ALPHA * x_ref[...] + ff, g_ref[...], b_ref[...])


def _combine(yg, gate, x2, g, b):
    n = x2.shape[0]
    tm = SEQ_BLOCK
    row = lambda width: pl.BlockSpec((tm, width), lambda i: (i, 0))
    full = lambda a: pl.BlockSpec(a.shape, lambda i: (0, 0))
    return pl.pallas_call(
        _combine_kernel,
        grid=(n // tm,),
        in_specs=[pl.BlockSpec((TOP_K, Y_PLANES, tm, LANES), lambda i: (0, 0, i, 0)),
                  row(LANES), row(D_MODEL), full(g), full(b)],
        out_specs=row(D_MODEL),
        out_shape=jax.ShapeDtypeStruct((n, D_MODEL), F32),
        compiler_params=_params(("parallel",), 32),
    )(yg, gate, x2, g, b)


SC_WINDOW = 128


def _sc_mesh():
    return plsc.VectorSubcoreMesh(core_axis_name="core", subcore_axis_name="subcore")


def _sc_gather(table, idx):
    m = idx.shape[0]

    @pl.kernel(out_type=jax.ShapeDtypeStruct((m, LANES), table.dtype), mesh=_sc_mesh())
    def gather_kernel(table_hbm, idx_hbm, out_hbm):
        def body(idx_vmem, out_vmem):
            pltpu.sync_copy(table_hbm.at[idx_vmem.at[0]], out_vmem)

        pltpu.emit_pipeline(
            body,
            grid=(m // SC_WINDOW,),
            in_specs=[pl.BlockSpec((1, SC_WINDOW), lambda i: (0, i))],
            out_specs=[pl.BlockSpec((SC_WINDOW, LANES), lambda i: (i, 0))],
            core_axis_name=("core", "subcore"),
            dimension_semantics=(pltpu.PARALLEL,),
        )(idx_hbm, out_hbm)

    return gather_kernel(table, idx.reshape(1, m))


def _sc_scatter(src, idx, out_rows):
    m = idx.shape[0]
    src_blocks = src.shape[0] // SC_WINDOW

    @pl.kernel(out_type=jax.ShapeDtypeStruct((out_rows, LANES), src.dtype), mesh=_sc_mesh())
    def scatter_kernel(src_hbm, idx_hbm, out_hbm):
        def body(src_vmem, idx_vmem):
            pltpu.sync_copy(src_vmem, out_hbm.at[idx_vmem.at[0]])

        pltpu.emit_pipeline(
            body,
            grid=(m // SC_WINDOW,),
            in_specs=[pl.BlockSpec((SC_WINDOW, LANES), lambda i: (i % src_blocks, 0)),
                      pl.BlockSpec((1, SC_WINDOW), lambda i: (0, i))],
            out_specs=[],
            core_axis_name=("core", "subcore"),
            dimension_semantics=(pltpu.PARALLEL,),
        )(src_hbm, idx_hbm)

    return scatter_kernel(src, idx.reshape(1, m))


def _layer(x, mem, w_in, fox_f_bias, conv_w, i_bias, f_bias, fox_g, mlstm_g, w_mix_out,
           ln1_g, ln1_b, w_xq, w_xk, w_xv, w_xo, ln2_g, ln2_b, w_router, b_router,
           w_gate_up, b_gate_up, w_down, b_down, ln3_g, ln3_b):
    batch, seq, d = x.shape
    n_mem = mem.shape[1]
    n = batch * seq
    x2d = x.reshape(n, d)

    o_ff = 3 * FOX_WIDTH
    o_mqk = o_ff + FOX_HEADS
    o_mv = o_mqk + 2 * MLSTM_QK_WIDTH
    o_mi = o_mv + MLSTM_V_WIDTH
    o_mf = o_mi + MLSTM_HEADS
    o_mo = o_mf + MLSTM_HEADS
    n_gate = FOX_HEADS + 2 * MLSTM_HEADS
    w_r = jnp.concatenate(
        [w_in[:, FOX_WIDTH:2 * FOX_WIDTH], w_in[:, o_mqk:o_mv],
         w_in[:, o_ff:o_mqk], w_in[:, o_mi:o_mo],
         jnp.zeros((d, LANES - n_gate), w_in.dtype)], axis=1).astype(BF16)
    w_t = jnp.concatenate([w_in[:, :FOX_WIDTH], w_in[:, 2 * FOX_WIDTH:o_ff],
                           w_in[:, o_mv:o_mi], w_in[:, o_mo:]], axis=1).T.astype(BF16)
    gate_bias = jnp.concatenate(
        [fox_f_bias, i_bias, f_bias, jnp.zeros((LANES - n_gate,), F32)]).reshape(1, LANES)

    fqt, fk, fvt, mqk, mvt, mot, gates = _inproj(x2d, w_r, w_t, gate_bias)
    gcol, grow, cpieces = _gateprep(gates, batch, seq)
    fo = _fox(fqt, fk, cpieces, fvt, fox_g.reshape(1, FOX_WIDTH), batch, seq)
    gain_lanes = jnp.broadcast_to(mlstm_g[:, None], (MLSTM_V_WIDTH, LANES))
    mo_out = _mlstm(mqk, mvt, mot, gcol, grow, conv_w, gain_lanes, batch, seq)
    x1 = _outproj(fo, mo_out, w_mix_out.astype(BF16), x2d, ln1_g.reshape(1, d), ln1_b.reshape(1, d))

    kmem, vmem = _memkv(mem.reshape(batch * n_mem, d), w_xk.astype(BF16), w_xv.astype(BF16), n_mem)
    wrt = w_router.T
    wrt_hi = wrt.astype(BF16)
    wrt_lo = (wrt - wrt_hi.astype(F32)).astype(BF16)
    x2, x2p, logits_t = _xattn(x1, kmem, vmem, w_xq.astype(BF16), w_xo.astype(BF16),
                               ln2_g.reshape(1, d), ln2_b.reshape(1, d),
                               jnp.concatenate([wrt_hi, wrt_lo], axis=0),
                               b_router.reshape(N_EXPERTS, 1), batch, seq, n_mem)

    idx_t, rank_t, gate, cnt = _route(logits_t)
    counts = cnt[:, 0]
    g_rows = EXPERT_ROWS
    padded = ((counts + g_rows - 1) // g_rows) * g_rows
    pad_end = jnp.cumsum(padded)
    pad_start = pad_end - padded
    experts = jnp.arange(N_EXPERTS, dtype=jnp.int32)
    sel = idx_t[:TOP_K, :, None] == experts[None, None, :]
    pos_t = jnp.sum(jnp.where(sel, pad_start[None, None, :], 0), axis=-1) + rank_t[:TOP_K]
    p_rows = n * TOP_K + N_EXPERTS * g_rows
    nb = p_rows // g_rows
    blk_start = jnp.arange(nb, dtype=jnp.int32) * g_rows
    blk_e = jnp.minimum(jnp.sum((pad_end[None, :] <= blk_start[:, None]).astype(jnp.int32), axis=1),
                        N_EXPERTS - 1)
    own = blk_e[:, None] == experts[None, :]
    row_end = jnp.sum(jnp.where(own, (pad_start + counts)[None, :], 0), axis=1)
    blk_rows = jnp.clip(row_end - blk_start, 0, g_rows).astype(jnp.int32)
    later = (experts[None, :] > blk_e[:, None]) & (counts[None, :] > 0)
    next_e = jnp.min(jnp.where(later, experts[None, :], N_EXPERTS), axis=1)
    next_e = jnp.where(next_e == N_EXPERTS, -1, next_e).astype(jnp.int32)

    def piece_index(planes):
        off = jnp.arange(planes, dtype=jnp.int32) * p_rows
        return (pos_t[:, None, :] + off[None, :, None]).reshape(-1)

    xs = _sc_scatter(x2p.reshape(X_PLANES * n, LANES), piece_index(X_PLANES), X_PLANES * p_rows)
    y = _experts(blk_e, blk_rows, next_e, xs.reshape(X_PLANES, p_rows, LANES), w_gate_up,
                 b_gate_up.reshape(N_EXPERTS, 1, -1), w_down, b_down.reshape(N_EXPERTS, 1, -1))
    yg = _sc_gather(y.reshape(Y_PLANES * p_rows, LANES), piece_index(Y_PLANES))
    out = _combine(yg.reshape(TOP_K, Y_PLANES, n, LANES), gate, x2,
                   ln3_g.reshape(1, d), ln3_b.reshape(1, d))
    return out.reshape(batch, seq, d)


def kernel(x, mem, w_in, fox_f_bias, mlstm_conv_w, mlstm_i_bias, mlstm_f_bias, fox_norm_g, mlstm_norm_g, w_mix_out, ln1_g, ln1_b, w_xq, w_xk, w_xv, w_xo, ln2_g, ln2_b, w_router, b_router, w_gate_up, b_gate_up, w_down, b_down, ln3_g, ln3_b):
    for l in range(w_in.shape[0]):
        x = _layer(x, mem, w_in[l], fox_f_bias[l], mlstm_conv_w[l], mlstm_i_bias[l],
                   mlstm_f_bias[l], fox_norm_g[l], mlstm_norm_g[l], w_mix_out[l],
                   ln1_g[l], ln1_b[l], w_xq[l], w_xk[l], w_xv[l], w_xo[l], ln2_g[l], ln2_b[l],
                   w_router[l], b_router[l], w_gate_up[l], b_gate_up[l], w_down[l], b_down[l],
                   ln3_g[l], ln3_b[l])
    return x
```

```python
import jax
import jax.numpy as jnp
from jax import lax
from jax.experimental import pallas as pl
from jax.experimental.pallas import tpu as pltpu
from jax.experimental.pallas import tpu_sc as plsc

F32 = jnp.float32
BF16 = jnp.bfloat16

D_MODEL = 1024
FOX_HEADS = 8
FOX_HEAD_DIM = 64
FOX_WIDTH = FOX_HEADS * FOX_HEAD_DIM
MLSTM_HEADS = 4
MLSTM_QK_DIM = 64
MLSTM_V_DIM = 128
MLSTM_QK_WIDTH = MLSTM_HEADS * MLSTM_QK_DIM
MLSTM_V_WIDTH = MLSTM_HEADS * MLSTM_V_DIM
CONV_WIDTH = 4
XATTN_HEADS = 4
XATTN_HEAD_DIM = D_MODEL // XATTN_HEADS
N_EXPERTS = 32
TOP_K = 4
D_EXPERT = D_MODEL
SWIGLU_LIMIT = 7.0
SWIGLU_ALPHA = 1.702
DEEPNORM_ALPHA = 2.0 ** 0.25
LN_EPS = 1e-5
RMS_EPS = 1e-6

LANES = 128
SEQ_BLOCK = 256
ROW_TILE = 512
XATTN_TILE = 1024
EXPERT_ROWS = 512
EXPERT_CHUNK = 256
ROUTE_TILE = 512
X_PLANES = D_MODEL // 2 // LANES
Y_PLANES = X_PLANES
GATE_I0 = FOX_HEADS
GATE_F0 = FOX_HEADS + MLSTM_HEADS

MIB = 1024 * 1024


def _params(semantics, vmem_mib):
    return pltpu.CompilerParams(dimension_semantics=semantics,
                                vmem_limit_bytes=vmem_mib * MIB)


def _layer_norm(y, g, b):
    mu = jnp.mean(y, axis=-1, keepdims=True)
    yc = y - mu
    var = jnp.mean(yc * yc, axis=-1, keepdims=True)
    return yc * lax.rsqrt(var + LN_EPS) * g + b


def _dot(a, b):
    return jnp.dot(a, b, preferred_element_type=F32)


def _dot_nt(a, b):
    return lax.dot_general(a, b, (((1,), (1,)), ((), ())), preferred_element_type=F32)


def _dot_tn(a, b):
    return lax.dot_general(a, b, (((0,), (0,)), ((), ())), preferred_element_type=F32)


def _pack_bf16_pairs(x):
    w = x.shape[1] // 2
    lo = pltpu.bitcast(x[:, :w].astype(BF16).astype(F32), jnp.uint32)
    hi = pltpu.bitcast(x[:, w:].astype(BF16).astype(F32), jnp.uint32)
    return (lo >> 16) | hi


def _unpack_bf16_pairs(u, dtype=BF16):
    lo = pltpu.bitcast(u << 16, F32).astype(dtype)
    hi = pltpu.bitcast(u & jnp.uint32(0xFFFF0000), F32).astype(dtype)
    return jnp.concatenate([lo, hi], axis=1)


def _inproj_kernel(x_ref, w_ref, wt_ref, gb_ref, fqt_ref, fk_ref, fvt_ref, mqk_ref, mvt_ref,
                   mot_ref, g_ref):
    xb = x_ref[...].astype(BF16)

    def mm(c0, width):
        return _dot(xb, w_ref[:, c0:c0 + width])

    def mm_t(r0, height):
        return _dot_nt(wt_ref[r0:r0 + height, :], xb)

    r_fv = FOX_WIDTH
    r_mv = r_fv + FOX_WIDTH
    r_mo = r_mv + MLSTM_V_WIDTH
    fqt_ref[...] = (mm_t(0, FOX_WIDTH) * (FOX_HEAD_DIM ** -0.5)).astype(BF16)
    fvt_ref[...] = mm_t(r_fv, FOX_WIDTH).astype(BF16)
    mvt_ref[...] = mm_t(r_mv, MLSTM_V_WIDTH).astype(BF16)
    mot_ref[...] = mm_t(r_mo, MLSTM_V_WIDTH)
    c_qk = FOX_WIDTH
    c_g = c_qk + 2 * MLSTM_QK_WIDTH
    fk_ref[...] = mm(0, FOX_WIDTH).astype(BF16)
    mqk_ref[...] = mm(c_qk, 2 * MLSTM_QK_WIDTH)
    g_ref[...] = mm(c_g, LANES) + gb_ref[...]


def _inproj(x2d, w_r, w_t, gate_bias):
    n = x2d.shape[0]
    tm = ROW_TILE
    row = lambda width: pl.BlockSpec((tm, width), lambda i: (i, 0))
    col = lambda height: pl.BlockSpec((height, tm), lambda i: (0, i))
    full = lambda a: pl.BlockSpec(a.shape, lambda i: (0, 0))
    out_shapes = (
        jax.ShapeDtypeStruct((FOX_WIDTH, n), BF16),
        jax.ShapeDtypeStruct((n, FOX_WIDTH), BF16),
        jax.ShapeDtypeStruct((FOX_WIDTH, n), BF16),
        jax.ShapeDtypeStruct((n, 2 * MLSTM_QK_WIDTH), F32),
        jax.ShapeDtypeStruct((MLSTM_V_WIDTH, n), BF16),
        jax.ShapeDtypeStruct((MLSTM_V_WIDTH, n), F32),
        jax.ShapeDtypeStruct((n, LANES), F32),
    )
    return pl.pallas_call(
        _inproj_kernel,
        grid=(n // tm,),
        in_specs=[row(D_MODEL), full(w_r), full(w_t), full(gate_bias)],
        out_specs=(col(FOX_WIDTH), row(FOX_WIDTH), col(FOX_WIDTH), row(2 * MLSTM_QK_WIDTH),
                   col(MLSTM_V_WIDTH), col(MLSTM_V_WIDTH), row(LANES)),
        out_shape=out_shapes,
        compiler_params=_params(("parallel",), 48),
    )(x2d, w_r, w_t, gate_bias)


def _split3(x):
    hi = x.astype(BF16)
    r1 = x - hi.astype(F32)
    mid = r1.astype(BF16)
    lo = (r1 - mid.astype(F32)).astype(BF16)
    return hi, mid, lo


def _gateprep_kernel(g_ref, sel_ref, col_ref, row_ref, cp_ref, carry_ref):
    c = pl.program_id(1)

    @pl.when(c == 0)
    def _():
        carry_ref[...] = jnp.zeros_like(carry_ref)

    g = g_ref[...]
    lane = lax.broadcasted_iota(jnp.int32, g.shape, 1)
    is_i = (lane >= GATE_I0) & (lane < GATE_F0)
    logsig = jnp.minimum(g, 0.0) - jnp.log1p(jnp.exp(-jnp.abs(g)))
    blk = g.shape[0]
    r = lax.broadcasted_iota(jnp.int32, (blk, blk), 0)
    s = lax.broadcasted_iota(jnp.int32, (blk, blk), 1)
    tri = (s <= r).astype(BF16)
    cs = None
    for piece in _split3(logsig):
        term = _dot(tri, piece)
        cs = term if cs is None else cs + term
    carry = carry_ref[0:1, :]
    glob = cs + carry
    carry_ref[...] = jnp.broadcast_to(glob[blk - 1:blk, :], carry_ref.shape)
    out = jnp.where(lane < GATE_I0, glob, jnp.where(is_i, g, cs))
    col_ref[...] = out
    row_ref[0] = out.T[0:16, :]

    pieces = jnp.concatenate(_split3(-glob), axis=1)
    moved = _dot(pieces, sel_ref[...])
    for p in range(FOX_HEADS // 2):
        cp_ref[0, p] = moved[:, p * LANES:(p + 1) * LANES].astype(BF16)


def _piece_selector():
    src = jnp.arange(3 * LANES, dtype=jnp.int32)[:, None]
    dst = jnp.arange(4 * LANES, dtype=jnp.int32)[None, :]
    piece, head = src // LANES, src % LANES
    pair, lane = dst // LANES, dst % LANES
    hit = (lane < 6) & (lane % 3 == piece) & (head == 2 * pair + lane // 3)
    return hit.astype(BF16)


def _gateprep(gates, batch, seq):
    n = gates.shape[0]
    nc = seq // SEQ_BLOCK
    sel = _piece_selector()
    return pl.pallas_call(
        _gateprep_kernel,
        grid=(batch, nc),
        in_specs=[pl.BlockSpec((SEQ_BLOCK, LANES), lambda b, c: (b * nc + c, 0)),
                  pl.BlockSpec(sel.shape, lambda b, c: (0, 0))],
        out_specs=(pl.BlockSpec((SEQ_BLOCK, LANES), lambda b, c: (b * nc + c, 0)),
                   pl.BlockSpec((1, 16, SEQ_BLOCK), lambda b, c: (b, 0, c)),
                   pl.BlockSpec((1, FOX_HEADS // 2, SEQ_BLOCK, LANES), lambda b, c: (b, 0, c, 0))),
        out_shape=(jax.ShapeDtypeStruct((n, LANES), F32),
                   jax.ShapeDtypeStruct((batch, 16, seq), F32),
                   jax.ShapeDtypeStruct((batch, FOX_HEADS // 2, seq, LANES), BF16)),
        scratch_shapes=[pltpu.VMEM((8, LANES), F32)],
        compiler_params=_params(("parallel", "arbitrary"), 32),
    )(gates, sel)


FOX_ONES_ROWS = 16
FOX_VAUG_ROWS = FOX_HEAD_DIM + FOX_ONES_ROWS
FOX_QUERY_TILE = 2 * SEQ_BLOCK


def _fox_kernel(qt_ref, k_ref, cp_ref, vt_ref, gain_ref, o_ref,
                kaug_sc, vaug_sc, qaug_sc, sta_sc, stb_sc, m_sc, acc0_sc, acc1_sc):
    qi = pl.program_id(2)
    tq = qt_ref.shape[1]
    tg = tq
    tk = SEQ_BLOCK
    hd = FOX_HEAD_DIM
    seq = k_ref.shape[0]

    @pl.when(qi == 0)
    def _():
        lane = lax.broadcasted_iota(jnp.int32, (tk, LANES), 1)

        def build(blk, carry):
            r0 = pl.multiple_of(blk * tk, tk)
            kp = k_ref[pl.ds(r0, tk), :].astype(F32)
            cp = cp_ref[0, 0, pl.ds(r0, tk), :].astype(F32)
            for h in range(2):
                kh = kp if h == 0 else pltpu.roll(kp, hd, axis=1)
                ch = pltpu.roll(cp, hd - 3 * h, axis=1)
                kaug = jnp.where(lane < hd, kh, jnp.where(lane < hd + 3, ch, 0.0))
                kaug_sc[h, pl.ds(r0, tk), :] = kaug.astype(BF16)
            return carry

        lax.fori_loop(0, seq // tk, build, 0)
        for h in range(2):
            vaug_sc[h, 0:hd, :] = vt_ref[h * hd:(h + 1) * hd, :]
            vaug_sc[h, hd:, :] = jnp.ones((FOX_ONES_ROWS, seq), BF16)

    ones3 = (lax.broadcasted_iota(jnp.int32, (FOX_ONES_ROWS, tq), 0) < 3).astype(BF16)
    for h in range(2):
        qaug_sc[h, 0:hd, :] = qt_ref[h * hd:(h + 1) * hd, :]
        qaug_sc[h, hd:hd + FOX_ONES_ROWS, :] = ones3
        qaug_sc[h, hd + FOX_ONES_ROWS:, :] = jnp.zeros((LANES - hd - FOX_ONES_ROWS, tq), BF16)

    def put_scores(kg, slot, masked=False):
        k0 = pl.multiple_of(kg * tg, tg)
        for h in range(2):
            st = _dot(kaug_sc[h, pl.ds(k0, tg), :], qaug_sc[h])
            if masked:
                kk = lax.broadcasted_iota(jnp.int32, (tg, tq), 0)
                tt = lax.broadcasted_iota(jnp.int32, (tg, tq), 1)
                st = jnp.where(kk <= tt, st, -jnp.inf)
            slot[h] = st

    m_sc[...] = jnp.full(m_sc.shape, -jnp.inf, F32)
    acc0_sc[...] = jnp.zeros(acc0_sc.shape, F32)
    acc1_sc[...] = jnp.zeros(acc1_sc.shape, F32)
    acc = (acc0_sc, acc1_sc)

    def absorb(kg, slot):
        k0 = pl.multiple_of(kg * tg, tg)
        half = tg // 2
        for h in range(2):
            st = slot[h]
            m_prev = m_sc[h]
            m_new = jnp.maximum(m_prev, jnp.max(st, axis=0, keepdims=True))
            pv = None
            for u in range(2):
                p = jnp.exp(st[u * half:(u + 1) * half] - m_new)
                part = _dot(vaug_sc[h, :, pl.ds(k0 + u * half, half)], p.astype(BF16))
                pv = part if pv is None else pv + part
            acc[h][...] = jnp.exp(m_prev - m_new) * acc[h][...] + pv
            m_sc[h] = m_new

    sa, sb = sta_sc, stb_sc
    n_loop = jnp.maximum(qi - 1, 0) // 2

    @pl.when(qi > 0)
    def _():
        put_scores(0, sa)

    def body(j, carry):
        put_scores(2 * j + 1, sb)
        absorb(2 * j, sa)
        put_scores(2 * j + 2, sa)
        absorb(2 * j + 1, sb)
        return carry

    lax.fori_loop(0, n_loop, body, 0)
    done = 2 * n_loop

    @pl.when(qi == 0)
    def _():
        put_scores(qi, sa, masked=True)
        absorb(qi, sa)

    @pl.when((qi > 0) & (qi - done == 1))
    def _():
        put_scores(qi, sb, masked=True)
        absorb(done, sa)
        absorb(qi, sb)

    @pl.when((qi > 0) & (qi - done == 2))
    def _():
        put_scores(done + 1, sb)
        absorb(done, sa)
        put_scores(qi, sa, masked=True)
        absorb(done + 1, sb)
        absorb(qi, sa)

    a0 = acc0_sc[...]
    a1 = acc1_sc[...]
    ot = jnp.concatenate([a0[0:hd] / a0[hd:hd + 1], a1[0:hd] / a1[hd:hd + 1]], axis=0)
    o = ot.T
    lane = lax.broadcasted_iota(jnp.int32, (tq, LANES), 1)
    lo = lane < FOX_HEAD_DIM
    sq = o * o
    ss0 = jnp.sum(jnp.where(lo, sq, 0.0), axis=-1, keepdims=True)
    ss1 = jnp.sum(jnp.where(lo, 0.0, sq), axis=-1, keepdims=True)
    inv = jnp.where(lo, lax.rsqrt(ss0 / FOX_HEAD_DIM + RMS_EPS),
                    lax.rsqrt(ss1 / FOX_HEAD_DIM + RMS_EPS))
    o_ref[...] = (o * inv * gain_ref[...]).astype(o_ref.dtype)


def _fox(fqt, fk, cpieces, fvt, fox_gain, batch, seq):
    n = fk.shape[0]
    tq = FOX_QUERY_TILE
    nq = seq // tq
    npair = FOX_HEADS // 2
    return pl.pallas_call(
        _fox_kernel,
        grid=(batch, npair, nq),
        in_specs=[pl.BlockSpec((LANES, tq), lambda b, hp, qi: (hp, b * nq + qi)),
                  pl.BlockSpec((seq, LANES), lambda b, hp, qi: (b, hp)),
                  pl.BlockSpec((1, 1, seq, LANES), lambda b, hp, qi: (b, hp, 0, 0)),
                  pl.BlockSpec((LANES, seq), lambda b, hp, qi: (hp, b)),
                  pl.BlockSpec((1, LANES), lambda b, hp, qi: (0, hp))],
        out_specs=pl.BlockSpec((tq, LANES), lambda b, hp, qi: (b * nq + qi, hp)),
        out_shape=jax.ShapeDtypeStruct((n, FOX_WIDTH), BF16),
        scratch_shapes=[pltpu.VMEM((2, seq, LANES), BF16),
                        pltpu.VMEM((2, FOX_VAUG_ROWS, seq), BF16),
                        pltpu.VMEM((2, LANES, tq), BF16),
                        pltpu.VMEM((2, tq, tq), F32),
                        pltpu.VMEM((2, tq, tq), F32),
                        pltpu.VMEM((2, 1, tq), F32),
                        pltpu.VMEM((FOX_VAUG_ROWS, tq), F32),
                        pltpu.VMEM((FOX_VAUG_ROWS, tq), F32)],
        compiler_params=_params(("parallel", "parallel", "arbitrary"), 48),
    )(fqt, fk, cpieces, fvt, fox_gain)


MLSTM_ONES_ROWS = 16


def _mlstm_kernel(qk_ref, vt_ref, ogt_ref, col_ref, row_ref, cw_ref, gain_ref, o_ref,
                  tail_ref, buf_ref, c_sc, m_sc):
    c = pl.program_id(1)
    L = SEQ_BLOCK

    @pl.when(c == 0)
    def _():
        tail_ref[...] = jnp.zeros_like(tail_ref)
        c_sc[...] = jnp.zeros_like(c_sc)
        m_sc[...] = jnp.zeros_like(m_sc)

    x = qk_ref[...]
    buf_ref[0:8, :] = tail_ref[...]
    buf_ref[8:8 + L, :] = x
    tail_ref[...] = x[L - 8:L, :]
    y = x * cw_ref[CONV_WIDTH - 1:CONV_WIDTH, :]
    for j in range(CONV_WIDTH - 1):
        shift = CONV_WIDTH - 1 - j
        y = y + buf_ref[8 - shift:8 - shift + L, :] * cw_ref[j:j + 1, :]
    y = y * jax.nn.sigmoid(y)
    kf = y[:, MLSTM_QK_WIDTH:] * (MLSTM_QK_DIM ** -0.5)
    n_pair = MLSTM_HEADS // 2
    qt = [y[:, p * LANES:(p + 1) * LANES].T for p in range(n_pair)]
    kb = [kf[:, p * LANES:(p + 1) * LANES].astype(BF16) for p in range(n_pair)]

    col = col_ref[...]
    row = row_ref[0]
    lane = lax.broadcasted_iota(jnp.int32, (L, LANES), 1)
    sub = lax.broadcasted_iota(jnp.int32, (LANES, L), 0)
    ss = lax.broadcasted_iota(jnp.int32, (L, L), 0)
    ll = lax.broadcasted_iota(jnp.int32, (L, L), 1)
    causal = ss <= ll
    ones_rows = jnp.ones((MLSTM_ONES_ROWS, L), BF16)

    for h in range(MLSTM_HEADS):
        pair, half = divmod(h, 2)
        head_sub = (sub < MLSTM_QK_DIM) if half == 0 else (sub >= MLSTM_QK_DIM)
        head_lane = (lane < MLSTM_QK_DIM) if half == 0 else (lane >= MLSTM_QK_DIM)
        qth = jnp.where(head_sub, qt[pair], 0.0).astype(BF16)
        rcol = col[:, GATE_I0 + h:GATE_I0 + h + 1] - col[:, GATE_F0 + h:GATE_F0 + h + 1]
        brow = row[GATE_F0 + h:GATE_F0 + h + 1, :]
        lirow = row[GATE_I0 + h:GATE_I0 + h + 1, :]
        g = brow[:, L - 1:L]
        m_prev = m_sc[h][0:1, 0:1]

        dt = jnp.where(causal, rcol + brow, -jnp.inf)
        inter_log = brow + m_prev
        m_t = jnp.maximum(inter_log, jnp.max(dt, axis=0, keepdims=True))
        w_inter = jnp.exp(inter_log - m_t)
        pt = jnp.exp(dt - m_t) * _dot(kb[pair], qth)
        vaug = jnp.concatenate([vt_ref[h * LANES:(h + 1) * LANES, :], ones_rows], axis=0)
        cstate = c_sc[h]
        tot = w_inter * _dot(cstate.astype(BF16), qth) + _dot(vaug, pt.astype(BF16))
        den = tot[MLSTM_V_DIM:MLSTM_V_DIM + 1]
        hout = tot[0:MLSTM_V_DIM] * (1.0 / jnp.maximum(jnp.abs(den), jnp.exp(-m_t)))

        a = g + (lirow - brow)
        m_loc = jnp.max(a, axis=1, keepdims=True)
        vw = (vaug.astype(F32) * jnp.exp(a - m_loc)).astype(BF16)
        kmask = jnp.where(head_lane, kb[pair], jnp.zeros_like(kb[pair]))
        kv = _dot(vw, kmask)
        m_new = jnp.maximum(g + m_prev, m_loc)
        c_sc[h] = jnp.exp(g + m_prev - m_new) * cstate + jnp.exp(m_loc - m_new) * kv
        m_sc[h] = jnp.broadcast_to(m_new, m_sc.shape[1:])

        ms = jnp.mean(hout * hout, axis=0, keepdims=True)
        gain = gain_ref[h * LANES:(h + 1) * LANES, :]
        hn = hout * lax.rsqrt(ms + RMS_EPS) * jnp.concatenate([gain] * (L // LANES), axis=1)
        gate = jax.nn.sigmoid(ogt_ref[h * LANES:(h + 1) * LANES, :])
        o_ref[:, h * LANES:(h + 1) * LANES] = (hn * gate).T.astype(o_ref.dtype)


def _mlstm(mqk, mvt, mot, gcol, grow, conv_w, gain_lanes, batch, seq):
    n = mqk.shape[0]
    L = SEQ_BLOCK
    nc = seq // L
    qk_width = 2 * MLSTM_QK_WIDTH
    row = lambda width: pl.BlockSpec((L, width), lambda b, c: (b * nc + c, 0))
    col = pl.BlockSpec((MLSTM_V_WIDTH, L), lambda b, c: (0, b * nc + c))
    full = lambda a: pl.BlockSpec(a.shape, lambda b, c: (0, 0))
    return pl.pallas_call(
        _mlstm_kernel,
        grid=(batch, nc),
        in_specs=[row(qk_width), col, col, row(LANES),
                  pl.BlockSpec((1, 16, L), lambda b, c: (b, 0, c)),
                  full(conv_w), full(gain_lanes)],
        out_specs=row(MLSTM_V_WIDTH),
        out_shape=jax.ShapeDtypeStruct((n, MLSTM_V_WIDTH), BF16),
        scratch_shapes=[pltpu.VMEM((8, qk_width), F32), pltpu.VMEM((8 + L, qk_width), F32),
                        pltpu.VMEM((MLSTM_HEADS, MLSTM_V_DIM + MLSTM_ONES_ROWS, LANES), F32),
                        pltpu.VMEM((MLSTM_HEADS, 8, LANES), F32)],
        compiler_params=_params(("parallel", "arbitrary"), 48),
    )(mqk, mvt, mot, gcol, grow, conv_w, gain_lanes)


def _outproj_kernel(fo_ref, mo_ref, w_ref, x_ref, g_ref, b_ref, o_ref):
    mix = _dot(fo_ref[...], w_ref[0:FOX_WIDTH, :]) + _dot(mo_ref[...], w_ref[FOX_WIDTH:, :])
    o_ref[...] = _layer_norm(DEEPNORM_ALPHA * x_ref[...] + mix, g_ref[...], b_ref[...])


def _outproj(fo, mo, w_out, x2d, g, b):
    n = x2d.shape[0]
    tm = ROW_TILE
    row = lambda width: pl.BlockSpec((tm, width), lambda i: (i, 0))
    full = lambda a: pl.BlockSpec(a.shape, lambda i: (0, 0))
    return pl.pallas_call(
        _outproj_kernel,
        grid=(n // tm,),
        in_specs=[row(512), row(512), full(w_out), row(D_MODEL), full(g), full(b)],
        out_specs=row(D_MODEL),
        out_shape=jax.ShapeDtypeStruct((n, D_MODEL), F32),
        compiler_params=_params(("parallel",), 48),
    )(fo, mo, w_out, x2d, g, b)


def _memkv_kernel(mem_ref, wk_ref, wv_ref, k_ref, v_ref):
    mb = mem_ref[...].astype(BF16)
    k_ref[...] = (_dot(mb, wk_ref[...]) * (XATTN_HEAD_DIM ** -0.5)).astype(BF16)
    v_ref[...] = _dot(mb, wv_ref[...]).astype(BF16)


def _memkv(mem2d, wk, wv, n_mem):
    n = mem2d.shape[0]
    row = pl.BlockSpec((n_mem, D_MODEL), lambda i: (i, 0))
    full = lambda a: pl.BlockSpec(a.shape, lambda i: (0, 0))
    return pl.pallas_call(
        _memkv_kernel,
        grid=(n // n_mem,),
        in_specs=[row, full(wk), full(wv)],
        out_specs=(row, row),
        out_shape=(jax.ShapeDtypeStruct((n, D_MODEL), BF16),) * 2,
        compiler_params=_params(("parallel",), 32),
    )(mem2d, wk, wv)


def _xattn_kernel(x_ref, k_ref, v_ref, wq_ref, wo_ref, g_ref, b_ref, wr_ref, br_ref,
                  o_ref, ob_ref, lg_ref):
    half = x_ref.shape[0] // 2
    for r in range(2):
        rows = slice(r * half, (r + 1) * half)
        x = x_ref[rows, :]
        q = _dot(x.astype(BF16), wq_ref[...]).astype(BF16)
        outs = []
        for h in range(XATTN_HEADS):
            sl = slice(h * XATTN_HEAD_DIM, (h + 1) * XATTN_HEAD_DIM)
            s = _dot_nt(q[:, sl], k_ref[:, sl])
            p = jnp.exp(s - jnp.max(s, axis=-1, keepdims=True))
            l = jnp.sum(p, axis=-1, keepdims=True)
            outs.append((_dot(p.astype(BF16), v_ref[:, sl]) / l).astype(BF16))
        o = jnp.concatenate(outs, axis=1)
        xa = _dot(o, wo_ref[...])
        x2 = _layer_norm(DEEPNORM_ALPHA * x + xa, g_ref[...], b_ref[...])
        o_ref[rows, :] = x2
        packed = _pack_bf16_pairs(x2)
        for j in range(ob_ref.shape[0]):
            ob_ref[j, rows, :] = packed[:, j * LANES:(j + 1) * LANES]
        x2h = x2.astype(BF16)
        x2l = (x2 - x2h.astype(F32)).astype(BF16)
        a = _dot_nt(wr_ref[...], x2h)
        b = _dot_nt(wr_ref[0:N_EXPERTS, :], x2l)
        lg_ref[:, rows] = a[0:N_EXPERTS] + a[N_EXPERTS:] + b + br_ref[...]


def _xattn(x1, kmem, vmem, wq, wo, g, b, wr, br, batch, seq, n_mem):
    n = x1.shape[0]
    tm = XATTN_TILE
    nt = seq // tm
    row = lambda width: pl.BlockSpec((tm, width), lambda bi, i: (bi * nt + i, 0))
    full = lambda a: pl.BlockSpec(a.shape, lambda bi, i: (0, 0))
    kv = pl.BlockSpec((n_mem, D_MODEL), lambda bi, i: (bi, 0))
    return pl.pallas_call(
        _xattn_kernel,
        grid=(batch, nt),
        in_specs=[row(D_MODEL), kv, kv, full(wq), full(wo), full(g), full(b), full(wr), full(br)],
        out_specs=(row(D_MODEL),
                   pl.BlockSpec((X_PLANES, tm, LANES), lambda bi, i: (0, bi * nt + i, 0)),
                   pl.BlockSpec((N_EXPERTS, tm), lambda bi, i: (0, bi * nt + i))),
        out_shape=(jax.ShapeDtypeStruct((n, D_MODEL), F32),
                   jax.ShapeDtypeStruct((X_PLANES, n, LANES), jnp.uint32),
                   jax.ShapeDtypeStruct((N_EXPERTS, n), F32)),
        compiler_params=_params(("parallel", "parallel"), 60),
    )(x1, kmem, vmem, wq, wo, g, b, wr, br)


def _route_kernel(lg_ref, idx_ref, rank_ref, gate_ref, cnt_ref, carry_ref):
    i = pl.program_id(0)

    @pl.when(i == 0)
    def _():
        carry_ref[...] = jnp.zeros_like(carry_ref)

    lg = lg_ref[...]
    t = lg.shape[1]
    e_idx = lax.broadcasted_iota(jnp.int32, lg.shape, 0).astype(F32)
    sels, vals, idxs = [], [], []
    for _ in range(TOP_K):
        mx = jnp.max(lg, axis=0, keepdims=True)
        first = jnp.min(jnp.where(lg == mx, e_idx, float(N_EXPERTS)), axis=0, keepdims=True)
        sel = e_idx == first
        sels.append(sel)
        vals.append(mx)
        idxs.append(first)
        lg = jnp.where(sel, -jnp.inf, lg)
    exps = [jnp.exp(v - vals[0]) for v in vals]
    tot = exps[0] + exps[1] + exps[2] + exps[3]

    selmat = (sels[0] | sels[1] | sels[2] | sels[3])
    r = lax.broadcasted_iota(jnp.int32, (t, t), 0)
    s = lax.broadcasted_iota(jnp.int32, (t, t), 1)
    earlier = (r < s).astype(BF16)
    carry = carry_ref[:, 0:1]
    rankmat = _dot(selmat.astype(BF16), earlier) + carry
    new_carry = carry + jnp.sum(selmat.astype(F32), axis=1, keepdims=True)
    carry_ref[...] = jnp.broadcast_to(new_carry, carry_ref.shape)
    cnt_ref[...] = jnp.broadcast_to(new_carry, cnt_ref.shape).astype(jnp.int32)

    row8 = lax.broadcasted_iota(jnp.int32, (8, t), 0)
    row128 = lax.broadcasted_iota(jnp.int32, (LANES, t), 0)
    idx_out = jnp.zeros((8, t), F32)
    rank_out = jnp.zeros((8, t), F32)
    gate_out = jnp.zeros((LANES, t), F32)
    for k in range(TOP_K):
        rk = jnp.sum(jnp.where(sels[k], rankmat, 0.0), axis=0, keepdims=True)
        idx_out = jnp.where(row8 == k, idxs[k], idx_out)
        rank_out = jnp.where(row8 == k, rk, rank_out)
        gate_out = jnp.where(row128 == k, exps[k] / tot, gate_out)
    idx_ref[...] = idx_out.astype(jnp.int32)
    rank_ref[...] = rank_out.astype(jnp.int32)
    gate_ref[...] = gate_out.T


def _route(logits_t):
    n = logits_t.shape[1]
    t = ROUTE_TILE
    col = lambda rows: pl.BlockSpec((rows, t), lambda i: (0, i))
    return pl.pallas_call(
        _route_kernel,
        grid=(n // t,),
        in_specs=[col(N_EXPERTS)],
        out_specs=(col(8), col(8), pl.BlockSpec((t, LANES), lambda i: (i, 0)),
                   pl.BlockSpec((N_EXPERTS, LANES), lambda i: (0, 0))),
        out_shape=(jax.ShapeDtypeStruct((8, n), jnp.int32),
                   jax.ShapeDtypeStruct((8, n), jnp.int32),
                   jax.ShapeDtypeStruct((n, LANES), F32),
                   jax.ShapeDtypeStruct((N_EXPERTS, LANES), jnp.int32)),
        scratch_shapes=[pltpu.VMEM((N_EXPERTS, LANES), F32)],
        compiler_params=_params(("arbitrary",), 32),
    )(logits_t)


def _expert_kernel(blk_e_ref, blk_rows_ref, next_e_ref, x_ref, wgu_hbm, bgu_ref, wd_hbm, bd_ref,
                   y_ref, wgu_f32, wd_f32, wgu_sc, wd_sc, state_ref, sem):
    i = pl.program_id(0)
    e = blk_e_ref[i]
    rows = blk_rows_ref[i]
    g = y_ref.shape[1]

    def weight_copies(expert, slot):
        return (pltpu.make_async_copy(wgu_hbm.at[expert], wgu_f32.at[slot], sem.at[0, slot]),
                pltpu.make_async_copy(wd_hbm.at[expert], wd_f32.at[slot], sem.at[1, slot]))

    @pl.when(i == 0)
    def _():
        state_ref[0] = -1
        state_ref[1] = 0

    @pl.when((rows > 0) & (e != state_ref[0]))
    def _():
        slot = state_ref[1]

        @pl.when(i == 0)
        def _():
            for cp in weight_copies(e, slot):
                cp.start()

        for cp in weight_copies(e, slot):
            cp.wait()
        wgu_sc[...] = wgu_f32[slot].astype(BF16)
        wd_sc[...] = wd_f32[slot].astype(BF16)
        nxt = next_e_ref[i]

        @pl.when(nxt >= 0)
        def _():
            for cp in weight_copies(nxt, 1 - slot):
                cp.start()

        state_ref[0] = e
        state_ref[1] = 1 - slot

    def ffn(m):
        packed = jnp.concatenate([x_ref[j, 0:m, :] for j in range(X_PLANES)], axis=1)
        xb = _unpack_bf16_pairs(packed)
        hids = []
        for c in range(D_EXPERT // EXPERT_CHUNK):
            g0 = c * EXPERT_CHUNK
            l0 = D_EXPERT + g0
            gate = _dot(xb, wgu_sc[:, g0:g0 + EXPERT_CHUNK]) + bgu_ref[0, :, g0:g0 + EXPERT_CHUNK]
            lin = _dot(xb, wgu_sc[:, l0:l0 + EXPERT_CHUNK]) + bgu_ref[0, :, l0:l0 + EXPERT_CHUNK]
            gate = jnp.minimum(gate, SWIGLU_LIMIT)
            lin = jnp.clip(lin, -SWIGLU_LIMIT, SWIGLU_LIMIT)
            hids.append((gate * jax.nn.sigmoid(SWIGLU_ALPHA * gate) * (lin + 1.0)).astype(BF16))
        y = _pack_bf16_pairs(_dot(jnp.concatenate(hids, axis=1), wd_sc[...]) + bd_ref[0])
        for j in range(Y_PLANES):
            y_ref[j, 0:m, :] = y[:, j * LANES:(j + 1) * LANES]

    @pl.when(rows > g // 2)
    def _():
        ffn(g)

    @pl.when((rows > 0) & (rows <= g // 2))
    def _():
        ffn(g // 2)
        y_ref[:, g // 2:, :] = jnp.zeros((Y_PLANES, g // 2, LANES), y_ref.dtype)

    @pl.when(rows == 0)
    def _():
        y_ref[...] = jnp.zeros_like(y_ref)


def _experts(blk_e, blk_rows, next_e, xs, w_gu, b_gu, w_d, b_d):
    p = xs.shape[1]
    g = EXPERT_ROWS
    grid_spec = pltpu.PrefetchScalarGridSpec(
        num_scalar_prefetch=3,
        grid=(p // g,),
        in_specs=[pl.BlockSpec((X_PLANES, g, LANES), lambda i, be, br, ne: (0, i, 0)),
                  pl.BlockSpec(memory_space=pl.ANY),
                  pl.BlockSpec((1, 1, 2 * D_EXPERT), lambda i, be, br, ne: (be[i], 0, 0)),
                  pl.BlockSpec(memory_space=pl.ANY),
                  pl.BlockSpec((1, 1, D_MODEL), lambda i, be, br, ne: (be[i], 0, 0))],
        out_specs=pl.BlockSpec((Y_PLANES, g, LANES), lambda i, be, br, ne: (0, i, 0)),
        scratch_shapes=[pltpu.VMEM((2, D_MODEL, 2 * D_EXPERT), F32),
                        pltpu.VMEM((2, D_EXPERT, D_MODEL), F32),
                        pltpu.VMEM((D_MODEL, 2 * D_EXPERT), BF16),
                        pltpu.VMEM((D_EXPERT, D_MODEL), BF16),
                        pltpu.SMEM((2,), jnp.int32),
                        pltpu.SemaphoreType.DMA((2, 2))],
    )
    return pl.pallas_call(
        _expert_kernel,
        grid_spec=grid_spec,
        out_shape=jax.ShapeDtypeStruct((Y_PLANES, p, LANES), jnp.uint32),
        compiler_params=_params(("arbitrary",), 56),
    )(blk_e, blk_rows, next_e, xs, w_gu, b_gu, w_d, b_d)


def _combine_kernel(y_ref, gate_ref, x_ref, g_ref, b_ref, o_ref):
    gate = gate_ref[...]
    ff = None
    for k in range(TOP_K):
        packed = jnp.concatenate([y_ref[k, j] for j in range(Y_PLANES)], axis=1)
        yk = _unpack_bf16_pairs(packed, F32) * gate[:, k:k + 1]
        ff = yk if ff is None else ff + yk
    o_ref[...] = _layer_norm(DEEPNORM_ALPHA * x_ref[...] + ff, g_ref[...], b_ref[...])


def _combine(yg, gate, x2, g, b):
    n = x2.shape[0]
    tm = SEQ_BLOCK
    row = lambda width: pl.BlockSpec((tm, width), lambda i: (i, 0))
    full = lambda a: pl.BlockSpec(a.shape, lambda i: (0, 0))
    return pl.pallas_call(
        _combine_kernel,
        grid=(n // tm,),
        in_specs=[pl.BlockSpec((TOP_K, Y_PLANES, tm, LANES), lambda i: (0, 0, i, 0)),
                  row(LANES), row(D_MODEL), full(g), full(b)],
        out_specs=row(D_MODEL),
        out_shape=jax.ShapeDtypeStruct((n, D_MODEL), F32),
        compiler_params=_params(("parallel",), 32),
    )(yg, gate, x2, g, b)


SC_WINDOW = 128


def _sc_mesh():
    return plsc.VectorSubcoreMesh(core_axis_name="core", subcore_axis_name="subcore")


def _sc_gather(table, idx):
    m = idx.shape[0]

    @pl.kernel(out_type=jax.ShapeDtypeStruct((m, LANES), table.dtype), mesh=_sc_mesh())
    def gather_kernel(table_hbm, idx_hbm, out_hbm):
        def body(idx_vmem, out_vmem):
            pltpu.sync_copy(table_hbm.at[idx_vmem.at[0]], out_vmem)

        pltpu.emit_pipeline(
            body,
            grid=(m // SC_WINDOW,),
            in_specs=[pl.BlockSpec((1, SC_WINDOW), lambda i: (0, i))],
            out_specs=[pl.BlockSpec((SC_WINDOW, LANES), lambda i: (i, 0))],
            core_axis_name=("core", "subcore"),
            dimension_semantics=(pltpu.PARALLEL,),
        )(idx_hbm, out_hbm)

    return gather_kernel(table, idx.reshape(1, m))


def _sc_scatter(src, idx_lists, out_rows):
    m = src.shape[0]
    n_lists = len(idx_lists)

    @pl.kernel(out_type=jax.ShapeDtypeStruct((out_rows, LANES), src.dtype), mesh=_sc_mesh(),
               scratch_types=[pltpu.SemaphoreType.DMA])
    def scatter_kernel(src_hbm, *refs):
        idx_hbm, out_hbm, sem = refs[:n_lists], refs[n_lists], refs[n_lists + 1]

        def body(src_vmem, *idx_vmem):
            copies = [pltpu.async_copy(src_vmem, out_hbm.at[iv.at[0]], sem) for iv in idx_vmem]
            for cp in copies:
                cp.wait()

        pltpu.emit_pipeline(
            body,
            grid=(m // SC_WINDOW,),
            in_specs=[pl.BlockSpec((SC_WINDOW, LANES), lambda i: (i, 0))]
            + [pl.BlockSpec((1, SC_WINDOW), lambda i: (0, i))] * n_lists,
            out_specs=[],
            core_axis_name=("core", "subcore"),
            dimension_semantics=(pltpu.PARALLEL,),
        )(src_hbm, *idx_hbm)

    return scatter_kernel(src, *[ix.reshape(1, m) for ix in idx_lists])


def _layer(x, mem, w_in, fox_f_bias, conv_w, i_bias, f_bias, fox_g, mlstm_g, w_mix_out,
           ln1_g, ln1_b, w_xq, w_xk, w_xv, w_xo, ln2_g, ln2_b, w_router, b_router,
           w_gate_up, b_gate_up, w_down, b_down, ln3_g, ln3_b):
    batch, seq, d = x.shape
    n_mem = mem.shape[1]
    n = batch * seq
    x2d = x.reshape(n, d)

    o_ff = 3 * FOX_WIDTH
    o_mqk = o_ff + FOX_HEADS
    o_mv = o_mqk + 2 * MLSTM_QK_WIDTH
    o_mi = o_mv + MLSTM_V_WIDTH
    o_mf = o_mi + MLSTM_HEADS
    o_mo = o_mf + MLSTM_HEADS
    n_gate = FOX_HEADS + 2 * MLSTM_HEADS
    w_r = jnp.concatenate(
        [w_in[:, FOX_WIDTH:2 * FOX_WIDTH], w_in[:, o_mqk:o_mv],
         w_in[:, o_ff:o_mqk], w_in[:, o_mi:o_mo],
         jnp.zeros((d, LANES - n_gate), w_in.dtype)], axis=1).astype(BF16)
    w_t = jnp.concatenate([w_in[:, :FOX_WIDTH], w_in[:, 2 * FOX_WIDTH:o_ff],
                           w_in[:, o_mv:o_mi], w_in[:, o_mo:]], axis=1).T.astype(BF16)
    gate_bias = jnp.concatenate(
        [fox_f_bias, i_bias, f_bias, jnp.zeros((LANES - n_gate,), F32)]).reshape(1, LANES)

    fqt, fk, fvt, mqk, mvt, mot, gates = _inproj(x2d, w_r, w_t, gate_bias)
    gcol, grow, cpieces = _gateprep(gates, batch, seq)
    fo = _fox(fqt, fk, cpieces, fvt, fox_g.reshape(1, FOX_WIDTH), batch, seq)
    gain_lanes = jnp.broadcast_to(mlstm_g[:, None], (MLSTM_V_WIDTH, LANES))
    mo_out = _mlstm(mqk, mvt, mot, gcol, grow, conv_w, gain_lanes, batch, seq)
    x1 = _outproj(fo, mo_out, w_mix_out.astype(BF16), x2d, ln1_g.reshape(1, d), ln1_b.reshape(1, d))

    kmem, vmem = _memkv(mem.reshape(batch * n_mem, d), w_xk.astype(BF16), w_xv.astype(BF16), n_mem)
    wrt = w_router.T
    wrt_hi = wrt.astype(BF16)
    wrt_lo = (wrt - wrt_hi.astype(F32)).astype(BF16)
    x2, x2p, logits_t = _xattn(x1, kmem, vmem, w_xq.astype(BF16), w_xo.astype(BF16),
                               ln2_g.reshape(1, d), ln2_b.reshape(1, d),
                               jnp.concatenate([wrt_hi, wrt_lo], axis=0),
                               b_router.reshape(N_EXPERTS, 1), batch, seq, n_mem)

    idx_t, rank_t, gate, cnt = _route(logits_t)
    counts = cnt[:, 0]
    g_rows = EXPERT_ROWS
    padded = ((counts + g_rows - 1) // g_rows) * g_rows
    pad_end = jnp.cumsum(padded)
    pad_start = pad_end - padded
    experts = jnp.arange(N_EXPERTS, dtype=jnp.int32)
    sel = idx_t[:TOP_K, :, None] == experts[None, None, :]
    pos_t = jnp.sum(jnp.where(sel, pad_start[None, None, :], 0), axis=-1) + rank_t[:TOP_K]
    p_rows = n * TOP_K + N_EXPERTS * g_rows
    nb = p_rows // g_rows
    blk_start = jnp.arange(nb, dtype=jnp.int32) * g_rows
    blk_e = jnp.minimum(jnp.sum((pad_end[None, :] <= blk_start[:, None]).astype(jnp.int32), axis=1),
                        N_EXPERTS - 1)
    own = blk_e[:, None] == experts[None, :]
    row_end = jnp.sum(jnp.where(own, (pad_start + counts)[None, :], 0), axis=1)
    blk_rows = jnp.clip(row_end - blk_start, 0, g_rows).astype(jnp.int32)
    later = (experts[None, :] > blk_e[:, None]) & (counts[None, :] > 0)
    next_e = jnp.min(jnp.where(later, experts[None, :], N_EXPERTS), axis=1)
    next_e = jnp.where(next_e == N_EXPERTS, -1, next_e).astype(jnp.int32)

    def piece_index(planes):
        off = jnp.arange(planes, dtype=jnp.int32) * p_rows
        return pos_t[:, None, :] + off[None, :, None]

    x_idx = piece_index(X_PLANES).reshape(TOP_K, X_PLANES * n)
    xs = _sc_scatter(x2p.reshape(X_PLANES * n, LANES), [x_idx[k] for k in range(TOP_K)],
                     X_PLANES * p_rows)
    y = _experts(blk_e, blk_rows, next_e, xs.reshape(X_PLANES, p_rows, LANES), w_gate_up,
                 b_gate_up.reshape(N_EXPERTS, 1, -1), w_down, b_down.reshape(N_EXPERTS, 1, -1))
    yg = _sc_gather(y.reshape(Y_PLANES * p_rows, LANES), piece_index(Y_PLANES).reshape(-1))
    out = _combine(yg.reshape(TOP_K, Y_PLANES, n, LANES), gate, x2,
                   ln3_g.reshape(1, d), ln3_b.reshape(1, d))
    return out.reshape(batch, seq, d)


def kernel(x, mem, w_in, fox_f_bias, mlstm_conv_w, mlstm_i_bias, mlstm_f_bias, fox_norm_g, mlstm_norm_g, w_mix_out, ln1_g, ln1_b, w_xq, w_xk, w_xv, w_xo, ln2_g, ln2_b, w_router, b_router, w_gate_up, b_gate_up, w_down, b_down, ln3_g, ln3_b):
    for l in range(w_in.shape[0]):
        x = _layer(x, mem, w_in[l], fox_f_bias[l], mlstm_conv_w[l], mlstm_i_bias[l],
                   mlstm_f_bias[l], fox_norm_g[l], mlstm_norm_g[l], w_mix_out[l],
                   ln1_g[l], ln1_b[l], w_xq[l], w_xk[l], w_xv[l], w_xo[l], ln2_g[l], ln2_b[l],
                   w_router[l], b_router[l], w_gate_up[l], b_gate_up[l], w_down[l], b_down[l],
                   ln3_g[l], ln3_b[l])
    return x
```

```python
import jax
import jax.numpy as jnp
from jax import lax
from jax.experimental import pallas as pl
from jax.experimental.pallas import tpu as pltpu
from jax.experimental.pallas import tpu_sc as plsc

F32 = jnp.float32
BF16 = jnp.bfloat16

D_MODEL = 1024
FOX_HEADS = 8
FOX_HEAD_DIM = 64
FOX_WIDTH = FOX_HEADS * FOX_HEAD_DIM
MLSTM_HEADS = 4
MLSTM_QK_DIM = 64
MLSTM_V_DIM = 128
MLSTM_QK_WIDTH = MLSTM_HEADS * MLSTM_QK_DIM
MLSTM_V_WIDTH = MLSTM_HEADS * MLSTM_V_DIM
CONV_WIDTH = 4
XATTN_HEADS = 4
XATTN_HEAD_DIM = D_MODEL // XATTN_HEADS
N_EXPERTS = 32
TOP_K = 4
D_EXPERT = D_MODEL
SWIGLU_LIMIT = 7.0
SWIGLU_ALPHA = 1.702
DEEPNORM_ALPHA = 2.0 ** 0.25
LN_EPS = 1e-5
RMS_EPS = 1e-6

LANES = 128
SEQ_BLOCK = 256
ROW_TILE = 512
XATTN_TILE = 1024
EXPERT_ROWS = 512
EXPERT_CHUNK = 256
ROUTE_TILE = 512
X_PLANES = D_MODEL // 2 // LANES
Y_PLANES = X_PLANES
GATE_I0 = FOX_HEADS
GATE_F0 = FOX_HEADS + MLSTM_HEADS

MIB = 1024 * 1024


def _params(semantics, vmem_mib):
    return pltpu.CompilerParams(dimension_semantics=semantics,
                                vmem_limit_bytes=vmem_mib * MIB)


def _layer_norm(y, g, b):
    mu = jnp.mean(y, axis=-1, keepdims=True)
    yc = y - mu
    var = jnp.mean(yc * yc, axis=-1, keepdims=True)
    return yc * lax.rsqrt(var + LN_EPS) * g + b


def _dot(a, b):
    return jnp.dot(a, b, preferred_element_type=F32)


def _dot_nt(a, b):
    return lax.dot_general(a, b, (((1,), (1,)), ((), ())), preferred_element_type=F32)


def _dot_tn(a, b):
    return lax.dot_general(a, b, (((0,), (0,)), ((), ())), preferred_element_type=F32)


def _pack_bf16_pairs(x):
    w = x.shape[1] // 2
    lo = pltpu.bitcast(x[:, :w].astype(BF16).astype(F32), jnp.uint32)
    hi = pltpu.bitcast(x[:, w:].astype(BF16).astype(F32), jnp.uint32)
    return (lo >> 16) | hi


def _unpack_bf16_pairs(u, dtype=BF16):
    lo = pltpu.bitcast(u << 16, F32).astype(dtype)
    hi = pltpu.bitcast(u & jnp.uint32(0xFFFF0000), F32).astype(dtype)
    return jnp.concatenate([lo, hi], axis=1)


def _inproj_kernel(x_ref, w_ref, wt_ref, gb_ref, fqt_ref, fk_ref, fvt_ref, mqk_ref, mvt_ref,
                   mot_ref, g_ref):
    xb = x_ref[...].astype(BF16)

    def mm(c0, width):
        return _dot(xb, w_ref[:, c0:c0 + width])

    def mm_t(r0, height):
        return _dot_nt(wt_ref[r0:r0 + height, :], xb)

    r_fv = FOX_WIDTH
    r_mv = r_fv + FOX_WIDTH
    r_mo = r_mv + MLSTM_V_WIDTH
    fqt_ref[...] = (mm_t(0, FOX_WIDTH) * (FOX_HEAD_DIM ** -0.5)).astype(BF16)
    fvt_ref[...] = mm_t(r_fv, FOX_WIDTH).astype(BF16)
    mvt_ref[...] = mm_t(r_mv, MLSTM_V_WIDTH).astype(BF16)
    mot_ref[...] = mm_t(r_mo, MLSTM_V_WIDTH)
    c_qk = FOX_WIDTH
    c_g = c_qk + 2 * MLSTM_QK_WIDTH
    fk_ref[...] = mm(0, FOX_WIDTH).astype(BF16)
    mqk_ref[...] = mm(c_qk, 2 * MLSTM_QK_WIDTH)
    g_ref[...] = mm(c_g, LANES) + gb_ref[...]


def _inproj(x2d, w_r, w_t, gate_bias):
    n = x2d.shape[0]
    tm = ROW_TILE
    row = lambda width: pl.BlockSpec((tm, width), lambda i: (i, 0))
    col = lambda height: pl.BlockSpec((height, tm), lambda i: (0, i))
    full = lambda a: pl.BlockSpec(a.shape, lambda i: (0, 0))
    out_shapes = (
        jax.ShapeDtypeStruct((FOX_WIDTH, n), BF16),
        jax.ShapeDtypeStruct((n, FOX_WIDTH), BF16),
        jax.ShapeDtypeStruct((FOX_WIDTH, n), BF16),
        jax.ShapeDtypeStruct((n, 2 * MLSTM_QK_WIDTH), F32),
        jax.ShapeDtypeStruct((MLSTM_V_WIDTH, n), BF16),
        jax.ShapeDtypeStruct((MLSTM_V_WIDTH, n), F32),
        jax.ShapeDtypeStruct((n, LANES), F32),
    )
    return pl.pallas_call(
        _inproj_kernel,
        grid=(n // tm,),
        in_specs=[row(D_MODEL), full(w_r), full(w_t), full(gate_bias)],
        out_specs=(col(FOX_WIDTH), row(FOX_WIDTH), col(FOX_WIDTH), row(2 * MLSTM_QK_WIDTH),
                   col(MLSTM_V_WIDTH), col(MLSTM_V_WIDTH), row(LANES)),
        out_shape=out_shapes,
        compiler_params=_params(("parallel",), 48),
    )(x2d, w_r, w_t, gate_bias)


def _split3(x):
    hi = x.astype(BF16)
    r1 = x - hi.astype(F32)
    mid = r1.astype(BF16)
    lo = (r1 - mid.astype(F32)).astype(BF16)
    return hi, mid, lo


def _gateprep_kernel(g_ref, sel_ref, col_ref, row_ref, cp_ref, carry_ref):
    c = pl.program_id(1)

    @pl.when(c == 0)
    def _():
        carry_ref[...] = jnp.zeros_like(carry_ref)

    g = g_ref[...]
    lane = lax.broadcasted_iota(jnp.int32, g.shape, 1)
    is_i = (lane >= GATE_I0) & (lane < GATE_F0)
    logsig = jnp.minimum(g, 0.0) - jnp.log1p(jnp.exp(-jnp.abs(g)))
    blk = g.shape[0]
    r = lax.broadcasted_iota(jnp.int32, (blk, blk), 0)
    s = lax.broadcasted_iota(jnp.int32, (blk, blk), 1)
    tri = (s <= r).astype(BF16)
    cs = None
    for piece in _split3(logsig):
        term = _dot(tri, piece)
        cs = term if cs is None else cs + term
    carry = carry_ref[0:1, :]
    glob = cs + carry
    carry_ref[...] = jnp.broadcast_to(glob[blk - 1:blk, :], carry_ref.shape)
    out = jnp.where(lane < GATE_I0, glob, jnp.where(is_i, g, cs))
    col_ref[...] = out
    row_ref[0] = out.T[0:16, :]

    pieces = jnp.concatenate(_split3(-glob), axis=1)
    moved = _dot(pieces, sel_ref[...])
    for p in range(FOX_HEADS // 2):
        cp_ref[0, p] = moved[:, p * LANES:(p + 1) * LANES].astype(BF16)


def _piece_selector():
    src = jnp.arange(3 * LANES, dtype=jnp.int32)[:, None]
    dst = jnp.arange(4 * LANES, dtype=jnp.int32)[None, :]
    piece, head = src // LANES, src % LANES
    pair, lane = dst // LANES, dst % LANES
    hit = (lane < 6) & (lane % 3 == piece) & (head == 2 * pair + lane // 3)
    return hit.astype(BF16)


def _gateprep(gates, batch, seq):
    n = gates.shape[0]
    nc = seq // SEQ_BLOCK
    sel = _piece_selector()
    return pl.pallas_call(
        _gateprep_kernel,
        grid=(batch, nc),
        in_specs=[pl.BlockSpec((SEQ_BLOCK, LANES), lambda b, c: (b * nc + c, 0)),
                  pl.BlockSpec(sel.shape, lambda b, c: (0, 0))],
        out_specs=(pl.BlockSpec((SEQ_BLOCK, LANES), lambda b, c: (b * nc + c, 0)),
                   pl.BlockSpec((1, 16, SEQ_BLOCK), lambda b, c: (b, 0, c)),
                   pl.BlockSpec((1, FOX_HEADS // 2, SEQ_BLOCK, LANES), lambda b, c: (b, 0, c, 0))),
        out_shape=(jax.ShapeDtypeStruct((n, LANES), F32),
                   jax.ShapeDtypeStruct((batch, 16, seq), F32),
                   jax.ShapeDtypeStruct((batch, FOX_HEADS // 2, seq, LANES), BF16)),
        scratch_shapes=[pltpu.VMEM((8, LANES), F32)],
        compiler_params=_params(("parallel", "arbitrary"), 32),
    )(gates, sel)


FOX_ONES_ROWS = 16
FOX_VAUG_ROWS = FOX_HEAD_DIM + FOX_ONES_ROWS
FOX_QUERY_TILE = 2 * SEQ_BLOCK


def _fox_kernel(qt_ref, k_ref, cp_ref, vt_ref, gain_ref, o_ref,
                kaug_sc, vaug_sc, qaug_sc, sta_sc, stb_sc, m_sc, acc0_sc, acc1_sc):
    qi = pl.program_id(2)
    tq = qt_ref.shape[1]
    tg = tq
    tk = SEQ_BLOCK
    hd = FOX_HEAD_DIM
    seq = k_ref.shape[0]

    @pl.when(qi == 0)
    def _():
        lane = lax.broadcasted_iota(jnp.int32, (tk, LANES), 1)

        def build(blk, carry):
            r0 = pl.multiple_of(blk * tk, tk)
            kp = k_ref[pl.ds(r0, tk), :].astype(F32)
            cp = cp_ref[0, 0, pl.ds(r0, tk), :].astype(F32)
            for h in range(2):
                kh = kp if h == 0 else pltpu.roll(kp, hd, axis=1)
                ch = pltpu.roll(cp, hd - 3 * h, axis=1)
                kaug = jnp.where(lane < hd, kh, jnp.where(lane < hd + 3, ch, 0.0))
                kaug_sc[h, pl.ds(r0, tk), :] = kaug.astype(BF16)
            return carry

        lax.fori_loop(0, seq // tk, build, 0)
        for h in range(2):
            vaug_sc[h, 0:hd, :] = vt_ref[h * hd:(h + 1) * hd, :]
            vaug_sc[h, hd:, :] = jnp.ones((FOX_ONES_ROWS, seq), BF16)

    ones3 = (lax.broadcasted_iota(jnp.int32, (FOX_ONES_ROWS, tq), 0) < 3).astype(BF16)
    for h in range(2):
        qaug_sc[h, 0:hd, :] = qt_ref[h * hd:(h + 1) * hd, :]
        qaug_sc[h, hd:hd + FOX_ONES_ROWS, :] = ones3
        qaug_sc[h, hd + FOX_ONES_ROWS:, :] = jnp.zeros((LANES - hd - FOX_ONES_ROWS, tq), BF16)

    def put_scores(kg, slot, masked=False):
        k0 = pl.multiple_of(kg * tg, tg)
        for h in range(2):
            st = _dot(kaug_sc[h, pl.ds(k0, tg), :], qaug_sc[h])
            if masked:
                kk = lax.broadcasted_iota(jnp.int32, (tg, tq), 0)
                tt = lax.broadcasted_iota(jnp.int32, (tg, tq), 1)
                st = jnp.where(kk <= tt, st, -jnp.inf)
            slot[h] = st

    m_sc[...] = jnp.full(m_sc.shape, -jnp.inf, F32)
    acc0_sc[...] = jnp.zeros(acc0_sc.shape, F32)
    acc1_sc[...] = jnp.zeros(acc1_sc.shape, F32)
    acc = (acc0_sc, acc1_sc)

    def absorb(kg, slot):
        k0 = pl.multiple_of(kg * tg, tg)
        half = tg // 2
        for h in range(2):
            st = slot[h]
            m_prev = m_sc[h]
            m_new = jnp.maximum(m_prev, jnp.max(st, axis=0, keepdims=True))
            pv = None
            for u in range(2):
                p = jnp.exp(st[u * half:(u + 1) * half] - m_new)
                part = _dot(vaug_sc[h, :, pl.ds(k0 + u * half, half)], p.astype(BF16))
                pv = part if pv is None else pv + part
            acc[h][...] = jnp.exp(m_prev - m_new) * acc[h][...] + pv
            m_sc[h] = m_new

    sa, sb = sta_sc, stb_sc
    n_loop = jnp.maximum(qi - 1, 0) // 2

    @pl.when(qi > 0)
    def _():
        put_scores(0, sa)

    def body(j, carry):
        put_scores(2 * j + 1, sb)
        absorb(2 * j, sa)
        put_scores(2 * j + 2, sa)
        absorb(2 * j + 1, sb)
        return carry

    lax.fori_loop(0, n_loop, body, 0)
    done = 2 * n_loop

    @pl.when(qi == 0)
    def _():
        put_scores(qi, sa, masked=True)
        absorb(qi, sa)

    @pl.when((qi > 0) & (qi - done == 1))
    def _():
        put_scores(qi, sb, masked=True)
        absorb(done, sa)
        absorb(qi, sb)

    @pl.when((qi > 0) & (qi - done == 2))
    def _():
        put_scores(done + 1, sb)
        absorb(done, sa)
        put_scores(qi, sa, masked=True)
        absorb(done + 1, sb)
        absorb(qi, sa)

    a0 = acc0_sc[...]
    a1 = acc1_sc[...]
    ot = jnp.concatenate([a0[0:hd] / a0[hd:hd + 1], a1[0:hd] / a1[hd:hd + 1]], axis=0)
    o = ot.T
    lane = lax.broadcasted_iota(jnp.int32, (tq, LANES), 1)
    lo = lane < FOX_HEAD_DIM
    sq = o * o
    ss0 = jnp.sum(jnp.where(lo, sq, 0.0), axis=-1, keepdims=True)
    ss1 = jnp.sum(jnp.where(lo, 0.0, sq), axis=-1, keepdims=True)
    inv = jnp.where(lo, lax.rsqrt(ss0 / FOX_HEAD_DIM + RMS_EPS),
                    lax.rsqrt(ss1 / FOX_HEAD_DIM + RMS_EPS))
    o_ref[...] = (o * inv * gain_ref[...]).astype(o_ref.dtype)


def _fox(fqt, fk, cpieces, fvt, fox_gain, batch, seq):
    n = fk.shape[0]
    tq = FOX_QUERY_TILE
    nq = seq // tq
    npair = FOX_HEADS // 2
    return pl.pallas_call(
        _fox_kernel,
        grid=(batch, npair, nq),
        in_specs=[pl.BlockSpec((LANES, tq), lambda b, hp, qi: (hp, b * nq + qi)),
                  pl.BlockSpec((seq, LANES), lambda b, hp, qi: (b, hp)),
                  pl.BlockSpec((1, 1, seq, LANES), lambda b, hp, qi: (b, hp, 0, 0)),
                  pl.BlockSpec((LANES, seq), lambda b, hp, qi: (hp, b)),
                  pl.BlockSpec((1, LANES), lambda b, hp, qi: (0, hp))],
        out_specs=pl.BlockSpec((tq, LANES), lambda b, hp, qi: (b * nq + qi, hp)),
        out_shape=jax.ShapeDtypeStruct((n, FOX_WIDTH), BF16),
        scratch_shapes=[pltpu.VMEM((2, seq, LANES), BF16),
                        pltpu.VMEM((2, FOX_VAUG_ROWS, seq), BF16),
                        pltpu.VMEM((2, LANES, tq), BF16),
                        pltpu.VMEM((2, tq, tq), F32),
                        pltpu.VMEM((2, tq, tq), F32),
                        pltpu.VMEM((2, 1, tq), F32),
                        pltpu.VMEM((FOX_VAUG_ROWS, tq), F32),
                        pltpu.VMEM((FOX_VAUG_ROWS, tq), F32)],
        compiler_params=_params(("parallel", "parallel", "arbitrary"), 48),
    )(fqt, fk, cpieces, fvt, fox_gain)


MLSTM_ONES_ROWS = 16


def _mlstm_kernel(qk_ref, vt_ref, ogt_ref, col_ref, row_ref, cw_ref, gain_ref, o_ref,
                  tail_ref, buf_ref, c_sc, m_sc):
    c = pl.program_id(1)
    L = SEQ_BLOCK

    @pl.when(c == 0)
    def _():
        tail_ref[...] = jnp.zeros_like(tail_ref)
        c_sc[...] = jnp.zeros_like(c_sc)
        m_sc[...] = jnp.zeros_like(m_sc)

    x = qk_ref[...]
    buf_ref[0:8, :] = tail_ref[...]
    buf_ref[8:8 + L, :] = x
    tail_ref[...] = x[L - 8:L, :]
    y = x * cw_ref[CONV_WIDTH - 1:CONV_WIDTH, :]
    for j in range(CONV_WIDTH - 1):
        shift = CONV_WIDTH - 1 - j
        y = y + buf_ref[8 - shift:8 - shift + L, :] * cw_ref[j:j + 1, :]
    y = y * jax.nn.sigmoid(y)
    kf = y[:, MLSTM_QK_WIDTH:] * (MLSTM_QK_DIM ** -0.5)
    n_pair = MLSTM_HEADS // 2
    qt = [y[:, p * LANES:(p + 1) * LANES].T for p in range(n_pair)]
    kb = [kf[:, p * LANES:(p + 1) * LANES].astype(BF16) for p in range(n_pair)]

    col = col_ref[...]
    row = row_ref[0]
    lane = lax.broadcasted_iota(jnp.int32, (L, LANES), 1)
    sub = lax.broadcasted_iota(jnp.int32, (LANES, L), 0)
    ss = lax.broadcasted_iota(jnp.int32, (L, L), 0)
    ll = lax.broadcasted_iota(jnp.int32, (L, L), 1)
    causal = ss <= ll
    ones_rows = jnp.ones((MLSTM_ONES_ROWS, L), BF16)

    for h in range(MLSTM_HEADS):
        pair, half = divmod(h, 2)
        head_sub = (sub < MLSTM_QK_DIM) if half == 0 else (sub >= MLSTM_QK_DIM)
        head_lane = (lane < MLSTM_QK_DIM) if half == 0 else (lane >= MLSTM_QK_DIM)
        qth = jnp.where(head_sub, qt[pair], 0.0).astype(BF16)
        rcol = col[:, GATE_I0 + h:GATE_I0 + h + 1] - col[:, GATE_F0 + h:GATE_F0 + h + 1]
        brow = row[GATE_F0 + h:GATE_F0 + h + 1, :]
        lirow = row[GATE_I0 + h:GATE_I0 + h + 1, :]
        g = brow[:, L - 1:L]
        m_prev = m_sc[h][0:1, 0:1]

        dt = jnp.where(causal, rcol + brow, -jnp.inf)
        inter_log = brow + m_prev
        m_t = jnp.maximum(inter_log, jnp.max(dt, axis=0, keepdims=True))
        w_inter = jnp.exp(inter_log - m_t)
        pt = jnp.exp(dt - m_t) * _dot(kb[pair], qth)
        vaug = jnp.concatenate([vt_ref[h * LANES:(h + 1) * LANES, :], ones_rows], axis=0)
        cstate = c_sc[h]
        tot = w_inter * _dot(cstate.astype(BF16), qth) + _dot(vaug, pt.astype(BF16))
        den = tot[MLSTM_V_DIM:MLSTM_V_DIM + 1]
        hout = tot[0:MLSTM_V_DIM] * (1.0 / jnp.maximum(jnp.abs(den), jnp.exp(-m_t)))

        a = g + (lirow - brow)
        m_loc = jnp.max(a, axis=1, keepdims=True)
        vw = (vaug.astype(F32) * jnp.exp(a - m_loc)).astype(BF16)
        kmask = jnp.where(head_lane, kb[pair], jnp.zeros_like(kb[pair]))
        kv = _dot(vw, kmask)
        m_new = jnp.maximum(g + m_prev, m_loc)
        c_sc[h] = jnp.exp(g + m_prev - m_new) * cstate + jnp.exp(m_loc - m_new) * kv
        m_sc[h] = jnp.broadcast_to(m_new, m_sc.shape[1:])

        ms = jnp.mean(hout * hout, axis=0, keepdims=True)
        gain = gain_ref[h * LANES:(h + 1) * LANES, :]
        hn = hout * lax.rsqrt(ms + RMS_EPS) * jnp.concatenate([gain] * (L // LANES), axis=1)
        gate = jax.nn.sigmoid(ogt_ref[h * LANES:(h + 1) * LANES, :])
        o_ref[:, h * LANES:(h + 1) * LANES] = (hn * gate).T.astype(o_ref.dtype)


def _mlstm(mqk, mvt, mot, gcol, grow, conv_w, gain_lanes, batch, seq):
    n = mqk.shape[0]
    L = SEQ_BLOCK
    nc = seq // L
    qk_width = 2 * MLSTM_QK_WIDTH
    row = lambda width: pl.BlockSpec((L, width), lambda b, c: (b * nc + c, 0))
    col = pl.BlockSpec((MLSTM_V_WIDTH, L), lambda b, c: (0, b * nc + c))
    full = lambda a: pl.BlockSpec(a.shape, lambda b, c: (0, 0))
    return pl.pallas_call(
        _mlstm_kernel,
        grid=(batch, nc),
        in_specs=[row(qk_width), col, col, row(LANES),
                  pl.BlockSpec((1, 16, L), lambda b, c: (b, 0, c)),
                  full(conv_w), full(gain_lanes)],
        out_specs=row(MLSTM_V_WIDTH),
        out_shape=jax.ShapeDtypeStruct((n, MLSTM_V_WIDTH), BF16),
        scratch_shapes=[pltpu.VMEM((8, qk_width), F32), pltpu.VMEM((8 + L, qk_width), F32),
                        pltpu.VMEM((MLSTM_HEADS, MLSTM_V_DIM + MLSTM_ONES_ROWS, LANES), F32),
                        pltpu.VMEM((MLSTM_HEADS, 8, LANES), F32)],
        compiler_params=_params(("parallel", "arbitrary"), 48),
    )(mqk, mvt, mot, gcol, grow, conv_w, gain_lanes)


def _outproj_kernel(fo_ref, mo_ref, w_ref, x_ref, g_ref, b_ref, o_ref):
    mix = _dot(fo_ref[...], w_ref[0:FOX_WIDTH, :]) + _dot(mo_ref[...], w_ref[FOX_WIDTH:, :])
    o_ref[...] = _layer_norm(DEEPNORM_ALPHA * x_ref[...] + mix, g_ref[...], b_ref[...])


def _outproj(fo, mo, w_out, x2d, g, b):
    n = x2d.shape[0]
    tm = ROW_TILE
    row = lambda width: pl.BlockSpec((tm, width), lambda i: (i, 0))
    full = lambda a: pl.BlockSpec(a.shape, lambda i: (0, 0))
    return pl.pallas_call(
        _outproj_kernel,
        grid=(n // tm,),
        in_specs=[row(512), row(512), full(w_out), row(D_MODEL), full(g), full(b)],
        out_specs=row(D_MODEL),
        out_shape=jax.ShapeDtypeStruct((n, D_MODEL), F32),
        compiler_params=_params(("parallel",), 48),
    )(fo, mo, w_out, x2d, g, b)


def _memkv_kernel(mem_ref, wk_ref, wv_ref, k_ref, v_ref):
    mb = mem_ref[...].astype(BF16)
    k_ref[...] = (_dot(mb, wk_ref[...]) * (XATTN_HEAD_DIM ** -0.5)).astype(BF16)
    v_ref[...] = _dot(mb, wv_ref[...]).astype(BF16)


def _memkv(mem2d, wk, wv, n_mem):
    n = mem2d.shape[0]
    row = pl.BlockSpec((n_mem, D_MODEL), lambda i: (i, 0))
    full = lambda a: pl.BlockSpec(a.shape, lambda i: (0, 0))
    return pl.pallas_call(
        _memkv_kernel,
        grid=(n // n_mem,),
        in_specs=[row, full(wk), full(wv)],
        out_specs=(row, row),
        out_shape=(jax.ShapeDtypeStruct((n, D_MODEL), BF16),) * 2,
        compiler_params=_params(("parallel",), 32),
    )(mem2d, wk, wv)


def _xattn_kernel(x_ref, k_ref, v_ref, wq_ref, wo_ref, g_ref, b_ref, wr_ref, br_ref,
                  o_ref, ob_ref, lg_ref):
    half = x_ref.shape[0] // 2
    for r in range(2):
        rows = slice(r * half, (r + 1) * half)
        x = x_ref[rows, :]
        q = _dot(x.astype(BF16), wq_ref[...]).astype(BF16)
        outs = []
        for h in range(XATTN_HEADS):
            sl = slice(h * XATTN_HEAD_DIM, (h + 1) * XATTN_HEAD_DIM)
            s = _dot_nt(q[:, sl], k_ref[:, sl])
            p = jnp.exp(s - jnp.max(s, axis=-1, keepdims=True))
            l = jnp.sum(p, axis=-1, keepdims=True)
            outs.append((_dot(p.astype(BF16), v_ref[:, sl]) / l).astype(BF16))
        o = jnp.concatenate(outs, axis=1)
        xa = _dot(o, wo_ref[...])
        x2 = _layer_norm(DEEPNORM_ALPHA * x + xa, g_ref[...], b_ref[...])
        o_ref[rows, :] = x2
        packed = _pack_bf16_pairs(x2)
        for j in range(ob_ref.shape[0]):
            ob_ref[j, rows, :] = packed[:, j * LANES:(j + 1) * LANES]
        x2h = x2.astype(BF16)
        x2l = (x2 - x2h.astype(F32)).astype(BF16)
        a = _dot_nt(wr_ref[...], x2h)
        b = _dot_nt(wr_ref[0:N_EXPERTS, :], x2l)
        lg_ref[:, rows] = a[0:N_EXPERTS] + a[N_EXPERTS:] + b + br_ref[...]


def _xattn(x1, kmem, vmem, wq, wo, g, b, wr, br, batch, seq, n_mem):
    n = x1.shape[0]
    tm = XATTN_TILE
    nt = seq // tm
    row = lambda width: pl.BlockSpec((tm, width), lambda bi, i: (bi * nt + i, 0))
    full = lambda a: pl.BlockSpec(a.shape, lambda bi, i: (0, 0))
    kv = pl.BlockSpec((n_mem, D_MODEL), lambda bi, i: (bi, 0))
    return pl.pallas_call(
        _xattn_kernel,
        grid=(batch, nt),
        in_specs=[row(D_MODEL), kv, kv, full(wq), full(wo), full(g), full(b), full(wr), full(br)],
        out_specs=(row(D_MODEL),
                   pl.BlockSpec((X_PLANES, tm, LANES), lambda bi, i: (0, bi * nt + i, 0)),
                   pl.BlockSpec((N_EXPERTS, tm), lambda bi, i: (0, bi * nt + i))),
        out_shape=(jax.ShapeDtypeStruct((n, D_MODEL), F32),
                   jax.ShapeDtypeStruct((X_PLANES, n, LANES), jnp.uint32),
                   jax.ShapeDtypeStruct((N_EXPERTS, n), F32)),
        compiler_params=_params(("parallel", "parallel"), 60),
    )(x1, kmem, vmem, wq, wo, g, b, wr, br)


def _route_kernel(lg_ref, idx_ref, rank_ref, gate_ref, cnt_ref, carry_ref):
    i = pl.program_id(0)

    @pl.when(i == 0)
    def _():
        carry_ref[...] = jnp.zeros_like(carry_ref)

    lg = lg_ref[...]
    t = lg.shape[1]
    e_idx = lax.broadcasted_iota(jnp.int32, lg.shape, 0).astype(F32)
    sels, vals, idxs = [], [], []
    for _ in range(TOP_K):
        mx = jnp.max(lg, axis=0, keepdims=True)
        first = jnp.min(jnp.where(lg == mx, e_idx, float(N_EXPERTS)), axis=0, keepdims=True)
        sel = e_idx == first
        sels.append(sel)
        vals.append(mx)
        idxs.append(first)
        lg = jnp.where(sel, -jnp.inf, lg)
    exps = [jnp.exp(v - vals[0]) for v in vals]
    tot = exps[0] + exps[1] + exps[2] + exps[3]

    selmat = (sels[0] | sels[1] | sels[2] | sels[3])
    r = lax.broadcasted_iota(jnp.int32, (t, t), 0)
    s = lax.broadcasted_iota(jnp.int32, (t, t), 1)
    earlier = (r < s).astype(BF16)
    carry = carry_ref[:, 0:1]
    rankmat = _dot(selmat.astype(BF16), earlier) + carry
    new_carry = carry + jnp.sum(selmat.astype(F32), axis=1, keepdims=True)
    carry_ref[...] = jnp.broadcast_to(new_carry, carry_ref.shape)
    cnt_ref[...] = jnp.broadcast_to(new_carry, cnt_ref.shape).astype(jnp.int32)

    row8 = lax.broadcasted_iota(jnp.int32, (8, t), 0)
    row128 = lax.broadcasted_iota(jnp.int32, (LANES, t), 0)
    idx_out = jnp.zeros((8, t), F32)
    rank_out = jnp.zeros((8, t), F32)
    gate_out = jnp.zeros((LANES, t), F32)
    for k in range(TOP_K):
        rk = jnp.sum(jnp.where(sels[k], rankmat, 0.0), axis=0, keepdims=True)
        idx_out = jnp.where(row8 == k, idxs[k], idx_out)
        rank_out = jnp.where(row8 == k, rk, rank_out)
        gate_out = jnp.where(row128 == k, exps[k] / tot, gate_out)
    idx_ref[...] = idx_out.astype(jnp.int32)
    rank_ref[...] = rank_out.astype(jnp.int32)
    gate_ref[...] = gate_out.T


def _route(logits_t):
    n = logits_t.shape[1]
    t = ROUTE_TILE
    col = lambda rows: pl.BlockSpec((rows, t), lambda i: (0, i))
    return pl.pallas_call(
        _route_kernel,
        grid=(n // t,),
        in_specs=[col(N_EXPERTS)],
        out_specs=(col(8), col(8), pl.BlockSpec((t, LANES), lambda i: (i, 0)),
                   pl.BlockSpec((N_EXPERTS, LANES), lambda i: (0, 0))),
        out_shape=(jax.ShapeDtypeStruct((8, n), jnp.int32),
                   jax.ShapeDtypeStruct((8, n), jnp.int32),
                   jax.ShapeDtypeStruct((n, LANES), F32),
                   jax.ShapeDtypeStruct((N_EXPERTS, LANES), jnp.int32)),
        scratch_shapes=[pltpu.VMEM((N_EXPERTS, LANES), F32)],
        compiler_params=_params(("arbitrary",), 32),
    )(logits_t)


def _expert_kernel(blk_e_ref, blk_rows_ref, next_e_ref, x_ref, wgu_hbm, bgu_ref, wd_hbm, bd_ref,
                   y_ref, wgu_f32, wd_f32, wgu_sc, wd_sc, state_ref, sem):
    i = pl.program_id(0)
    e = blk_e_ref[i]
    rows = blk_rows_ref[i]
    g = y_ref.shape[1]

    def weight_copies(expert, slot):
        return (pltpu.make_async_copy(wgu_hbm.at[expert], wgu_f32.at[slot], sem.at[0, slot]),
                pltpu.make_async_copy(wd_hbm.at[expert], wd_f32.at[slot], sem.at[1, slot]))

    @pl.when(i == 0)
    def _():
        state_ref[0] = -1
        state_ref[1] = 0

    @pl.when((rows > 0) & (e != state_ref[0]))
    def _():
        slot = state_ref[1]

        @pl.when(i == 0)
        def _():
            for cp in weight_copies(e, slot):
                cp.start()

        for cp in weight_copies(e, slot):
            cp.wait()
        wgu_sc[...] = wgu_f32[slot].astype(BF16)
        wd_sc[...] = wd_f32[slot].astype(BF16)
        nxt = next_e_ref[i]

        @pl.when(nxt >= 0)
        def _():
            for cp in weight_copies(nxt, 1 - slot):
                cp.start()

        state_ref[0] = e
        state_ref[1] = 1 - slot

    def ffn(m):
        packed = jnp.concatenate([x_ref[j, 0:m, :] for j in range(X_PLANES)], axis=1)
        xb = _unpack_bf16_pairs(packed)
        hids = []
        for c in range(D_EXPERT // EXPERT_CHUNK):
            g0 = c * EXPERT_CHUNK
            l0 = D_EXPERT + g0
            gate = _dot(xb, wgu_sc[:, g0:g0 + EXPERT_CHUNK]) + bgu_ref[0, :, g0:g0 + EXPERT_CHUNK]
            lin = _dot(xb, wgu_sc[:, l0:l0 + EXPERT_CHUNK]) + bgu_ref[0, :, l0:l0 + EXPERT_CHUNK]
            gate = jnp.minimum(gate, SWIGLU_LIMIT)
            lin = jnp.clip(lin, -SWIGLU_LIMIT, SWIGLU_LIMIT)
            hids.append((gate * jax.nn.sigmoid(SWIGLU_ALPHA * gate) * (lin + 1.0)).astype(BF16))
        y = _pack_bf16_pairs(_dot(jnp.concatenate(hids, axis=1), wd_sc[...]) + bd_ref[0])
        for j in range(Y_PLANES):
            y_ref[j, 0:m, :] = y[:, j * LANES:(j + 1) * LANES]

    @pl.when(rows > g // 2)
    def _():
        ffn(g)

    @pl.when((rows > 0) & (rows <= g // 2))
    def _():
        ffn(g // 2)
        y_ref[:, g // 2:, :] = jnp.zeros((Y_PLANES, g // 2, LANES), y_ref.dtype)

    @pl.when(rows == 0)
    def _():
        y_ref[...] = jnp.zeros_like(y_ref)


def _experts(blk_e, blk_rows, next_e, xs, w_gu, b_gu, w_d, b_d):
    p = xs.shape[1]
    g = EXPERT_ROWS
    grid_spec = pltpu.PrefetchScalarGridSpec(
        num_scalar_prefetch=3,
        grid=(p // g,),
        in_specs=[pl.BlockSpec((X_PLANES, g, LANES), lambda i, be, br, ne: (0, i, 0)),
                  pl.BlockSpec(memory_space=pl.ANY),
                  pl.BlockSpec((1, 1, 2 * D_EXPERT), lambda i, be, br, ne: (be[i], 0, 0)),
                  pl.BlockSpec(memory_space=pl.ANY),
                  pl.BlockSpec((1, 1, D_MODEL), lambda i, be, br, ne: (be[i], 0, 0))],
        out_specs=pl.BlockSpec((Y_PLANES, g, LANES), lambda i, be, br, ne: (0, i, 0)),
        scratch_shapes=[pltpu.VMEM((2, D_MODEL, 2 * D_EXPERT), F32),
                        pltpu.VMEM((2, D_EXPERT, D_MODEL), F32),
                        pltpu.VMEM((D_MODEL, 2 * D_EXPERT), BF16),
                        pltpu.VMEM((D_EXPERT, D_MODEL), BF16),
                        pltpu.SMEM((2,), jnp.int32),
                        pltpu.SemaphoreType.DMA((2, 2))],
    )
    return pl.pallas_call(
        _expert_kernel,
        grid_spec=grid_spec,
        out_shape=jax.ShapeDtypeStruct((Y_PLANES, p, LANES), jnp.uint32),
        compiler_params=_params(("arbitrary",), 56),
    )(blk_e, blk_rows, next_e, xs, w_gu, b_gu, w_d, b_d)


def _combine_kernel(y_ref, gate_ref, x_ref, g_ref, b_ref, o_ref):
    gate = gate_ref[...]
    ff = None
    for k in range(TOP_K):
        packed = jnp.concatenate([y_ref[k, j] for j in range(Y_PLANES)], axis=1)
        yk = _unpack_bf16_pairs(packed, F32) * gate[:, k:k + 1]
        ff = yk if ff is None else ff + yk
    o_ref[...] = _layer_norm(DEEPNORM_ALPHA * x_ref[...] + ff, g_ref[...], b_ref[...])


def _combine(yg, gate, x2, g, b):
    n = x2.shape[0]
    tm = SEQ_BLOCK
    row = lambda width: pl.BlockSpec((tm, width), lambda i: (i, 0))
    full = lambda a: pl.BlockSpec(a.shape, lambda i: (0, 0))
    return pl.pallas_call(
        _combine_kernel,
        grid=(n // tm,),
        in_specs=[pl.BlockSpec((TOP_K, Y_PLANES, tm, LANES), lambda i: (0, 0, i, 0)),
                  row(LANES), row(D_MODEL), full(g), full(b)],
        out_specs=row(D_MODEL),
        out_shape=jax.ShapeDtypeStruct((n, D_MODEL), F32),
        compiler_params=_params(("parallel",), 32),
    )(yg, gate, x2, g, b)


SC_WINDOW = 128
SC_GATHERS_IN_FLIGHT = 2


def _sc_mesh():
    return plsc.VectorSubcoreMesh(core_axis_name="core", subcore_axis_name="subcore")


def _sc_gather(table, idx):
    m = idx.shape[0]
    n_fly = SC_GATHERS_IN_FLIGHT

    @pl.kernel(out_type=jax.ShapeDtypeStruct((m, LANES), table.dtype), mesh=_sc_mesh(),
               scratch_types=[pltpu.SemaphoreType.DMA])
    def gather_kernel(table_hbm, idx_hbm, out_hbm, sem):
        def body(*refs):
            idx_vmem, out_vmem = refs[:n_fly], refs[n_fly]
            copies = [pltpu.async_copy(table_hbm.at[iv.at[0]],
                                       out_vmem.at[pl.ds(u * SC_WINDOW, SC_WINDOW)], sem)
                      for u, iv in enumerate(idx_vmem)]
            for cp in copies:
                cp.wait()

        pltpu.emit_pipeline(
            body,
            grid=(m // (n_fly * SC_WINDOW),),
            in_specs=[pl.BlockSpec((1, SC_WINDOW), lambda i, u=u: (0, n_fly * i + u))
                      for u in range(n_fly)],
            out_specs=[pl.BlockSpec((n_fly * SC_WINDOW, LANES), lambda i: (i, 0))],
            core_axis_name=("core", "subcore"),
            dimension_semantics=(pltpu.PARALLEL,),
        )(*([idx_hbm] * n_fly), out_hbm)

    return gather_kernel(table, idx.reshape(1, m))


def _sc_scatter(src, idx_lists, out_rows):
    m = src.shape[0]
    n_lists = len(idx_lists)

    @pl.kernel(out_type=jax.ShapeDtypeStruct((out_rows, LANES), src.dtype), mesh=_sc_mesh(),
               scratch_types=[pltpu.SemaphoreType.DMA])
    def scatter_kernel(src_hbm, *refs):
        idx_hbm, out_hbm, sem = refs[:n_lists], refs[n_lists], refs[n_lists + 1]

        def body(src_vmem, *idx_vmem):
            copies = [pltpu.async_copy(src_vmem, out_hbm.at[iv.at[0]], sem) for iv in idx_vmem]
            for cp in copies:
                cp.wait()

        pltpu.emit_pipeline(
            body,
            grid=(m // SC_WINDOW,),
            in_specs=[pl.BlockSpec((SC_WINDOW, LANES), lambda i: (i, 0))]
            + [pl.BlockSpec((1, SC_WINDOW), lambda i: (0, i))] * n_lists,
            out_specs=[],
            core_axis_name=("core", "subcore"),
            dimension_semantics=(pltpu.PARALLEL,),
        )(src_hbm, *idx_hbm)

    return scatter_kernel(src, *[ix.reshape(1, m) for ix in idx_lists])


def _layer(x, mem, w_in, fox_f_bias, conv_w, i_bias, f_bias, fox_g, mlstm_g, w_mix_out,
           ln1_g, ln1_b, w_xq, w_xk, w_xv, w_xo, ln2_g, ln2_b, w_router, b_router,
           w_gate_up, b_gate_up, w_down, b_down, ln3_g, ln3_b):
    batch, seq, d = x.shape
    n_mem = mem.shape[1]
    n = batch * seq
    x2d = x.reshape(n, d)

    o_ff = 3 * FOX_WIDTH
    o_mqk = o_ff + FOX_HEADS
    o_mv = o_mqk + 2 * MLSTM_QK_WIDTH
    o_mi = o_mv + MLSTM_V_WIDTH
    o_mf = o_mi + MLSTM_HEADS
    o_mo = o_mf + MLSTM_HEADS
    n_gate = FOX_HEADS + 2 * MLSTM_HEADS
    w_r = jnp.concatenate(
        [w_in[:, FOX_WIDTH:2 * FOX_WIDTH], w_in[:, o_mqk:o_mv],
         w_in[:, o_ff:o_mqk], w_in[:, o_mi:o_mo],
         jnp.zeros((d, LANES - n_gate), w_in.dtype)], axis=1).astype(BF16)
    w_t = jnp.concatenate([w_in[:, :FOX_WIDTH], w_in[:, 2 * FOX_WIDTH:o_ff],
                           w_in[:, o_mv:o_mi], w_in[:, o_mo:]], axis=1).T.astype(BF16)
    gate_bias = jnp.concatenate(
        [fox_f_bias, i_bias, f_bias, jnp.zeros((LANES - n_gate,), F32)]).reshape(1, LANES)

    fqt, fk, fvt, mqk, mvt, mot, gates = _inproj(x2d, w_r, w_t, gate_bias)
    gcol, grow, cpieces = _gateprep(gates, batch, seq)
    fo = _fox(fqt, fk, cpieces, fvt, fox_g.reshape(1, FOX_WIDTH), batch, seq)
    gain_lanes = jnp.broadcast_to(mlstm_g[:, None], (MLSTM_V_WIDTH, LANES))
    mo_out = _mlstm(mqk, mvt, mot, gcol, grow, conv_w, gain_lanes, batch, seq)
    x1 = _outproj(fo, mo_out, w_mix_out.astype(BF16), x2d, ln1_g.reshape(1, d), ln1_b.reshape(1, d))

    kmem, vmem = _memkv(mem.reshape(batch * n_mem, d), w_xk.astype(BF16), w_xv.astype(BF16), n_mem)
    wrt = w_router.T
    wrt_hi = wrt.astype(BF16)
    wrt_lo = (wrt - wrt_hi.astype(F32)).astype(BF16)
    x2, x2p, logits_t = _xattn(x1, kmem, vmem, w_xq.astype(BF16), w_xo.astype(BF16),
                               ln2_g.reshape(1, d), ln2_b.reshape(1, d),
                               jnp.concatenate([wrt_hi, wrt_lo], axis=0),
                               b_router.reshape(N_EXPERTS, 1), batch, seq, n_mem)

    idx_t, rank_t, gate, cnt = _route(logits_t)
    counts = cnt[:, 0]
    g_rows = EXPERT_ROWS
    padded = ((counts + g_rows - 1) // g_rows) * g_rows
    pad_end = jnp.cumsum(padded)
    pad_start = pad_end - padded
    experts = jnp.arange(N_EXPERTS, dtype=jnp.int32)
    sel = idx_t[:TOP_K, :, None] == experts[None, None, :]
    pos_t = jnp.sum(jnp.where(sel, pad_start[None, None, :], 0), axis=-1) + rank_t[:TOP_K]
    p_rows = n * TOP_K + N_EXPERTS * g_rows
    nb = p_rows // g_rows
    blk_start = jnp.arange(nb, dtype=jnp.int32) * g_rows
    blk_e = jnp.minimum(jnp.sum((pad_end[None, :] <= blk_start[:, None]).astype(jnp.int32), axis=1),
                        N_EXPERTS - 1)
    own = blk_e[:, None] == experts[None, :]
    row_end = jnp.sum(jnp.where(own, (pad_start + counts)[None, :], 0), axis=1)
    blk_rows = jnp.clip(row_end - blk_start, 0, g_rows).astype(jnp.int32)
    later = (experts[None, :] > blk_e[:, None]) & (counts[None, :] > 0)
    next_e = jnp.min(jnp.where(later, experts[None, :], N_EXPERTS), axis=1)
    next_e = jnp.where(next_e == N_EXPERTS, -1, next_e).astype(jnp.int32)

    def piece_index(planes):
        off = jnp.arange(planes, dtype=jnp.int32) * p_rows
        return pos_t[:, None, :] + off[None, :, None]

    x_idx = piece_index(X_PLANES).reshape(TOP_K, X_PLANES * n)
    xs = _sc_scatter(x2p.reshape(X_PLANES * n, LANES), [x_idx[k] for k in range(TOP_K)],
                     X_PLANES * p_rows)
    y = _experts(blk_e, blk_rows, next_e, xs.reshape(X_PLANES, p_rows, LANES), w_gate_up,
                 b_gate_up.reshape(N_EXPERTS, 1, -1), w_down, b_down.reshape(N_EXPERTS, 1, -1))
    yg = _sc_gather(y.reshape(Y_PLANES * p_rows, LANES), piece_index(Y_PLANES).reshape(-1))
    out = _combine(yg.reshape(TOP_K, Y_PLANES, n, LANES), gate, x2,
                   ln3_g.reshape(1, d), ln3_b.reshape(1, d))
    return out.reshape(batch, seq, d)


def kernel(x, mem, w_in, fox_f_bias, mlstm_conv_w, mlstm_i_bias, mlstm_f_bias, fox_norm_g, mlstm_norm_g, w_mix_out, ln1_g, ln1_b, w_xq, w_xk, w_xv, w_xo, ln2_g, ln2_b, w_router, b_router, w_gate_up, b_gate_up, w_down, b_down, ln3_g, ln3_b):
    for l in range(w_in.shape[0]):
        x = _layer(x, mem, w_in[l], fox_f_bias[l], mlstm_conv_w[l], mlstm_i_bias[l],
                   mlstm_f_bias[l], fox_norm_g[l], mlstm_norm_g[l], w_mix_out[l],
                   ln1_g[l], ln1_b[l], w_xq[l], w_xk[l], w_xv[l], w_xo[l], ln2_g[l], ln2_b[l],
                   w_router[l], b_router[l], w_gate_up[l], b_gate_up[l], w_down[l], b_down[l],
                   ln3_g[l], ln3_b[l])
    return x
```

```python
import jax
import jax.numpy as jnp
from jax import lax
from jax.experimental import pallas as pl
from jax.experimental.pallas import tpu as pltpu
from jax.experimental.pallas import tpu_sc as plsc

F32 = jnp.float32
BF16 = jnp.bfloat16

D_MODEL = 1024
FOX_HEADS = 8
FOX_HEAD_DIM = 64
FOX_WIDTH = FOX_HEADS * FOX_HEAD_DIM
MLSTM_HEADS = 4
MLSTM_QK_DIM = 64
MLSTM_V_DIM = 128
MLSTM_QK_WIDTH = MLSTM_HEADS * MLSTM_QK_DIM
MLSTM_V_WIDTH = MLSTM_HEADS * MLSTM_V_DIM
CONV_WIDTH = 4
XATTN_HEADS = 4
XATTN_HEAD_DIM = D_MODEL // XATTN_HEADS
N_EXPERTS = 32
TOP_K = 4
D_EXPERT = D_MODEL
SWIGLU_LIMIT = 7.0
SWIGLU_ALPHA = 1.702
DEEPNORM_ALPHA = 2.0 ** 0.25
LN_EPS = 1e-5
RMS_EPS = 1e-6

LANES = 128
SEQ_BLOCK = 256
ROW_TILE = 512
XATTN_TILE = 1024
EXPERT_ROWS = 512
EXPERT_CHUNK = 256
ROUTE_TILE = 512
X_PLANES = D_MODEL // 2 // LANES
Y_PLANES = X_PLANES
GATE_I0 = FOX_HEADS
GATE_F0 = FOX_HEADS + MLSTM_HEADS

MIB = 1024 * 1024


def _params(semantics, vmem_mib):
    return pltpu.CompilerParams(dimension_semantics=semantics,
                                vmem_limit_bytes=vmem_mib * MIB)


def _layer_norm(y, g, b):
    mu = jnp.mean(y, axis=-1, keepdims=True)
    yc = y - mu
    var = jnp.mean(yc * yc, axis=-1, keepdims=True)
    return yc * lax.rsqrt(var + LN_EPS) * g + b


def _dot(a, b):
    return jnp.dot(a, b, preferred_element_type=F32)


def _dot_nt(a, b):
    return lax.dot_general(a, b, (((1,), (1,)), ((), ())), preferred_element_type=F32)


def _dot_tn(a, b):
    return lax.dot_general(a, b, (((0,), (0,)), ((), ())), preferred_element_type=F32)


def _pack_bf16_pairs(x):
    w = x.shape[1] // 2
    lo = pltpu.bitcast(x[:, :w].astype(BF16).astype(F32), jnp.uint32)
    hi = pltpu.bitcast(x[:, w:].astype(BF16).astype(F32), jnp.uint32)
    return (lo >> 16) | hi


def _unpack_bf16_pairs(u, dtype=BF16):
    lo = pltpu.bitcast(u << 16, F32).astype(dtype)
    hi = pltpu.bitcast(u & jnp.uint32(0xFFFF0000), F32).astype(dtype)
    return jnp.concatenate([lo, hi], axis=1)


def _inproj_kernel(x_ref, w_ref, wt_ref, gb_ref, fqt_ref, fk_ref, fvt_ref, mqk_ref, mvt_ref,
                   mot_ref, g_ref):
    xb = x_ref[...].astype(BF16)

    def mm(c0, width):
        return _dot(xb, w_ref[:, c0:c0 + width])

    def mm_t(r0, height):
        return _dot_nt(wt_ref[r0:r0 + height, :], xb)

    r_fv = FOX_WIDTH
    r_mv = r_fv + FOX_WIDTH
    r_mo = r_mv + MLSTM_V_WIDTH
    fqt_ref[...] = (mm_t(0, FOX_WIDTH) * (FOX_HEAD_DIM ** -0.5)).astype(BF16)
    fvt_ref[...] = mm_t(r_fv, FOX_WIDTH).astype(BF16)
    mvt_ref[...] = mm_t(r_mv, MLSTM_V_WIDTH).astype(BF16)
    mot_ref[...] = mm_t(r_mo, MLSTM_V_WIDTH)
    c_qk = FOX_WIDTH
    c_g = c_qk + 2 * MLSTM_QK_WIDTH
    fk_ref[...] = mm(0, FOX_WIDTH).astype(BF16)
    mqk_ref[...] = mm(c_qk, 2 * MLSTM_QK_WIDTH)
    g_ref[...] = mm(c_g, LANES) + gb_ref[...]


def _inproj(x2d, w_r, w_t, gate_bias):
    n = x2d.shape[0]
    tm = ROW_TILE
    row = lambda width: pl.BlockSpec((tm, width), lambda i: (i, 0))
    col = lambda height: pl.BlockSpec((height, tm), lambda i: (0, i))
    full = lambda a: pl.BlockSpec(a.shape, lambda i: (0, 0))
    out_shapes = (
        jax.ShapeDtypeStruct((FOX_WIDTH, n), BF16),
        jax.ShapeDtypeStruct((n, FOX_WIDTH), BF16),
        jax.ShapeDtypeStruct((FOX_WIDTH, n), BF16),
        jax.ShapeDtypeStruct((n, 2 * MLSTM_QK_WIDTH), F32),
        jax.ShapeDtypeStruct((MLSTM_V_WIDTH, n), BF16),
        jax.ShapeDtypeStruct((MLSTM_V_WIDTH, n), F32),
        jax.ShapeDtypeStruct((n, LANES), F32),
    )
    return pl.pallas_call(
        _inproj_kernel,
        grid=(n // tm,),
        in_specs=[row(D_MODEL), full(w_r), full(w_t), full(gate_bias)],
        out_specs=(col(FOX_WIDTH), row(FOX_WIDTH), col(FOX_WIDTH), row(2 * MLSTM_QK_WIDTH),
                   col(MLSTM_V_WIDTH), col(MLSTM_V_WIDTH), row(LANES)),
        out_shape=out_shapes,
        compiler_params=_params(("parallel",), 48),
    )(x2d, w_r, w_t, gate_bias)


def _split3(x):
    hi = x.astype(BF16)
    r1 = x - hi.astype(F32)
    mid = r1.astype(BF16)
    lo = (r1 - mid.astype(F32)).astype(BF16)
    return hi, mid, lo


def _gateprep_kernel(g_ref, sel_ref, col_ref, row_ref, cp_ref, carry_ref):
    c = pl.program_id(1)

    @pl.when(c == 0)
    def _():
        carry_ref[...] = jnp.zeros_like(carry_ref)

    g = g_ref[...]
    lane = lax.broadcasted_iota(jnp.int32, g.shape, 1)
    is_i = (lane >= GATE_I0) & (lane < GATE_F0)
    logsig = jnp.minimum(g, 0.0) - jnp.log1p(jnp.exp(-jnp.abs(g)))
    blk = g.shape[0]
    r = lax.broadcasted_iota(jnp.int32, (blk, blk), 0)
    s = lax.broadcasted_iota(jnp.int32, (blk, blk), 1)
    tri = (s <= r).astype(BF16)
    cs = None
    for piece in _split3(logsig):
        term = _dot(tri, piece)
        cs = term if cs is None else cs + term
    carry = carry_ref[0:1, :]
    glob = cs + carry
    carry_ref[...] = jnp.broadcast_to(glob[blk - 1:blk, :], carry_ref.shape)
    out = jnp.where(lane < GATE_I0, glob, jnp.where(is_i, g, cs))
    col_ref[...] = out
    row_ref[0] = out.T[0:16, :]

    pieces = jnp.concatenate(_split3(-glob), axis=1)
    moved = _dot(pieces, sel_ref[...])
    for p in range(FOX_HEADS // 2):
        cp_ref[0, p] = moved[:, p * LANES:(p + 1) * LANES].astype(BF16)


def _piece_selector():
    src = jnp.arange(3 * LANES, dtype=jnp.int32)[:, None]
    dst = jnp.arange(4 * LANES, dtype=jnp.int32)[None, :]
    piece, head = src // LANES, src % LANES
    pair, lane = dst // LANES, dst % LANES
    hit = (lane < 6) & (lane % 3 == piece) & (head == 2 * pair + lane // 3)
    return hit.astype(BF16)


def _gateprep(gates, batch, seq):
    n = gates.shape[0]
    nc = seq // SEQ_BLOCK
    sel = _piece_selector()
    return pl.pallas_call(
        _gateprep_kernel,
        grid=(batch, nc),
        in_specs=[pl.BlockSpec((SEQ_BLOCK, LANES), lambda b, c: (b * nc + c, 0)),
                  pl.BlockSpec(sel.shape, lambda b, c: (0, 0))],
        out_specs=(pl.BlockSpec((SEQ_BLOCK, LANES), lambda b, c: (b * nc + c, 0)),
                   pl.BlockSpec((1, 16, SEQ_BLOCK), lambda b, c: (b, 0, c)),
                   pl.BlockSpec((1, FOX_HEADS // 2, SEQ_BLOCK, LANES), lambda b, c: (b, 0, c, 0))),
        out_shape=(jax.ShapeDtypeStruct((n, LANES), F32),
                   jax.ShapeDtypeStruct((batch, 16, seq), F32),
                   jax.ShapeDtypeStruct((batch, FOX_HEADS // 2, seq, LANES), BF16)),
        scratch_shapes=[pltpu.VMEM((8, LANES), F32)],
        compiler_params=_params(("parallel", "arbitrary"), 32),
    )(gates, sel)


FOX_ONES_ROWS = 16
FOX_VAUG_ROWS = FOX_HEAD_DIM + FOX_ONES_ROWS
FOX_QUERY_TILE = 2 * SEQ_BLOCK


def _fox_kernel(qt_ref, k_ref, cp_ref, vt_ref, gain_ref, o_ref,
                kaug_sc, vaug_sc, qaug_sc, sta_sc, stb_sc, stc_sc, m_sc, acc0_sc, acc1_sc):
    qi = pl.program_id(2)
    tq = o_ref.shape[0]
    tg = tq
    tk = SEQ_BLOCK
    hd = FOX_HEAD_DIM
    seq = k_ref.shape[0]
    nq = seq // tq
    par = qi % 2

    def load_queries(tile, slot):
        q0 = pl.multiple_of(tile * tq, tq)
        for h in range(2):
            qaug_sc[slot, h, 0:hd, :] = qt_ref[h * hd:(h + 1) * hd, pl.ds(q0, tq)]

    @pl.when(qi == 0)
    def _():
        ones3 = (lax.broadcasted_iota(jnp.int32, (FOX_ONES_ROWS, tq), 0) < 3).astype(BF16)
        for slot in range(2):
            for h in range(2):
                qaug_sc[slot, h, hd:hd + FOX_ONES_ROWS, :] = ones3
                qaug_sc[slot, h, hd + FOX_ONES_ROWS:, :] = jnp.zeros(
                    (LANES - hd - FOX_ONES_ROWS, tq), BF16)
        load_queries(0, 0)

    nxt = jnp.minimum(qi + 1, nq - 1)
    load_queries(nxt, 1 - par)

    @pl.when(qi == 0)
    def _():
        lane = lax.broadcasted_iota(jnp.int32, (tk, LANES), 1)

        def build(blk, carry):
            r0 = pl.multiple_of(blk * tk, tk)
            kp = k_ref[pl.ds(r0, tk), :].astype(F32)
            cp = cp_ref[0, 0, pl.ds(r0, tk), :].astype(F32)
            for h in range(2):
                kh = kp if h == 0 else pltpu.roll(kp, hd, axis=1)
                ch = pltpu.roll(cp, hd - 3 * h, axis=1)
                kaug = jnp.where(lane < hd, kh, jnp.where(lane < hd + 3, ch, 0.0))
                kaug_sc[h, pl.ds(r0, tk), :] = kaug.astype(BF16)
            return carry

        lax.fori_loop(0, seq // tk, build, 0)
        for h in range(2):
            vaug_sc[h, 0:hd, :] = vt_ref[h * hd:(h + 1) * hd, :]
            vaug_sc[h, hd:, :] = jnp.ones((FOX_ONES_ROWS, seq), BF16)

    def put_scores(kg, slot, masked=False, queries=None):
        k0 = pl.multiple_of(kg * tg, tg)
        queries = par if queries is None else queries
        for h in range(2):
            st = _dot(kaug_sc[h, pl.ds(k0, tg), :], qaug_sc[queries, h])
            if masked:
                kk = lax.broadcasted_iota(jnp.int32, (tg, tq), 0)
                tt = lax.broadcasted_iota(jnp.int32, (tg, tq), 1)
                st = jnp.where(kk <= tt, st, -jnp.inf)
            slot[h] = st

    m_sc[...] = jnp.full(m_sc.shape, -jnp.inf, F32)
    acc0_sc[...] = jnp.zeros(acc0_sc.shape, F32)
    acc1_sc[...] = jnp.zeros(acc1_sc.shape, F32)
    acc = (acc0_sc, acc1_sc)

    def absorb(kg, slot):
        k0 = pl.multiple_of(kg * tg, tg)
        half = tg // 2
        for h in range(2):
            st = slot[h]
            m_prev = m_sc[h]
            m_new = jnp.maximum(m_prev, jnp.max(st, axis=0, keepdims=True))
            pv = None
            for u in range(2):
                p = jnp.exp(st[u * half:(u + 1) * half] - m_new)
                part = _dot(vaug_sc[h, :, pl.ds(k0 + u * half, half)], p.astype(BF16))
                pv = part if pv is None else pv + part
            acc[h][...] = jnp.exp(m_prev - m_new) * acc[h][...] + pv
            m_sc[h] = m_new

    sa, sb, sc = sta_sc, stb_sc, stc_sc
    n_loop = jnp.maximum(qi - 1, 0) // 2

    def body(j, carry):
        put_scores(2 * j + 1, sb)
        absorb(2 * j, sa)
        put_scores(2 * j + 2, sa)
        absorb(2 * j + 1, sb)
        return carry

    lax.fori_loop(0, n_loop, body, 0)
    done = 2 * n_loop

    def put_next():
        put_scores(0, sa, queries=1 - par)

    @pl.when(qi == 0)
    def _():
        put_scores(qi, sc, masked=True)
        put_next()
        absorb(qi, sc)

    @pl.when((qi > 0) & (qi - done == 1))
    def _():
        put_scores(qi, sc, masked=True)
        absorb(done, sa)
        put_next()
        absorb(qi, sc)

    @pl.when((qi > 0) & (qi - done == 2))
    def _():
        put_scores(done + 1, sb)
        absorb(done, sa)
        put_scores(qi, sc, masked=True)
        absorb(done + 1, sb)
        put_next()
        absorb(qi, sc)

    a0 = acc0_sc[...]
    a1 = acc1_sc[...]
    ot = jnp.concatenate([a0[0:hd] / a0[hd:hd + 1], a1[0:hd] / a1[hd:hd + 1]], axis=0)
    o = ot.T
    lane = lax.broadcasted_iota(jnp.int32, (tq, LANES), 1)
    lo = lane < FOX_HEAD_DIM
    sq = o * o
    ss0 = jnp.sum(jnp.where(lo, sq, 0.0), axis=-1, keepdims=True)
    ss1 = jnp.sum(jnp.where(lo, 0.0, sq), axis=-1, keepdims=True)
    inv = jnp.where(lo, lax.rsqrt(ss0 / FOX_HEAD_DIM + RMS_EPS),
                    lax.rsqrt(ss1 / FOX_HEAD_DIM + RMS_EPS))
    o_ref[...] = (o * inv * gain_ref[...]).astype(o_ref.dtype)


def _fox(fqt, fk, cpieces, fvt, fox_gain, batch, seq):
    n = fk.shape[0]
    tq = FOX_QUERY_TILE
    nq = seq // tq
    npair = FOX_HEADS // 2
    return pl.pallas_call(
        _fox_kernel,
        grid=(batch, npair, nq),
        in_specs=[pl.BlockSpec((LANES, seq), lambda b, hp, qi: (hp, b)),
                  pl.BlockSpec((seq, LANES), lambda b, hp, qi: (b, hp)),
                  pl.BlockSpec((1, 1, seq, LANES), lambda b, hp, qi: (b, hp, 0, 0)),
                  pl.BlockSpec((LANES, seq), lambda b, hp, qi: (hp, b)),
                  pl.BlockSpec((1, LANES), lambda b, hp, qi: (0, hp))],
        out_specs=pl.BlockSpec((tq, LANES), lambda b, hp, qi: (b * nq + qi, hp)),
        out_shape=jax.ShapeDtypeStruct((n, FOX_WIDTH), BF16),
        scratch_shapes=[pltpu.VMEM((2, seq, LANES), BF16),
                        pltpu.VMEM((2, FOX_VAUG_ROWS, seq), BF16),
                        pltpu.VMEM((2, 2, LANES, tq), BF16),
                        pltpu.VMEM((2, tq, tq), F32),
                        pltpu.VMEM((2, tq, tq), F32),
                        pltpu.VMEM((2, tq, tq), F32),
                        pltpu.VMEM((2, 1, tq), F32),
                        pltpu.VMEM((FOX_VAUG_ROWS, tq), F32),
                        pltpu.VMEM((FOX_VAUG_ROWS, tq), F32)],
        compiler_params=_params(("parallel", "parallel", "arbitrary"), 48),
    )(fqt, fk, cpieces, fvt, fox_gain)


MLSTM_ONES_ROWS = 16


def _mlstm_kernel(qk_ref, vt_ref, ogt_ref, col_ref, row_ref, cw_ref, gain_ref, o_ref,
                  tail_ref, buf_ref, c_sc, m_sc):
    c = pl.program_id(1)
    L = SEQ_BLOCK

    @pl.when(c == 0)
    def _():
        tail_ref[...] = jnp.zeros_like(tail_ref)
        c_sc[...] = jnp.zeros_like(c_sc)
        m_sc[...] = jnp.zeros_like(m_sc)

    x = qk_ref[...]
    buf_ref[0:8, :] = tail_ref[...]
    buf_ref[8:8 + L, :] = x
    tail_ref[...] = x[L - 8:L, :]
    y = x * cw_ref[CONV_WIDTH - 1:CONV_WIDTH, :]
    for j in range(CONV_WIDTH - 1):
        shift = CONV_WIDTH - 1 - j
        y = y + buf_ref[8 - shift:8 - shift + L, :] * cw_ref[j:j + 1, :]
    y = y * jax.nn.sigmoid(y)
    kf = y[:, MLSTM_QK_WIDTH:] * (MLSTM_QK_DIM ** -0.5)
    n_pair = MLSTM_HEADS // 2
    qt = [y[:, p * LANES:(p + 1) * LANES].T for p in range(n_pair)]
    kb = [kf[:, p * LANES:(p + 1) * LANES].astype(BF16) for p in range(n_pair)]

    col = col_ref[...]
    row = row_ref[0]
    lane = lax.broadcasted_iota(jnp.int32, (L, LANES), 1)
    sub = lax.broadcasted_iota(jnp.int32, (LANES, L), 0)
    ss = lax.broadcasted_iota(jnp.int32, (L, L), 0)
    ll = lax.broadcasted_iota(jnp.int32, (L, L), 1)
    causal = ss <= ll
    ones_rows = jnp.ones((MLSTM_ONES_ROWS, L), BF16)

    for h in range(MLSTM_HEADS):
        pair, half = divmod(h, 2)
        head_sub = (sub < MLSTM_QK_DIM) if half == 0 else (sub >= MLSTM_QK_DIM)
        head_lane = (lane < MLSTM_QK_DIM) if half == 0 else (lane >= MLSTM_QK_DIM)
        qth = jnp.where(head_sub, qt[pair], 0.0).astype(BF16)
        rcol = col[:, GATE_I0 + h:GATE_I0 + h + 1] - col[:, GATE_F0 + h:GATE_F0 + h + 1]
        brow = row[GATE_F0 + h:GATE_F0 + h + 1, :]
        lirow = row[GATE_I0 + h:GATE_I0 + h + 1, :]
        g = brow[:, L - 1:L]
        m_prev = m_sc[h][0:1, 0:1]

        dt = jnp.where(causal, rcol + brow, -jnp.inf)
        inter_log = brow + m_prev
        m_t = jnp.maximum(inter_log, jnp.max(dt, axis=0, keepdims=True))
        w_inter = jnp.exp(inter_log - m_t)
        pt = jnp.exp(dt - m_t) * _dot(kb[pair], qth)
        vaug = jnp.concatenate([vt_ref[h * LANES:(h + 1) * LANES, :], ones_rows], axis=0)
        cstate = c_sc[h]
        tot = w_inter * _dot(cstate.astype(BF16), qth) + _dot(vaug, pt.astype(BF16))
        den = tot[MLSTM_V_DIM:MLSTM_V_DIM + 1]
        hout = tot[0:MLSTM_V_DIM] * (1.0 / jnp.maximum(jnp.abs(den), jnp.exp(-m_t)))

        a = g + (lirow - brow)
        m_loc = jnp.max(a, axis=1, keepdims=True)
        vw = (vaug.astype(F32) * jnp.exp(a - m_loc)).astype(BF16)
        kmask = jnp.where(head_lane, kb[pair], jnp.zeros_like(kb[pair]))
        kv = _dot(vw, kmask)
        m_new = jnp.maximum(g + m_prev, m_loc)
        c_sc[h] = jnp.exp(g + m_prev - m_new) * cstate + jnp.exp(m_loc - m_new) * kv
        m_sc[h] = jnp.broadcast_to(m_new, m_sc.shape[1:])

        ms = jnp.mean(hout * hout, axis=0, keepdims=True)
        gain = gain_ref[h * LANES:(h + 1) * LANES, :]
        hn = hout * lax.rsqrt(ms + RMS_EPS) * jnp.concatenate([gain] * (L // LANES), axis=1)
        gate = jax.nn.sigmoid(ogt_ref[h * LANES:(h + 1) * LANES, :])
        o_ref[:, h * LANES:(h + 1) * LANES] = (hn * gate).T.astype(o_ref.dtype)


def _mlstm(mqk, mvt, mot, gcol, grow, conv_w, gain_lanes, batch, seq):
    n = mqk.shape[0]
    L = SEQ_BLOCK
    nc = seq // L
    qk_width = 2 * MLSTM_QK_WIDTH
    row = lambda width: pl.BlockSpec((L, width), lambda b, c: (b * nc + c, 0))
    col = pl.BlockSpec((MLSTM_V_WIDTH, L), lambda b, c: (0, b * nc + c))
    full = lambda a: pl.BlockSpec(a.shape, lambda b, c: (0, 0))
    return pl.pallas_call(
        _mlstm_kernel,
        grid=(batch, nc),
        in_specs=[row(qk_width), col, col, row(LANES),
                  pl.BlockSpec((1, 16, L), lambda b, c: (b, 0, c)),
                  full(conv_w), full(gain_lanes)],
        out_specs=row(MLSTM_V_WIDTH),
        out_shape=jax.ShapeDtypeStruct((n, MLSTM_V_WIDTH), BF16),
        scratch_shapes=[pltpu.VMEM((8, qk_width), F32), pltpu.VMEM((8 + L, qk_width), F32),
                        pltpu.VMEM((MLSTM_HEADS, MLSTM_V_DIM + MLSTM_ONES_ROWS, LANES), F32),
                        pltpu.VMEM((MLSTM_HEADS, 8, LANES), F32)],
        compiler_params=_params(("parallel", "arbitrary"), 48),
    )(mqk, mvt, mot, gcol, grow, conv_w, gain_lanes)


def _outproj_kernel(fo_ref, mo_ref, w_ref, x_ref, g_ref, b_ref, o_ref):
    mix = _dot(fo_ref[...], w_ref[0:FOX_WIDTH, :]) + _dot(mo_ref[...], w_ref[FOX_WIDTH:, :])
    o_ref[...] = _layer_norm(DEEPNORM_ALPHA * x_ref[...] + mix, g_ref[...], b_ref[...])


def _outproj(fo, mo, w_out, x2d, g, b):
    n = x2d.shape[0]
    tm = ROW_TILE
    row = lambda width: pl.BlockSpec((tm, width), lambda i: (i, 0))
    full = lambda a: pl.BlockSpec(a.shape, lambda i: (0, 0))
    return pl.pallas_call(
        _outproj_kernel,
        grid=(n // tm,),
        in_specs=[row(512), row(512), full(w_out), row(D_MODEL), full(g), full(b)],
        out_specs=row(D_MODEL),
        out_shape=jax.ShapeDtypeStruct((n, D_MODEL), F32),
        compiler_params=_params(("parallel",), 48),
    )(fo, mo, w_out, x2d, g, b)


def _memkv_kernel(mem_ref, wk_ref, wv_ref, k_ref, v_ref):
    mb = mem_ref[...].astype(BF16)
    k_ref[...] = (_dot(mb, wk_ref[...]) * (XATTN_HEAD_DIM ** -0.5)).astype(BF16)
    v_ref[...] = _dot(mb, wv_ref[...]).astype(BF16)


def _memkv(mem2d, wk, wv, n_mem):
    n = mem2d.shape[0]
    row = pl.BlockSpec((n_mem, D_MODEL), lambda i: (i, 0))
    full = lambda a: pl.BlockSpec(a.shape, lambda i: (0, 0))
    return pl.pallas_call(
        _memkv_kernel,
        grid=(n // n_mem,),
        in_specs=[row, full(wk), full(wv)],
        out_specs=(row, row),
        out_shape=(jax.ShapeDtypeStruct((n, D_MODEL), BF16),) * 2,
        compiler_params=_params(("parallel",), 32),
    )(mem2d, wk, wv)


def _xattn_kernel(x_ref, k_ref, v_ref, wq_ref, wo_ref, g_ref, b_ref, wr_ref, br_ref,
                  o_ref, ob_ref, lg_ref):
    half = x_ref.shape[0] // 2
    for r in range(2):
        rows = slice(r * half, (r + 1) * half)
        x = x_ref[rows, :]
        q = _dot(x.astype(BF16), wq_ref[...]).astype(BF16)
        outs = []
        for h in range(XATTN_HEADS):
            sl = slice(h * XATTN_HEAD_DIM, (h + 1) * XATTN_HEAD_DIM)
            s = _dot_nt(q[:, sl], k_ref[:, sl])
            p = jnp.exp(s - jnp.max(s, axis=-1, keepdims=True))
            l = jnp.sum(p, axis=-1, keepdims=True)
            outs.append((_dot(p.astype(BF16), v_ref[:, sl]) / l).astype(BF16))
        o = jnp.concatenate(outs, axis=1)
        xa = _dot(o, wo_ref[...])
        x2 = _layer_norm(DEEPNORM_ALPHA * x + xa, g_ref[...], b_ref[...])
        o_ref[rows, :] = x2
        packed = _pack_bf16_pairs(x2)
        for j in range(ob_ref.shape[0]):
            ob_ref[j, rows, :] = packed[:, j * LANES:(j + 1) * LANES]
        x2h = x2.astype(BF16)
        x2l = (x2 - x2h.astype(F32)).astype(BF16)
        a = _dot_nt(wr_ref[...], x2h)
        b = _dot_nt(wr_ref[0:N_EXPERTS, :], x2l)
        lg_ref[:, rows] = a[0:N_EXPERTS] + a[N_EXPERTS:] + b + br_ref[...]


def _xattn(x1, kmem, vmem, wq, wo, g, b, wr, br, batch, seq, n_mem):
    n = x1.shape[0]
    tm = XATTN_TILE
    nt = seq // tm
    row = lambda width: pl.BlockSpec((tm, width), lambda bi, i: (bi * nt + i, 0))
    full = lambda a: pl.BlockSpec(a.shape, lambda bi, i: (0, 0))
    kv = pl.BlockSpec((n_mem, D_MODEL), lambda bi, i: (bi, 0))
    return pl.pallas_call(
        _xattn_kernel,
        grid=(batch, nt),
        in_specs=[row(D_MODEL), kv, kv, full(wq), full(wo), full(g), full(b), full(wr), full(br)],
        out_specs=(row(D_MODEL),
                   pl.BlockSpec((X_PLANES, tm, LANES), lambda bi, i: (0, bi * nt + i, 0)),
                   pl.BlockSpec((N_EXPERTS, tm), lambda bi, i: (0, bi * nt + i))),
        out_shape=(jax.ShapeDtypeStruct((n, D_MODEL), F32),
                   jax.ShapeDtypeStruct((X_PLANES, n, LANES), jnp.uint32),
                   jax.ShapeDtypeStruct((N_EXPERTS, n), F32)),
        compiler_params=_params(("parallel", "parallel"), 60),
    )(x1, kmem, vmem, wq, wo, g, b, wr, br)


def _route_kernel(lg_ref, idx_ref, rank_ref, gate_ref, cnt_ref, carry_ref):
    i = pl.program_id(0)

    @pl.when(i == 0)
    def _():
        carry_ref[...] = jnp.zeros_like(carry_ref)

    lg = lg_ref[...]
    t = lg.shape[1]
    e_idx = lax.broadcasted_iota(jnp.int32, lg.shape, 0).astype(F32)
    sels, vals, idxs = [], [], []
    for _ in range(TOP_K):
        mx = jnp.max(lg, axis=0, keepdims=True)
        first = jnp.min(jnp.where(lg == mx, e_idx, float(N_EXPERTS)), axis=0, keepdims=True)
        sel = e_idx == first
        sels.append(sel)
        vals.append(mx)
        idxs.append(first)
        lg = jnp.where(sel, -jnp.inf, lg)
    exps = [jnp.exp(v - vals[0]) for v in vals]
    tot = exps[0] + exps[1] + exps[2] + exps[3]

    selmat = (sels[0] | sels[1] | sels[2] | sels[3])
    r = lax.broadcasted_iota(jnp.int32, (t, t), 0)
    s = lax.broadcasted_iota(jnp.int32, (t, t), 1)
    earlier = (r < s).astype(BF16)
    carry = carry_ref[:, 0:1]
    rankmat = _dot(selmat.astype(BF16), earlier) + carry
    new_carry = carry + jnp.sum(selmat.astype(F32), axis=1, keepdims=True)
    carry_ref[...] = jnp.broadcast_to(new_carry, carry_ref.shape)
    cnt_ref[...] = jnp.broadcast_to(new_carry, cnt_ref.shape).astype(jnp.int32)

    row8 = lax.broadcasted_iota(jnp.int32, (8, t), 0)
    row128 = lax.broadcasted_iota(jnp.int32, (LANES, t), 0)
    idx_out = jnp.zeros((8, t), F32)
    rank_out = jnp.zeros((8, t), F32)
    gate_out = jnp.zeros((LANES, t), F32)
    for k in range(TOP_K):
        rk = jnp.sum(jnp.where(sels[k], rankmat, 0.0), axis=0, keepdims=True)
        idx_out = jnp.where(row8 == k, idxs[k], idx_out)
        rank_out = jnp.where(row8 == k, rk, rank_out)
        gate_out = jnp.where(row128 == k, exps[k] / tot, gate_out)
    idx_ref[...] = idx_out.astype(jnp.int32)
    rank_ref[...] = rank_out.astype(jnp.int32)
    gate_ref[...] = gate_out.T


def _route(logits_t):
    n = logits_t.shape[1]
    t = ROUTE_TILE
    col = lambda rows: pl.BlockSpec((rows, t), lambda i: (0, i))
    return pl.pallas_call(
        _route_kernel,
        grid=(n // t,),
        in_specs=[col(N_EXPERTS)],
        out_specs=(col(8), col(8), pl.BlockSpec((t, LANES), lambda i: (i, 0)),
                   pl.BlockSpec((N_EXPERTS, LANES), lambda i: (0, 0))),
        out_shape=(jax.ShapeDtypeStruct((8, n), jnp.int32),
                   jax.ShapeDtypeStruct((8, n), jnp.int32),
                   jax.ShapeDtypeStruct((n, LANES), F32),
                   jax.ShapeDtypeStruct((N_EXPERTS, LANES), jnp.int32)),
        scratch_shapes=[pltpu.VMEM((N_EXPERTS, LANES), F32)],
        compiler_params=_params(("arbitrary",), 32),
    )(logits_t)


def _expert_kernel(blk_e_ref, blk_rows_ref, next_e_ref, x_ref, wgu_hbm, bgu_ref, wd_hbm, bd_ref,
                   y_ref, wgu_f32, wd_f32, wgu_sc, wd_sc, state_ref, sem):
    i = pl.program_id(0)
    e = blk_e_ref[i]
    rows = blk_rows_ref[i]
    g = y_ref.shape[1]

    def weight_copies(expert, slot):
        return (pltpu.make_async_copy(wgu_hbm.at[expert], wgu_f32.at[slot], sem.at[0, slot]),
                pltpu.make_async_copy(wd_hbm.at[expert], wd_f32.at[slot], sem.at[1, slot]))

    @pl.when(i == 0)
    def _():
        state_ref[0] = -1
        state_ref[1] = 0

    @pl.when((rows > 0) & (e != state_ref[0]))
    def _():
        slot = state_ref[1]

        @pl.when(i == 0)
        def _():
            for cp in weight_copies(e, slot):
                cp.start()

        for cp in weight_copies(e, slot):
            cp.wait()
        wgu_sc[...] = wgu_f32[slot].astype(BF16)
        wd_sc[...] = wd_f32[slot].astype(BF16)
        nxt = next_e_ref[i]

        @pl.when(nxt >= 0)
        def _():
            for cp in weight_copies(nxt, 1 - slot):
                cp.start()

        state_ref[0] = e
        state_ref[1] = 1 - slot

    def ffn(m):
        packed = jnp.concatenate([x_ref[j, 0:m, :] for j in range(X_PLANES)], axis=1)
        xb = _unpack_bf16_pairs(packed)
        hids = []
        for c in range(D_EXPERT // EXPERT_CHUNK):
            g0 = c * EXPERT_CHUNK
            l0 = D_EXPERT + g0
            gate = _dot(xb, wgu_sc[:, g0:g0 + EXPERT_CHUNK]) + bgu_ref[0, :, g0:g0 + EXPERT_CHUNK]
            lin = _dot(xb, wgu_sc[:, l0:l0 + EXPERT_CHUNK]) + bgu_ref[0, :, l0:l0 + EXPERT_CHUNK]
            gate = jnp.minimum(gate, SWIGLU_LIMIT)
            lin = jnp.clip(lin, -SWIGLU_LIMIT, SWIGLU_LIMIT)
            hids.append((gate * jax.nn.sigmoid(SWIGLU_ALPHA * gate) * (lin + 1.0)).astype(BF16))
        y = _pack_bf16_pairs(_dot(jnp.concatenate(hids, axis=1), wd_sc[...]) + bd_ref[0])
        for j in range(Y_PLANES):
            y_ref[j, 0:m, :] = y[:, j * LANES:(j + 1) * LANES]

    @pl.when(rows > g // 2)
    def _():
        ffn(g)

    @pl.when((rows > 0) & (rows <= g // 2))
    def _():
        ffn(g // 2)
        y_ref[:, g // 2:, :] = jnp.zeros((Y_PLANES, g // 2, LANES), y_ref.dtype)

    @pl.when(rows == 0)
    def _():
        y_ref[...] = jnp.zeros_like(y_ref)


def _experts(blk_e, blk_rows, next_e, xs, w_gu, b_gu, w_d, b_d):
    p = xs.shape[1]
    g = EXPERT_ROWS
    grid_spec = pltpu.PrefetchScalarGridSpec(
        num_scalar_prefetch=3,
        grid=(p // g,),
        in_specs=[pl.BlockSpec((X_PLANES, g, LANES), lambda i, be, br, ne: (0, i, 0)),
                  pl.BlockSpec(memory_space=pl.ANY),
                  pl.BlockSpec((1, 1, 2 * D_EXPERT), lambda i, be, br, ne: (be[i], 0, 0)),
                  pl.BlockSpec(memory_space=pl.ANY),
                  pl.BlockSpec((1, 1, D_MODEL), lambda i, be, br, ne: (be[i], 0, 0))],
        out_specs=pl.BlockSpec((Y_PLANES, g, LANES), lambda i, be, br, ne: (0, i, 0)),
        scratch_shapes=[pltpu.VMEM((2, D_MODEL, 2 * D_EXPERT), F32),
                        pltpu.VMEM((2, D_EXPERT, D_MODEL), F32),
                        pltpu.VMEM((D_MODEL, 2 * D_EXPERT), BF16),
                        pltpu.VMEM((D_EXPERT, D_MODEL), BF16),
                        pltpu.SMEM((2,), jnp.int32),
                        pltpu.SemaphoreType.DMA((2, 2))],
    )
    return pl.pallas_call(
        _expert_kernel,
        grid_spec=grid_spec,
        out_shape=jax.ShapeDtypeStruct((Y_PLANES, p, LANES), jnp.uint32),
        compiler_params=_params(("arbitrary",), 56),
    )(blk_e, blk_rows, next_e, xs, w_gu, b_gu, w_d, b_d)


def _combine_kernel(y_ref, gate_ref, x_ref, g_ref, b_ref, o_ref):
    gate = gate_ref[...]
    ff = None
    for k in range(TOP_K):
        packed = jnp.concatenate([y_ref[k, j] for j in range(Y_PLANES)], axis=1)
        yk = _unpack_bf16_pairs(packed, F32) * gate[:, k:k + 1]
        ff = yk if ff is None else ff + yk
    o_ref[...] = _layer_norm(DEEPNORM_ALPHA * x_ref[...] + ff, g_ref[...], b_ref[...])


def _combine(yg, gate, x2, g, b):
    n = x2.shape[0]
    tm = SEQ_BLOCK
    row = lambda width: pl.BlockSpec((tm, width), lambda i: (i, 0))
    full = lambda a: pl.BlockSpec(a.shape, lambda i: (0, 0))
    return pl.pallas_call(
        _combine_kernel,
        grid=(n // tm,),
        in_specs=[pl.BlockSpec((TOP_K, Y_PLANES, tm, LANES), lambda i: (0, 0, i, 0)),
                  row(LANES), row(D_MODEL), full(g), full(b)],
        out_specs=row(D_MODEL),
        out_shape=jax.ShapeDtypeStruct((n, D_MODEL), F32),
        compiler_params=_params(("parallel",), 32),
    )(yg, gate, x2, g, b)


SC_WINDOW = 128
SC_GATHERS_IN_FLIGHT = 2


def _sc_mesh():
    return plsc.VectorSubcoreMesh(core_axis_name="core", subcore_axis_name="subcore")


def _sc_gather(table, idx):
    m = idx.shape[0]
    n_fly = SC_GATHERS_IN_FLIGHT

    @pl.kernel(out_type=jax.ShapeDtypeStruct((m, LANES), table.dtype), mesh=_sc_mesh(),
               scratch_types=[pltpu.SemaphoreType.DMA])
    def gather_kernel(table_hbm, idx_hbm, out_hbm, sem):
        def body(*refs):
            idx_vmem, out_vmem = refs[:n_fly], refs[n_fly]
            copies = [pltpu.async_copy(table_hbm.at[iv.at[0]],
                                       out_vmem.at[pl.ds(u * SC_WINDOW, SC_WINDOW)], sem)
                      for u, iv in enumerate(idx_vmem)]
            for cp in copies:
                cp.wait()

        pltpu.emit_pipeline(
            body,
            grid=(m // (n_fly * SC_WINDOW),),
            in_specs=[pl.BlockSpec((1, SC_WINDOW), lambda i, u=u: (0, n_fly * i + u))
                      for u in range(n_fly)],
            out_specs=[pl.BlockSpec((n_fly * SC_WINDOW, LANES), lambda i: (i, 0))],
            core_axis_name=("core", "subcore"),
            dimension_semantics=(pltpu.PARALLEL,),
        )(*([idx_hbm] * n_fly), out_hbm)

    return gather_kernel(table, idx.reshape(1, m))


def _sc_scatter(src, idx_lists, out_rows):
    m = src.shape[0]
    n_lists = len(idx_lists)

    @pl.kernel(out_type=jax.ShapeDtypeStruct((out_rows, LANES), src.dtype), mesh=_sc_mesh(),
               scratch_types=[pltpu.SemaphoreType.DMA])
    def scatter_kernel(src_hbm, *refs):
        idx_hbm, out_hbm, sem = refs[:n_lists], refs[n_lists], refs[n_lists + 1]

        def body(src_vmem, *idx_vmem):
            copies = [pltpu.async_copy(src_vmem, out_hbm.at[iv.at[0]], sem) for iv in idx_vmem]
            for cp in copies:
                cp.wait()

        pltpu.emit_pipeline(
            body,
            grid=(m // SC_WINDOW,),
            in_specs=[pl.BlockSpec((SC_WINDOW, LANES), lambda i: (i, 0))]
            + [pl.BlockSpec((1, SC_WINDOW), lambda i: (0, i))] * n_lists,
            out_specs=[],
            core_axis_name=("core", "subcore"),
            dimension_semantics=(pltpu.PARALLEL,),
        )(src_hbm, *idx_hbm)

    return scatter_kernel(src, *[ix.reshape(1, m) for ix in idx_lists])


def _layer(x, mem, w_in, fox_f_bias, conv_w, i_bias, f_bias, fox_g, mlstm_g, w_mix_out,
           ln1_g, ln1_b, w_xq, w_xk, w_xv, w_xo, ln2_g, ln2_b, w_router, b_router,
           w_gate_up, b_gate_up, w_down, b_down, ln3_g, ln3_b):
    batch, seq, d = x.shape
    n_mem = mem.shape[1]
    n = batch * seq
    x2d = x.reshape(n, d)

    o_ff = 3 * FOX_WIDTH
    o_mqk = o_ff + FOX_HEADS
    o_mv = o_mqk + 2 * MLSTM_QK_WIDTH
    o_mi = o_mv + MLSTM_V_WIDTH
    o_mf = o_mi + MLSTM_HEADS
    o_mo = o_mf + MLSTM_HEADS
    n_gate = FOX_HEADS + 2 * MLSTM_HEADS
    w_r = jnp.concatenate(
        [w_in[:, FOX_WIDTH:2 * FOX_WIDTH], w_in[:, o_mqk:o_mv],
         w_in[:, o_ff:o_mqk], w_in[:, o_mi:o_mo],
         jnp.zeros((d, LANES - n_gate), w_in.dtype)], axis=1).astype(BF16)
    w_t = jnp.concatenate([w_in[:, :FOX_WIDTH], w_in[:, 2 * FOX_WIDTH:o_ff],
                           w_in[:, o_mv:o_mi], w_in[:, o_mo:]], axis=1).T.astype(BF16)
    gate_bias = jnp.concatenate(
        [fox_f_bias, i_bias, f_bias, jnp.zeros((LANES - n_gate,), F32)]).reshape(1, LANES)

    fqt, fk, fvt, mqk, mvt, mot, gates = _inproj(x2d, w_r, w_t, gate_bias)
    gcol, grow, cpieces = _gateprep(gates, batch, seq)
    fo = _fox(fqt, fk, cpieces, fvt, fox_g.reshape(1, FOX_WIDTH), batch, seq)
    gain_lanes = jnp.broadcast_to(mlstm_g[:, None], (MLSTM_V_WIDTH, LANES))
    mo_out = _mlstm(mqk, mvt, mot, gcol, grow, conv_w, gain_lanes, batch, seq)
    x1 = _outproj(fo, mo_out, w_mix_out.astype(BF16), x2d, ln1_g.reshape(1, d), ln1_b.reshape(1, d))

    kmem, vmem = _memkv(mem.reshape(batch * n_mem, d), w_xk.astype(BF16), w_xv.astype(BF16), n_mem)
    wrt = w_router.T
    wrt_hi = wrt.astype(BF16)
    wrt_lo = (wrt - wrt_hi.astype(F32)).astype(BF16)
    x2, x2p, logits_t = _xattn(x1, kmem, vmem, w_xq.astype(BF16), w_xo.astype(BF16),
                               ln2_g.reshape(1, d), ln2_b.reshape(1, d),
                               jnp.concatenate([wrt_hi, wrt_lo], axis=0),
                               b_router.reshape(N_EXPERTS, 1), batch, seq, n_mem)

    idx_t, rank_t, gate, cnt = _route(logits_t)
    counts = cnt[:, 0]
    g_rows = EXPERT_ROWS
    padded = ((counts + g_rows - 1) // g_rows) * g_rows
    pad_end = jnp.cumsum(padded)
    pad_start = pad_end - padded
    experts = jnp.arange(N_EXPERTS, dtype=jnp.int32)
    sel = idx_t[:TOP_K, :, None] == experts[None, None, :]
    pos_t = jnp.sum(jnp.where(sel, pad_start[None, None, :], 0), axis=-1) + rank_t[:TOP_K]
    p_rows = n * TOP_K + N_EXPERTS * g_rows
    nb = p_rows // g_rows
    blk_start = jnp.arange(nb, dtype=jnp.int32) * g_rows
    blk_e = jnp.minimum(jnp.sum((pad_end[None, :] <= blk_start[:, None]).astype(jnp.int32), axis=1),
                        N_EXPERTS - 1)
    own = blk_e[:, None] == experts[None, :]
    row_end = jnp.sum(jnp.where(own, (pad_start + counts)[None, :], 0), axis=1)
    blk_rows = jnp.clip(row_end - blk_start, 0, g_rows).astype(jnp.int32)
    later = (experts[None, :] > blk_e[:, None]) & (counts[None, :] > 0)
    next_e = jnp.min(jnp.where(later, experts[None, :], N_EXPERTS), axis=1)
    next_e = jnp.where(next_e == N_EXPERTS, -1, next_e).astype(jnp.int32)

    def piece_index(planes):
        off = jnp.arange(planes, dtype=jnp.int32) * p_rows
        return pos_t[:, None, :] + off[None, :, None]

    x_idx = piece_index(X_PLANES).reshape(TOP_K, X_PLANES * n)
    xs = _sc_scatter(x2p.reshape(X_PLANES * n, LANES), [x_idx[k] for k in range(TOP_K)],
                     X_PLANES * p_rows)
    y = _experts(blk_e, blk_rows, next_e, xs.reshape(X_PLANES, p_rows, LANES), w_gate_up,
                 b_gate_up.reshape(N_EXPERTS, 1, -1), w_down, b_down.reshape(N_EXPERTS, 1, -1))
    yg = _sc_gather(y.reshape(Y_PLANES * p_rows, LANES), piece_index(Y_PLANES).reshape(-1))
    out = _combine(yg.reshape(TOP_K, Y_PLANES, n, LANES), gate, x2,
                   ln3_g.reshape(1, d), ln3_b.reshape(1, d))
    return out.reshape(batch, seq, d)


def kernel(x, mem, w_in, fox_f_bias, mlstm_conv_w, mlstm_i_bias, mlstm_f_bias, fox_norm_g, mlstm_norm_g, w_mix_out, ln1_g, ln1_b, w_xq, w_xk, w_xv, w_xo, ln2_g, ln2_b, w_router, b_router, w_gate_up, b_gate_up, w_down, b_down, ln3_g, ln3_b):
    for l in range(w_in.shape[0]):
        x = _layer(x, mem, w_in[l], fox_f_bias[l], mlstm_conv_w[l], mlstm_i_bias[l],
                   mlstm_f_bias[l], fox_norm_g[l], mlstm_norm_g[l], w_mix_out[l],
                   ln1_g[l], ln1_b[l], w_xq[l], w_xk[l], w_xv[l], w_xo[l], ln2_g[l], ln2_b[l],
                   w_router[l], b_router[l], w_gate_up[l], b_gate_up[l], w_down[l], b_down[l],
                   ln3_g[l], ln3_b[l])
    return x
```

```python
import jax
import jax.numpy as jnp
from jax import lax
from jax.experimental import pallas as pl
from jax.experimental.pallas import tpu as pltpu
from jax.experimental.pallas import tpu_sc as plsc

F32 = jnp.float32
BF16 = jnp.bfloat16

D_MODEL = 1024
FOX_HEADS = 8
FOX_HEAD_DIM = 64
FOX_WIDTH = FOX_HEADS * FOX_HEAD_DIM
MLSTM_HEADS = 4
MLSTM_QK_DIM = 64
MLSTM_V_DIM = 128
MLSTM_QK_WIDTH = MLSTM_HEADS * MLSTM_QK_DIM
MLSTM_V_WIDTH = MLSTM_HEADS * MLSTM_V_DIM
CONV_WIDTH = 4
XATTN_HEADS = 4
XATTN_HEAD_DIM = D_MODEL // XATTN_HEADS
N_EXPERTS = 32
TOP_K = 4
D_EXPERT = D_MODEL
SWIGLU_LIMIT = 7.0
SWIGLU_ALPHA = 1.702
DEEPNORM_ALPHA = 2.0 ** 0.25
LN_EPS = 1e-5
RMS_EPS = 1e-6

LANES = 128
SEQ_BLOCK = 256
ROW_TILE = 512
WIDE_ROW_TILE = 1024
EXPERT_ROWS = 512
EXPERT_CHUNK = 256
ROUTE_TILE = 512
X_PLANES = D_MODEL // 2 // LANES
Y_PLANES = X_PLANES
GATE_I0 = FOX_HEADS
GATE_F0 = FOX_HEADS + MLSTM_HEADS

MIB = 1024 * 1024


def _params(semantics, vmem_mib):
    return pltpu.CompilerParams(dimension_semantics=semantics,
                                vmem_limit_bytes=vmem_mib * MIB)


def _layer_norm(y, g, b):
    mu = jnp.mean(y, axis=-1, keepdims=True)
    yc = y - mu
    var = jnp.mean(yc * yc, axis=-1, keepdims=True)
    return yc * lax.rsqrt(var + LN_EPS) * g + b


def _dot(a, b):
    return jnp.dot(a, b, preferred_element_type=F32)


def _dot_nt(a, b):
    return lax.dot_general(a, b, (((1,), (1,)), ((), ())), preferred_element_type=F32)


def _dot_tn(a, b):
    return lax.dot_general(a, b, (((0,), (0,)), ((), ())), preferred_element_type=F32)


def _pack_bf16_pairs(x):
    w = x.shape[1] // 2
    lo = pltpu.bitcast(x[:, :w].astype(BF16).astype(F32), jnp.uint32)
    hi = pltpu.bitcast(x[:, w:].astype(BF16).astype(F32), jnp.uint32)
    return (lo >> 16) | hi


def _unpack_bf16_pairs(u, dtype=BF16):
    lo = pltpu.bitcast(u << 16, F32).astype(dtype)
    hi = pltpu.bitcast(u & jnp.uint32(0xFFFF0000), F32).astype(dtype)
    return jnp.concatenate([lo, hi], axis=1)


def _inproj_kernel(x_ref, w_ref, wt_ref, gb_ref, fqt_ref, fk_ref, fvt_ref, mqk_ref, mvt_ref,
                   mot_ref, g_ref):
    xb = x_ref[...].astype(BF16)

    def mm(c0, width):
        return _dot(xb, w_ref[:, c0:c0 + width])

    def mm_t(r0, height):
        return _dot_nt(wt_ref[r0:r0 + height, :], xb)

    r_fv = FOX_WIDTH
    r_mv = r_fv + FOX_WIDTH
    r_mo = r_mv + MLSTM_V_WIDTH
    fqt_ref[...] = (mm_t(0, FOX_WIDTH) * (FOX_HEAD_DIM ** -0.5)).astype(BF16)
    fvt_ref[...] = mm_t(r_fv, FOX_WIDTH).astype(BF16)
    mvt_ref[...] = mm_t(r_mv, MLSTM_V_WIDTH).astype(BF16)
    mot_ref[...] = mm_t(r_mo, MLSTM_V_WIDTH)
    c_qk = FOX_WIDTH
    c_g = c_qk + 2 * MLSTM_QK_WIDTH
    fk_ref[...] = mm(0, FOX_WIDTH).astype(BF16)
    mqk_ref[...] = mm(c_qk, 2 * MLSTM_QK_WIDTH)
    g_ref[...] = mm(c_g, LANES) + gb_ref[...]


def _inproj(x2d, w_r, w_t, gate_bias):
    n = x2d.shape[0]
    tm = ROW_TILE
    row = lambda width: pl.BlockSpec((tm, width), lambda i: (i, 0))
    col = lambda height: pl.BlockSpec((height, tm), lambda i: (0, i))
    full = lambda a: pl.BlockSpec(a.shape, lambda i: (0, 0))
    out_shapes = (
        jax.ShapeDtypeStruct((FOX_WIDTH, n), BF16),
        jax.ShapeDtypeStruct((n, FOX_WIDTH), BF16),
        jax.ShapeDtypeStruct((FOX_WIDTH, n), BF16),
        jax.ShapeDtypeStruct((n, 2 * MLSTM_QK_WIDTH), F32),
        jax.ShapeDtypeStruct((MLSTM_V_WIDTH, n), BF16),
        jax.ShapeDtypeStruct((MLSTM_V_WIDTH, n), F32),
        jax.ShapeDtypeStruct((n, LANES), F32),
    )
    return pl.pallas_call(
        _inproj_kernel,
        grid=(n // tm,),
        in_specs=[row(D_MODEL), full(w_r), full(w_t), full(gate_bias)],
        out_specs=(col(FOX_WIDTH), row(FOX_WIDTH), col(FOX_WIDTH), row(2 * MLSTM_QK_WIDTH),
                   col(MLSTM_V_WIDTH), col(MLSTM_V_WIDTH), row(LANES)),
        out_shape=out_shapes,
        compiler_params=_params(("parallel",), 48),
    )(x2d, w_r, w_t, gate_bias)


def _split3(x):
    hi = x.astype(BF16)
    r1 = x - hi.astype(F32)
    mid = r1.astype(BF16)
    lo = (r1 - mid.astype(F32)).astype(BF16)
    return hi, mid, lo


def _gateprep_kernel(g_ref, sel_ref, col_ref, row_ref, cp_ref, carry_ref):
    c = pl.program_id(1)

    @pl.when(c == 0)
    def _():
        carry_ref[...] = jnp.zeros_like(carry_ref)

    g = g_ref[...]
    lane = lax.broadcasted_iota(jnp.int32, g.shape, 1)
    is_i = (lane >= GATE_I0) & (lane < GATE_F0)
    logsig = jnp.minimum(g, 0.0) - jnp.log1p(jnp.exp(-jnp.abs(g)))
    blk = g.shape[0]
    r = lax.broadcasted_iota(jnp.int32, (blk, blk), 0)
    s = lax.broadcasted_iota(jnp.int32, (blk, blk), 1)
    tri = (s <= r).astype(BF16)
    cs = None
    for piece in _split3(logsig):
        term = _dot(tri, piece)
        cs = term if cs is None else cs + term
    carry = carry_ref[0:1, :]
    glob = cs + carry
    carry_ref[...] = jnp.broadcast_to(glob[blk - 1:blk, :], carry_ref.shape)
    out = jnp.where(lane < GATE_I0, glob, jnp.where(is_i, g, cs))
    col_ref[...] = out
    row_ref[0] = out.T[0:16, :]

    pieces = jnp.concatenate(_split3(-glob), axis=1)
    moved = _dot(pieces, sel_ref[...])
    for p in range(FOX_HEADS // 2):
        cp_ref[0, p] = moved[:, p * LANES:(p + 1) * LANES].astype(BF16)


def _piece_selector():
    src = jnp.arange(3 * LANES, dtype=jnp.int32)[:, None]
    dst = jnp.arange(4 * LANES, dtype=jnp.int32)[None, :]
    piece, head = src // LANES, src % LANES
    pair, lane = dst // LANES, dst % LANES
    hit = (lane < 6) & (lane % 3 == piece) & (head == 2 * pair + lane // 3)
    return hit.astype(BF16)


def _gateprep(gates, batch, seq):
    n = gates.shape[0]
    nc = seq // SEQ_BLOCK
    sel = _piece_selector()
    return pl.pallas_call(
        _gateprep_kernel,
        grid=(batch, nc),
        in_specs=[pl.BlockSpec((SEQ_BLOCK, LANES), lambda b, c: (b * nc + c, 0)),
                  pl.BlockSpec(sel.shape, lambda b, c: (0, 0))],
        out_specs=(pl.BlockSpec((SEQ_BLOCK, LANES), lambda b, c: (b * nc + c, 0)),
                   pl.BlockSpec((1, 16, SEQ_BLOCK), lambda b, c: (b, 0, c)),
                   pl.BlockSpec((1, FOX_HEADS // 2, SEQ_BLOCK, LANES), lambda b, c: (b, 0, c, 0))),
        out_shape=(jax.ShapeDtypeStruct((n, LANES), F32),
                   jax.ShapeDtypeStruct((batch, 16, seq), F32),
                   jax.ShapeDtypeStruct((batch, FOX_HEADS // 2, seq, LANES), BF16)),
        scratch_shapes=[pltpu.VMEM((8, LANES), F32)],
        compiler_params=_params(("parallel", "arbitrary"), 32),
    )(gates, sel)


FOX_ONES_ROWS = 16
FOX_VAUG_ROWS = FOX_HEAD_DIM + FOX_ONES_ROWS
FOX_QUERY_TILE = 2 * SEQ_BLOCK


def _fox_kernel(qt_ref, k_ref, cp_ref, vt_ref, gain_ref, o_ref,
                kaug_sc, vaug_sc, qaug_sc, sta_sc, stb_sc, stc_sc, m_sc, acc0_sc, acc1_sc):
    qi = pl.program_id(2)
    tq = o_ref.shape[0]
    tg = tq
    tk = SEQ_BLOCK
    hd = FOX_HEAD_DIM
    seq = k_ref.shape[0]
    nq = seq // tq
    par = qi % 2

    def load_queries(tile, slot):
        q0 = pl.multiple_of(tile * tq, tq)
        for h in range(2):
            qaug_sc[slot, h, 0:hd, :] = qt_ref[h * hd:(h + 1) * hd, pl.ds(q0, tq)]

    @pl.when(qi == 0)
    def _():
        ones3 = (lax.broadcasted_iota(jnp.int32, (FOX_ONES_ROWS, tq), 0) < 3).astype(BF16)
        for slot in range(2):
            for h in range(2):
                qaug_sc[slot, h, hd:hd + FOX_ONES_ROWS, :] = ones3
                qaug_sc[slot, h, hd + FOX_ONES_ROWS:, :] = jnp.zeros(
                    (LANES - hd - FOX_ONES_ROWS, tq), BF16)
        load_queries(0, 0)

    nxt = jnp.minimum(qi + 1, nq - 1)
    load_queries(nxt, 1 - par)

    @pl.when(qi == 0)
    def _():
        lane = lax.broadcasted_iota(jnp.int32, (tk, LANES), 1)

        def build(blk, carry):
            r0 = pl.multiple_of(blk * tk, tk)
            kp = k_ref[pl.ds(r0, tk), :].astype(F32)
            cp = cp_ref[0, 0, pl.ds(r0, tk), :].astype(F32)
            for h in range(2):
                kh = kp if h == 0 else pltpu.roll(kp, hd, axis=1)
                ch = pltpu.roll(cp, hd - 3 * h, axis=1)
                kaug = jnp.where(lane < hd, kh, jnp.where(lane < hd + 3, ch, 0.0))
                kaug_sc[h, pl.ds(r0, tk), :] = kaug.astype(BF16)
            return carry

        lax.fori_loop(0, seq // tk, build, 0)
        for h in range(2):
            vaug_sc[h, 0:hd, :] = vt_ref[h * hd:(h + 1) * hd, :]
            vaug_sc[h, hd:, :] = jnp.ones((FOX_ONES_ROWS, seq), BF16)

    def put_scores(kg, slot, masked=False, queries=None):
        k0 = pl.multiple_of(kg * tg, tg)
        queries = par if queries is None else queries
        for h in range(2):
            st = _dot(kaug_sc[h, pl.ds(k0, tg), :], qaug_sc[queries, h])
            if masked:
                kk = lax.broadcasted_iota(jnp.int32, (tg, tq), 0)
                tt = lax.broadcasted_iota(jnp.int32, (tg, tq), 1)
                st = jnp.where(kk <= tt, st, -jnp.inf)
            slot[h] = st

    m_sc[...] = jnp.full(m_sc.shape, -jnp.inf, F32)
    acc0_sc[...] = jnp.zeros(acc0_sc.shape, F32)
    acc1_sc[...] = jnp.zeros(acc1_sc.shape, F32)
    acc = (acc0_sc, acc1_sc)

    def absorb(kg, slot):
        k0 = pl.multiple_of(kg * tg, tg)
        half = tg // 2
        for h in range(2):
            st = slot[h]
            m_prev = m_sc[h]
            m_new = jnp.maximum(m_prev, jnp.max(st, axis=0, keepdims=True))
            pv = None
            for u in range(2):
                p = jnp.exp(st[u * half:(u + 1) * half] - m_new)
                part = _dot(vaug_sc[h, :, pl.ds(k0 + u * half, half)], p.astype(BF16))
                pv = part if pv is None else pv + part
            acc[h][...] = jnp.exp(m_prev - m_new) * acc[h][...] + pv
            m_sc[h] = m_new

    sa, sb, sc = sta_sc, stb_sc, stc_sc
    n_loop = jnp.maximum(qi - 1, 0) // 2

    def body(j, carry):
        put_scores(2 * j + 1, sb)
        absorb(2 * j, sa)
        put_scores(2 * j + 2, sa)
        absorb(2 * j + 1, sb)
        return carry

    lax.fori_loop(0, n_loop, body, 0)
    done = 2 * n_loop

    def put_next():
        put_scores(0, sa, queries=1 - par)

    @pl.when(qi == 0)
    def _():
        put_scores(qi, sc, masked=True)
        put_next()
        absorb(qi, sc)

    @pl.when((qi > 0) & (qi - done == 1))
    def _():
        put_scores(qi, sc, masked=True)
        absorb(done, sa)
        put_next()
        absorb(qi, sc)

    @pl.when((qi > 0) & (qi - done == 2))
    def _():
        put_scores(done + 1, sb)
        absorb(done, sa)
        put_scores(qi, sc, masked=True)
        absorb(done + 1, sb)
        put_next()
        absorb(qi, sc)

    a0 = acc0_sc[...]
    a1 = acc1_sc[...]
    ot = jnp.concatenate([a0[0:hd] / a0[hd:hd + 1], a1[0:hd] / a1[hd:hd + 1]], axis=0)
    o = ot.T
    lane = lax.broadcasted_iota(jnp.int32, (tq, LANES), 1)
    lo = lane < FOX_HEAD_DIM
    sq = o * o
    ss0 = jnp.sum(jnp.where(lo, sq, 0.0), axis=-1, keepdims=True)
    ss1 = jnp.sum(jnp.where(lo, 0.0, sq), axis=-1, keepdims=True)
    inv = jnp.where(lo, lax.rsqrt(ss0 / FOX_HEAD_DIM + RMS_EPS),
                    lax.rsqrt(ss1 / FOX_HEAD_DIM + RMS_EPS))
    o_ref[...] = (o * inv * gain_ref[...]).astype(o_ref.dtype)


def _fox(fqt, fk, cpieces, fvt, fox_gain, batch, seq):
    n = fk.shape[0]
    tq = FOX_QUERY_TILE
    nq = seq // tq
    npair = FOX_HEADS // 2
    return pl.pallas_call(
        _fox_kernel,
        grid=(batch, npair, nq),
        in_specs=[pl.BlockSpec((LANES, seq), lambda b, hp, qi: (hp, b)),
                  pl.BlockSpec((seq, LANES), lambda b, hp, qi: (b, hp)),
                  pl.BlockSpec((1, 1, seq, LANES), lambda b, hp, qi: (b, hp, 0, 0)),
                  pl.BlockSpec((LANES, seq), lambda b, hp, qi: (hp, b)),
                  pl.BlockSpec((1, LANES), lambda b, hp, qi: (0, hp))],
        out_specs=pl.BlockSpec((tq, LANES), lambda b, hp, qi: (b * nq + qi, hp)),
        out_shape=jax.ShapeDtypeStruct((n, FOX_WIDTH), BF16),
        scratch_shapes=[pltpu.VMEM((2, seq, LANES), BF16),
                        pltpu.VMEM((2, FOX_VAUG_ROWS, seq), BF16),
                        pltpu.VMEM((2, 2, LANES, tq), BF16),
                        pltpu.VMEM((2, tq, tq), F32),
                        pltpu.VMEM((2, tq, tq), F32),
                        pltpu.VMEM((2, tq, tq), F32),
                        pltpu.VMEM((2, 1, tq), F32),
                        pltpu.VMEM((FOX_VAUG_ROWS, tq), F32),
                        pltpu.VMEM((FOX_VAUG_ROWS, tq), F32)],
        compiler_params=_params(("parallel", "parallel", "arbitrary"), 48),
    )(fqt, fk, cpieces, fvt, fox_gain)


MLSTM_ONES_ROWS = 16


def _mlstm_kernel(qk_ref, vt_ref, ogt_ref, col_ref, row_ref, cw_ref, gain_ref, o_ref,
                  tail_ref, buf_ref, c_sc, m_sc):
    c = pl.program_id(1)
    L = SEQ_BLOCK

    @pl.when(c == 0)
    def _():
        tail_ref[...] = jnp.zeros_like(tail_ref)
        c_sc[...] = jnp.zeros_like(c_sc)
        m_sc[...] = jnp.zeros_like(m_sc)

    x = qk_ref[...]
    buf_ref[0:8, :] = tail_ref[...]
    buf_ref[8:8 + L, :] = x
    tail_ref[...] = x[L - 8:L, :]
    y = x * cw_ref[CONV_WIDTH - 1:CONV_WIDTH, :]
    for j in range(CONV_WIDTH - 1):
        shift = CONV_WIDTH - 1 - j
        y = y + buf_ref[8 - shift:8 - shift + L, :] * cw_ref[j:j + 1, :]
    y = y * jax.nn.sigmoid(y)
    kf = y[:, MLSTM_QK_WIDTH:] * (MLSTM_QK_DIM ** -0.5)
    n_pair = MLSTM_HEADS // 2
    qt = [y[:, p * LANES:(p + 1) * LANES].T for p in range(n_pair)]
    kb = [kf[:, p * LANES:(p + 1) * LANES].astype(BF16) for p in range(n_pair)]

    col = col_ref[...]
    row = row_ref[0]
    lane = lax.broadcasted_iota(jnp.int32, (L, LANES), 1)
    sub = lax.broadcasted_iota(jnp.int32, (LANES, L), 0)
    ss = lax.broadcasted_iota(jnp.int32, (L, L), 0)
    ll = lax.broadcasted_iota(jnp.int32, (L, L), 1)
    causal = ss <= ll
    ones_rows = jnp.ones((MLSTM_ONES_ROWS, L), BF16)

    for h in range(MLSTM_HEADS):
        pair, half = divmod(h, 2)
        head_sub = (sub < MLSTM_QK_DIM) if half == 0 else (sub >= MLSTM_QK_DIM)
        head_lane = (lane < MLSTM_QK_DIM) if half == 0 else (lane >= MLSTM_QK_DIM)
        qth = jnp.where(head_sub, qt[pair], 0.0).astype(BF16)
        rcol = col[:, GATE_I0 + h:GATE_I0 + h + 1] - col[:, GATE_F0 + h:GATE_F0 + h + 1]
        brow = row[GATE_F0 + h:GATE_F0 + h + 1, :]
        lirow = row[GATE_I0 + h:GATE_I0 + h + 1, :]
        g = brow[:, L - 1:L]
        m_prev = m_sc[h][0:1, 0:1]

        dt = jnp.where(causal, rcol + brow, -jnp.inf)
        inter_log = brow + m_prev
        m_t = jnp.maximum(inter_log, jnp.max(dt, axis=0, keepdims=True))
        w_inter = jnp.exp(inter_log - m_t)
        pt = jnp.exp(dt - m_t) * _dot(kb[pair], qth)
        vaug = jnp.concatenate([vt_ref[h * LANES:(h + 1) * LANES, :], ones_rows], axis=0)
        cstate = c_sc[h]
        tot = w_inter * _dot(cstate.astype(BF16), qth) + _dot(vaug, pt.astype(BF16))
        den = tot[MLSTM_V_DIM:MLSTM_V_DIM + 1]
        hout = tot[0:MLSTM_V_DIM] * (1.0 / jnp.maximum(jnp.abs(den), jnp.exp(-m_t)))

        a = g + (lirow - brow)
        m_loc = jnp.max(a, axis=1, keepdims=True)
        vw = (vaug.astype(F32) * jnp.exp(a - m_loc)).astype(BF16)
        kmask = jnp.where(head_lane, kb[pair], jnp.zeros_like(kb[pair]))
        kv = _dot(vw, kmask)
        m_new = jnp.maximum(g + m_prev, m_loc)
        c_sc[h] = jnp.exp(g + m_prev - m_new) * cstate + jnp.exp(m_loc - m_new) * kv
        m_sc[h] = jnp.broadcast_to(m_new, m_sc.shape[1:])

        ms = jnp.mean(hout * hout, axis=0, keepdims=True)
        gain = gain_ref[h * LANES:(h + 1) * LANES, :]
        hn = hout * lax.rsqrt(ms + RMS_EPS) * jnp.concatenate([gain] * (L // LANES), axis=1)
        gate = jax.nn.sigmoid(ogt_ref[h * LANES:(h + 1) * LANES, :])
        o_ref[:, h * LANES:(h + 1) * LANES] = (hn * gate).T.astype(o_ref.dtype)


def _mlstm(mqk, mvt, mot, gcol, grow, conv_w, gain_lanes, batch, seq):
    n = mqk.shape[0]
    L = SEQ_BLOCK
    nc = seq // L
    qk_width = 2 * MLSTM_QK_WIDTH
    row = lambda width: pl.BlockSpec((L, width), lambda b, c: (b * nc + c, 0))
    col = pl.BlockSpec((MLSTM_V_WIDTH, L), lambda b, c: (0, b * nc + c))
    full = lambda a: pl.BlockSpec(a.shape, lambda b, c: (0, 0))
    return pl.pallas_call(
        _mlstm_kernel,
        grid=(batch, nc),
        in_specs=[row(qk_width), col, col, row(LANES),
                  pl.BlockSpec((1, 16, L), lambda b, c: (b, 0, c)),
                  full(conv_w), full(gain_lanes)],
        out_specs=row(MLSTM_V_WIDTH),
        out_shape=jax.ShapeDtypeStruct((n, MLSTM_V_WIDTH), BF16),
        scratch_shapes=[pltpu.VMEM((8, qk_width), F32), pltpu.VMEM((8 + L, qk_width), F32),
                        pltpu.VMEM((MLSTM_HEADS, MLSTM_V_DIM + MLSTM_ONES_ROWS, LANES), F32),
                        pltpu.VMEM((MLSTM_HEADS, 8, LANES), F32)],
        compiler_params=_params(("parallel", "arbitrary"), 48),
    )(mqk, mvt, mot, gcol, grow, conv_w, gain_lanes)


def _outproj_kernel(fo_ref, mo_ref, w_ref, x_ref, g_ref, b_ref, o_ref):
    mix = _dot(fo_ref[...], w_ref[0:FOX_WIDTH, :]) + _dot(mo_ref[...], w_ref[FOX_WIDTH:, :])
    o_ref[...] = _layer_norm(DEEPNORM_ALPHA * x_ref[...] + mix, g_ref[...], b_ref[...])


def _outproj(fo, mo, w_out, x2d, g, b):
    n = x2d.shape[0]
    tm = WIDE_ROW_TILE
    row = lambda width: pl.BlockSpec((tm, width), lambda i: (i, 0))
    full = lambda a: pl.BlockSpec(a.shape, lambda i: (0, 0))
    return pl.pallas_call(
        _outproj_kernel,
        grid=(n // tm,),
        in_specs=[row(512), row(512), full(w_out), row(D_MODEL), full(g), full(b)],
        out_specs=row(D_MODEL),
        out_shape=jax.ShapeDtypeStruct((n, D_MODEL), F32),
        compiler_params=_params(("parallel",), 48),
    )(fo, mo, w_out, x2d, g, b)


def _memkv_kernel(mem_ref, wk_ref, wv_ref, k_ref, v_ref):
    mb = mem_ref[...].astype(BF16)
    k_ref[...] = (_dot(mb, wk_ref[...]) * (XATTN_HEAD_DIM ** -0.5)).astype(BF16)
    v_ref[...] = _dot(mb, wv_ref[...]).astype(BF16)


def _memkv(mem2d, wk, wv, n_mem):
    n = mem2d.shape[0]
    row = pl.BlockSpec((n_mem, D_MODEL), lambda i: (i, 0))
    full = lambda a: pl.BlockSpec(a.shape, lambda i: (0, 0))
    return pl.pallas_call(
        _memkv_kernel,
        grid=(n // n_mem,),
        in_specs=[row, full(wk), full(wv)],
        out_specs=(row, row),
        out_shape=(jax.ShapeDtypeStruct((n, D_MODEL), BF16),) * 2,
        compiler_params=_params(("parallel",), 32),
    )(mem2d, wk, wv)


def _xattn_kernel(x_ref, k_ref, v_ref, wq_ref, wo_ref, g_ref, b_ref, wr_ref, br_ref,
                  o_ref, ob_ref, lg_ref):
    half = x_ref.shape[0] // 2
    for r in range(2):
        rows = slice(r * half, (r + 1) * half)
        x = x_ref[rows, :]
        q = _dot(x.astype(BF16), wq_ref[...]).astype(BF16)
        outs = []
        for h in range(XATTN_HEADS):
            sl = slice(h * XATTN_HEAD_DIM, (h + 1) * XATTN_HEAD_DIM)
            s = _dot_nt(q[:, sl], k_ref[:, sl])
            p = jnp.exp(s - jnp.max(s, axis=-1, keepdims=True))
            l = jnp.sum(p, axis=-1, keepdims=True)
            outs.append((_dot(p.astype(BF16), v_ref[:, sl]) / l).astype(BF16))
        o = jnp.concatenate(outs, axis=1)
        xa = _dot(o, wo_ref[...])
        x2 = _layer_norm(DEEPNORM_ALPHA * x + xa, g_ref[...], b_ref[...])
        o_ref[rows, :] = x2
        packed = _pack_bf16_pairs(x2)
        for j in range(ob_ref.shape[0]):
            ob_ref[j, rows, :] = packed[:, j * LANES:(j + 1) * LANES]
        x2h = x2.astype(BF16)
        x2l = (x2 - x2h.astype(F32)).astype(BF16)
        a = _dot_nt(wr_ref[...], x2h)
        b = _dot_nt(wr_ref[0:N_EXPERTS, :], x2l)
        lg_ref[:, rows] = a[0:N_EXPERTS] + a[N_EXPERTS:] + b + br_ref[...]


def _xattn(x1, kmem, vmem, wq, wo, g, b, wr, br, batch, seq, n_mem):
    n = x1.shape[0]
    tm = WIDE_ROW_TILE
    nt = seq // tm
    row = lambda width: pl.BlockSpec((tm, width), lambda bi, i: (bi * nt + i, 0))
    full = lambda a: pl.BlockSpec(a.shape, lambda bi, i: (0, 0))
    kv = pl.BlockSpec((n_mem, D_MODEL), lambda bi, i: (bi, 0))
    return pl.pallas_call(
        _xattn_kernel,
        grid=(batch, nt),
        in_specs=[row(D_MODEL), kv, kv, full(wq), full(wo), full(g), full(b), full(wr), full(br)],
        out_specs=(row(D_MODEL),
                   pl.BlockSpec((X_PLANES, tm, LANES), lambda bi, i: (0, bi * nt + i, 0)),
                   pl.BlockSpec((N_EXPERTS, tm), lambda bi, i: (0, bi * nt + i))),
        out_shape=(jax.ShapeDtypeStruct((n, D_MODEL), F32),
                   jax.ShapeDtypeStruct((X_PLANES, n, LANES), jnp.uint32),
                   jax.ShapeDtypeStruct((N_EXPERTS, n), F32)),
        compiler_params=_params(("parallel", "parallel"), 60),
    )(x1, kmem, vmem, wq, wo, g, b, wr, br)


def _route_kernel(lg_ref, idx_ref, rank_ref, gate_ref, cnt_ref, carry_ref):
    i = pl.program_id(0)

    @pl.when(i == 0)
    def _():
        carry_ref[...] = jnp.zeros_like(carry_ref)

    lg = lg_ref[...]
    t = lg.shape[1]
    e_idx = lax.broadcasted_iota(jnp.int32, lg.shape, 0).astype(F32)
    sels, vals, idxs = [], [], []
    for _ in range(TOP_K):
        mx = jnp.max(lg, axis=0, keepdims=True)
        first = jnp.min(jnp.where(lg == mx, e_idx, float(N_EXPERTS)), axis=0, keepdims=True)
        sel = e_idx == first
        sels.append(sel)
        vals.append(mx)
        idxs.append(first)
        lg = jnp.where(sel, -jnp.inf, lg)
    exps = [jnp.exp(v - vals[0]) for v in vals]
    tot = exps[0] + exps[1] + exps[2] + exps[3]

    selmat = (sels[0] | sels[1] | sels[2] | sels[3])
    r = lax.broadcasted_iota(jnp.int32, (t, t), 0)
    s = lax.broadcasted_iota(jnp.int32, (t, t), 1)
    earlier = (r < s).astype(BF16)
    carry = carry_ref[:, 0:1]
    rankmat = _dot(selmat.astype(BF16), earlier) + carry
    new_carry = carry + jnp.sum(selmat.astype(F32), axis=1, keepdims=True)
    carry_ref[...] = jnp.broadcast_to(new_carry, carry_ref.shape)
    cnt_ref[...] = jnp.broadcast_to(new_carry, cnt_ref.shape).astype(jnp.int32)

    row8 = lax.broadcasted_iota(jnp.int32, (8, t), 0)
    row128 = lax.broadcasted_iota(jnp.int32, (LANES, t), 0)
    idx_out = jnp.zeros((8, t), F32)
    rank_out = jnp.zeros((8, t), F32)
    gate_out = jnp.zeros((LANES, t), F32)
    for k in range(TOP_K):
        rk = jnp.sum(jnp.where(sels[k], rankmat, 0.0), axis=0, keepdims=True)
        idx_out = jnp.where(row8 == k, idxs[k], idx_out)
        rank_out = jnp.where(row8 == k, rk, rank_out)
        gate_out = jnp.where(row128 == k, exps[k] / tot, gate_out)
    idx_ref[...] = idx_out.astype(jnp.int32)
    rank_ref[...] = rank_out.astype(jnp.int32)
    gate_ref[...] = gate_out.T


def _route(logits_t):
    n = logits_t.shape[1]
    t = ROUTE_TILE
    col = lambda rows: pl.BlockSpec((rows, t), lambda i: (0, i))
    return pl.pallas_call(
        _route_kernel,
        grid=(n // t,),
        in_specs=[col(N_EXPERTS)],
        out_specs=(col(8), col(8), pl.BlockSpec((t, LANES), lambda i: (i, 0)),
                   pl.BlockSpec((N_EXPERTS, LANES), lambda i: (0, 0))),
        out_shape=(jax.ShapeDtypeStruct((8, n), jnp.int32),
                   jax.ShapeDtypeStruct((8, n), jnp.int32),
                   jax.ShapeDtypeStruct((n, LANES), F32),
                   jax.ShapeDtypeStruct((N_EXPERTS, LANES), jnp.int32)),
        scratch_shapes=[pltpu.VMEM((N_EXPERTS, LANES), F32)],
        compiler_params=_params(("arbitrary",), 32),
    )(logits_t)


def _expert_kernel(blk_e_ref, blk_rows_ref, next_e_ref, x_ref, wgu_hbm, bgu_ref, wd_hbm, bd_ref,
                   y_ref, wgu_f32, wd_f32, wgu_sc, wd_sc, state_ref, sem):
    i = pl.program_id(0)
    e = blk_e_ref[i]
    rows = blk_rows_ref[i]
    g = y_ref.shape[1]

    def weight_copies(expert, slot):
        return (pltpu.make_async_copy(wgu_hbm.at[expert], wgu_f32.at[slot], sem.at[0, slot]),
                pltpu.make_async_copy(wd_hbm.at[expert], wd_f32.at[slot], sem.at[1, slot]))

    @pl.when(i == 0)
    def _():
        state_ref[0] = -1
        state_ref[1] = 0

    @pl.when((rows > 0) & (e != state_ref[0]))
    def _():
        slot = state_ref[1]

        @pl.when(i == 0)
        def _():
            for cp in weight_copies(e, slot):
                cp.start()

        for cp in weight_copies(e, slot):
            cp.wait()
        wgu_sc[...] = wgu_f32[slot].astype(BF16)
        wd_sc[...] = wd_f32[slot].astype(BF16)
        nxt = next_e_ref[i]

        @pl.when(nxt >= 0)
        def _():
            for cp in weight_copies(nxt, 1 - slot):
                cp.start()

        state_ref[0] = e
        state_ref[1] = 1 - slot

    def ffn(m):
        packed = jnp.concatenate([x_ref[j, 0:m, :] for j in range(X_PLANES)], axis=1)
        xb = _unpack_bf16_pairs(packed)
        hids = []
        for c in range(D_EXPERT // EXPERT_CHUNK):
            g0 = c * EXPERT_CHUNK
            l0 = D_EXPERT + g0
            gate = _dot(xb, wgu_sc[:, g0:g0 + EXPERT_CHUNK]) + bgu_ref[0, :, g0:g0 + EXPERT_CHUNK]
            lin = _dot(xb, wgu_sc[:, l0:l0 + EXPERT_CHUNK]) + bgu_ref[0, :, l0:l0 + EXPERT_CHUNK]
            gate = jnp.minimum(gate, SWIGLU_LIMIT)
            lin = jnp.clip(lin, -SWIGLU_LIMIT, SWIGLU_LIMIT)
            hids.append((gate * jax.nn.sigmoid(SWIGLU_ALPHA * gate) * (lin + 1.0)).astype(BF16))
        y = _pack_bf16_pairs(_dot(jnp.concatenate(hids, axis=1), wd_sc[...]) + bd_ref[0])
        for j in range(Y_PLANES):
            y_ref[j, 0:m, :] = y[:, j * LANES:(j + 1) * LANES]

    @pl.when(rows > g // 2)
    def _():
        ffn(g)

    @pl.when((rows > 0) & (rows <= g // 2))
    def _():
        ffn(g // 2)
        y_ref[:, g // 2:, :] = jnp.zeros((Y_PLANES, g // 2, LANES), y_ref.dtype)

    @pl.when(rows == 0)
    def _():
        y_ref[...] = jnp.zeros_like(y_ref)


def _experts(blk_e, blk_rows, next_e, xs, w_gu, b_gu, w_d, b_d):
    p = xs.shape[1]
    g = EXPERT_ROWS
    grid_spec = pltpu.PrefetchScalarGridSpec(
        num_scalar_prefetch=3,
        grid=(p // g,),
        in_specs=[pl.BlockSpec((X_PLANES, g, LANES), lambda i, be, br, ne: (0, i, 0)),
                  pl.BlockSpec(memory_space=pl.ANY),
                  pl.BlockSpec((1, 1, 2 * D_EXPERT), lambda i, be, br, ne: (be[i], 0, 0)),
                  pl.BlockSpec(memory_space=pl.ANY),
                  pl.BlockSpec((1, 1, D_MODEL), lambda i, be, br, ne: (be[i], 0, 0))],
        out_specs=pl.BlockSpec((Y_PLANES, g, LANES), lambda i, be, br, ne: (0, i, 0)),
        scratch_shapes=[pltpu.VMEM((2, D_MODEL, 2 * D_EXPERT), F32),
                        pltpu.VMEM((2, D_EXPERT, D_MODEL), F32),
                        pltpu.VMEM((D_MODEL, 2 * D_EXPERT), BF16),
                        pltpu.VMEM((D_EXPERT, D_MODEL), BF16),
                        pltpu.SMEM((2,), jnp.int32),
                        pltpu.SemaphoreType.DMA((2, 2))],
    )
    return pl.pallas_call(
        _expert_kernel,
        grid_spec=grid_spec,
        out_shape=jax.ShapeDtypeStruct((Y_PLANES, p, LANES), jnp.uint32),
        compiler_params=_params(("arbitrary",), 56),
    )(blk_e, blk_rows, next_e, xs, w_gu, b_gu, w_d, b_d)


def _combine_kernel(y_ref, gate_ref, x_ref, g_ref, b_ref, o_ref):
    gate = gate_ref[...]
    ff = None
    for k in range(TOP_K):
        packed = jnp.concatenate([y_ref[k, j] for j in range(Y_PLANES)], axis=1)
        yk = _unpack_bf16_pairs(packed, F32) * gate[:, k:k + 1]
        ff = yk if ff is None else ff + yk
    o_ref[...] = _layer_norm(DEEPNORM_ALPHA * x_ref[...] + ff, g_ref[...], b_ref[...])


def _combine(yg, gate, x2, g, b):
    n = x2.shape[0]
    tm = ROW_TILE
    row = lambda width: pl.BlockSpec((tm, width), lambda i: (i, 0))
    full = lambda a: pl.BlockSpec(a.shape, lambda i: (0, 0))
    return pl.pallas_call(
        _combine_kernel,
        grid=(n // tm,),
        in_specs=[pl.BlockSpec((TOP_K, Y_PLANES, tm, LANES), lambda i: (0, 0, i, 0)),
                  row(LANES), row(D_MODEL), full(g), full(b)],
        out_specs=row(D_MODEL),
        out_shape=jax.ShapeDtypeStruct((n, D_MODEL), F32),
        compiler_params=_params(("parallel",), 32),
    )(yg, gate, x2, g, b)


SC_WINDOW = 128
SC_GATHERS_IN_FLIGHT = 2


def _sc_mesh():
    return plsc.VectorSubcoreMesh(core_axis_name="core", subcore_axis_name="subcore")


def _sc_gather(table, idx):
    m = idx.shape[0]
    n_fly = SC_GATHERS_IN_FLIGHT

    @pl.kernel(out_type=jax.ShapeDtypeStruct((m, LANES), table.dtype), mesh=_sc_mesh(),
               scratch_types=[pltpu.SemaphoreType.DMA])
    def gather_kernel(table_hbm, idx_hbm, out_hbm, sem):
        def body(*refs):
            idx_vmem, out_vmem = refs[:n_fly], refs[n_fly]
            copies = [pltpu.async_copy(table_hbm.at[iv.at[0]],
                                       out_vmem.at[pl.ds(u * SC_WINDOW, SC_WINDOW)], sem)
                      for u, iv in enumerate(idx_vmem)]
            for cp in copies:
                cp.wait()

        pltpu.emit_pipeline(
            body,
            grid=(m // (n_fly * SC_WINDOW),),
            in_specs=[pl.BlockSpec((1, SC_WINDOW), lambda i, u=u: (0, n_fly * i + u))
                      for u in range(n_fly)],
            out_specs=[pl.BlockSpec((n_fly * SC_WINDOW, LANES), lambda i: (i, 0))],
            core_axis_name=("core", "subcore"),
            dimension_semantics=(pltpu.PARALLEL,),
        )(*([idx_hbm] * n_fly), out_hbm)

    return gather_kernel(table, idx.reshape(1, m))


def _sc_scatter(src, idx_lists, out_rows):
    m = src.shape[0]
    n_lists = len(idx_lists)

    @pl.kernel(out_type=jax.ShapeDtypeStruct((out_rows, LANES), src.dtype), mesh=_sc_mesh(),
               scratch_types=[pltpu.SemaphoreType.DMA])
    def scatter_kernel(src_hbm, *refs):
        idx_hbm, out_hbm, sem = refs[:n_lists], refs[n_lists], refs[n_lists + 1]

        def body(src_vmem, *idx_vmem):
            copies = [pltpu.async_copy(src_vmem, out_hbm.at[iv.at[0]], sem) for iv in idx_vmem]
            for cp in copies:
                cp.wait()

        pltpu.emit_pipeline(
            body,
            grid=(m // SC_WINDOW,),
            in_specs=[pl.BlockSpec((SC_WINDOW, LANES), lambda i: (i, 0))]
            + [pl.BlockSpec((1, SC_WINDOW), lambda i: (0, i))] * n_lists,
            out_specs=[],
            core_axis_name=("core", "subcore"),
            dimension_semantics=(pltpu.PARALLEL,),
        )(src_hbm, *idx_hbm)

    return scatter_kernel(src, *[ix.reshape(1, m) for ix in idx_lists])


def _layer(x, mem, w_in, fox_f_bias, conv_w, i_bias, f_bias, fox_g, mlstm_g, w_mix_out,
           ln1_g, ln1_b, w_xq, w_xk, w_xv, w_xo, ln2_g, ln2_b, w_router, b_router,
           w_gate_up, b_gate_up, w_down, b_down, ln3_g, ln3_b):
    batch, seq, d = x.shape
    n_mem = mem.shape[1]
    n = batch * seq
    x2d = x.reshape(n, d)

    o_ff = 3 * FOX_WIDTH
    o_mqk = o_ff + FOX_HEADS
    o_mv = o_mqk + 2 * MLSTM_QK_WIDTH
    o_mi = o_mv + MLSTM_V_WIDTH
    o_mf = o_mi + MLSTM_HEADS
    o_mo = o_mf + MLSTM_HEADS
    n_gate = FOX_HEADS + 2 * MLSTM_HEADS
    w_r = jnp.concatenate(
        [w_in[:, FOX_WIDTH:2 * FOX_WIDTH], w_in[:, o_mqk:o_mv],
         w_in[:, o_ff:o_mqk], w_in[:, o_mi:o_mo],
         jnp.zeros((d, LANES - n_gate), w_in.dtype)], axis=1).astype(BF16)
    w_t = jnp.concatenate([w_in[:, :FOX_WIDTH], w_in[:, 2 * FOX_WIDTH:o_ff],
                           w_in[:, o_mv:o_mi], w_in[:, o_mo:]], axis=1).T.astype(BF16)
    gate_bias = jnp.concatenate(
        [fox_f_bias, i_bias, f_bias, jnp.zeros((LANES - n_gate,), F32)]).reshape(1, LANES)

    fqt, fk, fvt, mqk, mvt, mot, gates = _inproj(x2d, w_r, w_t, gate_bias)
    gcol, grow, cpieces = _gateprep(gates, batch, seq)
    fo = _fox(fqt, fk, cpieces, fvt, fox_g.reshape(1, FOX_WIDTH), batch, seq)
    gain_lanes = jnp.broadcast_to(mlstm_g[:, None], (MLSTM_V_WIDTH, LANES))
    mo_out = _mlstm(mqk, mvt, mot, gcol, grow, conv_w, gain_lanes, batch, seq)
    x1 = _outproj(fo, mo_out, w_mix_out.astype(BF16), x2d, ln1_g.reshape(1, d), ln1_b.reshape(1, d))

    kmem, vmem = _memkv(mem.reshape(batch * n_mem, d), w_xk.astype(BF16), w_xv.astype(BF16), n_mem)
    wrt = w_router.T
    wrt_hi = wrt.astype(BF16)
    wrt_lo = (wrt - wrt_hi.astype(F32)).astype(BF16)
    x2, x2p, logits_t = _xattn(x1, kmem, vmem, w_xq.astype(BF16), w_xo.astype(BF16),
                               ln2_g.reshape(1, d), ln2_b.reshape(1, d),
                               jnp.concatenate([wrt_hi, wrt_lo], axis=0),
                               b_router.reshape(N_EXPERTS, 1), batch, seq, n_mem)

    idx_t, rank_t, gate, cnt = _route(logits_t)
    counts = cnt[:, 0]
    g_rows = EXPERT_ROWS
    padded = ((counts + g_rows - 1) // g_rows) * g_rows
    pad_end = jnp.cumsum(padded)
    pad_start = pad_end - padded
    experts = jnp.arange(N_EXPERTS, dtype=jnp.int32)
    sel = idx_t[:TOP_K, :, None] == experts[None, None, :]
    pos_t = jnp.sum(jnp.where(sel, pad_start[None, None, :], 0), axis=-1) + rank_t[:TOP_K]
    p_rows = n * TOP_K + N_EXPERTS * g_rows
    nb = p_rows // g_rows
    blk_start = jnp.arange(nb, dtype=jnp.int32) * g_rows
    blk_e = jnp.minimum(jnp.sum((pad_end[None, :] <= blk_start[:, None]).astype(jnp.int32), axis=1),
                        N_EXPERTS - 1)
    own = blk_e[:, None] == experts[None, :]
    row_end = jnp.sum(jnp.where(own, (pad_start + counts)[None, :], 0), axis=1)
    blk_rows = jnp.clip(row_end - blk_start, 0, g_rows).astype(jnp.int32)
    later = (experts[None, :] > blk_e[:, None]) & (counts[None, :] > 0)
    next_e = jnp.min(jnp.where(later, experts[None, :], N_EXPERTS), axis=1)
    next_e = jnp.where(next_e == N_EXPERTS, -1, next_e).astype(jnp.int32)

    def piece_index(planes):
        off = jnp.arange(planes, dtype=jnp.int32) * p_rows
        return pos_t[:, None, :] + off[None, :, None]

    x_idx = piece_index(X_PLANES).reshape(TOP_K, X_PLANES * n)
    xs = _sc_scatter(x2p.reshape(X_PLANES * n, LANES), [x_idx[k] for k in range(TOP_K)],
                     X_PLANES * p_rows)
    y = _experts(blk_e, blk_rows, next_e, xs.reshape(X_PLANES, p_rows, LANES), w_gate_up,
                 b_gate_up.reshape(N_EXPERTS, 1, -1), w_down, b_down.reshape(N_EXPERTS, 1, -1))
    yg = _sc_gather(y.reshape(Y_PLANES * p_rows, LANES), piece_index(Y_PLANES).reshape(-1))
    out = _combine(yg.reshape(TOP_K, Y_PLANES, n, LANES), gate, x2,
                   ln3_g.reshape(1, d), ln3_b.reshape(1, d))
    return out.reshape(batch, seq, d)


def kernel(x, mem, w_in, fox_f_bias, mlstm_conv_w, mlstm_i_bias, mlstm_f_bias, fox_norm_g, mlstm_norm_g, w_mix_out, ln1_g, ln1_b, w_xq, w_xk, w_xv, w_xo, ln2_g, ln2_b, w_router, b_router, w_gate_up, b_gate_up, w_down, b_down, ln3_g, ln3_b):
    for l in range(w_in.shape[0]):
        x = _layer(x, mem, w_in[l], fox_f_bias[l], mlstm_conv_w[l], mlstm_i_bias[l],
                   mlstm_f_bias[l], fox_norm_g[l], mlstm_norm_g[l], w_mix_out[l],
                   ln1_g[l], ln1_b[l], w_xq[l], w_xk[l], w_xv[l], w_xo[l], ln2_g[l], ln2_b[l],
                   w_router[l], b_router[l], w_gate_up[l], b_gate_up[l], w_down[l], b_down[l],
                   ln3_g[l], ln3_b[l])
    return x
```

```python
import jax
import jax.numpy as jnp
from jax import lax
from jax.experimental import pallas as pl
from jax.experimental.pallas import tpu as pltpu
from jax.experimental.pallas import tpu_sc as plsc

F32 = jnp.float32
BF16 = jnp.bfloat16

D_MODEL = 1024
FOX_HEADS = 8
FOX_HEAD_DIM = 64
FOX_WIDTH = FOX_HEADS * FOX_HEAD_DIM
MLSTM_HEADS = 4
MLSTM_QK_DIM = 64
MLSTM_V_DIM = 128
MLSTM_QK_WIDTH = MLSTM_HEADS * MLSTM_QK_DIM
MLSTM_V_WIDTH = MLSTM_HEADS * MLSTM_V_DIM
CONV_WIDTH = 4
XATTN_HEADS = 4
XATTN_HEAD_DIM = D_MODEL // XATTN_HEADS
N_EXPERTS = 32
TOP_K = 4
D_EXPERT = D_MODEL
SWIGLU_LIMIT = 7.0
SWIGLU_ALPHA = 1.702
DEEPNORM_ALPHA = 2.0 ** 0.25
LN_EPS = 1e-5
RMS_EPS = 1e-6

LANES = 128
SEQ_BLOCK = 256
ROW_TILE = 512
WIDE_ROW_TILE = 1024
EXPERT_ROWS = 512
EXPERT_CHUNK = 256
ROUTE_TILE = 512
X_PLANES = D_MODEL // 2 // LANES
Y_PLANES = X_PLANES
GATE_I0 = FOX_HEADS
GATE_F0 = FOX_HEADS + MLSTM_HEADS

MIB = 1024 * 1024


def _params(semantics, vmem_mib):
    return pltpu.CompilerParams(dimension_semantics=semantics,
                                vmem_limit_bytes=vmem_mib * MIB)


def _layer_norm(y, g, b):
    mu = jnp.mean(y, axis=-1, keepdims=True)
    yc = y - mu
    var = jnp.mean(yc * yc, axis=-1, keepdims=True)
    return yc * lax.rsqrt(var + LN_EPS) * g + b


def _dot(a, b):
    return jnp.dot(a, b, preferred_element_type=F32)


def _dot_nt(a, b):
    return lax.dot_general(a, b, (((1,), (1,)), ((), ())), preferred_element_type=F32)


def _dot_tn(a, b):
    return lax.dot_general(a, b, (((0,), (0,)), ((), ())), preferred_element_type=F32)


def _pack_bf16_pairs(x):
    w = x.shape[1] // 2
    lo = pltpu.bitcast(x[:, :w].astype(BF16).astype(F32), jnp.uint32)
    hi = pltpu.bitcast(x[:, w:].astype(BF16).astype(F32), jnp.uint32)
    return (lo >> 16) | hi


def _unpack_bf16_pairs(u, dtype=BF16):
    lo = pltpu.bitcast(u << 16, F32).astype(dtype)
    hi = pltpu.bitcast(u & jnp.uint32(0xFFFF0000), F32).astype(dtype)
    return jnp.concatenate([lo, hi], axis=1)


def _inproj_kernel(x_ref, w_ref, wt_ref, gb_ref, fqt_ref, fk_ref, fvt_ref, mqk_ref, mvt_ref,
                   mot_ref, g_ref):
    xb = x_ref[...].astype(BF16)

    def mm(c0, width):
        return _dot(xb, w_ref[:, c0:c0 + width])

    def mm_t(r0, height):
        return _dot_nt(wt_ref[r0:r0 + height, :], xb)

    r_fv = FOX_WIDTH
    r_mv = r_fv + FOX_WIDTH
    r_mo = r_mv + MLSTM_V_WIDTH
    fqt_ref[...] = (mm_t(0, FOX_WIDTH) * (FOX_HEAD_DIM ** -0.5)).astype(BF16)
    fvt_ref[...] = mm_t(r_fv, FOX_WIDTH).astype(BF16)
    mvt_ref[...] = mm_t(r_mv, MLSTM_V_WIDTH).astype(BF16)
    mot_ref[...] = mm_t(r_mo, MLSTM_V_WIDTH)
    c_qk = FOX_WIDTH
    c_g = c_qk + 2 * MLSTM_QK_WIDTH
    fk_ref[...] = mm(0, FOX_WIDTH).astype(BF16)
    mqk_ref[...] = mm(c_qk, 2 * MLSTM_QK_WIDTH)
    g_ref[...] = mm(c_g, LANES) + gb_ref[...]


def _inproj(x2d, w_r, w_t, gate_bias):
    n = x2d.shape[0]
    tm = WIDE_ROW_TILE
    row = lambda width: pl.BlockSpec((tm, width), lambda i: (i, 0))
    col = lambda height: pl.BlockSpec((height, tm), lambda i: (0, i))
    full = lambda a: pl.BlockSpec(a.shape, lambda i: (0, 0))
    out_shapes = (
        jax.ShapeDtypeStruct((FOX_WIDTH, n), BF16),
        jax.ShapeDtypeStruct((n, FOX_WIDTH), BF16),
        jax.ShapeDtypeStruct((FOX_WIDTH, n), BF16),
        jax.ShapeDtypeStruct((n, 2 * MLSTM_QK_WIDTH), F32),
        jax.ShapeDtypeStruct((MLSTM_V_WIDTH, n), BF16),
        jax.ShapeDtypeStruct((MLSTM_V_WIDTH, n), F32),
        jax.ShapeDtypeStruct((n, LANES), F32),
    )
    return pl.pallas_call(
        _inproj_kernel,
        grid=(n // tm,),
        in_specs=[row(D_MODEL), full(w_r), full(w_t), full(gate_bias)],
        out_specs=(col(FOX_WIDTH), row(FOX_WIDTH), col(FOX_WIDTH), row(2 * MLSTM_QK_WIDTH),
                   col(MLSTM_V_WIDTH), col(MLSTM_V_WIDTH), row(LANES)),
        out_shape=out_shapes,
        compiler_params=_params(("parallel",), 48),
    )(x2d, w_r, w_t, gate_bias)


def _split3(x):
    hi = x.astype(BF16)
    r1 = x - hi.astype(F32)
    mid = r1.astype(BF16)
    lo = (r1 - mid.astype(F32)).astype(BF16)
    return hi, mid, lo


def _gateprep_kernel(g_ref, sel_ref, col_ref, row_ref, cp_ref, carry_ref):
    c = pl.program_id(1)

    @pl.when(c == 0)
    def _():
        carry_ref[...] = jnp.zeros_like(carry_ref)

    g = g_ref[...]
    lane = lax.broadcasted_iota(jnp.int32, g.shape, 1)
    is_i = (lane >= GATE_I0) & (lane < GATE_F0)
    logsig = jnp.minimum(g, 0.0) - jnp.log1p(jnp.exp(-jnp.abs(g)))
    blk = g.shape[0]
    r = lax.broadcasted_iota(jnp.int32, (blk, blk), 0)
    s = lax.broadcasted_iota(jnp.int32, (blk, blk), 1)
    tri = (s <= r).astype(BF16)
    cs = None
    for piece in _split3(logsig):
        term = _dot(tri, piece)
        cs = term if cs is None else cs + term
    carry = carry_ref[0:1, :]
    glob = cs + carry
    carry_ref[...] = jnp.broadcast_to(glob[blk - 1:blk, :], carry_ref.shape)
    out = jnp.where(lane < GATE_I0, glob, jnp.where(is_i, g, cs))
    col_ref[...] = out
    row_ref[0] = out.T[0:16, :]

    pieces = jnp.concatenate(_split3(-glob), axis=1)
    moved = _dot(pieces, sel_ref[...])
    for p in range(FOX_HEADS // 2):
        cp_ref[0, p] = moved[:, p * LANES:(p + 1) * LANES].astype(BF16)


def _piece_selector():
    src = jnp.arange(3 * LANES, dtype=jnp.int32)[:, None]
    dst = jnp.arange(4 * LANES, dtype=jnp.int32)[None, :]
    piece, head = src // LANES, src % LANES
    pair, lane = dst // LANES, dst % LANES
    hit = (lane < 6) & (lane % 3 == piece) & (head == 2 * pair + lane // 3)
    return hit.astype(BF16)


def _gateprep(gates, batch, seq):
    n = gates.shape[0]
    nc = seq // SEQ_BLOCK
    sel = _piece_selector()
    return pl.pallas_call(
        _gateprep_kernel,
        grid=(batch, nc),
        in_specs=[pl.BlockSpec((SEQ_BLOCK, LANES), lambda b, c: (b * nc + c, 0)),
                  pl.BlockSpec(sel.shape, lambda b, c: (0, 0))],
        out_specs=(pl.BlockSpec((SEQ_BLOCK, LANES), lambda b, c: (b * nc + c, 0)),
                   pl.BlockSpec((1, 16, SEQ_BLOCK), lambda b, c: (b, 0, c)),
                   pl.BlockSpec((1, FOX_HEADS // 2, SEQ_BLOCK, LANES), lambda b, c: (b, 0, c, 0))),
        out_shape=(jax.ShapeDtypeStruct((n, LANES), F32),
                   jax.ShapeDtypeStruct((batch, 16, seq), F32),
                   jax.ShapeDtypeStruct((batch, FOX_HEADS // 2, seq, LANES), BF16)),
        scratch_shapes=[pltpu.VMEM((8, LANES), F32)],
        compiler_params=_params(("parallel", "arbitrary"), 32),
    )(gates, sel)


FOX_ONES_ROWS = 16
FOX_VAUG_ROWS = FOX_HEAD_DIM + FOX_ONES_ROWS
FOX_QUERY_TILE = 2 * SEQ_BLOCK


def _fox_kernel(qt_ref, k_ref, cp_ref, vt_ref, gain_ref, o_ref,
                kaug_sc, vaug_sc, qaug_sc, sta_sc, stb_sc, stc_sc, m_sc, acc0_sc, acc1_sc):
    qi = pl.program_id(2)
    tq = o_ref.shape[0]
    tg = tq
    tk = SEQ_BLOCK
    hd = FOX_HEAD_DIM
    seq = k_ref.shape[0]
    nq = seq // tq
    par = qi % 2

    def load_queries(tile, slot):
        q0 = pl.multiple_of(tile * tq, tq)
        for h in range(2):
            qaug_sc[slot, h, 0:hd, :] = qt_ref[h * hd:(h + 1) * hd, pl.ds(q0, tq)]

    @pl.when(qi == 0)
    def _():
        ones3 = (lax.broadcasted_iota(jnp.int32, (FOX_ONES_ROWS, tq), 0) < 3).astype(BF16)
        for slot in range(2):
            for h in range(2):
                qaug_sc[slot, h, hd:hd + FOX_ONES_ROWS, :] = ones3
                qaug_sc[slot, h, hd + FOX_ONES_ROWS:, :] = jnp.zeros(
                    (LANES - hd - FOX_ONES_ROWS, tq), BF16)
        load_queries(0, 0)

    nxt = jnp.minimum(qi + 1, nq - 1)
    load_queries(nxt, 1 - par)

    @pl.when(qi == 0)
    def _():
        lane = lax.broadcasted_iota(jnp.int32, (tk, LANES), 1)

        def build(blk, carry):
            r0 = pl.multiple_of(blk * tk, tk)
            kp = k_ref[pl.ds(r0, tk), :].astype(F32)
            cp = cp_ref[0, 0, pl.ds(r0, tk), :].astype(F32)
            for h in range(2):
                kh = kp if h == 0 else pltpu.roll(kp, hd, axis=1)
                ch = pltpu.roll(cp, hd - 3 * h, axis=1)
                kaug = jnp.where(lane < hd, kh, jnp.where(lane < hd + 3, ch, 0.0))
                kaug_sc[h, pl.ds(r0, tk), :] = kaug.astype(BF16)
            return carry

        lax.fori_loop(0, seq // tk, build, 0)
        for h in range(2):
            vaug_sc[h, 0:hd, :] = vt_ref[h * hd:(h + 1) * hd, :]
            vaug_sc[h, hd:, :] = jnp.ones((FOX_ONES_ROWS, seq), BF16)

    def put_scores(kg, slot, masked=False, queries=None):
        k0 = pl.multiple_of(kg * tg, tg)
        queries = par if queries is None else queries
        for h in range(2):
            st = _dot(kaug_sc[h, pl.ds(k0, tg), :], qaug_sc[queries, h])
            if masked:
                kk = lax.broadcasted_iota(jnp.int32, (tg, tq), 0)
                tt = lax.broadcasted_iota(jnp.int32, (tg, tq), 1)
                st = jnp.where(kk <= tt, st, -jnp.inf)
            slot[h] = st

    m_sc[...] = jnp.full(m_sc.shape, -jnp.inf, F32)
    acc0_sc[...] = jnp.zeros(acc0_sc.shape, F32)
    acc1_sc[...] = jnp.zeros(acc1_sc.shape, F32)
    acc = (acc0_sc, acc1_sc)

    def absorb(kg, slot):
        k0 = pl.multiple_of(kg * tg, tg)
        half = tg // 2
        for h in range(2):
            st = slot[h]
            m_prev = m_sc[h]
            m_new = jnp.maximum(m_prev, jnp.max(st, axis=0, keepdims=True))
            pv = None
            for u in range(2):
                p = jnp.exp(st[u * half:(u + 1) * half] - m_new)
                part = _dot(vaug_sc[h, :, pl.ds(k0 + u * half, half)], p.astype(BF16))
                pv = part if pv is None else pv + part
            acc[h][...] = jnp.exp(m_prev - m_new) * acc[h][...] + pv
            m_sc[h] = m_new

    sa, sb, sc = sta_sc, stb_sc, stc_sc
    n_loop = jnp.maximum(qi - 1, 0) // 2

    def body(j, carry):
        put_scores(2 * j + 1, sb)
        absorb(2 * j, sa)
        put_scores(2 * j + 2, sa)
        absorb(2 * j + 1, sb)
        return carry

    lax.fori_loop(0, n_loop, body, 0)
    done = 2 * n_loop

    def put_next():
        put_scores(0, sa, queries=1 - par)

    @pl.when(qi == 0)
    def _():
        put_scores(qi, sc, masked=True)
        put_next()
        absorb(qi, sc)

    @pl.when((qi > 0) & (qi - done == 1))
    def _():
        put_scores(qi, sc, masked=True)
        absorb(done, sa)
        put_next()
        absorb(qi, sc)

    @pl.when((qi > 0) & (qi - done == 2))
    def _():
        put_scores(done + 1, sb)
        absorb(done, sa)
        put_scores(qi, sc, masked=True)
        absorb(done + 1, sb)
        put_next()
        absorb(qi, sc)

    a0 = acc0_sc[...]
    a1 = acc1_sc[...]
    ot = jnp.concatenate([a0[0:hd] / a0[hd:hd + 1], a1[0:hd] / a1[hd:hd + 1]], axis=0)
    o = ot.T
    lane = lax.broadcasted_iota(jnp.int32, (tq, LANES), 1)
    lo = lane < FOX_HEAD_DIM
    sq = o * o
    ss0 = jnp.sum(jnp.where(lo, sq, 0.0), axis=-1, keepdims=True)
    ss1 = jnp.sum(jnp.where(lo, 0.0, sq), axis=-1, keepdims=True)
    inv = jnp.where(lo, lax.rsqrt(ss0 / FOX_HEAD_DIM + RMS_EPS),
                    lax.rsqrt(ss1 / FOX_HEAD_DIM + RMS_EPS))
    o_ref[...] = (o * inv * gain_ref[...]).astype(o_ref.dtype)


def _fox(fqt, fk, cpieces, fvt, fox_gain, batch, seq):
    n = fk.shape[0]
    tq = FOX_QUERY_TILE
    nq = seq // tq
    npair = FOX_HEADS // 2
    return pl.pallas_call(
        _fox_kernel,
        grid=(batch, npair, nq),
        in_specs=[pl.BlockSpec((LANES, seq), lambda b, hp, qi: (hp, b)),
                  pl.BlockSpec((seq, LANES), lambda b, hp, qi: (b, hp)),
                  pl.BlockSpec((1, 1, seq, LANES), lambda b, hp, qi: (b, hp, 0, 0)),
                  pl.BlockSpec((LANES, seq), lambda b, hp, qi: (hp, b)),
                  pl.BlockSpec((1, LANES), lambda b, hp, qi: (0, hp))],
        out_specs=pl.BlockSpec((tq, LANES), lambda b, hp, qi: (b * nq + qi, hp)),
        out_shape=jax.ShapeDtypeStruct((n, FOX_WIDTH), BF16),
        scratch_shapes=[pltpu.VMEM((2, seq, LANES), BF16),
                        pltpu.VMEM((2, FOX_VAUG_ROWS, seq), BF16),
                        pltpu.VMEM((2, 2, LANES, tq), BF16),
                        pltpu.VMEM((2, tq, tq), F32),
                        pltpu.VMEM((2, tq, tq), F32),
                        pltpu.VMEM((2, tq, tq), F32),
                        pltpu.VMEM((2, 1, tq), F32),
                        pltpu.VMEM((FOX_VAUG_ROWS, tq), F32),
                        pltpu.VMEM((FOX_VAUG_ROWS, tq), F32)],
        compiler_params=_params(("parallel", "parallel", "arbitrary"), 48),
    )(fqt, fk, cpieces, fvt, fox_gain)


MLSTM_ONES_ROWS = 16


def _mlstm_kernel(qk_ref, vt_ref, ogt_ref, col_ref, row_ref, cw_ref, gain_ref, o_ref,
                  tail_ref, buf_ref, c_sc, m_sc):
    c = pl.program_id(1)
    L = SEQ_BLOCK

    @pl.when(c == 0)
    def _():
        tail_ref[...] = jnp.zeros_like(tail_ref)
        c_sc[...] = jnp.zeros_like(c_sc)
        m_sc[...] = jnp.zeros_like(m_sc)

    x = qk_ref[...]
    buf_ref[0:8, :] = tail_ref[...]
    buf_ref[8:8 + L, :] = x
    tail_ref[...] = x[L - 8:L, :]
    y = x * cw_ref[CONV_WIDTH - 1:CONV_WIDTH, :]
    for j in range(CONV_WIDTH - 1):
        shift = CONV_WIDTH - 1 - j
        y = y + buf_ref[8 - shift:8 - shift + L, :] * cw_ref[j:j + 1, :]
    y = y * jax.nn.sigmoid(y)
    kf = y[:, MLSTM_QK_WIDTH:] * (MLSTM_QK_DIM ** -0.5)
    n_pair = MLSTM_HEADS // 2
    qt = [y[:, p * LANES:(p + 1) * LANES].T for p in range(n_pair)]
    kb = [kf[:, p * LANES:(p + 1) * LANES].astype(BF16) for p in range(n_pair)]

    col = col_ref[...]
    row = row_ref[0]
    lane = lax.broadcasted_iota(jnp.int32, (L, LANES), 1)
    sub = lax.broadcasted_iota(jnp.int32, (LANES, L), 0)
    ss = lax.broadcasted_iota(jnp.int32, (L, L), 0)
    ll = lax.broadcasted_iota(jnp.int32, (L, L), 1)
    causal = ss <= ll
    ones_rows = jnp.ones((MLSTM_ONES_ROWS, L), BF16)

    for h in range(MLSTM_HEADS):
        pair, half = divmod(h, 2)
        head_sub = (sub < MLSTM_QK_DIM) if half == 0 else (sub >= MLSTM_QK_DIM)
        head_lane = (lane < MLSTM_QK_DIM) if half == 0 else (lane >= MLSTM_QK_DIM)
        qth = jnp.where(head_sub, qt[pair], 0.0).astype(BF16)
        rcol = col[:, GATE_I0 + h:GATE_I0 + h + 1] - col[:, GATE_F0 + h:GATE_F0 + h + 1]
        brow = row[GATE_F0 + h:GATE_F0 + h + 1, :]
        lirow = row[GATE_I0 + h:GATE_I0 + h + 1, :]
        g = brow[:, L - 1:L]
        m_prev = m_sc[h][0:1, 0:1]

        dt = jnp.where(causal, rcol + brow, -jnp.inf)
        inter_log = brow + m_prev
        m_t = jnp.maximum(inter_log, jnp.max(dt, axis=0, keepdims=True))
        w_inter = jnp.exp(inter_log - m_t)
        pt = jnp.exp(dt - m_t) * _dot(kb[pair], qth)
        vaug = jnp.concatenate([vt_ref[h * LANES:(h + 1) * LANES, :], ones_rows], axis=0)
        cstate = c_sc[h]
        tot = w_inter * _dot(cstate.astype(BF16), qth) + _dot(vaug, pt.astype(BF16))
        den = tot[MLSTM_V_DIM:MLSTM_V_DIM + 1]
        hout = tot[0:MLSTM_V_DIM] * (1.0 / jnp.maximum(jnp.abs(den), jnp.exp(-m_t)))

        a = g + (lirow - brow)
        m_loc = jnp.max(a, axis=1, keepdims=True)
        vw = (vaug.astype(F32) * jnp.exp(a - m_loc)).astype(BF16)
        kmask = jnp.where(head_lane, kb[pair], jnp.zeros_like(kb[pair]))
        kv = _dot(vw, kmask)
        m_new = jnp.maximum(g + m_prev, m_loc)
        c_sc[h] = jnp.exp(g + m_prev - m_new) * cstate + jnp.exp(m_loc - m_new) * kv
        m_sc[h] = jnp.broadcast_to(m_new, m_sc.shape[1:])

        ms = jnp.mean(hout * hout, axis=0, keepdims=True)
        gain = gain_ref[h * LANES:(h + 1) * LANES, :]
        hn = hout * lax.rsqrt(ms + RMS_EPS) * jnp.concatenate([gain] * (L // LANES), axis=1)
        gate = jax.nn.sigmoid(ogt_ref[h * LANES:(h + 1) * LANES, :])
        o_ref[:, h * LANES:(h + 1) * LANES] = (hn * gate).T.astype(o_ref.dtype)


def _mlstm(mqk, mvt, mot, gcol, grow, conv_w, gain_lanes, batch, seq):
    n = mqk.shape[0]
    L = SEQ_BLOCK
    nc = seq // L
    qk_width = 2 * MLSTM_QK_WIDTH
    row = lambda width: pl.BlockSpec((L, width), lambda b, c: (b * nc + c, 0))
    col = pl.BlockSpec((MLSTM_V_WIDTH, L), lambda b, c: (0, b * nc + c))
    full = lambda a: pl.BlockSpec(a.shape, lambda b, c: (0, 0))
    return pl.pallas_call(
        _mlstm_kernel,
        grid=(batch, nc),
        in_specs=[row(qk_width), col, col, row(LANES),
                  pl.BlockSpec((1, 16, L), lambda b, c: (b, 0, c)),
                  full(conv_w), full(gain_lanes)],
        out_specs=row(MLSTM_V_WIDTH),
        out_shape=jax.ShapeDtypeStruct((n, MLSTM_V_WIDTH), BF16),
        scratch_shapes=[pltpu.VMEM((8, qk_width), F32), pltpu.VMEM((8 + L, qk_width), F32),
                        pltpu.VMEM((MLSTM_HEADS, MLSTM_V_DIM + MLSTM_ONES_ROWS, LANES), F32),
                        pltpu.VMEM((MLSTM_HEADS, 8, LANES), F32)],
        compiler_params=_params(("parallel", "arbitrary"), 48),
    )(mqk, mvt, mot, gcol, grow, conv_w, gain_lanes)


def _outproj_kernel(fo_ref, mo_ref, w_ref, x_ref, g_ref, b_ref, o_ref):
    mix = _dot(fo_ref[...], w_ref[0:FOX_WIDTH, :]) + _dot(mo_ref[...], w_ref[FOX_WIDTH:, :])
    o_ref[...] = _layer_norm(DEEPNORM_ALPHA * x_ref[...] + mix, g_ref[...], b_ref[...])


def _outproj(fo, mo, w_out, x2d, g, b):
    n = x2d.shape[0]
    tm = WIDE_ROW_TILE
    row = lambda width: pl.BlockSpec((tm, width), lambda i: (i, 0))
    full = lambda a: pl.BlockSpec(a.shape, lambda i: (0, 0))
    return pl.pallas_call(
        _outproj_kernel,
        grid=(n // tm,),
        in_specs=[row(512), row(512), full(w_out), row(D_MODEL), full(g), full(b)],
        out_specs=row(D_MODEL),
        out_shape=jax.ShapeDtypeStruct((n, D_MODEL), F32),
        compiler_params=_params(("parallel",), 48),
    )(fo, mo, w_out, x2d, g, b)


def _memkv_kernel(mem_ref, wk_ref, wv_ref, k_ref, v_ref):
    mb = mem_ref[...].astype(BF16)
    k_ref[...] = (_dot(mb, wk_ref[...]) * (XATTN_HEAD_DIM ** -0.5)).astype(BF16)
    v_ref[...] = _dot(mb, wv_ref[...]).astype(BF16)


def _memkv(mem2d, wk, wv, n_mem):
    n = mem2d.shape[0]
    row = pl.BlockSpec((n_mem, D_MODEL), lambda i: (i, 0))
    full = lambda a: pl.BlockSpec(a.shape, lambda i: (0, 0))
    return pl.pallas_call(
        _memkv_kernel,
        grid=(n // n_mem,),
        in_specs=[row, full(wk), full(wv)],
        out_specs=(row, row),
        out_shape=(jax.ShapeDtypeStruct((n, D_MODEL), BF16),) * 2,
        compiler_params=_params(("parallel",), 32),
    )(mem2d, wk, wv)


def _xattn_kernel(x_ref, k_ref, v_ref, wq_ref, wo_ref, g_ref, b_ref, wr_ref, br_ref,
                  o_ref, ob_ref, lg_ref):
    half = x_ref.shape[0] // 2
    for r in range(2):
        rows = slice(r * half, (r + 1) * half)
        x = x_ref[rows, :]
        q = _dot(x.astype(BF16), wq_ref[...]).astype(BF16)
        outs = []
        for h in range(XATTN_HEADS):
            sl = slice(h * XATTN_HEAD_DIM, (h + 1) * XATTN_HEAD_DIM)
            s = _dot_nt(q[:, sl], k_ref[:, sl])
            p = jnp.exp(s - jnp.max(s, axis=-1, keepdims=True))
            l = jnp.sum(p, axis=-1, keepdims=True)
            outs.append((_dot(p.astype(BF16), v_ref[:, sl]) / l).astype(BF16))
        o = jnp.concatenate(outs, axis=1)
        xa = _dot(o, wo_ref[...])
        x2 = _layer_norm(DEEPNORM_ALPHA * x + xa, g_ref[...], b_ref[...])
        o_ref[rows, :] = x2
        packed = _pack_bf16_pairs(x2)
        for j in range(ob_ref.shape[0]):
            ob_ref[j, rows, :] = packed[:, j * LANES:(j + 1) * LANES]
        x2h = x2.astype(BF16)
        x2l = (x2 - x2h.astype(F32)).astype(BF16)
        a = _dot_nt(wr_ref[...], x2h)
        b = _dot_nt(wr_ref[0:N_EXPERTS, :], x2l)
        lg_ref[:, rows] = a[0:N_EXPERTS] + a[N_EXPERTS:] + b + br_ref[...]


def _xattn(x1, kmem, vmem, wq, wo, g, b, wr, br, batch, seq, n_mem):
    n = x1.shape[0]
    tm = WIDE_ROW_TILE
    nt = seq // tm
    row = lambda width: pl.BlockSpec((tm, width), lambda bi, i: (bi * nt + i, 0))
    full = lambda a: pl.BlockSpec(a.shape, lambda bi, i: (0, 0))
    kv = pl.BlockSpec((n_mem, D_MODEL), lambda bi, i: (bi, 0))
    return pl.pallas_call(
        _xattn_kernel,
        grid=(batch, nt),
        in_specs=[row(D_MODEL), kv, kv, full(wq), full(wo), full(g), full(b), full(wr), full(br)],
        out_specs=(row(D_MODEL),
                   pl.BlockSpec((X_PLANES, tm, LANES), lambda bi, i: (0, bi * nt + i, 0)),
                   pl.BlockSpec((N_EXPERTS, tm), lambda bi, i: (0, bi * nt + i))),
        out_shape=(jax.ShapeDtypeStruct((n, D_MODEL), F32),
                   jax.ShapeDtypeStruct((X_PLANES, n, LANES), jnp.uint32),
                   jax.ShapeDtypeStruct((N_EXPERTS, n), F32)),
        compiler_params=_params(("parallel", "parallel"), 60),
    )(x1, kmem, vmem, wq, wo, g, b, wr, br)


def _route_kernel(lg_ref, idx_ref, rank_ref, gate_ref, cnt_ref, carry_ref):
    i = pl.program_id(0)

    @pl.when(i == 0)
    def _():
        carry_ref[...] = jnp.zeros_like(carry_ref)

    lg = lg_ref[...]
    t = lg.shape[1]
    e_idx = lax.broadcasted_iota(jnp.int32, lg.shape, 0).astype(F32)
    sels, vals, idxs = [], [], []
    for _ in range(TOP_K):
        mx = jnp.max(lg, axis=0, keepdims=True)
        first = jnp.min(jnp.where(lg == mx, e_idx, float(N_EXPERTS)), axis=0, keepdims=True)
        sel = e_idx == first
        sels.append(sel)
        vals.append(mx)
        idxs.append(first)
        lg = jnp.where(sel, -jnp.inf, lg)
    exps = [jnp.exp(v - vals[0]) for v in vals]
    tot = exps[0] + exps[1] + exps[2] + exps[3]

    selmat = (sels[0] | sels[1] | sels[2] | sels[3])
    r = lax.broadcasted_iota(jnp.int32, (t, t), 0)
    s = lax.broadcasted_iota(jnp.int32, (t, t), 1)
    earlier = (r < s).astype(BF16)
    carry = carry_ref[:, 0:1]
    rankmat = _dot(selmat.astype(BF16), earlier) + carry
    new_carry = carry + jnp.sum(selmat.astype(F32), axis=1, keepdims=True)
    carry_ref[...] = jnp.broadcast_to(new_carry, carry_ref.shape)
    cnt_ref[...] = jnp.broadcast_to(new_carry, cnt_ref.shape).astype(jnp.int32)

    row8 = lax.broadcasted_iota(jnp.int32, (8, t), 0)
    row128 = lax.broadcasted_iota(jnp.int32, (LANES, t), 0)
    idx_out = jnp.zeros((8, t), F32)
    rank_out = jnp.zeros((8, t), F32)
    gate_out = jnp.zeros((LANES, t), F32)
    for k in range(TOP_K):
        rk = jnp.sum(jnp.where(sels[k], rankmat, 0.0), axis=0, keepdims=True)
        idx_out = jnp.where(row8 == k, idxs[k], idx_out)
        rank_out = jnp.where(row8 == k, rk, rank_out)
        gate_out = jnp.where(row128 == k, exps[k] / tot, gate_out)
    idx_ref[...] = idx_out.astype(jnp.int32)
    rank_ref[...] = rank_out.astype(jnp.int32)
    gate_ref[...] = gate_out.T


def _route(logits_t):
    n = logits_t.shape[1]
    t = ROUTE_TILE
    col = lambda rows: pl.BlockSpec((rows, t), lambda i: (0, i))
    return pl.pallas_call(
        _route_kernel,
        grid=(n // t,),
        in_specs=[col(N_EXPERTS)],
        out_specs=(col(8), col(8), pl.BlockSpec((t, LANES), lambda i: (i, 0)),
                   pl.BlockSpec((N_EXPERTS, LANES), lambda i: (0, 0))),
        out_shape=(jax.ShapeDtypeStruct((8, n), jnp.int32),
                   jax.ShapeDtypeStruct((8, n), jnp.int32),
                   jax.ShapeDtypeStruct((n, LANES), F32),
                   jax.ShapeDtypeStruct((N_EXPERTS, LANES), jnp.int32)),
        scratch_shapes=[pltpu.VMEM((N_EXPERTS, LANES), F32)],
        compiler_params=_params(("arbitrary",), 32),
    )(logits_t)


def _expert_kernel(blk_e_ref, blk_rows_ref, next_e_ref, x_ref, wgu_hbm, bgu_ref, wd_hbm, bd_ref,
                   y_ref, wgu_f32, wd_f32, wgu_sc, wd_sc, state_ref, sem):
    i = pl.program_id(0)
    e = blk_e_ref[i]
    rows = blk_rows_ref[i]
    g = y_ref.shape[1]

    def weight_copies(expert, slot):
        return (pltpu.make_async_copy(wgu_hbm.at[expert], wgu_f32.at[slot], sem.at[0, slot]),
                pltpu.make_async_copy(wd_hbm.at[expert], wd_f32.at[slot], sem.at[1, slot]))

    @pl.when(i == 0)
    def _():
        state_ref[0] = -1
        state_ref[1] = 0

    @pl.when((rows > 0) & (e != state_ref[0]))
    def _():
        slot = state_ref[1]

        @pl.when(i == 0)
        def _():
            for cp in weight_copies(e, slot):
                cp.start()

        for cp in weight_copies(e, slot):
            cp.wait()
        wgu_sc[...] = wgu_f32[slot].astype(BF16)
        wd_sc[...] = wd_f32[slot].astype(BF16)
        nxt = next_e_ref[i]

        @pl.when(nxt >= 0)
        def _():
            for cp in weight_copies(nxt, 1 - slot):
                cp.start()

        state_ref[0] = e
        state_ref[1] = 1 - slot

    def ffn(m):
        packed = jnp.concatenate([x_ref[j, 0:m, :] for j in range(X_PLANES)], axis=1)
        xb = _unpack_bf16_pairs(packed)
        hids = []
        for c in range(D_EXPERT // EXPERT_CHUNK):
            g0 = c * EXPERT_CHUNK
            l0 = D_EXPERT + g0
            gate = _dot(xb, wgu_sc[:, g0:g0 + EXPERT_CHUNK]) + bgu_ref[0, :, g0:g0 + EXPERT_CHUNK]
            lin = _dot(xb, wgu_sc[:, l0:l0 + EXPERT_CHUNK]) + bgu_ref[0, :, l0:l0 + EXPERT_CHUNK]
            gate = jnp.minimum(gate, SWIGLU_LIMIT)
            lin = jnp.clip(lin, -SWIGLU_LIMIT, SWIGLU_LIMIT)
            hids.append((gate * jax.nn.sigmoid(SWIGLU_ALPHA * gate) * (lin + 1.0)).astype(BF16))
        y = _pack_bf16_pairs(_dot(jnp.concatenate(hids, axis=1), wd_sc[...]) + bd_ref[0])
        for j in range(Y_PLANES):
            y_ref[j, 0:m, :] = y[:, j * LANES:(j + 1) * LANES]

    @pl.when(rows > g // 2)
    def _():
        ffn(g)

    @pl.when((rows > 0) & (rows <= g // 2))
    def _():
        ffn(g // 2)
        y_ref[:, g // 2:, :] = jnp.zeros((Y_PLANES, g // 2, LANES), y_ref.dtype)

    @pl.when(rows == 0)
    def _():
        y_ref[...] = jnp.zeros_like(y_ref)


def _experts(blk_e, blk_rows, next_e, xs, w_gu, b_gu, w_d, b_d):
    p = xs.shape[1]
    g = EXPERT_ROWS
    grid_spec = pltpu.PrefetchScalarGridSpec(
        num_scalar_prefetch=3,
        grid=(p // g,),
        in_specs=[pl.BlockSpec((X_PLANES, g, LANES), lambda i, be, br, ne: (0, i, 0)),
                  pl.BlockSpec(memory_space=pl.ANY),
                  pl.BlockSpec((1, 1, 2 * D_EXPERT), lambda i, be, br, ne: (be[i], 0, 0)),
                  pl.BlockSpec(memory_space=pl.ANY),
                  pl.BlockSpec((1, 1, D_MODEL), lambda i, be, br, ne: (be[i], 0, 0))],
        out_specs=pl.BlockSpec((Y_PLANES, g, LANES), lambda i, be, br, ne: (0, i, 0)),
        scratch_shapes=[pltpu.VMEM((2, D_MODEL, 2 * D_EXPERT), F32),
                        pltpu.VMEM((2, D_EXPERT, D_MODEL), F32),
                        pltpu.VMEM((D_MODEL, 2 * D_EXPERT), BF16),
                        pltpu.VMEM((D_EXPERT, D_MODEL), BF16),
                        pltpu.SMEM((2,), jnp.int32),
                        pltpu.SemaphoreType.DMA((2, 2))],
    )
    return pl.pallas_call(
        _expert_kernel,
        grid_spec=grid_spec,
        out_shape=jax.ShapeDtypeStruct((Y_PLANES, p, LANES), jnp.uint32),
        compiler_params=_params(("arbitrary",), 56),
    )(blk_e, blk_rows, next_e, xs, w_gu, b_gu, w_d, b_d)


def _combine_kernel(y_ref, gate_ref, x_ref, g_ref, b_ref, o_ref):
    gate = gate_ref[...]
    ff = None
    for k in range(TOP_K):
        packed = jnp.concatenate([y_ref[k, j] for j in range(Y_PLANES)], axis=1)
        yk = _unpack_bf16_pairs(packed, F32) * gate[:, k:k + 1]
        ff = yk if ff is None else ff + yk
    o_ref[...] = _layer_norm(DEEPNORM_ALPHA * x_ref[...] + ff, g_ref[...], b_ref[...])


def _combine(yg, gate, x2, g, b):
    n = x2.shape[0]
    tm = WIDE_ROW_TILE
    row = lambda width: pl.BlockSpec((tm, width), lambda i: (i, 0))
    full = lambda a: pl.BlockSpec(a.shape, lambda i: (0, 0))
    return pl.pallas_call(
        _combine_kernel,
        grid=(n // tm,),
        in_specs=[pl.BlockSpec((TOP_K, Y_PLANES, tm, LANES), lambda i: (0, 0, i, 0)),
                  row(LANES), row(D_MODEL), full(g), full(b)],
        out_specs=row(D_MODEL),
        out_shape=jax.ShapeDtypeStruct((n, D_MODEL), F32),
        compiler_params=_params(("parallel",), 56),
    )(yg, gate, x2, g, b)


SC_WINDOW = 128
SC_GATHERS_IN_FLIGHT = 2


def _sc_mesh():
    return plsc.VectorSubcoreMesh(core_axis_name="core", subcore_axis_name="subcore")


def _sc_gather(table, idx):
    m = idx.shape[0]
    n_fly = SC_GATHERS_IN_FLIGHT

    @pl.kernel(out_type=jax.ShapeDtypeStruct((m, LANES), table.dtype), mesh=_sc_mesh(),
               scratch_types=[pltpu.SemaphoreType.DMA])
    def gather_kernel(table_hbm, idx_hbm, out_hbm, sem):
        def body(*refs):
            idx_vmem, out_vmem = refs[:n_fly], refs[n_fly]
            copies = [pltpu.async_copy(table_hbm.at[iv.at[0]],
                                       out_vmem.at[pl.ds(u * SC_WINDOW, SC_WINDOW)], sem)
                      for u, iv in enumerate(idx_vmem)]
            for cp in copies:
                cp.wait()

        pltpu.emit_pipeline(
            body,
            grid=(m // (n_fly * SC_WINDOW),),
            in_specs=[pl.BlockSpec((1, SC_WINDOW), lambda i, u=u: (0, n_fly * i + u))
                      for u in range(n_fly)],
            out_specs=[pl.BlockSpec((n_fly * SC_WINDOW, LANES), lambda i: (i, 0))],
            core_axis_name=("core", "subcore"),
            dimension_semantics=(pltpu.PARALLEL,),
        )(*([idx_hbm] * n_fly), out_hbm)

    return gather_kernel(table, idx.reshape(1, m))


def _sc_scatter(src, idx_lists, out_rows):
    m = src.shape[0]
    n_lists = len(idx_lists)

    @pl.kernel(out_type=jax.ShapeDtypeStruct((out_rows, LANES), src.dtype), mesh=_sc_mesh(),
               scratch_types=[pltpu.SemaphoreType.DMA])
    def scatter_kernel(src_hbm, *refs):
        idx_hbm, out_hbm, sem = refs[:n_lists], refs[n_lists], refs[n_lists + 1]

        def body(src_vmem, *idx_vmem):
            copies = [pltpu.async_copy(src_vmem, out_hbm.at[iv.at[0]], sem) for iv in idx_vmem]
            for cp in copies:
                cp.wait()

        pltpu.emit_pipeline(
            body,
            grid=(m // SC_WINDOW,),
            in_specs=[pl.BlockSpec((SC_WINDOW, LANES), lambda i: (i, 0))]
            + [pl.BlockSpec((1, SC_WINDOW), lambda i: (0, i))] * n_lists,
            out_specs=[],
            core_axis_name=("core", "subcore"),
            dimension_semantics=(pltpu.PARALLEL,),
        )(src_hbm, *idx_hbm)

    return scatter_kernel(src, *[ix.reshape(1, m) for ix in idx_lists])


def _layer(x, mem, w_in, fox_f_bias, conv_w, i_bias, f_bias, fox_g, mlstm_g, w_mix_out,
           ln1_g, ln1_b, w_xq, w_xk, w_xv, w_xo, ln2_g, ln2_b, w_router, b_router,
           w_gate_up, b_gate_up, w_down, b_down, ln3_g, ln3_b):
    batch, seq, d = x.shape
    n_mem = mem.shape[1]
    n = batch * seq
    x2d = x.reshape(n, d)

    o_ff = 3 * FOX_WIDTH
    o_mqk = o_ff + FOX_HEADS
    o_mv = o_mqk + 2 * MLSTM_QK_WIDTH
    o_mi = o_mv + MLSTM_V_WIDTH
    o_mf = o_mi + MLSTM_HEADS
    o_mo = o_mf + MLSTM_HEADS
    n_gate = FOX_HEADS + 2 * MLSTM_HEADS
    w_r = jnp.concatenate(
        [w_in[:, FOX_WIDTH:2 * FOX_WIDTH], w_in[:, o_mqk:o_mv],
         w_in[:, o_ff:o_mqk], w_in[:, o_mi:o_mo],
         jnp.zeros((d, LANES - n_gate), w_in.dtype)], axis=1).astype(BF16)
    w_t = jnp.concatenate([w_in[:, :FOX_WIDTH], w_in[:, 2 * FOX_WIDTH:o_ff],
                           w_in[:, o_mv:o_mi], w_in[:, o_mo:]], axis=1).T.astype(BF16)
    gate_bias = jnp.concatenate(
        [fox_f_bias, i_bias, f_bias, jnp.zeros((LANES - n_gate,), F32)]).reshape(1, LANES)

    fqt, fk, fvt, mqk, mvt, mot, gates = _inproj(x2d, w_r, w_t, gate_bias)
    gcol, grow, cpieces = _gateprep(gates, batch, seq)
    fo = _fox(fqt, fk, cpieces, fvt, fox_g.reshape(1, FOX_WIDTH), batch, seq)
    gain_lanes = jnp.broadcast_to(mlstm_g[:, None], (MLSTM_V_WIDTH, LANES))
    mo_out = _mlstm(mqk, mvt, mot, gcol, grow, conv_w, gain_lanes, batch, seq)
    x1 = _outproj(fo, mo_out, w_mix_out.astype(BF16), x2d, ln1_g.reshape(1, d), ln1_b.reshape(1, d))

    kmem, vmem = _memkv(mem.reshape(batch * n_mem, d), w_xk.astype(BF16), w_xv.astype(BF16), n_mem)
    wrt = w_router.T
    wrt_hi = wrt.astype(BF16)
    wrt_lo = (wrt - wrt_hi.astype(F32)).astype(BF16)
    x2, x2p, logits_t = _xattn(x1, kmem, vmem, w_xq.astype(BF16), w_xo.astype(BF16),
                               ln2_g.reshape(1, d), ln2_b.reshape(1, d),
                               jnp.concatenate([wrt_hi, wrt_lo], axis=0),
                               b_router.reshape(N_EXPERTS, 1), batch, seq, n_mem)

    idx_t, rank_t, gate, cnt = _route(logits_t)
    counts = cnt[:, 0]
    g_rows = EXPERT_ROWS
    padded = ((counts + g_rows - 1) // g_rows) * g_rows
    pad_end = jnp.cumsum(padded)
    pad_start = pad_end - padded
    experts = jnp.arange(N_EXPERTS, dtype=jnp.int32)
    sel = idx_t[:TOP_K, :, None] == experts[None, None, :]
    pos_t = jnp.sum(jnp.where(sel, pad_start[None, None, :], 0), axis=-1) + rank_t[:TOP_K]
    p_rows = n * TOP_K + N_EXPERTS * g_rows
    nb = p_rows // g_rows
    blk_start = jnp.arange(nb, dtype=jnp.int32) * g_rows
    blk_e = jnp.minimum(jnp.sum((pad_end[None, :] <= blk_start[:, None]).astype(jnp.int32), axis=1),
                        N_EXPERTS - 1)
    own = blk_e[:, None] == experts[None, :]
    row_end = jnp.sum(jnp.where(own, (pad_start + counts)[None, :], 0), axis=1)
    blk_rows = jnp.clip(row_end - blk_start, 0, g_rows).astype(jnp.int32)
    later = (experts[None, :] > blk_e[:, None]) & (counts[None, :] > 0)
    next_e = jnp.min(jnp.where(later, experts[None, :], N_EXPERTS), axis=1)
    next_e = jnp.where(next_e == N_EXPERTS, -1, next_e).astype(jnp.int32)

    def piece_index(planes):
        off = jnp.arange(planes, dtype=jnp.int32) * p_rows
        return pos_t[:, None, :] + off[None, :, None]

    x_idx = piece_index(X_PLANES).reshape(TOP_K, X_PLANES * n)
    xs = _sc_scatter(x2p.reshape(X_PLANES * n, LANES), [x_idx[k] for k in range(TOP_K)],
                     X_PLANES * p_rows)
    y = _experts(blk_e, blk_rows, next_e, xs.reshape(X_PLANES, p_rows, LANES), w_gate_up,
                 b_gate_up.reshape(N_EXPERTS, 1, -1), w_down, b_down.reshape(N_EXPERTS, 1, -1))
    yg = _sc_gather(y.reshape(Y_PLANES * p_rows, LANES), piece_index(Y_PLANES).reshape(-1))
    out = _combine(yg.reshape(TOP_K, Y_PLANES, n, LANES), gate, x2,
                   ln3_g.reshape(1, d), ln3_b.reshape(1, d))
    return out.reshape(batch, seq, d)


def kernel(x, mem, w_in, fox_f_bias, mlstm_conv_w, mlstm_i_bias, mlstm_f_bias, fox_norm_g, mlstm_norm_g, w_mix_out, ln1_g, ln1_b, w_xq, w_xk, w_xv, w_xo, ln2_g, ln2_b, w_router, b_router, w_gate_up, b_gate_up, w_down, b_down, ln3_g, ln3_b):
    for l in range(w_in.shape[0]):
        x = _layer(x, mem, w_in[l], fox_f_bias[l], mlstm_conv_w[l], mlstm_i_bias[l],
                   mlstm_f_bias[l], fox_norm_g[l], mlstm_norm_g[l], w_mix_out[l],
                   ln1_g[l], ln1_b[l], w_xq[l], w_xk[l], w_xv[l], w_xo[l], ln2_g[l], ln2_b[l],
                   w_router[l], b_router[l], w_gate_up[l], b_gate_up[l], w_down[l], b_down[l],
                   ln3_g[l], ln3_b[l])
    return x
```

```python
import jax
import jax.numpy as jnp
from jax import lax
from jax.experimental import pallas as pl
from jax.experimental.pallas import tpu as pltpu
from jax.experimental.pallas import tpu_sc as plsc

F32 = jnp.float32
BF16 = jnp.bfloat16

D_MODEL = 1024
FOX_HEADS = 8
FOX_HEAD_DIM = 64
FOX_WIDTH = FOX_HEADS * FOX_HEAD_DIM
MLSTM_HEADS = 4
MLSTM_QK_DIM = 64
MLSTM_V_DIM = 128
MLSTM_QK_WIDTH = MLSTM_HEADS * MLSTM_QK_DIM
MLSTM_V_WIDTH = MLSTM_HEADS * MLSTM_V_DIM
CONV_WIDTH = 4
XATTN_HEADS = 4
XATTN_HEAD_DIM = D_MODEL // XATTN_HEADS
N_EXPERTS = 32
TOP_K = 4
D_EXPERT = D_MODEL
SWIGLU_LIMIT = 7.0
SWIGLU_ALPHA = 1.702
DEEPNORM_ALPHA = 2.0 ** 0.25
LN_EPS = 1e-5
RMS_EPS = 1e-6

LANES = 128
SEQ_BLOCK = 256
ROW_TILE = 512
WIDE_ROW_TILE = 1024
EXPERT_ROWS = 512
EXPERT_CHUNK = 256
ROUTE_TILE = 512
X_PLANES = D_MODEL // 2 // LANES
Y_PLANES = X_PLANES
GATE_I0 = FOX_HEADS
GATE_F0 = FOX_HEADS + MLSTM_HEADS

MIB = 1024 * 1024


def _params(semantics, vmem_mib):
    return pltpu.CompilerParams(dimension_semantics=semantics,
                                vmem_limit_bytes=vmem_mib * MIB)


def _layer_norm(y, g, b):
    mu = jnp.mean(y, axis=-1, keepdims=True)
    yc = y - mu
    var = jnp.mean(yc * yc, axis=-1, keepdims=True)
    return yc * lax.rsqrt(var + LN_EPS) * g + b


def _dot(a, b):
    return jnp.dot(a, b, preferred_element_type=F32)


def _dot_nt(a, b):
    return lax.dot_general(a, b, (((1,), (1,)), ((), ())), preferred_element_type=F32)


def _dot_tn(a, b):
    return lax.dot_general(a, b, (((0,), (0,)), ((), ())), preferred_element_type=F32)


def _pack_bf16_pairs(x):
    w = x.shape[1] // 2
    lo = pltpu.bitcast(x[:, :w].astype(BF16).astype(F32), jnp.uint32)
    hi = pltpu.bitcast(x[:, w:].astype(BF16).astype(F32), jnp.uint32)
    return (lo >> 16) | hi


def _unpack_bf16_pairs(u, dtype=BF16):
    lo = pltpu.bitcast(u << 16, F32).astype(dtype)
    hi = pltpu.bitcast(u & jnp.uint32(0xFFFF0000), F32).astype(dtype)
    return jnp.concatenate([lo, hi], axis=1)


def _inproj_kernel(x_ref, w_ref, wt_ref, gb_ref, fqt_ref, fk_ref, fvt_ref, mqk_ref, mvt_ref,
                   mot_ref, g_ref):
    xb = x_ref[...].astype(BF16)

    def mm(c0, width):
        return _dot(xb, w_ref[:, c0:c0 + width])

    def mm_t(r0, height):
        return _dot_nt(wt_ref[r0:r0 + height, :], xb)

    r_fv = FOX_WIDTH
    r_mv = r_fv + FOX_WIDTH
    r_mo = r_mv + MLSTM_V_WIDTH
    fqt_ref[...] = (mm_t(0, FOX_WIDTH) * (FOX_HEAD_DIM ** -0.5)).astype(BF16)
    fvt_ref[...] = mm_t(r_fv, FOX_WIDTH).astype(BF16)
    mvt_ref[...] = mm_t(r_mv, MLSTM_V_WIDTH).astype(BF16)
    mot_ref[...] = mm_t(r_mo, MLSTM_V_WIDTH)
    c_qk = FOX_WIDTH
    c_g = c_qk + 2 * MLSTM_QK_WIDTH
    fk_ref[...] = mm(0, FOX_WIDTH).astype(BF16)
    mqk_ref[...] = mm(c_qk, 2 * MLSTM_QK_WIDTH)
    g_ref[...] = mm(c_g, LANES) + gb_ref[...]


def _inproj(x2d, w_r, w_t, gate_bias):
    n = x2d.shape[0]
    tm = WIDE_ROW_TILE
    row = lambda width: pl.BlockSpec((tm, width), lambda i: (i, 0))
    col = lambda height: pl.BlockSpec((height, tm), lambda i: (0, i))
    full = lambda a: pl.BlockSpec(a.shape, lambda i: (0, 0))
    out_shapes = (
        jax.ShapeDtypeStruct((FOX_WIDTH, n), BF16),
        jax.ShapeDtypeStruct((n, FOX_WIDTH), BF16),
        jax.ShapeDtypeStruct((FOX_WIDTH, n), BF16),
        jax.ShapeDtypeStruct((n, 2 * MLSTM_QK_WIDTH), F32),
        jax.ShapeDtypeStruct((MLSTM_V_WIDTH, n), BF16),
        jax.ShapeDtypeStruct((MLSTM_V_WIDTH, n), F32),
        jax.ShapeDtypeStruct((n, LANES), F32),
    )
    return pl.pallas_call(
        _inproj_kernel,
        grid=(n // tm,),
        in_specs=[row(D_MODEL), full(w_r), full(w_t), full(gate_bias)],
        out_specs=(col(FOX_WIDTH), row(FOX_WIDTH), col(FOX_WIDTH), row(2 * MLSTM_QK_WIDTH),
                   col(MLSTM_V_WIDTH), col(MLSTM_V_WIDTH), row(LANES)),
        out_shape=out_shapes,
        compiler_params=_params(("parallel",), 48),
    )(x2d, w_r, w_t, gate_bias)


def _split3(x):
    hi = x.astype(BF16)
    r1 = x - hi.astype(F32)
    mid = r1.astype(BF16)
    lo = (r1 - mid.astype(F32)).astype(BF16)
    return hi, mid, lo


def _gateprep_kernel(g_ref, sel_ref, col_ref, row_ref, cp_ref, carry_ref):
    c = pl.program_id(1)

    @pl.when(c == 0)
    def _():
        carry_ref[...] = jnp.zeros_like(carry_ref)

    g = g_ref[...]
    lane = lax.broadcasted_iota(jnp.int32, g.shape, 1)
    is_i = (lane >= GATE_I0) & (lane < GATE_F0)
    logsig = jnp.minimum(g, 0.0) - jnp.log1p(jnp.exp(-jnp.abs(g)))
    blk = g.shape[0]
    r = lax.broadcasted_iota(jnp.int32, (blk, blk), 0)
    s = lax.broadcasted_iota(jnp.int32, (blk, blk), 1)
    tri = (s <= r).astype(BF16)
    cs = None
    for piece in _split3(logsig):
        term = _dot(tri, piece)
        cs = term if cs is None else cs + term
    carry = carry_ref[0:1, :]
    glob = cs + carry
    carry_ref[...] = jnp.broadcast_to(glob[blk - 1:blk, :], carry_ref.shape)
    out = jnp.where(lane < GATE_I0, glob, jnp.where(is_i, g, cs))
    col_ref[...] = out
    row_ref[0] = out.T[0:16, :]

    pieces = jnp.concatenate(_split3(-glob), axis=1)
    moved = _dot(pieces, sel_ref[...])
    for p in range(FOX_HEADS // 2):
        cp_ref[0, p] = moved[:, p * LANES:(p + 1) * LANES].astype(BF16)


def _piece_selector():
    src = jnp.arange(3 * LANES, dtype=jnp.int32)[:, None]
    dst = jnp.arange(4 * LANES, dtype=jnp.int32)[None, :]
    piece, head = src // LANES, src % LANES
    pair, lane = dst // LANES, dst % LANES
    hit = (lane < 6) & (lane % 3 == piece) & (head == 2 * pair + lane // 3)
    return hit.astype(BF16)


def _gateprep(gates, batch, seq):
    n = gates.shape[0]
    nc = seq // SEQ_BLOCK
    sel = _piece_selector()
    return pl.pallas_call(
        _gateprep_kernel,
        grid=(batch, nc),
        in_specs=[pl.BlockSpec((SEQ_BLOCK, LANES), lambda b, c: (b * nc + c, 0)),
                  pl.BlockSpec(sel.shape, lambda b, c: (0, 0))],
        out_specs=(pl.BlockSpec((SEQ_BLOCK, LANES), lambda b, c: (b * nc + c, 0)),
                   pl.BlockSpec((1, 16, SEQ_BLOCK), lambda b, c: (b, 0, c)),
                   pl.BlockSpec((1, FOX_HEADS // 2, SEQ_BLOCK, LANES), lambda b, c: (b, 0, c, 0))),
        out_shape=(jax.ShapeDtypeStruct((n, LANES), F32),
                   jax.ShapeDtypeStruct((batch, 16, seq), F32),
                   jax.ShapeDtypeStruct((batch, FOX_HEADS // 2, seq, LANES), BF16)),
        scratch_shapes=[pltpu.VMEM((8, LANES), F32)],
        compiler_params=_params(("parallel", "arbitrary"), 32),
    )(gates, sel)


FOX_ONES_ROWS = 16
FOX_VAUG_ROWS = FOX_HEAD_DIM + FOX_ONES_ROWS
FOX_QUERY_TILE = 2 * SEQ_BLOCK


def _fox_kernel(qt_ref, k_ref, cp_ref, vt_ref, gain_ref, o_ref,
                kaug_sc, vaug_sc, qaug_sc, sta_sc, stb_sc, stc_sc, m_sc, acc0_sc, acc1_sc):
    qi = pl.program_id(2)
    tq = o_ref.shape[0]
    tg = tq
    tk = SEQ_BLOCK
    hd = FOX_HEAD_DIM
    seq = k_ref.shape[0]
    nq = seq // tq
    par = qi % 2

    def load_queries(tile, slot):
        q0 = pl.multiple_of(tile * tq, tq)
        for h in range(2):
            qaug_sc[slot, h, 0:hd, :] = qt_ref[h * hd:(h + 1) * hd, pl.ds(q0, tq)]

    @pl.when(qi == 0)
    def _():
        ones3 = (lax.broadcasted_iota(jnp.int32, (FOX_ONES_ROWS, tq), 0) < 3).astype(BF16)
        for slot in range(2):
            for h in range(2):
                qaug_sc[slot, h, hd:hd + FOX_ONES_ROWS, :] = ones3
                qaug_sc[slot, h, hd + FOX_ONES_ROWS:, :] = jnp.zeros(
                    (LANES - hd - FOX_ONES_ROWS, tq), BF16)
        load_queries(0, 0)

    nxt = jnp.minimum(qi + 1, nq - 1)
    load_queries(nxt, 1 - par)

    @pl.when(qi == 0)
    def _():
        lane = lax.broadcasted_iota(jnp.int32, (tk, LANES), 1)

        def build(blk, carry):
            r0 = pl.multiple_of(blk * tk, tk)
            kp = k_ref[pl.ds(r0, tk), :].astype(F32)
            cp = cp_ref[0, 0, pl.ds(r0, tk), :].astype(F32)
            for h in range(2):
                kh = kp if h == 0 else pltpu.roll(kp, hd, axis=1)
                ch = pltpu.roll(cp, hd - 3 * h, axis=1)
                kaug = jnp.where(lane < hd, kh, jnp.where(lane < hd + 3, ch, 0.0))
                kaug_sc[h, pl.ds(r0, tk), :] = kaug.astype(BF16)
            return carry

        lax.fori_loop(0, seq // tk, build, 0)
        for h in range(2):
            vaug_sc[h, 0:hd, :] = vt_ref[h * hd:(h + 1) * hd, :]
            vaug_sc[h, hd:, :] = jnp.ones((FOX_ONES_ROWS, seq), BF16)

    def put_scores(kg, slot, masked=False, queries=None):
        k0 = pl.multiple_of(kg * tg, tg)
        queries = par if queries is None else queries
        for h in range(2):
            st = _dot(kaug_sc[h, pl.ds(k0, tg), :], qaug_sc[queries, h])
            if masked:
                kk = lax.broadcasted_iota(jnp.int32, (tg, tq), 0)
                tt = lax.broadcasted_iota(jnp.int32, (tg, tq), 1)
                st = jnp.where(kk <= tt, st, -jnp.inf)
            slot[h] = st

    m_sc[...] = jnp.full(m_sc.shape, -jnp.inf, F32)
    acc0_sc[...] = jnp.zeros(acc0_sc.shape, F32)
    acc1_sc[...] = jnp.zeros(acc1_sc.shape, F32)
    acc = (acc0_sc, acc1_sc)

    def absorb(kg, slot):
        k0 = pl.multiple_of(kg * tg, tg)
        half = tg // 2
        for h in range(2):
            st = slot[h]
            m_prev = m_sc[h]
            m_new = jnp.maximum(m_prev, jnp.max(st, axis=0, keepdims=True))
            pv = None
            for u in range(2):
                p = jnp.exp(st[u * half:(u + 1) * half] - m_new)
                part = _dot(vaug_sc[h, :, pl.ds(k0 + u * half, half)], p.astype(BF16))
                pv = part if pv is None else pv + part
            acc[h][...] = jnp.exp(m_prev - m_new) * acc[h][...] + pv
            m_sc[h] = m_new

    sa, sb, sc = sta_sc, stb_sc, stc_sc
    n_loop = jnp.maximum(qi - 1, 0) // 2

    def body(j, carry):
        put_scores(2 * j + 1, sb)
        absorb(2 * j, sa)
        put_scores(2 * j + 2, sa)
        absorb(2 * j + 1, sb)
        return carry

    lax.fori_loop(0, n_loop, body, 0)
    done = 2 * n_loop

    def put_next():
        put_scores(0, sa, queries=1 - par)

    @pl.when(qi == 0)
    def _():
        put_scores(qi, sc, masked=True)
        put_next()
        absorb(qi, sc)

    @pl.when((qi > 0) & (qi - done == 1))
    def _():
        put_scores(qi, sc, masked=True)
        absorb(done, sa)
        put_next()
        absorb(qi, sc)

    @pl.when((qi > 0) & (qi - done == 2))
    def _():
        put_scores(done + 1, sb)
        absorb(done, sa)
        put_scores(qi, sc, masked=True)
        absorb(done + 1, sb)
        put_next()
        absorb(qi, sc)

    normed = []
    for h in range(2):
        a = acc[h][...]
        oh = a[0:hd] * (1.0 / a[hd:hd + 1])
        ms = jnp.mean(oh * oh, axis=0, keepdims=True)
        gain = gain_ref[h * hd:(h + 1) * hd, :]
        normed.append(oh * lax.rsqrt(ms + RMS_EPS) * jnp.concatenate([gain] * (tq // LANES), axis=1))
    o_ref[...] = jnp.concatenate(normed, axis=0).T.astype(o_ref.dtype)


def _fox(fqt, fk, cpieces, fvt, fox_gain, batch, seq):
    n = fk.shape[0]
    tq = FOX_QUERY_TILE
    nq = seq // tq
    npair = FOX_HEADS // 2
    return pl.pallas_call(
        _fox_kernel,
        grid=(batch, npair, nq),
        in_specs=[pl.BlockSpec((LANES, seq), lambda b, hp, qi: (hp, b)),
                  pl.BlockSpec((seq, LANES), lambda b, hp, qi: (b, hp)),
                  pl.BlockSpec((1, 1, seq, LANES), lambda b, hp, qi: (b, hp, 0, 0)),
                  pl.BlockSpec((LANES, seq), lambda b, hp, qi: (hp, b)),
                  pl.BlockSpec((LANES, LANES), lambda b, hp, qi: (hp, 0))],
        out_specs=pl.BlockSpec((tq, LANES), lambda b, hp, qi: (b * nq + qi, hp)),
        out_shape=jax.ShapeDtypeStruct((n, FOX_WIDTH), BF16),
        scratch_shapes=[pltpu.VMEM((2, seq, LANES), BF16),
                        pltpu.VMEM((2, FOX_VAUG_ROWS, seq), BF16),
                        pltpu.VMEM((2, 2, LANES, tq), BF16),
                        pltpu.VMEM((2, tq, tq), F32),
                        pltpu.VMEM((2, tq, tq), F32),
                        pltpu.VMEM((2, tq, tq), F32),
                        pltpu.VMEM((2, 1, tq), F32),
                        pltpu.VMEM((FOX_VAUG_ROWS, tq), F32),
                        pltpu.VMEM((FOX_VAUG_ROWS, tq), F32)],
        compiler_params=_params(("parallel", "parallel", "arbitrary"), 48),
    )(fqt, fk, cpieces, fvt, fox_gain)


MLSTM_ONES_ROWS = 16


def _mlstm_kernel(qk_ref, vt_ref, ogt_ref, col_ref, row_ref, cw_ref, gain_ref, o_ref,
                  tail_ref, buf_ref, c_sc, m_sc):
    c = pl.program_id(1)
    L = SEQ_BLOCK

    @pl.when(c == 0)
    def _():
        tail_ref[...] = jnp.zeros_like(tail_ref)
        c_sc[...] = jnp.zeros_like(c_sc)
        m_sc[...] = jnp.zeros_like(m_sc)

    x = qk_ref[...]
    buf_ref[0:8, :] = tail_ref[...]
    buf_ref[8:8 + L, :] = x
    tail_ref[...] = x[L - 8:L, :]
    y = x * cw_ref[CONV_WIDTH - 1:CONV_WIDTH, :]
    for j in range(CONV_WIDTH - 1):
        shift = CONV_WIDTH - 1 - j
        y = y + buf_ref[8 - shift:8 - shift + L, :] * cw_ref[j:j + 1, :]
    y = y * jax.nn.sigmoid(y)
    kf = y[:, MLSTM_QK_WIDTH:] * (MLSTM_QK_DIM ** -0.5)
    n_pair = MLSTM_HEADS // 2
    qt = [y[:, p * LANES:(p + 1) * LANES].T for p in range(n_pair)]
    kb = [kf[:, p * LANES:(p + 1) * LANES].astype(BF16) for p in range(n_pair)]

    col = col_ref[...]
    row = row_ref[0]
    lane = lax.broadcasted_iota(jnp.int32, (L, LANES), 1)
    sub = lax.broadcasted_iota(jnp.int32, (LANES, L), 0)
    ss = lax.broadcasted_iota(jnp.int32, (L, L), 0)
    ll = lax.broadcasted_iota(jnp.int32, (L, L), 1)
    causal = ss <= ll
    ones_rows = jnp.ones((MLSTM_ONES_ROWS, L), BF16)

    for h in range(MLSTM_HEADS):
        pair, half = divmod(h, 2)
        head_sub = (sub < MLSTM_QK_DIM) if half == 0 else (sub >= MLSTM_QK_DIM)
        head_lane = (lane < MLSTM_QK_DIM) if half == 0 else (lane >= MLSTM_QK_DIM)
        qth = jnp.where(head_sub, qt[pair], 0.0).astype(BF16)
        rcol = col[:, GATE_I0 + h:GATE_I0 + h + 1] - col[:, GATE_F0 + h:GATE_F0 + h + 1]
        brow = row[GATE_F0 + h:GATE_F0 + h + 1, :]
        lirow = row[GATE_I0 + h:GATE_I0 + h + 1, :]
        g = brow[:, L - 1:L]
        m_prev = m_sc[h][0:1, 0:1]

        dt = jnp.where(causal, rcol + brow, -jnp.inf)
        inter_log = brow + m_prev
        m_t = jnp.maximum(inter_log, jnp.max(dt, axis=0, keepdims=True))
        w_inter = jnp.exp(inter_log - m_t)
        pt = jnp.exp(dt - m_t) * _dot(kb[pair], qth)
        vaug = jnp.concatenate([vt_ref[h * LANES:(h + 1) * LANES, :], ones_rows], axis=0)
        cstate = c_sc[h]
        tot = w_inter * _dot(cstate.astype(BF16), qth) + _dot(vaug, pt.astype(BF16))
        den = tot[MLSTM_V_DIM:MLSTM_V_DIM + 1]
        hout = tot[0:MLSTM_V_DIM] * (1.0 / jnp.maximum(jnp.abs(den), jnp.exp(-m_t)))

        a = g + (lirow - brow)
        m_loc = jnp.max(a, axis=1, keepdims=True)
        vw = (vaug.astype(F32) * jnp.exp(a - m_loc)).astype(BF16)
        kmask = jnp.where(head_lane, kb[pair], jnp.zeros_like(kb[pair]))
        kv = _dot(vw, kmask)
        m_new = jnp.maximum(g + m_prev, m_loc)
        c_sc[h] = jnp.exp(g + m_prev - m_new) * cstate + jnp.exp(m_loc - m_new) * kv
        m_sc[h] = jnp.broadcast_to(m_new, m_sc.shape[1:])

        ms = jnp.mean(hout * hout, axis=0, keepdims=True)
        gain = gain_ref[h * LANES:(h + 1) * LANES, :]
        hn = hout * lax.rsqrt(ms + RMS_EPS) * jnp.concatenate([gain] * (L // LANES), axis=1)
        gate = jax.nn.sigmoid(ogt_ref[h * LANES:(h + 1) * LANES, :])
        o_ref[:, h * LANES:(h + 1) * LANES] = (hn * gate).T.astype(o_ref.dtype)


def _mlstm(mqk, mvt, mot, gcol, grow, conv_w, gain_lanes, batch, seq):
    n = mqk.shape[0]
    L = SEQ_BLOCK
    nc = seq // L
    qk_width = 2 * MLSTM_QK_WIDTH
    row = lambda width: pl.BlockSpec((L, width), lambda b, c: (b * nc + c, 0))
    col = pl.BlockSpec((MLSTM_V_WIDTH, L), lambda b, c: (0, b * nc + c))
    full = lambda a: pl.BlockSpec(a.shape, lambda b, c: (0, 0))
    return pl.pallas_call(
        _mlstm_kernel,
        grid=(batch, nc),
        in_specs=[row(qk_width), col, col, row(LANES),
                  pl.BlockSpec((1, 16, L), lambda b, c: (b, 0, c)),
                  full(conv_w), full(gain_lanes)],
        out_specs=row(MLSTM_V_WIDTH),
        out_shape=jax.ShapeDtypeStruct((n, MLSTM_V_WIDTH), BF16),
        scratch_shapes=[pltpu.VMEM((8, qk_width), F32), pltpu.VMEM((8 + L, qk_width), F32),
                        pltpu.VMEM((MLSTM_HEADS, MLSTM_V_DIM + MLSTM_ONES_ROWS, LANES), F32),
                        pltpu.VMEM((MLSTM_HEADS, 8, LANES), F32)],
        compiler_params=_params(("parallel", "arbitrary"), 48),
    )(mqk, mvt, mot, gcol, grow, conv_w, gain_lanes)


def _outproj_kernel(fo_ref, mo_ref, w_ref, x_ref, g_ref, b_ref, o_ref):
    mix = _dot(fo_ref[...], w_ref[0:FOX_WIDTH, :]) + _dot(mo_ref[...], w_ref[FOX_WIDTH:, :])
    o_ref[...] = _layer_norm(DEEPNORM_ALPHA * x_ref[...] + mix, g_ref[...], b_ref[...])


def _outproj(fo, mo, w_out, x2d, g, b):
    n = x2d.shape[0]
    tm = WIDE_ROW_TILE
    row = lambda width: pl.BlockSpec((tm, width), lambda i: (i, 0))
    full = lambda a: pl.BlockSpec(a.shape, lambda i: (0, 0))
    return pl.pallas_call(
        _outproj_kernel,
        grid=(n // tm,),
        in_specs=[row(512), row(512), full(w_out), row(D_MODEL), full(g), full(b)],
        out_specs=row(D_MODEL),
        out_shape=jax.ShapeDtypeStruct((n, D_MODEL), F32),
        compiler_params=_params(("parallel",), 48),
    )(fo, mo, w_out, x2d, g, b)


def _memkv_kernel(mem_ref, wk_ref, wv_ref, k_ref, v_ref):
    mb = mem_ref[...].astype(BF16)
    k_ref[...] = (_dot(mb, wk_ref[...]) * (XATTN_HEAD_DIM ** -0.5)).astype(BF16)
    v_ref[...] = _dot(mb, wv_ref[...]).astype(BF16)


def _memkv(mem2d, wk, wv, n_mem):
    n = mem2d.shape[0]
    row = pl.BlockSpec((n_mem, D_MODEL), lambda i: (i, 0))
    full = lambda a: pl.BlockSpec(a.shape, lambda i: (0, 0))
    return pl.pallas_call(
        _memkv_kernel,
        grid=(n // n_mem,),
        in_specs=[row, full(wk), full(wv)],
        out_specs=(row, row),
        out_shape=(jax.ShapeDtypeStruct((n, D_MODEL), BF16),) * 2,
        compiler_params=_params(("parallel",), 32),
    )(mem2d, wk, wv)


def _xattn_kernel(x_ref, k_ref, v_ref, wq_ref, wo_ref, g_ref, b_ref, wr_ref, br_ref,
                  o_ref, ob_ref, lg_ref):
    half = x_ref.shape[0] // 2
    for r in range(2):
        rows = slice(r * half, (r + 1) * half)
        x = x_ref[rows, :]
        q = _dot(x.astype(BF16), wq_ref[...]).astype(BF16)
        outs = []
        for h in range(XATTN_HEADS):
            sl = slice(h * XATTN_HEAD_DIM, (h + 1) * XATTN_HEAD_DIM)
            s = _dot_nt(q[:, sl], k_ref[:, sl])
            p = jnp.exp(s - jnp.max(s, axis=-1, keepdims=True))
            l = jnp.sum(p, axis=-1, keepdims=True)
            outs.append((_dot(p.astype(BF16), v_ref[:, sl]) / l).astype(BF16))
        o = jnp.concatenate(outs, axis=1)
        xa = _dot(o, wo_ref[...])
        x2 = _layer_norm(DEEPNORM_ALPHA * x + xa, g_ref[...], b_ref[...])
        o_ref[rows, :] = x2
        packed = _pack_bf16_pairs(x2)
        for j in range(ob_ref.shape[0]):
            ob_ref[j, rows, :] = packed[:, j * LANES:(j + 1) * LANES]
        x2h = x2.astype(BF16)
        x2l = (x2 - x2h.astype(F32)).astype(BF16)
        a = _dot_nt(wr_ref[...], x2h)
        b = _dot_nt(wr_ref[0:N_EXPERTS, :], x2l)
        lg_ref[:, rows] = a[0:N_EXPERTS] + a[N_EXPERTS:] + b + br_ref[...]


def _xattn(x1, kmem, vmem, wq, wo, g, b, wr, br, batch, seq, n_mem):
    n = x1.shape[0]
    tm = WIDE_ROW_TILE
    nt = seq // tm
    row = lambda width: pl.BlockSpec((tm, width), lambda bi, i: (bi * nt + i, 0))
    full = lambda a: pl.BlockSpec(a.shape, lambda bi, i: (0, 0))
    kv = pl.BlockSpec((n_mem, D_MODEL), lambda bi, i: (bi, 0))
    return pl.pallas_call(
        _xattn_kernel,
        grid=(batch, nt),
        in_specs=[row(D_MODEL), kv, kv, full(wq), full(wo), full(g), full(b), full(wr), full(br)],
        out_specs=(row(D_MODEL),
                   pl.BlockSpec((X_PLANES, tm, LANES), lambda bi, i: (0, bi * nt + i, 0)),
                   pl.BlockSpec((N_EXPERTS, tm), lambda bi, i: (0, bi * nt + i))),
        out_shape=(jax.ShapeDtypeStruct((n, D_MODEL), F32),
                   jax.ShapeDtypeStruct((X_PLANES, n, LANES), jnp.uint32),
                   jax.ShapeDtypeStruct((N_EXPERTS, n), F32)),
        compiler_params=_params(("parallel", "parallel"), 60),
    )(x1, kmem, vmem, wq, wo, g, b, wr, br)


def _route_kernel(lg_ref, idx_ref, rank_ref, gate_ref, cnt_ref, carry_ref):
    i = pl.program_id(0)

    @pl.when(i == 0)
    def _():
        carry_ref[...] = jnp.zeros_like(carry_ref)

    lg = lg_ref[...]
    t = lg.shape[1]
    e_idx = lax.broadcasted_iota(jnp.int32, lg.shape, 0).astype(F32)
    sels, vals, idxs = [], [], []
    for _ in range(TOP_K):
        mx = jnp.max(lg, axis=0, keepdims=True)
        first = jnp.min(jnp.where(lg == mx, e_idx, float(N_EXPERTS)), axis=0, keepdims=True)
        sel = e_idx == first
        sels.append(sel)
        vals.append(mx)
        idxs.append(first)
        lg = jnp.where(sel, -jnp.inf, lg)
    exps = [jnp.exp(v - vals[0]) for v in vals]
    tot = exps[0] + exps[1] + exps[2] + exps[3]

    selmat = (sels[0] | sels[1] | sels[2] | sels[3])
    r = lax.broadcasted_iota(jnp.int32, (t, t), 0)
    s = lax.broadcasted_iota(jnp.int32, (t, t), 1)
    earlier = (r < s).astype(BF16)
    carry = carry_ref[:, 0:1]
    rankmat = _dot(selmat.astype(BF16), earlier) + carry
    new_carry = carry + jnp.sum(selmat.astype(F32), axis=1, keepdims=True)
    carry_ref[...] = jnp.broadcast_to(new_carry, carry_ref.shape)
    cnt_ref[...] = jnp.broadcast_to(new_carry, cnt_ref.shape).astype(jnp.int32)

    row8 = lax.broadcasted_iota(jnp.int32, (8, t), 0)
    row128 = lax.broadcasted_iota(jnp.int32, (LANES, t), 0)
    idx_out = jnp.zeros((8, t), F32)
    rank_out = jnp.zeros((8, t), F32)
    gate_out = jnp.zeros((LANES, t), F32)
    for k in range(TOP_K):
        rk = jnp.sum(jnp.where(sels[k], rankmat, 0.0), axis=0, keepdims=True)
        idx_out = jnp.where(row8 == k, idxs[k], idx_out)
        rank_out = jnp.where(row8 == k, rk, rank_out)
        gate_out = jnp.where(row128 == k, exps[k] / tot, gate_out)
    idx_ref[...] = idx_out.astype(jnp.int32)
    rank_ref[...] = rank_out.astype(jnp.int32)
    gate_ref[...] = gate_out.T


def _route(logits_t):
    n = logits_t.shape[1]
    t = ROUTE_TILE
    col = lambda rows: pl.BlockSpec((rows, t), lambda i: (0, i))
    return pl.pallas_call(
        _route_kernel,
        grid=(n // t,),
        in_specs=[col(N_EXPERTS)],
        out_specs=(col(8), col(8), pl.BlockSpec((t, LANES), lambda i: (i, 0)),
                   pl.BlockSpec((N_EXPERTS, LANES), lambda i: (0, 0))),
        out_shape=(jax.ShapeDtypeStruct((8, n), jnp.int32),
                   jax.ShapeDtypeStruct((8, n), jnp.int32),
                   jax.ShapeDtypeStruct((n, LANES), F32),
                   jax.ShapeDtypeStruct((N_EXPERTS, LANES), jnp.int32)),
        scratch_shapes=[pltpu.VMEM((N_EXPERTS, LANES), F32)],
        compiler_params=_params(("arbitrary",), 32),
    )(logits_t)


def _expert_kernel(blk_e_ref, blk_rows_ref, next_e_ref, x_ref, wgu_hbm, bgu_ref, wd_hbm, bd_ref,
                   y_ref, wgu_f32, wd_f32, wgu_sc, wd_sc, state_ref, sem):
    i = pl.program_id(0)
    e = blk_e_ref[i]
    rows = blk_rows_ref[i]
    g = y_ref.shape[1]

    def weight_copies(expert, slot):
        return (pltpu.make_async_copy(wgu_hbm.at[expert], wgu_f32.at[slot], sem.at[0, slot]),
                pltpu.make_async_copy(wd_hbm.at[expert], wd_f32.at[slot], sem.at[1, slot]))

    @pl.when(i == 0)
    def _():
        state_ref[0] = -1
        state_ref[1] = 0

    @pl.when((rows > 0) & (e != state_ref[0]))
    def _():
        slot = state_ref[1]

        @pl.when(i == 0)
        def _():
            for cp in weight_copies(e, slot):
                cp.start()

        for cp in weight_copies(e, slot):
            cp.wait()
        wgu_sc[...] = wgu_f32[slot].astype(BF16)
        wd_sc[...] = wd_f32[slot].astype(BF16)
        nxt = next_e_ref[i]

        @pl.when(nxt >= 0)
        def _():
            for cp in weight_copies(nxt, 1 - slot):
                cp.start()

        state_ref[0] = e
        state_ref[1] = 1 - slot

    def ffn(m):
        packed = jnp.concatenate([x_ref[j, 0:m, :] for j in range(X_PLANES)], axis=1)
        xb = _unpack_bf16_pairs(packed)
        hids = []
        for c in range(D_EXPERT // EXPERT_CHUNK):
            g0 = c * EXPERT_CHUNK
            l0 = D_EXPERT + g0
            gate = _dot(xb, wgu_sc[:, g0:g0 + EXPERT_CHUNK]) + bgu_ref[0, :, g0:g0 + EXPERT_CHUNK]
            lin = _dot(xb, wgu_sc[:, l0:l0 + EXPERT_CHUNK]) + bgu_ref[0, :, l0:l0 + EXPERT_CHUNK]
            gate = jnp.minimum(gate, SWIGLU_LIMIT)
            lin = jnp.clip(lin, -SWIGLU_LIMIT, SWIGLU_LIMIT)
            hids.append((gate * jax.nn.sigmoid(SWIGLU_ALPHA * gate) * (lin + 1.0)).astype(BF16))
        y = _pack_bf16_pairs(_dot(jnp.concatenate(hids, axis=1), wd_sc[...]) + bd_ref[0])
        for j in range(Y_PLANES):
            y_ref[j, 0:m, :] = y[:, j * LANES:(j + 1) * LANES]

    @pl.when(rows > g // 2)
    def _():
        ffn(g)

    @pl.when((rows > 0) & (rows <= g // 2))
    def _():
        ffn(g // 2)
        y_ref[:, g // 2:, :] = jnp.zeros((Y_PLANES, g // 2, LANES), y_ref.dtype)

    @pl.when(rows == 0)
    def _():
        y_ref[...] = jnp.zeros_like(y_ref)


def _experts(blk_e, blk_rows, next_e, xs, w_gu, b_gu, w_d, b_d):
    p = xs.shape[1]
    g = EXPERT_ROWS
    grid_spec = pltpu.PrefetchScalarGridSpec(
        num_scalar_prefetch=3,
        grid=(p // g,),
        in_specs=[pl.BlockSpec((X_PLANES, g, LANES), lambda i, be, br, ne: (0, i, 0)),
                  pl.BlockSpec(memory_space=pl.ANY),
                  pl.BlockSpec((1, 1, 2 * D_EXPERT), lambda i, be, br, ne: (be[i], 0, 0)),
                  pl.BlockSpec(memory_space=pl.ANY),
                  pl.BlockSpec((1, 1, D_MODEL), lambda i, be, br, ne: (be[i], 0, 0))],
        out_specs=pl.BlockSpec((Y_PLANES, g, LANES), lambda i, be, br, ne: (0, i, 0)),
        scratch_shapes=[pltpu.VMEM((2, D_MODEL, 2 * D_EXPERT), F32),
                        pltpu.VMEM((2, D_EXPERT, D_MODEL), F32),
                        pltpu.VMEM((D_MODEL, 2 * D_EXPERT), BF16),
                        pltpu.VMEM((D_EXPERT, D_MODEL), BF16),
                        pltpu.SMEM((2,), jnp.int32),
                        pltpu.SemaphoreType.DMA((2, 2))],
    )
    return pl.pallas_call(
        _expert_kernel,
        grid_spec=grid_spec,
        out_shape=jax.ShapeDtypeStruct((Y_PLANES, p, LANES), jnp.uint32),
        compiler_params=_params(("arbitrary",), 56),
    )(blk_e, blk_rows, next_e, xs, w_gu, b_gu, w_d, b_d)


def _combine_kernel(y_ref, gate_ref, x_ref, g_ref, b_ref, o_ref):
    gate = gate_ref[...]
    ff = None
    for k in range(TOP_K):
        packed = jnp.concatenate([y_ref[k, j] for j in range(Y_PLANES)], axis=1)
        yk = _unpack_bf16_pairs(packed, F32) * gate[:, k:k + 1]
        ff = yk if ff is None else ff + yk
    o_ref[...] = _layer_norm(DEEPNORM_ALPHA * x_ref[...] + ff, g_ref[...], b_ref[...])


def _combine(yg, gate, x2, g, b):
    n = x2.shape[0]
    tm = WIDE_ROW_TILE
    row = lambda width: pl.BlockSpec((tm, width), lambda i: (i, 0))
    full = lambda a: pl.BlockSpec(a.shape, lambda i: (0, 0))
    return pl.pallas_call(
        _combine_kernel,
        grid=(n // tm,),
        in_specs=[pl.BlockSpec((TOP_K, Y_PLANES, tm, LANES), lambda i: (0, 0, i, 0)),
                  row(LANES), row(D_MODEL), full(g), full(b)],
        out_specs=row(D_MODEL),
        out_shape=jax.ShapeDtypeStruct((n, D_MODEL), F32),
        compiler_params=_params(("parallel",), 56),
    )(yg, gate, x2, g, b)


SC_WINDOW = 128
SC_GATHERS_IN_FLIGHT = 2


def _sc_mesh():
    return plsc.VectorSubcoreMesh(core_axis_name="core", subcore_axis_name="subcore")


def _sc_gather(table, idx):
    m = idx.shape[0]
    n_fly = SC_GATHERS_IN_FLIGHT

    @pl.kernel(out_type=jax.ShapeDtypeStruct((m, LANES), table.dtype), mesh=_sc_mesh(),
               scratch_types=[pltpu.SemaphoreType.DMA])
    def gather_kernel(table_hbm, idx_hbm, out_hbm, sem):
        def body(*refs):
            idx_vmem, out_vmem = refs[:n_fly], refs[n_fly]
            copies = [pltpu.async_copy(table_hbm.at[iv.at[0]],
                                       out_vmem.at[pl.ds(u * SC_WINDOW, SC_WINDOW)], sem)
                      for u, iv in enumerate(idx_vmem)]
            for cp in copies:
                cp.wait()

        pltpu.emit_pipeline(
            body,
            grid=(m // (n_fly * SC_WINDOW),),
            in_specs=[pl.BlockSpec((1, SC_WINDOW), lambda i, u=u: (0, n_fly * i + u))
                      for u in range(n_fly)],
            out_specs=[pl.BlockSpec((n_fly * SC_WINDOW, LANES), lambda i: (i, 0))],
            core_axis_name=("core", "subcore"),
            dimension_semantics=(pltpu.PARALLEL,),
        )(*([idx_hbm] * n_fly), out_hbm)

    return gather_kernel(table, idx.reshape(1, m))


def _sc_scatter(src, idx_lists, out_rows):
    m = src.shape[0]
    n_lists = len(idx_lists)

    @pl.kernel(out_type=jax.ShapeDtypeStruct((out_rows, LANES), src.dtype), mesh=_sc_mesh(),
               scratch_types=[pltpu.SemaphoreType.DMA])
    def scatter_kernel(src_hbm, *refs):
        idx_hbm, out_hbm, sem = refs[:n_lists], refs[n_lists], refs[n_lists + 1]

        def body(src_vmem, *idx_vmem):
            copies = [pltpu.async_copy(src_vmem, out_hbm.at[iv.at[0]], sem) for iv in idx_vmem]
            for cp in copies:
                cp.wait()

        pltpu.emit_pipeline(
            body,
            grid=(m // SC_WINDOW,),
            in_specs=[pl.BlockSpec((SC_WINDOW, LANES), lambda i: (i, 0))]
            + [pl.BlockSpec((1, SC_WINDOW), lambda i: (0, i))] * n_lists,
            out_specs=[],
            core_axis_name=("core", "subcore"),
            dimension_semantics=(pltpu.PARALLEL,),
        )(src_hbm, *idx_hbm)

    return scatter_kernel(src, *[ix.reshape(1, m) for ix in idx_lists])


def _layer(x, mem, w_in, fox_f_bias, conv_w, i_bias, f_bias, fox_g, mlstm_g, w_mix_out,
           ln1_g, ln1_b, w_xq, w_xk, w_xv, w_xo, ln2_g, ln2_b, w_router, b_router,
           w_gate_up, b_gate_up, w_down, b_down, ln3_g, ln3_b):
    batch, seq, d = x.shape
    n_mem = mem.shape[1]
    n = batch * seq
    x2d = x.reshape(n, d)

    o_ff = 3 * FOX_WIDTH
    o_mqk = o_ff + FOX_HEADS
    o_mv = o_mqk + 2 * MLSTM_QK_WIDTH
    o_mi = o_mv + MLSTM_V_WIDTH
    o_mf = o_mi + MLSTM_HEADS
    o_mo = o_mf + MLSTM_HEADS
    n_gate = FOX_HEADS + 2 * MLSTM_HEADS
    w_r = jnp.concatenate(
        [w_in[:, FOX_WIDTH:2 * FOX_WIDTH], w_in[:, o_mqk:o_mv],
         w_in[:, o_ff:o_mqk], w_in[:, o_mi:o_mo],
         jnp.zeros((d, LANES - n_gate), w_in.dtype)], axis=1).astype(BF16)
    w_t = jnp.concatenate([w_in[:, :FOX_WIDTH], w_in[:, 2 * FOX_WIDTH:o_ff],
                           w_in[:, o_mv:o_mi], w_in[:, o_mo:]], axis=1).T.astype(BF16)
    gate_bias = jnp.concatenate(
        [fox_f_bias, i_bias, f_bias, jnp.zeros((LANES - n_gate,), F32)]).reshape(1, LANES)

    fqt, fk, fvt, mqk, mvt, mot, gates = _inproj(x2d, w_r, w_t, gate_bias)
    gcol, grow, cpieces = _gateprep(gates, batch, seq)
    fox_gain_lanes = jnp.broadcast_to(fox_g[:, None], (FOX_WIDTH, LANES))
    fo = _fox(fqt, fk, cpieces, fvt, fox_gain_lanes, batch, seq)
    gain_lanes = jnp.broadcast_to(mlstm_g[:, None], (MLSTM_V_WIDTH, LANES))
    mo_out = _mlstm(mqk, mvt, mot, gcol, grow, conv_w, gain_lanes, batch, seq)
    x1 = _outproj(fo, mo_out, w_mix_out.astype(BF16), x2d, ln1_g.reshape(1, d), ln1_b.reshape(1, d))

    kmem, vmem = _memkv(mem.reshape(batch * n_mem, d), w_xk.astype(BF16), w_xv.astype(BF16), n_mem)
    wrt = w_router.T
    wrt_hi = wrt.astype(BF16)
    wrt_lo = (wrt - wrt_hi.astype(F32)).astype(BF16)
    x2, x2p, logits_t = _xattn(x1, kmem, vmem, w_xq.astype(BF16), w_xo.astype(BF16),
                               ln2_g.reshape(1, d), ln2_b.reshape(1, d),
                               jnp.concatenate([wrt_hi, wrt_lo], axis=0),
                               b_router.reshape(N_EXPERTS, 1), batch, seq, n_mem)

    idx_t, rank_t, gate, cnt = _route(logits_t)
    counts = cnt[:, 0]
    g_rows = EXPERT_ROWS
    padded = ((counts + g_rows - 1) // g_rows) * g_rows
    pad_end = jnp.cumsum(padded)
    pad_start = pad_end - padded
    experts = jnp.arange(N_EXPERTS, dtype=jnp.int32)
    sel = idx_t[:TOP_K, :, None] == experts[None, None, :]
    pos_t = jnp.sum(jnp.where(sel, pad_start[None, None, :], 0), axis=-1) + rank_t[:TOP_K]
    p_rows = n * TOP_K + N_EXPERTS * g_rows
    nb = p_rows // g_rows
    blk_start = jnp.arange(nb, dtype=jnp.int32) * g_rows
    blk_e = jnp.minimum(jnp.sum((pad_end[None, :] <= blk_start[:, None]).astype(jnp.int32), axis=1),
                        N_EXPERTS - 1)
    own = blk_e[:, None] == experts[None, :]
    row_end = jnp.sum(jnp.where(own, (pad_start + counts)[None, :], 0), axis=1)
    blk_rows = jnp.clip(row_end - blk_start, 0, g_rows).astype(jnp.int32)
    later = (experts[None, :] > blk_e[:, None]) & (counts[None, :] > 0)
    next_e = jnp.min(jnp.where(later, experts[None, :], N_EXPERTS), axis=1)
    next_e = jnp.where(next_e == N_EXPERTS, -1, next_e).astype(jnp.int32)

    def piece_index(planes):
        off = jnp.arange(planes, dtype=jnp.int32) * p_rows
        return pos_t[:, None, :] + off[None, :, None]

    x_idx = piece_index(X_PLANES).reshape(TOP_K, X_PLANES * n)
    xs = _sc_scatter(x2p.reshape(X_PLANES * n, LANES), [x_idx[k] for k in range(TOP_K)],
                     X_PLANES * p_rows)
    y = _experts(blk_e, blk_rows, next_e, xs.reshape(X_PLANES, p_rows, LANES), w_gate_up,
                 b_gate_up.reshape(N_EXPERTS, 1, -1), w_down, b_down.reshape(N_EXPERTS, 1, -1))
    yg = _sc_gather(y.reshape(Y_PLANES * p_rows, LANES), piece_index(Y_PLANES).reshape(-1))
    out = _combine(yg.reshape(TOP_K, Y_PLANES, n, LANES), gate, x2,
                   ln3_g.reshape(1, d), ln3_b.reshape(1, d))
    return out.reshape(batch, seq, d)


def kernel(x, mem, w_in, fox_f_bias, mlstm_conv_w, mlstm_i_bias, mlstm_f_bias, fox_norm_g, mlstm_norm_g, w_mix_out, ln1_g, ln1_b, w_xq, w_xk, w_xv, w_xo, ln2_g, ln2_b, w_router, b_router, w_gate_up, b_gate_up, w_down, b_down, ln3_g, ln3_b):
    for l in range(w_in.shape[0]):
        x = _layer(x, mem, w_in[l], fox_f_bias[l], mlstm_conv_w[l], mlstm_i_bias[l],
                   mlstm_f_bias[l], fox_norm_g[l], mlstm_norm_g[l], w_mix_out[l],
                   ln1_g[l], ln1_b[l], w_xq[l], w_xk[l], w_xv[l], w_xo[l], ln2_g[l], ln2_b[l],
                   w_router[l], b_router[l], w_gate_up[l], b_gate_up[l], w_down[l], b_down[l],
                   ln3_g[l], ln3_b[l])
    return x
```

```python
import jax
import jax.numpy as jnp
from jax import lax
from jax.experimental import pallas as pl
from jax.experimental.pallas import tpu as pltpu
from jax.experimental.pallas import tpu_sc as plsc

F32 = jnp.float32
BF16 = jnp.bfloat16

D_MODEL = 1024
FOX_HEADS = 8
FOX_HEAD_DIM = 64
FOX_WIDTH = FOX_HEADS * FOX_HEAD_DIM
MLSTM_HEADS = 4
MLSTM_QK_DIM = 64
MLSTM_V_DIM = 128
MLSTM_QK_WIDTH = MLSTM_HEADS * MLSTM_QK_DIM
MLSTM_V_WIDTH = MLSTM_HEADS * MLSTM_V_DIM
CONV_WIDTH = 4
XATTN_HEADS = 4
XATTN_HEAD_DIM = D_MODEL // XATTN_HEADS
N_EXPERTS = 32
TOP_K = 4
D_EXPERT = D_MODEL
SWIGLU_LIMIT = 7.0
SWIGLU_ALPHA = 1.702
DEEPNORM_ALPHA = 2.0 ** 0.25
LN_EPS = 1e-5
RMS_EPS = 1e-6

LANES = 128
SEQ_BLOCK = 256
ROW_TILE = 1024
EXPERT_ROWS = 512
EXPERT_CHUNK = 256
ROUTE_TILE = 512
COMBINE_PARTS = 2
X_PLANES = D_MODEL // 2 // LANES
Y_PLANES = X_PLANES
GATE_I0 = FOX_HEADS
GATE_F0 = FOX_HEADS + MLSTM_HEADS

MIB = 1024 * 1024


def _params(semantics, vmem_mib):
    return pltpu.CompilerParams(dimension_semantics=semantics,
                                vmem_limit_bytes=vmem_mib * MIB)


def _layer_norm(y, g, b):
    mu = jnp.mean(y, axis=-1, keepdims=True)
    yc = y - mu
    var = jnp.mean(yc * yc, axis=-1, keepdims=True)
    return yc * lax.rsqrt(var + LN_EPS) * g + b


def _dot(a, b):
    return jnp.dot(a, b, preferred_element_type=F32)


def _dot_nt(a, b):
    return lax.dot_general(a, b, (((1,), (1,)), ((), ())), preferred_element_type=F32)


def _pack_bf16_pairs(x):
    w = x.shape[1] // 2
    lo = pltpu.bitcast(x[:, :w].astype(BF16).astype(F32), jnp.uint32)
    hi = pltpu.bitcast(x[:, w:].astype(BF16).astype(F32), jnp.uint32)
    return (lo >> 16) | hi


def _unpack_bf16_pairs(u, dtype=BF16):
    lo = pltpu.bitcast(u << 16, F32).astype(dtype)
    hi = pltpu.bitcast(u & jnp.uint32(0xFFFF0000), F32).astype(dtype)
    return jnp.concatenate([lo, hi], axis=1)


def _inproj_kernel(x_ref, w_ref, wt_ref, gb_ref, fqt_ref, fk_ref, fvt_ref, mqk_ref, mvt_ref,
                   mot_ref, g_ref):
    xb = x_ref[...].astype(BF16)

    def mm(c0, width):
        return _dot(xb, w_ref[:, c0:c0 + width])

    def mm_t(r0, height):
        return _dot_nt(wt_ref[r0:r0 + height, :], xb)

    r_fv = FOX_WIDTH
    r_mv = r_fv + FOX_WIDTH
    r_mo = r_mv + MLSTM_V_WIDTH
    fqt_ref[...] = (mm_t(0, FOX_WIDTH) * (FOX_HEAD_DIM ** -0.5)).astype(BF16)
    fvt_ref[...] = mm_t(r_fv, FOX_WIDTH).astype(BF16)
    mvt_ref[...] = mm_t(r_mv, MLSTM_V_WIDTH).astype(BF16)
    mot_ref[...] = mm_t(r_mo, MLSTM_V_WIDTH)
    c_qk = FOX_WIDTH
    c_g = c_qk + 2 * MLSTM_QK_WIDTH
    fk_ref[...] = mm(0, FOX_WIDTH).astype(BF16)
    mqk_ref[...] = mm(c_qk, 2 * MLSTM_QK_WIDTH)
    g_ref[...] = mm(c_g, LANES) + gb_ref[...]


def _inproj(x2d, w_r, w_t, gate_bias):
    n = x2d.shape[0]
    tm = ROW_TILE
    row = lambda width: pl.BlockSpec((tm, width), lambda i: (i, 0))
    col = lambda height: pl.BlockSpec((height, tm), lambda i: (0, i))
    full = lambda a: pl.BlockSpec(a.shape, lambda i: (0, 0))
    out_shapes = (
        jax.ShapeDtypeStruct((FOX_WIDTH, n), BF16),
        jax.ShapeDtypeStruct((n, FOX_WIDTH), BF16),
        jax.ShapeDtypeStruct((FOX_WIDTH, n), BF16),
        jax.ShapeDtypeStruct((n, 2 * MLSTM_QK_WIDTH), F32),
        jax.ShapeDtypeStruct((MLSTM_V_WIDTH, n), BF16),
        jax.ShapeDtypeStruct((MLSTM_V_WIDTH, n), F32),
        jax.ShapeDtypeStruct((n, LANES), F32),
    )
    return pl.pallas_call(
        _inproj_kernel,
        grid=(n // tm,),
        in_specs=[row(D_MODEL), full(w_r), full(w_t), full(gate_bias)],
        out_specs=(col(FOX_WIDTH), row(FOX_WIDTH), col(FOX_WIDTH), row(2 * MLSTM_QK_WIDTH),
                   col(MLSTM_V_WIDTH), col(MLSTM_V_WIDTH), row(LANES)),
        out_shape=out_shapes,
        compiler_params=_params(("parallel",), 48),
    )(x2d, w_r, w_t, gate_bias)


def _split3(x):
    hi = x.astype(BF16)
    r1 = x - hi.astype(F32)
    mid = r1.astype(BF16)
    lo = (r1 - mid.astype(F32)).astype(BF16)
    return hi, mid, lo


def _gateprep_kernel(g_ref, sel_ref, col_ref, row_ref, cp_ref, carry_ref):
    c = pl.program_id(1)

    @pl.when(c == 0)
    def _():
        carry_ref[...] = jnp.zeros_like(carry_ref)

    g = g_ref[...]
    lane = lax.broadcasted_iota(jnp.int32, g.shape, 1)
    is_i = (lane >= GATE_I0) & (lane < GATE_F0)
    logsig = jnp.minimum(g, 0.0) - jnp.log1p(jnp.exp(-jnp.abs(g)))
    blk = g.shape[0]
    r = lax.broadcasted_iota(jnp.int32, (blk, blk), 0)
    s = lax.broadcasted_iota(jnp.int32, (blk, blk), 1)
    tri = (s <= r).astype(BF16)
    cs = None
    for piece in _split3(logsig):
        term = _dot(tri, piece)
        cs = term if cs is None else cs + term
    carry = carry_ref[0:1, :]
    glob = cs + carry
    carry_ref[...] = jnp.broadcast_to(glob[blk - 1:blk, :], carry_ref.shape)
    out = jnp.where(lane < GATE_I0, glob, jnp.where(is_i, g, cs))
    col_ref[...] = out
    row_ref[0] = out.T[0:16, :]

    pieces = jnp.concatenate(_split3(-glob), axis=1)
    moved = _dot(pieces, sel_ref[...])
    for p in range(FOX_HEADS // 2):
        cp_ref[0, p] = moved[:, p * LANES:(p + 1) * LANES].astype(BF16)


def _piece_selector():
    src = jnp.arange(3 * LANES, dtype=jnp.int32)[:, None]
    dst = jnp.arange(4 * LANES, dtype=jnp.int32)[None, :]
    piece, head = src // LANES, src % LANES
    pair, lane = dst // LANES, dst % LANES
    hit = (lane < 6) & (lane % 3 == piece) & (head == 2 * pair + lane // 3)
    return hit.astype(BF16)


def _gateprep(gates, batch, seq):
    n = gates.shape[0]
    nc = seq // SEQ_BLOCK
    sel = _piece_selector()
    return pl.pallas_call(
        _gateprep_kernel,
        grid=(batch, nc),
        in_specs=[pl.BlockSpec((SEQ_BLOCK, LANES), lambda b, c: (b * nc + c, 0)),
                  pl.BlockSpec(sel.shape, lambda b, c: (0, 0))],
        out_specs=(pl.BlockSpec((SEQ_BLOCK, LANES), lambda b, c: (b * nc + c, 0)),
                   pl.BlockSpec((1, 16, SEQ_BLOCK), lambda b, c: (b, 0, c)),
                   pl.BlockSpec((1, FOX_HEADS // 2, SEQ_BLOCK, LANES), lambda b, c: (b, 0, c, 0))),
        out_shape=(jax.ShapeDtypeStruct((n, LANES), F32),
                   jax.ShapeDtypeStruct((batch, 16, seq), F32),
                   jax.ShapeDtypeStruct((batch, FOX_HEADS // 2, seq, LANES), BF16)),
        scratch_shapes=[pltpu.VMEM((8, LANES), F32)],
        compiler_params=_params(("parallel", "arbitrary"), 32),
    )(gates, sel)


FOX_ONES_ROWS = 16
FOX_VAUG_ROWS = FOX_HEAD_DIM + FOX_ONES_ROWS
FOX_QUERY_TILE = 2 * SEQ_BLOCK


def _fox_kernel(qt_ref, k_ref, cp_ref, vt_ref, gain_ref, o_ref,
                kaug_sc, vaug_sc, qaug_sc, sta_sc, stb_sc, stc_sc, m_sc, acc0_sc, acc1_sc):
    qi = pl.program_id(2)
    tq = o_ref.shape[0]
    tg = tq
    tk = SEQ_BLOCK
    hd = FOX_HEAD_DIM
    seq = k_ref.shape[0]
    nq = seq // tq
    par = qi % 2

    def load_queries(tile, slot):
        q0 = pl.multiple_of(tile * tq, tq)
        for h in range(2):
            qaug_sc[slot, h, 0:hd, :] = qt_ref[h * hd:(h + 1) * hd, pl.ds(q0, tq)]

    @pl.when(qi == 0)
    def _():
        ones3 = (lax.broadcasted_iota(jnp.int32, (FOX_ONES_ROWS, tq), 0) < 3).astype(BF16)
        for slot in range(2):
            for h in range(2):
                qaug_sc[slot, h, hd:hd + FOX_ONES_ROWS, :] = ones3
                qaug_sc[slot, h, hd + FOX_ONES_ROWS:, :] = jnp.zeros(
                    (LANES - hd - FOX_ONES_ROWS, tq), BF16)
        load_queries(0, 0)

    nxt = jnp.minimum(qi + 1, nq - 1)
    load_queries(nxt, 1 - par)

    @pl.when(qi == 0)
    def _():
        lane = lax.broadcasted_iota(jnp.int32, (tk, LANES), 1)

        def build(blk, carry):
            r0 = pl.multiple_of(blk * tk, tk)
            kp = k_ref[pl.ds(r0, tk), :].astype(F32)
            cp = cp_ref[0, 0, pl.ds(r0, tk), :].astype(F32)
            for h in range(2):
                kh = kp if h == 0 else pltpu.roll(kp, hd, axis=1)
                ch = pltpu.roll(cp, hd - 3 * h, axis=1)
                kaug = jnp.where(lane < hd, kh, jnp.where(lane < hd + 3, ch, 0.0))
                kaug_sc[h, pl.ds(r0, tk), :] = kaug.astype(BF16)
            return carry

        lax.fori_loop(0, seq // tk, build, 0)
        for h in range(2):
            vaug_sc[h, 0:hd, :] = vt_ref[h * hd:(h + 1) * hd, :]
            vaug_sc[h, hd:, :] = jnp.ones((FOX_ONES_ROWS, seq), BF16)

    def put_scores(kg, slot, masked=False, queries=None):
        k0 = pl.multiple_of(kg * tg, tg)
        queries = par if queries is None else queries
        for h in range(2):
            st = _dot(kaug_sc[h, pl.ds(k0, tg), :], qaug_sc[queries, h])
            if masked:
                kk = lax.broadcasted_iota(jnp.int32, (tg, tq), 0)
                tt = lax.broadcasted_iota(jnp.int32, (tg, tq), 1)
                st = jnp.where(kk <= tt, st, -jnp.inf)
            slot[h] = st

    m_sc[...] = jnp.full(m_sc.shape, -jnp.inf, F32)
    acc0_sc[...] = jnp.zeros(acc0_sc.shape, F32)
    acc1_sc[...] = jnp.zeros(acc1_sc.shape, F32)
    acc = (acc0_sc, acc1_sc)

    def absorb(kg, slot):
        k0 = pl.multiple_of(kg * tg, tg)
        half = tg // 2
        for h in range(2):
            st = slot[h]
            m_prev = m_sc[h]
            m_new = jnp.maximum(m_prev, jnp.max(st, axis=0, keepdims=True))
            pv = None
            for u in range(2):
                p = jnp.exp(st[u * half:(u + 1) * half] - m_new)
                part = _dot(vaug_sc[h, :, pl.ds(k0 + u * half, half)], p.astype(BF16))
                pv = part if pv is None else pv + part
            acc[h][...] = jnp.exp(m_prev - m_new) * acc[h][...] + pv
            m_sc[h] = m_new

    sa, sb, sc = sta_sc, stb_sc, stc_sc
    n_loop = jnp.maximum(qi - 1, 0) // 2

    def body(j, carry):
        put_scores(2 * j + 1, sb)
        absorb(2 * j, sa)
        put_scores(2 * j + 2, sa)
        absorb(2 * j + 1, sb)
        return carry

    lax.fori_loop(0, n_loop, body, 0)
    done = 2 * n_loop

    def put_next():
        put_scores(0, sa, queries=1 - par)

    @pl.when(qi == 0)
    def _():
        put_scores(qi, sc, masked=True)
        put_next()
        absorb(qi, sc)

    @pl.when((qi > 0) & (qi - done == 1))
    def _():
        put_scores(qi, sc, masked=True)
        absorb(done, sa)
        put_next()
        absorb(qi, sc)

    @pl.when((qi > 0) & (qi - done == 2))
    def _():
        put_scores(done + 1, sb)
        absorb(done, sa)
        put_scores(qi, sc, masked=True)
        absorb(done + 1, sb)
        put_next()
        absorb(qi, sc)

    normed = []
    for h in range(2):
        a = acc[h][...]
        oh = a[0:hd] * (1.0 / a[hd:hd + 1])
        ms = jnp.mean(oh * oh, axis=0, keepdims=True)
        gain = gain_ref[h * hd:(h + 1) * hd, :]
        normed.append(oh * lax.rsqrt(ms + RMS_EPS) * jnp.concatenate([gain] * (tq // LANES), axis=1))
    o_ref[...] = jnp.concatenate(normed, axis=0).T.astype(o_ref.dtype)


def _fox(fqt, fk, cpieces, fvt, fox_gain, batch, seq):
    n = fk.shape[0]
    tq = FOX_QUERY_TILE
    nq = seq // tq
    npair = FOX_HEADS // 2
    return pl.pallas_call(
        _fox_kernel,
        grid=(batch, npair, nq),
        in_specs=[pl.BlockSpec((LANES, seq), lambda b, hp, qi: (hp, b)),
                  pl.BlockSpec((seq, LANES), lambda b, hp, qi: (b, hp)),
                  pl.BlockSpec((1, 1, seq, LANES), lambda b, hp, qi: (b, hp, 0, 0)),
                  pl.BlockSpec((LANES, seq), lambda b, hp, qi: (hp, b)),
                  pl.BlockSpec((LANES, LANES), lambda b, hp, qi: (hp, 0))],
        out_specs=pl.BlockSpec((tq, LANES), lambda b, hp, qi: (b * nq + qi, hp)),
        out_shape=jax.ShapeDtypeStruct((n, FOX_WIDTH), BF16),
        scratch_shapes=[pltpu.VMEM((2, seq, LANES), BF16),
                        pltpu.VMEM((2, FOX_VAUG_ROWS, seq), BF16),
                        pltpu.VMEM((2, 2, LANES, tq), BF16),
                        pltpu.VMEM((2, tq, tq), F32),
                        pltpu.VMEM((2, tq, tq), F32),
                        pltpu.VMEM((2, tq, tq), F32),
                        pltpu.VMEM((2, 1, tq), F32),
                        pltpu.VMEM((FOX_VAUG_ROWS, tq), F32),
                        pltpu.VMEM((FOX_VAUG_ROWS, tq), F32)],
        compiler_params=_params(("parallel", "parallel", "arbitrary"), 48),
    )(fqt, fk, cpieces, fvt, fox_gain)


MLSTM_ONES_ROWS = 16


def _mlstm_kernel(qk_ref, vt_ref, ogt_ref, col_ref, row_ref, cw_ref, gain_ref, o_ref,
                  tail_ref, buf_ref, c_sc, m_sc):
    c = pl.program_id(1)
    L = SEQ_BLOCK

    @pl.when(c == 0)
    def _():
        tail_ref[...] = jnp.zeros_like(tail_ref)
        c_sc[...] = jnp.zeros_like(c_sc)
        m_sc[...] = jnp.zeros_like(m_sc)

    x = qk_ref[...]
    buf_ref[0:8, :] = tail_ref[...]
    buf_ref[8:8 + L, :] = x
    tail_ref[...] = x[L - 8:L, :]
    y = x * cw_ref[CONV_WIDTH - 1:CONV_WIDTH, :]
    for j in range(CONV_WIDTH - 1):
        shift = CONV_WIDTH - 1 - j
        y = y + buf_ref[8 - shift:8 - shift + L, :] * cw_ref[j:j + 1, :]
    y = y * jax.nn.sigmoid(y)
    kf = y[:, MLSTM_QK_WIDTH:] * (MLSTM_QK_DIM ** -0.5)
    n_pair = MLSTM_HEADS // 2
    qt = [y[:, p * LANES:(p + 1) * LANES].T for p in range(n_pair)]
    kb = [kf[:, p * LANES:(p + 1) * LANES].astype(BF16) for p in range(n_pair)]

    col = col_ref[...]
    row = row_ref[0]
    lane = lax.broadcasted_iota(jnp.int32, (L, LANES), 1)
    sub = lax.broadcasted_iota(jnp.int32, (LANES, L), 0)
    ss = lax.broadcasted_iota(jnp.int32, (L, L), 0)
    ll = lax.broadcasted_iota(jnp.int32, (L, L), 1)
    causal = ss <= ll
    ones_rows = jnp.ones((MLSTM_ONES_ROWS, L), BF16)

    for h in range(MLSTM_HEADS):
        pair, half = divmod(h, 2)
        head_sub = (sub < MLSTM_QK_DIM) if half == 0 else (sub >= MLSTM_QK_DIM)
        head_lane = (lane < MLSTM_QK_DIM) if half == 0 else (lane >= MLSTM_QK_DIM)
        qth = jnp.where(head_sub, qt[pair], 0.0).astype(BF16)
        rcol = col[:, GATE_I0 + h:GATE_I0 + h + 1] - col[:, GATE_F0 + h:GATE_F0 + h + 1]
        brow = row[GATE_F0 + h:GATE_F0 + h + 1, :]
        lirow = row[GATE_I0 + h:GATE_I0 + h + 1, :]
        g = brow[:, L - 1:L]
        m_prev = m_sc[h][0:1, 0:1]

        dt = jnp.where(causal, rcol + brow, -jnp.inf)
        inter_log = brow + m_prev
        m_t = jnp.maximum(inter_log, jnp.max(dt, axis=0, keepdims=True))
        w_inter = jnp.exp(inter_log - m_t)
        pt = jnp.exp(dt - m_t) * _dot(kb[pair], qth)
        vaug = jnp.concatenate([vt_ref[h * LANES:(h + 1) * LANES, :], ones_rows], axis=0)
        cstate = c_sc[h]
        tot = w_inter * _dot(cstate.astype(BF16), qth) + _dot(vaug, pt.astype(BF16))
        den = tot[MLSTM_V_DIM:MLSTM_V_DIM + 1]
        hout = tot[0:MLSTM_V_DIM] * (1.0 / jnp.maximum(jnp.abs(den), jnp.exp(-m_t)))

        a = g + (lirow - brow)
        m_loc = jnp.max(a, axis=1, keepdims=True)
        vw = (vaug.astype(F32) * jnp.exp(a - m_loc)).astype(BF16)
        kmask = jnp.where(head_lane, kb[pair], jnp.zeros_like(kb[pair]))
        kv = _dot(vw, kmask)
        m_new = jnp.maximum(g + m_prev, m_loc)
        c_sc[h] = jnp.exp(g + m_prev - m_new) * cstate + jnp.exp(m_loc - m_new) * kv
        m_sc[h] = jnp.broadcast_to(m_new, m_sc.shape[1:])

        ms = jnp.mean(hout * hout, axis=0, keepdims=True)
        gain = gain_ref[h * LANES:(h + 1) * LANES, :]
        hn = hout * lax.rsqrt(ms + RMS_EPS) * jnp.concatenate([gain] * (L // LANES), axis=1)
        gate = jax.nn.sigmoid(ogt_ref[h * LANES:(h + 1) * LANES, :])
        o_ref[:, h * LANES:(h + 1) * LANES] = (hn * gate).T.astype(o_ref.dtype)


def _mlstm(mqk, mvt, mot, gcol, grow, conv_w, gain_lanes, batch, seq):
    n = mqk.shape[0]
    L = SEQ_BLOCK
    nc = seq // L
    qk_width = 2 * MLSTM_QK_WIDTH
    row = lambda width: pl.BlockSpec((L, width), lambda b, c: (b * nc + c, 0))
    col = pl.BlockSpec((MLSTM_V_WIDTH, L), lambda b, c: (0, b * nc + c))
    full = lambda a: pl.BlockSpec(a.shape, lambda b, c: (0, 0))
    return pl.pallas_call(
        _mlstm_kernel,
        grid=(batch, nc),
        in_specs=[row(qk_width), col, col, row(LANES),
                  pl.BlockSpec((1, 16, L), lambda b, c: (b, 0, c)),
                  full(conv_w), full(gain_lanes)],
        out_specs=row(MLSTM_V_WIDTH),
        out_shape=jax.ShapeDtypeStruct((n, MLSTM_V_WIDTH), BF16),
        scratch_shapes=[pltpu.VMEM((8, qk_width), F32), pltpu.VMEM((8 + L, qk_width), F32),
                        pltpu.VMEM((MLSTM_HEADS, MLSTM_V_DIM + MLSTM_ONES_ROWS, LANES), F32),
                        pltpu.VMEM((MLSTM_HEADS, 8, LANES), F32)],
        compiler_params=_params(("parallel", "arbitrary"), 48),
    )(mqk, mvt, mot, gcol, grow, conv_w, gain_lanes)


def _outproj_kernel(fo_ref, mo_ref, w_ref, x_ref, g_ref, b_ref, o_ref):
    mix = _dot(fo_ref[...], w_ref[0:FOX_WIDTH, :]) + _dot(mo_ref[...], w_ref[FOX_WIDTH:, :])
    o_ref[...] = _layer_norm(DEEPNORM_ALPHA * x_ref[...] + mix, g_ref[...], b_ref[...])


def _outproj(fo, mo, w_out, x2d, g, b):
    n = x2d.shape[0]
    tm = ROW_TILE
    row = lambda width: pl.BlockSpec((tm, width), lambda i: (i, 0))
    full = lambda a: pl.BlockSpec(a.shape, lambda i: (0, 0))
    return pl.pallas_call(
        _outproj_kernel,
        grid=(n // tm,),
        in_specs=[row(512), row(512), full(w_out), row(D_MODEL), full(g), full(b)],
        out_specs=row(D_MODEL),
        out_shape=jax.ShapeDtypeStruct((n, D_MODEL), F32),
        compiler_params=_params(("parallel",), 48),
    )(fo, mo, w_out, x2d, g, b)


def _memkv_kernel(mem_ref, wk_ref, wv_ref, k_ref, v_ref):
    mb = mem_ref[...].astype(BF16)
    k_ref[...] = (_dot(mb, wk_ref[...]) * (XATTN_HEAD_DIM ** -0.5)).astype(BF16)
    v_ref[...] = _dot(mb, wv_ref[...]).astype(BF16)


def _memkv(mem2d, wk, wv, n_mem):
    n = mem2d.shape[0]
    row = pl.BlockSpec((n_mem, D_MODEL), lambda i: (i, 0))
    full = lambda a: pl.BlockSpec(a.shape, lambda i: (0, 0))
    return pl.pallas_call(
        _memkv_kernel,
        grid=(n // n_mem,),
        in_specs=[row, full(wk), full(wv)],
        out_specs=(row, row),
        out_shape=(jax.ShapeDtypeStruct((n, D_MODEL), BF16),) * 2,
        compiler_params=_params(("parallel",), 32),
    )(mem2d, wk, wv)


def _xattn_kernel(x_ref, k_ref, v_ref, wq_ref, wo_ref, g_ref, b_ref, wr_ref, br_ref,
                  o_ref, ob_ref, lg_ref):
    half = x_ref.shape[0] // 2
    for r in range(2):
        rows = slice(r * half, (r + 1) * half)
        x = x_ref[rows, :]
        q = _dot(x.astype(BF16), wq_ref[...]).astype(BF16)
        outs = []
        for h in range(XATTN_HEADS):
            sl = slice(h * XATTN_HEAD_DIM, (h + 1) * XATTN_HEAD_DIM)
            s = _dot_nt(q[:, sl], k_ref[:, sl])
            p = jnp.exp(s - jnp.max(s, axis=-1, keepdims=True))
            l = jnp.sum(p, axis=-1, keepdims=True)
            outs.append((_dot(p.astype(BF16), v_ref[:, sl]) / l).astype(BF16))
        o = jnp.concatenate(outs, axis=1)
        xa = _dot(o, wo_ref[...])
        x2 = _layer_norm(DEEPNORM_ALPHA * x + xa, g_ref[...], b_ref[...])
        o_ref[rows, :] = x2
        packed = _pack_bf16_pairs(x2)
        for j in range(ob_ref.shape[0]):
            ob_ref[j, rows, :] = packed[:, j * LANES:(j + 1) * LANES]
        x2h = x2.astype(BF16)
        x2l = (x2 - x2h.astype(F32)).astype(BF16)
        a = _dot_nt(wr_ref[...], x2h)
        b = _dot_nt(wr_ref[0:N_EXPERTS, :], x2l)
        lg_ref[:, rows] = a[0:N_EXPERTS] + a[N_EXPERTS:] + b + br_ref[...]


def _xattn(x1, kmem, vmem, wq, wo, g, b, wr, br, batch, seq, n_mem):
    n = x1.shape[0]
    tm = ROW_TILE
    nt = seq // tm
    row = lambda width: pl.BlockSpec((tm, width), lambda bi, i: (bi * nt + i, 0))
    full = lambda a: pl.BlockSpec(a.shape, lambda bi, i: (0, 0))
    kv = pl.BlockSpec((n_mem, D_MODEL), lambda bi, i: (bi, 0))
    return pl.pallas_call(
        _xattn_kernel,
        grid=(batch, nt),
        in_specs=[row(D_MODEL), kv, kv, full(wq), full(wo), full(g), full(b), full(wr), full(br)],
        out_specs=(row(D_MODEL),
                   pl.BlockSpec((X_PLANES, tm, LANES), lambda bi, i: (0, bi * nt + i, 0)),
                   pl.BlockSpec((N_EXPERTS, tm), lambda bi, i: (0, bi * nt + i))),
        out_shape=(jax.ShapeDtypeStruct((n, D_MODEL), F32),
                   jax.ShapeDtypeStruct((X_PLANES, n, LANES), jnp.uint32),
                   jax.ShapeDtypeStruct((N_EXPERTS, n), F32)),
        compiler_params=_params(("parallel", "parallel"), 60),
    )(x1, kmem, vmem, wq, wo, g, b, wr, br)


def _route_kernel(lg_ref, idx_ref, rank_ref, gate_ref, cnt_ref, carry_ref):
    i = pl.program_id(0)

    @pl.when(i == 0)
    def _():
        carry_ref[...] = jnp.zeros_like(carry_ref)

    lg = lg_ref[...]
    t = lg.shape[1]
    e_idx = lax.broadcasted_iota(jnp.int32, lg.shape, 0).astype(F32)
    sels, vals, idxs = [], [], []
    for _ in range(TOP_K):
        mx = jnp.max(lg, axis=0, keepdims=True)
        first = jnp.min(jnp.where(lg == mx, e_idx, float(N_EXPERTS)), axis=0, keepdims=True)
        sel = e_idx == first
        sels.append(sel)
        vals.append(mx)
        idxs.append(first)
        lg = jnp.where(sel, -jnp.inf, lg)
    exps = [jnp.exp(v - vals[0]) for v in vals]
    tot = exps[0] + exps[1] + exps[2] + exps[3]

    selmat = (sels[0] | sels[1] | sels[2] | sels[3])
    r = lax.broadcasted_iota(jnp.int32, (t, t), 0)
    s = lax.broadcasted_iota(jnp.int32, (t, t), 1)
    earlier = (r < s).astype(BF16)
    carry = carry_ref[:, 0:1]
    rankmat = _dot(selmat.astype(BF16), earlier) + carry
    new_carry = carry + jnp.sum(selmat.astype(F32), axis=1, keepdims=True)
    carry_ref[...] = jnp.broadcast_to(new_carry, carry_ref.shape)
    cnt_ref[...] = jnp.broadcast_to(new_carry, cnt_ref.shape).astype(jnp.int32)

    row8 = lax.broadcasted_iota(jnp.int32, (8, t), 0)
    row128 = lax.broadcasted_iota(jnp.int32, (LANES, t), 0)
    idx_out = jnp.zeros((8, t), F32)
    rank_out = jnp.zeros((8, t), F32)
    gate_out = jnp.zeros((LANES, t), F32)
    for k in range(TOP_K):
        rk = jnp.sum(jnp.where(sels[k], rankmat, 0.0), axis=0, keepdims=True)
        idx_out = jnp.where(row8 == k, idxs[k], idx_out)
        rank_out = jnp.where(row8 == k, rk, rank_out)
        gate_out = jnp.where(row128 == k, exps[k] / tot, gate_out)
    idx_ref[...] = idx_out.astype(jnp.int32)
    rank_ref[...] = rank_out.astype(jnp.int32)
    gate_ref[...] = gate_out.T


def _route(logits_t):
    n = logits_t.shape[1]
    t = ROUTE_TILE
    col = lambda rows: pl.BlockSpec((rows, t), lambda i: (0, i))
    return pl.pallas_call(
        _route_kernel,
        grid=(n // t,),
        in_specs=[col(N_EXPERTS)],
        out_specs=(col(8), col(8), pl.BlockSpec((t, LANES), lambda i: (i, 0)),
                   pl.BlockSpec((N_EXPERTS, LANES), lambda i: (0, 0))),
        out_shape=(jax.ShapeDtypeStruct((8, n), jnp.int32),
                   jax.ShapeDtypeStruct((8, n), jnp.int32),
                   jax.ShapeDtypeStruct((n, LANES), F32),
                   jax.ShapeDtypeStruct((N_EXPERTS, LANES), jnp.int32)),
        scratch_shapes=[pltpu.VMEM((N_EXPERTS, LANES), F32)],
        compiler_params=_params(("arbitrary",), 32),
    )(logits_t)


def _expert_kernel(blk_e_ref, blk_rows_ref, next_e_ref, x_ref, wgu_hbm, bgu_ref, wd_hbm, bd_ref,
                   y_ref, wgu_f32, wd_f32, wgu_sc, wd_sc, state_ref, sem):
    i = pl.program_id(0)
    e = blk_e_ref[i]
    rows = blk_rows_ref[i]
    g = y_ref.shape[1]

    def weight_copies(expert, slot):
        return (pltpu.make_async_copy(wgu_hbm.at[expert], wgu_f32.at[slot], sem.at[0, slot]),
                pltpu.make_async_copy(wd_hbm.at[expert], wd_f32.at[slot], sem.at[1, slot]))

    @pl.when(i == 0)
    def _():
        state_ref[0] = -1
        state_ref[1] = 0

    @pl.when((rows > 0) & (e != state_ref[0]))
    def _():
        slot = state_ref[1]

        @pl.when(i == 0)
        def _():
            for cp in weight_copies(e, slot):
                cp.start()

        for cp in weight_copies(e, slot):
            cp.wait()
        wgu_sc[...] = wgu_f32[slot].astype(BF16)
        wd_sc[...] = wd_f32[slot].astype(BF16)
        nxt = next_e_ref[i]

        @pl.when(nxt >= 0)
        def _():
            for cp in weight_copies(nxt, 1 - slot):
                cp.start()

        state_ref[0] = e
        state_ref[1] = 1 - slot

    def ffn(m):
        packed = jnp.concatenate([x_ref[j, 0:m, :] for j in range(X_PLANES)], axis=1)
        xb = _unpack_bf16_pairs(packed)
        hids = []
        for c in range(D_EXPERT // EXPERT_CHUNK):
            g0 = c * EXPERT_CHUNK
            l0 = D_EXPERT + g0
            gate = _dot(xb, wgu_sc[:, g0:g0 + EXPERT_CHUNK]) + bgu_ref[0, :, g0:g0 + EXPERT_CHUNK]
            lin = _dot(xb, wgu_sc[:, l0:l0 + EXPERT_CHUNK]) + bgu_ref[0, :, l0:l0 + EXPERT_CHUNK]
            gate = jnp.minimum(gate, SWIGLU_LIMIT)
            lin = jnp.clip(lin, -SWIGLU_LIMIT, SWIGLU_LIMIT)
            hids.append((gate * jax.nn.sigmoid(SWIGLU_ALPHA * gate) * (lin + 1.0)).astype(BF16))
        y = _pack_bf16_pairs(_dot(jnp.concatenate(hids, axis=1), wd_sc[...]) + bd_ref[0])
        for j in range(Y_PLANES):
            y_ref[j, 0:m, :] = y[:, j * LANES:(j + 1) * LANES]

    @pl.when(rows > g // 2)
    def _():
        ffn(g)

    @pl.when((rows > 0) & (rows <= g // 2))
    def _():
        ffn(g // 2)
        y_ref[:, g // 2:, :] = jnp.zeros((Y_PLANES, g // 2, LANES), y_ref.dtype)

    @pl.when(rows == 0)
    def _():
        y_ref[...] = jnp.zeros_like(y_ref)


def _experts(blk_e, blk_rows, next_e, xs, w_gu, b_gu, w_d, b_d):
    p = xs.shape[1]
    g = EXPERT_ROWS
    grid_spec = pltpu.PrefetchScalarGridSpec(
        num_scalar_prefetch=3,
        grid=(p // g,),
        in_specs=[pl.BlockSpec((X_PLANES, g, LANES), lambda i, be, br, ne: (0, i, 0)),
                  pl.BlockSpec(memory_space=pl.ANY),
                  pl.BlockSpec((1, 1, 2 * D_EXPERT), lambda i, be, br, ne: (be[i], 0, 0)),
                  pl.BlockSpec(memory_space=pl.ANY),
                  pl.BlockSpec((1, 1, D_MODEL), lambda i, be, br, ne: (be[i], 0, 0))],
        out_specs=pl.BlockSpec((Y_PLANES, g, LANES), lambda i, be, br, ne: (0, i, 0)),
        scratch_shapes=[pltpu.VMEM((2, D_MODEL, 2 * D_EXPERT), F32),
                        pltpu.VMEM((2, D_EXPERT, D_MODEL), F32),
                        pltpu.VMEM((D_MODEL, 2 * D_EXPERT), BF16),
                        pltpu.VMEM((D_EXPERT, D_MODEL), BF16),
                        pltpu.SMEM((2,), jnp.int32),
                        pltpu.SemaphoreType.DMA((2, 2))],
    )
    return pl.pallas_call(
        _expert_kernel,
        grid_spec=grid_spec,
        out_shape=jax.ShapeDtypeStruct((Y_PLANES, p, LANES), jnp.uint32),
        compiler_params=_params(("arbitrary",), 56),
    )(blk_e, blk_rows, next_e, xs, w_gu, b_gu, w_d, b_d)


def _combine_kernel(y_ref, gate_ref, x_ref, g_ref, b_ref, *rest):
    o_ref = rest[-1]
    gate = gate_ref[...]
    ff = None
    for k in range(TOP_K):
        packed = jnp.concatenate([y_ref[k, j] for j in range(Y_PLANES)], axis=1)
        yk = _unpack_bf16_pairs(packed, F32) * gate[:, k:k + 1]
        ff = yk if ff is None else ff + yk
    o_ref[...] = _layer_norm(DEEPNORM_ALPHA * x_ref[...] + ff, g_ref[...], b_ref[...])


def _combine(yg, gate, x2, g, b, first_row, earlier_out):
    n = x2.shape[0]
    n_part = yg.shape[2]
    tm = ROW_TILE
    off = first_row // tm
    row = lambda width: pl.BlockSpec((tm, width), lambda i: (i + off, 0))
    full = lambda a: pl.BlockSpec(a.shape, lambda i: (0, 0))
    in_specs = [pl.BlockSpec((TOP_K, Y_PLANES, tm, LANES), lambda i: (0, 0, i, 0)),
                row(LANES), row(D_MODEL), full(g), full(b)]
    operands = [yg, gate, x2, g, b]
    aliases = {}
    if earlier_out is not None:
        in_specs.append(pl.BlockSpec(memory_space=pl.ANY))
        operands.append(earlier_out)
        aliases = {len(operands) - 1: 0}
    return pl.pallas_call(
        _combine_kernel,
        grid=(n_part // tm,),
        in_specs=in_specs,
        out_specs=row(D_MODEL),
        out_shape=jax.ShapeDtypeStruct((n, D_MODEL), F32),
        input_output_aliases=aliases,
        compiler_params=_params(("parallel",), 56),
    )(*operands)


SC_WINDOW = 128
SC_GATHERS_IN_FLIGHT = 2


def _sc_mesh():
    return plsc.VectorSubcoreMesh(core_axis_name="core", subcore_axis_name="subcore")


def _sc_gather(table, idx):
    m = idx.shape[0]
    n_fly = SC_GATHERS_IN_FLIGHT

    @pl.kernel(out_type=jax.ShapeDtypeStruct((m, LANES), table.dtype), mesh=_sc_mesh(),
               scratch_types=[pltpu.SemaphoreType.DMA])
    def gather_kernel(table_hbm, idx_hbm, out_hbm, sem):
        def body(*refs):
            idx_vmem, out_vmem = refs[:n_fly], refs[n_fly]
            copies = [pltpu.async_copy(table_hbm.at[iv.at[0]],
                                       out_vmem.at[pl.ds(u * SC_WINDOW, SC_WINDOW)], sem)
                      for u, iv in enumerate(idx_vmem)]
            for cp in copies:
                cp.wait()

        pltpu.emit_pipeline(
            body,
            grid=(m // (n_fly * SC_WINDOW),),
            in_specs=[pl.BlockSpec((1, SC_WINDOW), lambda i, u=u: (0, n_fly * i + u))
                      for u in range(n_fly)],
            out_specs=[pl.BlockSpec((n_fly * SC_WINDOW, LANES), lambda i: (i, 0))],
            core_axis_name=("core", "subcore"),
            dimension_semantics=(pltpu.PARALLEL,),
        )(*([idx_hbm] * n_fly), out_hbm)

    return gather_kernel(table, idx.reshape(1, m))


def _sc_scatter(src, idx_lists, out_rows):
    m = src.shape[0]
    n_lists = len(idx_lists)

    @pl.kernel(out_type=jax.ShapeDtypeStruct((out_rows, LANES), src.dtype), mesh=_sc_mesh(),
               scratch_types=[pltpu.SemaphoreType.DMA])
    def scatter_kernel(src_hbm, *refs):
        idx_hbm, out_hbm, sem = refs[:n_lists], refs[n_lists], refs[n_lists + 1]

        def body(src_vmem, *idx_vmem):
            copies = [pltpu.async_copy(src_vmem, out_hbm.at[iv.at[0]], sem) for iv in idx_vmem]
            for cp in copies:
                cp.wait()

        pltpu.emit_pipeline(
            body,
            grid=(m // SC_WINDOW,),
            in_specs=[pl.BlockSpec((SC_WINDOW, LANES), lambda i: (i, 0))]
            + [pl.BlockSpec((1, SC_WINDOW), lambda i: (0, i))] * n_lists,
            out_specs=[],
            core_axis_name=("core", "subcore"),
            dimension_semantics=(pltpu.PARALLEL,),
        )(src_hbm, *idx_hbm)

    return scatter_kernel(src, *[ix.reshape(1, m) for ix in idx_lists])


def _layer(x, mem, w_in, fox_f_bias, conv_w, i_bias, f_bias, fox_g, mlstm_g, w_mix_out,
           ln1_g, ln1_b, w_xq, w_xk, w_xv, w_xo, ln2_g, ln2_b, w_router, b_router,
           w_gate_up, b_gate_up, w_down, b_down, ln3_g, ln3_b):
    batch, seq, d = x.shape
    n_mem = mem.shape[1]
    n = batch * seq
    x2d = x.reshape(n, d)

    o_ff = 3 * FOX_WIDTH
    o_mqk = o_ff + FOX_HEADS
    o_mv = o_mqk + 2 * MLSTM_QK_WIDTH
    o_mi = o_mv + MLSTM_V_WIDTH
    o_mf = o_mi + MLSTM_HEADS
    o_mo = o_mf + MLSTM_HEADS
    n_gate = FOX_HEADS + 2 * MLSTM_HEADS
    w_r = jnp.concatenate(
        [w_in[:, FOX_WIDTH:2 * FOX_WIDTH], w_in[:, o_mqk:o_mv],
         w_in[:, o_ff:o_mqk], w_in[:, o_mi:o_mo],
         jnp.zeros((d, LANES - n_gate), w_in.dtype)], axis=1).astype(BF16)
    w_t = jnp.concatenate([w_in[:, :FOX_WIDTH], w_in[:, 2 * FOX_WIDTH:o_ff],
                           w_in[:, o_mv:o_mi], w_in[:, o_mo:]], axis=1).T.astype(BF16)
    gate_bias = jnp.concatenate(
        [fox_f_bias, i_bias, f_bias, jnp.zeros((LANES - n_gate,), F32)]).reshape(1, LANES)

    fqt, fk, fvt, mqk, mvt, mot, gates = _inproj(x2d, w_r, w_t, gate_bias)
    gcol, grow, cpieces = _gateprep(gates, batch, seq)
    fox_gain_lanes = jnp.broadcast_to(fox_g[:, None], (FOX_WIDTH, LANES))
    fo = _fox(fqt, fk, cpieces, fvt, fox_gain_lanes, batch, seq)
    gain_lanes = jnp.broadcast_to(mlstm_g[:, None], (MLSTM_V_WIDTH, LANES))
    mo_out = _mlstm(mqk, mvt, mot, gcol, grow, conv_w, gain_lanes, batch, seq)
    x1 = _outproj(fo, mo_out, w_mix_out.astype(BF16), x2d, ln1_g.reshape(1, d), ln1_b.reshape(1, d))

    kmem, vmem = _memkv(mem.reshape(batch * n_mem, d), w_xk.astype(BF16), w_xv.astype(BF16), n_mem)
    wrt = w_router.T
    wrt_hi = wrt.astype(BF16)
    wrt_lo = (wrt - wrt_hi.astype(F32)).astype(BF16)
    x2, x2p, logits_t = _xattn(x1, kmem, vmem, w_xq.astype(BF16), w_xo.astype(BF16),
                               ln2_g.reshape(1, d), ln2_b.reshape(1, d),
                               jnp.concatenate([wrt_hi, wrt_lo], axis=0),
                               b_router.reshape(N_EXPERTS, 1), batch, seq, n_mem)

    idx_t, rank_t, gate, cnt = _route(logits_t)
    counts = cnt[:, 0]
    g_rows = EXPERT_ROWS
    padded = ((counts + g_rows - 1) // g_rows) * g_rows
    pad_end = jnp.cumsum(padded)
    pad_start = pad_end - padded
    experts = jnp.arange(N_EXPERTS, dtype=jnp.int32)
    sel = idx_t[:TOP_K, :, None] == experts[None, None, :]
    pos_t = jnp.sum(jnp.where(sel, pad_start[None, None, :], 0), axis=-1) + rank_t[:TOP_K]
    p_rows = n * TOP_K + N_EXPERTS * g_rows
    nb = p_rows // g_rows
    blk_start = jnp.arange(nb, dtype=jnp.int32) * g_rows
    blk_e = jnp.minimum(jnp.sum((pad_end[None, :] <= blk_start[:, None]).astype(jnp.int32), axis=1),
                        N_EXPERTS - 1)
    own = blk_e[:, None] == experts[None, :]
    row_end = jnp.sum(jnp.where(own, (pad_start + counts)[None, :], 0), axis=1)
    blk_rows = jnp.clip(row_end - blk_start, 0, g_rows).astype(jnp.int32)
    later = (experts[None, :] > blk_e[:, None]) & (counts[None, :] > 0)
    next_e = jnp.min(jnp.where(later, experts[None, :], N_EXPERTS), axis=1)
    next_e = jnp.where(next_e == N_EXPERTS, -1, next_e).astype(jnp.int32)

    def piece_index(planes):
        off = jnp.arange(planes, dtype=jnp.int32) * p_rows
        return pos_t[:, None, :] + off[None, :, None]

    x_idx = piece_index(X_PLANES).reshape(TOP_K, X_PLANES * n)
    xs = _sc_scatter(x2p.reshape(X_PLANES * n, LANES), [x_idx[k] for k in range(TOP_K)],
                     X_PLANES * p_rows)
    y = _experts(blk_e, blk_rows, next_e, xs.reshape(X_PLANES, p_rows, LANES), w_gate_up,
                 b_gate_up.reshape(N_EXPERTS, 1, -1), w_down, b_down.reshape(N_EXPERTS, 1, -1))
    y_idx = piece_index(Y_PLANES)
    n_part = n // COMBINE_PARTS
    out = None
    for part in range(COMBINE_PARTS):
        t0 = part * n_part
        yg = _sc_gather(y.reshape(Y_PLANES * p_rows, LANES),
                        y_idx[:, :, t0:t0 + n_part].reshape(-1))
        out = _combine(yg.reshape(TOP_K, Y_PLANES, n_part, LANES), gate, x2,
                       ln3_g.reshape(1, d), ln3_b.reshape(1, d), t0, out)
    return out.reshape(batch, seq, d)


def kernel(x, mem, w_in, fox_f_bias, mlstm_conv_w, mlstm_i_bias, mlstm_f_bias, fox_norm_g, mlstm_norm_g, w_mix_out, ln1_g, ln1_b, w_xq, w_xk, w_xv, w_xo, ln2_g, ln2_b, w_router, b_router, w_gate_up, b_gate_up, w_down, b_down, ln3_g, ln3_b):
    for l in range(w_in.shape[0]):
        x = _layer(x, mem, w_in[l], fox_f_bias[l], mlstm_conv_w[l], mlstm_i_bias[l],
                   mlstm_f_bias[l], fox_norm_g[l], mlstm_norm_g[l], w_mix_out[l],
                   ln1_g[l], ln1_b[l], w_xq[l], w_xk[l], w_xv[l], w_xo[l], ln2_g[l], ln2_b[l],
                   w_router[l], b_router[l], w_gate_up[l], b_gate_up[l], w_down[l], b_down[l],
                   ln3_g[l], ln3_b[l])
    return x
```

```python
import jax
import jax.numpy as jnp
from jax import lax
from jax.experimental import pallas as pl
from jax.experimental.pallas import tpu as pltpu
from jax.experimental.pallas import tpu_sc as plsc

F32 = jnp.float32
BF16 = jnp.bfloat16

D_MODEL = 1024
FOX_HEADS = 8
FOX_HEAD_DIM = 64
FOX_WIDTH = FOX_HEADS * FOX_HEAD_DIM
MLSTM_HEADS = 4
MLSTM_QK_DIM = 64
MLSTM_V_DIM = 128
MLSTM_QK_WIDTH = MLSTM_HEADS * MLSTM_QK_DIM
MLSTM_V_WIDTH = MLSTM_HEADS * MLSTM_V_DIM
CONV_WIDTH = 4
XATTN_HEADS = 4
XATTN_HEAD_DIM = D_MODEL // XATTN_HEADS
N_EXPERTS = 32
TOP_K = 4
D_EXPERT = D_MODEL
SWIGLU_LIMIT = 7.0
SWIGLU_ALPHA = 1.702
DEEPNORM_ALPHA = 2.0 ** 0.25
LN_EPS = 1e-5
RMS_EPS = 1e-6

LANES = 128
SEQ_BLOCK = 256
ROW_TILE = 1024
EXPERT_ROWS = 512
EXPERT_CHUNK = 256
ROUTE_TILE = 512
X_PLANES = D_MODEL // 2 // LANES
Y_PLANES = X_PLANES
GATE_I0 = FOX_HEADS
GATE_F0 = FOX_HEADS + MLSTM_HEADS

MIB = 1024 * 1024


def _params(semantics, vmem_mib):
    return pltpu.CompilerParams(dimension_semantics=semantics,
                                vmem_limit_bytes=vmem_mib * MIB)


def _layer_norm(y, g, b):
    mu = jnp.mean(y, axis=-1, keepdims=True)
    yc = y - mu
    var = jnp.mean(yc * yc, axis=-1, keepdims=True)
    return yc * lax.rsqrt(var + LN_EPS) * g + b


def _dot(a, b):
    return jnp.dot(a, b, preferred_element_type=F32)


def _dot_nt(a, b):
    return lax.dot_general(a, b, (((1,), (1,)), ((), ())), preferred_element_type=F32)


def _pack_bf16_pairs(x):
    w = x.shape[1] // 2
    lo = pltpu.bitcast(x[:, :w].astype(BF16).astype(F32), jnp.uint32)
    hi = pltpu.bitcast(x[:, w:].astype(BF16).astype(F32), jnp.uint32)
    return (lo >> 16) | hi


def _unpack_bf16_pairs(u, dtype=BF16):
    lo = pltpu.bitcast(u << 16, F32).astype(dtype)
    hi = pltpu.bitcast(u & jnp.uint32(0xFFFF0000), F32).astype(dtype)
    return jnp.concatenate([lo, hi], axis=1)


def _inproj_kernel(x_ref, w_ref, wt_ref, gb_ref, fqt_ref, fk_ref, fvt_ref, mqk_ref, mvt_ref,
                   mot_ref, g_ref):
    xb = x_ref[...].astype(BF16)

    def mm(c0, width):
        return _dot(xb, w_ref[:, c0:c0 + width])

    def mm_t(r0, height):
        return _dot_nt(wt_ref[r0:r0 + height, :], xb)

    r_fv = FOX_WIDTH
    r_mv = r_fv + FOX_WIDTH
    r_mo = r_mv + MLSTM_V_WIDTH
    fqt_ref[...] = (mm_t(0, FOX_WIDTH) * (FOX_HEAD_DIM ** -0.5)).astype(BF16)
    fvt_ref[...] = mm_t(r_fv, FOX_WIDTH).astype(BF16)
    mvt_ref[...] = mm_t(r_mv, MLSTM_V_WIDTH).astype(BF16)
    mot_ref[...] = mm_t(r_mo, MLSTM_V_WIDTH)
    c_qk = FOX_WIDTH
    c_g = c_qk + 2 * MLSTM_QK_WIDTH
    fk_ref[...] = mm(0, FOX_WIDTH).astype(BF16)
    mqk_ref[...] = mm(c_qk, 2 * MLSTM_QK_WIDTH)
    g_ref[...] = mm(c_g, LANES) + gb_ref[...]


def _inproj(x2d, w_r, w_t, gate_bias):
    n = x2d.shape[0]
    tm = ROW_TILE
    row = lambda width: pl.BlockSpec((tm, width), lambda i: (i, 0))
    col = lambda height: pl.BlockSpec((height, tm), lambda i: (0, i))
    full = lambda a: pl.BlockSpec(a.shape, lambda i: (0, 0))
    out_shapes = (
        jax.ShapeDtypeStruct((FOX_WIDTH, n), BF16),
        jax.ShapeDtypeStruct((n, FOX_WIDTH), BF16),
        jax.ShapeDtypeStruct((FOX_WIDTH, n), BF16),
        jax.ShapeDtypeStruct((n, 2 * MLSTM_QK_WIDTH), F32),
        jax.ShapeDtypeStruct((MLSTM_V_WIDTH, n), BF16),
        jax.ShapeDtypeStruct((MLSTM_V_WIDTH, n), F32),
        jax.ShapeDtypeStruct((n, LANES), F32),
    )
    return pl.pallas_call(
        _inproj_kernel,
        grid=(n // tm,),
        in_specs=[row(D_MODEL), full(w_r), full(w_t), full(gate_bias)],
        out_specs=(col(FOX_WIDTH), row(FOX_WIDTH), col(FOX_WIDTH), row(2 * MLSTM_QK_WIDTH),
                   col(MLSTM_V_WIDTH), col(MLSTM_V_WIDTH), row(LANES)),
        out_shape=out_shapes,
        compiler_params=_params(("parallel",), 48),
    )(x2d, w_r, w_t, gate_bias)


def _split3(x):
    hi = x.astype(BF16)
    r1 = x - hi.astype(F32)
    mid = r1.astype(BF16)
    lo = (r1 - mid.astype(F32)).astype(BF16)
    return hi, mid, lo


def _gateprep_kernel(g_ref, sel_ref, col_ref, row_ref, cp_ref, carry_ref):
    c = pl.program_id(1)

    @pl.when(c == 0)
    def _():
        carry_ref[...] = jnp.zeros_like(carry_ref)

    blk = SEQ_BLOCK
    lane = lax.broadcasted_iota(jnp.int32, (blk, LANES), 1)
    is_i = (lane >= GATE_I0) & (lane < GATE_F0)
    r = lax.broadcasted_iota(jnp.int32, (blk, blk), 0)
    s = lax.broadcasted_iota(jnp.int32, (blk, blk), 1)
    tri = (s <= r).astype(BF16)
    carry = carry_ref[0:1, :]
    globs = []
    for j in range(g_ref.shape[0] // blk):
        g = g_ref[j * blk:(j + 1) * blk, :]
        logsig = jnp.minimum(g, 0.0) - jnp.log1p(jnp.exp(-jnp.abs(g)))
        cs = None
        for piece in _split3(logsig):
            term = _dot(tri, piece)
            cs = term if cs is None else cs + term
        glob = cs + carry
        carry = glob[blk - 1:blk, :]
        out = jnp.where(lane < GATE_I0, glob, jnp.where(is_i, g, cs))
        col_ref[j * blk:(j + 1) * blk, :] = out
        row_ref[0, :, j * blk:(j + 1) * blk] = out.T[0:16, :]
        globs.append(glob)
    carry_ref[...] = jnp.broadcast_to(carry, carry_ref.shape)
    glob = jnp.concatenate(globs, axis=0)

    pieces = jnp.concatenate(_split3(-glob), axis=1)
    moved = _dot(pieces, sel_ref[...])
    for p in range(FOX_HEADS // 2):
        cp_ref[0, p] = moved[:, p * LANES:(p + 1) * LANES].astype(BF16)


def _piece_selector():
    src = jnp.arange(3 * LANES, dtype=jnp.int32)[:, None]
    dst = jnp.arange(4 * LANES, dtype=jnp.int32)[None, :]
    piece, head = src // LANES, src % LANES
    pair, lane = dst // LANES, dst % LANES
    hit = (lane < 6) & (lane % 3 == piece) & (head == 2 * pair + lane // 3)
    return hit.astype(BF16)


def _gateprep(gates, batch, seq):
    n = gates.shape[0]
    rows = ROW_TILE
    nc = seq // rows
    sel = _piece_selector()
    return pl.pallas_call(
        _gateprep_kernel,
        grid=(batch, nc),
        in_specs=[pl.BlockSpec((rows, LANES), lambda b, c: (b * nc + c, 0)),
                  pl.BlockSpec(sel.shape, lambda b, c: (0, 0))],
        out_specs=(pl.BlockSpec((rows, LANES), lambda b, c: (b * nc + c, 0)),
                   pl.BlockSpec((1, 16, rows), lambda b, c: (b, 0, c)),
                   pl.BlockSpec((1, FOX_HEADS // 2, rows, LANES), lambda b, c: (b, 0, c, 0))),
        out_shape=(jax.ShapeDtypeStruct((n, LANES), F32),
                   jax.ShapeDtypeStruct((batch, 16, seq), F32),
                   jax.ShapeDtypeStruct((batch, FOX_HEADS // 2, seq, LANES), BF16)),
        scratch_shapes=[pltpu.VMEM((8, LANES), F32)],
        compiler_params=_params(("parallel", "arbitrary"), 32),
    )(gates, sel)


FOX_ONES_ROWS = 16
FOX_VAUG_ROWS = FOX_HEAD_DIM + FOX_ONES_ROWS
FOX_QUERY_TILE = 2 * SEQ_BLOCK


def _fox_kernel(qt_ref, k_ref, cp_ref, vt_ref, gain_ref, o_ref,
                kaug_sc, vaug_sc, qaug_sc, sta_sc, stb_sc, stc_sc, m_sc, acc0_sc, acc1_sc):
    qi = pl.program_id(2)
    tq = o_ref.shape[0]
    tg = tq
    tk = SEQ_BLOCK
    hd = FOX_HEAD_DIM
    seq = k_ref.shape[0]
    nq = seq // tq
    par = qi % 2

    def load_queries(tile, slot):
        q0 = pl.multiple_of(tile * tq, tq)
        for h in range(2):
            qaug_sc[slot, h, 0:hd, :] = qt_ref[h * hd:(h + 1) * hd, pl.ds(q0, tq)]

    @pl.when(qi == 0)
    def _():
        ones3 = (lax.broadcasted_iota(jnp.int32, (FOX_ONES_ROWS, tq), 0) < 3).astype(BF16)
        for slot in range(2):
            for h in range(2):
                qaug_sc[slot, h, hd:hd + FOX_ONES_ROWS, :] = ones3
                qaug_sc[slot, h, hd + FOX_ONES_ROWS:, :] = jnp.zeros(
                    (LANES - hd - FOX_ONES_ROWS, tq), BF16)
        load_queries(0, 0)

    nxt = jnp.minimum(qi + 1, nq - 1)
    load_queries(nxt, 1 - par)

    @pl.when(qi == 0)
    def _():
        lane = lax.broadcasted_iota(jnp.int32, (tk, LANES), 1)

        def build(blk, carry):
            r0 = pl.multiple_of(blk * tk, tk)
            kp = k_ref[pl.ds(r0, tk), :].astype(F32)
            cp = cp_ref[0, 0, pl.ds(r0, tk), :].astype(F32)
            for h in range(2):
                kh = kp if h == 0 else pltpu.roll(kp, hd, axis=1)
                ch = pltpu.roll(cp, hd - 3 * h, axis=1)
                kaug = jnp.where(lane < hd, kh, jnp.where(lane < hd + 3, ch, 0.0))
                kaug_sc[h, pl.ds(r0, tk), :] = kaug.astype(BF16)
            return carry

        lax.fori_loop(0, seq // tk, build, 0)
        for h in range(2):
            vaug_sc[h, 0:hd, :] = vt_ref[h * hd:(h + 1) * hd, :]
            vaug_sc[h, hd:, :] = jnp.ones((FOX_ONES_ROWS, seq), BF16)

    def put_scores(kg, slot, masked=False, queries=None):
        k0 = pl.multiple_of(kg * tg, tg)
        queries = par if queries is None else queries
        for h in range(2):
            st = _dot(kaug_sc[h, pl.ds(k0, tg), :], qaug_sc[queries, h])
            if masked:
                kk = lax.broadcasted_iota(jnp.int32, (tg, tq), 0)
                tt = lax.broadcasted_iota(jnp.int32, (tg, tq), 1)
                st = jnp.where(kk <= tt, st, -jnp.inf)
            slot[h] = st

    m_sc[...] = jnp.full(m_sc.shape, -jnp.inf, F32)
    acc0_sc[...] = jnp.zeros(acc0_sc.shape, F32)
    acc1_sc[...] = jnp.zeros(acc1_sc.shape, F32)
    acc = (acc0_sc, acc1_sc)

    def absorb(kg, slot):
        k0 = pl.multiple_of(kg * tg, tg)
        half = tg // 2
        for h in range(2):
            st = slot[h]
            m_prev = m_sc[h]
            m_new = jnp.maximum(m_prev, jnp.max(st, axis=0, keepdims=True))
            pv = None
            for u in range(2):
                p = jnp.exp(st[u * half:(u + 1) * half] - m_new)
                part = _dot(vaug_sc[h, :, pl.ds(k0 + u * half, half)], p.astype(BF16))
                pv = part if pv is None else pv + part
            acc[h][...] = jnp.exp(m_prev - m_new) * acc[h][...] + pv
            m_sc[h] = m_new

    sa, sb, sc = sta_sc, stb_sc, stc_sc
    n_loop = jnp.maximum(qi - 1, 0) // 2

    def body(j, carry):
        put_scores(2 * j + 1, sb)
        absorb(2 * j, sa)
        put_scores(2 * j + 2, sa)
        absorb(2 * j + 1, sb)
        return carry

    lax.fori_loop(0, n_loop, body, 0)
    done = 2 * n_loop

    def put_next():
        put_scores(0, sa, queries=1 - par)

    @pl.when(qi == 0)
    def _():
        put_scores(qi, sc, masked=True)
        put_next()
        absorb(qi, sc)

    @pl.when((qi > 0) & (qi - done == 1))
    def _():
        put_scores(qi, sc, masked=True)
        absorb(done, sa)
        put_next()
        absorb(qi, sc)

    @pl.when((qi > 0) & (qi - done == 2))
    def _():
        put_scores(done + 1, sb)
        absorb(done, sa)
        put_scores(qi, sc, masked=True)
        absorb(done + 1, sb)
        put_next()
        absorb(qi, sc)

    normed = []
    for h in range(2):
        a = acc[h][...]
        oh = a[0:hd] * (1.0 / a[hd:hd + 1])
        ms = jnp.mean(oh * oh, axis=0, keepdims=True)
        gain = gain_ref[h * hd:(h + 1) * hd, :]
        normed.append(oh * lax.rsqrt(ms + RMS_EPS) * jnp.concatenate([gain] * (tq // LANES), axis=1))
    o_ref[...] = jnp.concatenate(normed, axis=0).T.astype(o_ref.dtype)


def _fox(fqt, fk, cpieces, fvt, fox_gain, batch, seq):
    n = fk.shape[0]
    tq = FOX_QUERY_TILE
    nq = seq // tq
    npair = FOX_HEADS // 2
    return pl.pallas_call(
        _fox_kernel,
        grid=(batch, npair, nq),
        in_specs=[pl.BlockSpec((LANES, seq), lambda b, hp, qi: (hp, b)),
                  pl.BlockSpec((seq, LANES), lambda b, hp, qi: (b, hp)),
                  pl.BlockSpec((1, 1, seq, LANES), lambda b, hp, qi: (b, hp, 0, 0)),
                  pl.BlockSpec((LANES, seq), lambda b, hp, qi: (hp, b)),
                  pl.BlockSpec((LANES, LANES), lambda b, hp, qi: (hp, 0))],
        out_specs=pl.BlockSpec((tq, LANES), lambda b, hp, qi: (b * nq + qi, hp)),
        out_shape=jax.ShapeDtypeStruct((n, FOX_WIDTH), BF16),
        scratch_shapes=[pltpu.VMEM((2, seq, LANES), BF16),
                        pltpu.VMEM((2, FOX_VAUG_ROWS, seq), BF16),
                        pltpu.VMEM((2, 2, LANES, tq), BF16),
                        pltpu.VMEM((2, tq, tq), F32),
                        pltpu.VMEM((2, tq, tq), F32),
                        pltpu.VMEM((2, tq, tq), F32),
                        pltpu.VMEM((2, 1, tq), F32),
                        pltpu.VMEM((FOX_VAUG_ROWS, tq), F32),
                        pltpu.VMEM((FOX_VAUG_ROWS, tq), F32)],
        compiler_params=_params(("parallel", "parallel", "arbitrary"), 48),
    )(fqt, fk, cpieces, fvt, fox_gain)


MLSTM_ONES_ROWS = 16


def _mlstm_kernel(qk_ref, vt_ref, ogt_ref, col_ref, row_ref, cw_ref, gain_ref, o_ref,
                  tail_ref, buf_ref, c_sc, m_sc):
    c = pl.program_id(1)
    L = SEQ_BLOCK

    @pl.when(c == 0)
    def _():
        tail_ref[...] = jnp.zeros_like(tail_ref)
        c_sc[...] = jnp.zeros_like(c_sc)
        m_sc[...] = jnp.zeros_like(m_sc)

    x = qk_ref[...]
    buf_ref[0:8, :] = tail_ref[...]
    buf_ref[8:8 + L, :] = x
    tail_ref[...] = x[L - 8:L, :]
    y = x * cw_ref[CONV_WIDTH - 1:CONV_WIDTH, :]
    for j in range(CONV_WIDTH - 1):
        shift = CONV_WIDTH - 1 - j
        y = y + buf_ref[8 - shift:8 - shift + L, :] * cw_ref[j:j + 1, :]
    y = y * jax.nn.sigmoid(y)
    kf = y[:, MLSTM_QK_WIDTH:] * (MLSTM_QK_DIM ** -0.5)
    n_pair = MLSTM_HEADS // 2
    qt = [y[:, p * LANES:(p + 1) * LANES].T for p in range(n_pair)]
    kb = [kf[:, p * LANES:(p + 1) * LANES].astype(BF16) for p in range(n_pair)]

    col = col_ref[...]
    row = row_ref[0]
    lane = lax.broadcasted_iota(jnp.int32, (L, LANES), 1)
    sub = lax.broadcasted_iota(jnp.int32, (LANES, L), 0)
    ss = lax.broadcasted_iota(jnp.int32, (L, L), 0)
    ll = lax.broadcasted_iota(jnp.int32, (L, L), 1)
    causal = ss <= ll
    ones_rows = jnp.ones((MLSTM_ONES_ROWS, L), BF16)

    for h in range(MLSTM_HEADS):
        pair, half = divmod(h, 2)
        head_sub = (sub < MLSTM_QK_DIM) if half == 0 else (sub >= MLSTM_QK_DIM)
        head_lane = (lane < MLSTM_QK_DIM) if half == 0 else (lane >= MLSTM_QK_DIM)
        qth = jnp.where(head_sub, qt[pair], 0.0).astype(BF16)
        rcol = col[:, GATE_I0 + h:GATE_I0 + h + 1] - col[:, GATE_F0 + h:GATE_F0 + h + 1]
        brow = row[GATE_F0 + h:GATE_F0 + h + 1, :]
        lirow = row[GATE_I0 + h:GATE_I0 + h + 1, :]
        g = brow[:, L - 1:L]
        m_prev = m_sc[h][0:1, 0:1]

        dt = jnp.where(causal, rcol + brow, -jnp.inf)
        inter_log = brow + m_prev
        m_t = jnp.maximum(inter_log, jnp.max(dt, axis=0, keepdims=True))
        w_inter = jnp.exp(inter_log - m_t)
        pt = jnp.exp(dt - m_t) * _dot(kb[pair], qth)
        vaug = jnp.concatenate([vt_ref[h * LANES:(h + 1) * LANES, :], ones_rows], axis=0)
        cstate = c_sc[h]
        tot = w_inter * _dot(cstate.astype(BF16), qth) + _dot(vaug, pt.astype(BF16))
        den = tot[MLSTM_V_DIM:MLSTM_V_DIM + 1]
        hout = tot[0:MLSTM_V_DIM] * (1.0 / jnp.maximum(jnp.abs(den), jnp.exp(-m_t)))

        a = g + (lirow - brow)
        m_loc = jnp.max(a, axis=1, keepdims=True)
        vw = (vaug.astype(F32) * jnp.exp(a - m_loc)).astype(BF16)
        kmask = jnp.where(head_lane, kb[pair], jnp.zeros_like(kb[pair]))
        kv = _dot(vw, kmask)
        m_new = jnp.maximum(g + m_prev, m_loc)
        c_sc[h] = jnp.exp(g + m_prev - m_new) * cstate + jnp.exp(m_loc - m_new) * kv
        m_sc[h] = jnp.broadcast_to(m_new, m_sc.shape[1:])

        ms = jnp.mean(hout * hout, axis=0, keepdims=True)
        gain = gain_ref[h * LANES:(h + 1) * LANES, :]
        hn = hout * lax.rsqrt(ms + RMS_EPS) * jnp.concatenate([gain] * (L // LANES), axis=1)
        gate = jax.nn.sigmoid(ogt_ref[h * LANES:(h + 1) * LANES, :])
        o_ref[:, h * LANES:(h + 1) * LANES] = (hn * gate).T.astype(o_ref.dtype)


def _mlstm(mqk, mvt, mot, gcol, grow, conv_w, gain_lanes, batch, seq):
    n = mqk.shape[0]
    L = SEQ_BLOCK
    nc = seq // L
    qk_width = 2 * MLSTM_QK_WIDTH
    row = lambda width: pl.BlockSpec((L, width), lambda b, c: (b * nc + c, 0))
    col = pl.BlockSpec((MLSTM_V_WIDTH, L), lambda b, c: (0, b * nc + c))
    full = lambda a: pl.BlockSpec(a.shape, lambda b, c: (0, 0))
    return pl.pallas_call(
        _mlstm_kernel,
        grid=(batch, nc),
        in_specs=[row(qk_width), col, col, row(LANES),
                  pl.BlockSpec((1, 16, L), lambda b, c: (b, 0, c)),
                  full(conv_w), full(gain_lanes)],
        out_specs=row(MLSTM_V_WIDTH),
        out_shape=jax.ShapeDtypeStruct((n, MLSTM_V_WIDTH), BF16),
        scratch_shapes=[pltpu.VMEM((8, qk_width), F32), pltpu.VMEM((8 + L, qk_width), F32),
                        pltpu.VMEM((MLSTM_HEADS, MLSTM_V_DIM + MLSTM_ONES_ROWS, LANES), F32),
                        pltpu.VMEM((MLSTM_HEADS, 8, LANES), F32)],
        compiler_params=_params(("parallel", "arbitrary"), 48),
    )(mqk, mvt, mot, gcol, grow, conv_w, gain_lanes)


def _outproj_kernel(fo_ref, mo_ref, w_ref, x_ref, g_ref, b_ref, o_ref):
    mix = _dot(fo_ref[...], w_ref[0:FOX_WIDTH, :]) + _dot(mo_ref[...], w_ref[FOX_WIDTH:, :])
    o_ref[...] = _layer_norm(DEEPNORM_ALPHA * x_ref[...] + mix, g_ref[...], b_ref[...])


def _outproj(fo, mo, w_out, x2d, g, b):
    n = x2d.shape[0]
    tm = ROW_TILE
    row = lambda width: pl.BlockSpec((tm, width), lambda i: (i, 0))
    full = lambda a: pl.BlockSpec(a.shape, lambda i: (0, 0))
    return pl.pallas_call(
        _outproj_kernel,
        grid=(n // tm,),
        in_specs=[row(512), row(512), full(w_out), row(D_MODEL), full(g), full(b)],
        out_specs=row(D_MODEL),
        out_shape=jax.ShapeDtypeStruct((n, D_MODEL), F32),
        compiler_params=_params(("parallel",), 48),
    )(fo, mo, w_out, x2d, g, b)


def _memkv_kernel(mem_ref, wk_ref, wv_ref, k_ref, v_ref):
    mb = mem_ref[...].astype(BF16)
    k_ref[...] = (_dot(mb, wk_ref[...]) * (XATTN_HEAD_DIM ** -0.5)).astype(BF16)
    v_ref[...] = _dot(mb, wv_ref[...]).astype(BF16)


def _memkv(mem2d, wk, wv, n_mem):
    n = mem2d.shape[0]
    row = pl.BlockSpec((n_mem, D_MODEL), lambda i: (i, 0))
    full = lambda a: pl.BlockSpec(a.shape, lambda i: (0, 0))
    return pl.pallas_call(
        _memkv_kernel,
        grid=(n // n_mem,),
        in_specs=[row, full(wk), full(wv)],
        out_specs=(row, row),
        out_shape=(jax.ShapeDtypeStruct((n, D_MODEL), BF16),) * 2,
        compiler_params=_params(("parallel",), 32),
    )(mem2d, wk, wv)


def _xattn_kernel(x_ref, k_ref, v_ref, wq_ref, wo_ref, g_ref, b_ref, wr_ref, br_ref,
                  o_ref, ob_ref, lg_ref):
    half = x_ref.shape[0] // 2
    for r in range(2):
        rows = slice(r * half, (r + 1) * half)
        x = x_ref[rows, :]
        q = _dot(x.astype(BF16), wq_ref[...]).astype(BF16)
        outs = []
        for h in range(XATTN_HEADS):
            sl = slice(h * XATTN_HEAD_DIM, (h + 1) * XATTN_HEAD_DIM)
            s = _dot_nt(q[:, sl], k_ref[:, sl])
            p = jnp.exp(s - jnp.max(s, axis=-1, keepdims=True))
            l = jnp.sum(p, axis=-1, keepdims=True)
            outs.append((_dot(p.astype(BF16), v_ref[:, sl]) / l).astype(BF16))
        o = jnp.concatenate(outs, axis=1)
        xa = _dot(o, wo_ref[...])
        x2 = _layer_norm(DEEPNORM_ALPHA * x + xa, g_ref[...], b_ref[...])
        o_ref[rows, :] = x2
        packed = _pack_bf16_pairs(x2)
        for j in range(ob_ref.shape[0]):
            ob_ref[j, rows, :] = packed[:, j * LANES:(j + 1) * LANES]
        x2h = x2.astype(BF16)
        x2l = (x2 - x2h.astype(F32)).astype(BF16)
        a = _dot_nt(wr_ref[...], x2h)
        b = _dot_nt(wr_ref[0:N_EXPERTS, :], x2l)
        lg_ref[:, rows] = a[0:N_EXPERTS] + a[N_EXPERTS:] + b + br_ref[...]


def _xattn(x1, kmem, vmem, wq, wo, g, b, wr, br, batch, seq, n_mem):
    n = x1.shape[0]
    tm = ROW_TILE
    nt = seq // tm
    row = lambda width: pl.BlockSpec((tm, width), lambda bi, i: (bi * nt + i, 0))
    full = lambda a: pl.BlockSpec(a.shape, lambda bi, i: (0, 0))
    kv = pl.BlockSpec((n_mem, D_MODEL), lambda bi, i: (bi, 0))
    return pl.pallas_call(
        _xattn_kernel,
        grid=(batch, nt),
        in_specs=[row(D_MODEL), kv, kv, full(wq), full(wo), full(g), full(b), full(wr), full(br)],
        out_specs=(row(D_MODEL),
                   pl.BlockSpec((X_PLANES, tm, LANES), lambda bi, i: (0, bi * nt + i, 0)),
                   pl.BlockSpec((N_EXPERTS, tm), lambda bi, i: (0, bi * nt + i))),
        out_shape=(jax.ShapeDtypeStruct((n, D_MODEL), F32),
                   jax.ShapeDtypeStruct((X_PLANES, n, LANES), jnp.uint32),
                   jax.ShapeDtypeStruct((N_EXPERTS, n), F32)),
        compiler_params=_params(("parallel", "parallel"), 60),
    )(x1, kmem, vmem, wq, wo, g, b, wr, br)


def _route_kernel(lg_ref, idx_ref, rank_ref, gate_ref, cnt_ref, carry_ref):
    i = pl.program_id(0)

    @pl.when(i == 0)
    def _():
        carry_ref[...] = jnp.zeros_like(carry_ref)

    lg = lg_ref[...]
    t = lg.shape[1]
    e_idx = lax.broadcasted_iota(jnp.int32, lg.shape, 0).astype(F32)
    sels, vals, idxs = [], [], []
    for _ in range(TOP_K):
        mx = jnp.max(lg, axis=0, keepdims=True)
        first = jnp.min(jnp.where(lg == mx, e_idx, float(N_EXPERTS)), axis=0, keepdims=True)
        sel = e_idx == first
        sels.append(sel)
        vals.append(mx)
        idxs.append(first)
        lg = jnp.where(sel, -jnp.inf, lg)
    exps = [jnp.exp(v - vals[0]) for v in vals]
    tot = exps[0] + exps[1] + exps[2] + exps[3]

    selmat = (sels[0] | sels[1] | sels[2] | sels[3])
    r = lax.broadcasted_iota(jnp.int32, (t, t), 0)
    s = lax.broadcasted_iota(jnp.int32, (t, t), 1)
    earlier = (r < s).astype(BF16)
    carry = carry_ref[:, 0:1]
    rankmat = _dot(selmat.astype(BF16), earlier) + carry
    new_carry = carry + jnp.sum(selmat.astype(F32), axis=1, keepdims=True)
    carry_ref[...] = jnp.broadcast_to(new_carry, carry_ref.shape)
    cnt_ref[...] = jnp.broadcast_to(new_carry, cnt_ref.shape).astype(jnp.int32)

    row8 = lax.broadcasted_iota(jnp.int32, (8, t), 0)
    row128 = lax.broadcasted_iota(jnp.int32, (LANES, t), 0)
    idx_out = jnp.zeros((8, t), F32)
    rank_out = jnp.zeros((8, t), F32)
    gate_out = jnp.zeros((LANES, t), F32)
    for k in range(TOP_K):
        rk = jnp.sum(jnp.where(sels[k], rankmat, 0.0), axis=0, keepdims=True)
        idx_out = jnp.where(row8 == k, idxs[k], idx_out)
        rank_out = jnp.where(row8 == k, rk, rank_out)
        gate_out = jnp.where(row128 == k, exps[k] / tot, gate_out)
    idx_ref[...] = idx_out.astype(jnp.int32)
    rank_ref[...] = rank_out.astype(jnp.int32)
    gate_ref[...] = gate_out.T


def _route(logits_t):
    n = logits_t.shape[1]
    t = ROUTE_TILE
    col = lambda rows: pl.BlockSpec((rows, t), lambda i: (0, i))
    return pl.pallas_call(
        _route_kernel,
        grid=(n // t,),
        in_specs=[col(N_EXPERTS)],
        out_specs=(col(8), col(8), pl.BlockSpec((t, LANES), lambda i: (i, 0)),
                   pl.BlockSpec((N_EXPERTS, LANES), lambda i: (0, 0))),
        out_shape=(jax.ShapeDtypeStruct((8, n), jnp.int32),
                   jax.ShapeDtypeStruct((8, n), jnp.int32),
                   jax.ShapeDtypeStruct((n, LANES), F32),
                   jax.ShapeDtypeStruct((N_EXPERTS, LANES), jnp.int32)),
        scratch_shapes=[pltpu.VMEM((N_EXPERTS, LANES), F32)],
        compiler_params=_params(("arbitrary",), 32),
    )(logits_t)


def _expert_kernel(blk_e_ref, blk_rows_ref, next_e_ref, x_ref, wgu_hbm, bgu_ref, wd_hbm, bd_ref,
                   y_ref, wgu_f32, wd_f32, wgu_sc, wd_sc, state_ref, sem):
    i = pl.program_id(0)
    e = blk_e_ref[i]
    rows = blk_rows_ref[i]
    g = y_ref.shape[1]

    def weight_copies(expert, slot):
        return (pltpu.make_async_copy(wgu_hbm.at[expert], wgu_f32.at[slot], sem.at[0, slot]),
                pltpu.make_async_copy(wd_hbm.at[expert], wd_f32.at[slot], sem.at[1, slot]))

    @pl.when(i == 0)
    def _():
        state_ref[0] = -1
        state_ref[1] = 0

    @pl.when((rows > 0) & (e != state_ref[0]))
    def _():
        slot = state_ref[1]

        @pl.when(i == 0)
        def _():
            for cp in weight_copies(e, slot):
                cp.start()

        for cp in weight_copies(e, slot):
            cp.wait()
        wgu_sc[...] = wgu_f32[slot].astype(BF16)
        wd_sc[...] = wd_f32[slot].astype(BF16)
        nxt = next_e_ref[i]

        @pl.when(nxt >= 0)
        def _():
            for cp in weight_copies(nxt, 1 - slot):
                cp.start()

        state_ref[0] = e
        state_ref[1] = 1 - slot

    def ffn(m):
        packed = jnp.concatenate([x_ref[j, 0:m, :] for j in range(X_PLANES)], axis=1)
        xb = _unpack_bf16_pairs(packed)
        hids = []
        for c in range(D_EXPERT // EXPERT_CHUNK):
            g0 = c * EXPERT_CHUNK
            l0 = D_EXPERT + g0
            gate = _dot(xb, wgu_sc[:, g0:g0 + EXPERT_CHUNK]) + bgu_ref[0, :, g0:g0 + EXPERT_CHUNK]
            lin = _dot(xb, wgu_sc[:, l0:l0 + EXPERT_CHUNK]) + bgu_ref[0, :, l0:l0 + EXPERT_CHUNK]
            gate = jnp.minimum(gate, SWIGLU_LIMIT)
            lin = jnp.clip(lin, -SWIGLU_LIMIT, SWIGLU_LIMIT)
            hids.append((gate * jax.nn.sigmoid(SWIGLU_ALPHA * gate) * (lin + 1.0)).astype(BF16))
        y = _pack_bf16_pairs(_dot(jnp.concatenate(hids, axis=1), wd_sc[...]) + bd_ref[0])
        for j in range(Y_PLANES):
            y_ref[j, 0:m, :] = y[:, j * LANES:(j + 1) * LANES]

    @pl.when(rows > g // 2)
    def _():
        ffn(g)

    @pl.when((rows > 0) & (rows <= g // 2))
    def _():
        ffn(g // 2)
        y_ref[:, g // 2:, :] = jnp.zeros((Y_PLANES, g // 2, LANES), y_ref.dtype)

    @pl.when(rows == 0)
    def _():
        y_ref[...] = jnp.zeros_like(y_ref)


def _experts(blk_e, blk_rows, next_e, xs, w_gu, b_gu, w_d, b_d):
    p = xs.shape[1]
    g = EXPERT_ROWS
    grid_spec = pltpu.PrefetchScalarGridSpec(
        num_scalar_prefetch=3,
        grid=(p // g,),
        in_specs=[pl.BlockSpec((X_PLANES, g, LANES), lambda i, be, br, ne: (0, i, 0)),
                  pl.BlockSpec(memory_space=pl.ANY),
                  pl.BlockSpec((1, 1, 2 * D_EXPERT), lambda i, be, br, ne: (be[i], 0, 0)),
                  pl.BlockSpec(memory_space=pl.ANY),
                  pl.BlockSpec((1, 1, D_MODEL), lambda i, be, br, ne: (be[i], 0, 0))],
        out_specs=pl.BlockSpec((Y_PLANES, g, LANES), lambda i, be, br, ne: (0, i, 0)),
        scratch_shapes=[pltpu.VMEM((2, D_MODEL, 2 * D_EXPERT), F32),
                        pltpu.VMEM((2, D_EXPERT, D_MODEL), F32),
                        pltpu.VMEM((D_MODEL, 2 * D_EXPERT), BF16),
                        pltpu.VMEM((D_EXPERT, D_MODEL), BF16),
                        pltpu.SMEM((2,), jnp.int32),
                        pltpu.SemaphoreType.DMA((2, 2))],
    )
    return pl.pallas_call(
        _expert_kernel,
        grid_spec=grid_spec,
        out_shape=jax.ShapeDtypeStruct((Y_PLANES, p, LANES), jnp.uint32),
        compiler_params=_params(("arbitrary",), 56),
    )(blk_e, blk_rows, next_e, xs, w_gu, b_gu, w_d, b_d)


def _combine_kernel(y_ref, gate_ref, x_ref, g_ref, b_ref, o_ref):
    gate = gate_ref[...]
    ff = None
    for k in range(TOP_K):
        packed = jnp.concatenate([y_ref[k, j] for j in range(Y_PLANES)], axis=1)
        yk = _unpack_bf16_pairs(packed, F32) * gate[:, k:k + 1]
        ff = yk if ff is None else ff + yk
    o_ref[...] = _layer_norm(DEEPNORM_ALPHA * x_ref[...] + ff, g_ref[...], b_ref[...])


def _combine(yg, gate, x2, g, b):
    n = x2.shape[0]
    tm = ROW_TILE
    row = lambda width: pl.BlockSpec((tm, width), lambda i: (i, 0))
    full = lambda a: pl.BlockSpec(a.shape, lambda i: (0, 0))
    return pl.pallas_call(
        _combine_kernel,
        grid=(n // tm,),
        in_specs=[pl.BlockSpec((TOP_K, Y_PLANES, tm, LANES), lambda i: (0, 0, i, 0)),
                  row(LANES), row(D_MODEL), full(g), full(b)],
        out_specs=row(D_MODEL),
        out_shape=jax.ShapeDtypeStruct((n, D_MODEL), F32),
        compiler_params=_params(("parallel",), 56),
    )(yg, gate, x2, g, b)


SC_WINDOW = 128
SC_GATHERS_IN_FLIGHT = 2


def _sc_mesh():
    return plsc.VectorSubcoreMesh(core_axis_name="core", subcore_axis_name="subcore")


def _sc_gather(table, idx):
    m = idx.shape[0]
    n_fly = SC_GATHERS_IN_FLIGHT

    @pl.kernel(out_type=jax.ShapeDtypeStruct((m, LANES), table.dtype), mesh=_sc_mesh(),
               scratch_types=[pltpu.SemaphoreType.DMA])
    def gather_kernel(table_hbm, idx_hbm, out_hbm, sem):
        def body(*refs):
            idx_vmem, out_vmem = refs[:n_fly], refs[n_fly]
            copies = [pltpu.async_copy(table_hbm.at[iv.at[0]],
                                       out_vmem.at[pl.ds(u * SC_WINDOW, SC_WINDOW)], sem)
                      for u, iv in enumerate(idx_vmem)]
            for cp in copies:
                cp.wait()

        pltpu.emit_pipeline(
            body,
            grid=(m // (n_fly * SC_WINDOW),),
            in_specs=[pl.BlockSpec((1, SC_WINDOW), lambda i, u=u: (0, n_fly * i + u))
                      for u in range(n_fly)],
            out_specs=[pl.BlockSpec((n_fly * SC_WINDOW, LANES), lambda i: (i, 0))],
            core_axis_name=("core", "subcore"),
            dimension_semantics=(pltpu.PARALLEL,),
        )(*([idx_hbm] * n_fly), out_hbm)

    return gather_kernel(table, idx.reshape(1, m))


def _sc_scatter(src, idx_lists, out_rows):
    m = src.shape[0]
    n_lists = len(idx_lists)

    @pl.kernel(out_type=jax.ShapeDtypeStruct((out_rows, LANES), src.dtype), mesh=_sc_mesh(),
               scratch_types=[pltpu.SemaphoreType.DMA])
    def scatter_kernel(src_hbm, *refs):
        idx_hbm, out_hbm, sem = refs[:n_lists], refs[n_lists], refs[n_lists + 1]

        def body(src_vmem, *idx_vmem):
            copies = [pltpu.async_copy(src_vmem, out_hbm.at[iv.at[0]], sem) for iv in idx_vmem]
            for cp in copies:
                cp.wait()

        pltpu.emit_pipeline(
            body,
            grid=(m // SC_WINDOW,),
            in_specs=[pl.BlockSpec((SC_WINDOW, LANES), lambda i: (i, 0))]
            + [pl.BlockSpec((1, SC_WINDOW), lambda i: (0, i))] * n_lists,
            out_specs=[],
            core_axis_name=("core", "subcore"),
            dimension_semantics=(pltpu.PARALLEL,),
        )(src_hbm, *idx_hbm)

    return scatter_kernel(src, *[ix.reshape(1, m) for ix in idx_lists])


def _layer(x, mem, w_in, fox_f_bias, conv_w, i_bias, f_bias, fox_g, mlstm_g, w_mix_out,
           ln1_g, ln1_b, w_xq, w_xk, w_xv, w_xo, ln2_g, ln2_b, w_router, b_router,
           w_gate_up, b_gate_up, w_down, b_down, ln3_g, ln3_b):
    batch, seq, d = x.shape
    n_mem = mem.shape[1]
    n = batch * seq
    x2d = x.reshape(n, d)

    o_ff = 3 * FOX_WIDTH
    o_mqk = o_ff + FOX_HEADS
    o_mv = o_mqk + 2 * MLSTM_QK_WIDTH
    o_mi = o_mv + MLSTM_V_WIDTH
    o_mf = o_mi + MLSTM_HEADS
    o_mo = o_mf + MLSTM_HEADS
    n_gate = FOX_HEADS + 2 * MLSTM_HEADS
    w_r = jnp.concatenate(
        [w_in[:, FOX_WIDTH:2 * FOX_WIDTH], w_in[:, o_mqk:o_mv],
         w_in[:, o_ff:o_mqk], w_in[:, o_mi:o_mo],
         jnp.zeros((d, LANES - n_gate), w_in.dtype)], axis=1).astype(BF16)
    w_t = jnp.concatenate([w_in[:, :FOX_WIDTH], w_in[:, 2 * FOX_WIDTH:o_ff],
                           w_in[:, o_mv:o_mi], w_in[:, o_mo:]], axis=1).T.astype(BF16)
    gate_bias = jnp.concatenate(
        [fox_f_bias, i_bias, f_bias, jnp.zeros((LANES - n_gate,), F32)]).reshape(1, LANES)

    fqt, fk, fvt, mqk, mvt, mot, gates = _inproj(x2d, w_r, w_t, gate_bias)
    gcol, grow, cpieces = _gateprep(gates, batch, seq)
    fox_gain_lanes = jnp.broadcast_to(fox_g[:, None], (FOX_WIDTH, LANES))
    fo = _fox(fqt, fk, cpieces, fvt, fox_gain_lanes, batch, seq)
    gain_lanes = jnp.broadcast_to(mlstm_g[:, None], (MLSTM_V_WIDTH, LANES))
    mo_out = _mlstm(mqk, mvt, mot, gcol, grow, conv_w, gain_lanes, batch, seq)
    x1 = _outproj(fo, mo_out, w_mix_out.astype(BF16), x2d, ln1_g.reshape(1, d), ln1_b.reshape(1, d))

    kmem, vmem = _memkv(mem.reshape(batch * n_mem, d), w_xk.astype(BF16), w_xv.astype(BF16), n_mem)
    wrt = w_router.T
    wrt_hi = wrt.astype(BF16)
    wrt_lo = (wrt - wrt_hi.astype(F32)).astype(BF16)
    x2, x2p, logits_t = _xattn(x1, kmem, vmem, w_xq.astype(BF16), w_xo.astype(BF16),
                               ln2_g.reshape(1, d), ln2_b.reshape(1, d),
                               jnp.concatenate([wrt_hi, wrt_lo], axis=0),
                               b_router.reshape(N_EXPERTS, 1), batch, seq, n_mem)

    idx_t, rank_t, gate, cnt = _route(logits_t)
    counts = cnt[:, 0]
    g_rows = EXPERT_ROWS
    padded = ((counts + g_rows - 1) // g_rows) * g_rows
    pad_end = jnp.cumsum(padded)
    pad_start = pad_end - padded
    experts = jnp.arange(N_EXPERTS, dtype=jnp.int32)
    sel = idx_t[:TOP_K, :, None] == experts[None, None, :]
    pos_t = jnp.sum(jnp.where(sel, pad_start[None, None, :], 0), axis=-1) + rank_t[:TOP_K]
    p_rows = n * TOP_K + N_EXPERTS * g_rows
    nb = p_rows // g_rows
    blk_start = jnp.arange(nb, dtype=jnp.int32) * g_rows
    blk_e = jnp.minimum(jnp.sum((pad_end[None, :] <= blk_start[:, None]).astype(jnp.int32), axis=1),
                        N_EXPERTS - 1)
    own = blk_e[:, None] == experts[None, :]
    row_end = jnp.sum(jnp.where(own, (pad_start + counts)[None, :], 0), axis=1)
    blk_rows = jnp.clip(row_end - blk_start, 0, g_rows).astype(jnp.int32)
    later = (experts[None, :] > blk_e[:, None]) & (counts[None, :] > 0)
    next_e = jnp.min(jnp.where(later, experts[None, :], N_EXPERTS), axis=1)
    next_e = jnp.where(next_e == N_EXPERTS, -1, next_e).astype(jnp.int32)

    def piece_index(planes):
        off = jnp.arange(planes, dtype=jnp.int32) * p_rows
        return pos_t[:, None, :] + off[None, :, None]

    x_idx = piece_index(X_PLANES).reshape(TOP_K, X_PLANES * n)
    xs = _sc_scatter(x2p.reshape(X_PLANES * n, LANES), [x_idx[k] for k in range(TOP_K)],
                     X_PLANES * p_rows)
    y = _experts(blk_e, blk_rows, next_e, xs.reshape(X_PLANES, p_rows, LANES), w_gate_up,
                 b_gate_up.reshape(N_EXPERTS, 1, -1), w_down, b_down.reshape(N_EXPERTS, 1, -1))
    yg = _sc_gather(y.reshape(Y_PLANES * p_rows, LANES), piece_index(Y_PLANES).reshape(-1))
    out = _combine(yg.reshape(TOP_K, Y_PLANES, n, LANES), gate, x2,
                   ln3_g.reshape(1, d), ln3_b.reshape(1, d))
    return out.reshape(batch, seq, d)


def kernel(x, mem, w_in, fox_f_bias, mlstm_conv_w, mlstm_i_bias, mlstm_f_bias, fox_norm_g, mlstm_norm_g, w_mix_out, ln1_g, ln1_b, w_xq, w_xk, w_xv, w_xo, ln2_g, ln2_b, w_router, b_router, w_gate_up, b_gate_up, w_down, b_down, ln3_g, ln3_b):
    for l in range(w_in.shape[0]):
        x = _layer(x, mem, w_in[l], fox_f_bias[l], mlstm_conv_w[l], mlstm_i_bias[l],
                   mlstm_f_bias[l], fox_norm_g[l], mlstm_norm_g[l], w_mix_out[l],
                   ln1_g[l], ln1_b[l], w_xq[l], w_xk[l], w_xv[l], w_xo[l], ln2_g[l], ln2_b[l],
                   w_router[l], b_router[l], w_gate_up[l], b_gate_up[l], w_down[l], b_down[l],
                   ln3_g[l], ln3_b[l])
    return x
```

```python
import jax
import jax.numpy as jnp
from jax import lax
from jax.experimental import pallas as pl
from jax.experimental.pallas import tpu as pltpu
from jax.experimental.pallas import tpu_sc as plsc

F32 = jnp.float32
BF16 = jnp.bfloat16

D_MODEL = 1024
FOX_HEADS = 8
FOX_HEAD_DIM = 64
FOX_WIDTH = FOX_HEADS * FOX_HEAD_DIM
MLSTM_HEADS = 4
MLSTM_QK_DIM = 64
MLSTM_V_DIM = 128
MLSTM_QK_WIDTH = MLSTM_HEADS * MLSTM_QK_DIM
MLSTM_V_WIDTH = MLSTM_HEADS * MLSTM_V_DIM
CONV_WIDTH = 4
XATTN_HEADS = 4
XATTN_HEAD_DIM = D_MODEL // XATTN_HEADS
N_EXPERTS = 32
TOP_K = 4
D_EXPERT = D_MODEL
SWIGLU_LIMIT = 7.0
SWIGLU_ALPHA = 1.702
DEEPNORM_ALPHA = 2.0 ** 0.25
LN_EPS = 1e-5
RMS_EPS = 1e-6

LANES = 128
SEQ_BLOCK = 256
ROW_TILE = 1024
EXPERT_ROWS = 512
EXPERT_CHUNK = 256
ROUTE_TILE = 512
X_PLANES = D_MODEL // 2 // LANES
Y_PLANES = X_PLANES
GATE_I0 = FOX_HEADS
GATE_F0 = FOX_HEADS + MLSTM_HEADS

MIB = 1024 * 1024


def _params(semantics, vmem_mib):
    return pltpu.CompilerParams(dimension_semantics=semantics,
                                vmem_limit_bytes=vmem_mib * MIB)


def _layer_norm(y, g, b):
    mu = jnp.mean(y, axis=-1, keepdims=True)
    yc = y - mu
    var = jnp.mean(yc * yc, axis=-1, keepdims=True)
    return yc * lax.rsqrt(var + LN_EPS) * g + b


def _dot(a, b):
    return jnp.dot(a, b, preferred_element_type=F32)


def _dot_nt(a, b):
    return lax.dot_general(a, b, (((1,), (1,)), ((), ())), preferred_element_type=F32)


def _pack_bf16_pairs(x):
    w = x.shape[1] // 2
    lo = pltpu.bitcast(x[:, :w].astype(BF16).astype(F32), jnp.uint32)
    hi = pltpu.bitcast(x[:, w:].astype(BF16).astype(F32), jnp.uint32)
    return (lo >> 16) | hi


def _unpack_bf16_pairs(u, dtype=BF16):
    lo = pltpu.bitcast(u << 16, F32).astype(dtype)
    hi = pltpu.bitcast(u & jnp.uint32(0xFFFF0000), F32).astype(dtype)
    return jnp.concatenate([lo, hi], axis=1)


def _inproj_kernel(x_ref, w_ref, wt_ref, gb_ref, fqt_ref, fk_ref, fvt_ref, mqk_ref, mvt_ref,
                   mot_ref, g_ref):
    xb = x_ref[...].astype(BF16)

    def mm(c0, width):
        return _dot(xb, w_ref[:, c0:c0 + width])

    def mm_t(r0, height):
        return _dot_nt(wt_ref[r0:r0 + height, :], xb)

    r_fv = FOX_WIDTH
    r_mv = r_fv + FOX_WIDTH
    r_mo = r_mv + MLSTM_V_WIDTH
    fqt_ref[...] = (mm_t(0, FOX_WIDTH) * (FOX_HEAD_DIM ** -0.5)).astype(BF16)
    fvt_ref[...] = mm_t(r_fv, FOX_WIDTH).astype(BF16)
    mvt_ref[...] = mm_t(r_mv, MLSTM_V_WIDTH).astype(BF16)
    mot_ref[...] = mm_t(r_mo, MLSTM_V_WIDTH)
    c_qk = FOX_WIDTH
    c_g = c_qk + 2 * MLSTM_QK_WIDTH
    fk_ref[...] = mm(0, FOX_WIDTH).astype(BF16)
    mqk_ref[...] = mm(c_qk, 2 * MLSTM_QK_WIDTH)
    g_ref[...] = mm(c_g, LANES) + gb_ref[...]


def _inproj(x2d, w_r, w_t, gate_bias):
    n = x2d.shape[0]
    tm = ROW_TILE
    row = lambda width: pl.BlockSpec((tm, width), lambda i: (i, 0))
    col = lambda height: pl.BlockSpec((height, tm), lambda i: (0, i))
    full = lambda a: pl.BlockSpec(a.shape, lambda i: (0, 0))
    out_shapes = (
        jax.ShapeDtypeStruct((FOX_WIDTH, n), BF16),
        jax.ShapeDtypeStruct((n, FOX_WIDTH), BF16),
        jax.ShapeDtypeStruct((FOX_WIDTH, n), BF16),
        jax.ShapeDtypeStruct((n, 2 * MLSTM_QK_WIDTH), F32),
        jax.ShapeDtypeStruct((MLSTM_V_WIDTH, n), BF16),
        jax.ShapeDtypeStruct((MLSTM_V_WIDTH, n), F32),
        jax.ShapeDtypeStruct((n, LANES), F32),
    )
    return pl.pallas_call(
        _inproj_kernel,
        grid=(n // tm,),
        in_specs=[row(D_MODEL), full(w_r), full(w_t), full(gate_bias)],
        out_specs=(col(FOX_WIDTH), row(FOX_WIDTH), col(FOX_WIDTH), row(2 * MLSTM_QK_WIDTH),
                   col(MLSTM_V_WIDTH), col(MLSTM_V_WIDTH), row(LANES)),
        out_shape=out_shapes,
        compiler_params=_params(("parallel",), 48),
    )(x2d, w_r, w_t, gate_bias)


def _split3(x):
    hi = x.astype(BF16)
    r1 = x - hi.astype(F32)
    mid = r1.astype(BF16)
    lo = (r1 - mid.astype(F32)).astype(BF16)
    return hi, mid, lo


def _gateprep_kernel(g_ref, sel_ref, col_ref, row_ref, cp_ref, carry_ref):
    c = pl.program_id(1)

    @pl.when(c == 0)
    def _():
        carry_ref[...] = jnp.zeros_like(carry_ref)

    blk = SEQ_BLOCK
    lane = lax.broadcasted_iota(jnp.int32, (blk, LANES), 1)
    is_i = (lane >= GATE_I0) & (lane < GATE_F0)
    r = lax.broadcasted_iota(jnp.int32, (blk, blk), 0)
    s = lax.broadcasted_iota(jnp.int32, (blk, blk), 1)
    tri = (s <= r).astype(BF16)
    carry = carry_ref[0:1, :]
    globs = []
    for j in range(g_ref.shape[0] // blk):
        g = g_ref[j * blk:(j + 1) * blk, :]
        logsig = jnp.minimum(g, 0.0) - jnp.log1p(jnp.exp(-jnp.abs(g)))
        cs = None
        for piece in _split3(logsig):
            term = _dot(tri, piece)
            cs = term if cs is None else cs + term
        glob = cs + carry
        carry = glob[blk - 1:blk, :]
        out = jnp.where(lane < GATE_I0, glob, jnp.where(is_i, g, cs))
        col_ref[j * blk:(j + 1) * blk, :] = out
        row_ref[0, :, j * blk:(j + 1) * blk] = out.T[0:16, :]
        globs.append(glob)
    carry_ref[...] = jnp.broadcast_to(carry, carry_ref.shape)
    glob = jnp.concatenate(globs, axis=0)

    pieces = jnp.concatenate(_split3(-glob), axis=1)
    moved = _dot(pieces, sel_ref[...])
    for p in range(FOX_HEADS // 2):
        cp_ref[0, p] = moved[:, p * LANES:(p + 1) * LANES].astype(BF16)


def _piece_selector():
    src = jnp.arange(3 * LANES, dtype=jnp.int32)[:, None]
    dst = jnp.arange(4 * LANES, dtype=jnp.int32)[None, :]
    piece, head = src // LANES, src % LANES
    pair, lane = dst // LANES, dst % LANES
    hit = (lane < 6) & (lane % 3 == piece) & (head == 2 * pair + lane // 3)
    return hit.astype(BF16)


def _gateprep(gates, batch, seq):
    n = gates.shape[0]
    rows = ROW_TILE
    nc = seq // rows
    sel = _piece_selector()
    return pl.pallas_call(
        _gateprep_kernel,
        grid=(batch, nc),
        in_specs=[pl.BlockSpec((rows, LANES), lambda b, c: (b * nc + c, 0)),
                  pl.BlockSpec(sel.shape, lambda b, c: (0, 0))],
        out_specs=(pl.BlockSpec((rows, LANES), lambda b, c: (b * nc + c, 0)),
                   pl.BlockSpec((1, 16, rows), lambda b, c: (b, 0, c)),
                   pl.BlockSpec((1, FOX_HEADS // 2, rows, LANES), lambda b, c: (b, 0, c, 0))),
        out_shape=(jax.ShapeDtypeStruct((n, LANES), F32),
                   jax.ShapeDtypeStruct((batch, 16, seq), F32),
                   jax.ShapeDtypeStruct((batch, FOX_HEADS // 2, seq, LANES), BF16)),
        scratch_shapes=[pltpu.VMEM((8, LANES), F32)],
        compiler_params=_params(("parallel", "arbitrary"), 32),
    )(gates, sel)


FOX_ONES_ROWS = 16
FOX_VAUG_ROWS = FOX_HEAD_DIM + FOX_ONES_ROWS
FOX_QUERY_TILE = 2 * SEQ_BLOCK


def _fox_kernel(qt_ref, k_ref, cp_ref, vt_ref, gain_ref, o_ref,
                kaug_sc, vaug_sc, qaug_sc, sta_sc, stb_sc, stc_sc, m_sc, acc0_sc, acc1_sc):
    qi = pl.program_id(2)
    tq = o_ref.shape[0]
    tg = tq
    tk = SEQ_BLOCK
    hd = FOX_HEAD_DIM
    seq = k_ref.shape[0]
    nq = seq // tq
    par = qi % 2

    def load_queries(tile, slot):
        q0 = pl.multiple_of(tile * tq, tq)
        for h in range(2):
            qaug_sc[slot, h, 0:hd, :] = qt_ref[h * hd:(h + 1) * hd, pl.ds(q0, tq)]

    @pl.when(qi == 0)
    def _():
        ones3 = (lax.broadcasted_iota(jnp.int32, (FOX_ONES_ROWS, tq), 0) < 3).astype(BF16)
        for slot in range(2):
            for h in range(2):
                qaug_sc[slot, h, hd:hd + FOX_ONES_ROWS, :] = ones3
                qaug_sc[slot, h, hd + FOX_ONES_ROWS:, :] = jnp.zeros(
                    (LANES - hd - FOX_ONES_ROWS, tq), BF16)
        load_queries(0, 0)

    nxt = jnp.minimum(qi + 1, nq - 1)
    load_queries(nxt, 1 - par)

    @pl.when(qi == 0)
    def _():
        lane = lax.broadcasted_iota(jnp.int32, (tk, LANES), 1)

        def build(blk, carry):
            r0 = pl.multiple_of(blk * tk, tk)
            kp = k_ref[pl.ds(r0, tk), :].astype(F32)
            cp = cp_ref[0, 0, pl.ds(r0, tk), :].astype(F32)
            for h in range(2):
                kh = kp if h == 0 else pltpu.roll(kp, hd, axis=1)
                ch = pltpu.roll(cp, hd - 3 * h, axis=1)
                kaug = jnp.where(lane < hd, kh, jnp.where(lane < hd + 3, ch, 0.0))
                kaug_sc[h, pl.ds(r0, tk), :] = kaug.astype(BF16)
            return carry

        lax.fori_loop(0, seq // tk, build, 0)
        for h in range(2):
            vaug_sc[h, 0:hd, :] = vt_ref[h * hd:(h + 1) * hd, :]
            vaug_sc[h, hd:, :] = jnp.ones((FOX_ONES_ROWS, seq), BF16)

    def put_scores(kg, slot, masked=False, queries=None):
        k0 = pl.multiple_of(kg * tg, tg)
        queries = par if queries is None else queries
        for h in range(2):
            st = _dot(kaug_sc[h, pl.ds(k0, tg), :], qaug_sc[queries, h])
            if masked:
                kk = lax.broadcasted_iota(jnp.int32, (tg, tq), 0)
                tt = lax.broadcasted_iota(jnp.int32, (tg, tq), 1)
                st = jnp.where(kk <= tt, st, -jnp.inf)
            slot[h] = st

    m_sc[...] = jnp.full(m_sc.shape, -jnp.inf, F32)
    acc0_sc[...] = jnp.zeros(acc0_sc.shape, F32)
    acc1_sc[...] = jnp.zeros(acc1_sc.shape, F32)
    acc = (acc0_sc, acc1_sc)

    def absorb(kg, slot):
        k0 = pl.multiple_of(kg * tg, tg)
        half = tg // 2
        for h in range(2):
            st = slot[h]
            m_prev = m_sc[h]
            m_new = jnp.maximum(m_prev, jnp.max(st, axis=0, keepdims=True))
            pv = None
            for u in range(2):
                p = jnp.exp(st[u * half:(u + 1) * half] - m_new)
                part = _dot(vaug_sc[h, :, pl.ds(k0 + u * half, half)], p.astype(BF16))
                pv = part if pv is None else pv + part
            acc[h][...] = jnp.exp(m_prev - m_new) * acc[h][...] + pv
            m_sc[h] = m_new

    sa, sb, sc = sta_sc, stb_sc, stc_sc
    n_loop = jnp.maximum(qi - 1, 0) // 2

    def body(j, carry):
        put_scores(2 * j + 1, sb)
        absorb(2 * j, sa)
        put_scores(2 * j + 2, sa)
        absorb(2 * j + 1, sb)
        return carry

    lax.fori_loop(0, n_loop, body, 0)
    done = 2 * n_loop

    def put_next():
        put_scores(0, sa, queries=1 - par)

    @pl.when(qi == 0)
    def _():
        put_scores(qi, sc, masked=True)
        put_next()
        absorb(qi, sc)

    @pl.when((qi > 0) & (qi - done == 1))
    def _():
        put_scores(qi, sc, masked=True)
        absorb(done, sa)
        put_next()
        absorb(qi, sc)

    @pl.when((qi > 0) & (qi - done == 2))
    def _():
        put_scores(done + 1, sb)
        absorb(done, sa)
        put_scores(qi, sc, masked=True)
        absorb(done + 1, sb)
        put_next()
        absorb(qi, sc)

    normed = []
    for h in range(2):
        a = acc[h][...]
        oh = a[0:hd] * (1.0 / a[hd:hd + 1])
        ms = jnp.mean(oh * oh, axis=0, keepdims=True)
        gain = gain_ref[h * hd:(h + 1) * hd, :]
        normed.append(oh * lax.rsqrt(ms + RMS_EPS) * jnp.concatenate([gain] * (tq // LANES), axis=1))
    o_ref[...] = jnp.concatenate(normed, axis=0).T.astype(o_ref.dtype)


def _fox(fqt, fk, cpieces, fvt, fox_gain, batch, seq):
    n = fk.shape[0]
    tq = FOX_QUERY_TILE
    nq = seq // tq
    npair = FOX_HEADS // 2
    return pl.pallas_call(
        _fox_kernel,
        grid=(batch, npair, nq),
        in_specs=[pl.BlockSpec((LANES, seq), lambda b, hp, qi: (hp, b)),
                  pl.BlockSpec((seq, LANES), lambda b, hp, qi: (b, hp)),
                  pl.BlockSpec((1, 1, seq, LANES), lambda b, hp, qi: (b, hp, 0, 0)),
                  pl.BlockSpec((LANES, seq), lambda b, hp, qi: (hp, b)),
                  pl.BlockSpec((LANES, LANES), lambda b, hp, qi: (hp, 0))],
        out_specs=pl.BlockSpec((tq, LANES), lambda b, hp, qi: (b * nq + qi, hp)),
        out_shape=jax.ShapeDtypeStruct((n, FOX_WIDTH), BF16),
        scratch_shapes=[pltpu.VMEM((2, seq, LANES), BF16),
                        pltpu.VMEM((2, FOX_VAUG_ROWS, seq), BF16),
                        pltpu.VMEM((2, 2, LANES, tq), BF16),
                        pltpu.VMEM((2, tq, tq), F32),
                        pltpu.VMEM((2, tq, tq), F32),
                        pltpu.VMEM((2, tq, tq), F32),
                        pltpu.VMEM((2, 1, tq), F32),
                        pltpu.VMEM((FOX_VAUG_ROWS, tq), F32),
                        pltpu.VMEM((FOX_VAUG_ROWS, tq), F32)],
        compiler_params=_params(("parallel", "parallel", "arbitrary"), 48),
    )(fqt, fk, cpieces, fvt, fox_gain)


MLSTM_ONES_ROWS = 16


def _mlstm_kernel(qk_ref, vt_ref, ogt_ref, col_ref, row_ref, cw_ref, gain_ref, o_ref,
                  tail_ref, buf_ref, c_sc, m_sc):
    c = pl.program_id(1)
    L = SEQ_BLOCK

    @pl.when(c == 0)
    def _():
        tail_ref[...] = jnp.zeros_like(tail_ref)
        c_sc[...] = jnp.zeros_like(c_sc)
        m_sc[...] = jnp.zeros_like(m_sc)

    rows_step = qk_ref.shape[0]
    x = qk_ref[...]
    buf_ref[0:8, :] = tail_ref[...]
    buf_ref[8:8 + rows_step, :] = x
    tail_ref[...] = x[rows_step - 8:rows_step, :]
    y = x * cw_ref[CONV_WIDTH - 1:CONV_WIDTH, :]
    for j in range(CONV_WIDTH - 1):
        shift = CONV_WIDTH - 1 - j
        y = y + buf_ref[8 - shift:8 - shift + rows_step, :] * cw_ref[j:j + 1, :]
    y = y * jax.nn.sigmoid(y)
    n_pair = MLSTM_HEADS // 2

    lane = lax.broadcasted_iota(jnp.int32, (L, LANES), 1)
    sub = lax.broadcasted_iota(jnp.int32, (LANES, L), 0)
    ss = lax.broadcasted_iota(jnp.int32, (L, L), 0)
    ll = lax.broadcasted_iota(jnp.int32, (L, L), 1)
    causal = ss <= ll
    ones_rows = jnp.ones((MLSTM_ONES_ROWS, L), BF16)
    cstates = [c_sc[h] for h in range(MLSTM_HEADS)]
    mstates = [m_sc[h][0:1, 0:1] for h in range(MLSTM_HEADS)]

    per_chunk = []
    for chunk in range(rows_step // L):
        r0 = chunk * L
        yc = y[r0:r0 + L, :]
        kf = yc[:, MLSTM_QK_WIDTH:] * (MLSTM_QK_DIM ** -0.5)
        per_chunk.append((
            [yc[:, p * LANES:(p + 1) * LANES].T for p in range(n_pair)],
            [kf[:, p * LANES:(p + 1) * LANES].astype(BF16) for p in range(n_pair)],
            col_ref[r0:r0 + L, :],
            row_ref[0, :, r0:r0 + L]))

    for chunk, h in [(c_, h_) for c_ in range(rows_step // L) for h_ in range(MLSTM_HEADS)]:
        r0 = chunk * L
        qt, kb, col, row = per_chunk[chunk]
        pair, half = divmod(h, 2)
        head_sub = (sub < MLSTM_QK_DIM) if half == 0 else (sub >= MLSTM_QK_DIM)
        head_lane = (lane < MLSTM_QK_DIM) if half == 0 else (lane >= MLSTM_QK_DIM)
        qth = jnp.where(head_sub, qt[pair], 0.0).astype(BF16)
        rcol = col[:, GATE_I0 + h:GATE_I0 + h + 1] - col[:, GATE_F0 + h:GATE_F0 + h + 1]
        brow = row[GATE_F0 + h:GATE_F0 + h + 1, :]
        lirow = row[GATE_I0 + h:GATE_I0 + h + 1, :]
        g = brow[:, L - 1:L]
        m_prev = mstates[h]

        dt = jnp.where(causal, rcol + brow, -jnp.inf)
        inter_log = brow + m_prev
        m_t = jnp.maximum(inter_log, jnp.max(dt, axis=0, keepdims=True))
        w_inter = jnp.exp(inter_log - m_t)
        pt = jnp.exp(dt - m_t) * _dot(kb[pair], qth)
        vaug = jnp.concatenate([vt_ref[h * LANES:(h + 1) * LANES, r0:r0 + L], ones_rows],
                               axis=0)
        cstate = cstates[h]
        tot = w_inter * _dot(cstate.astype(BF16), qth) + _dot(vaug, pt.astype(BF16))
        den = tot[MLSTM_V_DIM:MLSTM_V_DIM + 1]
        hout = tot[0:MLSTM_V_DIM] * (1.0 / jnp.maximum(jnp.abs(den), jnp.exp(-m_t)))

        a = g + (lirow - brow)
        m_loc = jnp.max(a, axis=1, keepdims=True)
        vw = (vaug.astype(F32) * jnp.exp(a - m_loc)).astype(BF16)
        kmask = jnp.where(head_lane, kb[pair], jnp.zeros_like(kb[pair]))
        kv = _dot(vw, kmask)
        m_new = jnp.maximum(g + m_prev, m_loc)
        cstates[h] = jnp.exp(g + m_prev - m_new) * cstate + jnp.exp(m_loc - m_new) * kv
        mstates[h] = m_new

        ms = jnp.mean(hout * hout, axis=0, keepdims=True)
        gain = gain_ref[h * LANES:(h + 1) * LANES, :]
        hn = hout * lax.rsqrt(ms + RMS_EPS) * jnp.concatenate([gain] * (L // LANES), axis=1)
        gate = jax.nn.sigmoid(ogt_ref[h * LANES:(h + 1) * LANES, r0:r0 + L])
        o_ref[r0:r0 + L, h * LANES:(h + 1) * LANES] = (hn * gate).T.astype(o_ref.dtype)

    for h in range(MLSTM_HEADS):
        c_sc[h] = cstates[h]
        m_sc[h] = jnp.broadcast_to(mstates[h], m_sc.shape[1:])


def _mlstm(mqk, mvt, mot, gcol, grow, conv_w, gain_lanes, batch, seq):
    n = mqk.shape[0]
    L = ROW_TILE
    nc = seq // L
    qk_width = 2 * MLSTM_QK_WIDTH
    row = lambda width: pl.BlockSpec((L, width), lambda b, c: (b * nc + c, 0))
    col = pl.BlockSpec((MLSTM_V_WIDTH, L), lambda b, c: (0, b * nc + c))
    full = lambda a: pl.BlockSpec(a.shape, lambda b, c: (0, 0))
    return pl.pallas_call(
        _mlstm_kernel,
        grid=(batch, nc),
        in_specs=[row(qk_width), col, col, row(LANES),
                  pl.BlockSpec((1, 16, L), lambda b, c: (b, 0, c)),
                  full(conv_w), full(gain_lanes)],
        out_specs=row(MLSTM_V_WIDTH),
        out_shape=jax.ShapeDtypeStruct((n, MLSTM_V_WIDTH), BF16),
        scratch_shapes=[pltpu.VMEM((8, qk_width), F32), pltpu.VMEM((8 + L, qk_width), F32),
                        pltpu.VMEM((MLSTM_HEADS, MLSTM_V_DIM + MLSTM_ONES_ROWS, LANES), F32),
                        pltpu.VMEM((MLSTM_HEADS, 8, LANES), F32)],
        compiler_params=_params(("parallel", "arbitrary"), 48),
    )(mqk, mvt, mot, gcol, grow, conv_w, gain_lanes)


def _outproj_kernel(fo_ref, mo_ref, w_ref, x_ref, g_ref, b_ref, o_ref):
    mix = _dot(fo_ref[...], w_ref[0:FOX_WIDTH, :]) + _dot(mo_ref[...], w_ref[FOX_WIDTH:, :])
    o_ref[...] = _layer_norm(DEEPNORM_ALPHA * x_ref[...] + mix, g_ref[...], b_ref[...])


def _outproj(fo, mo, w_out, x2d, g, b):
    n = x2d.shape[0]
    tm = ROW_TILE
    row = lambda width: pl.BlockSpec((tm, width), lambda i: (i, 0))
    full = lambda a: pl.BlockSpec(a.shape, lambda i: (0, 0))
    return pl.pallas_call(
        _outproj_kernel,
        grid=(n // tm,),
        in_specs=[row(512), row(512), full(w_out), row(D_MODEL), full(g), full(b)],
        out_specs=row(D_MODEL),
        out_shape=jax.ShapeDtypeStruct((n, D_MODEL), F32),
        compiler_params=_params(("parallel",), 48),
    )(fo, mo, w_out, x2d, g, b)


def _memkv_kernel(mem_ref, wk_ref, wv_ref, k_ref, v_ref):
    mb = mem_ref[...].astype(BF16)
    k_ref[...] = (_dot(mb, wk_ref[...]) * (XATTN_HEAD_DIM ** -0.5)).astype(BF16)
    v_ref[...] = _dot(mb, wv_ref[...]).astype(BF16)


def _memkv(mem2d, wk, wv, n_mem):
    n = mem2d.shape[0]
    row = pl.BlockSpec((n_mem, D_MODEL), lambda i: (i, 0))
    full = lambda a: pl.BlockSpec(a.shape, lambda i: (0, 0))
    return pl.pallas_call(
        _memkv_kernel,
        grid=(n // n_mem,),
        in_specs=[row, full(wk), full(wv)],
        out_specs=(row, row),
        out_shape=(jax.ShapeDtypeStruct((n, D_MODEL), BF16),) * 2,
        compiler_params=_params(("parallel",), 32),
    )(mem2d, wk, wv)


def _xattn_kernel(x_ref, k_ref, v_ref, wq_ref, wo_ref, g_ref, b_ref, wr_ref, br_ref,
                  o_ref, ob_ref, lg_ref):
    half = x_ref.shape[0] // 2
    for r in range(2):
        rows = slice(r * half, (r + 1) * half)
        x = x_ref[rows, :]
        q = _dot(x.astype(BF16), wq_ref[...]).astype(BF16)
        outs = []
        for h in range(XATTN_HEADS):
            sl = slice(h * XATTN_HEAD_DIM, (h + 1) * XATTN_HEAD_DIM)
            s = _dot_nt(q[:, sl], k_ref[:, sl])
            p = jnp.exp(s - jnp.max(s, axis=-1, keepdims=True))
            l = jnp.sum(p, axis=-1, keepdims=True)
            outs.append((_dot(p.astype(BF16), v_ref[:, sl]) / l).astype(BF16))
        o = jnp.concatenate(outs, axis=1)
        xa = _dot(o, wo_ref[...])
        x2 = _layer_norm(DEEPNORM_ALPHA * x + xa, g_ref[...], b_ref[...])
        o_ref[rows, :] = x2
        packed = _pack_bf16_pairs(x2)
        for j in range(ob_ref.shape[0]):
            ob_ref[j, rows, :] = packed[:, j * LANES:(j + 1) * LANES]
        x2h = x2.astype(BF16)
        x2l = (x2 - x2h.astype(F32)).astype(BF16)
        a = _dot_nt(wr_ref[...], x2h)
        b = _dot_nt(wr_ref[0:N_EXPERTS, :], x2l)
        lg_ref[:, rows] = a[0:N_EXPERTS] + a[N_EXPERTS:] + b + br_ref[...]


def _xattn(x1, kmem, vmem, wq, wo, g, b, wr, br, batch, seq, n_mem):
    n = x1.shape[0]
    tm = ROW_TILE
    nt = seq // tm
    row = lambda width: pl.BlockSpec((tm, width), lambda bi, i: (bi * nt + i, 0))
    full = lambda a: pl.BlockSpec(a.shape, lambda bi, i: (0, 0))
    kv = pl.BlockSpec((n_mem, D_MODEL), lambda bi, i: (bi, 0))
    return pl.pallas_call(
        _xattn_kernel,
        grid=(batch, nt),
        in_specs=[row(D_MODEL), kv, kv, full(wq), full(wo), full(g), full(b), full(wr), full(br)],
        out_specs=(row(D_MODEL),
                   pl.BlockSpec((X_PLANES, tm, LANES), lambda bi, i: (0, bi * nt + i, 0)),
                   pl.BlockSpec((N_EXPERTS, tm), lambda bi, i: (0, bi * nt + i))),
        out_shape=(jax.ShapeDtypeStruct((n, D_MODEL), F32),
                   jax.ShapeDtypeStruct((X_PLANES, n, LANES), jnp.uint32),
                   jax.ShapeDtypeStruct((N_EXPERTS, n), F32)),
        compiler_params=_params(("parallel", "parallel"), 60),
    )(x1, kmem, vmem, wq, wo, g, b, wr, br)


def _route_kernel(lg_ref, idx_ref, rank_ref, gate_ref, cnt_ref, carry_ref):
    i = pl.program_id(0)

    @pl.when(i == 0)
    def _():
        carry_ref[...] = jnp.zeros_like(carry_ref)

    lg = lg_ref[...]
    t = lg.shape[1]
    e_idx = lax.broadcasted_iota(jnp.int32, lg.shape, 0).astype(F32)
    sels, vals, idxs = [], [], []
    for _ in range(TOP_K):
        mx = jnp.max(lg, axis=0, keepdims=True)
        first = jnp.min(jnp.where(lg == mx, e_idx, float(N_EXPERTS)), axis=0, keepdims=True)
        sel = e_idx == first
        sels.append(sel)
        vals.append(mx)
        idxs.append(first)
        lg = jnp.where(sel, -jnp.inf, lg)
    exps = [jnp.exp(v - vals[0]) for v in vals]
    tot = exps[0] + exps[1] + exps[2] + exps[3]

    selmat = (sels[0] | sels[1] | sels[2] | sels[3])
    r = lax.broadcasted_iota(jnp.int32, (t, t), 0)
    s = lax.broadcasted_iota(jnp.int32, (t, t), 1)
    earlier = (r < s).astype(BF16)
    carry = carry_ref[:, 0:1]
    rankmat = _dot(selmat.astype(BF16), earlier) + carry
    new_carry = carry + jnp.sum(selmat.astype(F32), axis=1, keepdims=True)
    carry_ref[...] = jnp.broadcast_to(new_carry, carry_ref.shape)
    cnt_ref[...] = jnp.broadcast_to(new_carry, cnt_ref.shape).astype(jnp.int32)

    row8 = lax.broadcasted_iota(jnp.int32, (8, t), 0)
    row128 = lax.broadcasted_iota(jnp.int32, (LANES, t), 0)
    idx_out = jnp.zeros((8, t), F32)
    rank_out = jnp.zeros((8, t), F32)
    gate_out = jnp.zeros((LANES, t), F32)
    for k in range(TOP_K):
        rk = jnp.sum(jnp.where(sels[k], rankmat, 0.0), axis=0, keepdims=True)
        idx_out = jnp.where(row8 == k, idxs[k], idx_out)
        rank_out = jnp.where(row8 == k, rk, rank_out)
        gate_out = jnp.where(row128 == k, exps[k] / tot, gate_out)
    idx_ref[...] = idx_out.astype(jnp.int32)
    rank_ref[...] = rank_out.astype(jnp.int32)
    gate_ref[...] = gate_out.T


def _route(logits_t):
    n = logits_t.shape[1]
    t = ROUTE_TILE
    col = lambda rows: pl.BlockSpec((rows, t), lambda i: (0, i))
    return pl.pallas_call(
        _route_kernel,
        grid=(n // t,),
        in_specs=[col(N_EXPERTS)],
        out_specs=(col(8), col(8), pl.BlockSpec((t, LANES), lambda i: (i, 0)),
                   pl.BlockSpec((N_EXPERTS, LANES), lambda i: (0, 0))),
        out_shape=(jax.ShapeDtypeStruct((8, n), jnp.int32),
                   jax.ShapeDtypeStruct((8, n), jnp.int32),
                   jax.ShapeDtypeStruct((n, LANES), F32),
                   jax.ShapeDtypeStruct((N_EXPERTS, LANES), jnp.int32)),
        scratch_shapes=[pltpu.VMEM((N_EXPERTS, LANES), F32)],
        compiler_params=_params(("arbitrary",), 32),
    )(logits_t)


def _expert_kernel(blk_e_ref, blk_rows_ref, next_e_ref, x_ref, wgu_hbm, bgu_ref, wd_hbm, bd_ref,
                   y_ref, wgu_f32, wd_f32, wgu_sc, wd_sc, state_ref, sem):
    i = pl.program_id(0)
    e = blk_e_ref[i]
    rows = blk_rows_ref[i]
    g = y_ref.shape[1]

    def weight_copies(expert, slot):
        return (pltpu.make_async_copy(wgu_hbm.at[expert], wgu_f32.at[slot], sem.at[0, slot]),
                pltpu.make_async_copy(wd_hbm.at[expert], wd_f32.at[slot], sem.at[1, slot]))

    @pl.when(i == 0)
    def _():
        state_ref[0] = -1
        state_ref[1] = 0

    @pl.when((rows > 0) & (e != state_ref[0]))
    def _():
        slot = state_ref[1]

        @pl.when(i == 0)
        def _():
            for cp in weight_copies(e, slot):
                cp.start()

        for cp in weight_copies(e, slot):
            cp.wait()
        wgu_sc[...] = wgu_f32[slot].astype(BF16)
        wd_sc[...] = wd_f32[slot].astype(BF16)
        nxt = next_e_ref[i]

        @pl.when(nxt >= 0)
        def _():
            for cp in weight_copies(nxt, 1 - slot):
                cp.start()

        state_ref[0] = e
        state_ref[1] = 1 - slot

    def ffn(m):
        packed = jnp.concatenate([x_ref[j, 0:m, :] for j in range(X_PLANES)], axis=1)
        xb = _unpack_bf16_pairs(packed)
        hids = []
        for c in range(D_EXPERT // EXPERT_CHUNK):
            g0 = c * EXPERT_CHUNK
            l0 = D_EXPERT + g0
            gate = _dot(xb, wgu_sc[:, g0:g0 + EXPERT_CHUNK]) + bgu_ref[0, :, g0:g0 + EXPERT_CHUNK]
            lin = _dot(xb, wgu_sc[:, l0:l0 + EXPERT_CHUNK]) + bgu_ref[0, :, l0:l0 + EXPERT_CHUNK]
            gate = jnp.minimum(gate, SWIGLU_LIMIT)
            lin = jnp.clip(lin, -SWIGLU_LIMIT, SWIGLU_LIMIT)
            hids.append((gate * jax.nn.sigmoid(SWIGLU_ALPHA * gate) * (lin + 1.0)).astype(BF16))
        y = _pack_bf16_pairs(_dot(jnp.concatenate(hids, axis=1), wd_sc[...]) + bd_ref[0])
        for j in range(Y_PLANES):
            y_ref[j, 0:m, :] = y[:, j * LANES:(j + 1) * LANES]

    @pl.when(rows > g // 2)
    def _():
        ffn(g)

    @pl.when((rows > 0) & (rows <= g // 2))
    def _():
        ffn(g // 2)
        y_ref[:, g // 2:, :] = jnp.zeros((Y_PLANES, g // 2, LANES), y_ref.dtype)

    @pl.when(rows == 0)
    def _():
        y_ref[...] = jnp.zeros_like(y_ref)


def _experts(blk_e, blk_rows, next_e, xs, w_gu, b_gu, w_d, b_d):
    p = xs.shape[1]
    g = EXPERT_ROWS
    grid_spec = pltpu.PrefetchScalarGridSpec(
        num_scalar_prefetch=3,
        grid=(p // g,),
        in_specs=[pl.BlockSpec((X_PLANES, g, LANES), lambda i, be, br, ne: (0, i, 0)),
                  pl.BlockSpec(memory_space=pl.ANY),
                  pl.BlockSpec((1, 1, 2 * D_EXPERT), lambda i, be, br, ne: (be[i], 0, 0)),
                  pl.BlockSpec(memory_space=pl.ANY),
                  pl.BlockSpec((1, 1, D_MODEL), lambda i, be, br, ne: (be[i], 0, 0))],
        out_specs=pl.BlockSpec((Y_PLANES, g, LANES), lambda i, be, br, ne: (0, i, 0)),
        scratch_shapes=[pltpu.VMEM((2, D_MODEL, 2 * D_EXPERT), F32),
                        pltpu.VMEM((2, D_EXPERT, D_MODEL), F32),
                        pltpu.VMEM((D_MODEL, 2 * D_EXPERT), BF16),
                        pltpu.VMEM((D_EXPERT, D_MODEL), BF16),
                        pltpu.SMEM((2,), jnp.int32),
                        pltpu.SemaphoreType.DMA((2, 2))],
    )
    return pl.pallas_call(
        _expert_kernel,
        grid_spec=grid_spec,
        out_shape=jax.ShapeDtypeStruct((Y_PLANES, p, LANES), jnp.uint32),
        compiler_params=_params(("arbitrary",), 56),
    )(blk_e, blk_rows, next_e, xs, w_gu, b_gu, w_d, b_d)


def _combine_kernel(y_ref, gate_ref, x_ref, g_ref, b_ref, o_ref):
    gate = gate_ref[...]
    ff = None
    for k in range(TOP_K):
        packed = jnp.concatenate([y_ref[k, j] for j in range(Y_PLANES)], axis=1)
        yk = _unpack_bf16_pairs(packed, F32) * gate[:, k:k + 1]
        ff = yk if ff is None else ff + yk
    o_ref[...] = _layer_norm(DEEPNORM_ALPHA * x_ref[...] + ff, g_ref[...], b_ref[...])


def _combine(yg, gate, x2, g, b):
    n = x2.shape[0]
    tm = ROW_TILE
    row = lambda width: pl.BlockSpec((tm, width), lambda i: (i, 0))
    full = lambda a: pl.BlockSpec(a.shape, lambda i: (0, 0))
    return pl.pallas_call(
        _combine_kernel,
        grid=(n // tm,),
        in_specs=[pl.BlockSpec((TOP_K, Y_PLANES, tm, LANES), lambda i: (0, 0, i, 0)),
                  row(LANES), row(D_MODEL), full(g), full(b)],
        out_specs=row(D_MODEL),
        out_shape=jax.ShapeDtypeStruct((n, D_MODEL), F32),
        compiler_params=_params(("parallel",), 56),
    )(yg, gate, x2, g, b)


SC_WINDOW = 128
SC_GATHERS_IN_FLIGHT = 2


def _sc_mesh():
    return plsc.VectorSubcoreMesh(core_axis_name="core", subcore_axis_name="subcore")


def _sc_gather(table, idx):
    m = idx.shape[0]
    n_fly = SC_GATHERS_IN_FLIGHT

    @pl.kernel(out_type=jax.ShapeDtypeStruct((m, LANES), table.dtype), mesh=_sc_mesh(),
               scratch_types=[pltpu.SemaphoreType.DMA])
    def gather_kernel(table_hbm, idx_hbm, out_hbm, sem):
        def body(*refs):
            idx_vmem, out_vmem = refs[:n_fly], refs[n_fly]
            copies = [pltpu.async_copy(table_hbm.at[iv.at[0]],
                                       out_vmem.at[pl.ds(u * SC_WINDOW, SC_WINDOW)], sem)
                      for u, iv in enumerate(idx_vmem)]
            for cp in copies:
                cp.wait()

        pltpu.emit_pipeline(
            body,
            grid=(m // (n_fly * SC_WINDOW),),
            in_specs=[pl.BlockSpec((1, SC_WINDOW), lambda i, u=u: (0, n_fly * i + u))
                      for u in range(n_fly)],
            out_specs=[pl.BlockSpec((n_fly * SC_WINDOW, LANES), lambda i: (i, 0))],
            core_axis_name=("core", "subcore"),
            dimension_semantics=(pltpu.PARALLEL,),
        )(*([idx_hbm] * n_fly), out_hbm)

    return gather_kernel(table, idx.reshape(1, m))


def _sc_scatter(src, idx_lists, out_rows):
    m = src.shape[0]
    n_lists = len(idx_lists)

    @pl.kernel(out_type=jax.ShapeDtypeStruct((out_rows, LANES), src.dtype), mesh=_sc_mesh(),
               scratch_types=[pltpu.SemaphoreType.DMA])
    def scatter_kernel(src_hbm, *refs):
        idx_hbm, out_hbm, sem = refs[:n_lists], refs[n_lists], refs[n_lists + 1]

        def body(src_vmem, *idx_vmem):
            copies = [pltpu.async_copy(src_vmem, out_hbm.at[iv.at[0]], sem) for iv in idx_vmem]
            for cp in copies:
                cp.wait()

        pltpu.emit_pipeline(
            body,
            grid=(m // SC_WINDOW,),
            in_specs=[pl.BlockSpec((SC_WINDOW, LANES), lambda i: (i, 0))]
            + [pl.BlockSpec((1, SC_WINDOW), lambda i: (0, i))] * n_lists,
            out_specs=[],
            core_axis_name=("core", "subcore"),
            dimension_semantics=(pltpu.PARALLEL,),
        )(src_hbm, *idx_hbm)

    return scatter_kernel(src, *[ix.reshape(1, m) for ix in idx_lists])


def _layer(x, mem, w_in, fox_f_bias, conv_w, i_bias, f_bias, fox_g, mlstm_g, w_mix_out,
           ln1_g, ln1_b, w_xq, w_xk, w_xv, w_xo, ln2_g, ln2_b, w_router, b_router,
           w_gate_up, b_gate_up, w_down, b_down, ln3_g, ln3_b):
    batch, seq, d = x.shape
    n_mem = mem.shape[1]
    n = batch * seq
    x2d = x.reshape(n, d)

    o_ff = 3 * FOX_WIDTH
    o_mqk = o_ff + FOX_HEADS
    o_mv = o_mqk + 2 * MLSTM_QK_WIDTH
    o_mi = o_mv + MLSTM_V_WIDTH
    o_mf = o_mi + MLSTM_HEADS
    o_mo = o_mf + MLSTM_HEADS
    n_gate = FOX_HEADS + 2 * MLSTM_HEADS
    w_r = jnp.concatenate(
        [w_in[:, FOX_WIDTH:2 * FOX_WIDTH], w_in[:, o_mqk:o_mv],
         w_in[:, o_ff:o_mqk], w_in[:, o_mi:o_mo],
         jnp.zeros((d, LANES - n_gate), w_in.dtype)], axis=1).astype(BF16)
    w_t = jnp.concatenate([w_in[:, :FOX_WIDTH], w_in[:, 2 * FOX_WIDTH:o_ff],
                           w_in[:, o_mv:o_mi], w_in[:, o_mo:]], axis=1).T.astype(BF16)
    gate_bias = jnp.concatenate(
        [fox_f_bias, i_bias, f_bias, jnp.zeros((LANES - n_gate,), F32)]).reshape(1, LANES)

    fqt, fk, fvt, mqk, mvt, mot, gates = _inproj(x2d, w_r, w_t, gate_bias)
    gcol, grow, cpieces = _gateprep(gates, batch, seq)
    fox_gain_lanes = jnp.broadcast_to(fox_g[:, None], (FOX_WIDTH, LANES))
    fo = _fox(fqt, fk, cpieces, fvt, fox_gain_lanes, batch, seq)
    gain_lanes = jnp.broadcast_to(mlstm_g[:, None], (MLSTM_V_WIDTH, LANES))
    mo_out = _mlstm(mqk, mvt, mot, gcol, grow, conv_w, gain_lanes, batch, seq)
    x1 = _outproj(fo, mo_out, w_mix_out.astype(BF16), x2d, ln1_g.reshape(1, d), ln1_b.reshape(1, d))

    kmem, vmem = _memkv(mem.reshape(batch * n_mem, d), w_xk.astype(BF16), w_xv.astype(BF16), n_mem)
    wrt = w_router.T
    wrt_hi = wrt.astype(BF16)
    wrt_lo = (wrt - wrt_hi.astype(F32)).astype(BF16)
    x2, x2p, logits_t = _xattn(x1, kmem, vmem, w_xq.astype(BF16), w_xo.astype(BF16),
                               ln2_g.reshape(1, d), ln2_b.reshape(1, d),
                               jnp.concatenate([wrt_hi, wrt_lo], axis=0),
                               b_router.reshape(N_EXPERTS, 1), batch, seq, n_mem)

    idx_t, rank_t, gate, cnt = _route(logits_t)
    counts = cnt[:, 0]
    g_rows = EXPERT_ROWS
    padded = ((counts + g_rows - 1) // g_rows) * g_rows
    pad_end = jnp.cumsum(padded)
    pad_start = pad_end - padded
    experts = jnp.arange(N_EXPERTS, dtype=jnp.int32)
    sel = idx_t[:TOP_K, :, None] == experts[None, None, :]
    pos_t = jnp.sum(jnp.where(sel, pad_start[None, None, :], 0), axis=-1) + rank_t[:TOP_K]
    p_rows = n * TOP_K + N_EXPERTS * g_rows
    nb = p_rows // g_rows
    blk_start = jnp.arange(nb, dtype=jnp.int32) * g_rows
    blk_e = jnp.minimum(jnp.sum((pad_end[None, :] <= blk_start[:, None]).astype(jnp.int32), axis=1),
                        N_EXPERTS - 1)
    own = blk_e[:, None] == experts[None, :]
    row_end = jnp.sum(jnp.where(own, (pad_start + counts)[None, :], 0), axis=1)
    blk_rows = jnp.clip(row_end - blk_start, 0, g_rows).astype(jnp.int32)
    later = (experts[None, :] > blk_e[:, None]) & (counts[None, :] > 0)
    next_e = jnp.min(jnp.where(later, experts[None, :], N_EXPERTS), axis=1)
    next_e = jnp.where(next_e == N_EXPERTS, -1, next_e).astype(jnp.int32)

    def piece_index(planes):
        off = jnp.arange(planes, dtype=jnp.int32) * p_rows
        return pos_t[:, None, :] + off[None, :, None]

    x_idx = piece_index(X_PLANES).reshape(TOP_K, X_PLANES * n)
    xs = _sc_scatter(x2p.reshape(X_PLANES * n, LANES), [x_idx[k] for k in range(TOP_K)],
                     X_PLANES * p_rows)
    y = _experts(blk_e, blk_rows, next_e, xs.reshape(X_PLANES, p_rows, LANES), w_gate_up,
                 b_gate_up.reshape(N_EXPERTS, 1, -1), w_down, b_down.reshape(N_EXPERTS, 1, -1))
    yg = _sc_gather(y.reshape(Y_PLANES * p_rows, LANES), piece_index(Y_PLANES).reshape(-1))
    out = _combine(yg.reshape(TOP_K, Y_PLANES, n, LANES), gate, x2,
                   ln3_g.reshape(1, d), ln3_b.reshape(1, d))
    return out.reshape(batch, seq, d)


def kernel(x, mem, w_in, fox_f_bias, mlstm_conv_w, mlstm_i_bias, mlstm_f_bias, fox_norm_g, mlstm_norm_g, w_mix_out, ln1_g, ln1_b, w_xq, w_xk, w_xv, w_xo, ln2_g, ln2_b, w_router, b_router, w_gate_up, b_gate_up, w_down, b_down, ln3_g, ln3_b):
    for l in range(w_in.shape[0]):
        x = _layer(x, mem, w_in[l], fox_f_bias[l], mlstm_conv_w[l], mlstm_i_bias[l],
                   mlstm_f_bias[l], fox_norm_g[l], mlstm_norm_g[l], w_mix_out[l],
                   ln1_g[l], ln1_b[l], w_xq[l], w_xk[l], w_xv[l], w_xo[l], ln2_g[l], ln2_b[l],
                   w_router[l], b_router[l], w_gate_up[l], b_gate_up[l], w_down[l], b_down[l],
                   ln3_g[l], ln3_b[l])
    return x
```

```python
import jax
import jax.numpy as jnp
from jax import lax
from jax.experimental import pallas as pl
from jax.experimental.pallas import tpu as pltpu
from jax.experimental.pallas import tpu_sc as plsc

F32 = jnp.float32
BF16 = jnp.bfloat16

D_MODEL = 1024
FOX_HEADS = 8
FOX_HEAD_DIM = 64
FOX_WIDTH = FOX_HEADS * FOX_HEAD_DIM
MLSTM_HEADS = 4
MLSTM_QK_DIM = 64
MLSTM_V_DIM = 128
MLSTM_QK_WIDTH = MLSTM_HEADS * MLSTM_QK_DIM
MLSTM_V_WIDTH = MLSTM_HEADS * MLSTM_V_DIM
CONV_WIDTH = 4
XATTN_HEADS = 4
XATTN_HEAD_DIM = D_MODEL // XATTN_HEADS
N_EXPERTS = 32
TOP_K = 4
D_EXPERT = D_MODEL
SWIGLU_LIMIT = 7.0
SWIGLU_ALPHA = 1.702
DEEPNORM_ALPHA = 2.0 ** 0.25
LN_EPS = 1e-5
RMS_EPS = 1e-6

LANES = 128
SEQ_BLOCK = 256
ROW_TILE = 1024
EXPERT_ROWS = 512
EXPERT_CHUNK = 256
ROUTE_TILE = 1024
X_PLANES = D_MODEL // 2 // LANES
Y_PLANES = X_PLANES
GATE_I0 = FOX_HEADS
GATE_F0 = FOX_HEADS + MLSTM_HEADS
GATE_ROWS = FOX_HEADS + 2 * MLSTM_HEADS

MIB = 1024 * 1024


def _params(semantics, vmem_mib):
    return pltpu.CompilerParams(dimension_semantics=semantics,
                                vmem_limit_bytes=vmem_mib * MIB)


def _layer_norm(y, g, b):
    mu = jnp.mean(y, axis=-1, keepdims=True)
    yc = y - mu
    var = jnp.mean(yc * yc, axis=-1, keepdims=True)
    return yc * lax.rsqrt(var + LN_EPS) * g + b


def _dot(a, b):
    return jnp.dot(a, b, preferred_element_type=F32)


def _dot_nt(a, b):
    return lax.dot_general(a, b, (((1,), (1,)), ((), ())), preferred_element_type=F32)


def _pack_bf16_pairs(x):
    w = x.shape[1] // 2
    lo = pltpu.bitcast(x[:, :w].astype(BF16).astype(F32), jnp.uint32)
    hi = pltpu.bitcast(x[:, w:].astype(BF16).astype(F32), jnp.uint32)
    return (lo >> 16) | hi


def _unpack_bf16_pairs(u, dtype=BF16):
    lo = pltpu.bitcast(u << 16, F32).astype(dtype)
    hi = pltpu.bitcast(u & jnp.uint32(0xFFFF0000), F32).astype(dtype)
    return jnp.concatenate([lo, hi], axis=1)


def _inproj_kernel(x_ref, w_ref, wt_ref, gb_ref, fqt_ref, fk_ref, fvt_ref, mqk_ref, mvt_ref,
                   mot_ref, g_ref):
    xb = x_ref[...].astype(BF16)

    def mm(c0, width):
        return _dot(xb, w_ref[:, c0:c0 + width])

    def mm_t(r0, height):
        return _dot_nt(wt_ref[r0:r0 + height, :], xb)

    r_fv = FOX_WIDTH
    r_mv = r_fv + FOX_WIDTH
    r_mo = r_mv + MLSTM_V_WIDTH
    fqt_ref[...] = (mm_t(0, FOX_WIDTH) * (FOX_HEAD_DIM ** -0.5)).astype(BF16)
    fvt_ref[...] = mm_t(r_fv, FOX_WIDTH).astype(BF16)
    mvt_ref[...] = mm_t(r_mv, MLSTM_V_WIDTH).astype(BF16)
    mot_ref[...] = mm_t(r_mo, MLSTM_V_WIDTH)
    c_qk = FOX_WIDTH
    c_g = c_qk + 2 * MLSTM_QK_WIDTH
    fk_ref[...] = mm(0, FOX_WIDTH).astype(BF16)
    mqk_ref[...] = mm(c_qk, 2 * MLSTM_QK_WIDTH)
    g_ref[...] = mm(c_g, LANES) + gb_ref[...]


def _inproj(x2d, w_r, w_t, gate_bias):
    n = x2d.shape[0]
    tm = ROW_TILE
    row = lambda width: pl.BlockSpec((tm, width), lambda i: (i, 0))
    col = lambda height: pl.BlockSpec((height, tm), lambda i: (0, i))
    full = lambda a: pl.BlockSpec(a.shape, lambda i: (0, 0))
    out_shapes = (
        jax.ShapeDtypeStruct((FOX_WIDTH, n), BF16),
        jax.ShapeDtypeStruct((n, FOX_WIDTH), BF16),
        jax.ShapeDtypeStruct((FOX_WIDTH, n), BF16),
        jax.ShapeDtypeStruct((n, 2 * MLSTM_QK_WIDTH), F32),
        jax.ShapeDtypeStruct((MLSTM_V_WIDTH, n), BF16),
        jax.ShapeDtypeStruct((MLSTM_V_WIDTH, n), F32),
        jax.ShapeDtypeStruct((n, LANES), F32),
    )
    return pl.pallas_call(
        _inproj_kernel,
        grid=(n // tm,),
        in_specs=[row(D_MODEL), full(w_r), full(w_t), full(gate_bias)],
        out_specs=(col(FOX_WIDTH), row(FOX_WIDTH), col(FOX_WIDTH), row(2 * MLSTM_QK_WIDTH),
                   col(MLSTM_V_WIDTH), col(MLSTM_V_WIDTH), row(LANES)),
        out_shape=out_shapes,
        compiler_params=_params(("parallel",), 48),
    )(x2d, w_r, w_t, gate_bias)


def _split3(x):
    hi = x.astype(BF16)
    r1 = x - hi.astype(F32)
    mid = r1.astype(BF16)
    lo = (r1 - mid.astype(F32)).astype(BF16)
    return hi, mid, lo


def _gateprep_kernel(g_ref, sel_ref, col_ref, row_ref, cp_ref, carry_ref):
    c = pl.program_id(1)

    @pl.when(c == 0)
    def _():
        carry_ref[...] = jnp.zeros_like(carry_ref)

    blk = SEQ_BLOCK
    lane = lax.broadcasted_iota(jnp.int32, (blk, LANES), 1)
    is_i = (lane >= GATE_I0) & (lane < GATE_F0)
    r = lax.broadcasted_iota(jnp.int32, (blk, blk), 0)
    s = lax.broadcasted_iota(jnp.int32, (blk, blk), 1)
    tri = (s <= r).astype(BF16)
    carry = carry_ref[0:1, :]
    globs = []
    for j in range(g_ref.shape[0] // blk):
        g = g_ref[j * blk:(j + 1) * blk, :]
        logsig = jnp.minimum(g, 0.0) - jnp.log1p(jnp.exp(-jnp.abs(g)))
        cs = None
        for piece in _split3(logsig):
            term = _dot(tri, piece)
            cs = term if cs is None else cs + term
        glob = cs + carry
        carry = glob[blk - 1:blk, :]
        out = jnp.where(lane < GATE_I0, glob, jnp.where(is_i, g, cs))
        col_ref[j * blk:(j + 1) * blk, :] = out
        row_ref[0, :, j * blk:(j + 1) * blk] = out.T[0:GATE_ROWS, :]
        globs.append(glob)
    carry_ref[...] = jnp.broadcast_to(carry, carry_ref.shape)
    glob = jnp.concatenate(globs, axis=0)

    pieces = jnp.concatenate(_split3(-glob), axis=1)
    moved = _dot(pieces, sel_ref[...])
    for p in range(FOX_HEADS // 2):
        cp_ref[0, p] = moved[:, p * LANES:(p + 1) * LANES].astype(BF16)


def _piece_selector():
    src = jnp.arange(3 * LANES, dtype=jnp.int32)[:, None]
    dst = jnp.arange(4 * LANES, dtype=jnp.int32)[None, :]
    piece, head = src // LANES, src % LANES
    pair, lane = dst // LANES, dst % LANES
    hit = (lane < 6) & (lane % 3 == piece) & (head == 2 * pair + lane // 3)
    return hit.astype(BF16)


def _gateprep(gates, batch, seq):
    n = gates.shape[0]
    rows = ROW_TILE
    nc = seq // rows
    sel = _piece_selector()
    return pl.pallas_call(
        _gateprep_kernel,
        grid=(batch, nc),
        in_specs=[pl.BlockSpec((rows, LANES), lambda b, c: (b * nc + c, 0)),
                  pl.BlockSpec(sel.shape, lambda b, c: (0, 0))],
        out_specs=(pl.BlockSpec((rows, LANES), lambda b, c: (b * nc + c, 0)),
                   pl.BlockSpec((1, GATE_ROWS, rows), lambda b, c: (b, 0, c)),
                   pl.BlockSpec((1, FOX_HEADS // 2, rows, LANES), lambda b, c: (b, 0, c, 0))),
        out_shape=(jax.ShapeDtypeStruct((n, LANES), F32),
                   jax.ShapeDtypeStruct((batch, GATE_ROWS, seq), F32),
                   jax.ShapeDtypeStruct((batch, FOX_HEADS // 2, seq, LANES), BF16)),
        scratch_shapes=[pltpu.VMEM((8, LANES), F32)],
        compiler_params=_params(("parallel", "arbitrary"), 32),
    )(gates, sel)


FOX_ONES_ROWS = 16
FOX_VAUG_ROWS = FOX_HEAD_DIM + FOX_ONES_ROWS
FOX_QUERY_TILE = 2 * SEQ_BLOCK


def _fox_kernel(qt_ref, k_ref, cp_ref, vt_ref, gain_ref, o_ref,
                kaug_sc, vaug_sc, qaug_sc, sta_sc, stb_sc, stc_sc, m_sc, acc0_sc, acc1_sc):
    qi = pl.program_id(2)
    tq = o_ref.shape[0]
    tg = tq
    tk = SEQ_BLOCK
    hd = FOX_HEAD_DIM
    seq = k_ref.shape[0]
    nq = seq // tq
    par = qi % 2

    def load_queries(tile, slot):
        q0 = pl.multiple_of(tile * tq, tq)
        for h in range(2):
            qaug_sc[slot, h, 0:hd, :] = qt_ref[h * hd:(h + 1) * hd, pl.ds(q0, tq)]

    @pl.when(qi == 0)
    def _():
        ones3 = (lax.broadcasted_iota(jnp.int32, (FOX_ONES_ROWS, tq), 0) < 3).astype(BF16)
        for slot in range(2):
            for h in range(2):
                qaug_sc[slot, h, hd:hd + FOX_ONES_ROWS, :] = ones3
                qaug_sc[slot, h, hd + FOX_ONES_ROWS:, :] = jnp.zeros(
                    (LANES - hd - FOX_ONES_ROWS, tq), BF16)
        load_queries(0, 0)

    nxt = jnp.minimum(qi + 1, nq - 1)
    load_queries(nxt, 1 - par)

    @pl.when(qi == 0)
    def _():
        lane = lax.broadcasted_iota(jnp.int32, (tk, LANES), 1)

        def build(blk, carry):
            r0 = pl.multiple_of(blk * tk, tk)
            kp = k_ref[pl.ds(r0, tk), :].astype(F32)
            cp = cp_ref[0, 0, pl.ds(r0, tk), :].astype(F32)
            for h in range(2):
                kh = kp if h == 0 else pltpu.roll(kp, hd, axis=1)
                ch = pltpu.roll(cp, hd - 3 * h, axis=1)
                kaug = jnp.where(lane < hd, kh, jnp.where(lane < hd + 3, ch, 0.0))
                kaug_sc[h, pl.ds(r0, tk), :] = kaug.astype(BF16)
            return carry

        lax.fori_loop(0, seq // tk, build, 0)
        for h in range(2):
            vaug_sc[h, 0:hd, :] = vt_ref[h * hd:(h + 1) * hd, :]
            vaug_sc[h, hd:, :] = jnp.ones((FOX_ONES_ROWS, seq), BF16)

    half = tg // 2

    def put_scores(kg, slot, queries=None):
        k0 = pl.multiple_of(kg * tg, tg)
        queries = par if queries is None else queries
        for h in range(2):
            slot[h] = _dot(kaug_sc[h, pl.ds(k0, tg), :], qaug_sc[queries, h])

    def put_diagonal(slot):
        k0 = pl.multiple_of(qi * tg, tg)
        causal_top = (lax.broadcasted_iota(jnp.int32, (half, tq), 0)
                      <= lax.broadcasted_iota(jnp.int32, (half, tq), 1))
        causal_bot = (lax.broadcasted_iota(jnp.int32, (half, half), 0)
                      <= lax.broadcasted_iota(jnp.int32, (half, half), 1))
        for h in range(2):
            top = _dot(kaug_sc[h, pl.ds(k0, half), :], qaug_sc[par, h])
            slot[h, 0:half, :] = jnp.where(causal_top, top, -jnp.inf)
            bot = _dot(kaug_sc[h, pl.ds(k0 + half, half), :], qaug_sc[par, h, :, half:])
            slot[h, half:, half:] = jnp.where(causal_bot, bot, -jnp.inf)

    m_sc[...] = jnp.full(m_sc.shape, -jnp.inf, F32)
    acc0_sc[...] = jnp.zeros(acc0_sc.shape, F32)
    acc1_sc[...] = jnp.zeros(acc1_sc.shape, F32)
    acc = (acc0_sc, acc1_sc)

    def absorb(kg, slot):
        k0 = pl.multiple_of(kg * tg, tg)
        for h in range(2):
            st = slot[h]
            m_prev = m_sc[h]
            m_new = jnp.maximum(m_prev, jnp.max(st, axis=0, keepdims=True))
            pv = None
            for u in range(2):
                p = jnp.exp(st[u * half:(u + 1) * half] - m_new)
                part = _dot(vaug_sc[h, :, pl.ds(k0 + u * half, half)], p.astype(BF16))
                pv = part if pv is None else pv + part
            acc[h][...] = jnp.exp(m_prev - m_new) * acc[h][...] + pv
            m_sc[h] = m_new

    def absorb_diagonal(slot):
        k0 = pl.multiple_of(qi * tg, tg)
        for h in range(2):
            top = slot[h, 0:half, :]
            bot = slot[h, half:, half:]
            m_prev = m_sc[h]
            m_sc[h] = jnp.maximum(m_prev, jnp.max(top, axis=0, keepdims=True))
            m_sc[h, :, half:] = jnp.maximum(m_sc[h, :, half:], jnp.max(bot, axis=0, keepdims=True))
            m_new = m_sc[h]
            acc[h][...] = (jnp.exp(m_prev - m_new) * acc[h][...]
                           + _dot(vaug_sc[h, :, pl.ds(k0, half)],
                                  jnp.exp(top - m_new).astype(BF16)))
            acc[h][:, half:] += _dot(vaug_sc[h, :, pl.ds(k0 + half, half)],
                                     jnp.exp(bot - m_sc[h, :, half:]).astype(BF16))

    sa, sb, sc = sta_sc, stb_sc, stc_sc
    n_loop = jnp.maximum(qi - 1, 0) // 2

    def body(j, carry):
        put_scores(2 * j + 1, sb)
        absorb(2 * j, sa)
        put_scores(2 * j + 2, sa)
        absorb(2 * j + 1, sb)
        return carry

    lax.fori_loop(0, n_loop, body, 0)
    done = 2 * n_loop

    def put_next():
        put_scores(0, sa, queries=1 - par)

    @pl.when(qi == 0)
    def _():
        put_diagonal(sc)
        put_next()
        absorb_diagonal(sc)

    @pl.when((qi > 0) & (qi - done == 1))
    def _():
        put_diagonal(sc)
        absorb(done, sa)
        put_next()
        absorb_diagonal(sc)

    @pl.when((qi > 0) & (qi - done == 2))
    def _():
        put_scores(done + 1, sb)
        absorb(done, sa)
        put_diagonal(sc)
        absorb(done + 1, sb)
        put_next()
        absorb_diagonal(sc)

    normed = []
    for h in range(2):
        a = acc[h][...]
        oh = a[0:hd] * (1.0 / a[hd:hd + 1])
        ms = jnp.mean(oh * oh, axis=0, keepdims=True)
        gain = gain_ref[h * hd:(h + 1) * hd, :]
        normed.append(oh * lax.rsqrt(ms + RMS_EPS) * jnp.concatenate([gain] * (tq // LANES), axis=1))
    o_ref[...] = jnp.concatenate(normed, axis=0).T.astype(o_ref.dtype)


def _fox(fqt, fk, cpieces, fvt, fox_gain, batch, seq):
    n = fk.shape[0]
    tq = FOX_QUERY_TILE
    nq = seq // tq
    npair = FOX_HEADS // 2
    return pl.pallas_call(
        _fox_kernel,
        grid=(batch, npair, nq),
        in_specs=[pl.BlockSpec((LANES, seq), lambda b, hp, qi: (hp, b)),
                  pl.BlockSpec((seq, LANES), lambda b, hp, qi: (b, hp)),
                  pl.BlockSpec((1, 1, seq, LANES), lambda b, hp, qi: (b, hp, 0, 0)),
                  pl.BlockSpec((LANES, seq), lambda b, hp, qi: (hp, b)),
                  pl.BlockSpec((LANES, LANES), lambda b, hp, qi: (hp, 0))],
        out_specs=pl.BlockSpec((tq, LANES), lambda b, hp, qi: (b * nq + qi, hp)),
        out_shape=jax.ShapeDtypeStruct((n, FOX_WIDTH), BF16),
        scratch_shapes=[pltpu.VMEM((2, seq, LANES), BF16),
                        pltpu.VMEM((2, FOX_VAUG_ROWS, seq), BF16),
                        pltpu.VMEM((2, 2, LANES, tq), BF16),
                        pltpu.VMEM((2, tq, tq), F32),
                        pltpu.VMEM((2, tq, tq), F32),
                        pltpu.VMEM((2, tq, tq), F32),
                        pltpu.VMEM((2, 1, tq), F32),
                        pltpu.VMEM((FOX_VAUG_ROWS, tq), F32),
                        pltpu.VMEM((FOX_VAUG_ROWS, tq), F32)],
        compiler_params=_params(("parallel", "parallel", "arbitrary"), 48),
    )(fqt, fk, cpieces, fvt, fox_gain)


MLSTM_ONES_ROWS = 16


def _mlstm_kernel(qk_ref, vt_ref, ogt_ref, col_ref, row_ref, cw_ref, gain_ref, o_ref,
                  tail_ref, buf_ref, c_sc, m_sc):
    c = pl.program_id(1)
    L = SEQ_BLOCK

    @pl.when(c == 0)
    def _():
        tail_ref[...] = jnp.zeros_like(tail_ref)
        c_sc[...] = jnp.zeros_like(c_sc)
        m_sc[...] = jnp.zeros_like(m_sc)

    rows_step = qk_ref.shape[0]
    x = qk_ref[...]
    buf_ref[0:8, :] = tail_ref[...]
    buf_ref[8:8 + rows_step, :] = x
    tail_ref[...] = x[rows_step - 8:rows_step, :]
    y = x * cw_ref[CONV_WIDTH - 1:CONV_WIDTH, :]
    for j in range(CONV_WIDTH - 1):
        shift = CONV_WIDTH - 1 - j
        y = y + buf_ref[8 - shift:8 - shift + rows_step, :] * cw_ref[j:j + 1, :]
    y = y * jax.nn.sigmoid(y)
    n_pair = MLSTM_HEADS // 2

    lane = lax.broadcasted_iota(jnp.int32, (L, LANES), 1)
    sub = lax.broadcasted_iota(jnp.int32, (LANES, L), 0)
    ss = lax.broadcasted_iota(jnp.int32, (L, L), 0)
    ll = lax.broadcasted_iota(jnp.int32, (L, L), 1)
    causal = ss <= ll
    ones_rows = jnp.ones((MLSTM_ONES_ROWS, L), BF16)
    cstates = [c_sc[h] for h in range(MLSTM_HEADS)]
    mstates = [m_sc[h][0:1, 0:1] for h in range(MLSTM_HEADS)]

    per_chunk = []
    for chunk in range(rows_step // L):
        r0 = chunk * L
        yc = y[r0:r0 + L, :]
        kf = yc[:, MLSTM_QK_WIDTH:] * (MLSTM_QK_DIM ** -0.5)
        per_chunk.append((
            [yc[:, p * LANES:(p + 1) * LANES].T for p in range(n_pair)],
            [kf[:, p * LANES:(p + 1) * LANES].astype(BF16) for p in range(n_pair)],
            col_ref[r0:r0 + L, :],
            row_ref[0, :, r0:r0 + L]))

    for chunk, h in [(c_, h_) for c_ in range(rows_step // L) for h_ in range(MLSTM_HEADS)]:
        r0 = chunk * L
        qt, kb, col, row = per_chunk[chunk]
        pair, half = divmod(h, 2)
        head_sub = (sub < MLSTM_QK_DIM) if half == 0 else (sub >= MLSTM_QK_DIM)
        head_lane = (lane < MLSTM_QK_DIM) if half == 0 else (lane >= MLSTM_QK_DIM)
        qth = jnp.where(head_sub, qt[pair], 0.0).astype(BF16)
        rcol = col[:, GATE_I0 + h:GATE_I0 + h + 1] - col[:, GATE_F0 + h:GATE_F0 + h + 1]
        brow = row[GATE_F0 + h:GATE_F0 + h + 1, :]
        lirow = row[GATE_I0 + h:GATE_I0 + h + 1, :]
        g = brow[:, L - 1:L]
        m_prev = mstates[h]

        dt = jnp.where(causal, rcol + brow, -jnp.inf)
        inter_log = brow + m_prev
        m_t = jnp.maximum(inter_log, jnp.max(dt, axis=0, keepdims=True))
        w_inter = jnp.exp(inter_log - m_t)
        pt = jnp.exp(dt - m_t) * _dot(kb[pair], qth)
        vaug = jnp.concatenate([vt_ref[h * LANES:(h + 1) * LANES, r0:r0 + L], ones_rows],
                               axis=0)
        cstate = cstates[h]
        tot = w_inter * _dot(cstate.astype(BF16), qth) + _dot(vaug, pt.astype(BF16))
        den = tot[MLSTM_V_DIM:MLSTM_V_DIM + 1]
        hout = tot[0:MLSTM_V_DIM] * (1.0 / jnp.maximum(jnp.abs(den), jnp.exp(-m_t)))

        a = g + (lirow - brow)
        m_loc = jnp.max(a, axis=1, keepdims=True)
        vw = (vaug.astype(F32) * jnp.exp(a - m_loc)).astype(BF16)
        kmask = jnp.where(head_lane, kb[pair], jnp.zeros_like(kb[pair]))
        kv = _dot(vw, kmask)
        m_new = jnp.maximum(g + m_prev, m_loc)
        cstates[h] = jnp.exp(g + m_prev - m_new) * cstate + jnp.exp(m_loc - m_new) * kv
        mstates[h] = m_new

        ms = jnp.mean(hout * hout, axis=0, keepdims=True)
        gain = gain_ref[h * LANES:(h + 1) * LANES, :]
        hn = hout * lax.rsqrt(ms + RMS_EPS) * jnp.concatenate([gain] * (L // LANES), axis=1)
        gate = jax.nn.sigmoid(ogt_ref[h * LANES:(h + 1) * LANES, r0:r0 + L])
        o_ref[r0:r0 + L, h * LANES:(h + 1) * LANES] = (hn * gate).T.astype(o_ref.dtype)

    for h in range(MLSTM_HEADS):
        c_sc[h] = cstates[h]
        m_sc[h] = jnp.broadcast_to(mstates[h], m_sc.shape[1:])


def _mlstm(mqk, mvt, mot, gcol, grow, conv_w, gain_lanes, batch, seq):
    n = mqk.shape[0]
    L = ROW_TILE
    nc = seq // L
    qk_width = 2 * MLSTM_QK_WIDTH
    row = lambda width: pl.BlockSpec((L, width), lambda b, c: (b * nc + c, 0))
    col = pl.BlockSpec((MLSTM_V_WIDTH, L), lambda b, c: (0, b * nc + c))
    full = lambda a: pl.BlockSpec(a.shape, lambda b, c: (0, 0))
    return pl.pallas_call(
        _mlstm_kernel,
        grid=(batch, nc),
        in_specs=[row(qk_width), col, col, row(LANES),
                  pl.BlockSpec((1, GATE_ROWS, L), lambda b, c: (b, 0, c)),
                  full(conv_w), full(gain_lanes)],
        out_specs=row(MLSTM_V_WIDTH),
        out_shape=jax.ShapeDtypeStruct((n, MLSTM_V_WIDTH), BF16),
        scratch_shapes=[pltpu.VMEM((8, qk_width), F32), pltpu.VMEM((8 + L, qk_width), F32),
                        pltpu.VMEM((MLSTM_HEADS, MLSTM_V_DIM + MLSTM_ONES_ROWS, LANES), F32),
                        pltpu.VMEM((MLSTM_HEADS, 8, LANES), F32)],
        compiler_params=_params(("parallel", "arbitrary"), 48),
    )(mqk, mvt, mot, gcol, grow, conv_w, gain_lanes)


def _outproj_kernel(fo_ref, mo_ref, w_ref, x_ref, g_ref, b_ref, o_ref):
    mix = _dot(fo_ref[...], w_ref[0:FOX_WIDTH, :]) + _dot(mo_ref[...], w_ref[FOX_WIDTH:, :])
    o_ref[...] = _layer_norm(DEEPNORM_ALPHA * x_ref[...] + mix, g_ref[...], b_ref[...])


def _outproj(fo, mo, w_out, x2d, g, b):
    n = x2d.shape[0]
    tm = ROW_TILE
    row = lambda width: pl.BlockSpec((tm, width), lambda i: (i, 0))
    full = lambda a: pl.BlockSpec(a.shape, lambda i: (0, 0))
    return pl.pallas_call(
        _outproj_kernel,
        grid=(n // tm,),
        in_specs=[row(512), row(512), full(w_out), row(D_MODEL), full(g), full(b)],
        out_specs=row(D_MODEL),
        out_shape=jax.ShapeDtypeStruct((n, D_MODEL), F32),
        compiler_params=_params(("parallel",), 48),
    )(fo, mo, w_out, x2d, g, b)


def _memkv_kernel(mem_ref, wk_ref, wv_ref, k_ref, v_ref):
    mb = mem_ref[...].astype(BF16)
    k_ref[...] = (_dot(mb, wk_ref[...]) * (XATTN_HEAD_DIM ** -0.5)).astype(BF16)
    v_ref[...] = _dot(mb, wv_ref[...]).astype(BF16)


def _memkv(mem2d, wk, wv, n_mem):
    n = mem2d.shape[0]
    row = pl.BlockSpec((n_mem, D_MODEL), lambda i: (i, 0))
    full = lambda a: pl.BlockSpec(a.shape, lambda i: (0, 0))
    return pl.pallas_call(
        _memkv_kernel,
        grid=(n // n_mem,),
        in_specs=[row, full(wk), full(wv)],
        out_specs=(row, row),
        out_shape=(jax.ShapeDtypeStruct((n, D_MODEL), BF16),) * 2,
        compiler_params=_params(("parallel",), 32),
    )(mem2d, wk, wv)


def _xattn_kernel(x_ref, k_ref, v_ref, wq_ref, wo_ref, g_ref, b_ref, wr_ref, br_ref,
                  o_ref, ob_ref, lg_ref):
    half = x_ref.shape[0] // 2
    for r in range(2):
        rows = slice(r * half, (r + 1) * half)
        x = x_ref[rows, :]
        q = _dot(x.astype(BF16), wq_ref[...]).astype(BF16)
        outs = []
        for h in range(XATTN_HEADS):
            sl = slice(h * XATTN_HEAD_DIM, (h + 1) * XATTN_HEAD_DIM)
            s = _dot_nt(q[:, sl], k_ref[:, sl])
            p = jnp.exp(s - jnp.max(s, axis=-1, keepdims=True))
            l = jnp.sum(p, axis=-1, keepdims=True)
            outs.append((_dot(p.astype(BF16), v_ref[:, sl]) / l).astype(BF16))
        o = jnp.concatenate(outs, axis=1)
        xa = _dot(o, wo_ref[...])
        x2 = _layer_norm(DEEPNORM_ALPHA * x + xa, g_ref[...], b_ref[...])
        o_ref[rows, :] = x2
        packed = _pack_bf16_pairs(x2)
        for j in range(ob_ref.shape[0]):
            ob_ref[j, rows, :] = packed[:, j * LANES:(j + 1) * LANES]
        x2h = x2.astype(BF16)
        x2l = (x2 - x2h.astype(F32)).astype(BF16)
        a = _dot_nt(wr_ref[...], x2h)
        b = _dot_nt(wr_ref[0:N_EXPERTS, :], x2l)
        lg_ref[:, rows] = a[0:N_EXPERTS] + a[N_EXPERTS:] + b + br_ref[...]


def _xattn(x1, kmem, vmem, wq, wo, g, b, wr, br, batch, seq, n_mem):
    n = x1.shape[0]
    tm = ROW_TILE
    nt = seq // tm
    row = lambda width: pl.BlockSpec((tm, width), lambda bi, i: (bi * nt + i, 0))
    full = lambda a: pl.BlockSpec(a.shape, lambda bi, i: (0, 0))
    kv = pl.BlockSpec((n_mem, D_MODEL), lambda bi, i: (bi, 0))
    return pl.pallas_call(
        _xattn_kernel,
        grid=(batch, nt),
        in_specs=[row(D_MODEL), kv, kv, full(wq), full(wo), full(g), full(b), full(wr), full(br)],
        out_specs=(row(D_MODEL),
                   pl.BlockSpec((X_PLANES, tm, LANES), lambda bi, i: (0, bi * nt + i, 0)),
                   pl.BlockSpec((N_EXPERTS, tm), lambda bi, i: (0, bi * nt + i))),
        out_shape=(jax.ShapeDtypeStruct((n, D_MODEL), F32),
                   jax.ShapeDtypeStruct((X_PLANES, n, LANES), jnp.uint32),
                   jax.ShapeDtypeStruct((N_EXPERTS, n), F32)),
        compiler_params=_params(("parallel", "parallel"), 60),
    )(x1, kmem, vmem, wq, wo, g, b, wr, br)


def _route_kernel(lg_ref, idx_ref, rank_ref, gate_ref, cnt_ref, carry_ref):
    i = pl.program_id(0)

    @pl.when(i == 0)
    def _():
        carry_ref[...] = jnp.zeros_like(carry_ref)

    lg = lg_ref[...]
    t = lg.shape[1]
    e_idx = lax.broadcasted_iota(jnp.int32, lg.shape, 0).astype(F32)
    sels, vals, idxs = [], [], []
    for _ in range(TOP_K):
        mx = jnp.max(lg, axis=0, keepdims=True)
        first = jnp.min(jnp.where(lg == mx, e_idx, float(N_EXPERTS)), axis=0, keepdims=True)
        sel = e_idx == first
        sels.append(sel)
        vals.append(mx)
        idxs.append(first)
        lg = jnp.where(sel, -jnp.inf, lg)
    exps = [jnp.exp(v - vals[0]) for v in vals]
    tot = exps[0] + exps[1] + exps[2] + exps[3]

    selmat = (sels[0] | sels[1] | sels[2] | sels[3])
    r = lax.broadcasted_iota(jnp.int32, (t, t), 0)
    s = lax.broadcasted_iota(jnp.int32, (t, t), 1)
    earlier = (r < s).astype(BF16)
    carry = carry_ref[:, 0:1]
    rankmat = _dot(selmat.astype(BF16), earlier) + carry
    new_carry = carry + jnp.sum(selmat.astype(F32), axis=1, keepdims=True)
    carry_ref[...] = jnp.broadcast_to(new_carry, carry_ref.shape)
    cnt_ref[...] = jnp.broadcast_to(new_carry, cnt_ref.shape).astype(jnp.int32)

    row8 = lax.broadcasted_iota(jnp.int32, (8, t), 0)
    row128 = lax.broadcasted_iota(jnp.int32, (LANES, t), 0)
    idx_out = jnp.zeros((8, t), F32)
    rank_out = jnp.zeros((8, t), F32)
    gate_out = jnp.zeros((LANES, t), F32)
    for k in range(TOP_K):
        rk = jnp.sum(jnp.where(sels[k], rankmat, 0.0), axis=0, keepdims=True)
        idx_out = jnp.where(row8 == k, idxs[k], idx_out)
        rank_out = jnp.where(row8 == k, rk, rank_out)
        gate_out = jnp.where(row128 == k, exps[k] / tot, gate_out)
    idx_ref[...] = idx_out.astype(jnp.int32)
    rank_ref[...] = rank_out.astype(jnp.int32)
    gate_ref[...] = gate_out.T


def _route(logits_t):
    n = logits_t.shape[1]
    t = ROUTE_TILE
    col = lambda rows: pl.BlockSpec((rows, t), lambda i: (0, i))
    return pl.pallas_call(
        _route_kernel,
        grid=(n // t,),
        in_specs=[col(N_EXPERTS)],
        out_specs=(col(8), col(8), pl.BlockSpec((t, LANES), lambda i: (i, 0)),
                   pl.BlockSpec((N_EXPERTS, LANES), lambda i: (0, 0))),
        out_shape=(jax.ShapeDtypeStruct((8, n), jnp.int32),
                   jax.ShapeDtypeStruct((8, n), jnp.int32),
                   jax.ShapeDtypeStruct((n, LANES), F32),
                   jax.ShapeDtypeStruct((N_EXPERTS, LANES), jnp.int32)),
        scratch_shapes=[pltpu.VMEM((N_EXPERTS, LANES), F32)],
        compiler_params=_params(("arbitrary",), 32),
    )(logits_t)


def _expert_kernel(blk_e_ref, blk_rows_ref, next_e_ref, x_ref, wgu_hbm, bgu_ref, wd_hbm, bd_ref,
                   y_ref, wgu_f32, wd_f32, wgu_sc, wd_sc, state_ref, sem):
    i = pl.program_id(0)
    e = blk_e_ref[i]
    rows = blk_rows_ref[i]
    g = y_ref.shape[1]

    def weight_copies(expert, slot):
        return (pltpu.make_async_copy(wgu_hbm.at[expert], wgu_f32.at[slot], sem.at[0, slot]),
                pltpu.make_async_copy(wd_hbm.at[expert], wd_f32.at[slot], sem.at[1, slot]))

    @pl.when(i == 0)
    def _():
        state_ref[0] = -1
        state_ref[1] = 0

    @pl.when((rows > 0) & (e != state_ref[0]))
    def _():
        slot = state_ref[1]

        @pl.when(i == 0)
        def _():
            for cp in weight_copies(e, slot):
                cp.start()

        for cp in weight_copies(e, slot):
            cp.wait()
        wgu_sc[...] = wgu_f32[slot].astype(BF16)
        wd_sc[...] = wd_f32[slot].astype(BF16)
        nxt = next_e_ref[i]

        @pl.when(nxt >= 0)
        def _():
            for cp in weight_copies(nxt, 1 - slot):
                cp.start()

        state_ref[0] = e
        state_ref[1] = 1 - slot

    def ffn(m):
        packed = jnp.concatenate([x_ref[j, 0:m, :] for j in range(X_PLANES)], axis=1)
        xb = _unpack_bf16_pairs(packed)
        hids = []
        for c in range(D_EXPERT // EXPERT_CHUNK):
            g0 = c * EXPERT_CHUNK
            l0 = D_EXPERT + g0
            gate = _dot(xb, wgu_sc[:, g0:g0 + EXPERT_CHUNK]) + bgu_ref[0, :, g0:g0 + EXPERT_CHUNK]
            lin = _dot(xb, wgu_sc[:, l0:l0 + EXPERT_CHUNK]) + bgu_ref[0, :, l0:l0 + EXPERT_CHUNK]
            gate = jnp.minimum(gate, SWIGLU_LIMIT)
            lin = jnp.clip(lin, -SWIGLU_LIMIT, SWIGLU_LIMIT)
            hids.append((gate * jax.nn.sigmoid(SWIGLU_ALPHA * gate) * (lin + 1.0)).astype(BF16))
        y = _pack_bf16_pairs(_dot(jnp.concatenate(hids, axis=1), wd_sc[...]) + bd_ref[0])
        for j in range(Y_PLANES):
            y_ref[j, 0:m, :] = y[:, j * LANES:(j + 1) * LANES]

    @pl.when(rows > g // 2)
    def _():
        ffn(g)

    @pl.when((rows > 0) & (rows <= g // 2))
    def _():
        ffn(g // 2)
        y_ref[:, g // 2:, :] = jnp.zeros((Y_PLANES, g // 2, LANES), y_ref.dtype)

    @pl.when(rows == 0)
    def _():
        y_ref[...] = jnp.zeros_like(y_ref)


def _experts(blk_e, blk_rows, next_e, xs, w_gu, b_gu, w_d, b_d):
    p = xs.shape[1]
    g = EXPERT_ROWS
    grid_spec = pltpu.PrefetchScalarGridSpec(
        num_scalar_prefetch=3,
        grid=(p // g,),
        in_specs=[pl.BlockSpec((X_PLANES, g, LANES), lambda i, be, br, ne: (0, i, 0)),
                  pl.BlockSpec(memory_space=pl.ANY),
                  pl.BlockSpec((1, 1, 2 * D_EXPERT), lambda i, be, br, ne: (be[i], 0, 0)),
                  pl.BlockSpec(memory_space=pl.ANY),
                  pl.BlockSpec((1, 1, D_MODEL), lambda i, be, br, ne: (be[i], 0, 0))],
        out_specs=pl.BlockSpec((Y_PLANES, g, LANES), lambda i, be, br, ne: (0, i, 0)),
        scratch_shapes=[pltpu.VMEM((2, D_MODEL, 2 * D_EXPERT), F32),
                        pltpu.VMEM((2, D_EXPERT, D_MODEL), F32),
                        pltpu.VMEM((D_MODEL, 2 * D_EXPERT), BF16),
                        pltpu.VMEM((D_EXPERT, D_MODEL), BF16),
                        pltpu.SMEM((2,), jnp.int32),
                        pltpu.SemaphoreType.DMA((2, 2))],
    )
    return pl.pallas_call(
        _expert_kernel,
        grid_spec=grid_spec,
        out_shape=jax.ShapeDtypeStruct((Y_PLANES, p, LANES), jnp.uint32),
        compiler_params=_params(("arbitrary",), 56),
    )(blk_e, blk_rows, next_e, xs, w_gu, b_gu, w_d, b_d)


def _combine_kernel(*refs):
    y_refs = refs[:TOP_K * Y_PLANES]
    gate_ref, x_ref, g_ref, b_ref, o_ref = refs[TOP_K * Y_PLANES:]
    gate = gate_ref[...]
    ff = None
    for k in range(TOP_K):
        packed = jnp.concatenate([y_refs[k * Y_PLANES + j][...] for j in range(Y_PLANES)], axis=1)
        yk = _unpack_bf16_pairs(packed, F32) * gate[:, k:k + 1]
        ff = yk if ff is None else ff + yk
    o_ref[...] = _layer_norm(DEEPNORM_ALPHA * x_ref[...] + ff, g_ref[...], b_ref[...])


def _combine(yg, gate, x2, g, b):
    n = x2.shape[0]
    tm = ROW_TILE
    row = lambda width: pl.BlockSpec((tm, width), lambda i: (i, 0))
    full = lambda a: pl.BlockSpec(a.shape, lambda i: (0, 0))
    planes = [pl.BlockSpec((tm, LANES), lambda i, kj=kj: (kj * (n // tm) + i, 0))
              for kj in range(TOP_K * Y_PLANES)]
    return pl.pallas_call(
        _combine_kernel,
        grid=(n // tm,),
        in_specs=planes + [row(LANES), row(D_MODEL), full(g), full(b)],
        out_specs=row(D_MODEL),
        out_shape=jax.ShapeDtypeStruct((n, D_MODEL), F32),
        compiler_params=_params(("parallel",), 56),
    )(*([yg] * (TOP_K * Y_PLANES)), gate, x2, g, b)


SC_WINDOW = 128
SC_GATHERS_IN_FLIGHT = 3


def _sc_mesh():
    return plsc.VectorSubcoreMesh(core_axis_name="core", subcore_axis_name="subcore")


def _sc_gather(table, idx):
    n_fly = SC_GATHERS_IN_FLIGHT
    step = n_fly * SC_WINDOW
    m = -(-idx.shape[0] // step) * step
    idx = jnp.concatenate([idx, jnp.zeros((m - idx.shape[0],), idx.dtype)])

    @pl.kernel(out_type=jax.ShapeDtypeStruct((m, LANES), table.dtype), mesh=_sc_mesh(),
               scratch_types=[pltpu.SemaphoreType.DMA])
    def gather_kernel(table_hbm, idx_hbm, out_hbm, sem):
        def body(*refs):
            idx_vmem, out_vmem = refs[:n_fly], refs[n_fly]
            copies = [pltpu.async_copy(table_hbm.at[iv.at[0]],
                                       out_vmem.at[pl.ds(u * SC_WINDOW, SC_WINDOW)], sem)
                      for u, iv in enumerate(idx_vmem)]
            for cp in copies:
                cp.wait()

        pltpu.emit_pipeline(
            body,
            grid=(m // (n_fly * SC_WINDOW),),
            in_specs=[pl.BlockSpec((1, SC_WINDOW), lambda i, u=u: (0, n_fly * i + u))
                      for u in range(n_fly)],
            out_specs=[pl.BlockSpec((n_fly * SC_WINDOW, LANES), lambda i: (i, 0))],
            core_axis_name=("core", "subcore"),
            dimension_semantics=(pltpu.PARALLEL,),
        )(*([idx_hbm] * n_fly), out_hbm)

    return gather_kernel(table, idx.reshape(1, m))


def _sc_scatter(src, idx_lists, out_rows):
    m = src.shape[0]
    n_lists = len(idx_lists)

    @pl.kernel(out_type=jax.ShapeDtypeStruct((out_rows, LANES), src.dtype), mesh=_sc_mesh(),
               scratch_types=[pltpu.SemaphoreType.DMA])
    def scatter_kernel(src_hbm, *refs):
        idx_hbm, out_hbm, sem = refs[:n_lists], refs[n_lists], refs[n_lists + 1]

        def body(src_vmem, *idx_vmem):
            copies = [pltpu.async_copy(src_vmem, out_hbm.at[iv.at[0]], sem) for iv in idx_vmem]
            for cp in copies:
                cp.wait()

        pltpu.emit_pipeline(
            body,
            grid=(m // SC_WINDOW,),
            in_specs=[pl.BlockSpec((SC_WINDOW, LANES), lambda i: (i, 0))]
            + [pl.BlockSpec((1, SC_WINDOW), lambda i: (0, i))] * n_lists,
            out_specs=[],
            core_axis_name=("core", "subcore"),
            dimension_semantics=(pltpu.PARALLEL,),
        )(src_hbm, *idx_hbm)

    return scatter_kernel(src, *[ix.reshape(1, m) for ix in idx_lists])


def _layer(x, mem, w_in, fox_f_bias, conv_w, i_bias, f_bias, fox_g, mlstm_g, w_mix_out,
           ln1_g, ln1_b, w_xq, w_xk, w_xv, w_xo, ln2_g, ln2_b, w_router, b_router,
           w_gate_up, b_gate_up, w_down, b_down, ln3_g, ln3_b):
    batch, seq, d = x.shape
    n_mem = mem.shape[1]
    n = batch * seq
    x2d = x.reshape(n, d)

    o_ff = 3 * FOX_WIDTH
    o_mqk = o_ff + FOX_HEADS
    o_mv = o_mqk + 2 * MLSTM_QK_WIDTH
    o_mi = o_mv + MLSTM_V_WIDTH
    o_mf = o_mi + MLSTM_HEADS
    o_mo = o_mf + MLSTM_HEADS
    n_gate = FOX_HEADS + 2 * MLSTM_HEADS
    w_r = jnp.concatenate(
        [w_in[:, FOX_WIDTH:2 * FOX_WIDTH], w_in[:, o_mqk:o_mv],
         w_in[:, o_ff:o_mqk], w_in[:, o_mi:o_mo],
         jnp.zeros((d, LANES - n_gate), w_in.dtype)], axis=1).astype(BF16)
    w_t = jnp.concatenate([w_in[:, :FOX_WIDTH], w_in[:, 2 * FOX_WIDTH:o_ff],
                           w_in[:, o_mv:o_mi], w_in[:, o_mo:]], axis=1).T.astype(BF16)
    gate_bias = jnp.concatenate(
        [fox_f_bias, i_bias, f_bias, jnp.zeros((LANES - n_gate,), F32)]).reshape(1, LANES)

    fqt, fk, fvt, mqk, mvt, mot, gates = _inproj(x2d, w_r, w_t, gate_bias)
    gcol, grow, cpieces = _gateprep(gates, batch, seq)
    fox_gain_lanes = jnp.broadcast_to(fox_g[:, None], (FOX_WIDTH, LANES))
    fo = _fox(fqt, fk, cpieces, fvt, fox_gain_lanes, batch, seq)
    gain_lanes = jnp.broadcast_to(mlstm_g[:, None], (MLSTM_V_WIDTH, LANES))
    mo_out = _mlstm(mqk, mvt, mot, gcol, grow, conv_w, gain_lanes, batch, seq)
    x1 = _outproj(fo, mo_out, w_mix_out.astype(BF16), x2d, ln1_g.reshape(1, d), ln1_b.reshape(1, d))

    kmem, vmem = _memkv(mem.reshape(batch * n_mem, d), w_xk.astype(BF16), w_xv.astype(BF16), n_mem)
    wrt = w_router.T
    wrt_hi = wrt.astype(BF16)
    wrt_lo = (wrt - wrt_hi.astype(F32)).astype(BF16)
    x2, x2p, logits_t = _xattn(x1, kmem, vmem, w_xq.astype(BF16), w_xo.astype(BF16),
                               ln2_g.reshape(1, d), ln2_b.reshape(1, d),
                               jnp.concatenate([wrt_hi, wrt_lo], axis=0),
                               b_router.reshape(N_EXPERTS, 1), batch, seq, n_mem)

    idx_t, rank_t, gate, cnt = _route(logits_t)
    counts = cnt[:, 0]
    g_rows = EXPERT_ROWS
    padded = ((counts + g_rows - 1) // g_rows) * g_rows
    pad_end = jnp.cumsum(padded)
    pad_start = pad_end - padded
    experts = jnp.arange(N_EXPERTS, dtype=jnp.int32)
    sel = idx_t[:TOP_K, :, None] == experts[None, None, :]
    pos_t = jnp.sum(jnp.where(sel, pad_start[None, None, :], 0), axis=-1) + rank_t[:TOP_K]
    p_rows = n * TOP_K + N_EXPERTS * g_rows
    nb = p_rows // g_rows
    blk_start = jnp.arange(nb, dtype=jnp.int32) * g_rows
    blk_e = jnp.minimum(jnp.sum((pad_end[None, :] <= blk_start[:, None]).astype(jnp.int32), axis=1),
                        N_EXPERTS - 1)
    own = blk_e[:, None] == experts[None, :]
    row_end = jnp.sum(jnp.where(own, (pad_start + counts)[None, :], 0), axis=1)
    blk_rows = jnp.clip(row_end - blk_start, 0, g_rows).astype(jnp.int32)
    later = (experts[None, :] > blk_e[:, None]) & (counts[None, :] > 0)
    next_e = jnp.min(jnp.where(later, experts[None, :], N_EXPERTS), axis=1)
    next_e = jnp.where(next_e == N_EXPERTS, -1, next_e).astype(jnp.int32)

    def piece_index(planes):
        off = jnp.arange(planes, dtype=jnp.int32) * p_rows
        return pos_t[:, None, :] + off[None, :, None]

    x_idx = piece_index(X_PLANES).reshape(TOP_K, X_PLANES * n)
    xs = _sc_scatter(x2p.reshape(X_PLANES * n, LANES), [x_idx[k] for k in range(TOP_K)],
                     X_PLANES * p_rows)
    y = _experts(blk_e, blk_rows, next_e, xs.reshape(X_PLANES, p_rows, LANES), w_gate_up,
                 b_gate_up.reshape(N_EXPERTS, 1, -1), w_down, b_down.reshape(N_EXPERTS, 1, -1))
    yg = _sc_gather(y.reshape(Y_PLANES * p_rows, LANES), piece_index(Y_PLANES).reshape(-1))
    out = _combine(yg, gate, x2, ln3_g.reshape(1, d), ln3_b.reshape(1, d))
    return out.reshape(batch, seq, d)


def kernel(x, mem, w_in, fox_f_bias, mlstm_conv_w, mlstm_i_bias, mlstm_f_bias, fox_norm_g, mlstm_norm_g, w_mix_out, ln1_g, ln1_b, w_xq, w_xk, w_xv, w_xo, ln2_g, ln2_b, w_router, b_router, w_gate_up, b_gate_up, w_down, b_down, ln3_g, ln3_b):
    for l in range(w_in.shape[0]):
        x = _layer(x, mem, w_in[l], fox_f_bias[l], mlstm_conv_w[l], mlstm_i_bias[l],
                   mlstm_f_bias[l], fox_norm_g[l], mlstm_norm_g[l], w_mix_out[l],
                   ln1_g[l], ln1_b[l], w_xq[l], w_xk[l], w_xv[l], w_xo[l], ln2_g[l], ln2_b[l],
                   w_router[l], b_router[l], w_gate_up[l], b_gate_up[l], w_down[l], b_down[l],
                   ln3_g[l], ln3_b[l])
    return x
```

```python
import jax
import jax.numpy as jnp
from jax import lax
from jax.experimental import pallas as pl
from jax.experimental.pallas import tpu as pltpu
from jax.experimental.pallas import tpu_sc as plsc

F32 = jnp.float32
BF16 = jnp.bfloat16

D_MODEL = 1024
FOX_HEADS = 8
FOX_HEAD_DIM = 64
FOX_WIDTH = FOX_HEADS * FOX_HEAD_DIM
MLSTM_HEADS = 4
MLSTM_QK_DIM = 64
MLSTM_V_DIM = 128
MLSTM_QK_WIDTH = MLSTM_HEADS * MLSTM_QK_DIM
MLSTM_V_WIDTH = MLSTM_HEADS * MLSTM_V_DIM
CONV_WIDTH = 4
XATTN_HEADS = 4
XATTN_HEAD_DIM = D_MODEL // XATTN_HEADS
N_EXPERTS = 32
TOP_K = 4
D_EXPERT = D_MODEL
SWIGLU_LIMIT = 7.0
SWIGLU_ALPHA = 1.702
DEEPNORM_ALPHA = 2.0 ** 0.25
LN_EPS = 1e-5
RMS_EPS = 1e-6

LANES = 128
SEQ_BLOCK = 256
ROW_TILE = 1024
EXPERT_ROWS = 512
EXPERT_CHUNK = 256
ROUTE_TILE = 1024
X_PLANES = D_MODEL // 2 // LANES
Y_PLANES = X_PLANES
GATE_I0 = FOX_HEADS
GATE_F0 = FOX_HEADS + MLSTM_HEADS
GATE_ROWS = FOX_HEADS + 2 * MLSTM_HEADS

MIB = 1024 * 1024


def _params(semantics, vmem_mib):
    return pltpu.CompilerParams(dimension_semantics=semantics,
                                vmem_limit_bytes=vmem_mib * MIB)


def _layer_norm(y, g, b):
    mu = jnp.mean(y, axis=-1, keepdims=True)
    yc = y - mu
    var = jnp.mean(yc * yc, axis=-1, keepdims=True)
    return yc * lax.rsqrt(var + LN_EPS) * g + b


def _dot(a, b):
    return jnp.dot(a, b, preferred_element_type=F32)


def _dot_nt(a, b):
    return lax.dot_general(a, b, (((1,), (1,)), ((), ())), preferred_element_type=F32)


def _pack_bf16_pairs(x):
    w = x.shape[1] // 2
    lo = pltpu.bitcast(x[:, :w].astype(BF16).astype(F32), jnp.uint32)
    hi = pltpu.bitcast(x[:, w:].astype(BF16).astype(F32), jnp.uint32)
    return (lo >> 16) | hi


def _unpack_bf16_pairs(u, dtype=BF16):
    lo = pltpu.bitcast(u << 16, F32).astype(dtype)
    hi = pltpu.bitcast(u & jnp.uint32(0xFFFF0000), F32).astype(dtype)
    return jnp.concatenate([lo, hi], axis=1)


def _inproj_kernel(x_ref, w_ref, wt_ref, gb_ref, fqt_ref, fk_ref, fvt_ref, mqk_ref, mvt_ref,
                   mot_ref, g_ref):
    xb = x_ref[...].astype(BF16)

    def mm(c0, width):
        return _dot(xb, w_ref[:, c0:c0 + width])

    def mm_t(r0, height):
        return _dot_nt(wt_ref[r0:r0 + height, :], xb)

    r_fv = FOX_WIDTH
    r_mv = r_fv + FOX_WIDTH
    r_mo = r_mv + MLSTM_V_WIDTH
    fqt_ref[...] = (mm_t(0, FOX_WIDTH) * (FOX_HEAD_DIM ** -0.5)).astype(BF16)
    fvt_ref[...] = mm_t(r_fv, FOX_WIDTH).astype(BF16)
    mvt_ref[...] = mm_t(r_mv, MLSTM_V_WIDTH).astype(BF16)
    mot_ref[...] = mm_t(r_mo, MLSTM_V_WIDTH)
    c_qk = FOX_WIDTH
    c_g = c_qk + 2 * MLSTM_QK_WIDTH
    fk_ref[...] = mm(0, FOX_WIDTH).astype(BF16)
    mqk_ref[...] = mm(c_qk, 2 * MLSTM_QK_WIDTH)
    g_ref[...] = mm(c_g, LANES) + gb_ref[...]


def _inproj(x2d, w_r, w_t, gate_bias):
    n = x2d.shape[0]
    tm = ROW_TILE
    row = lambda width: pl.BlockSpec((tm, width), lambda i: (i, 0))
    col = lambda height: pl.BlockSpec((height, tm), lambda i: (0, i))
    full = lambda a: pl.BlockSpec(a.shape, lambda i: (0, 0))
    out_shapes = (
        jax.ShapeDtypeStruct((FOX_WIDTH, n), BF16),
        jax.ShapeDtypeStruct((n, FOX_WIDTH), BF16),
        jax.ShapeDtypeStruct((FOX_WIDTH, n), BF16),
        jax.ShapeDtypeStruct((n, 2 * MLSTM_QK_WIDTH), F32),
        jax.ShapeDtypeStruct((MLSTM_V_WIDTH, n), BF16),
        jax.ShapeDtypeStruct((MLSTM_V_WIDTH, n), F32),
        jax.ShapeDtypeStruct((n, LANES), F32),
    )
    return pl.pallas_call(
        _inproj_kernel,
        grid=(n // tm,),
        in_specs=[row(D_MODEL), full(w_r), full(w_t), full(gate_bias)],
        out_specs=(col(FOX_WIDTH), row(FOX_WIDTH), col(FOX_WIDTH), row(2 * MLSTM_QK_WIDTH),
                   col(MLSTM_V_WIDTH), col(MLSTM_V_WIDTH), row(LANES)),
        out_shape=out_shapes,
        compiler_params=_params(("parallel",), 48),
    )(x2d, w_r, w_t, gate_bias)


def _split3(x):
    hi = x.astype(BF16)
    r1 = x - hi.astype(F32)
    mid = r1.astype(BF16)
    lo = (r1 - mid.astype(F32)).astype(BF16)
    return hi, mid, lo


def _gateprep_kernel(g_ref, sel_ref, col_ref, row_ref, cp_ref, carry_ref):
    c = pl.program_id(1)

    @pl.when(c == 0)
    def _():
        carry_ref[...] = jnp.zeros_like(carry_ref)

    blk = SEQ_BLOCK
    lane = lax.broadcasted_iota(jnp.int32, (blk, LANES), 1)
    is_i = (lane >= GATE_I0) & (lane < GATE_F0)
    r = lax.broadcasted_iota(jnp.int32, (blk, blk), 0)
    s = lax.broadcasted_iota(jnp.int32, (blk, blk), 1)
    tri = (s <= r).astype(BF16)
    carry = carry_ref[0:1, :]
    globs = []
    for j in range(g_ref.shape[0] // blk):
        g = g_ref[j * blk:(j + 1) * blk, :]
        logsig = jnp.minimum(g, 0.0) - jnp.log1p(jnp.exp(-jnp.abs(g)))
        cs = None
        for piece in _split3(logsig):
            term = _dot(tri, piece)
            cs = term if cs is None else cs + term
        glob = cs + carry
        carry = glob[blk - 1:blk, :]
        out = jnp.where(lane < GATE_I0, glob, jnp.where(is_i, g, cs))
        col_ref[j * blk:(j + 1) * blk, :] = out
        row_ref[0, :, j * blk:(j + 1) * blk] = out.T[0:GATE_ROWS, :]
        globs.append(glob)
    carry_ref[...] = jnp.broadcast_to(carry, carry_ref.shape)
    glob = jnp.concatenate(globs, axis=0)

    pieces = jnp.concatenate(_split3(-glob), axis=1)
    moved = _dot(pieces, sel_ref[...])
    for p in range(FOX_HEADS // 2):
        cp_ref[0, p] = moved[:, p * LANES:(p + 1) * LANES].astype(BF16)


def _piece_selector():
    src = jnp.arange(3 * LANES, dtype=jnp.int32)[:, None]
    dst = jnp.arange(4 * LANES, dtype=jnp.int32)[None, :]
    piece, head = src // LANES, src % LANES
    pair, lane = dst // LANES, dst % LANES
    hit = (lane < 6) & (lane % 3 == piece) & (head == 2 * pair + lane // 3)
    return hit.astype(BF16)


def _gateprep(gates, batch, seq):
    n = gates.shape[0]
    rows = ROW_TILE
    nc = seq // rows
    sel = _piece_selector()
    return pl.pallas_call(
        _gateprep_kernel,
        grid=(batch, nc),
        in_specs=[pl.BlockSpec((rows, LANES), lambda b, c: (b * nc + c, 0)),
                  pl.BlockSpec(sel.shape, lambda b, c: (0, 0))],
        out_specs=(pl.BlockSpec((rows, LANES), lambda b, c: (b * nc + c, 0)),
                   pl.BlockSpec((1, GATE_ROWS, rows), lambda b, c: (b, 0, c)),
                   pl.BlockSpec((1, FOX_HEADS // 2, rows, LANES), lambda b, c: (b, 0, c, 0))),
        out_shape=(jax.ShapeDtypeStruct((n, LANES), F32),
                   jax.ShapeDtypeStruct((batch, GATE_ROWS, seq), F32),
                   jax.ShapeDtypeStruct((batch, FOX_HEADS // 2, seq, LANES), BF16)),
        scratch_shapes=[pltpu.VMEM((8, LANES), F32)],
        compiler_params=_params(("parallel", "arbitrary"), 32),
    )(gates, sel)


FOX_ONES_ROWS = 16
FOX_VAUG_ROWS = FOX_HEAD_DIM + FOX_ONES_ROWS
FOX_QUERY_TILE = 2 * SEQ_BLOCK


def _fox_kernel(qt_ref, k_ref, cp_ref, vt_ref, gain_ref, o_ref,
                kaug_sc, vaug_sc, qaug_sc, sta_sc, stb_sc, stc_sc, m_sc, acc0_sc, acc1_sc):
    qi = pl.program_id(2)
    tq = o_ref.shape[0]
    tg = tq
    tk = SEQ_BLOCK
    hd = FOX_HEAD_DIM
    seq = k_ref.shape[0]
    nq = seq // tq
    par = qi % 2

    def load_queries(tile, slot):
        q0 = pl.multiple_of(tile * tq, tq)
        for h in range(2):
            qaug_sc[slot, h, 0:hd, :] = qt_ref[h * hd:(h + 1) * hd, pl.ds(q0, tq)]

    @pl.when(qi == 0)
    def _():
        ones3 = (lax.broadcasted_iota(jnp.int32, (FOX_ONES_ROWS, tq), 0) < 3).astype(BF16)
        for slot in range(2):
            for h in range(2):
                qaug_sc[slot, h, hd:hd + FOX_ONES_ROWS, :] = ones3
                qaug_sc[slot, h, hd + FOX_ONES_ROWS:, :] = jnp.zeros(
                    (LANES - hd - FOX_ONES_ROWS, tq), BF16)
        load_queries(0, 0)

    nxt = jnp.minimum(qi + 1, nq - 1)
    load_queries(nxt, 1 - par)

    @pl.when(qi == 0)
    def _():
        lane = lax.broadcasted_iota(jnp.int32, (tk, LANES), 1)

        def build(blk, carry):
            r0 = pl.multiple_of(blk * tk, tk)
            kp = k_ref[pl.ds(r0, tk), :].astype(F32)
            cp = cp_ref[0, 0, pl.ds(r0, tk), :].astype(F32)
            for h in range(2):
                kh = kp if h == 0 else pltpu.roll(kp, hd, axis=1)
                ch = pltpu.roll(cp, hd - 3 * h, axis=1)
                kaug = jnp.where(lane < hd, kh, jnp.where(lane < hd + 3, ch, 0.0))
                kaug_sc[h, pl.ds(r0, tk), :] = kaug.astype(BF16)
            return carry

        lax.fori_loop(0, seq // tk, build, 0)
        for h in range(2):
            vaug_sc[h, 0:hd, :] = vt_ref[h * hd:(h + 1) * hd, :]
            vaug_sc[h, hd:, :] = jnp.ones((FOX_ONES_ROWS, seq), BF16)

    half = tg // 2

    def put_scores(kg, slot, queries=None):
        k0 = pl.multiple_of(kg * tg, tg)
        queries = par if queries is None else queries
        for h in range(2):
            slot[h] = _dot(kaug_sc[h, pl.ds(k0, tg), :], qaug_sc[queries, h])

    def put_diagonal(slot):
        k0 = pl.multiple_of(qi * tg, tg)
        causal_top = (lax.broadcasted_iota(jnp.int32, (half, tq), 0)
                      <= lax.broadcasted_iota(jnp.int32, (half, tq), 1))
        causal_bot = (lax.broadcasted_iota(jnp.int32, (half, half), 0)
                      <= lax.broadcasted_iota(jnp.int32, (half, half), 1))
        for h in range(2):
            top = _dot(kaug_sc[h, pl.ds(k0, half), :], qaug_sc[par, h])
            slot[h, 0:half, :] = jnp.where(causal_top, top, -jnp.inf)
            bot = _dot(kaug_sc[h, pl.ds(k0 + half, half), :], qaug_sc[par, h, :, half:])
            slot[h, half:, half:] = jnp.where(causal_bot, bot, -jnp.inf)

    m_sc[...] = jnp.full(m_sc.shape, -jnp.inf, F32)
    acc0_sc[...] = jnp.zeros(acc0_sc.shape, F32)
    acc1_sc[...] = jnp.zeros(acc1_sc.shape, F32)
    acc = (acc0_sc, acc1_sc)

    def absorb(kg, slot):
        k0 = pl.multiple_of(kg * tg, tg)
        for h in range(2):
            st = slot[h]
            m_prev = m_sc[h]
            m_new = jnp.maximum(m_prev, jnp.max(st, axis=0, keepdims=True))
            pv = None
            for u in range(2):
                p = jnp.exp(st[u * half:(u + 1) * half] - m_new)
                part = _dot(vaug_sc[h, :, pl.ds(k0 + u * half, half)], p.astype(BF16))
                pv = part if pv is None else pv + part
            acc[h][...] = jnp.exp(m_prev - m_new) * acc[h][...] + pv
            m_sc[h] = m_new

    def absorb_diagonal(slot):
        k0 = pl.multiple_of(qi * tg, tg)
        for h in range(2):
            top = slot[h, 0:half, :]
            bot = slot[h, half:, half:]
            m_prev = m_sc[h]
            m_sc[h] = jnp.maximum(m_prev, jnp.max(top, axis=0, keepdims=True))
            m_sc[h, :, half:] = jnp.maximum(m_sc[h, :, half:], jnp.max(bot, axis=0, keepdims=True))
            m_new = m_sc[h]
            acc[h][...] = (jnp.exp(m_prev - m_new) * acc[h][...]
                           + _dot(vaug_sc[h, :, pl.ds(k0, half)],
                                  jnp.exp(top - m_new).astype(BF16)))
            acc[h][:, half:] += _dot(vaug_sc[h, :, pl.ds(k0 + half, half)],
                                     jnp.exp(bot - m_sc[h, :, half:]).astype(BF16))

    sa, sb, sc = sta_sc, stb_sc, stc_sc
    n_loop = jnp.maximum(qi - 1, 0) // 2

    def body(j, carry):
        put_scores(2 * j + 1, sb)
        absorb(2 * j, sa)
        put_scores(2 * j + 2, sa)
        absorb(2 * j + 1, sb)
        return carry

    lax.fori_loop(0, n_loop, body, 0)
    done = 2 * n_loop

    def put_next():
        put_scores(0, sa, queries=1 - par)

    @pl.when(qi == 0)
    def _():
        put_diagonal(sc)
        put_next()
        absorb_diagonal(sc)

    @pl.when((qi > 0) & (qi - done == 1))
    def _():
        put_diagonal(sc)
        absorb(done, sa)
        put_next()
        absorb_diagonal(sc)

    @pl.when((qi > 0) & (qi - done == 2))
    def _():
        put_scores(done + 1, sb)
        absorb(done, sa)
        put_diagonal(sc)
        absorb(done + 1, sb)
        put_next()
        absorb_diagonal(sc)

    normed = []
    for h in range(2):
        a = acc[h][...]
        oh = a[0:hd] * (1.0 / a[hd:hd + 1])
        ms = jnp.mean(oh * oh, axis=0, keepdims=True)
        gain = gain_ref[h * hd:(h + 1) * hd, :]
        normed.append(oh * lax.rsqrt(ms + RMS_EPS) * jnp.concatenate([gain] * (tq // LANES), axis=1))
    o_ref[...] = jnp.concatenate(normed, axis=0).T.astype(o_ref.dtype)


def _fox(fqt, fk, cpieces, fvt, fox_gain, batch, seq):
    n = fk.shape[0]
    tq = FOX_QUERY_TILE
    nq = seq // tq
    npair = FOX_HEADS // 2
    return pl.pallas_call(
        _fox_kernel,
        grid=(batch, npair, nq),
        in_specs=[pl.BlockSpec((LANES, seq), lambda b, hp, qi: (hp, b)),
                  pl.BlockSpec((seq, LANES), lambda b, hp, qi: (b, hp)),
                  pl.BlockSpec((1, 1, seq, LANES), lambda b, hp, qi: (b, hp, 0, 0)),
                  pl.BlockSpec((LANES, seq), lambda b, hp, qi: (hp, b)),
                  pl.BlockSpec((LANES, LANES), lambda b, hp, qi: (hp, 0))],
        out_specs=pl.BlockSpec((tq, LANES), lambda b, hp, qi: (b * nq + qi, hp)),
        out_shape=jax.ShapeDtypeStruct((n, FOX_WIDTH), BF16),
        scratch_shapes=[pltpu.VMEM((2, seq, LANES), BF16),
                        pltpu.VMEM((2, FOX_VAUG_ROWS, seq), BF16),
                        pltpu.VMEM((2, 2, LANES, tq), BF16),
                        pltpu.VMEM((2, tq, tq), F32),
                        pltpu.VMEM((2, tq, tq), F32),
                        pltpu.VMEM((2, tq, tq), F32),
                        pltpu.VMEM((2, 1, tq), F32),
                        pltpu.VMEM((FOX_VAUG_ROWS, tq), F32),
                        pltpu.VMEM((FOX_VAUG_ROWS, tq), F32)],
        compiler_params=_params(("parallel", "parallel", "arbitrary"), 48),
    )(fqt, fk, cpieces, fvt, fox_gain)


MLSTM_ONES_ROWS = 16


def _mlstm_kernel(qk_ref, vt_ref, ogt_ref, col_ref, row_ref, cw_ref, gain_ref, o_ref,
                  tail_ref, buf_ref, c_sc, m_sc):
    c = pl.program_id(1)
    L = SEQ_BLOCK

    @pl.when(c == 0)
    def _():
        tail_ref[...] = jnp.zeros_like(tail_ref)
        c_sc[...] = jnp.zeros_like(c_sc)
        m_sc[...] = jnp.zeros_like(m_sc)

    rows_step = qk_ref.shape[0]
    x = qk_ref[...]
    buf_ref[0:8, :] = tail_ref[...]
    buf_ref[8:8 + rows_step, :] = x
    tail_ref[...] = x[rows_step - 8:rows_step, :]
    y = x * cw_ref[CONV_WIDTH - 1:CONV_WIDTH, :]
    for j in range(CONV_WIDTH - 1):
        shift = CONV_WIDTH - 1 - j
        y = y + buf_ref[8 - shift:8 - shift + rows_step, :] * cw_ref[j:j + 1, :]
    y = y * jax.nn.sigmoid(y)
    n_pair = MLSTM_HEADS // 2

    lane = lax.broadcasted_iota(jnp.int32, (L, LANES), 1)
    sub = lax.broadcasted_iota(jnp.int32, (LANES, L), 0)
    ss = lax.broadcasted_iota(jnp.int32, (L, L), 0)
    ll = lax.broadcasted_iota(jnp.int32, (L, L), 1)
    causal = ss <= ll
    ones_rows = jnp.ones((MLSTM_ONES_ROWS, L), BF16)
    cstates = [c_sc[h] for h in range(MLSTM_HEADS)]
    mstates = [m_sc[h][0:1, 0:1] for h in range(MLSTM_HEADS)]

    per_chunk = []
    for chunk in range(rows_step // L):
        r0 = chunk * L
        yc = y[r0:r0 + L, :]
        kf = yc[:, MLSTM_QK_WIDTH:] * (MLSTM_QK_DIM ** -0.5)
        per_chunk.append((
            [yc[:, p * LANES:(p + 1) * LANES].T for p in range(n_pair)],
            [kf[:, p * LANES:(p + 1) * LANES].astype(BF16) for p in range(n_pair)],
            col_ref[r0:r0 + L, :],
            row_ref[0, :, r0:r0 + L]))

    for chunk, h in [(c_, h_) for c_ in range(rows_step // L) for h_ in range(MLSTM_HEADS)]:
        r0 = chunk * L
        qt, kb, col, row = per_chunk[chunk]
        pair, half = divmod(h, 2)
        head_sub = (sub < MLSTM_QK_DIM) if half == 0 else (sub >= MLSTM_QK_DIM)
        head_lane = (lane < MLSTM_QK_DIM) if half == 0 else (lane >= MLSTM_QK_DIM)
        qth = jnp.where(head_sub, qt[pair], 0.0).astype(BF16)
        rcol = col[:, GATE_I0 + h:GATE_I0 + h + 1] - col[:, GATE_F0 + h:GATE_F0 + h + 1]
        brow = row[GATE_F0 + h:GATE_F0 + h + 1, :]
        lirow = row[GATE_I0 + h:GATE_I0 + h + 1, :]
        g = brow[:, L - 1:L]
        m_prev = mstates[h]

        dt = jnp.where(causal, rcol + brow, -jnp.inf)
        inter_log = brow + m_prev
        m_t = jnp.maximum(inter_log, jnp.max(dt, axis=0, keepdims=True))
        w_inter = jnp.exp(inter_log - m_t)
        pt = jnp.exp(dt - m_t) * _dot(kb[pair], qth)
        vaug = jnp.concatenate([vt_ref[h * LANES:(h + 1) * LANES, r0:r0 + L], ones_rows],
                               axis=0)
        cstate = cstates[h]
        tot = w_inter * _dot(cstate.astype(BF16), qth) + _dot(vaug, pt.astype(BF16))
        den = tot[MLSTM_V_DIM:MLSTM_V_DIM + 1]
        hout = tot[0:MLSTM_V_DIM] * (1.0 / jnp.maximum(jnp.abs(den), jnp.exp(-m_t)))

        a = g + (lirow - brow)
        m_loc = jnp.max(a, axis=1, keepdims=True)
        vw = (vaug.astype(F32) * jnp.exp(a - m_loc)).astype(BF16)
        kmask = jnp.where(head_lane, kb[pair], jnp.zeros_like(kb[pair]))
        kv = _dot(vw, kmask)
        m_new = jnp.maximum(g + m_prev, m_loc)
        cstates[h] = jnp.exp(g + m_prev - m_new) * cstate + jnp.exp(m_loc - m_new) * kv
        mstates[h] = m_new

        ms = jnp.mean(hout * hout, axis=0, keepdims=True)
        gain = gain_ref[h * LANES:(h + 1) * LANES, :]
        hn = hout * lax.rsqrt(ms + RMS_EPS) * jnp.concatenate([gain] * (L // LANES), axis=1)
        gate = jax.nn.sigmoid(ogt_ref[h * LANES:(h + 1) * LANES, r0:r0 + L])
        o_ref[r0:r0 + L, h * LANES:(h + 1) * LANES] = (hn * gate).T.astype(o_ref.dtype)

    for h in range(MLSTM_HEADS):
        c_sc[h] = cstates[h]
        m_sc[h] = jnp.broadcast_to(mstates[h], m_sc.shape[1:])


def _mlstm(mqk, mvt, mot, gcol, grow, conv_w, gain_lanes, batch, seq):
    n = mqk.shape[0]
    L = ROW_TILE
    nc = seq // L
    qk_width = 2 * MLSTM_QK_WIDTH
    row = lambda width: pl.BlockSpec((L, width), lambda b, c: (b * nc + c, 0))
    col = pl.BlockSpec((MLSTM_V_WIDTH, L), lambda b, c: (0, b * nc + c))
    full = lambda a: pl.BlockSpec(a.shape, lambda b, c: (0, 0))
    return pl.pallas_call(
        _mlstm_kernel,
        grid=(batch, nc),
        in_specs=[row(qk_width), col, col, row(LANES),
                  pl.BlockSpec((1, GATE_ROWS, L), lambda b, c: (b, 0, c)),
                  full(conv_w), full(gain_lanes)],
        out_specs=row(MLSTM_V_WIDTH),
        out_shape=jax.ShapeDtypeStruct((n, MLSTM_V_WIDTH), BF16),
        scratch_shapes=[pltpu.VMEM((8, qk_width), F32), pltpu.VMEM((8 + L, qk_width), F32),
                        pltpu.VMEM((MLSTM_HEADS, MLSTM_V_DIM + MLSTM_ONES_ROWS, LANES), F32),
                        pltpu.VMEM((MLSTM_HEADS, 8, LANES), F32)],
        compiler_params=_params(("parallel", "arbitrary"), 48),
    )(mqk, mvt, mot, gcol, grow, conv_w, gain_lanes)


def _outproj_kernel(fo_ref, mo_ref, w_ref, x_ref, g_ref, b_ref, o_ref):
    mix = _dot(fo_ref[...], w_ref[0:FOX_WIDTH, :]) + _dot(mo_ref[...], w_ref[FOX_WIDTH:, :])
    o_ref[...] = _layer_norm(DEEPNORM_ALPHA * x_ref[...] + mix, g_ref[...], b_ref[...])


def _outproj(fo, mo, w_out, x2d, g, b):
    n = x2d.shape[0]
    tm = ROW_TILE
    row = lambda width: pl.BlockSpec((tm, width), lambda i: (i, 0))
    full = lambda a: pl.BlockSpec(a.shape, lambda i: (0, 0))
    return pl.pallas_call(
        _outproj_kernel,
        grid=(n // tm,),
        in_specs=[row(512), row(512), full(w_out), row(D_MODEL), full(g), full(b)],
        out_specs=row(D_MODEL),
        out_shape=jax.ShapeDtypeStruct((n, D_MODEL), F32),
        compiler_params=_params(("parallel",), 48),
    )(fo, mo, w_out, x2d, g, b)


def _memkv_kernel(mem_ref, wk_ref, wv_ref, k_ref, v_ref):
    mb = mem_ref[...].astype(BF16)
    k_ref[...] = (_dot(mb, wk_ref[...]) * (XATTN_HEAD_DIM ** -0.5)).astype(BF16)
    v_ref[...] = _dot(mb, wv_ref[...]).astype(BF16)


def _memkv(mem2d, wk, wv, n_mem):
    n = mem2d.shape[0]
    row = pl.BlockSpec((n_mem, D_MODEL), lambda i: (i, 0))
    full = lambda a: pl.BlockSpec(a.shape, lambda i: (0, 0))
    return pl.pallas_call(
        _memkv_kernel,
        grid=(n // n_mem,),
        in_specs=[row, full(wk), full(wv)],
        out_specs=(row, row),
        out_shape=(jax.ShapeDtypeStruct((n, D_MODEL), BF16),) * 2,
        compiler_params=_params(("parallel",), 32),
    )(mem2d, wk, wv)


def _xattn_kernel(x_ref, k_ref, v_ref, wq_ref, wo_ref, g_ref, b_ref, wr_ref, br_ref,
                  o_ref, ob_ref, lg_ref):
    half = x_ref.shape[0] // 2
    for r in range(2):
        rows = slice(r * half, (r + 1) * half)
        x = x_ref[rows, :]
        q = _dot(x.astype(BF16), wq_ref[...]).astype(BF16)
        outs = []
        for h in range(XATTN_HEADS):
            sl = slice(h * XATTN_HEAD_DIM, (h + 1) * XATTN_HEAD_DIM)
            s = _dot_nt(q[:, sl], k_ref[:, sl])
            p = jnp.exp(s - jnp.max(s, axis=-1, keepdims=True))
            l = jnp.sum(p, axis=-1, keepdims=True)
            outs.append((_dot(p.astype(BF16), v_ref[:, sl]) / l).astype(BF16))
        o = jnp.concatenate(outs, axis=1)
        xa = _dot(o, wo_ref[...])
        x2 = _layer_norm(DEEPNORM_ALPHA * x + xa, g_ref[...], b_ref[...])
        o_ref[rows, :] = x2
        packed = _pack_bf16_pairs(x2)
        for j in range(ob_ref.shape[0]):
            ob_ref[j, rows, :] = packed[:, j * LANES:(j + 1) * LANES]
        x2h = x2.astype(BF16)
        x2l = (x2 - x2h.astype(F32)).astype(BF16)
        a = _dot_nt(wr_ref[...], x2h)
        b = _dot_nt(wr_ref[0:N_EXPERTS, :], x2l)
        lg_ref[:, rows] = a[0:N_EXPERTS] + a[N_EXPERTS:] + b + br_ref[...]


def _xattn(x1, kmem, vmem, wq, wo, g, b, wr, br, batch, seq, n_mem):
    n = x1.shape[0]
    tm = ROW_TILE
    nt = seq // tm
    row = lambda width: pl.BlockSpec((tm, width), lambda bi, i: (bi * nt + i, 0))
    full = lambda a: pl.BlockSpec(a.shape, lambda bi, i: (0, 0))
    kv = pl.BlockSpec((n_mem, D_MODEL), lambda bi, i: (bi, 0))
    return pl.pallas_call(
        _xattn_kernel,
        grid=(batch, nt),
        in_specs=[row(D_MODEL), kv, kv, full(wq), full(wo), full(g), full(b), full(wr), full(br)],
        out_specs=(row(D_MODEL),
                   pl.BlockSpec((X_PLANES, tm, LANES), lambda bi, i: (0, bi * nt + i, 0)),
                   pl.BlockSpec((N_EXPERTS, tm), lambda bi, i: (0, bi * nt + i))),
        out_shape=(jax.ShapeDtypeStruct((n, D_MODEL), F32),
                   jax.ShapeDtypeStruct((X_PLANES, n, LANES), jnp.uint32),
                   jax.ShapeDtypeStruct((N_EXPERTS, n), F32)),
        compiler_params=_params(("parallel", "parallel"), 60),
    )(x1, kmem, vmem, wq, wo, g, b, wr, br)


def _route_kernel(lg_ref, idx_ref, rank_ref, gate_ref, cnt_ref, carry_ref):
    i = pl.program_id(0)

    @pl.when(i == 0)
    def _():
        carry_ref[...] = jnp.zeros_like(carry_ref)

    lg = lg_ref[...]
    t = lg.shape[1]
    e_idx = lax.broadcasted_iota(jnp.int32, lg.shape, 0).astype(F32)
    sels, vals, idxs = [], [], []
    for _ in range(TOP_K):
        mx = jnp.max(lg, axis=0, keepdims=True)
        first = jnp.min(jnp.where(lg == mx, e_idx, float(N_EXPERTS)), axis=0, keepdims=True)
        sel = e_idx == first
        sels.append(sel)
        vals.append(mx)
        idxs.append(first)
        lg = jnp.where(sel, -jnp.inf, lg)
    exps = [jnp.exp(v - vals[0]) for v in vals]
    tot = exps[0] + exps[1] + exps[2] + exps[3]

    selmat = (sels[0] | sels[1] | sels[2] | sels[3])
    r = lax.broadcasted_iota(jnp.int32, (t, t), 0)
    s = lax.broadcasted_iota(jnp.int32, (t, t), 1)
    earlier = (r < s).astype(BF16)
    carry = carry_ref[:, 0:1]
    rankmat = _dot(selmat.astype(BF16), earlier) + carry
    new_carry = carry + jnp.sum(selmat.astype(F32), axis=1, keepdims=True)
    carry_ref[...] = jnp.broadcast_to(new_carry, carry_ref.shape)
    cnt_ref[...] = jnp.broadcast_to(new_carry, cnt_ref.shape).astype(jnp.int32)

    row8 = lax.broadcasted_iota(jnp.int32, (8, t), 0)
    row128 = lax.broadcasted_iota(jnp.int32, (LANES, t), 0)
    idx_out = jnp.zeros((8, t), F32)
    rank_out = jnp.zeros((8, t), F32)
    gate_out = jnp.zeros((LANES, t), F32)
    for k in range(TOP_K):
        rk = jnp.sum(jnp.where(sels[k], rankmat, 0.0), axis=0, keepdims=True)
        idx_out = jnp.where(row8 == k, idxs[k], idx_out)
        rank_out = jnp.where(row8 == k, rk, rank_out)
        gate_out = jnp.where(row128 == k, exps[k] / tot, gate_out)
    idx_ref[...] = idx_out.astype(jnp.int32)
    rank_ref[...] = rank_out.astype(jnp.int32)
    gate_ref[...] = gate_out.T


def _route(logits_t):
    n = logits_t.shape[1]
    t = ROUTE_TILE
    col = lambda rows: pl.BlockSpec((rows, t), lambda i: (0, i))
    return pl.pallas_call(
        _route_kernel,
        grid=(n // t,),
        in_specs=[col(N_EXPERTS)],
        out_specs=(col(8), col(8), pl.BlockSpec((t, LANES), lambda i: (i, 0)),
                   pl.BlockSpec((N_EXPERTS, LANES), lambda i: (0, 0))),
        out_shape=(jax.ShapeDtypeStruct((8, n), jnp.int32),
                   jax.ShapeDtypeStruct((8, n), jnp.int32),
                   jax.ShapeDtypeStruct((n, LANES), F32),
                   jax.ShapeDtypeStruct((N_EXPERTS, LANES), jnp.int32)),
        scratch_shapes=[pltpu.VMEM((N_EXPERTS, LANES), F32)],
        compiler_params=_params(("arbitrary",), 32),
    )(logits_t)


def _expert_kernel(blk_e_ref, blk_rows_ref, next_e_ref, x_ref, wgu_hbm, bgu_ref, wd_hbm, bd_ref,
                   y_ref, wgu_f32, wd_f32, wgu_sc, wd_sc, state_ref, sem):
    i = pl.program_id(0)
    e = blk_e_ref[i]
    rows = blk_rows_ref[i]
    g = y_ref.shape[1]

    def weight_copies(expert, slot):
        return (pltpu.make_async_copy(wgu_hbm.at[expert], wgu_f32.at[slot], sem.at[0, slot]),
                pltpu.make_async_copy(wd_hbm.at[expert], wd_f32.at[slot], sem.at[1, slot]))

    @pl.when(i == 0)
    def _():
        state_ref[0] = -1
        state_ref[1] = 0

    @pl.when((rows > 0) & (e != state_ref[0]))
    def _():
        slot = state_ref[1]

        @pl.when(i == 0)
        def _():
            for cp in weight_copies(e, slot):
                cp.start()

        for cp in weight_copies(e, slot):
            cp.wait()
        wgu_sc[...] = wgu_f32[slot].astype(BF16)
        wd_sc[...] = wd_f32[slot].astype(BF16)
        nxt = next_e_ref[i]

        @pl.when(nxt >= 0)
        def _():
            for cp in weight_copies(nxt, 1 - slot):
                cp.start()

        state_ref[0] = e
        state_ref[1] = 1 - slot

    def ffn(m):
        packed = jnp.concatenate([x_ref[j, 0:m, :] for j in range(X_PLANES)], axis=1)
        xb = _unpack_bf16_pairs(packed)
        hids = []
        for c in range(D_EXPERT // EXPERT_CHUNK):
            g0 = c * EXPERT_CHUNK
            l0 = D_EXPERT + g0
            gate = _dot(xb, wgu_sc[:, g0:g0 + EXPERT_CHUNK]) + bgu_ref[0, :, g0:g0 + EXPERT_CHUNK]
            lin = _dot(xb, wgu_sc[:, l0:l0 + EXPERT_CHUNK]) + bgu_ref[0, :, l0:l0 + EXPERT_CHUNK]
            gate = jnp.minimum(gate, SWIGLU_LIMIT)
            lin = jnp.clip(lin, -SWIGLU_LIMIT, SWIGLU_LIMIT)
            hids.append((gate * jax.nn.sigmoid(SWIGLU_ALPHA * gate) * (lin + 1.0)).astype(BF16))
        y = _pack_bf16_pairs(_dot(jnp.concatenate(hids, axis=1), wd_sc[...]) + bd_ref[0])
        for j in range(Y_PLANES):
            y_ref[j, 0:m, :] = y[:, j * LANES:(j + 1) * LANES]

    @pl.when(rows > g // 2)
    def _():
        ffn(g)

    @pl.when((rows > 0) & (rows <= g // 2))
    def _():
        ffn(g // 2)
        y_ref[:, g // 2:, :] = jnp.zeros((Y_PLANES, g // 2, LANES), y_ref.dtype)

    @pl.when(rows == 0)
    def _():
        y_ref[...] = jnp.zeros_like(y_ref)


def _experts(blk_e, blk_rows, next_e, xs, w_gu, b_gu, w_d, b_d):
    p = xs.shape[1]
    g = EXPERT_ROWS
    grid_spec = pltpu.PrefetchScalarGridSpec(
        num_scalar_prefetch=3,
        grid=(p // g,),
        in_specs=[pl.BlockSpec((X_PLANES, g, LANES), lambda i, be, br, ne: (0, i, 0)),
                  pl.BlockSpec(memory_space=pl.ANY),
                  pl.BlockSpec((1, 1, 2 * D_EXPERT), lambda i, be, br, ne: (be[i], 0, 0)),
                  pl.BlockSpec(memory_space=pl.ANY),
                  pl.BlockSpec((1, 1, D_MODEL), lambda i, be, br, ne: (be[i], 0, 0))],
        out_specs=pl.BlockSpec((Y_PLANES, g, LANES), lambda i, be, br, ne: (0, i, 0)),
        scratch_shapes=[pltpu.VMEM((2, D_MODEL, 2 * D_EXPERT), F32),
                        pltpu.VMEM((2, D_EXPERT, D_MODEL), F32),
                        pltpu.VMEM((D_MODEL, 2 * D_EXPERT), BF16),
                        pltpu.VMEM((D_EXPERT, D_MODEL), BF16),
                        pltpu.SMEM((2,), jnp.int32),
                        pltpu.SemaphoreType.DMA((2, 2))],
    )
    return pl.pallas_call(
        _expert_kernel,
        grid_spec=grid_spec,
        out_shape=jax.ShapeDtypeStruct((Y_PLANES, p, LANES), jnp.uint32),
        compiler_params=_params(("arbitrary",), 56),
    )(blk_e, blk_rows, next_e, xs, w_gu, b_gu, w_d, b_d)


def _combine_kernel(*refs):
    y_refs = refs[:TOP_K * Y_PLANES]
    gate_ref, x_ref, g_ref, b_ref, o_ref = refs[TOP_K * Y_PLANES:]
    gate = gate_ref[...]
    ff = None
    for k in range(TOP_K):
        packed = jnp.concatenate([y_refs[k * Y_PLANES + j][...] for j in range(Y_PLANES)], axis=1)
        yk = _unpack_bf16_pairs(packed, F32) * gate[:, k:k + 1]
        ff = yk if ff is None else ff + yk
    o_ref[...] = _layer_norm(DEEPNORM_ALPHA * x_ref[...] + ff, g_ref[...], b_ref[...])


def _combine(yg, gate, x2, g, b):
    n = x2.shape[0]
    tm = ROW_TILE
    row = lambda width: pl.BlockSpec((tm, width), lambda i: (i, 0))
    full = lambda a: pl.BlockSpec(a.shape, lambda i: (0, 0))
    planes = [pl.BlockSpec((tm, LANES), lambda i, kj=kj: (kj * (n // tm) + i, 0))
              for kj in range(TOP_K * Y_PLANES)]
    return pl.pallas_call(
        _combine_kernel,
        grid=(n // tm,),
        in_specs=planes + [row(LANES), row(D_MODEL), full(g), full(b)],
        out_specs=row(D_MODEL),
        out_shape=jax.ShapeDtypeStruct((n, D_MODEL), F32),
        compiler_params=_params(("parallel",), 56),
    )(*([yg] * (TOP_K * Y_PLANES)), gate, x2, g, b)


SC_WINDOW = 128
SC_GATHERS_IN_FLIGHT = 2


def _sc_mesh():
    return plsc.VectorSubcoreMesh(core_axis_name="core", subcore_axis_name="subcore")


def _sc_gather(table, idx):
    n_fly = SC_GATHERS_IN_FLIGHT
    step = n_fly * SC_WINDOW
    m = -(-idx.shape[0] // step) * step
    idx = jnp.concatenate([idx, jnp.zeros((m - idx.shape[0],), idx.dtype)])

    @pl.kernel(out_type=jax.ShapeDtypeStruct((m, LANES), table.dtype), mesh=_sc_mesh(),
               scratch_types=[pltpu.SemaphoreType.DMA])
    def gather_kernel(table_hbm, idx_hbm, out_hbm, sem):
        def body(*refs):
            idx_vmem, out_vmem = refs[:n_fly], refs[n_fly]
            copies = [pltpu.async_copy(table_hbm.at[iv.at[0]],
                                       out_vmem.at[pl.ds(u * SC_WINDOW, SC_WINDOW)], sem)
                      for u, iv in enumerate(idx_vmem)]
            for cp in copies:
                cp.wait()

        pltpu.emit_pipeline(
            body,
            grid=(m // (n_fly * SC_WINDOW),),
            in_specs=[pl.BlockSpec((1, SC_WINDOW), lambda i, u=u: (0, n_fly * i + u))
                      for u in range(n_fly)],
            out_specs=[pl.BlockSpec((n_fly * SC_WINDOW, LANES), lambda i: (i, 0))],
            core_axis_name=("core", "subcore"),
            dimension_semantics=(pltpu.PARALLEL,),
        )(*([idx_hbm] * n_fly), out_hbm)

    return gather_kernel(table, idx.reshape(1, m))


def _sc_scatter(src, idx_lists, out_rows):
    m = src.shape[0]
    n_lists = len(idx_lists)

    @pl.kernel(out_type=jax.ShapeDtypeStruct((out_rows, LANES), src.dtype), mesh=_sc_mesh(),
               scratch_types=[pltpu.SemaphoreType.DMA])
    def scatter_kernel(src_hbm, *refs):
        idx_hbm, out_hbm, sem = refs[:n_lists], refs[n_lists], refs[n_lists + 1]

        def body(src_vmem, *idx_vmem):
            copies = [pltpu.async_copy(src_vmem, out_hbm.at[iv.at[0]], sem) for iv in idx_vmem]
            for cp in copies:
                cp.wait()

        pltpu.emit_pipeline(
            body,
            grid=(m // SC_WINDOW,),
            in_specs=[pl.BlockSpec((SC_WINDOW, LANES), lambda i: (i, 0))]
            + [pl.BlockSpec((1, SC_WINDOW), lambda i: (0, i))] * n_lists,
            out_specs=[],
            core_axis_name=("core", "subcore"),
            dimension_semantics=(pltpu.PARALLEL,),
        )(src_hbm, *idx_hbm)

    return scatter_kernel(src, *[ix.reshape(1, m) for ix in idx_lists])


def _layer(x, mem, w_in, fox_f_bias, conv_w, i_bias, f_bias, fox_g, mlstm_g, w_mix_out,
           ln1_g, ln1_b, w_xq, w_xk, w_xv, w_xo, ln2_g, ln2_b, w_router, b_router,
           w_gate_up, b_gate_up, w_down, b_down, ln3_g, ln3_b):
    batch, seq, d = x.shape
    n_mem = mem.shape[1]
    n = batch * seq
    x2d = x.reshape(n, d)

    o_ff = 3 * FOX_WIDTH
    o_mqk = o_ff + FOX_HEADS
    o_mv = o_mqk + 2 * MLSTM_QK_WIDTH
    o_mi = o_mv + MLSTM_V_WIDTH
    o_mf = o_mi + MLSTM_HEADS
    o_mo = o_mf + MLSTM_HEADS
    n_gate = FOX_HEADS + 2 * MLSTM_HEADS
    w_r = jnp.concatenate(
        [w_in[:, FOX_WIDTH:2 * FOX_WIDTH], w_in[:, o_mqk:o_mv],
         w_in[:, o_ff:o_mqk], w_in[:, o_mi:o_mo],
         jnp.zeros((d, LANES - n_gate), w_in.dtype)], axis=1).astype(BF16)
    w_t = jnp.concatenate([w_in[:, :FOX_WIDTH], w_in[:, 2 * FOX_WIDTH:o_ff],
                           w_in[:, o_mv:o_mi], w_in[:, o_mo:]], axis=1).T.astype(BF16)
    gate_bias = jnp.concatenate(
        [fox_f_bias, i_bias, f_bias, jnp.zeros((LANES - n_gate,), F32)]).reshape(1, LANES)

    fqt, fk, fvt, mqk, mvt, mot, gates = _inproj(x2d, w_r, w_t, gate_bias)
    gcol, grow, cpieces = _gateprep(gates, batch, seq)
    fox_gain_lanes = jnp.broadcast_to(fox_g[:, None], (FOX_WIDTH, LANES))
    fo = _fox(fqt, fk, cpieces, fvt, fox_gain_lanes, batch, seq)
    gain_lanes = jnp.broadcast_to(mlstm_g[:, None], (MLSTM_V_WIDTH, LANES))
    mo_out = _mlstm(mqk, mvt, mot, gcol, grow, conv_w, gain_lanes, batch, seq)
    x1 = _outproj(fo, mo_out, w_mix_out.astype(BF16), x2d, ln1_g.reshape(1, d), ln1_b.reshape(1, d))

    kmem, vmem = _memkv(mem.reshape(batch * n_mem, d), w_xk.astype(BF16), w_xv.astype(BF16), n_mem)
    wrt = w_router.T
    wrt_hi = wrt.astype(BF16)
    wrt_lo = (wrt - wrt_hi.astype(F32)).astype(BF16)
    x2, x2p, logits_t = _xattn(x1, kmem, vmem, w_xq.astype(BF16), w_xo.astype(BF16),
                               ln2_g.reshape(1, d), ln2_b.reshape(1, d),
                               jnp.concatenate([wrt_hi, wrt_lo], axis=0),
                               b_router.reshape(N_EXPERTS, 1), batch, seq, n_mem)

    idx_t, rank_t, gate, cnt = _route(logits_t)
    counts = cnt[:, 0]
    g_rows = EXPERT_ROWS
    padded = ((counts + g_rows - 1) // g_rows) * g_rows
    pad_end = jnp.cumsum(padded)
    pad_start = pad_end - padded
    experts = jnp.arange(N_EXPERTS, dtype=jnp.int32)
    sel = idx_t[:TOP_K, :, None] == experts[None, None, :]
    pos_t = jnp.sum(jnp.where(sel, pad_start[None, None, :], 0), axis=-1) + rank_t[:TOP_K]
    p_rows = n * TOP_K + N_EXPERTS * g_rows
    nb = p_rows // g_rows
    blk_start = jnp.arange(nb, dtype=jnp.int32) * g_rows
    blk_e = jnp.minimum(jnp.sum((pad_end[None, :] <= blk_start[:, None]).astype(jnp.int32), axis=1),
                        N_EXPERTS - 1)
    own = blk_e[:, None] == experts[None, :]
    row_end = jnp.sum(jnp.where(own, (pad_start + counts)[None, :], 0), axis=1)
    blk_rows = jnp.clip(row_end - blk_start, 0, g_rows).astype(jnp.int32)
    later = (experts[None, :] > blk_e[:, None]) & (counts[None, :] > 0)
    next_e = jnp.min(jnp.where(later, experts[None, :], N_EXPERTS), axis=1)
    next_e = jnp.where(next_e == N_EXPERTS, -1, next_e).astype(jnp.int32)

    def piece_index(planes):
        off = jnp.arange(planes, dtype=jnp.int32) * p_rows
        return pos_t[:, None, :] + off[None, :, None]

    x_idx = piece_index(X_PLANES).reshape(TOP_K, X_PLANES * n)
    xs = _sc_scatter(x2p.reshape(X_PLANES * n, LANES), [x_idx[k] for k in range(TOP_K)],
                     X_PLANES * p_rows)
    y = _experts(blk_e, blk_rows, next_e, xs.reshape(X_PLANES, p_rows, LANES), w_gate_up,
                 b_gate_up.reshape(N_EXPERTS, 1, -1), w_down, b_down.reshape(N_EXPERTS, 1, -1))
    yg = _sc_gather(y.reshape(Y_PLANES * p_rows, LANES), piece_index(Y_PLANES).reshape(-1))
    out = _combine(yg, gate, x2, ln3_g.reshape(1, d), ln3_b.reshape(1, d))
    return out.reshape(batch, seq, d)


def kernel(x, mem, w_in, fox_f_bias, mlstm_conv_w, mlstm_i_bias, mlstm_f_bias, fox_norm_g, mlstm_norm_g, w_mix_out, ln1_g, ln1_b, w_xq, w_xk, w_xv, w_xo, ln2_g, ln2_b, w_router, b_router, w_gate_up, b_gate_up, w_down, b_down, ln3_g, ln3_b):
    for l in range(w_in.shape[0]):
        x = _layer(x, mem, w_in[l], fox_f_bias[l], mlstm_conv_w[l], mlstm_i_bias[l],
                   mlstm_f_bias[l], fox_norm_g[l], mlstm_norm_g[l], w_mix_out[l],
                   ln1_g[l], ln1_b[l], w_xq[l], w_xk[l], w_xv[l], w_xo[l], ln2_g[l], ln2_b[l],
                   w_router[l], b_router[l], w_gate_up[l], b_gate_up[l], w_down[l], b_down[l],
                   ln3_g[l], ln3_b[l])
    return x
```

```python
import jax
import jax.numpy as jnp
from jax import lax
from jax.experimental import pallas as pl
from jax.experimental.pallas import tpu as pltpu
from jax.experimental.pallas import tpu_sc as plsc

F32 = jnp.float32
BF16 = jnp.bfloat16

D_MODEL = 1024
FOX_HEADS = 8
FOX_HEAD_DIM = 64
FOX_WIDTH = FOX_HEADS * FOX_HEAD_DIM
MLSTM_HEADS = 4
MLSTM_QK_DIM = 64
MLSTM_V_DIM = 128
MLSTM_QK_WIDTH = MLSTM_HEADS * MLSTM_QK_DIM
MLSTM_V_WIDTH = MLSTM_HEADS * MLSTM_V_DIM
CONV_WIDTH = 4
XATTN_HEADS = 4
XATTN_HEAD_DIM = D_MODEL // XATTN_HEADS
N_EXPERTS = 32
TOP_K = 4
D_EXPERT = D_MODEL
SWIGLU_LIMIT = 7.0
SWIGLU_ALPHA = 1.702
DEEPNORM_ALPHA = 2.0 ** 0.25
LN_EPS = 1e-5
RMS_EPS = 1e-6

LANES = 128
SEQ_BLOCK = 256
ROW_TILE = 1024
EXPERT_ROWS = 512
EXPERT_CHUNK = 256
ROUTE_TILE = 1024
X_PLANES = D_MODEL // 2 // LANES
Y_PLANES = X_PLANES
GATE_I0 = FOX_HEADS
GATE_F0 = FOX_HEADS + MLSTM_HEADS
GATE_ROWS = FOX_HEADS + 2 * MLSTM_HEADS

MIB = 1024 * 1024


def _params(semantics, vmem_mib):
    return pltpu.CompilerParams(dimension_semantics=semantics,
                                vmem_limit_bytes=vmem_mib * MIB)


def _layer_norm(y, g, b):
    mu = jnp.mean(y, axis=-1, keepdims=True)
    yc = y - mu
    var = jnp.mean(yc * yc, axis=-1, keepdims=True)
    return yc * lax.rsqrt(var + LN_EPS) * g + b


def _dot(a, b):
    return jnp.dot(a, b, preferred_element_type=F32)


def _dot_nt(a, b):
    return lax.dot_general(a, b, (((1,), (1,)), ((), ())), preferred_element_type=F32)


def _pack_bf16_pairs(x):
    w = x.shape[1] // 2
    lo = pltpu.bitcast(x[:, :w].astype(BF16).astype(F32), jnp.uint32)
    hi = pltpu.bitcast(x[:, w:].astype(BF16).astype(F32), jnp.uint32)
    return (lo >> 16) | hi


def _unpack_bf16_pairs(u, dtype=BF16):
    lo = pltpu.bitcast(u << 16, F32).astype(dtype)
    hi = pltpu.bitcast(u & jnp.uint32(0xFFFF0000), F32).astype(dtype)
    return jnp.concatenate([lo, hi], axis=1)


def _inproj_kernel(x_ref, w_ref, wt_ref, gb_ref, fqt_ref, fk_ref, fvt_ref, mqk_ref, mvt_ref,
                   mot_ref, g_ref):
    xb = x_ref[...].astype(BF16)

    def mm(c0, width):
        return _dot(xb, w_ref[:, c0:c0 + width])

    def mm_t(r0, height):
        return _dot_nt(wt_ref[r0:r0 + height, :], xb)

    r_fv = FOX_WIDTH
    r_mv = r_fv + FOX_WIDTH
    r_mo = r_mv + MLSTM_V_WIDTH
    fqt_ref[...] = (mm_t(0, FOX_WIDTH) * (FOX_HEAD_DIM ** -0.5)).astype(BF16)
    fvt_ref[...] = mm_t(r_fv, FOX_WIDTH).astype(BF16)
    mvt_ref[...] = mm_t(r_mv, MLSTM_V_WIDTH).astype(BF16)
    mot_ref[...] = mm_t(r_mo, MLSTM_V_WIDTH)
    c_qk = FOX_WIDTH
    c_g = c_qk + 2 * MLSTM_QK_WIDTH
    fk_ref[...] = mm(0, FOX_WIDTH).astype(BF16)
    mqk_ref[...] = mm(c_qk, 2 * MLSTM_QK_WIDTH)
    g_ref[...] = mm(c_g, LANES) + gb_ref[...]


def _inproj(x2d, w_r, w_t, gate_bias):
    n = x2d.shape[0]
    tm = ROW_TILE
    row = lambda width: pl.BlockSpec((tm, width), lambda i: (i, 0))
    col = lambda height: pl.BlockSpec((height, tm), lambda i: (0, i))
    full = lambda a: pl.BlockSpec(a.shape, lambda i: (0, 0))
    out_shapes = (
        jax.ShapeDtypeStruct((FOX_WIDTH, n), BF16),
        jax.ShapeDtypeStruct((n, FOX_WIDTH), BF16),
        jax.ShapeDtypeStruct((FOX_WIDTH, n), BF16),
        jax.ShapeDtypeStruct((n, 2 * MLSTM_QK_WIDTH), F32),
        jax.ShapeDtypeStruct((MLSTM_V_WIDTH, n), BF16),
        jax.ShapeDtypeStruct((MLSTM_V_WIDTH, n), F32),
        jax.ShapeDtypeStruct((n, LANES), F32),
    )
    return pl.pallas_call(
        _inproj_kernel,
        grid=(n // tm,),
        in_specs=[row(D_MODEL), full(w_r), full(w_t), full(gate_bias)],
        out_specs=(col(FOX_WIDTH), row(FOX_WIDTH), col(FOX_WIDTH), row(2 * MLSTM_QK_WIDTH),
                   col(MLSTM_V_WIDTH), col(MLSTM_V_WIDTH), row(LANES)),
        out_shape=out_shapes,
        compiler_params=_params(("parallel",), 48),
    )(x2d, w_r, w_t, gate_bias)


def _split3(x):
    hi = x.astype(BF16)
    r1 = x - hi.astype(F32)
    mid = r1.astype(BF16)
    lo = (r1 - mid.astype(F32)).astype(BF16)
    return hi, mid, lo


def _gateprep_kernel(g_ref, sel_ref, col_ref, row_ref, cp_ref, carry_ref):
    c = pl.program_id(1)

    @pl.when(c == 0)
    def _():
        carry_ref[...] = jnp.zeros_like(carry_ref)

    blk = SEQ_BLOCK
    lane = lax.broadcasted_iota(jnp.int32, (blk, LANES), 1)
    is_i = (lane >= GATE_I0) & (lane < GATE_F0)
    r = lax.broadcasted_iota(jnp.int32, (blk, blk), 0)
    s = lax.broadcasted_iota(jnp.int32, (blk, blk), 1)
    tri = (s <= r).astype(BF16)
    carry = carry_ref[0:1, :]
    globs = []
    for j in range(g_ref.shape[0] // blk):
        g = g_ref[j * blk:(j + 1) * blk, :]
        logsig = jnp.minimum(g, 0.0) - jnp.log1p(jnp.exp(-jnp.abs(g)))
        cs = None
        for piece in _split3(logsig):
            term = _dot(tri, piece)
            cs = term if cs is None else cs + term
        glob = cs + carry
        carry = glob[blk - 1:blk, :]
        out = jnp.where(lane < GATE_I0, glob, jnp.where(is_i, g, cs))
        col_ref[j * blk:(j + 1) * blk, :] = out
        row_ref[0, :, j * blk:(j + 1) * blk] = out.T[0:GATE_ROWS, :]
        globs.append(glob)
    carry_ref[...] = jnp.broadcast_to(carry, carry_ref.shape)
    glob = jnp.concatenate(globs, axis=0)

    pieces = jnp.concatenate(_split3(-glob), axis=1)
    moved = _dot(pieces, sel_ref[...])
    for p in range(FOX_HEADS // 2):
        cp_ref[0, p] = moved[:, p * LANES:(p + 1) * LANES].astype(BF16)


def _piece_selector():
    src = jnp.arange(3 * LANES, dtype=jnp.int32)[:, None]
    dst = jnp.arange(4 * LANES, dtype=jnp.int32)[None, :]
    piece, head = src // LANES, src % LANES
    pair, lane = dst // LANES, dst % LANES
    hit = (lane < 6) & (lane % 3 == piece) & (head == 2 * pair + lane // 3)
    return hit.astype(BF16)


def _gateprep(gates, batch, seq):
    n = gates.shape[0]
    rows = ROW_TILE
    nc = seq // rows
    sel = _piece_selector()
    return pl.pallas_call(
        _gateprep_kernel,
        grid=(batch, nc),
        in_specs=[pl.BlockSpec((rows, LANES), lambda b, c: (b * nc + c, 0)),
                  pl.BlockSpec(sel.shape, lambda b, c: (0, 0))],
        out_specs=(pl.BlockSpec((rows, LANES), lambda b, c: (b * nc + c, 0)),
                   pl.BlockSpec((1, GATE_ROWS, rows), lambda b, c: (b, 0, c)),
                   pl.BlockSpec((1, FOX_HEADS // 2, rows, LANES), lambda b, c: (b, 0, c, 0))),
        out_shape=(jax.ShapeDtypeStruct((n, LANES), F32),
                   jax.ShapeDtypeStruct((batch, GATE_ROWS, seq), F32),
                   jax.ShapeDtypeStruct((batch, FOX_HEADS // 2, seq, LANES), BF16)),
        scratch_shapes=[pltpu.VMEM((8, LANES), F32)],
        compiler_params=_params(("parallel", "arbitrary"), 32),
    )(gates, sel)


FOX_ONES_ROWS = 16
FOX_VAUG_ROWS = FOX_HEAD_DIM + FOX_ONES_ROWS
FOX_QUERY_TILE = 2 * SEQ_BLOCK


def _fox_kernel(qt_ref, k_ref, cp_ref, vt_ref, gain_ref, o_ref,
                kaug_sc, vaug_sc, qaug_sc, sta_sc, stb_sc, stc_sc, m_sc, acc0_sc, acc1_sc):
    qi = pl.program_id(2)
    tq = o_ref.shape[0]
    tg = tq
    tk = SEQ_BLOCK
    hd = FOX_HEAD_DIM
    seq = k_ref.shape[0]
    nq = seq // tq
    par = qi % 2

    def load_queries(tile, slot):
        q0 = pl.multiple_of(tile * tq, tq)
        for h in range(2):
            qaug_sc[slot, h, 0:hd, :] = qt_ref[h * hd:(h + 1) * hd, pl.ds(q0, tq)]

    @pl.when(qi == 0)
    def _():
        ones3 = (lax.broadcasted_iota(jnp.int32, (FOX_ONES_ROWS, tq), 0) < 3).astype(BF16)
        for slot in range(2):
            for h in range(2):
                qaug_sc[slot, h, hd:hd + FOX_ONES_ROWS, :] = ones3
                qaug_sc[slot, h, hd + FOX_ONES_ROWS:, :] = jnp.zeros(
                    (LANES - hd - FOX_ONES_ROWS, tq), BF16)
        load_queries(0, 0)

    nxt = jnp.minimum(qi + 1, nq - 1)
    load_queries(nxt, 1 - par)

    @pl.when(qi == 0)
    def _():
        lane = lax.broadcasted_iota(jnp.int32, (tk, LANES), 1)

        def build(blk, carry):
            r0 = pl.multiple_of(blk * tk, tk)
            kp = k_ref[pl.ds(r0, tk), :].astype(F32)
            cp = cp_ref[0, 0, pl.ds(r0, tk), :].astype(F32)
            for h in range(2):
                kh = kp if h == 0 else pltpu.roll(kp, hd, axis=1)
                ch = pltpu.roll(cp, hd - 3 * h, axis=1)
                kaug = jnp.where(lane < hd, kh, jnp.where(lane < hd + 3, ch, 0.0))
                kaug_sc[h, pl.ds(r0, tk), :] = kaug.astype(BF16)
            return carry

        lax.fori_loop(0, seq // tk, build, 0)
        for h in range(2):
            vaug_sc[h, 0:hd, :] = vt_ref[h * hd:(h + 1) * hd, :]
            vaug_sc[h, hd:, :] = jnp.ones((FOX_ONES_ROWS, seq), BF16)

    half = tg // 2

    def put_scores(kg, slot, queries=None):
        k0 = pl.multiple_of(kg * tg, tg)
        queries = par if queries is None else queries
        for h in range(2):
            slot[h] = _dot(kaug_sc[h, pl.ds(k0, tg), :], qaug_sc[queries, h])

    def put_diagonal(slot):
        k0 = pl.multiple_of(qi * tg, tg)
        causal_top = (lax.broadcasted_iota(jnp.int32, (half, tq), 0)
                      <= lax.broadcasted_iota(jnp.int32, (half, tq), 1))
        causal_bot = (lax.broadcasted_iota(jnp.int32, (half, half), 0)
                      <= lax.broadcasted_iota(jnp.int32, (half, half), 1))
        for h in range(2):
            top = _dot(kaug_sc[h, pl.ds(k0, half), :], qaug_sc[par, h])
            slot[h, 0:half, :] = jnp.where(causal_top, top, -jnp.inf)
            bot = _dot(kaug_sc[h, pl.ds(k0 + half, half), :], qaug_sc[par, h, :, half:])
            slot[h, half:, half:] = jnp.where(causal_bot, bot, -jnp.inf)

    m_sc[...] = jnp.full(m_sc.shape, -jnp.inf, F32)
    acc0_sc[...] = jnp.zeros(acc0_sc.shape, F32)
    acc1_sc[...] = jnp.zeros(acc1_sc.shape, F32)
    acc = (acc0_sc, acc1_sc)

    def absorb(kg, slot):
        k0 = pl.multiple_of(kg * tg, tg)
        for h in range(2):
            st = slot[h]
            m_prev = m_sc[h]
            m_new = jnp.maximum(m_prev, jnp.max(st, axis=0, keepdims=True))
            pv = None
            for u in range(2):
                p = jnp.exp(st[u * half:(u + 1) * half] - m_new)
                part = _dot(vaug_sc[h, :, pl.ds(k0 + u * half, half)], p.astype(BF16))
                pv = part if pv is None else pv + part
            acc[h][...] = jnp.exp(m_prev - m_new) * acc[h][...] + pv
            m_sc[h] = m_new

    def absorb_diagonal(slot):
        k0 = pl.multiple_of(qi * tg, tg)
        for h in range(2):
            top = slot[h, 0:half, :]
            bot = slot[h, half:, half:]
            m_prev = m_sc[h]
            m_sc[h] = jnp.maximum(m_prev, jnp.max(top, axis=0, keepdims=True))
            m_sc[h, :, half:] = jnp.maximum(m_sc[h, :, half:], jnp.max(bot, axis=0, keepdims=True))
            m_new = m_sc[h]
            acc[h][...] = (jnp.exp(m_prev - m_new) * acc[h][...]
                           + _dot(vaug_sc[h, :, pl.ds(k0, half)],
                                  jnp.exp(top - m_new).astype(BF16)))
            acc[h][:, half:] += _dot(vaug_sc[h, :, pl.ds(k0 + half, half)],
                                     jnp.exp(bot - m_sc[h, :, half:]).astype(BF16))

    sa, sb, sc = sta_sc, stb_sc, stc_sc
    n_loop = jnp.maximum(qi - 1, 0) // 2

    def body(j, carry):
        put_scores(2 * j + 1, sb)
        absorb(2 * j, sa)
        put_scores(2 * j + 2, sa)
        absorb(2 * j + 1, sb)
        return carry

    lax.fori_loop(0, n_loop, body, 0)
    done = 2 * n_loop

    def put_next():
        put_scores(0, sa, queries=1 - par)

    @pl.when(qi == 0)
    def _():
        put_diagonal(sc)
        put_next()
        absorb_diagonal(sc)

    @pl.when((qi > 0) & (qi - done == 1))
    def _():
        put_diagonal(sc)
        absorb(done, sa)
        put_next()
        absorb_diagonal(sc)

    @pl.when((qi > 0) & (qi - done == 2))
    def _():
        put_scores(done + 1, sb)
        absorb(done, sa)
        put_diagonal(sc)
        absorb(done + 1, sb)
        put_next()
        absorb_diagonal(sc)

    normed = []
    for h in range(2):
        a = acc[h][...]
        oh = a[0:hd] * (1.0 / a[hd:hd + 1])
        ms = jnp.mean(oh * oh, axis=0, keepdims=True)
        gain = gain_ref[h * hd:(h + 1) * hd, :]
        normed.append(oh * lax.rsqrt(ms + RMS_EPS) * jnp.concatenate([gain] * (tq // LANES), axis=1))
    o_ref[...] = jnp.concatenate(normed, axis=0).T.astype(o_ref.dtype)


def _fox(fqt, fk, cpieces, fvt, fox_gain, batch, seq):
    n = fk.shape[0]
    tq = FOX_QUERY_TILE
    nq = seq // tq
    npair = FOX_HEADS // 2
    return pl.pallas_call(
        _fox_kernel,
        grid=(batch, npair, nq),
        in_specs=[pl.BlockSpec((LANES, seq), lambda b, hp, qi: (hp, b)),
                  pl.BlockSpec((seq, LANES), lambda b, hp, qi: (b, hp)),
                  pl.BlockSpec((1, 1, seq, LANES), lambda b, hp, qi: (b, hp, 0, 0)),
                  pl.BlockSpec((LANES, seq), lambda b, hp, qi: (hp, b)),
                  pl.BlockSpec((LANES, LANES), lambda b, hp, qi: (hp, 0))],
        out_specs=pl.BlockSpec((tq, LANES), lambda b, hp, qi: (b * nq + qi, hp)),
        out_shape=jax.ShapeDtypeStruct((n, FOX_WIDTH), BF16),
        scratch_shapes=[pltpu.VMEM((2, seq, LANES), BF16),
                        pltpu.VMEM((2, FOX_VAUG_ROWS, seq), BF16),
                        pltpu.VMEM((2, 2, LANES, tq), BF16),
                        pltpu.VMEM((2, tq, tq), F32),
                        pltpu.VMEM((2, tq, tq), F32),
                        pltpu.VMEM((2, tq, tq), F32),
                        pltpu.VMEM((2, 1, tq), F32),
                        pltpu.VMEM((FOX_VAUG_ROWS, tq), F32),
                        pltpu.VMEM((FOX_VAUG_ROWS, tq), F32)],
        compiler_params=_params(("parallel", "parallel", "arbitrary"), 48),
    )(fqt, fk, cpieces, fvt, fox_gain)


MLSTM_ONES_ROWS = 16


def _mlstm_kernel(qk_ref, vt_ref, ogt_ref, col_ref, row_ref, cw_ref, gain_ref, o_ref,
                  tail_ref, buf_ref, c_sc, m_sc):
    c = pl.program_id(1)
    L = SEQ_BLOCK

    @pl.when(c == 0)
    def _():
        tail_ref[...] = jnp.zeros_like(tail_ref)
        c_sc[...] = jnp.zeros_like(c_sc)
        m_sc[...] = jnp.zeros_like(m_sc)

    rows_step = qk_ref.shape[0]
    x = qk_ref[...]
    buf_ref[0:8, :] = tail_ref[...]
    buf_ref[8:8 + rows_step, :] = x
    tail_ref[...] = x[rows_step - 8:rows_step, :]
    y = x * cw_ref[CONV_WIDTH - 1:CONV_WIDTH, :]
    for j in range(CONV_WIDTH - 1):
        shift = CONV_WIDTH - 1 - j
        y = y + buf_ref[8 - shift:8 - shift + rows_step, :] * cw_ref[j:j + 1, :]
    y = y * jax.nn.sigmoid(y)
    n_pair = MLSTM_HEADS // 2

    lane = lax.broadcasted_iota(jnp.int32, (L, LANES), 1)
    sub = lax.broadcasted_iota(jnp.int32, (LANES, L), 0)
    ss = lax.broadcasted_iota(jnp.int32, (L, L), 0)
    ll = lax.broadcasted_iota(jnp.int32, (L, L), 1)
    causal = ss <= ll
    ones_rows = jnp.ones((MLSTM_ONES_ROWS, L), BF16)
    cstates = [c_sc[h] for h in range(MLSTM_HEADS)]
    mstates = [m_sc[h][0:1, 0:1] for h in range(MLSTM_HEADS)]

    per_chunk = []
    for chunk in range(rows_step // L):
        r0 = chunk * L
        yc = y[r0:r0 + L, :]
        kf = yc[:, MLSTM_QK_WIDTH:] * (MLSTM_QK_DIM ** -0.5)
        per_chunk.append((
            [yc[:, p * LANES:(p + 1) * LANES].T for p in range(n_pair)],
            [kf[:, p * LANES:(p + 1) * LANES].astype(BF16) for p in range(n_pair)],
            col_ref[r0:r0 + L, :],
            row_ref[0, :, r0:r0 + L]))

    for chunk, h in [(c_, h_) for c_ in range(rows_step // L) for h_ in range(MLSTM_HEADS)]:
        r0 = chunk * L
        qt, kb, col, row = per_chunk[chunk]
        pair, half = divmod(h, 2)
        head_sub = (sub < MLSTM_QK_DIM) if half == 0 else (sub >= MLSTM_QK_DIM)
        head_lane = (lane < MLSTM_QK_DIM) if half == 0 else (lane >= MLSTM_QK_DIM)
        qth = jnp.where(head_sub, qt[pair], 0.0).astype(BF16)
        rcol = col[:, GATE_I0 + h:GATE_I0 + h + 1] - col[:, GATE_F0 + h:GATE_F0 + h + 1]
        brow = row[GATE_F0 + h:GATE_F0 + h + 1, :]
        lirow = row[GATE_I0 + h:GATE_I0 + h + 1, :]
        g = brow[:, L - 1:L]
        m_prev = mstates[h]

        dt = jnp.where(causal, rcol + brow, -jnp.inf)
        inter_log = brow + m_prev
        m_t = jnp.maximum(inter_log, jnp.max(dt, axis=0, keepdims=True))
        w_inter = jnp.exp(inter_log - m_t)
        pt = jnp.exp(dt - m_t) * _dot(kb[pair], qth)
        vaug = jnp.concatenate([vt_ref[h * LANES:(h + 1) * LANES, r0:r0 + L], ones_rows],
                               axis=0)
        cstate = cstates[h]
        tot = w_inter * _dot(cstate.astype(BF16), qth) + _dot(vaug, pt.astype(BF16))
        den = tot[MLSTM_V_DIM:MLSTM_V_DIM + 1]
        hout = tot[0:MLSTM_V_DIM] * (1.0 / jnp.maximum(jnp.abs(den), jnp.exp(-m_t)))

        a = g + (lirow - brow)
        m_loc = jnp.max(a, axis=1, keepdims=True)
        vw = (vaug.astype(F32) * jnp.exp(a - m_loc)).astype(BF16)
        kmask = jnp.where(head_lane, kb[pair], jnp.zeros_like(kb[pair]))
        kv = _dot(vw, kmask)
        m_new = jnp.maximum(g + m_prev, m_loc)
        cstates[h] = jnp.exp(g + m_prev - m_new) * cstate + jnp.exp(m_loc - m_new) * kv
        mstates[h] = m_new

        ms = jnp.mean(hout * hout, axis=0, keepdims=True)
        gain = gain_ref[h * LANES:(h + 1) * LANES, :]
        hn = hout * lax.rsqrt(ms + RMS_EPS) * jnp.concatenate([gain] * (L // LANES), axis=1)
        gate = jax.nn.sigmoid(ogt_ref[h * LANES:(h + 1) * LANES, r0:r0 + L])
        o_ref[r0:r0 + L, h * LANES:(h + 1) * LANES] = (hn * gate).T.astype(o_ref.dtype)

    for h in range(MLSTM_HEADS):
        c_sc[h] = cstates[h]
        m_sc[h] = jnp.broadcast_to(mstates[h], m_sc.shape[1:])


def _mlstm(mqk, mvt, mot, gcol, grow, conv_w, gain_lanes, batch, seq):
    n = mqk.shape[0]
    L = ROW_TILE
    nc = seq // L
    qk_width = 2 * MLSTM_QK_WIDTH
    row = lambda width: pl.BlockSpec((L, width), lambda b, c: (b * nc + c, 0))
    col = pl.BlockSpec((MLSTM_V_WIDTH, L), lambda b, c: (0, b * nc + c))
    full = lambda a: pl.BlockSpec(a.shape, lambda b, c: (0, 0))
    return pl.pallas_call(
        _mlstm_kernel,
        grid=(batch, nc),
        in_specs=[row(qk_width), col, col, row(LANES),
                  pl.BlockSpec((1, GATE_ROWS, L), lambda b, c: (b, 0, c)),
                  full(conv_w), full(gain_lanes)],
        out_specs=row(MLSTM_V_WIDTH),
        out_shape=jax.ShapeDtypeStruct((n, MLSTM_V_WIDTH), BF16),
        scratch_shapes=[pltpu.VMEM((8, qk_width), F32), pltpu.VMEM((8 + L, qk_width), F32),
                        pltpu.VMEM((MLSTM_HEADS, MLSTM_V_DIM + MLSTM_ONES_ROWS, LANES), F32),
                        pltpu.VMEM((MLSTM_HEADS, 8, LANES), F32)],
        compiler_params=_params(("parallel", "arbitrary"), 48),
    )(mqk, mvt, mot, gcol, grow, conv_w, gain_lanes)


def _outproj_kernel(fo_ref, mo_ref, w_ref, x_ref, g_ref, b_ref, o_ref):
    mix = _dot(fo_ref[...], w_ref[0:FOX_WIDTH, :]) + _dot(mo_ref[...], w_ref[FOX_WIDTH:, :])
    o_ref[...] = _layer_norm(DEEPNORM_ALPHA * x_ref[...] + mix, g_ref[...], b_ref[...])


def _outproj(fo, mo, w_out, x2d, g, b):
    n = x2d.shape[0]
    tm = ROW_TILE
    row = lambda width: pl.BlockSpec((tm, width), lambda i: (i, 0))
    full = lambda a: pl.BlockSpec(a.shape, lambda i: (0, 0))
    return pl.pallas_call(
        _outproj_kernel,
        grid=(n // tm,),
        in_specs=[row(512), row(512), full(w_out), row(D_MODEL), full(g), full(b)],
        out_specs=row(D_MODEL),
        out_shape=jax.ShapeDtypeStruct((n, D_MODEL), F32),
        compiler_params=_params(("parallel",), 48),
    )(fo, mo, w_out, x2d, g, b)


def _memkv_kernel(mem_ref, wk_ref, wv_ref, k_ref, v_ref):
    mb = mem_ref[...].astype(BF16)
    k_ref[...] = (_dot(mb, wk_ref[...]) * (XATTN_HEAD_DIM ** -0.5)).astype(BF16)
    v_ref[...] = _dot(mb, wv_ref[...]).astype(BF16)


def _memkv(mem2d, wk, wv, n_mem):
    n = mem2d.shape[0]
    row = pl.BlockSpec((n_mem, D_MODEL), lambda i: (i, 0))
    full = lambda a: pl.BlockSpec(a.shape, lambda i: (0, 0))
    return pl.pallas_call(
        _memkv_kernel,
        grid=(n // n_mem,),
        in_specs=[row, full(wk), full(wv)],
        out_specs=(row, row),
        out_shape=(jax.ShapeDtypeStruct((n, D_MODEL), BF16),) * 2,
        compiler_params=_params(("parallel",), 32),
    )(mem2d, wk, wv)


def _xattn_kernel(x_ref, k_ref, v_ref, wq_ref, wo_ref, g_ref, b_ref, wr_ref, br_ref,
                  o_ref, ob_ref, lg_ref):
    half = x_ref.shape[0] // 2
    for r in range(2):
        rows = slice(r * half, (r + 1) * half)
        x = x_ref[rows, :]
        q = _dot(x.astype(BF16), wq_ref[...]).astype(BF16)
        outs = []
        for h in range(XATTN_HEADS):
            sl = slice(h * XATTN_HEAD_DIM, (h + 1) * XATTN_HEAD_DIM)
            s = _dot_nt(q[:, sl], k_ref[:, sl])
            p = jnp.exp(s - jnp.max(s, axis=-1, keepdims=True))
            l = jnp.sum(p, axis=-1, keepdims=True)
            outs.append((_dot(p.astype(BF16), v_ref[:, sl]) / l).astype(BF16))
        o = jnp.concatenate(outs, axis=1)
        xa = _dot(o, wo_ref[...])
        x2 = _layer_norm(DEEPNORM_ALPHA * x + xa, g_ref[...], b_ref[...])
        o_ref[rows, :] = x2
        packed = _pack_bf16_pairs(x2)
        for j in range(ob_ref.shape[0]):
            ob_ref[j, rows, :] = packed[:, j * LANES:(j + 1) * LANES]
        x2h = x2.astype(BF16)
        x2l = (x2 - x2h.astype(F32)).astype(BF16)
        a = _dot_nt(wr_ref[...], x2h)
        b = _dot_nt(wr_ref[0:N_EXPERTS, :], x2l)
        lg_ref[:, rows] = a[0:N_EXPERTS] + a[N_EXPERTS:] + b + br_ref[...]


def _xattn(x1, kmem, vmem, wq, wo, g, b, wr, br, batch, seq, n_mem):
    n = x1.shape[0]
    tm = ROW_TILE
    nt = seq // tm
    row = lambda width: pl.BlockSpec((tm, width), lambda bi, i: (bi * nt + i, 0))
    full = lambda a: pl.BlockSpec(a.shape, lambda bi, i: (0, 0))
    kv = pl.BlockSpec((n_mem, D_MODEL), lambda bi, i: (bi, 0))
    return pl.pallas_call(
        _xattn_kernel,
        grid=(batch, nt),
        in_specs=[row(D_MODEL), kv, kv, full(wq), full(wo), full(g), full(b), full(wr), full(br)],
        out_specs=(row(D_MODEL),
                   pl.BlockSpec((X_PLANES, tm, LANES), lambda bi, i: (0, bi * nt + i, 0)),
                   pl.BlockSpec((N_EXPERTS, tm), lambda bi, i: (0, bi * nt + i))),
        out_shape=(jax.ShapeDtypeStruct((n, D_MODEL), F32),
                   jax.ShapeDtypeStruct((X_PLANES, n, LANES), jnp.uint32),
                   jax.ShapeDtypeStruct((N_EXPERTS, n), F32)),
        compiler_params=_params(("parallel", "parallel"), 60),
    )(x1, kmem, vmem, wq, wo, g, b, wr, br)


def _route_kernel(lg_ref, idx_ref, rank_ref, gate_ref, cnt_ref, carry_ref):
    i = pl.program_id(0)

    @pl.when(i == 0)
    def _():
        carry_ref[...] = jnp.zeros_like(carry_ref)

    lg = lg_ref[...]
    t = lg.shape[1]
    e_idx = lax.broadcasted_iota(jnp.int32, lg.shape, 0).astype(F32)
    sels, vals, idxs = [], [], []
    for _ in range(TOP_K):
        mx = jnp.max(lg, axis=0, keepdims=True)
        first = jnp.min(jnp.where(lg == mx, e_idx, float(N_EXPERTS)), axis=0, keepdims=True)
        sel = e_idx == first
        sels.append(sel)
        vals.append(mx)
        idxs.append(first)
        lg = jnp.where(sel, -jnp.inf, lg)
    exps = [jnp.exp(v - vals[0]) for v in vals]
    tot = exps[0] + exps[1] + exps[2] + exps[3]

    selmat = (sels[0] | sels[1] | sels[2] | sels[3])
    r = lax.broadcasted_iota(jnp.int32, (t, t), 0)
    s = lax.broadcasted_iota(jnp.int32, (t, t), 1)
    earlier = (r < s).astype(BF16)
    carry = carry_ref[:, 0:1]
    rankmat = _dot(selmat.astype(BF16), earlier) + carry
    new_carry = carry + jnp.sum(selmat.astype(F32), axis=1, keepdims=True)
    carry_ref[...] = jnp.broadcast_to(new_carry, carry_ref.shape)
    cnt_ref[...] = jnp.broadcast_to(new_carry, cnt_ref.shape).astype(jnp.int32)

    row8 = lax.broadcasted_iota(jnp.int32, (8, t), 0)
    row128 = lax.broadcasted_iota(jnp.int32, (LANES, t), 0)
    idx_out = jnp.zeros((8, t), F32)
    rank_out = jnp.zeros((8, t), F32)
    gate_out = jnp.zeros((LANES, t), F32)
    for k in range(TOP_K):
        rk = jnp.sum(jnp.where(sels[k], rankmat, 0.0), axis=0, keepdims=True)
        idx_out = jnp.where(row8 == k, idxs[k], idx_out)
        rank_out = jnp.where(row8 == k, rk, rank_out)
        gate_out = jnp.where(row128 == k, exps[k] / tot, gate_out)
    idx_ref[...] = idx_out.astype(jnp.int32)
    rank_ref[...] = rank_out.astype(jnp.int32)
    gate_ref[...] = gate_out.T


def _route(logits_t):
    n = logits_t.shape[1]
    t = ROUTE_TILE
    col = lambda rows: pl.BlockSpec((rows, t), lambda i: (0, i))
    return pl.pallas_call(
        _route_kernel,
        grid=(n // t,),
        in_specs=[col(N_EXPERTS)],
        out_specs=(col(8), col(8), pl.BlockSpec((t, LANES), lambda i: (i, 0)),
                   pl.BlockSpec((N_EXPERTS, LANES), lambda i: (0, 0))),
        out_shape=(jax.ShapeDtypeStruct((8, n), jnp.int32),
                   jax.ShapeDtypeStruct((8, n), jnp.int32),
                   jax.ShapeDtypeStruct((n, LANES), F32),
                   jax.ShapeDtypeStruct((N_EXPERTS, LANES), jnp.int32)),
        scratch_shapes=[pltpu.VMEM((N_EXPERTS, LANES), F32)],
        compiler_params=_params(("arbitrary",), 32),
    )(logits_t)


def _expert_kernel(blk_e_ref, blk_rows_ref, next_e_ref, x_ref, wgu_hbm, bgu_ref, wd_hbm, bd_ref,
                   y_ref, wgu_f32, wd_f32, wgu_sc, wd_sc, state_ref, sem):
    i = pl.program_id(0)
    e = blk_e_ref[i]
    rows = blk_rows_ref[i]
    g = y_ref.shape[1]

    def weight_copies(expert, slot):
        return (pltpu.make_async_copy(wgu_hbm.at[expert], wgu_f32.at[slot], sem.at[0, slot]),
                pltpu.make_async_copy(wd_hbm.at[expert], wd_f32.at[slot], sem.at[1, slot]))

    @pl.when(i == 0)
    def _():
        state_ref[0] = -1
        state_ref[1] = 0

    @pl.when((rows > 0) & (e != state_ref[0]))
    def _():
        slot = state_ref[1]

        @pl.when(i == 0)
        def _():
            for cp in weight_copies(e, slot):
                cp.start()

        for cp in weight_copies(e, slot):
            cp.wait()
        wgu_sc[...] = wgu_f32[slot].astype(BF16)
        wd_sc[...] = wd_f32[slot].astype(BF16)
        nxt = next_e_ref[i]

        @pl.when(nxt >= 0)
        def _():
            for cp in weight_copies(nxt, 1 - slot):
                cp.start()

        state_ref[0] = e
        state_ref[1] = 1 - slot

    def ffn(m):
        packed = jnp.concatenate([x_ref[j, 0:m, :] for j in range(X_PLANES)], axis=1)
        xb = _unpack_bf16_pairs(packed)
        hids = []
        for c in range(D_EXPERT // EXPERT_CHUNK):
            g0 = c * EXPERT_CHUNK
            l0 = D_EXPERT + g0
            gate = _dot(xb, wgu_sc[:, g0:g0 + EXPERT_CHUNK]) + bgu_ref[0, :, g0:g0 + EXPERT_CHUNK]
            lin = _dot(xb, wgu_sc[:, l0:l0 + EXPERT_CHUNK]) + bgu_ref[0, :, l0:l0 + EXPERT_CHUNK]
            gate = jnp.minimum(gate, SWIGLU_LIMIT)
            lin = jnp.clip(lin, -SWIGLU_LIMIT, SWIGLU_LIMIT)
            hids.append((gate * jax.nn.sigmoid(SWIGLU_ALPHA * gate) * (lin + 1.0)).astype(BF16))
        y = _pack_bf16_pairs(_dot(jnp.concatenate(hids, axis=1), wd_sc[...]) + bd_ref[0])
        for j in range(Y_PLANES):
            y_ref[j, 0:m, :] = y[:, j * LANES:(j + 1) * LANES]

    @pl.when(rows > g // 2)
    def _():
        ffn(g)

    @pl.when((rows > 0) & (rows <= g // 2))
    def _():
        ffn(g // 2)
        y_ref[:, g // 2:, :] = jnp.zeros((Y_PLANES, g // 2, LANES), y_ref.dtype)

    @pl.when(rows == 0)
    def _():
        y_ref[...] = jnp.zeros_like(y_ref)


def _experts(blk_e, blk_rows, next_e, xs, w_gu, b_gu, w_d, b_d):
    p = xs.shape[1]
    g = EXPERT_ROWS
    grid_spec = pltpu.PrefetchScalarGridSpec(
        num_scalar_prefetch=3,
        grid=(p // g,),
        in_specs=[pl.BlockSpec((X_PLANES, g, LANES), lambda i, be, br, ne: (0, i, 0)),
                  pl.BlockSpec(memory_space=pl.ANY),
                  pl.BlockSpec((1, 1, 2 * D_EXPERT), lambda i, be, br, ne: (be[i], 0, 0)),
                  pl.BlockSpec(memory_space=pl.ANY),
                  pl.BlockSpec((1, 1, D_MODEL), lambda i, be, br, ne: (be[i], 0, 0))],
        out_specs=pl.BlockSpec((Y_PLANES, g, LANES), lambda i, be, br, ne: (0, i, 0)),
        scratch_shapes=[pltpu.VMEM((2, D_MODEL, 2 * D_EXPERT), F32),
                        pltpu.VMEM((2, D_EXPERT, D_MODEL), F32),
                        pltpu.VMEM((D_MODEL, 2 * D_EXPERT), BF16),
                        pltpu.VMEM((D_EXPERT, D_MODEL), BF16),
                        pltpu.SMEM((2,), jnp.int32),
                        pltpu.SemaphoreType.DMA((2, 2))],
    )
    return pl.pallas_call(
        _expert_kernel,
        grid_spec=grid_spec,
        out_shape=jax.ShapeDtypeStruct((Y_PLANES, p, LANES), jnp.uint32),
        compiler_params=_params(("arbitrary",), 56),
    )(blk_e, blk_rows, next_e, xs, w_gu, b_gu, w_d, b_d)


def _combine_kernel(y_ref, gate_ref, x_ref, g_ref, b_ref, o_ref):
    gate = gate_ref[...]
    ff = None
    for k in range(TOP_K):
        packed = jnp.concatenate([y_ref[k, j] for j in range(Y_PLANES)], axis=1)
        yk = _unpack_bf16_pairs(packed, F32) * gate[:, k:k + 1]
        ff = yk if ff is None else ff + yk
    o_ref[...] = _layer_norm(DEEPNORM_ALPHA * x_ref[...] + ff, g_ref[...], b_ref[...])


def _combine(yg, gate, x2, g, b):
    n = x2.shape[0]
    tm = ROW_TILE
    row = lambda width: pl.BlockSpec((tm, width), lambda i: (i, 0))
    full = lambda a: pl.BlockSpec(a.shape, lambda i: (0, 0))
    return pl.pallas_call(
        _combine_kernel,
        grid=(n // tm,),
        in_specs=[pl.BlockSpec((TOP_K, Y_PLANES, tm, LANES), lambda i: (0, 0, i, 0)),
                  row(LANES), row(D_MODEL), full(g), full(b)],
        out_specs=row(D_MODEL),
        out_shape=jax.ShapeDtypeStruct((n, D_MODEL), F32),
        compiler_params=_params(("parallel",), 56),
    )(yg, gate, x2, g, b)


SC_WINDOW = 128
SC_GATHERS_IN_FLIGHT = 2
SC_SCATTER_WINDOWS = 2


def _sc_mesh():
    return plsc.VectorSubcoreMesh(core_axis_name="core", subcore_axis_name="subcore")


def _sc_gather(table, idx):
    m = idx.shape[0]
    n_fly = SC_GATHERS_IN_FLIGHT

    @pl.kernel(out_type=jax.ShapeDtypeStruct((m, LANES), table.dtype), mesh=_sc_mesh(),
               scratch_types=[pltpu.SemaphoreType.DMA])
    def gather_kernel(table_hbm, idx_hbm, out_hbm, sem):
        def body(*refs):
            idx_vmem, out_vmem = refs[:n_fly], refs[n_fly]
            copies = [pltpu.async_copy(table_hbm.at[iv.at[0]],
                                       out_vmem.at[pl.ds(u * SC_WINDOW, SC_WINDOW)], sem)
                      for u, iv in enumerate(idx_vmem)]
            for cp in copies:
                cp.wait()

        pltpu.emit_pipeline(
            body,
            grid=(m // (n_fly * SC_WINDOW),),
            in_specs=[pl.BlockSpec((1, SC_WINDOW), lambda i, u=u: (0, n_fly * i + u))
                      for u in range(n_fly)],
            out_specs=[pl.BlockSpec((n_fly * SC_WINDOW, LANES), lambda i: (i, 0))],
            core_axis_name=("core", "subcore"),
            dimension_semantics=(pltpu.PARALLEL,),
        )(*([idx_hbm] * n_fly), out_hbm)

    return gather_kernel(table, idx.reshape(1, m))


def _sc_scatter(src, idx_lists, out_rows):
    m = src.shape[0]
    n_lists = len(idx_lists)

    @pl.kernel(out_type=jax.ShapeDtypeStruct((out_rows, LANES), src.dtype), mesh=_sc_mesh(),
               scratch_types=[pltpu.SemaphoreType.DMA])
    def scatter_kernel(src_hbm, *refs):
        idx_hbm, out_hbm, sem = refs[:n_lists], refs[n_lists], refs[n_lists + 1]

        n_src = SC_SCATTER_WINDOWS

        def body(*vmem):
            src_vmem, idx_vmem = vmem[:n_src], vmem[n_src:]
            copies = [pltpu.async_copy(src_vmem[u], out_hbm.at[idx_vmem[u * n_lists + k].at[0]], sem)
                      for u in range(n_src) for k in range(n_lists)]
            for cp in copies:
                cp.wait()

        pltpu.emit_pipeline(
            body,
            grid=(m // (n_src * SC_WINDOW),),
            in_specs=[pl.BlockSpec((SC_WINDOW, LANES), lambda i, u=u: (n_src * i + u, 0))
                      for u in range(n_src)]
            + [pl.BlockSpec((1, SC_WINDOW), lambda i, u=u: (0, n_src * i + u))
               for u in range(n_src) for _ in range(n_lists)],
            out_specs=[],
            core_axis_name=("core", "subcore"),
            dimension_semantics=(pltpu.PARALLEL,),
        )(*([src_hbm] * n_src), *[idx_hbm[k] for _ in range(n_src) for k in range(n_lists)])

    return scatter_kernel(src, *[ix.reshape(1, m) for ix in idx_lists])


def _layer(x, mem, w_in, fox_f_bias, conv_w, i_bias, f_bias, fox_g, mlstm_g, w_mix_out,
           ln1_g, ln1_b, w_xq, w_xk, w_xv, w_xo, ln2_g, ln2_b, w_router, b_router,
           w_gate_up, b_gate_up, w_down, b_down, ln3_g, ln3_b):
    batch, seq, d = x.shape
    n_mem = mem.shape[1]
    n = batch * seq
    x2d = x.reshape(n, d)

    o_ff = 3 * FOX_WIDTH
    o_mqk = o_ff + FOX_HEADS
    o_mv = o_mqk + 2 * MLSTM_QK_WIDTH
    o_mi = o_mv + MLSTM_V_WIDTH
    o_mf = o_mi + MLSTM_HEADS
    o_mo = o_mf + MLSTM_HEADS
    n_gate = FOX_HEADS + 2 * MLSTM_HEADS
    w_r = jnp.concatenate(
        [w_in[:, FOX_WIDTH:2 * FOX_WIDTH], w_in[:, o_mqk:o_mv],
         w_in[:, o_ff:o_mqk], w_in[:, o_mi:o_mo],
         jnp.zeros((d, LANES - n_gate), w_in.dtype)], axis=1).astype(BF16)
    w_t = jnp.concatenate([w_in[:, :FOX_WIDTH], w_in[:, 2 * FOX_WIDTH:o_ff],
                           w_in[:, o_mv:o_mi], w_in[:, o_mo:]], axis=1).T.astype(BF16)
    gate_bias = jnp.concatenate(
        [fox_f_bias, i_bias, f_bias, jnp.zeros((LANES - n_gate,), F32)]).reshape(1, LANES)

    fqt, fk, fvt, mqk, mvt, mot, gates = _inproj(x2d, w_r, w_t, gate_bias)
    gcol, grow, cpieces = _gateprep(gates, batch, seq)
    fox_gain_lanes = jnp.broadcast_to(fox_g[:, None], (FOX_WIDTH, LANES))
    fo = _fox(fqt, fk, cpieces, fvt, fox_gain_lanes, batch, seq)
    gain_lanes = jnp.broadcast_to(mlstm_g[:, None], (MLSTM_V_WIDTH, LANES))
    mo_out = _mlstm(mqk, mvt, mot, gcol, grow, conv_w, gain_lanes, batch, seq)
    x1 = _outproj(fo, mo_out, w_mix_out.astype(BF16), x2d, ln1_g.reshape(1, d), ln1_b.reshape(1, d))

    kmem, vmem = _memkv(mem.reshape(batch * n_mem, d), w_xk.astype(BF16), w_xv.astype(BF16), n_mem)
    wrt = w_router.T
    wrt_hi = wrt.astype(BF16)
    wrt_lo = (wrt - wrt_hi.astype(F32)).astype(BF16)
    x2, x2p, logits_t = _xattn(x1, kmem, vmem, w_xq.astype(BF16), w_xo.astype(BF16),
                               ln2_g.reshape(1, d), ln2_b.reshape(1, d),
                               jnp.concatenate([wrt_hi, wrt_lo], axis=0),
                               b_router.reshape(N_EXPERTS, 1), batch, seq, n_mem)

    idx_t, rank_t, gate, cnt = _route(logits_t)
    counts = cnt[:, 0]
    g_rows = EXPERT_ROWS
    padded = ((counts + g_rows - 1) // g_rows) * g_rows
    pad_end = jnp.cumsum(padded)
    pad_start = pad_end - padded
    experts = jnp.arange(N_EXPERTS, dtype=jnp.int32)
    sel = idx_t[:TOP_K, :, None] == experts[None, None, :]
    pos_t = jnp.sum(jnp.where(sel, pad_start[None, None, :], 0), axis=-1) + rank_t[:TOP_K]
    p_rows = n * TOP_K + N_EXPERTS * g_rows
    nb = p_rows // g_rows
    blk_start = jnp.arange(nb, dtype=jnp.int32) * g_rows
    blk_e = jnp.minimum(jnp.sum((pad_end[None, :] <= blk_start[:, None]).astype(jnp.int32), axis=1),
                        N_EXPERTS - 1)
    own = blk_e[:, None] == experts[None, :]
    row_end = jnp.sum(jnp.where(own, (pad_start + counts)[None, :], 0), axis=1)
    blk_rows = jnp.clip(row_end - blk_start, 0, g_rows).astype(jnp.int32)
    later = (experts[None, :] > blk_e[:, None]) & (counts[None, :] > 0)
    next_e = jnp.min(jnp.where(later, experts[None, :], N_EXPERTS), axis=1)
    next_e = jnp.where(next_e == N_EXPERTS, -1, next_e).astype(jnp.int32)

    def piece_index(planes):
        off = jnp.arange(planes, dtype=jnp.int32) * p_rows
        return pos_t[:, None, :] + off[None, :, None]

    x_idx = piece_index(X_PLANES).reshape(TOP_K, X_PLANES * n)
    xs = _sc_scatter(x2p.reshape(X_PLANES * n, LANES), [x_idx[k] for k in range(TOP_K)],
                     X_PLANES * p_rows)
    y = _experts(blk_e, blk_rows, next_e, xs.reshape(X_PLANES, p_rows, LANES), w_gate_up,
                 b_gate_up.reshape(N_EXPERTS, 1, -1), w_down, b_down.reshape(N_EXPERTS, 1, -1))
    yg = _sc_gather(y.reshape(Y_PLANES * p_rows, LANES), piece_index(Y_PLANES).reshape(-1))
    out = _combine(yg.reshape(TOP_K, Y_PLANES, n, LANES), gate, x2,
                   ln3_g.reshape(1, d), ln3_b.reshape(1, d))
    return out.reshape(batch, seq, d)


def kernel(x, mem, w_in, fox_f_bias, mlstm_conv_w, mlstm_i_bias, mlstm_f_bias, fox_norm_g, mlstm_norm_g, w_mix_out, ln1_g, ln1_b, w_xq, w_xk, w_xv, w_xo, ln2_g, ln2_b, w_router, b_router, w_gate_up, b_gate_up, w_down, b_down, ln3_g, ln3_b):
    for l in range(w_in.shape[0]):
        x = _layer(x, mem, w_in[l], fox_f_bias[l], mlstm_conv_w[l], mlstm_i_bias[l],
                   mlstm_f_bias[l], fox_norm_g[l], mlstm_norm_g[l], w_mix_out[l],
                   ln1_g[l], ln1_b[l], w_xq[l], w_xk[l], w_xv[l], w_xo[l], ln2_g[l], ln2_b[l],
                   w_router[l], b_router[l], w_gate_up[l], b_gate_up[l], w_down[l], b_down[l],
                   ln3_g[l], ln3_b[l])
    return x
```

```python
import jax
import jax.numpy as jnp
from jax import lax
from jax.experimental import pallas as pl
from jax.experimental.pallas import tpu as pltpu
from jax.experimental.pallas import tpu_sc as plsc

F32 = jnp.float32
BF16 = jnp.bfloat16

D_MODEL = 1024
FOX_HEADS = 8
FOX_HEAD_DIM = 64
FOX_WIDTH = FOX_HEADS * FOX_HEAD_DIM
MLSTM_HEADS = 4
MLSTM_QK_DIM = 64
MLSTM_V_DIM = 128
MLSTM_QK_WIDTH = MLSTM_HEADS * MLSTM_QK_DIM
MLSTM_V_WIDTH = MLSTM_HEADS * MLSTM_V_DIM
CONV_WIDTH = 4
XATTN_HEADS = 4
XATTN_HEAD_DIM = D_MODEL // XATTN_HEADS
N_EXPERTS = 32
TOP_K = 4
D_EXPERT = D_MODEL
SWIGLU_LIMIT = 7.0
SWIGLU_ALPHA = 1.702
DEEPNORM_ALPHA = 2.0 ** 0.25
LN_EPS = 1e-5
RMS_EPS = 1e-6

LANES = 128
SEQ_BLOCK = 256
ROW_TILE = 1024
EXPERT_ROWS = 512
EXPERT_CHUNK = 256
ROUTE_TILE = 1024
X_PLANES = D_MODEL // 2 // LANES
Y_PLANES = X_PLANES
GATE_I0 = FOX_HEADS
GATE_F0 = FOX_HEADS + MLSTM_HEADS
GATE_ROWS = FOX_HEADS + 2 * MLSTM_HEADS

MIB = 1024 * 1024


def _params(semantics, vmem_mib):
    return pltpu.CompilerParams(dimension_semantics=semantics,
                                vmem_limit_bytes=vmem_mib * MIB)


def _layer_norm(y, g, b):
    mu = jnp.mean(y, axis=-1, keepdims=True)
    yc = y - mu
    var = jnp.mean(yc * yc, axis=-1, keepdims=True)
    return yc * lax.rsqrt(var + LN_EPS) * g + b


def _dot(a, b):
    return jnp.dot(a, b, preferred_element_type=F32)


def _dot_nt(a, b):
    return lax.dot_general(a, b, (((1,), (1,)), ((), ())), preferred_element_type=F32)


def _pack_bf16_pairs(x):
    w = x.shape[1] // 2
    lo = pltpu.bitcast(x[:, :w].astype(BF16).astype(F32), jnp.uint32)
    hi = pltpu.bitcast(x[:, w:].astype(BF16).astype(F32), jnp.uint32)
    return (lo >> 16) | hi


def _unpack_bf16_pairs(u, dtype=BF16):
    lo = pltpu.bitcast(u << 16, F32).astype(dtype)
    hi = pltpu.bitcast(u & jnp.uint32(0xFFFF0000), F32).astype(dtype)
    return jnp.concatenate([lo, hi], axis=1)


def _inproj_kernel(x_ref, w_ref, wt_ref, gb_ref, fqt_ref, fk_ref, fvt_ref, mqk_ref, mvt_ref,
                   mot_ref, g_ref):
    xb = x_ref[...].astype(BF16)

    def mm(c0, width):
        return _dot(xb, w_ref[:, c0:c0 + width])

    def mm_t(r0, height):
        return _dot_nt(wt_ref[r0:r0 + height, :], xb)

    r_fv = FOX_WIDTH
    r_mv = r_fv + FOX_WIDTH
    r_mo = r_mv + MLSTM_V_WIDTH
    fqt_ref[...] = (mm_t(0, FOX_WIDTH) * (FOX_HEAD_DIM ** -0.5)).astype(BF16)
    fvt_ref[...] = mm_t(r_fv, FOX_WIDTH).astype(BF16)
    mvt_ref[...] = mm_t(r_mv, MLSTM_V_WIDTH).astype(BF16)
    mot_ref[...] = mm_t(r_mo, MLSTM_V_WIDTH)
    c_qk = FOX_WIDTH
    c_g = c_qk + 2 * MLSTM_QK_WIDTH
    fk_ref[...] = mm(0, FOX_WIDTH).astype(BF16)
    mqk_ref[...] = mm(c_qk, 2 * MLSTM_QK_WIDTH)
    g_ref[...] = mm(c_g, LANES) + gb_ref[...]


def _inproj(x2d, w_r, w_t, gate_bias):
    n = x2d.shape[0]
    tm = ROW_TILE
    row = lambda width: pl.BlockSpec((tm, width), lambda i: (i, 0))
    col = lambda height: pl.BlockSpec((height, tm), lambda i: (0, i))
    full = lambda a: pl.BlockSpec(a.shape, lambda i: (0, 0))
    out_shapes = (
        jax.ShapeDtypeStruct((FOX_WIDTH, n), BF16),
        jax.ShapeDtypeStruct((n, FOX_WIDTH), BF16),
        jax.ShapeDtypeStruct((FOX_WIDTH, n), BF16),
        jax.ShapeDtypeStruct((n, 2 * MLSTM_QK_WIDTH), F32),
        jax.ShapeDtypeStruct((MLSTM_V_WIDTH, n), BF16),
        jax.ShapeDtypeStruct((MLSTM_V_WIDTH, n), F32),
        jax.ShapeDtypeStruct((n, LANES), F32),
    )
    return pl.pallas_call(
        _inproj_kernel,
        grid=(n // tm,),
        in_specs=[row(D_MODEL), full(w_r), full(w_t), full(gate_bias)],
        out_specs=(col(FOX_WIDTH), row(FOX_WIDTH), col(FOX_WIDTH), row(2 * MLSTM_QK_WIDTH),
                   col(MLSTM_V_WIDTH), col(MLSTM_V_WIDTH), row(LANES)),
        out_shape=out_shapes,
        compiler_params=_params(("parallel",), 48),
    )(x2d, w_r, w_t, gate_bias)


def _split3(x):
    hi = x.astype(BF16)
    r1 = x - hi.astype(F32)
    mid = r1.astype(BF16)
    lo = (r1 - mid.astype(F32)).astype(BF16)
    return hi, mid, lo


def _gateprep_kernel(g_ref, sel_ref, col_ref, row_ref, cp_ref, carry_ref):
    c = pl.program_id(1)

    @pl.when(c == 0)
    def _():
        carry_ref[...] = jnp.zeros_like(carry_ref)

    blk = SEQ_BLOCK
    lane = lax.broadcasted_iota(jnp.int32, (blk, LANES), 1)
    is_i = (lane >= GATE_I0) & (lane < GATE_F0)
    r = lax.broadcasted_iota(jnp.int32, (blk, blk), 0)
    s = lax.broadcasted_iota(jnp.int32, (blk, blk), 1)
    tri = (s <= r).astype(BF16)
    carry = carry_ref[0:1, :]
    globs = []
    for j in range(g_ref.shape[0] // blk):
        g = g_ref[j * blk:(j + 1) * blk, :]
        logsig = jnp.minimum(g, 0.0) - jnp.log1p(jnp.exp(-jnp.abs(g)))
        cs = None
        for piece in _split3(logsig):
            term = _dot(tri, piece)
            cs = term if cs is None else cs + term
        glob = cs + carry
        carry = glob[blk - 1:blk, :]
        out = jnp.where(lane < GATE_I0, glob, jnp.where(is_i, g, cs))
        col_ref[j * blk:(j + 1) * blk, :] = out
        row_ref[0, :, j * blk:(j + 1) * blk] = out.T[0:GATE_ROWS, :]
        globs.append(glob)
    carry_ref[...] = jnp.broadcast_to(carry, carry_ref.shape)
    glob = jnp.concatenate(globs, axis=0)

    pieces = jnp.concatenate(_split3(-glob), axis=1)
    moved = _dot(pieces, sel_ref[...])
    for p in range(FOX_HEADS // 2):
        cp_ref[0, p] = moved[:, p * LANES:(p + 1) * LANES].astype(BF16)


def _piece_selector():
    src = jnp.arange(3 * LANES, dtype=jnp.int32)[:, None]
    dst = jnp.arange(4 * LANES, dtype=jnp.int32)[None, :]
    piece, head = src // LANES, src % LANES
    pair, lane = dst // LANES, dst % LANES
    hit = (lane < 6) & (lane % 3 == piece) & (head == 2 * pair + lane // 3)
    return hit.astype(BF16)


def _gateprep(gates, batch, seq):
    n = gates.shape[0]
    rows = ROW_TILE
    nc = seq // rows
    sel = _piece_selector()
    return pl.pallas_call(
        _gateprep_kernel,
        grid=(batch, nc),
        in_specs=[pl.BlockSpec((rows, LANES), lambda b, c: (b * nc + c, 0)),
                  pl.BlockSpec(sel.shape, lambda b, c: (0, 0))],
        out_specs=(pl.BlockSpec((rows, LANES), lambda b, c: (b * nc + c, 0)),
                   pl.BlockSpec((1, GATE_ROWS, rows), lambda b, c: (b, 0, c)),
                   pl.BlockSpec((1, FOX_HEADS // 2, rows, LANES), lambda b, c: (b, 0, c, 0))),
        out_shape=(jax.ShapeDtypeStruct((n, LANES), F32),
                   jax.ShapeDtypeStruct((batch, GATE_ROWS, seq), F32),
                   jax.ShapeDtypeStruct((batch, FOX_HEADS // 2, seq, LANES), BF16)),
        scratch_shapes=[pltpu.VMEM((8, LANES), F32)],
        compiler_params=_params(("parallel", "arbitrary"), 32),
    )(gates, sel)


FOX_ONES_ROWS = 16
FOX_VAUG_ROWS = FOX_HEAD_DIM + FOX_ONES_ROWS
FOX_QUERY_TILE = 2 * SEQ_BLOCK


def _fox_kernel(qt_ref, k_ref, cp_ref, vt_ref, gain_ref, o_ref,
                kaug_sc, vaug_sc, qaug_sc, sta_sc, stb_sc, stc_sc, m_sc, acc0_sc, acc1_sc):
    qi = pl.program_id(2)
    tq = o_ref.shape[0]
    tg = tq
    tk = SEQ_BLOCK
    hd = FOX_HEAD_DIM
    seq = k_ref.shape[0]
    nq = seq // tq
    par = qi % 2

    def load_queries(tile, slot):
        q0 = pl.multiple_of(tile * tq, tq)
        for h in range(2):
            qaug_sc[slot, h, 0:hd, :] = qt_ref[h * hd:(h + 1) * hd, pl.ds(q0, tq)]

    @pl.when(qi == 0)
    def _():
        ones3 = (lax.broadcasted_iota(jnp.int32, (FOX_ONES_ROWS, tq), 0) < 3).astype(BF16)
        for slot in range(2):
            for h in range(2):
                qaug_sc[slot, h, hd:hd + FOX_ONES_ROWS, :] = ones3
                qaug_sc[slot, h, hd + FOX_ONES_ROWS:, :] = jnp.zeros(
                    (LANES - hd - FOX_ONES_ROWS, tq), BF16)
        load_queries(0, 0)

    nxt = jnp.minimum(qi + 1, nq - 1)
    load_queries(nxt, 1 - par)

    @pl.when(qi == 0)
    def _():
        lane = lax.broadcasted_iota(jnp.int32, (tk, LANES), 1)

        def build(blk, carry):
            r0 = pl.multiple_of(blk * tk, tk)
            kp = k_ref[pl.ds(r0, tk), :].astype(F32)
            cp = cp_ref[0, 0, pl.ds(r0, tk), :].astype(F32)
            for h in range(2):
                kh = kp if h == 0 else pltpu.roll(kp, hd, axis=1)
                ch = pltpu.roll(cp, hd - 3 * h, axis=1)
                kaug = jnp.where(lane < hd, kh, jnp.where(lane < hd + 3, ch, 0.0))
                kaug_sc[h, pl.ds(r0, tk), :] = kaug.astype(BF16)
            return carry

        lax.fori_loop(0, seq // tk, build, 0)
        for h in range(2):
            vaug_sc[h, 0:hd, :] = vt_ref[h * hd:(h + 1) * hd, :]
            vaug_sc[h, hd:, :] = jnp.ones((FOX_ONES_ROWS, seq), BF16)

    half = tg // 2

    def put_scores(kg, slot, queries=None):
        k0 = pl.multiple_of(kg * tg, tg)
        queries = par if queries is None else queries
        for h in range(2):
            slot[h] = _dot(kaug_sc[h, pl.ds(k0, tg), :], qaug_sc[queries, h])

    def put_diagonal(slot):
        k0 = pl.multiple_of(qi * tg, tg)
        causal_top = (lax.broadcasted_iota(jnp.int32, (half, tq), 0)
                      <= lax.broadcasted_iota(jnp.int32, (half, tq), 1))
        causal_bot = (lax.broadcasted_iota(jnp.int32, (half, half), 0)
                      <= lax.broadcasted_iota(jnp.int32, (half, half), 1))
        for h in range(2):
            top = _dot(kaug_sc[h, pl.ds(k0, half), :], qaug_sc[par, h])
            slot[h, 0:half, :] = jnp.where(causal_top, top, -jnp.inf)
            bot = _dot(kaug_sc[h, pl.ds(k0 + half, half), :], qaug_sc[par, h, :, half:])
            slot[h, half:, half:] = jnp.where(causal_bot, bot, -jnp.inf)

    m_sc[...] = jnp.full(m_sc.shape, -jnp.inf, F32)
    acc0_sc[...] = jnp.zeros(acc0_sc.shape, F32)
    acc1_sc[...] = jnp.zeros(acc1_sc.shape, F32)
    acc = (acc0_sc, acc1_sc)

    def absorb(kg, slot):
        k0 = pl.multiple_of(kg * tg, tg)
        for h in range(2):
            st = slot[h]
            m_prev = m_sc[h]
            m_new = jnp.maximum(m_prev, jnp.max(st, axis=0, keepdims=True))
            pv = None
            for u in range(2):
                p = jnp.exp(st[u * half:(u + 1) * half] - m_new)
                part = _dot(vaug_sc[h, :, pl.ds(k0 + u * half, half)], p.astype(BF16))
                pv = part if pv is None else pv + part
            acc[h][...] = jnp.exp(m_prev - m_new) * acc[h][...] + pv
            m_sc[h] = m_new

    def absorb_diagonal(slot):
        k0 = pl.multiple_of(qi * tg, tg)
        for h in range(2):
            top = slot[h, 0:half, :]
            bot = slot[h, half:, half:]
            m_prev = m_sc[h]
            m_sc[h] = jnp.maximum(m_prev, jnp.max(top, axis=0, keepdims=True))
            m_sc[h, :, half:] = jnp.maximum(m_sc[h, :, half:], jnp.max(bot, axis=0, keepdims=True))
            m_new = m_sc[h]
            acc[h][...] = (jnp.exp(m_prev - m_new) * acc[h][...]
                           + _dot(vaug_sc[h, :, pl.ds(k0, half)],
                                  jnp.exp(top - m_new).astype(BF16)))
            acc[h][:, half:] += _dot(vaug_sc[h, :, pl.ds(k0 + half, half)],
                                     jnp.exp(bot - m_sc[h, :, half:]).astype(BF16))

    sa, sb, sc = sta_sc, stb_sc, stc_sc
    n_loop = jnp.maximum(qi - 1, 0) // 2

    def body(j, carry):
        put_scores(2 * j + 1, sb)
        absorb(2 * j, sa)
        put_scores(2 * j + 2, sa)
        absorb(2 * j + 1, sb)
        return carry

    lax.fori_loop(0, n_loop, body, 0)
    done = 2 * n_loop

    def put_next():
        put_scores(0, sa, queries=1 - par)

    @pl.when(qi == 0)
    def _():
        put_diagonal(sc)
        put_next()
        absorb_diagonal(sc)

    @pl.when((qi > 0) & (qi - done == 1))
    def _():
        put_diagonal(sc)
        absorb(done, sa)
        put_next()
        absorb_diagonal(sc)

    @pl.when((qi > 0) & (qi - done == 2))
    def _():
        put_scores(done + 1, sb)
        absorb(done, sa)
        put_diagonal(sc)
        absorb(done + 1, sb)
        put_next()
        absorb_diagonal(sc)

    normed = []
    for h in range(2):
        a = acc[h][...]
        oh = a[0:hd] * (1.0 / a[hd:hd + 1])
        ms = jnp.mean(oh * oh, axis=0, keepdims=True)
        gain = gain_ref[h * hd:(h + 1) * hd, :]
        normed.append(oh * lax.rsqrt(ms + RMS_EPS) * jnp.concatenate([gain] * (tq // LANES), axis=1))
    o_ref[...] = jnp.concatenate(normed, axis=0).T.astype(o_ref.dtype)


def _fox(fqt, fk, cpieces, fvt, fox_gain, batch, seq):
    n = fk.shape[0]
    tq = FOX_QUERY_TILE
    nq = seq // tq
    npair = FOX_HEADS // 2
    return pl.pallas_call(
        _fox_kernel,
        grid=(batch, npair, nq),
        in_specs=[pl.BlockSpec((LANES, seq), lambda b, hp, qi: (hp, b)),
                  pl.BlockSpec((seq, LANES), lambda b, hp, qi: (b, hp)),
                  pl.BlockSpec((1, 1, seq, LANES), lambda b, hp, qi: (b, hp, 0, 0)),
                  pl.BlockSpec((LANES, seq), lambda b, hp, qi: (hp, b)),
                  pl.BlockSpec((LANES, LANES), lambda b, hp, qi: (hp, 0))],
        out_specs=pl.BlockSpec((tq, LANES), lambda b, hp, qi: (b * nq + qi, hp)),
        out_shape=jax.ShapeDtypeStruct((n, FOX_WIDTH), BF16),
        scratch_shapes=[pltpu.VMEM((2, seq, LANES), BF16),
                        pltpu.VMEM((2, FOX_VAUG_ROWS, seq), BF16),
                        pltpu.VMEM((2, 2, LANES, tq), BF16),
                        pltpu.VMEM((2, tq, tq), F32),
                        pltpu.VMEM((2, tq, tq), F32),
                        pltpu.VMEM((2, tq, tq), F32),
                        pltpu.VMEM((2, 1, tq), F32),
                        pltpu.VMEM((FOX_VAUG_ROWS, tq), F32),
                        pltpu.VMEM((FOX_VAUG_ROWS, tq), F32)],
        compiler_params=_params(("parallel", "parallel", "arbitrary"), 48),
    )(fqt, fk, cpieces, fvt, fox_gain)


MLSTM_ONES_ROWS = 16


def _mlstm_kernel(qk_ref, vt_ref, ogt_ref, col_ref, row_ref, cw_ref, gain_ref, o_ref,
                  tail_ref, buf_ref, c_sc, m_sc):
    c = pl.program_id(1)
    L = SEQ_BLOCK

    @pl.when(c == 0)
    def _():
        tail_ref[...] = jnp.zeros_like(tail_ref)
        c_sc[...] = jnp.zeros_like(c_sc)
        m_sc[...] = jnp.zeros_like(m_sc)

    rows_step = qk_ref.shape[0]
    x = qk_ref[...]
    buf_ref[0:8, :] = tail_ref[...]
    buf_ref[8:8 + rows_step, :] = x
    tail_ref[...] = x[rows_step - 8:rows_step, :]
    y = x * cw_ref[CONV_WIDTH - 1:CONV_WIDTH, :]
    for j in range(CONV_WIDTH - 1):
        shift = CONV_WIDTH - 1 - j
        y = y + buf_ref[8 - shift:8 - shift + rows_step, :] * cw_ref[j:j + 1, :]
    y = y * jax.nn.sigmoid(y)
    n_pair = MLSTM_HEADS // 2

    lane = lax.broadcasted_iota(jnp.int32, (L, LANES), 1)
    sub = lax.broadcasted_iota(jnp.int32, (LANES, L), 0)
    ss = lax.broadcasted_iota(jnp.int32, (L, L), 0)
    ll = lax.broadcasted_iota(jnp.int32, (L, L), 1)
    causal = ss <= ll
    ones_rows = jnp.ones((MLSTM_ONES_ROWS, L), BF16)
    cstates = [c_sc[h] for h in range(MLSTM_HEADS)]
    mstates = [m_sc[h][0:1, 0:1] for h in range(MLSTM_HEADS)]

    per_chunk = []
    for chunk in range(rows_step // L):
        r0 = chunk * L
        yc = y[r0:r0 + L, :]
        kf = yc[:, MLSTM_QK_WIDTH:] * (MLSTM_QK_DIM ** -0.5)
        per_chunk.append((
            [yc[:, p * LANES:(p + 1) * LANES].T for p in range(n_pair)],
            [kf[:, p * LANES:(p + 1) * LANES].astype(BF16) for p in range(n_pair)],
            col_ref[r0:r0 + L, :],
            row_ref[0, :, r0:r0 + L]))

    for chunk, h in [(c_, h_) for c_ in range(rows_step // L) for h_ in range(MLSTM_HEADS)]:
        r0 = chunk * L
        qt, kb, col, row = per_chunk[chunk]
        pair, half = divmod(h, 2)
        head_sub = (sub < MLSTM_QK_DIM) if half == 0 else (sub >= MLSTM_QK_DIM)
        head_lane = (lane < MLSTM_QK_DIM) if half == 0 else (lane >= MLSTM_QK_DIM)
        qth = jnp.where(head_sub, qt[pair], 0.0).astype(BF16)
        rcol = col[:, GATE_I0 + h:GATE_I0 + h + 1] - col[:, GATE_F0 + h:GATE_F0 + h + 1]
        brow = row[GATE_F0 + h:GATE_F0 + h + 1, :]
        lirow = row[GATE_I0 + h:GATE_I0 + h + 1, :]
        g = brow[:, L - 1:L]
        m_prev = mstates[h]

        dt = jnp.where(causal, rcol + brow, -jnp.inf)
        inter_log = brow + m_prev
        m_t = jnp.maximum(inter_log, jnp.max(dt, axis=0, keepdims=True))
        w_inter = jnp.exp(inter_log - m_t)
        pt = jnp.exp(dt - m_t) * _dot(kb[pair], qth)
        vaug = jnp.concatenate([vt_ref[h * LANES:(h + 1) * LANES, r0:r0 + L], ones_rows],
                               axis=0)
        cstate = cstates[h]
        tot = w_inter * _dot(cstate.astype(BF16), qth) + _dot(vaug, pt.astype(BF16))
        den = tot[MLSTM_V_DIM:MLSTM_V_DIM + 1]
        hout = tot[0:MLSTM_V_DIM] * (1.0 / jnp.maximum(jnp.abs(den), jnp.exp(-m_t)))

        a = g + (lirow - brow)
        m_loc = jnp.max(a, axis=1, keepdims=True)
        vw = (vaug.astype(F32) * jnp.exp(a - m_loc)).astype(BF16)
        kmask = jnp.where(head_lane, kb[pair], jnp.zeros_like(kb[pair]))
        kv = _dot(vw, kmask)
        m_new = jnp.maximum(g + m_prev, m_loc)
        cstates[h] = jnp.exp(g + m_prev - m_new) * cstate + jnp.exp(m_loc - m_new) * kv
        mstates[h] = m_new

        ms = jnp.mean(hout * hout, axis=0, keepdims=True)
        gain = gain_ref[h * LANES:(h + 1) * LANES, :]
        hn = hout * lax.rsqrt(ms + RMS_EPS) * jnp.concatenate([gain] * (L // LANES), axis=1)
        gate = jax.nn.sigmoid(ogt_ref[h * LANES:(h + 1) * LANES, r0:r0 + L])
        o_ref[r0:r0 + L, h * LANES:(h + 1) * LANES] = (hn * gate).T.astype(o_ref.dtype)

    for h in range(MLSTM_HEADS):
        c_sc[h] = cstates[h]
        m_sc[h] = jnp.broadcast_to(mstates[h], m_sc.shape[1:])


def _mlstm(mqk, mvt, mot, gcol, grow, conv_w, gain_lanes, batch, seq):
    n = mqk.shape[0]
    L = ROW_TILE
    nc = seq // L
    qk_width = 2 * MLSTM_QK_WIDTH
    row = lambda width: pl.BlockSpec((L, width), lambda b, c: (b * nc + c, 0))
    col = pl.BlockSpec((MLSTM_V_WIDTH, L), lambda b, c: (0, b * nc + c))
    full = lambda a: pl.BlockSpec(a.shape, lambda b, c: (0, 0))
    return pl.pallas_call(
        _mlstm_kernel,
        grid=(batch, nc),
        in_specs=[row(qk_width), col, col, row(LANES),
                  pl.BlockSpec((1, GATE_ROWS, L), lambda b, c: (b, 0, c)),
                  full(conv_w), full(gain_lanes)],
        out_specs=row(MLSTM_V_WIDTH),
        out_shape=jax.ShapeDtypeStruct((n, MLSTM_V_WIDTH), BF16),
        scratch_shapes=[pltpu.VMEM((8, qk_width), F32), pltpu.VMEM((8 + L, qk_width), F32),
                        pltpu.VMEM((MLSTM_HEADS, MLSTM_V_DIM + MLSTM_ONES_ROWS, LANES), F32),
                        pltpu.VMEM((MLSTM_HEADS, 8, LANES), F32)],
        compiler_params=_params(("parallel", "arbitrary"), 48),
    )(mqk, mvt, mot, gcol, grow, conv_w, gain_lanes)


def _outproj_kernel(fo_ref, mo_ref, w_ref, x_ref, g_ref, b_ref, o_ref):
    mix = _dot(fo_ref[...], w_ref[0:FOX_WIDTH, :]) + _dot(mo_ref[...], w_ref[FOX_WIDTH:, :])
    o_ref[...] = _layer_norm(DEEPNORM_ALPHA * x_ref[...] + mix, g_ref[...], b_ref[...])


def _outproj(fo, mo, w_out, x2d, g, b):
    n = x2d.shape[0]
    tm = ROW_TILE
    row = lambda width: pl.BlockSpec((tm, width), lambda i: (i, 0))
    full = lambda a: pl.BlockSpec(a.shape, lambda i: (0, 0))
    return pl.pallas_call(
        _outproj_kernel,
        grid=(n // tm,),
        in_specs=[row(512), row(512), full(w_out), row(D_MODEL), full(g), full(b)],
        out_specs=row(D_MODEL),
        out_shape=jax.ShapeDtypeStruct((n, D_MODEL), F32),
        compiler_params=_params(("parallel",), 48),
    )(fo, mo, w_out, x2d, g, b)


def _memkv_kernel(mem_ref, wk_ref, wv_ref, k_ref, v_ref):
    mb = mem_ref[...].astype(BF16)
    k_ref[...] = (_dot(mb, wk_ref[...]) * (XATTN_HEAD_DIM ** -0.5)).astype(BF16)
    v_ref[...] = _dot(mb, wv_ref[...]).astype(BF16)


def _memkv(mem2d, wk, wv, n_mem):
    n = mem2d.shape[0]
    row = pl.BlockSpec((n_mem, D_MODEL), lambda i: (i, 0))
    full = lambda a: pl.BlockSpec(a.shape, lambda i: (0, 0))
    return pl.pallas_call(
        _memkv_kernel,
        grid=(n // n_mem,),
        in_specs=[row, full(wk), full(wv)],
        out_specs=(row, row),
        out_shape=(jax.ShapeDtypeStruct((n, D_MODEL), BF16),) * 2,
        compiler_params=_params(("parallel",), 32),
    )(mem2d, wk, wv)


def _xattn_kernel(x_ref, k_ref, v_ref, wq_ref, wo_ref, g_ref, b_ref, wr_ref, br_ref,
                  o_ref, ob_ref, lg_ref):
    half = x_ref.shape[0] // 2
    for r in range(2):
        rows = slice(r * half, (r + 1) * half)
        x = x_ref[rows, :]
        q = _dot(x.astype(BF16), wq_ref[...]).astype(BF16)
        outs = []
        for h in range(XATTN_HEADS):
            sl = slice(h * XATTN_HEAD_DIM, (h + 1) * XATTN_HEAD_DIM)
            s = _dot_nt(q[:, sl], k_ref[:, sl])
            p = jnp.exp(s - jnp.max(s, axis=-1, keepdims=True))
            l = jnp.sum(p, axis=-1, keepdims=True)
            outs.append((_dot(p.astype(BF16), v_ref[:, sl]) / l).astype(BF16))
        o = jnp.concatenate(outs, axis=1)
        xa = _dot(o, wo_ref[...])
        x2 = _layer_norm(DEEPNORM_ALPHA * x + xa, g_ref[...], b_ref[...])
        o_ref[rows, :] = x2
        packed = _pack_bf16_pairs(x2)
        for j in range(ob_ref.shape[0]):
            ob_ref[j, rows, :] = packed[:, j * LANES:(j + 1) * LANES]
        x2h = x2.astype(BF16)
        x2l = (x2 - x2h.astype(F32)).astype(BF16)
        a = _dot_nt(wr_ref[...], x2h)
        b = _dot_nt(wr_ref[0:N_EXPERTS, :], x2l)
        lg_ref[:, rows] = a[0:N_EXPERTS] + a[N_EXPERTS:] + b + br_ref[...]


def _xattn(x1, kmem, vmem, wq, wo, g, b, wr, br, batch, seq, n_mem):
    n = x1.shape[0]
    tm = ROW_TILE
    nt = seq // tm
    row = lambda width: pl.BlockSpec((tm, width), lambda bi, i: (bi * nt + i, 0))
    full = lambda a: pl.BlockSpec(a.shape, lambda bi, i: (0, 0))
    kv = pl.BlockSpec((n_mem, D_MODEL), lambda bi, i: (bi, 0))
    return pl.pallas_call(
        _xattn_kernel,
        grid=(batch, nt),
        in_specs=[row(D_MODEL), kv, kv, full(wq), full(wo), full(g), full(b), full(wr), full(br)],
        out_specs=(row(D_MODEL),
                   pl.BlockSpec((X_PLANES, tm, LANES), lambda bi, i: (0, bi * nt + i, 0)),
                   pl.BlockSpec((N_EXPERTS, tm), lambda bi, i: (0, bi * nt + i))),
        out_shape=(jax.ShapeDtypeStruct((n, D_MODEL), F32),
                   jax.ShapeDtypeStruct((X_PLANES, n, LANES), jnp.uint32),
                   jax.ShapeDtypeStruct((N_EXPERTS, n), F32)),
        compiler_params=_params(("parallel", "parallel"), 60),
    )(x1, kmem, vmem, wq, wo, g, b, wr, br)


def _route_kernel(lg_ref, idx_ref, rank_ref, gate_ref, cnt_ref, carry_ref):
    i = pl.program_id(0)

    @pl.when(i == 0)
    def _():
        carry_ref[...] = jnp.zeros_like(carry_ref)

    lg = lg_ref[...]
    t = lg.shape[1]
    e_idx = lax.broadcasted_iota(jnp.int32, lg.shape, 0).astype(F32)
    sels, vals, idxs = [], [], []
    for _ in range(TOP_K):
        mx = jnp.max(lg, axis=0, keepdims=True)
        first = jnp.min(jnp.where(lg == mx, e_idx, float(N_EXPERTS)), axis=0, keepdims=True)
        sel = e_idx == first
        sels.append(sel)
        vals.append(mx)
        idxs.append(first)
        lg = jnp.where(sel, -jnp.inf, lg)
    exps = [jnp.exp(v - vals[0]) for v in vals]
    tot = exps[0] + exps[1] + exps[2] + exps[3]

    selmat = (sels[0] | sels[1] | sels[2] | sels[3])
    r = lax.broadcasted_iota(jnp.int32, (t, t), 0)
    s = lax.broadcasted_iota(jnp.int32, (t, t), 1)
    earlier = (r < s).astype(BF16)
    carry = carry_ref[:, 0:1]
    rankmat = _dot(selmat.astype(BF16), earlier) + carry
    new_carry = carry + jnp.sum(selmat.astype(F32), axis=1, keepdims=True)
    carry_ref[...] = jnp.broadcast_to(new_carry, carry_ref.shape)
    cnt_ref[...] = jnp.broadcast_to(new_carry, cnt_ref.shape).astype(jnp.int32)

    row8 = lax.broadcasted_iota(jnp.int32, (8, t), 0)
    row128 = lax.broadcasted_iota(jnp.int32, (LANES, t), 0)
    idx_out = jnp.zeros((8, t), F32)
    rank_out = jnp.zeros((8, t), F32)
    gate_out = jnp.zeros((LANES, t), F32)
    for k in range(TOP_K):
        rk = jnp.sum(jnp.where(sels[k], rankmat, 0.0), axis=0, keepdims=True)
        idx_out = jnp.where(row8 == k, idxs[k], idx_out)
        rank_out = jnp.where(row8 == k, rk, rank_out)
        gate_out = jnp.where(row128 == k, exps[k] / tot, gate_out)
    idx_ref[...] = idx_out.astype(jnp.int32)
    rank_ref[...] = rank_out.astype(jnp.int32)
    gate_ref[...] = gate_out.T


def _route(logits_t):
    n = logits_t.shape[1]
    t = ROUTE_TILE
    col = lambda rows: pl.BlockSpec((rows, t), lambda i: (0, i))
    return pl.pallas_call(
        _route_kernel,
        grid=(n // t,),
        in_specs=[col(N_EXPERTS)],
        out_specs=(col(8), col(8), pl.BlockSpec((t, LANES), lambda i: (i, 0)),
                   pl.BlockSpec((N_EXPERTS, LANES), lambda i: (0, 0))),
        out_shape=(jax.ShapeDtypeStruct((8, n), jnp.int32),
                   jax.ShapeDtypeStruct((8, n), jnp.int32),
                   jax.ShapeDtypeStruct((n, LANES), F32),
                   jax.ShapeDtypeStruct((N_EXPERTS, LANES), jnp.int32)),
        scratch_shapes=[pltpu.VMEM((N_EXPERTS, LANES), F32)],
        compiler_params=_params(("arbitrary",), 32),
    )(logits_t)


def _expert_kernel(blk_e_ref, blk_rows_ref, next_e_ref, x_ref, wgu_hbm, bgu_ref, wd_hbm, bd_ref,
                   y_ref, wgu_f32, wd_f32, wgu_sc, wd_sc, state_ref, sem):
    i = pl.program_id(0)
    e = blk_e_ref[i]
    rows = blk_rows_ref[i]
    g = y_ref.shape[1]

    def weight_copies(expert, slot):
        return (pltpu.make_async_copy(wgu_hbm.at[expert], wgu_f32.at[slot], sem.at[0, slot]),
                pltpu.make_async_copy(wd_hbm.at[expert], wd_f32.at[slot], sem.at[1, slot]))

    @pl.when(i == 0)
    def _():
        state_ref[0] = -1
        state_ref[1] = 0

    @pl.when((rows > 0) & (e != state_ref[0]))
    def _():
        slot = state_ref[1]

        @pl.when(i == 0)
        def _():
            for cp in weight_copies(e, slot):
                cp.start()

        for cp in weight_copies(e, slot):
            cp.wait()
        wgu_sc[...] = wgu_f32[slot].astype(BF16)
        wd_sc[...] = wd_f32[slot].astype(BF16)
        nxt = next_e_ref[i]

        @pl.when(nxt >= 0)
        def _():
            for cp in weight_copies(nxt, 1 - slot):
                cp.start()

        state_ref[0] = e
        state_ref[1] = 1 - slot

    def ffn(m):
        packed = jnp.concatenate([x_ref[j, 0:m, :] for j in range(X_PLANES)], axis=1)
        xb = _unpack_bf16_pairs(packed)
        hids = []
        for c in range(D_EXPERT // EXPERT_CHUNK):
            g0 = c * EXPERT_CHUNK
            l0 = D_EXPERT + g0
            gate = _dot(xb, wgu_sc[:, g0:g0 + EXPERT_CHUNK]) + bgu_ref[0, :, g0:g0 + EXPERT_CHUNK]
            lin = _dot(xb, wgu_sc[:, l0:l0 + EXPERT_CHUNK]) + bgu_ref[0, :, l0:l0 + EXPERT_CHUNK]
            gate = jnp.minimum(gate, SWIGLU_LIMIT)
            lin = jnp.clip(lin, -SWIGLU_LIMIT, SWIGLU_LIMIT)
            hids.append((gate * jax.nn.sigmoid(SWIGLU_ALPHA * gate) * (lin + 1.0)).astype(BF16))
        y = _pack_bf16_pairs(_dot(jnp.concatenate(hids, axis=1), wd_sc[...]) + bd_ref[0])
        for j in range(Y_PLANES):
            y_ref[j, 0:m, :] = y[:, j * LANES:(j + 1) * LANES]

    @pl.when(rows > g // 2)
    def _():
        ffn(g)

    @pl.when((rows > 0) & (rows <= g // 2))
    def _():
        ffn(g // 2)
        y_ref[:, g // 2:, :] = jnp.zeros((Y_PLANES, g // 2, LANES), y_ref.dtype)

    @pl.when(rows == 0)
    def _():
        y_ref[...] = jnp.zeros_like(y_ref)


def _experts(blk_e, blk_rows, next_e, xs, w_gu, b_gu, w_d, b_d):
    p = xs.shape[1]
    g = EXPERT_ROWS
    grid_spec = pltpu.PrefetchScalarGridSpec(
        num_scalar_prefetch=3,
        grid=(p // g,),
        in_specs=[pl.BlockSpec((X_PLANES, g, LANES), lambda i, be, br, ne: (0, i, 0)),
                  pl.BlockSpec(memory_space=pl.ANY),
                  pl.BlockSpec((1, 1, 2 * D_EXPERT), lambda i, be, br, ne: (be[i], 0, 0)),
                  pl.BlockSpec(memory_space=pl.ANY),
                  pl.BlockSpec((1, 1, D_MODEL), lambda i, be, br, ne: (be[i], 0, 0))],
        out_specs=pl.BlockSpec((Y_PLANES, g, LANES), lambda i, be, br, ne: (0, i, 0)),
        scratch_shapes=[pltpu.VMEM((2, D_MODEL, 2 * D_EXPERT), F32),
                        pltpu.VMEM((2, D_EXPERT, D_MODEL), F32),
                        pltpu.VMEM((D_MODEL, 2 * D_EXPERT), BF16),
                        pltpu.VMEM((D_EXPERT, D_MODEL), BF16),
                        pltpu.SMEM((2,), jnp.int32),
                        pltpu.SemaphoreType.DMA((2, 2))],
    )
    return pl.pallas_call(
        _expert_kernel,
        grid_spec=grid_spec,
        out_shape=jax.ShapeDtypeStruct((Y_PLANES, p, LANES), jnp.uint32),
        compiler_params=_params(("arbitrary",), 56),
    )(blk_e, blk_rows, next_e, xs, w_gu, b_gu, w_d, b_d)


COMBINE_RING = 3


def _combine_kernel(y_hbm, gate_ref, x_ref, g_ref, b_ref, o_ref, ybuf, sem):
    i = pl.program_id(0)
    steps = pl.num_programs(0)
    tm = o_ref.shape[0]

    def window_copy(step):
        slot = step % COMBINE_RING
        return pltpu.make_async_copy(y_hbm.at[:, :, pl.ds(step * tm, tm), :], ybuf.at[slot],
                                     sem.at[slot])

    @pl.when(i == 0)
    def _():
        for s in range(COMBINE_RING - 1):
            @pl.when(s < steps)
            def _():
                window_copy(s).start()

    @pl.when(i + COMBINE_RING - 1 < steps)
    def _():
        window_copy(i + COMBINE_RING - 1).start()

    window_copy(i).wait()
    y_ref = ybuf.at[i % COMBINE_RING]
    gate = gate_ref[...]
    ff = None
    for k in range(TOP_K):
        packed = jnp.concatenate([y_ref[k, j] for j in range(Y_PLANES)], axis=1)
        yk = _unpack_bf16_pairs(packed, F32) * gate[:, k:k + 1]
        ff = yk if ff is None else ff + yk
    o_ref[...] = _layer_norm(DEEPNORM_ALPHA * x_ref[...] + ff, g_ref[...], b_ref[...])


def _combine(yg, gate, x2, g, b):
    n = x2.shape[0]
    tm = ROW_TILE
    row = lambda width: pl.BlockSpec((tm, width), lambda i: (i, 0))
    full = lambda a: pl.BlockSpec(a.shape, lambda i: (0, 0))
    return pl.pallas_call(
        _combine_kernel,
        grid=(n // tm,),
        in_specs=[pl.BlockSpec(memory_space=pl.ANY),
                  row(LANES), row(D_MODEL), full(g), full(b)],
        out_specs=row(D_MODEL),
        out_shape=jax.ShapeDtypeStruct((n, D_MODEL), F32),
        scratch_shapes=[pltpu.VMEM((COMBINE_RING, TOP_K, Y_PLANES, tm, LANES), yg.dtype),
                        pltpu.SemaphoreType.DMA((COMBINE_RING,))],
        compiler_params=_params(("arbitrary",), 60),
    )(yg, gate, x2, g, b)


SC_WINDOW = 128
SC_GATHERS_IN_FLIGHT = 2


def _sc_mesh():
    return plsc.VectorSubcoreMesh(core_axis_name="core", subcore_axis_name="subcore")


def _sc_gather(table, idx):
    m = idx.shape[0]
    n_fly = SC_GATHERS_IN_FLIGHT

    @pl.kernel(out_type=jax.ShapeDtypeStruct((m, LANES), table.dtype), mesh=_sc_mesh(),
               scratch_types=[pltpu.SemaphoreType.DMA])
    def gather_kernel(table_hbm, idx_hbm, out_hbm, sem):
        def body(*refs):
            idx_vmem, out_vmem = refs[:n_fly], refs[n_fly]
            copies = [pltpu.async_copy(table_hbm.at[iv.at[0]],
                                       out_vmem.at[pl.ds(u * SC_WINDOW, SC_WINDOW)], sem)
                      for u, iv in enumerate(idx_vmem)]
            for cp in copies:
                cp.wait()

        pltpu.emit_pipeline(
            body,
            grid=(m // (n_fly * SC_WINDOW),),
            in_specs=[pl.BlockSpec((1, SC_WINDOW), lambda i, u=u: (0, n_fly * i + u))
                      for u in range(n_fly)],
            out_specs=[pl.BlockSpec((n_fly * SC_WINDOW, LANES), lambda i: (i, 0))],
            core_axis_name=("core", "subcore"),
            dimension_semantics=(pltpu.PARALLEL,),
        )(*([idx_hbm] * n_fly), out_hbm)

    return gather_kernel(table, idx.reshape(1, m))


def _sc_scatter(src, idx_lists, out_rows):
    m = src.shape[0]
    n_lists = len(idx_lists)

    @pl.kernel(out_type=jax.ShapeDtypeStruct((out_rows, LANES), src.dtype), mesh=_sc_mesh(),
               scratch_types=[pltpu.SemaphoreType.DMA])
    def scatter_kernel(src_hbm, *refs):
        idx_hbm, out_hbm, sem = refs[:n_lists], refs[n_lists], refs[n_lists + 1]

        def body(src_vmem, *idx_vmem):
            copies = [pltpu.async_copy(src_vmem, out_hbm.at[iv.at[0]], sem) for iv in idx_vmem]
            for cp in copies:
                cp.wait()

        pltpu.emit_pipeline(
            body,
            grid=(m // SC_WINDOW,),
            in_specs=[pl.BlockSpec((SC_WINDOW, LANES), lambda i: (i, 0))]
            + [pl.BlockSpec((1, SC_WINDOW), lambda i: (0, i))] * n_lists,
            out_specs=[],
            core_axis_name=("core", "subcore"),
            dimension_semantics=(pltpu.PARALLEL,),
        )(src_hbm, *idx_hbm)

    return scatter_kernel(src, *[ix.reshape(1, m) for ix in idx_lists])


def _layer(x, mem, w_in, fox_f_bias, conv_w, i_bias, f_bias, fox_g, mlstm_g, w_mix_out,
           ln1_g, ln1_b, w_xq, w_xk, w_xv, w_xo, ln2_g, ln2_b, w_router, b_router,
           w_gate_up, b_gate_up, w_down, b_down, ln3_g, ln3_b):
    batch, seq, d = x.shape
    n_mem = mem.shape[1]
    n = batch * seq
    x2d = x.reshape(n, d)

    o_ff = 3 * FOX_WIDTH
    o_mqk = o_ff + FOX_HEADS
    o_mv = o_mqk + 2 * MLSTM_QK_WIDTH
    o_mi = o_mv + MLSTM_V_WIDTH
    o_mf = o_mi + MLSTM_HEADS
    o_mo = o_mf + MLSTM_HEADS
    n_gate = FOX_HEADS + 2 * MLSTM_HEADS
    w_r = jnp.concatenate(
        [w_in[:, FOX_WIDTH:2 * FOX_WIDTH], w_in[:, o_mqk:o_mv],
         w_in[:, o_ff:o_mqk], w_in[:, o_mi:o_mo],
         jnp.zeros((d, LANES - n_gate), w_in.dtype)], axis=1).astype(BF16)
    w_t = jnp.concatenate([w_in[:, :FOX_WIDTH], w_in[:, 2 * FOX_WIDTH:o_ff],
                           w_in[:, o_mv:o_mi], w_in[:, o_mo:]], axis=1).T.astype(BF16)
    gate_bias = jnp.concatenate(
        [fox_f_bias, i_bias, f_bias, jnp.zeros((LANES - n_gate,), F32)]).reshape(1, LANES)

    fqt, fk, fvt, mqk, mvt, mot, gates = _inproj(x2d, w_r, w_t, gate_bias)
    gcol, grow, cpieces = _gateprep(gates, batch, seq)
    fox_gain_lanes = jnp.broadcast_to(fox_g[:, None], (FOX_WIDTH, LANES))
    fo = _fox(fqt, fk, cpieces, fvt, fox_gain_lanes, batch, seq)
    gain_lanes = jnp.broadcast_to(mlstm_g[:, None], (MLSTM_V_WIDTH, LANES))
    mo_out = _mlstm(mqk, mvt, mot, gcol, grow, conv_w, gain_lanes, batch, seq)
    x1 = _outproj(fo, mo_out, w_mix_out.astype(BF16), x2d, ln1_g.reshape(1, d), ln1_b.reshape(1, d))

    kmem, vmem = _memkv(mem.reshape(batch * n_mem, d), w_xk.astype(BF16), w_xv.astype(BF16), n_mem)
    wrt = w_router.T
    wrt_hi = wrt.astype(BF16)
    wrt_lo = (wrt - wrt_hi.astype(F32)).astype(BF16)
    x2, x2p, logits_t = _xattn(x1, kmem, vmem, w_xq.astype(BF16), w_xo.astype(BF16),
                               ln2_g.reshape(1, d), ln2_b.reshape(1, d),
                               jnp.concatenate([wrt_hi, wrt_lo], axis=0),
                               b_router.reshape(N_EXPERTS, 1), batch, seq, n_mem)

    idx_t, rank_t, gate, cnt = _route(logits_t)
    counts = cnt[:, 0]
    g_rows = EXPERT_ROWS
    padded = ((counts + g_rows - 1) // g_rows) * g_rows
    pad_end = jnp.cumsum(padded)
    pad_start = pad_end - padded
    experts = jnp.arange(N_EXPERTS, dtype=jnp.int32)
    sel = idx_t[:TOP_K, :, None] == experts[None, None, :]
    pos_t = jnp.sum(jnp.where(sel, pad_start[None, None, :], 0), axis=-1) + rank_t[:TOP_K]
    p_rows = n * TOP_K + N_EXPERTS * g_rows
    nb = p_rows // g_rows
    blk_start = jnp.arange(nb, dtype=jnp.int32) * g_rows
    blk_e = jnp.minimum(jnp.sum((pad_end[None, :] <= blk_start[:, None]).astype(jnp.int32), axis=1),
                        N_EXPERTS - 1)
    own = blk_e[:, None] == experts[None, :]
    row_end = jnp.sum(jnp.where(own, (pad_start + counts)[None, :], 0), axis=1)
    blk_rows = jnp.clip(row_end - blk_start, 0, g_rows).astype(jnp.int32)
    later = (experts[None, :] > blk_e[:, None]) & (counts[None, :] > 0)
    next_e = jnp.min(jnp.where(later, experts[None, :], N_EXPERTS), axis=1)
    next_e = jnp.where(next_e == N_EXPERTS, -1, next_e).astype(jnp.int32)

    def piece_index(planes):
        off = jnp.arange(planes, dtype=jnp.int32) * p_rows
        return pos_t[:, None, :] + off[None, :, None]

    x_idx = piece_index(X_PLANES).reshape(TOP_K, X_PLANES * n)
    xs = _sc_scatter(x2p.reshape(X_PLANES * n, LANES), [x_idx[k] for k in range(TOP_K)],
                     X_PLANES * p_rows)
    y = _experts(blk_e, blk_rows, next_e, xs.reshape(X_PLANES, p_rows, LANES), w_gate_up,
                 b_gate_up.reshape(N_EXPERTS, 1, -1), w_down, b_down.reshape(N_EXPERTS, 1, -1))
    yg = _sc_gather(y.reshape(Y_PLANES * p_rows, LANES), piece_index(Y_PLANES).reshape(-1))
    out = _combine(yg.reshape(TOP_K, Y_PLANES, n, LANES), gate, x2,
                   ln3_g.reshape(1, d), ln3_b.reshape(1, d))
    return out.reshape(batch, seq, d)


def kernel(x, mem, w_in, fox_f_bias, mlstm_conv_w, mlstm_i_bias, mlstm_f_bias, fox_norm_g, mlstm_norm_g, w_mix_out, ln1_g, ln1_b, w_xq, w_xk, w_xv, w_xo, ln2_g, ln2_b, w_router, b_router, w_gate_up, b_gate_up, w_down, b_down, ln3_g, ln3_b):
    for l in range(w_in.shape[0]):
        x = _layer(x, mem, w_in[l], fox_f_bias[l], mlstm_conv_w[l], mlstm_i_bias[l],
                   mlstm_f_bias[l], fox_norm_g[l], mlstm_norm_g[l], w_mix_out[l],
                   ln1_g[l], ln1_b[l], w_xq[l], w_xk[l], w_xv[l], w_xo[l], ln2_g[l], ln2_b[l],
                   w_router[l], b_router[l], w_gate_up[l], b_gate_up[l], w_down[l], b_down[l],
                   ln3_g[l], ln3_b[l])
    return x
```
